```python
import math
import jax, jax.numpy as jnp
from jax import lax
import numpy as np

D_MODEL = 1024
BATCH = 8
SEQ = 16384
DEPTH = 2

N_HEADS = 16
N_KV_HEADS = 2
GROUP = N_HEADS // N_KV_HEADS
HEAD_DIM = 64
WINDOW = 128
BLOCK = 128
CONV_WIDTH = 31
D_FF = 2816
FFN_CONV_WIDTH = 3
RMS_EPS = 1e-6
LN_EPS = 1e-5
N_MIXERS = 2
N_ATTN = (DEPTH + 1) // 2
N_CONV = DEPTH // 2

kernel_name = "hybrid_swa_sink_conformer_convffn"


def rmsnorm(x, g):
    xf = x.astype(jnp.float32)
    y = xf * lax.rsqrt(jnp.mean(xf * xf, axis=-1, keepdims=True) + RMS_EPS)
    return (y * g.astype(jnp.float32)).astype(x.dtype)


def layernorm(x, g, b):
    xf = x.astype(jnp.float32)
    mu = jnp.mean(xf, axis=-1, keepdims=True)
    var = jnp.mean(jnp.square(xf - mu), axis=-1, keepdims=True)
    y = (xf - mu) * lax.rsqrt(var + LN_EPS)
    return (y * g.astype(jnp.float32) + b.astype(jnp.float32)).astype(x.dtype)


def causal_dwconv(x, w, b):
    k, c = w.shape
    y = lax.conv_general_dilated(
        x, w[:, None, :].astype(x.dtype), window_strides=(1,),
        padding=((k - 1, 0),), dimension_numbers=("NWC", "WIO", "NWC"),
        feature_group_count=c)
    return y + b.astype(x.dtype)


def sliding_window_sink_attention(h, w_qkv, b_qkv, sinks, w_o, b_o):
    bsz, seq, _ = h.shape
    nb = seq // BLOCK
    qkv = h @ w_qkv + b_qkv
    q_dim = N_HEADS * HEAD_DIM
    kv_dim = N_KV_HEADS * HEAD_DIM
    q = qkv[..., :q_dim].reshape(bsz, nb, BLOCK, N_KV_HEADS, GROUP, HEAD_DIM)
    k = qkv[..., q_dim:q_dim + kv_dim].reshape(bsz, seq, N_KV_HEADS, HEAD_DIM)
    v = qkv[..., q_dim + kv_dim:].reshape(bsz, seq, N_KV_HEADS, HEAD_DIM)

    def band_keys(t):
        tp = jnp.pad(t, ((0, 0), (BLOCK, 0), (0, 0), (0, 0)))
        tp = tp.reshape(bsz, nb + 1, BLOCK, N_KV_HEADS, HEAD_DIM)
        return jnp.concatenate([tp[:, :-1], tp[:, 1:]], axis=2)

    kb = band_keys(k)
    vb = band_keys(v)
    scale = 1.0 / math.sqrt(HEAD_DIM)
    s = jnp.einsum("bnqhgd,bnkhd->bnhgqk", q, kb).astype(jnp.float32) * scale

    qi = jnp.arange(BLOCK)[:, None]
    ki = jnp.arange(2 * BLOCK)[None, :]
    dist = qi + BLOCK - ki
    band = (dist >= 0) & (dist < WINDOW)
    first = (jnp.arange(nb) == 0)[:, None, None] & (ki < BLOCK)[None]
    valid = band[None] & ~first
    s = jnp.where(valid[None, :, None, None], s, jnp.finfo(jnp.float32).min)

    sink = sinks.astype(jnp.float32).reshape(1, 1, N_KV_HEADS, GROUP, 1, 1)
    m = jnp.maximum(jnp.max(s, axis=-1, keepdims=True), sink)
    p = jnp.exp(s - m)
    denom = jnp.sum(p, axis=-1, keepdims=True) + jnp.exp(sink - m)
    probs = (p / denom).astype(v.dtype)
    o = jnp.einsum("bnhgqk,bnkhd->bnqhgd", probs, vb)
    o = o.reshape(bsz, seq, q_dim)
    return o @ w_o + b_o


def conformer_conv_module(h, w_pw1, b_pw1, w_dw, b_dw, ln_g, ln_b, w_pw2, b_pw2):
    a = h @ w_pw1 + b_pw1
    u = a[..., :D_MODEL] * jax.nn.sigmoid(a[..., D_MODEL:])
    u = causal_dwconv(u, w_dw, b_dw)
    u = jax.nn.silu(layernorm(u, ln_g, ln_b))
    return u @ w_pw2 + b_pw2


def conv_ffn(h, w_up, w_dw, b_dw, w_down):
    up = h @ w_up
    gate = causal_dwconv(up[..., :D_FF], w_dw, b_dw)
    return (jax.nn.silu(gate) * up[..., D_FF:]) @ w_down


def _fwd_setup_inputs(seed: int = 0) -> dict:
    key = jax.random.key(seed)
    ks = jax.random.split(key, 24)
    f32 = jnp.float32
    qkv_dim = (N_HEADS + 2 * N_KV_HEADS) * HEAD_DIM

    def nrm(k, shape, scale):
        return jax.random.normal(k, shape, f32) * scale

    def gain(k, shape):
        return 1.0 + 0.02 * jax.random.normal(k, shape, f32)

    return {
        "x": jax.random.normal(ks[0], (BATCH, SEQ, D_MODEL), f32),
        "norm_mix": gain(ks[1], (DEPTH, D_MODEL)),
        "attn_w_qkv": nrm(ks[2], (N_ATTN, D_MODEL, qkv_dim), D_MODEL ** -0.5),
        "attn_b_qkv": nrm(ks[3], (N_ATTN, qkv_dim), 0.02),
        "attn_sinks": nrm(ks[4], (N_ATTN, N_HEADS), 0.5),
        "attn_w_o": nrm(ks[5], (N_ATTN, N_HEADS * HEAD_DIM, D_MODEL), (N_HEADS * HEAD_DIM) ** -0.5),
        "attn_b_o": nrm(ks[6], (N_ATTN, D_MODEL), 0.02),
        "conv_w_pw1": nrm(ks[7], (N_CONV, D_MODEL, 2 * D_MODEL), D_MODEL ** -0.5),
        "conv_b_pw1": nrm(ks[8], (N_CONV, 2 * D_MODEL), 0.02),
        "conv_w_dw": nrm(ks[9], (N_CONV, CONV_WIDTH, D_MODEL), CONV_WIDTH ** -0.5),
        "conv_b_dw": nrm(ks[10], (N_CONV, D_MODEL), 0.02),
        "conv_ln_g": gain(ks[11], (N_CONV, D_MODEL)),
        "conv_ln_b": nrm(ks[12], (N_CONV, D_MODEL), 0.02),
        "conv_w_pw2": nrm(ks[13], (N_CONV, D_MODEL, D_MODEL), D_MODEL ** -0.5),
        "conv_b_pw2": nrm(ks[14], (N_CONV, D_MODEL), 0.02),
        "norm_ffn": gain(ks[15], (DEPTH, D_MODEL)),
        "ffn_w_up": nrm(ks[16], (DEPTH, D_MODEL, 2 * D_FF), D_MODEL ** -0.5),
        "ffn_w_dw": nrm(ks[17], (DEPTH, FFN_CONV_WIDTH, D_FF), FFN_CONV_WIDTH ** -0.5),
        "ffn_b_dw": nrm(ks[18], (DEPTH, D_FF), 0.02),
        "ffn_w_down": nrm(ks[19], (DEPTH, D_FF, D_MODEL), D_FF ** -0.5),
        "final_norm": gain(ks[20], (D_MODEL,)),
    }


def _fwd_reference(x, norm_mix, attn_w_qkv, attn_b_qkv, attn_sinks, attn_w_o, attn_b_o,
              conv_w_pw1, conv_b_pw1, conv_w_dw, conv_b_dw, conv_ln_g, conv_ln_b,
              conv_w_pw2, conv_b_pw2, norm_ffn, ffn_w_up, ffn_w_dw, ffn_b_dw,
              ffn_w_down, final_norm):
    for i in range(DEPTH):
        h = rmsnorm(x, norm_mix[i])
        j = i // N_MIXERS
        if i % N_MIXERS == 0:
            x = x + sliding_window_sink_attention(
                h, attn_w_qkv[j], attn_b_qkv[j], attn_sinks[j], attn_w_o[j], attn_b_o[j])
        else:
            x = x + conformer_conv_module(
                h, conv_w_pw1[j], conv_b_pw1[j], conv_w_dw[j], conv_b_dw[j],
                conv_ln_g[j], conv_ln_b[j], conv_w_pw2[j], conv_b_pw2[j])
        h = rmsnorm(x, norm_ffn[i])
        x = x + conv_ffn(h, ffn_w_up[i], ffn_w_dw[i], ffn_b_dw[i], ffn_w_down[i])
    return rmsnorm(x, final_norm)


import jax as _jax
import jax.numpy as _jnp

TWIN_FORMAT = 'train_step'
FWD_PARAMS = ['x', 'norm_mix', 'attn_w_qkv', 'attn_b_qkv', 'attn_sinks', 'attn_w_o', 'attn_b_o', 'conv_w_pw1', 'conv_b_pw1', 'conv_w_dw', 'conv_b_dw', 'conv_ln_g', 'conv_ln_b', 'conv_w_pw2', 'conv_b_pw2', 'norm_ffn', 'ffn_w_up', 'ffn_w_dw', 'ffn_b_dw', 'ffn_w_down', 'final_norm']
TWIN_WEIGHTS = ['norm_mix', 'attn_w_qkv', 'attn_b_qkv', 'attn_sinks', 'attn_w_o', 'attn_b_o', 'conv_w_pw1', 'conv_b_pw1', 'conv_w_dw', 'conv_b_dw', 'conv_ln_g', 'conv_ln_b', 'conv_w_pw2', 'conv_b_pw2', 'norm_ffn', 'ffn_w_up', 'ffn_w_dw', 'ffn_b_dw', 'ffn_w_down', 'final_norm']
TWIN_DIFF_INPUT = 'x'
TWIN_INPUTS = ['x', 'norm_mix', 'attn_w_qkv', 'attn_b_qkv', 'attn_sinks', 'attn_w_o', 'attn_b_o', 'conv_w_pw1', 'conv_b_pw1', 'conv_w_dw', 'conv_b_dw', 'conv_ln_g', 'conv_ln_b', 'conv_w_pw2', 'conv_b_pw2', 'norm_ffn', 'ffn_w_up', 'ffn_w_dw', 'ffn_b_dw', 'ffn_w_down', 'final_norm', 'loss_target', 'm_norm_mix', 'm_attn_w_qkv', 'm_attn_b_qkv', 'm_attn_sinks', 'm_attn_w_o', 'm_attn_b_o', 'm_conv_w_pw1', 'm_conv_b_pw1', 'm_conv_w_dw', 'm_conv_b_dw', 'm_conv_ln_g', 'm_conv_ln_b', 'm_conv_w_pw2', 'm_conv_b_pw2', 'm_norm_ffn', 'm_ffn_w_up', 'm_ffn_w_dw', 'm_ffn_b_dw', 'm_ffn_w_down', 'm_final_norm', 'v_norm_mix', 'v_attn_w_qkv', 'v_attn_b_qkv', 'v_attn_sinks', 'v_attn_w_o', 'v_attn_b_o', 'v_conv_w_pw1', 'v_conv_b_pw1', 'v_conv_w_dw', 'v_conv_b_dw', 'v_conv_ln_g', 'v_conv_ln_b', 'v_conv_w_pw2', 'v_conv_b_pw2', 'v_norm_ffn', 'v_ffn_w_up', 'v_ffn_w_dw', 'v_ffn_b_dw', 'v_ffn_w_down', 'v_final_norm']
TWIN_OUTPUTS = ['loss', 'grad_x', 'grad_norm_mix', 'grad_attn_w_qkv', 'grad_attn_b_qkv', 'grad_attn_sinks', 'grad_attn_w_o', 'grad_attn_b_o', 'grad_conv_w_pw1', 'grad_conv_b_pw1', 'grad_conv_w_dw', 'grad_conv_b_dw', 'grad_conv_ln_g', 'grad_conv_ln_b', 'grad_conv_w_pw2', 'grad_conv_b_pw2', 'grad_norm_ffn', 'grad_ffn_w_up', 'grad_ffn_w_dw', 'grad_ffn_b_dw', 'grad_ffn_w_down', 'grad_final_norm', 'delta_norm_mix', 'delta_attn_w_qkv', 'delta_attn_b_qkv', 'delta_attn_sinks', 'delta_attn_w_o', 'delta_attn_b_o', 'delta_conv_w_pw1', 'delta_conv_b_pw1', 'delta_conv_w_dw', 'delta_conv_b_dw', 'delta_conv_ln_g', 'delta_conv_ln_b', 'delta_conv_w_pw2', 'delta_conv_b_pw2', 'delta_norm_ffn', 'delta_ffn_w_up', 'delta_ffn_w_dw', 'delta_ffn_b_dw', 'delta_ffn_w_down', 'delta_final_norm', 'new_m_norm_mix', 'new_m_attn_w_qkv', 'new_m_attn_b_qkv', 'new_m_attn_sinks', 'new_m_attn_w_o', 'new_m_attn_b_o', 'new_m_conv_w_pw1', 'new_m_conv_b_pw1', 'new_m_conv_w_dw', 'new_m_conv_b_dw', 'new_m_conv_ln_g', 'new_m_conv_ln_b', 'new_m_conv_w_pw2', 'new_m_conv_b_pw2', 'new_m_norm_ffn', 'new_m_ffn_w_up', 'new_m_ffn_w_dw', 'new_m_ffn_b_dw', 'new_m_ffn_w_down', 'new_m_final_norm', 'new_v_norm_mix', 'new_v_attn_w_qkv', 'new_v_attn_b_qkv', 'new_v_attn_sinks', 'new_v_attn_w_o', 'new_v_attn_b_o', 'new_v_conv_w_pw1', 'new_v_conv_b_pw1', 'new_v_conv_w_dw', 'new_v_conv_b_dw', 'new_v_conv_ln_g', 'new_v_conv_ln_b', 'new_v_conv_w_pw2', 'new_v_conv_b_pw2', 'new_v_norm_ffn', 'new_v_ffn_w_up', 'new_v_ffn_w_dw', 'new_v_ffn_b_dw', 'new_v_ffn_w_down', 'new_v_final_norm']
TWIN_LEAF_KINDS = {'loss': 'loss', 'grad_x': 'grad_x', 'grad_norm_mix': 'grad_w', 'grad_attn_w_qkv': 'grad_w', 'grad_attn_b_qkv': 'grad_w', 'grad_attn_sinks': 'grad_w', 'grad_attn_w_o': 'grad_w', 'grad_attn_b_o': 'grad_w', 'grad_conv_w_pw1': 'grad_w', 'grad_conv_b_pw1': 'grad_w', 'grad_conv_w_dw': 'grad_w', 'grad_conv_b_dw': 'grad_w', 'grad_conv_ln_g': 'grad_w', 'grad_conv_ln_b': 'grad_w', 'grad_conv_w_pw2': 'grad_w', 'grad_conv_b_pw2': 'grad_w', 'grad_norm_ffn': 'grad_w', 'grad_ffn_w_up': 'grad_w', 'grad_ffn_w_dw': 'grad_w', 'grad_ffn_b_dw': 'grad_w', 'grad_ffn_w_down': 'grad_w', 'grad_final_norm': 'grad_w', 'delta_norm_mix': 'delta_w', 'delta_attn_w_qkv': 'delta_w', 'delta_attn_b_qkv': 'delta_w', 'delta_attn_sinks': 'delta_w', 'delta_attn_w_o': 'delta_w', 'delta_attn_b_o': 'delta_w', 'delta_conv_w_pw1': 'delta_w', 'delta_conv_b_pw1': 'delta_w', 'delta_conv_w_dw': 'delta_w', 'delta_conv_b_dw': 'delta_w', 'delta_conv_ln_g': 'delta_w', 'delta_conv_ln_b': 'delta_w', 'delta_conv_w_pw2': 'delta_w', 'delta_conv_b_pw2': 'delta_w', 'delta_norm_ffn': 'delta_w', 'delta_ffn_w_up': 'delta_w', 'delta_ffn_w_dw': 'delta_w', 'delta_ffn_b_dw': 'delta_w', 'delta_ffn_w_down': 'delta_w', 'delta_final_norm': 'delta_w', 'new_m_norm_mix': 'new_m', 'new_m_attn_w_qkv': 'new_m', 'new_m_attn_b_qkv': 'new_m', 'new_m_attn_sinks': 'new_m', 'new_m_attn_w_o': 'new_m', 'new_m_attn_b_o': 'new_m', 'new_m_conv_w_pw1': 'new_m', 'new_m_conv_b_pw1': 'new_m', 'new_m_conv_w_dw': 'new_m', 'new_m_conv_b_dw': 'new_m', 'new_m_conv_ln_g': 'new_m', 'new_m_conv_ln_b': 'new_m', 'new_m_conv_w_pw2': 'new_m', 'new_m_conv_b_pw2': 'new_m', 'new_m_norm_ffn': 'new_m', 'new_m_ffn_w_up': 'new_m', 'new_m_ffn_w_dw': 'new_m', 'new_m_ffn_b_dw': 'new_m', 'new_m_ffn_w_down': 'new_m', 'new_m_final_norm': 'new_m', 'new_v_norm_mix': 'new_v', 'new_v_attn_w_qkv': 'new_v', 'new_v_attn_b_qkv': 'new_v', 'new_v_attn_sinks': 'new_v', 'new_v_attn_w_o': 'new_v', 'new_v_attn_b_o': 'new_v', 'new_v_conv_w_pw1': 'new_v', 'new_v_conv_b_pw1': 'new_v', 'new_v_conv_w_dw': 'new_v', 'new_v_conv_b_dw': 'new_v', 'new_v_conv_ln_g': 'new_v', 'new_v_conv_ln_b': 'new_v', 'new_v_conv_w_pw2': 'new_v', 'new_v_conv_b_pw2': 'new_v', 'new_v_norm_ffn': 'new_v', 'new_v_ffn_w_up': 'new_v', 'new_v_ffn_w_dw': 'new_v', 'new_v_ffn_b_dw': 'new_v', 'new_v_ffn_w_down': 'new_v', 'new_v_final_norm': 'new_v'}


def _forward(args):
    return _fwd_reference(*[args[k] for k in FWD_PARAMS])


def _output_shape():
    def fwd():
        inp = _fwd_setup_inputs(0)
        return _fwd_reference(*[inp[k] for k in FWD_PARAMS])
    out = _jax.eval_shape(fwd)
    return out.shape, out.dtype

N_MICROBATCH = 1
ADAM_LR = 0.001
ADAM_B1 = 0.9
ADAM_B2 = 0.999
ADAM_EPS = 1e-08
ADAM_WD = 0.01
ADAM_STEP = 10
PER_EXAMPLE_BATCH_AXIS = {'x': 0, 'loss_target': 0}
SHARED_INPUTS = []
_WEIGHT_DTYPES = {'norm_mix': _jnp.float32, 'attn_w_qkv': _jnp.float32, 'attn_b_qkv': _jnp.float32, 'attn_sinks': _jnp.float32, 'attn_w_o': _jnp.float32, 'attn_b_o': _jnp.float32, 'conv_w_pw1': _jnp.float32, 'conv_b_pw1': _jnp.float32, 'conv_w_dw': _jnp.float32, 'conv_b_dw': _jnp.float32, 'conv_ln_g': _jnp.float32, 'conv_ln_b': _jnp.float32, 'conv_w_pw2': _jnp.float32, 'conv_b_pw2': _jnp.float32, 'norm_ffn': _jnp.float32, 'ffn_w_up': _jnp.float32, 'ffn_w_dw': _jnp.float32, 'ffn_b_dw': _jnp.float32, 'ffn_w_down': _jnp.float32, 'final_norm': _jnp.float32}
MOMENT_SCALE = {'norm_mix': 1.734247e-01, 'attn_w_qkv': 1.276283e-01, 'attn_b_qkv': 5.513591e-01, 'attn_sinks': 7.558634e-02, 'attn_w_o': 9.083952e-02, 'attn_b_o': 5.638200e-01, 'conv_w_pw1': 1.386437e-01, 'conv_b_pw1': 1.518604e-01, 'conv_w_dw': 1.832311e-01, 'conv_b_dw': 3.859829e-01, 'conv_ln_g': 2.454111e-01, 'conv_ln_b': 2.229267e-01, 'conv_w_pw2': 1.866238e-01, 'conv_b_pw2': 4.251392e-01, 'norm_ffn': 2.735093e-01, 'ffn_w_up': 1.104673e-01, 'ffn_w_dw': 1.137069e-01, 'ffn_b_dw': 1.086146e-01, 'ffn_w_down': 1.805932e-01, 'final_norm': 1.279912e+02}


def _to_microbatches(a, axis):
    t = _jnp.moveaxis(a, axis, 0)
    t = t.reshape((N_MICROBATCH, t.shape[0] // N_MICROBATCH) + t.shape[1:])
    return _jnp.moveaxis(t, 1, axis + 1)


def setup_inputs(seed: int = 0) -> dict:
    inp = _fwd_setup_inputs(seed)
    key = _jax.random.fold_in(_jax.random.key(seed), 7919)
    shape, _ = _output_shape()
    out = dict(inp)
    out["loss_target"] = _jax.random.normal(_jax.random.fold_in(key, 0), shape, _jnp.float32)
    for i, name in enumerate(TWIN_WEIGHTS):
        w = inp[name].astype(_jnp.float32)
        if MOMENT_SCALE is None:
            s = _jnp.sqrt(_jnp.mean(_jnp.square(w)) + 1e-30)
        else:
            s = MOMENT_SCALE[name]
        km, kv = _jax.random.split(_jax.random.fold_in(key, i + 1))
        out[name] = w
        out["m_" + name] = s * _jax.random.normal(km, w.shape, _jnp.float32)
        out["v_" + name] = (s * s) * _jax.random.uniform(kv, w.shape, _jnp.float32, 0.5, 1.5)
    if N_MICROBATCH > 1:
        for name, axis in PER_EXAMPLE_BATCH_AXIS.items():
            out[name] = _to_microbatches(out[name], axis)
    return {'x': out['x'], 'norm_mix': out['norm_mix'], 'attn_w_qkv': out['attn_w_qkv'], 'attn_b_qkv': out['attn_b_qkv'], 'attn_sinks': out['attn_sinks'], 'attn_w_o': out['attn_w_o'], 'attn_b_o': out['attn_b_o'], 'conv_w_pw1': out['conv_w_pw1'], 'conv_b_pw1': out['conv_b_pw1'], 'conv_w_dw': out['conv_w_dw'], 'conv_b_dw': out['conv_b_dw'], 'conv_ln_g': out['conv_ln_g'], 'conv_ln_b': out['conv_ln_b'], 'conv_w_pw2': out['conv_w_pw2'], 'conv_b_pw2': out['conv_b_pw2'], 'norm_ffn': out['norm_ffn'], 'ffn_w_up': out['ffn_w_up'], 'ffn_w_dw': out['ffn_w_dw'], 'ffn_b_dw': out['ffn_b_dw'], 'ffn_w_down': out['ffn_w_down'], 'final_norm': out['final_norm'], 'loss_target': out['loss_target'], 'm_norm_mix': out['m_norm_mix'], 'm_attn_w_qkv': out['m_attn_w_qkv'], 'm_attn_b_qkv': out['m_attn_b_qkv'], 'm_attn_sinks': out['m_attn_sinks'], 'm_attn_w_o': out['m_attn_w_o'], 'm_attn_b_o': out['m_attn_b_o'], 'm_conv_w_pw1': out['m_conv_w_pw1'], 'm_conv_b_pw1': out['m_conv_b_pw1'], 'm_conv_w_dw': out['m_conv_w_dw'], 'm_conv_b_dw': out['m_conv_b_dw'], 'm_conv_ln_g': out['m_conv_ln_g'], 'm_conv_ln_b': out['m_conv_ln_b'], 'm_conv_w_pw2': out['m_conv_w_pw2'], 'm_conv_b_pw2': out['m_conv_b_pw2'], 'm_norm_ffn': out['m_norm_ffn'], 'm_ffn_w_up': out['m_ffn_w_up'], 'm_ffn_w_dw': out['m_ffn_w_dw'], 'm_ffn_b_dw': out['m_ffn_b_dw'], 'm_ffn_w_down': out['m_ffn_w_down'], 'm_final_norm': out['m_final_norm'], 'v_norm_mix': out['v_norm_mix'], 'v_attn_w_qkv': out['v_attn_w_qkv'], 'v_attn_b_qkv': out['v_attn_b_qkv'], 'v_attn_sinks': out['v_attn_sinks'], 'v_attn_w_o': out['v_attn_w_o'], 'v_attn_b_o': out['v_attn_b_o'], 'v_conv_w_pw1': out['v_conv_w_pw1'], 'v_conv_b_pw1': out['v_conv_b_pw1'], 'v_conv_w_dw': out['v_conv_w_dw'], 'v_conv_b_dw': out['v_conv_b_dw'], 'v_conv_ln_g': out['v_conv_ln_g'], 'v_conv_ln_b': out['v_conv_ln_b'], 'v_conv_w_pw2': out['v_conv_w_pw2'], 'v_conv_b_pw2': out['v_conv_b_pw2'], 'v_norm_ffn': out['v_norm_ffn'], 'v_ffn_w_up': out['v_ffn_w_up'], 'v_ffn_w_dw': out['v_ffn_w_dw'], 'v_ffn_b_dw': out['v_ffn_b_dw'], 'v_ffn_w_down': out['v_ffn_w_down'], 'v_final_norm': out['v_final_norm']}


def _loss(weights, diff, rest, loss_target):
    with _jax.named_scope("forward"):
        args = {**rest, TWIN_DIFF_INPUT: diff, **{k: w.astype(_WEIGHT_DTYPES[k]) for k, w in weights.items()}}
        y = _forward(args)
    with _jax.named_scope("loss_head"):
        err = _jnp.square(y.astype(_jnp.float32) - loss_target)
        return 0.5 * _jnp.sum(_jnp.mean(err, axis=-1)) if err.ndim else 0.5 * err


def _adamw(w, g, m, v):
    m = ADAM_B1 * m + (1.0 - ADAM_B1) * g
    v = ADAM_B2 * v + (1.0 - ADAM_B2) * _jnp.square(g)
    m_hat = m / (1.0 - ADAM_B1 ** ADAM_STEP)
    v_hat = v / (1.0 - ADAM_B2 ** ADAM_STEP)
    delta = -ADAM_LR * (m_hat / (_jnp.sqrt(v_hat) + ADAM_EPS) + ADAM_WD * w)
    return delta, m, v


def reference(x, norm_mix, attn_w_qkv, attn_b_qkv, attn_sinks, attn_w_o, attn_b_o, conv_w_pw1, conv_b_pw1, conv_w_dw, conv_b_dw, conv_ln_g, conv_ln_b, conv_w_pw2, conv_b_pw2, norm_ffn, ffn_w_up, ffn_w_dw, ffn_b_dw, ffn_w_down, final_norm, loss_target, m_norm_mix, m_attn_w_qkv, m_attn_b_qkv, m_attn_sinks, m_attn_w_o, m_attn_b_o, m_conv_w_pw1, m_conv_b_pw1, m_conv_w_dw, m_conv_b_dw, m_conv_ln_g, m_conv_ln_b, m_conv_w_pw2, m_conv_b_pw2, m_norm_ffn, m_ffn_w_up, m_ffn_w_dw, m_ffn_b_dw, m_ffn_w_down, m_final_norm, v_norm_mix, v_attn_w_qkv, v_attn_b_qkv, v_attn_sinks, v_attn_w_o, v_attn_b_o, v_conv_w_pw1, v_conv_b_pw1, v_conv_w_dw, v_conv_b_dw, v_conv_ln_g, v_conv_ln_b, v_conv_w_pw2, v_conv_b_pw2, v_norm_ffn, v_ffn_w_up, v_ffn_w_dw, v_ffn_b_dw, v_ffn_w_down, v_final_norm):
    given = dict(x=x, norm_mix=norm_mix, attn_w_qkv=attn_w_qkv, attn_b_qkv=attn_b_qkv, attn_sinks=attn_sinks, attn_w_o=attn_w_o, attn_b_o=attn_b_o, conv_w_pw1=conv_w_pw1, conv_b_pw1=conv_b_pw1, conv_w_dw=conv_w_dw, conv_b_dw=conv_b_dw, conv_ln_g=conv_ln_g, conv_ln_b=conv_ln_b, conv_w_pw2=conv_w_pw2, conv_b_pw2=conv_b_pw2, norm_ffn=norm_ffn, ffn_w_up=ffn_w_up, ffn_w_dw=ffn_w_dw, ffn_b_dw=ffn_b_dw, ffn_w_down=ffn_w_down, final_norm=final_norm, loss_target=loss_target, m_norm_mix=m_norm_mix, m_attn_w_qkv=m_attn_w_qkv, m_attn_b_qkv=m_attn_b_qkv, m_attn_sinks=m_attn_sinks, m_attn_w_o=m_attn_w_o, m_attn_b_o=m_attn_b_o, m_conv_w_pw1=m_conv_w_pw1, m_conv_b_pw1=m_conv_b_pw1, m_conv_w_dw=m_conv_w_dw, m_conv_b_dw=m_conv_b_dw, m_conv_ln_g=m_conv_ln_g, m_conv_ln_b=m_conv_ln_b, m_conv_w_pw2=m_conv_w_pw2, m_conv_b_pw2=m_conv_b_pw2, m_norm_ffn=m_norm_ffn, m_ffn_w_up=m_ffn_w_up, m_ffn_w_dw=m_ffn_w_dw, m_ffn_b_dw=m_ffn_b_dw, m_ffn_w_down=m_ffn_w_down, m_final_norm=m_final_norm, v_norm_mix=v_norm_mix, v_attn_w_qkv=v_attn_w_qkv, v_attn_b_qkv=v_attn_b_qkv, v_attn_sinks=v_attn_sinks, v_attn_w_o=v_attn_w_o, v_attn_b_o=v_attn_b_o, v_conv_w_pw1=v_conv_w_pw1, v_conv_b_pw1=v_conv_b_pw1, v_conv_w_dw=v_conv_w_dw, v_conv_b_dw=v_conv_b_dw, v_conv_ln_g=v_conv_ln_g, v_conv_ln_b=v_conv_ln_b, v_conv_w_pw2=v_conv_w_pw2, v_conv_b_pw2=v_conv_b_pw2, v_norm_ffn=v_norm_ffn, v_ffn_w_up=v_ffn_w_up, v_ffn_w_dw=v_ffn_w_dw, v_ffn_b_dw=v_ffn_b_dw, v_ffn_w_down=v_ffn_w_down, v_final_norm=v_final_norm)
    weights = {n: given[n] for n in TWIN_WEIGHTS}
    shared = {n: given[n] for n in SHARED_INPUTS}
    per_example = {n: given[n] for n in ['x']}
    grad_fn = _jax.value_and_grad(_loss, argnums=(0, 1))

    def one_microbatch(ex, loss_target):
        ex = dict(ex)
        diff = ex.pop(TWIN_DIFF_INPUT)
        return grad_fn(weights, diff, {**shared, **ex}, loss_target)

    if N_MICROBATCH == 1:
        loss, (grad_w, grad_x) = one_microbatch(per_example, given["loss_target"])
    else:
        def body(carry, xs):
            loss_sum, grad_sum = carry
            l_k, (gw_k, gx_k) = one_microbatch(xs[0], xs[1])
            with _jax.named_scope("update"):
                return (loss_sum + l_k, _jax.tree.map(_jnp.add, grad_sum, gw_k)), gx_k

        init = (_jnp.zeros((), _jnp.float32), _jax.tree.map(_jnp.zeros_like, weights))
        (loss, grad_w), grad_x = _jax.lax.scan(body, init, (per_example, given["loss_target"]))
    with _jax.named_scope("update"):
        delta_w, new_m, new_v = {}, {}, {}
        for n in TWIN_WEIGHTS:
            delta_w[n], new_m[n], new_v[n] = _adamw(weights[n], grad_w[n], given["m_" + n], given["v_" + n])
    return (loss, grad_x, *[grad_w[n] for n in TWIN_WEIGHTS], *[delta_w[n] for n in TWIN_WEIGHTS],
            *[new_m[n] for n in TWIN_WEIGHTS], *[new_v[n] for n in TWIN_WEIGHTS])
```

```python
import functools
import math

import jax
import jax.numpy as jnp
from jax import lax
from jax.experimental import pallas as pl
from jax.experimental.pallas import tpu as pltpu

f32 = jnp.float32
bf16 = jnp.bfloat16
SDS = jax.ShapeDtypeStruct
MESH = pl.DeviceIdType.MESH

D = 1024
N_HEADS = 16
N_KV = 2
GROUP = 8
HD = 64
BLK = 128
QKV = (N_HEADS + 2 * N_KV) * HD
KV_COL_BLOCK = (N_HEADS * HD) // (2 * N_KV * HD)
CONV_W = 31
CONV_HALO = 32
DFF = 2816
FFN_HALO = 16
RMS_EPS = 1e-6
LN_EPS = 1e-5
LR, B1, B2, ADAM_EPS, WD, STEP = 0.001, 0.9, 0.999, 1e-08, 0.01, 10

N_CHIPS = 4
N_DEV = 8
VMEM_LIMIT = 56 * 1024 * 1024
LANES = 128
ADAM_BLOCK_BYTES = 1 << 20


def _cp(*sem):
    return pltpu.CompilerParams(dimension_semantics=sem, vmem_limit_bytes=VMEM_LIMIT)


def _row(tm, n):
    return pl.BlockSpec((tm, n), lambda i: (i, 0))


def _const(shape):
    return pl.BlockSpec(shape, lambda *_: (0,) * len(shape), pipeline_mode=pl.Buffered(1))


def _acc(shape):
    return pl.BlockSpec(shape, lambda *_: (0,) * len(shape))


def _rms_fwd(x, g):
    r = lax.rsqrt(jnp.mean(x * x, axis=-1, keepdims=True) + RMS_EPS)
    xn = x * r
    return xn * g, xn, r


def _rms_bwd(xn, r, g, dh):
    dyn = dh * g
    dx = r * (dyn - xn * jnp.mean(dyn * xn, axis=-1, keepdims=True))
    dg = jnp.sum(dh * xn, axis=0, keepdims=True)
    return dx, dg


def _sigmoid(z):
    return 1.0 / (1.0 + jnp.exp(-z))


def _dsilu(z, sg):
    return sg * (1.0 + z * (1.0 - sg))


def _dot(a, b):
    return jnp.dot(a, b, preferred_element_type=f32)


def qkv_fwd(x, g, w, b, tm):
    s = x.shape[0]

    def body(x_ref, g_ref, w_ref, b_ref, h_ref, o_ref):
        h, _, _ = _rms_fwd(x_ref[...], g_ref[...])
        hb = h.astype(bf16)
        h_ref[...] = hb
        o_ref[...] = (_dot(hb, w_ref[...]) + b_ref[...]).astype(bf16)

    return pl.pallas_call(
        body, grid=(s // tm,), name="qkv_fwd",
        in_specs=[_row(tm, D), _const((1, D)), _const((D, QKV)), _const((1, QKV))],
        out_specs=[_row(tm, D), _row(tm, QKV)],
        out_shape=[SDS((s, D), bf16), SDS((s, QKV), bf16)],
        compiler_params=_cp("parallel"),
    )(x, g, w, b)


def _band_mask(i):
    qi = lax.broadcasted_iota(jnp.int32, (BLK, 2 * BLK), 0)
    ki = lax.broadcasted_iota(jnp.int32, (BLK, 2 * BLK), 1)
    dist = qi + BLK - ki
    return (dist >= 0) & (dist < BLK) & ((ki >= BLK) | (i > 0))


_NEG = float(jnp.finfo(jnp.float32).min)
_NT = (((1,), (1,)), ((), ()))
_TN = (((0,), (0,)), ((), ()))


def _kv_heads(kvp_ref, kvc_ref, kvh):
    ks = slice(kvh * HD, (kvh + 1) * HD)
    vs = slice(N_KV * HD + kvh * HD, N_KV * HD + (kvh + 1) * HD)
    k = jnp.concatenate([kvp_ref[:, ks], kvc_ref[:, ks]], axis=0)
    v = jnp.concatenate([kvp_ref[:, vs], kvc_ref[:, vs]], axis=0)
    return k, v


def attn_fwd(qkv, sinks):
    s = qkv.shape[0]
    scale = 1.0 / math.sqrt(HD)

    def body(q_ref, kvc_ref, kvp_ref, sink_ref, o_ref, lse_ref):
        valid = _band_mask(pl.program_id(0))
        for kvh in range(N_KV):
            k, v = _kv_heads(kvp_ref, kvc_ref, kvh)
            for gi in range(GROUP):
                h = kvh * GROUP + gi
                hs = slice(h * HD, (h + 1) * HD)
                sc = lax.dot_general(q_ref[:, hs], k, _NT, preferred_element_type=f32) * scale
                sc = jnp.where(valid, sc, _NEG)
                sink = sink_ref[h]
                m = jnp.maximum(jnp.max(sc, axis=-1, keepdims=True), sink)
                p = jnp.exp(sc - m)
                denom = jnp.sum(p, axis=-1, keepdims=True) + jnp.exp(sink - m)
                probs = (p / denom).astype(bf16)
                o_ref[:, hs] = _dot(probs, v).astype(bf16)
                lse_ref[:, h:h + 1] = m + jnp.log(denom)

    return pl.pallas_call(
        body, grid=(s // BLK,), name="attn_fwd",
        in_specs=[
            pl.BlockSpec((BLK, N_HEADS * HD), lambda i: (i, 0)),
            pl.BlockSpec((BLK, 2 * N_KV * HD), lambda i: (i, KV_COL_BLOCK)),
            pl.BlockSpec((BLK, 2 * N_KV * HD), lambda i: (jnp.maximum(i - 1, 0), KV_COL_BLOCK)),
            pl.BlockSpec(memory_space=pltpu.SMEM),
        ],
        out_specs=[_row(BLK, D), _row(BLK, N_HEADS)],
        out_shape=[SDS((s, D), bf16), SDS((s, N_HEADS), f32)],
        compiler_params=_cp("parallel"),
    )(qkv, qkv, qkv, sinks)


def attn_out_fwd(x, o, w, b, tm):
    s = x.shape[0]

    def body(x_ref, o_ref, w_ref, b_ref, y_ref):
        y_ref[...] = x_ref[...] + _dot(o_ref[...], w_ref[...]) + b_ref[...]

    return pl.pallas_call(
        body, grid=(s // tm,), name="attn_out_fwd",
        in_specs=[_row(tm, D), _row(tm, D), _const((D, D)), _const((1, D))],
        out_specs=_row(tm, D), out_shape=SDS((s, D), f32),
        compiler_params=_cp("parallel"),
    )(x, o, w, b)


def attn_out_bwd(dy, wt, tm):
    s = dy.shape[0]

    def body(dy_ref, wt_ref, do_ref, db_ref):
        @pl.when(pl.program_id(0) == 0)
        def _():
            db_ref[...] = jnp.zeros_like(db_ref)

        dy = dy_ref[...]
        do_ref[...] = _dot(dy.astype(bf16), wt_ref[...]).astype(bf16)
        db_ref[...] += jnp.sum(dy, axis=0, keepdims=True)

    return pl.pallas_call(
        body, grid=(s // tm,), name="attn_out_bwd",
        in_specs=[_row(tm, D), _const((D, D))],
        out_specs=[_row(tm, D), _acc((1, D))],
        out_shape=[SDS((s, D), bf16), SDS((1, D), f32)],
        compiler_params=_cp("arbitrary"),
    )(dy, wt)


def attn_bwd(qkv, o, do, lse, sinks):
    s = qkv.shape[0]
    nb = s // BLK
    scale = 1.0 / math.sqrt(HD)
    kvw = 2 * N_KV * HD

    def body(q_ref, kvc_ref, kvp_ref, o_ref, do_ref, lse_ref, sink_ref, dq_ref, dkv_ref, ds_ref, carry_ref):
        i = pl.program_id(0)

        @pl.when(i == 0)
        def _():
            ds_ref[...] = jnp.zeros_like(ds_ref)
            carry_ref[...] = jnp.zeros_like(carry_ref)

        @pl.when(i < nb)
        def _():
            valid = _band_mask(i)
            for kvh in range(N_KV):
                k, v = _kv_heads(kvp_ref, kvc_ref, kvh)
                dk = jnp.zeros((2 * BLK, HD), f32)
                dv = jnp.zeros((2 * BLK, HD), f32)
                for gi in range(GROUP):
                    h = kvh * GROUP + gi
                    hs = slice(h * HD, (h + 1) * HD)
                    qh = q_ref[:, hs]
                    doh = do_ref[:, hs]
                    lse_h = lse_ref[:, h:h + 1]
                    sc = lax.dot_general(qh, k, _NT, preferred_element_type=f32) * scale
                    sc = jnp.where(valid, sc, _NEG)
                    p = jnp.exp(sc - lse_h)
                    dp = lax.dot_general(doh, v, _NT, preferred_element_type=f32)
                    dlt = jnp.sum(doh.astype(f32) * o_ref[:, hs].astype(f32), axis=-1, keepdims=True)
                    dsc = (p * (dp - dlt)).astype(bf16)
                    dq_ref[:, hs] = (_dot(dsc, k) * scale).astype(bf16)
                    dk = dk + lax.dot_general(dsc, qh, _TN, preferred_element_type=f32) * scale
                    dv = dv + lax.dot_general(p.astype(bf16), doh, _TN, preferred_element_type=f32)
                    psink = jnp.exp(sink_ref[h] - lse_h)
                    ds_ref[:, h:h + 1] += -jnp.sum(psink * dlt, axis=0, keepdims=True)
                ks = slice(kvh * HD, (kvh + 1) * HD)
                vs = slice(N_KV * HD + kvh * HD, N_KV * HD + (kvh + 1) * HD)
                dkv_ref[:, ks] = carry_ref[:, ks] + dk[:BLK]
                dkv_ref[:, vs] = carry_ref[:, vs] + dv[:BLK]
                carry_ref[:, ks] = dk[BLK:]
                carry_ref[:, vs] = dv[BLK:]

        @pl.when(i == nb)
        def _():
            dkv_ref[...] = carry_ref[...]

    cur = lambda i: (jnp.minimum(i, nb - 1), 0)
    prev = lambda i: (jnp.clip(i - 1, 0, nb - 1), KV_COL_BLOCK)
    return pl.pallas_call(
        body, grid=(nb + 1,), name="attn_bwd",
        in_specs=[
            pl.BlockSpec((BLK, D), cur),
            pl.BlockSpec((BLK, kvw), lambda i: (jnp.minimum(i, nb - 1), KV_COL_BLOCK)),
            pl.BlockSpec((BLK, kvw), prev),
            pl.BlockSpec((BLK, D), cur),
            pl.BlockSpec((BLK, D), cur),
            pl.BlockSpec((BLK, N_HEADS), cur),
            pl.BlockSpec(memory_space=pltpu.SMEM),
        ],
        out_specs=[
            pl.BlockSpec((BLK, D), cur),
            pl.BlockSpec((BLK, kvw), lambda i: (jnp.maximum(i - 1, 0), 0)),
            _acc((1, N_HEADS)),
        ],
        out_shape=[SDS((s, D), bf16), SDS((s, kvw), f32), SDS((1, N_HEADS), f32)],
        scratch_shapes=[pltpu.VMEM((BLK, kvw), f32)],
        compiler_params=_cp("arbitrary"),
    )(qkv, qkv, qkv, o, do, lse, sinks)


def qkv_bwd(dq, dkv, x, g, wt, dres, tm):
    s = x.shape[0]
    qd = N_HEADS * HD
    kvw = 2 * N_KV * HD

    def body(dq_ref, dkv_ref, x_ref, g_ref, wt_ref, dres_ref, dx_ref, dg_ref, db_ref, dkvb_ref):
        @pl.when(pl.program_id(0) == 0)
        def _():
            dg_ref[...] = jnp.zeros_like(dg_ref)
            db_ref[...] = jnp.zeros_like(db_ref)

        dq = dq_ref[...]
        dkv = dkv_ref[...]
        dkvb = dkv.astype(bf16)
        dkvb_ref[...] = dkvb
        dh = _dot(dq, wt_ref[0:qd, :]) + _dot(dkvb, wt_ref[qd:QKV, :])
        g = g_ref[...]
        _, xn, r = _rms_fwd(x_ref[...], g)
        dx, dg = _rms_bwd(xn, r, g, dh)
        dx_ref[...] = dres_ref[...] + dx
        dg_ref[...] += dg
        db_ref[:, 0:qd] += jnp.sum(dq.astype(f32), axis=0, keepdims=True)
        db_ref[:, qd:QKV] += jnp.sum(dkv, axis=0, keepdims=True)

    return pl.pallas_call(
        body, grid=(s // tm,), name="qkv_bwd",
        in_specs=[_row(tm, qd), _row(tm, kvw), _row(tm, D), _const((1, D)), _const((QKV, D)), _row(tm, D)],
        out_specs=[_row(tm, D), _acc((1, D)), _acc((1, QKV)), _row(tm, kvw)],
        out_shape=[SDS((s, D), f32), SDS((1, D), f32), SDS((1, QKV), f32), SDS((s, kvw), bf16)],
        compiler_params=_cp("arbitrary"),
    )(dq, dkv, x, g, wt, dres)


def ffn_fwd(x, g, wup, wdw, bdw, wdn, tm, cw):
    s = x.shape[0]
    tail = 8

    def body(x_ref, g_ref, wup_ref, wdw_ref, bdw_ref, wdn_ref, xo_ref, h_ref, up_ref, act_ref, carry_ref, ext_ref):
        @pl.when(pl.program_id(0) == 0)
        def _():
            carry_ref[...] = jnp.zeros_like(carry_ref)

        x = x_ref[...]
        h, _, _ = _rms_fwd(x, g_ref[...])
        hb = h.astype(bf16)
        h_ref[...] = hb
        for c in range(DFF // cw):
            cs = slice(c * cw, (c + 1) * cw)
            vs = slice(DFF + c * cw, DFF + (c + 1) * cw)
            ug = _dot(hb, wup_ref[:, cs])
            uv = _dot(hb, wup_ref[:, vs])
            up_ref[:, cs] = ug.astype(bf16)
            up_ref[:, vs] = uv.astype(bf16)
            ext_ref[0:tail, :] = carry_ref[:, cs]
            ext_ref[tail:tail + tm, :] = ug
            carry_ref[:, cs] = ug[tm - tail:, :]
            gate = (wdw_ref[0:1, cs] * ext_ref[pl.ds(tail - 2, tm), :]
                    + wdw_ref[1:2, cs] * ext_ref[pl.ds(tail - 1, tm), :]
                    + wdw_ref[2:3, cs] * ug) + bdw_ref[:, cs]
            act_ref[:, cs] = (gate * _sigmoid(gate) * uv).astype(bf16)
        xo_ref[...] = x + _dot(act_ref[...], wdn_ref[...])

    return pl.pallas_call(
        body, grid=(s // tm,), name="ffn_fwd",
        in_specs=[_row(tm, D), _const((1, D)), _const((D, 2 * DFF)), _const((3, DFF)), _const((1, DFF)),
                  _const((DFF, D))],
        out_specs=[_row(tm, D), _row(tm, D), _row(tm, 2 * DFF), _row(tm, DFF)],
        out_shape=[SDS((s, D), f32), SDS((s, D), bf16), SDS((s, 2 * DFF), bf16), SDS((s, DFF), bf16)],
        scratch_shapes=[pltpu.VMEM((tail, DFF), f32), pltpu.VMEM((tail + tm, cw), f32)],
        compiler_params=_cp("arbitrary"),
    )(x, g, wup, wdw, bdw, wdn)


def ffn_bwd(dxo, x, g, up, wdw, bdw, wdnt, wupt, tm, cw):
    s = x.shape[0]
    nt = s // tm
    hb = FFN_HALO
    rev = lambda i: (nt - 1 - i, 0)

    def body(dxo_ref, x_ref, g_ref, up_ref, halo_ref, wdw_ref, bdw_ref, wdnt_ref, wupt_ref,
             dxi_ref, dup_ref, dg_ref, dwdw_ref, dbdw_ref, carry_ref, ext_ref, ext2_ref):
        i = pl.program_id(0)

        @pl.when(i == 0)
        def _():
            carry_ref[...] = jnp.zeros_like(carry_ref)
            dg_ref[...] = jnp.zeros_like(dg_ref)
            dwdw_ref[...] = jnp.zeros_like(dwdw_ref)
            dbdw_ref[...] = jnp.zeros_like(dbdw_ref)

        has_prev = (i < nt - 1).astype(f32)
        dxo = dxo_ref[...]
        dxb = dxo.astype(bf16)
        for c in range(DFF // cw):
            cs = slice(c * cw, (c + 1) * cw)
            vs = slice(DFF + c * cw, DFF + (c + 1) * cw)
            d_act = _dot(dxb, wdnt_ref[:, cs])
            ug = up_ref[:, cs].astype(f32)
            uv = up_ref[:, vs].astype(f32)
            ext_ref[0:hb, :] = halo_ref[:, cs].astype(f32) * has_prev
            ext_ref[hb:hb + tm, :] = ug
            e0 = ext_ref[pl.ds(hb - 2, tm), :]
            e1 = ext_ref[pl.ds(hb - 1, tm), :]
            w0, w1, w2 = wdw_ref[0:1, cs], wdw_ref[1:2, cs], wdw_ref[2:3, cs]
            gate = (w0 * e0 + w1 * e1 + w2 * ug) + bdw_ref[:, cs]
            sg = _sigmoid(gate)
            dup_ref[:, vs] = (d_act * (gate * sg)).astype(bf16)
            d_gate = d_act * uv * _dsilu(gate, sg)
            dbdw_ref[:, cs] += jnp.sum(d_gate, axis=0, keepdims=True)
            dwdw_ref[0:1, cs] += jnp.sum(d_gate * e0, axis=0, keepdims=True)
            dwdw_ref[1:2, cs] += jnp.sum(d_gate * e1, axis=0, keepdims=True)
            dwdw_ref[2:3, cs] += jnp.sum(d_gate * ug, axis=0, keepdims=True)
            ext2_ref[0:tm, :] = d_gate
            ext2_ref[tm:tm + 8, :] = carry_ref[:, cs]
            carry_ref[:, cs] = d_gate[0:8, :]
            d_ug = w0 * ext2_ref[pl.ds(2, tm), :] + w1 * ext2_ref[pl.ds(1, tm), :] + w2 * d_gate
            dup_ref[:, cs] = d_ug.astype(bf16)
        dh = _dot(dup_ref[...], wupt_ref[...])
        gv = g_ref[...]
        _, xn, r = _rms_fwd(x_ref[...], gv)
        dx, dg = _rms_bwd(xn, r, gv, dh)
        dxi_ref[...] = dxo + dx
        dg_ref[...] += dg

    return pl.pallas_call(
        body, grid=(nt,), name="ffn_bwd",
        in_specs=[
            pl.BlockSpec((tm, D), rev), pl.BlockSpec((tm, D), rev), _const((1, D)),
            pl.BlockSpec((tm, 2 * DFF), rev),
            pl.BlockSpec((hb, 2 * DFF), lambda i: (jnp.maximum((nt - 1 - i) * (tm // hb) - 1, 0), 0)),
            _const((3, DFF)), _const((1, DFF)), _const((D, DFF)), _const((2 * DFF, D)),
        ],
        out_specs=[pl.BlockSpec((tm, D), rev), pl.BlockSpec((tm, 2 * DFF), rev), _acc((1, D)), _acc((3, DFF)),
                   _acc((1, DFF))],
        out_shape=[SDS((s, D), f32), SDS((s, 2 * DFF), bf16), SDS((1, D), f32), SDS((3, DFF), f32),
                   SDS((1, DFF), f32)],
        scratch_shapes=[pltpu.VMEM((8, DFF), f32), pltpu.VMEM((hb + tm, cw), f32), pltpu.VMEM((tm + 8, cw), f32)],
        compiler_params=_cp("arbitrary"),
    )(dxo, x, g, up, up, wdw, bdw, wdnt, wupt)


def pw1_fwd(x, g, w, b, tm):
    s = x.shape[0]

    def body(x_ref, g_ref, w_ref, b_ref, h_ref, a_ref, u_ref):
        h, _, _ = _rms_fwd(x_ref[...], g_ref[...])
        hb = h.astype(bf16)
        h_ref[...] = hb
        a = _dot(hb, w_ref[...]) + b_ref[...]
        a_ref[...] = a.astype(bf16)
        u_ref[...] = a[:, :D] * _sigmoid(a[:, D:])

    return pl.pallas_call(
        body, grid=(s // tm,), name="pw1_fwd",
        in_specs=[_row(tm, D), _const((1, D)), _const((D, 2 * D)), _const((1, 2 * D))],
        out_specs=[_row(tm, D), _row(tm, 2 * D), _row(tm, D)],
        out_shape=[SDS((s, D), bf16), SDS((s, 2 * D), bf16), SDS((s, D), f32)],
        compiler_params=_cp("parallel"),
    )(x, g, w, b)


def _ln_silu(c, lg, lb):
    mu = jnp.mean(c, axis=-1, keepdims=True)
    cc = c - mu
    var = jnp.mean(cc * cc, axis=-1, keepdims=True)
    rstd = lax.rsqrt(var + LN_EPS)
    xh = cc * rstd
    ln = xh * lg + lb
    sg = _sigmoid(ln)
    return xh, rstd, ln, sg


def conv_fwd(u, x, wdw, bdw, lg, lb, w2, b2, tm, rc):
    s = x.shape[0]
    hl = CONV_HALO
    off = hl - (CONV_W - 1)

    def body(u_ref, halo_ref, x_ref, wdw_ref, bdw_ref, lg_ref, lb_ref, w2_ref, b2_ref, c_ref, xo_ref, ext_ref):
        has_prev = (pl.program_id(0) > 0).astype(f32)
        ext_ref[0:hl, :] = halo_ref[...] * has_prev
        ext_ref[hl:hl + tm, :] = u_ref[...]
        for cc in range(D // LANES):
            cs = slice(cc * LANES, (cc + 1) * LANES)
            for rr in range(tm // rc):
                acc = jnp.zeros((rc, LANES), f32) + bdw_ref[:, cs]
                for j in range(CONV_W):
                    acc = acc + wdw_ref[j:j + 1, cs] * ext_ref[pl.ds(rr * rc + off + j, rc), cs]
                c_ref[rr * rc:(rr + 1) * rc, cs] = acc
        _, _, ln, sg = _ln_silu(c_ref[...], lg_ref[...], lb_ref[...])
        xo_ref[...] = x_ref[...] + _dot((ln * sg).astype(bf16), w2_ref[...]) + b2_ref[...]

    return pl.pallas_call(
        body, grid=(s // tm,), name="conv_fwd",
        in_specs=[_row(tm, D), pl.BlockSpec((hl, D), lambda i: (jnp.maximum(i * (tm // hl) - 1, 0), 0)), _row(tm, D),
                  _const((CONV_W, D)), _const((1, D)), _const((1, D)), _const((1, D)), _const((D, D)), _const((1, D))],
        out_specs=[_row(tm, D), _row(tm, D)],
        out_shape=[SDS((s, D), f32), SDS((s, D), f32)],
        scratch_shapes=[pltpu.VMEM((hl + tm, D), f32)],
        compiler_params=_cp("parallel"),
    )(u, u, x, wdw, bdw, lg, lb, w2, b2)


def conv_bwd_a(dy, c, lg, lb, w2t, tm):
    s = dy.shape[0]

    def body(dy_ref, c_ref, lg_ref, lb_ref, w2t_ref, dc_ref, u3_ref, dlg_ref, dlb_ref, db2_ref, dbdw_ref):
        @pl.when(pl.program_id(0) == 0)
        def _():
            for r in (dlg_ref, dlb_ref, db2_ref, dbdw_ref):
                r[...] = jnp.zeros_like(r)

        dy = dy_ref[...]
        lg = lg_ref[...]
        xh, rstd, ln, sg = _ln_silu(c_ref[...], lg, lb_ref[...])
        u3_ref[...] = (ln * sg).astype(bf16)
        du3 = _dot(dy.astype(bf16), w2t_ref[...])
        dln = du3 * _dsilu(ln, sg)
        dxh = dln * lg
        dc = rstd * (dxh - jnp.mean(dxh, axis=-1, keepdims=True) - xh * jnp.mean(dxh * xh, axis=-1, keepdims=True))
        dc_ref[...] = dc
        dlg_ref[...] += jnp.sum(dln * xh, axis=0, keepdims=True)
        dlb_ref[...] += jnp.sum(dln, axis=0, keepdims=True)
        db2_ref[...] += jnp.sum(dy, axis=0, keepdims=True)
        dbdw_ref[...] += jnp.sum(dc, axis=0, keepdims=True)

    return pl.pallas_call(
        body, grid=(s // tm,), name="conv_bwd_a",
        in_specs=[_row(tm, D), _row(tm, D), _const((1, D)), _const((1, D)), _const((D, D))],
        out_specs=[_row(tm, D), _row(tm, D)] + [_acc((1, D))] * 4,
        out_shape=[SDS((s, D), f32), SDS((s, D), bf16)] + [SDS((1, D), f32)] * 4,
        compiler_params=_cp("arbitrary"),
    )(dy, c, lg, lb, w2t)


def conv_bwd_b(dc, u, a, wdw, x, g, w1t, dres, tm, rc):
    s = x.shape[0]
    nt = s // tm
    hl = CONV_HALO
    off = hl - (CONV_W - 1)

    def body(dc_ref, dnext_ref, u_ref, uprev_ref, a_ref, wdw_ref, x_ref, g_ref, w1t_ref, dres_ref,
             dx_ref, da_ref, dwdw_ref, db1_ref, dg_ref, ext_ref, ext2_ref, du_ref):
        i = pl.program_id(0)

        @pl.when(i == 0)
        def _():
            for r in (dwdw_ref, db1_ref, dg_ref):
                r[...] = jnp.zeros_like(r)

        ext_ref[0:hl, :] = uprev_ref[...] * (i > 0).astype(f32)
        ext_ref[hl:hl + tm, :] = u_ref[...]
        ext2_ref[0:tm, :] = dc_ref[...]
        ext2_ref[tm:tm + hl, :] = dnext_ref[...] * (i < nt - 1).astype(f32)
        for cc in range(D // LANES):
            cs = slice(cc * LANES, (cc + 1) * LANES)
            for rr in range(tm // rc):
                r0 = rr * rc
                dcb = ext2_ref[r0:r0 + rc, cs]
                acc = jnp.zeros((rc, LANES), f32)
                for j in range(CONV_W):
                    acc = acc + wdw_ref[j:j + 1, cs] * ext2_ref[pl.ds(r0 + CONV_W - 1 - j, rc), cs]
                    prod = dcb * ext_ref[pl.ds(r0 + off + j, rc), cs]
                    dwdw_ref[j, :, cs] += jnp.sum(prod.reshape(rc // 8, 8, LANES), axis=0)
                du_ref[r0:r0 + rc, cs] = acc
        du = du_ref[...]
        a1 = a_ref[:, :D].astype(f32)
        sg = _sigmoid(a_ref[:, D:].astype(f32))
        da1 = du * sg
        da2 = du * a1 * sg * (1.0 - sg)
        da_ref[:, :D] = da1.astype(bf16)
        da_ref[:, D:] = da2.astype(bf16)
        db1_ref[:, :D] += jnp.sum(da1, axis=0, keepdims=True)
        db1_ref[:, D:] += jnp.sum(da2, axis=0, keepdims=True)
        dh = _dot(da_ref[...], w1t_ref[...])
        gv = g_ref[...]
        _, xn, r = _rms_fwd(x_ref[...], gv)
        dx, dg = _rms_bwd(xn, r, gv, dh)
        dx_ref[...] = dres_ref[...] + dx
        dg_ref[...] += dg

    blocks = tm // hl
    return pl.pallas_call(
        body, grid=(nt,), name="conv_bwd_b",
        in_specs=[
            _row(tm, D), pl.BlockSpec((hl, D), lambda i: (jnp.minimum((i + 1) * blocks, s // hl - 1), 0)),
            _row(tm, D), pl.BlockSpec((hl, D), lambda i: (jnp.maximum(i * blocks - 1, 0), 0)),
            _row(tm, 2 * D), _const((CONV_W, D)), _row(tm, D), _const((1, D)), _const((2 * D, D)), _row(tm, D),
        ],
        out_specs=[_row(tm, D), _row(tm, 2 * D), _acc((CONV_W, 8, D)), _acc((1, 2 * D)), _acc((1, D))],
        out_shape=[SDS((s, D), f32), SDS((s, 2 * D), bf16), SDS((CONV_W, 8, D), f32), SDS((1, 2 * D), f32),
                   SDS((1, D), f32)],
        scratch_shapes=[pltpu.VMEM((hl + tm, D), f32), pltpu.VMEM((tm + hl, D), f32), pltpu.VMEM((tm, D), f32)],
        compiler_params=_cp("arbitrary"),
    )(dc, dc, u, u, a, wdw, x, g, w1t, dres)


def loss_bwd(x, g, t, tm):
    s = x.shape[0]

    def body(x_ref, g_ref, t_ref, dx_ref, dg_ref, loss_ref):
        @pl.when(pl.program_id(0) == 0)
        def _():
            dg_ref[...] = jnp.zeros_like(dg_ref)
            loss_ref[...] = jnp.zeros_like(loss_ref)

        gv = g_ref[...]
        y, xn, r = _rms_fwd(x_ref[...], gv)
        e = y - t_ref[...]
        loss_ref[...] += 0.5 * jnp.sum(jnp.mean(e * e, axis=-1, keepdims=True), axis=0, keepdims=True)
        dx, dg = _rms_bwd(xn, r, gv, e / D)
        dx_ref[...] = dx
        dg_ref[...] += dg

    return pl.pallas_call(
        body, grid=(s // tm,), name="loss_bwd",
        in_specs=[_row(tm, D), _const((1, D)), _row(tm, D)],
        out_specs=[_row(tm, D), _acc((1, D)), _acc((1, 1))],
        out_shape=[SDS((s, D), f32), SDS((1, D), f32), SDS((1, 1), f32)],
        compiler_params=_cp("arbitrary"),
    )(x, g, t)


def wgrad(a, b, nb, tk, name):
    s, k1 = a.shape
    n = b.shape[1]

    def body(a_ref, b_ref, o_ref):
        @pl.when(pl.program_id(1) == 0)
        def _():
            o_ref[...] = jnp.zeros_like(o_ref)

        o_ref[...] += lax.dot_general(a_ref[...], b_ref[...].astype(bf16), _TN, preferred_element_type=f32)

    return pl.pallas_call(
        body, grid=(n // nb, s // tk), name=name,
        in_specs=[pl.BlockSpec((tk, k1), lambda j, k: (k, 0)), pl.BlockSpec((tk, nb), lambda j, k: (k, j))],
        out_specs=pl.BlockSpec((k1, nb), lambda j, k: (0, j)),
        out_shape=SDS((k1, n), f32),
        compiler_params=_cp("parallel", "arbitrary"),
    )(a, b)


def _adam_math(w, g, m, v):
    m = B1 * m + (1.0 - B1) * g
    v = B2 * v + (1.0 - B2) * (g * g)
    m_hat = m / (1.0 - B1 ** STEP)
    v_hat = v / (1.0 - B2 ** STEP)
    delta = -LR * (m_hat / (jnp.sqrt(v_hat) + ADAM_EPS) + WD * w)
    return delta, m, v


def adamw(w, g, m, v, name):
    r, c = w.shape
    tr = r
    if r * c * 4 > ADAM_BLOCK_BYTES:
        tr = next(cand for cand in (512, 256, 128, 64, 32, 16, 8) if r % cand == 0 and cand * c * 4 <= ADAM_BLOCK_BYTES)

    def body(w_ref, g_ref, m_ref, v_ref, d_ref, mo_ref, vo_ref):
        d, mn, vn = _adam_math(w_ref[...], g_ref[...], m_ref[...], v_ref[...])
        d_ref[...] = d
        mo_ref[...] = mn
        vo_ref[...] = vn

    return pl.pallas_call(
        body, grid=(r // tr,), name=name,
        in_specs=[_row(tr, c)] * 4, out_specs=[_row(tr, c)] * 3,
        out_shape=[SDS((r, c), f32)] * 3,
        compiler_params=_cp("parallel"),
    )(w, g, m, v)


def local_grads(x, t, p, tm=512, tmf=256, cw=256, rc=128, tk=1024):
    s = x.shape[0]
    tm = min(tm, s)
    tk = min(tk, s)
    tmf = min(tmf, s)
    row = lambda v: v.reshape(1, -1)

    h0, qkv = qkv_fwd(x, row(p["norm_mix"][0]), p["w_qkv"], p["b_qkv"], tm)
    o, lse = attn_fwd(qkv, p["sinks"])
    x1 = attn_out_fwd(x, o, p["w_o"], p["b_o"], tm)
    x2, h1, up0, act0 = ffn_fwd(x1, row(p["norm_ffn"][0]), p["w_up"][0], p["ffn_w_dw"][0], row(p["ffn_b_dw"][0]),
                                p["w_down"][0], tmf, cw)
    h2, a, u = pw1_fwd(x2, row(p["norm_mix"][1]), p["w_pw1"], p["b_pw1"], tm)
    c, x3 = conv_fwd(u, x2, p["conv_w_dw"], p["conv_b_dw"], p["ln_g"], p["ln_b"], p["w_pw2"], p["b_pw2"], tm, rc)
    x4, h3, up1, act1 = ffn_fwd(x3, row(p["norm_ffn"][1]), p["w_up"][1], p["ffn_w_dw"][1], row(p["ffn_b_dw"][1]),
                                p["w_down"][1], tmf, cw)

    g = {}
    dx4, g["final_norm"], loss = loss_bwd(x4, p["final_norm"], t, tm)
    dx3, dup1, dgf1, dwdw1, dbdw1 = ffn_bwd(dx4, x3, row(p["norm_ffn"][1]), up1, p["ffn_w_dw"][1],
                                            row(p["ffn_b_dw"][1]), p["w_down_t"][1], p["w_up_t"][1], tmf, cw)
    gw_up1 = wgrad(h3, dup1, 512, tk, "wgrad_up1")
    gw_dn1 = wgrad(act1, dx4, 512, tk, "wgrad_down1")
    dc, u3, g["ln_g"], g["ln_b"], g["b_pw2"], g["conv_b_dw"] = conv_bwd_a(dx3, c, p["ln_g"], p["ln_b"], p["w_pw2_t"], tm)
    g["w_pw2"] = wgrad(u3, dx3, 512, tk, "wgrad_pw2")
    dx2, da, dwdw8, g["b_pw1"], dgm1 = conv_bwd_b(dc, u, a, p["conv_w_dw"], x2, row(p["norm_mix"][1]), p["w_pw1_t"],
                                                  dx3, tm, rc)
    g["conv_w_dw"] = jnp.sum(dwdw8, axis=1)
    g["w_pw1"] = wgrad(h2, da, 512, tk, "wgrad_pw1")
    dx1, dup0, dgf0, dwdw0, dbdw0 = ffn_bwd(dx2, x1, row(p["norm_ffn"][0]), up0, p["ffn_w_dw"][0],
                                            row(p["ffn_b_dw"][0]), p["w_down_t"][0], p["w_up_t"][0], tmf, cw)
    gw_up0 = wgrad(h1, dup0, 512, tk, "wgrad_up0")
    gw_dn0 = wgrad(act0, dx2, 512, tk, "wgrad_down0")
    do, g["b_o"] = attn_out_bwd(dx1, p["w_o_t"], tm)
    g["w_o"] = wgrad(o, dx1, 512, tk, "wgrad_o")
    dq, dkv, g["sinks"] = attn_bwd(qkv, o, do, lse, p["sinks"])
    dx0, dgm0, g["b_qkv"], dkvb = qkv_bwd(dq, dkv, x, row(p["norm_mix"][0]), p["w_qkv_t"], dx1, tm)
    g["w_qkv"] = jnp.concatenate([wgrad(h0, dq, 512, tk, "wgrad_q"), wgrad(h0, dkvb, 2 * N_KV * HD, tk, "wgrad_kv")],
                                 axis=1)
    g["norm_mix"] = jnp.concatenate([dgm0, dgm1], axis=0)
    g["norm_ffn"] = jnp.concatenate([dgf0, dgf1], axis=0)
    g["w_up"] = jnp.stack([gw_up0, gw_up1])
    g["w_down"] = jnp.stack([gw_dn0, gw_dn1])
    g["ffn_w_dw"] = jnp.stack([dwdw0, dwdw1])
    g["ffn_b_dw"] = jnp.concatenate([dbdw0, dbdw1], axis=0)
    return loss, dx0, g


_ANY = pl.BlockSpec(memory_space=pl.ANY)
_VMEM = pl.BlockSpec(memory_space=pltpu.VMEM)


def _place():
    return lax.axis_index("x"), lax.axis_index("y"), lax.axis_index("c")


def _chip_peer(xi, yi, r):
    px = 1 - xi if r & 2 else xi
    py = 1 - yi if r & 1 else yi
    return px, py


def gather_chips(arrs, name):
    n = len(arrs)

    def body(*refs):
        ins, outs = refs[:n], refs[n:2 * n]
        send_sems, recv_sems, loc_sems = refs[2 * n:]
        xi, yi, ci = _place()
        me = 2 * xi + yi
        local, sends = [], []
        for a in range(n):
            cp = pltpu.make_async_copy(ins[a], outs[a].at[me], loc_sems.at[a])
            cp.start()
            local.append(cp)

        def copy(a, r, slot):
            px, py = _chip_peer(xi, yi, r)
            k = 3 * a + r - 1
            return pltpu.make_async_remote_copy(
                src_ref=ins[a], dst_ref=outs[a].at[slot], send_sem=send_sems.at[k], recv_sem=recv_sems.at[k],
                device_id=(px, py, ci), device_id_type=MESH)

        for a in range(n):
            for r in (1, 2, 3):
                cp = copy(a, r, me)
                cp.start()
                sends.append(cp)
        for a in range(n):
            for r in (1, 2, 3):
                px, py = _chip_peer(xi, yi, r)
                copy(a, r, 2 * px + py).wait_recv()
        for cp in sends:
            cp.wait_send()
        for cp in local:
            cp.wait()

    return pl.pallas_call(
        body, name=name, in_specs=[_ANY] * n, out_specs=[_ANY] * n,
        out_shape=[SDS((N_CHIPS,) + a.shape, a.dtype) for a in arrs],
        scratch_shapes=[pltpu.SemaphoreType.DMA((3 * n,)), pltpu.SemaphoreType.DMA((3 * n,)),
                        pltpu.SemaphoreType.DMA((n,))],
    )(*arrs)


def allreduce_small(v):
    rows = v.shape[0]

    def body(v_ref, o_ref, buf_ref, send_sems, recv_sems):
        xi, yi, ci = _place()
        me = 4 * xi + 2 * yi + ci
        buf_ref[me] = v_ref[...]

        def copy(r, slot):
            px, py = _chip_peer(xi, yi, r >> 1)
            pc = 1 - ci if r & 1 else ci
            return pltpu.make_async_remote_copy(
                src_ref=v_ref, dst_ref=buf_ref.at[slot], send_sem=send_sems.at[r - 1], recv_sem=recv_sems.at[r - 1],
                device_id=(px, py, pc), device_id_type=MESH), 4 * px + 2 * py + pc

        sends = []
        for r in range(1, N_DEV):
            cp, _ = copy(r, me)
            cp.start()
            sends.append(cp)
        for r in range(1, N_DEV):
            _, peer = copy(r, me)
            copy(r, peer)[0].wait_recv()
        for cp in sends:
            cp.wait_send()
        acc = buf_ref[0]
        for d in range(1, N_DEV):
            acc = acc + buf_ref[d]
        o_ref[...] = acc

    return pl.pallas_call(
        body, name="allreduce_small", in_specs=[_VMEM], out_specs=_VMEM, out_shape=SDS(v.shape, f32),
        scratch_shapes=[pltpu.VMEM((N_DEV, rows, LANES), f32), pltpu.SemaphoreType.DMA((N_DEV - 1,)),
                        pltpu.SemaphoreType.DMA((N_DEV - 1,))],
    )(v)


def rs_pair_send(g):
    def body(g_ref, o_ref, send_sem, recv_sem):
        xi, yi, ci = _place()
        cp = pltpu.make_async_remote_copy(
            src_ref=g_ref.at[1 - ci], dst_ref=o_ref, send_sem=send_sem, recv_sem=recv_sem,
            device_id=(xi, yi, 1 - ci), device_id_type=MESH)
        cp.start()
        cp.wait()

    return pl.pallas_call(
        body, name="rs_pair_send", in_specs=[_ANY], out_specs=_ANY, out_shape=SDS(g.shape[1:], g.dtype),
        scratch_shapes=[pltpu.SemaphoreType.DMA, pltpu.SemaphoreType.DMA],
    )(g)


def rs_pair_add(g, other, ci, tr):
    _, nsh, h, w = g.shape

    def body(c_ref, g_ref, o_ref, out_ref):
        out_ref[...] = g_ref[0] + o_ref[...]

    return pl.pallas_call(
        body, name="rs_pair_add",
        grid_spec=pltpu.PrefetchScalarGridSpec(
            num_scalar_prefetch=1, grid=(nsh, h // tr),
            in_specs=[pl.BlockSpec((1, 1, tr, w), lambda j, r, c: (c[0], j, r, 0)),
                      pl.BlockSpec((1, tr, w), lambda j, r, c: (j, r, 0))],
            out_specs=pl.BlockSpec((1, tr, w), lambda j, r, c: (j, r, 0))),
        out_shape=SDS((nsh, h, w), f32),
        compiler_params=_cp("parallel", "parallel"),
    )(ci.reshape(1).astype(jnp.int32), g, other)


def rs_chip_send(p):
    def body(p_ref, o_ref, send_sems, recv_sems, loc_sem):
        xi, yi, ci = _place()
        me = 2 * xi + yi
        loc = pltpu.make_async_copy(p_ref.at[me], o_ref.at[me], loc_sem)
        loc.start()

        def copy(r, src_slot, dst_slot):
            px, py = _chip_peer(xi, yi, r)
            return pltpu.make_async_remote_copy(
                src_ref=p_ref.at[src_slot], dst_ref=o_ref.at[dst_slot], send_sem=send_sems.at[r - 1],
                recv_sem=recv_sems.at[r - 1], device_id=(px, py, ci), device_id_type=MESH)

        sends = []
        for r in (1, 2, 3):
            px, py = _chip_peer(xi, yi, r)
            cp = copy(r, 2 * px + py, me)
            cp.start()
            sends.append(cp)
        for r in (1, 2, 3):
            px, py = _chip_peer(xi, yi, r)
            copy(r, me, 2 * px + py).wait_recv()
        for cp in sends:
            cp.wait_send()
        loc.wait()

    return pl.pallas_call(
        body, name="rs_chip_send", in_specs=[_ANY], out_specs=_ANY, out_shape=SDS(p.shape, p.dtype),
        scratch_shapes=[pltpu.SemaphoreType.DMA((3,)), pltpu.SemaphoreType.DMA((3,)), pltpu.SemaphoreType.DMA],
    )(p)


def rs_chip_add(parts, tr):
    n, h, w = parts.shape

    def body(p_ref, o_ref):
        o_ref[...] = ((p_ref[0] + p_ref[1]) + p_ref[2]) + p_ref[3]

    return pl.pallas_call(
        body, name="rs_chip_add", grid=(h // tr,),
        in_specs=[pl.BlockSpec((n, tr, w), lambda r: (0, r, 0))], out_specs=pl.BlockSpec((tr, w), lambda r: (r, 0)),
        out_shape=SDS((h, w), f32), compiler_params=_cp("parallel"),
    )(parts)


def pair_exchange(half):
    def body(h_ref, o_ref, send_sem, recv_sem, loc_sem):
        xi, yi, ci = _place()
        loc = pltpu.make_async_copy(h_ref, o_ref.at[ci], loc_sem)
        loc.start()

        def copy(slot):
            return pltpu.make_async_remote_copy(
                src_ref=h_ref, dst_ref=o_ref.at[slot], send_sem=send_sem, recv_sem=recv_sem,
                device_id=(xi, yi, 1 - ci), device_id_type=MESH)

        cp = copy(ci)
        cp.start()
        copy(1 - ci).wait_recv()
        cp.wait_send()
        loc.wait()

    return pl.pallas_call(
        body, name="pair_exchange", in_specs=[_ANY], out_specs=_ANY, out_shape=SDS((2,) + half.shape, half.dtype),
        scratch_shapes=[pltpu.SemaphoreType.DMA, pltpu.SemaphoreType.DMA, pltpu.SemaphoreType.DMA],
    )(half)


PACK_W = 1024
BIG = (("attn_w_qkv", (1, D, QKV // 4), 2), ("attn_w_o", (1, D // 4, D), 1), ("conv_w_pw1", (1, D, 2 * D // 4), 2),
       ("conv_w_pw2", (1, D // 4, D), 1), ("ffn_w_up", (2, D, 2 * DFF // 4), 2), ("ffn_w_down", (2, DFF // 4, D), 1))
SMALL_SH = (("conv_b_pw1", (1, 2 * D // 4), 1), ("conv_w_dw", (1, CONV_W, D // 4), 2), ("conv_b_dw", (1, D // 4), 1),
            ("conv_ln_g", (1, D // 4), 1), ("conv_ln_b", (1, D // 4), 1), ("conv_b_pw2", (1, D // 4), 1),
            ("ffn_w_dw", (2, 3, DFF // 4), 2))
SMALL_REP = (("norm_mix", (2, D)), ("attn_b_qkv", (1, QKV)), ("attn_sinks", (1, N_HEADS)), ("attn_b_o", (1, D)),
             ("norm_ffn", (2, D)), ("ffn_b_dw", (2, DFF)), ("final_norm", (D,)))


def _size(shape):
    return math.prod(shape)


def _pack(pieces, width, row_multiple, dtype):
    flat = jnp.concatenate([p.astype(dtype).reshape(-1) for p in pieces])
    unit = width * row_multiple
    pad = -flat.shape[0] % unit
    if pad:
        flat = jnp.concatenate([flat, jnp.zeros((pad,), dtype)])
    return flat.reshape(-1, width)


def _unpack(flat, shapes):
    out, off = [], 0
    for shp in shapes:
        n = _size(shp)
        out.append(flat[off:off + n].reshape(shp))
        off += n
    return out


def _join_shards(g4, spec):
    out, off = {}, 0
    for name, shp, axis in spec:
        n = _size(shp)
        parts = g4[:, off:off + n].reshape((N_CHIPS,) + shp)
        out[name] = jnp.concatenate([parts[j] for j in range(N_CHIPS)], axis=axis)
        off += n
    return out


def _split_shards(full, spec):
    rows = []
    for j in range(N_CHIPS):
        pieces = []
        for name, shp, axis in spec:
            w = shp[axis]
            pieces.append(lax.slice_in_dim(full[name], j * w, (j + 1) * w, axis=axis).reshape(-1))
        rows.append(jnp.concatenate(pieces))
    return jnp.stack(rows)


def kernel(x, norm_mix, attn_w_qkv, attn_b_qkv, attn_sinks, attn_w_o, attn_b_o, conv_w_pw1, conv_b_pw1, conv_w_dw, conv_b_dw, conv_ln_g, conv_ln_b, conv_w_pw2, conv_b_pw2, norm_ffn, ffn_w_up, ffn_w_dw, ffn_b_dw, ffn_w_down, final_norm, loss_target, m_norm_mix, m_attn_w_qkv, m_attn_b_qkv, m_attn_sinks, m_attn_w_o, m_attn_b_o, m_conv_w_pw1, m_conv_b_pw1, m_conv_w_dw, m_conv_b_dw, m_conv_ln_g, m_conv_ln_b, m_conv_w_pw2, m_conv_b_pw2, m_norm_ffn, m_ffn_w_up, m_ffn_w_dw, m_ffn_b_dw, m_ffn_w_down, m_final_norm, v_norm_mix, v_attn_w_qkv, v_attn_b_qkv, v_attn_sinks, v_attn_w_o, v_attn_b_o, v_conv_w_pw1, v_conv_b_pw1, v_conv_w_dw, v_conv_b_dw, v_conv_ln_g, v_conv_ln_b, v_conv_w_pw2, v_conv_b_pw2, v_norm_ffn, v_ffn_w_up, v_ffn_w_dw, v_ffn_b_dw, v_ffn_w_down, v_final_norm):
    w = dict(norm_mix=norm_mix, attn_w_qkv=attn_w_qkv, attn_b_qkv=attn_b_qkv, attn_sinks=attn_sinks, attn_w_o=attn_w_o,
             attn_b_o=attn_b_o, conv_w_pw1=conv_w_pw1, conv_b_pw1=conv_b_pw1, conv_w_dw=conv_w_dw, conv_b_dw=conv_b_dw,
             conv_ln_g=conv_ln_g, conv_ln_b=conv_ln_b, conv_w_pw2=conv_w_pw2, conv_b_pw2=conv_b_pw2, norm_ffn=norm_ffn,
             ffn_w_up=ffn_w_up, ffn_w_dw=ffn_w_dw, ffn_b_dw=ffn_b_dw, ffn_w_down=ffn_w_down, final_norm=final_norm)
    mom = dict(norm_mix=m_norm_mix, attn_w_qkv=m_attn_w_qkv, attn_b_qkv=m_attn_b_qkv, attn_sinks=m_attn_sinks,
               attn_w_o=m_attn_w_o, attn_b_o=m_attn_b_o, conv_w_pw1=m_conv_w_pw1, conv_b_pw1=m_conv_b_pw1,
               conv_w_dw=m_conv_w_dw, conv_b_dw=m_conv_b_dw, conv_ln_g=m_conv_ln_g, conv_ln_b=m_conv_ln_b,
               conv_w_pw2=m_conv_w_pw2, conv_b_pw2=m_conv_b_pw2, norm_ffn=m_norm_ffn, ffn_w_up=m_ffn_w_up,
               ffn_w_dw=m_ffn_w_dw, ffn_b_dw=m_ffn_b_dw, ffn_w_down=m_ffn_w_down, final_norm=m_final_norm)
    vel = dict(norm_mix=v_norm_mix, attn_w_qkv=v_attn_w_qkv, attn_b_qkv=v_attn_b_qkv, attn_sinks=v_attn_sinks,
               attn_w_o=v_attn_w_o, attn_b_o=v_attn_b_o, conv_w_pw1=v_conv_w_pw1, conv_b_pw1=v_conv_b_pw1,
               conv_w_dw=v_conv_w_dw, conv_b_dw=v_conv_b_dw, conv_ln_g=v_conv_ln_g, conv_ln_b=v_conv_ln_b,
               conv_w_pw2=v_conv_w_pw2, conv_b_pw2=v_conv_b_pw2, norm_ffn=v_norm_ffn, ffn_w_up=v_ffn_w_up,
               ffn_w_dw=v_ffn_w_dw, ffn_b_dw=v_ffn_b_dw, ffn_w_down=v_ffn_w_down, final_norm=v_final_norm)
    order = ("norm_mix", "attn_w_qkv", "attn_b_qkv", "attn_sinks", "attn_w_o", "attn_b_o", "conv_w_pw1", "conv_b_pw1",
             "conv_w_dw", "conv_b_dw", "conv_ln_g", "conv_ln_b", "conv_w_pw2", "conv_b_pw2", "norm_ffn", "ffn_w_up",
             "ffn_w_dw", "ffn_b_dw", "ffn_w_down", "final_norm")
    xi, yi, ci = _place()
    chip = 2 * xi + yi

    big_mine = _pack([w[n] for n, _, _ in BIG], PACK_W, 16, bf16)
    small_mine = _pack([w[n] for n, _, _ in SMALL_SH], LANES, 8, f32)
    big_all, small_all = gather_chips([big_mine, small_mine], "gather_weights")
    wb = _join_shards(big_all.reshape(N_CHIPS, -1), BIG)
    ws = _join_shards(small_all.reshape(N_CHIPS, -1), SMALL_SH)
    p = {
        "norm_mix": norm_mix, "norm_ffn": norm_ffn, "final_norm": final_norm.reshape(1, D),
        "w_qkv": wb["attn_w_qkv"][0], "w_qkv_t": wb["attn_w_qkv"][0].T, "b_qkv": attn_b_qkv,
        "sinks": attn_sinks.reshape(N_HEADS),
        "w_o": wb["attn_w_o"][0], "w_o_t": wb["attn_w_o"][0].T, "b_o": attn_b_o,
        "w_pw1": wb["conv_w_pw1"][0], "w_pw1_t": wb["conv_w_pw1"][0].T, "b_pw1": ws["conv_b_pw1"],
        "conv_w_dw": ws["conv_w_dw"][0], "conv_b_dw": ws["conv_b_dw"], "ln_g": ws["conv_ln_g"], "ln_b": ws["conv_ln_b"],
        "w_pw2": wb["conv_w_pw2"][0], "w_pw2_t": wb["conv_w_pw2"][0].T, "b_pw2": ws["conv_b_pw2"],
        "w_up": wb["ffn_w_up"], "w_up_t": jnp.swapaxes(wb["ffn_w_up"], 1, 2),
        "ffn_w_dw": ws["ffn_w_dw"], "ffn_b_dw": ffn_b_dw,
        "w_down": wb["ffn_w_down"], "w_down_t": jnp.swapaxes(wb["ffn_w_down"], 1, 2),
    }

    loss, dx, g = local_grads(x[0], loss_target[0], p)

    gfull = {"attn_w_qkv": g["w_qkv"][None], "attn_w_o": g["w_o"][None], "conv_w_pw1": g["w_pw1"][None],
             "conv_w_pw2": g["w_pw2"][None], "ffn_w_up": g["w_up"], "ffn_w_down": g["w_down"]}
    g4 = _split_shards(gfull, BIG)
    half_rows = g4.shape[1] // (2 * PACK_W)
    gsend = jnp.swapaxes(g4.reshape(N_CHIPS, 2, half_rows, PACK_W), 0, 1)
    from_pair = rs_pair_send(gsend)
    chip_part = rs_pair_add(gsend, from_pair, ci, half_rows // 3)
    parts = rs_chip_send(chip_part)
    my_half = rs_chip_add(parts, half_rows // 6)
    gred = pair_exchange(my_half).reshape(-1)
    gbig = dict(zip([n for n, _, _ in BIG], _unpack(gred, [shp for _, shp, _ in BIG])))

    gsmall_full = {"conv_b_pw1": g["b_pw1"], "conv_w_dw": g["conv_w_dw"][None], "conv_b_dw": g["conv_b_dw"],
                   "conv_ln_g": g["ln_g"], "conv_ln_b": g["ln_b"], "conv_b_pw2": g["b_pw2"], "ffn_w_dw": g["ffn_w_dw"]}
    grep = {"norm_mix": g["norm_mix"], "attn_b_qkv": g["b_qkv"], "attn_sinks": g["sinks"], "attn_b_o": g["b_o"],
            "norm_ffn": g["norm_ffn"], "ffn_b_dw": g["ffn_b_dw"], "final_norm": g["final_norm"]}
    full_shapes = [tuple(s * (N_CHIPS if a == axis else 1) for a, s in enumerate(shp)) for _, shp, axis in SMALL_SH]
    vec = _pack([grep[n] for n, _ in SMALL_REP] + [gsmall_full[n] for n, _, _ in SMALL_SH] + [loss], LANES, 8, f32)
    red = allreduce_small(vec).reshape(-1)
    red_parts = _unpack(red, [shp for _, shp in SMALL_REP] + full_shapes + [()])
    gout = dict(zip([n for n, _ in SMALL_REP], red_parts[:len(SMALL_REP)]))
    for (name, shp, axis), full in zip(SMALL_SH, red_parts[len(SMALL_REP):-1]):
        gout[name] = lax.dynamic_slice_in_dim(full, chip * shp[axis], shp[axis], axis=axis)
    loss_out = red_parts[-1]
    gout.update(gbig)

    delta, new_m, new_v = {}, {}, {}
    for name, shp, _ in BIG:
        r2 = (shp[0] * shp[1], shp[2])
        d_, m_, v_ = adamw(w[name].reshape(r2), gout[name].reshape(r2), mom[name].reshape(r2), vel[name].reshape(r2),
                           "adamw_" + name)
        delta[name], new_m[name], new_v[name] = d_.reshape(shp), m_.reshape(shp), v_.reshape(shp)
    small_names = [n for n, _ in SMALL_REP] + [n for n, _, _ in SMALL_SH]
    small_shapes = [shp for _, shp in SMALL_REP] + [shp for _, shp, _ in SMALL_SH]
    packed = [_pack([src[n] for n in small_names], LANES, 8, f32) for src in (w, gout, mom, vel)]
    outs = adamw(*packed, "adamw_small")
    for dst, arr in zip((delta, new_m, new_v), outs):
        dst.update(zip(small_names, _unpack(arr.reshape(-1), small_shapes)))

    return (loss_out, dx[None], *[gout[n].reshape(w[n].shape) for n in order], *[delta[n] for n in order],
            *[new_m[n] for n in order], *[new_v[n] for n in order])
```

```python
import math

import jax
import jax.numpy as jnp
from jax import lax
from jax.experimental import pallas as pl
from jax.experimental.pallas import tpu as pltpu

f32 = jnp.float32
bf16 = jnp.bfloat16
SDS = jax.ShapeDtypeStruct
MESH = pl.DeviceIdType.MESH

D = 1024
N_HEADS = 16
N_KV = 2
GROUP = 8
HD = 64
BLK = 128
QKV = (N_HEADS + 2 * N_KV) * HD
KV_COL_BLOCK = (N_HEADS * HD) // (2 * N_KV * HD)
CONV_W = 31
CONV_HALO = 32
DFF = 2816
FFN_HALO = 16
RMS_EPS = 1e-6
LN_EPS = 1e-5
LR, B1, B2, ADAM_EPS, WD, STEP = 0.001, 0.9, 0.999, 1e-08, 0.01, 10

N_CHIPS = 4
N_DEV = 8
VMEM_LIMIT = 56 * 1024 * 1024
LANES = 128
SUB = 8
ELEMENTWISE_BLOCK_BYTES = 1 << 20


def _cp(*sem):
    return pltpu.CompilerParams(dimension_semantics=sem, vmem_limit_bytes=VMEM_LIMIT)


def _row(tm, n):
    return pl.BlockSpec((tm, n), lambda i: (i, 0))


def _const(shape):
    return pl.BlockSpec(shape, lambda *_: (0,) * len(shape), pipeline_mode=pl.Buffered(1))


def _acc(shape):
    return pl.BlockSpec(shape, lambda *_: (0,) * len(shape))


def _rms_fwd(x, g):
    r = lax.rsqrt(jnp.mean(x * x, axis=-1, keepdims=True) + RMS_EPS)
    xn = x * r
    return xn * g, xn, r


def _colsum8(v):
    return jnp.sum(v.reshape(v.shape[0] // SUB, SUB, v.shape[1]), axis=0)


def _rms_bwd(xn, r, g, dh):
    dyn = dh * g
    dx = r * (dyn - xn * jnp.mean(dyn * xn, axis=-1, keepdims=True))
    return dx, _colsum8(dh * xn)


def _sigmoid(z):
    return 1.0 / (1.0 + jnp.exp(-z))


def _dsilu(z, sg):
    return sg * (1.0 + z * (1.0 - sg))


def _dot(a, b):
    return jnp.dot(a, b, preferred_element_type=f32)


def qkv_fwd(x, g, w, b, tm):
    s = x.shape[0]

    def body(x_ref, g_ref, w_ref, b_ref, h_ref, o_ref):
        h, _, _ = _rms_fwd(x_ref[...], g_ref[...])
        hb = h.astype(bf16)
        h_ref[...] = hb
        o_ref[...] = (_dot(hb, w_ref[...]) + b_ref[...]).astype(bf16)

    return pl.pallas_call(
        body, grid=(s // tm,), name="qkv_fwd",
        in_specs=[_row(tm, D), _const((1, D)), _const((D, QKV)), _const((1, QKV))],
        out_specs=[_row(tm, D), _row(tm, QKV)],
        out_shape=[SDS((s, D), bf16), SDS((s, QKV), bf16)],
        compiler_params=_cp("parallel"),
    )(x, g, w, b)


def _band_mask(i):
    qi = lax.broadcasted_iota(jnp.int32, (BLK, 2 * BLK), 0)
    ki = lax.broadcasted_iota(jnp.int32, (BLK, 2 * BLK), 1)
    dist = qi + BLK - ki
    return (dist >= 0) & (dist < BLK) & ((ki >= BLK) | (i > 0))


_NEG = float(jnp.finfo(jnp.float32).min)
_NT = (((1,), (1,)), ((), ()))
_TN = (((0,), (0,)), ((), ()))


def _kv_heads(kvp_ref, kvc_ref, kvh):
    ks = slice(kvh * HD, (kvh + 1) * HD)
    vs = slice(N_KV * HD + kvh * HD, N_KV * HD + (kvh + 1) * HD)
    k = jnp.concatenate([kvp_ref[:, ks], kvc_ref[:, ks]], axis=0)
    v = jnp.concatenate([kvp_ref[:, vs], kvc_ref[:, vs]], axis=0)
    return k, v


def attn_fwd(qkv, sinks):
    s = qkv.shape[0]
    scale = 1.0 / math.sqrt(HD)

    def body(q_ref, kvc_ref, kvp_ref, sink_ref, o_ref, lse_ref):
        valid = _band_mask(pl.program_id(0))
        for kvh in range(N_KV):
            k, v = _kv_heads(kvp_ref, kvc_ref, kvh)
            for gi in range(GROUP):
                h = kvh * GROUP + gi
                hs = slice(h * HD, (h + 1) * HD)
                sc = lax.dot_general(q_ref[:, hs], k, _NT, preferred_element_type=f32) * scale
                sc = jnp.where(valid, sc, _NEG)
                sink = sink_ref[h]
                m = jnp.maximum(jnp.max(sc, axis=-1, keepdims=True), sink)
                p = jnp.exp(sc - m)
                denom = jnp.sum(p, axis=-1, keepdims=True) + jnp.exp(sink - m)
                probs = (p / denom).astype(bf16)
                o_ref[:, hs] = _dot(probs, v).astype(bf16)
                lse_ref[:, h:h + 1] = m + jnp.log(denom)

    return pl.pallas_call(
        body, grid=(s // BLK,), name="attn_fwd",
        in_specs=[
            pl.BlockSpec((BLK, N_HEADS * HD), lambda i: (i, 0)),
            pl.BlockSpec((BLK, 2 * N_KV * HD), lambda i: (i, KV_COL_BLOCK)),
            pl.BlockSpec((BLK, 2 * N_KV * HD), lambda i: (jnp.maximum(i - 1, 0), KV_COL_BLOCK)),
            pl.BlockSpec(memory_space=pltpu.SMEM),
        ],
        out_specs=[_row(BLK, D), _row(BLK, N_HEADS)],
        out_shape=[SDS((s, D), bf16), SDS((s, N_HEADS), f32)],
        compiler_params=_cp("parallel"),
    )(qkv, qkv, qkv, sinks)


def attn_out_fwd(x, o, w, b, tm):
    s = x.shape[0]

    def body(x_ref, o_ref, w_ref, b_ref, y_ref):
        y_ref[...] = x_ref[...] + _dot(o_ref[...], w_ref[...]) + b_ref[...]

    return pl.pallas_call(
        body, grid=(s // tm,), name="attn_out_fwd",
        in_specs=[_row(tm, D), _row(tm, D), _const((D, D)), _const((1, D))],
        out_specs=_row(tm, D), out_shape=SDS((s, D), f32),
        compiler_params=_cp("parallel"),
    )(x, o, w, b)


def attn_out_bwd(dy, wt, tm):
    s = dy.shape[0]

    def body(dy_ref, wt_ref, do_ref, db_ref):
        @pl.when(pl.program_id(0) == 0)
        def _():
            db_ref[...] = jnp.zeros_like(db_ref)

        dy = dy_ref[...]
        do_ref[...] = _dot(dy.astype(bf16), wt_ref[...]).astype(bf16)
        db_ref[...] += _colsum8(dy)

    return pl.pallas_call(
        body, grid=(s // tm,), name="attn_out_bwd",
        in_specs=[_row(tm, D), _const((D, D))],
        out_specs=[_row(tm, D), _acc((SUB, D))],
        out_shape=[SDS((s, D), bf16), SDS((SUB, D), f32)],
        compiler_params=_cp("arbitrary"),
    )(dy, wt)


def attn_bwd(qkv, o, do, lse, sinks):
    s = qkv.shape[0]
    nb = s // BLK
    scale = 1.0 / math.sqrt(HD)
    kvw = 2 * N_KV * HD

    def body(q_ref, kvc_ref, kvp_ref, o_ref, do_ref, lse_ref, sink_ref, dq_ref, dkv_ref, ds_ref, carry_ref):
        i = pl.program_id(0)

        @pl.when(i == 0)
        def _():
            ds_ref[...] = jnp.zeros_like(ds_ref)
            carry_ref[...] = jnp.zeros_like(carry_ref)

        @pl.when(i < nb)
        def _():
            valid = _band_mask(i)
            for kvh in range(N_KV):
                k, v = _kv_heads(kvp_ref, kvc_ref, kvh)
                dk = jnp.zeros((2 * BLK, HD), f32)
                dv = jnp.zeros((2 * BLK, HD), f32)
                for gi in range(GROUP):
                    h = kvh * GROUP + gi
                    hs = slice(h * HD, (h + 1) * HD)
                    qh = q_ref[:, hs]
                    doh = do_ref[:, hs]
                    lse_h = lse_ref[:, h:h + 1]
                    sc = lax.dot_general(qh, k, _NT, preferred_element_type=f32) * scale
                    sc = jnp.where(valid, sc, _NEG)
                    p = jnp.exp(sc - lse_h)
                    dp = lax.dot_general(doh, v, _NT, preferred_element_type=f32)
                    dlt = jnp.sum(doh.astype(f32) * o_ref[:, hs].astype(f32), axis=-1, keepdims=True)
                    dsc = (p * (dp - dlt)).astype(bf16)
                    dq_ref[:, hs] = (_dot(dsc, k) * scale).astype(bf16)
                    dk = dk + lax.dot_general(dsc, qh, _TN, preferred_element_type=f32) * scale
                    dv = dv + lax.dot_general(p.astype(bf16), doh, _TN, preferred_element_type=f32)
                    psink = jnp.exp(sink_ref[h] - lse_h)
                    ds_ref[:, h:h + 1] += -jnp.sum(psink * dlt, axis=0, keepdims=True)
                ks = slice(kvh * HD, (kvh + 1) * HD)
                vs = slice(N_KV * HD + kvh * HD, N_KV * HD + (kvh + 1) * HD)
                dkv_ref[:, ks] = carry_ref[:, ks] + dk[:BLK]
                dkv_ref[:, vs] = carry_ref[:, vs] + dv[:BLK]
                carry_ref[:, ks] = dk[BLK:]
                carry_ref[:, vs] = dv[BLK:]

        @pl.when(i == nb)
        def _():
            dkv_ref[...] = carry_ref[...]

    cur = lambda i: (jnp.minimum(i, nb - 1), 0)
    prev = lambda i: (jnp.clip(i - 1, 0, nb - 1), KV_COL_BLOCK)
    return pl.pallas_call(
        body, grid=(nb + 1,), name="attn_bwd",
        in_specs=[
            pl.BlockSpec((BLK, D), cur),
            pl.BlockSpec((BLK, kvw), lambda i: (jnp.minimum(i, nb - 1), KV_COL_BLOCK)),
            pl.BlockSpec((BLK, kvw), prev),
            pl.BlockSpec((BLK, D), cur),
            pl.BlockSpec((BLK, D), cur),
            pl.BlockSpec((BLK, N_HEADS), cur),
            pl.BlockSpec(memory_space=pltpu.SMEM),
        ],
        out_specs=[
            pl.BlockSpec((BLK, D), cur),
            pl.BlockSpec((BLK, kvw), lambda i: (jnp.maximum(i - 1, 0), 0)),
            _acc((1, N_HEADS)),
        ],
        out_shape=[SDS((s, D), bf16), SDS((s, kvw), f32), SDS((1, N_HEADS), f32)],
        scratch_shapes=[pltpu.VMEM((BLK, kvw), f32)],
        compiler_params=_cp("arbitrary"),
    )(qkv, qkv, qkv, o, do, lse, sinks)


def qkv_bwd(dq, dkv, x, g, wt, dres, tm):
    s = x.shape[0]
    qd = N_HEADS * HD
    kvw = 2 * N_KV * HD

    def body(dq_ref, dkv_ref, x_ref, g_ref, wt_ref, dres_ref, dx_ref, dg_ref, db_ref, dkvb_ref):
        @pl.when(pl.program_id(0) == 0)
        def _():
            dg_ref[...] = jnp.zeros_like(dg_ref)
            db_ref[...] = jnp.zeros_like(db_ref)

        dq = dq_ref[...]
        dkv = dkv_ref[...]
        dkvb = dkv.astype(bf16)
        dkvb_ref[...] = dkvb
        dh = _dot(dq, wt_ref[0:qd, :]) + _dot(dkvb, wt_ref[qd:QKV, :])
        g = g_ref[...]
        _, xn, r = _rms_fwd(x_ref[...], g)
        dx, dg = _rms_bwd(xn, r, g, dh)
        dx_ref[...] = dres_ref[...] + dx
        dg_ref[...] += dg
        db_ref[:, 0:qd] += _colsum8(dq.astype(f32))
        db_ref[:, qd:QKV] += _colsum8(dkv)

    return pl.pallas_call(
        body, grid=(s // tm,), name="qkv_bwd",
        in_specs=[_row(tm, qd), _row(tm, kvw), _row(tm, D), _const((1, D)), _const((QKV, D)), _row(tm, D)],
        out_specs=[_row(tm, D), _acc((SUB, D)), _acc((SUB, QKV)), _row(tm, kvw)],
        out_shape=[SDS((s, D), f32), SDS((SUB, D), f32), SDS((SUB, QKV), f32), SDS((s, kvw), bf16)],
        compiler_params=_cp("arbitrary"),
    )(dq, dkv, x, g, wt, dres)


def ffn_fwd(x, g, wup, wdw, bdw, wdn, tm, cw):
    s = x.shape[0]
    tail = 8

    def body(x_ref, g_ref, wup_ref, wdw_ref, bdw_ref, wdn_ref, xo_ref, h_ref, up_ref, act_ref, carry_ref, ext_ref):
        @pl.when(pl.program_id(0) == 0)
        def _():
            carry_ref[...] = jnp.zeros_like(carry_ref)

        x = x_ref[...]
        h, _, _ = _rms_fwd(x, g_ref[...])
        hb = h.astype(bf16)
        h_ref[...] = hb
        for c in range(DFF // cw):
            cs = slice(c * cw, (c + 1) * cw)
            vs = slice(DFF + c * cw, DFF + (c + 1) * cw)
            ug = _dot(hb, wup_ref[:, cs])
            uv = _dot(hb, wup_ref[:, vs])
            up_ref[:, cs] = ug.astype(bf16)
            up_ref[:, vs] = uv.astype(bf16)
            ext_ref[0:tail, :] = carry_ref[:, cs]
            ext_ref[tail:tail + tm, :] = ug
            carry_ref[:, cs] = ug[tm - tail:, :]
            gate = (wdw_ref[0:1, cs] * ext_ref[pl.ds(tail - 2, tm), :]
                    + wdw_ref[1:2, cs] * ext_ref[pl.ds(tail - 1, tm), :]
                    + wdw_ref[2:3, cs] * ug) + bdw_ref[:, cs]
            act_ref[:, cs] = (gate * _sigmoid(gate) * uv).astype(bf16)
        xo_ref[...] = x + _dot(act_ref[...], wdn_ref[...])

    return pl.pallas_call(
        body, grid=(s // tm,), name="ffn_fwd",
        in_specs=[_row(tm, D), _const((1, D)), _const((D, 2 * DFF)), _const((3, DFF)), _const((1, DFF)),
                  _const((DFF, D))],
        out_specs=[_row(tm, D), _row(tm, D), _row(tm, 2 * DFF), _row(tm, DFF)],
        out_shape=[SDS((s, D), f32), SDS((s, D), bf16), SDS((s, 2 * DFF), bf16), SDS((s, DFF), bf16)],
        scratch_shapes=[pltpu.VMEM((tail, DFF), f32), pltpu.VMEM((tail + tm, cw), f32)],
        compiler_params=_cp("arbitrary"),
    )(x, g, wup, wdw, bdw, wdn)


def ffn_bwd(dxo, x, g, up, wdw, bdw, wdnt, wupt, tm, cw):
    s = x.shape[0]
    nt = s // tm
    hb = FFN_HALO
    rev = lambda i: (nt - 1 - i, 0)

    def body(dxo_ref, x_ref, g_ref, up_ref, halo_ref, wdw_ref, bdw_ref, wdnt_ref, wupt_ref,
             dxi_ref, dup_ref, dg_ref, dwdw_ref, dbdw_ref, carry_ref, ext_ref, ext2_ref):
        i = pl.program_id(0)

        @pl.when(i == 0)
        def _():
            carry_ref[...] = jnp.zeros_like(carry_ref)
            dg_ref[...] = jnp.zeros_like(dg_ref)
            dwdw_ref[...] = jnp.zeros_like(dwdw_ref)
            dbdw_ref[...] = jnp.zeros_like(dbdw_ref)

        has_prev = (i < nt - 1).astype(f32)
        dxo = dxo_ref[...]
        dxb = dxo.astype(bf16)
        for c in range(DFF // cw):
            cs = slice(c * cw, (c + 1) * cw)
            vs = slice(DFF + c * cw, DFF + (c + 1) * cw)
            d_act = _dot(dxb, wdnt_ref[:, cs])
            ug = up_ref[:, cs].astype(f32)
            uv = up_ref[:, vs].astype(f32)
            ext_ref[0:hb, :] = halo_ref[:, cs].astype(f32) * has_prev
            ext_ref[hb:hb + tm, :] = ug
            e0 = ext_ref[pl.ds(hb - 2, tm), :]
            e1 = ext_ref[pl.ds(hb - 1, tm), :]
            w0, w1, w2 = wdw_ref[0:1, cs], wdw_ref[1:2, cs], wdw_ref[2:3, cs]
            gate = (w0 * e0 + w1 * e1 + w2 * ug) + bdw_ref[:, cs]
            sg = _sigmoid(gate)
            dup_ref[:, vs] = (d_act * (gate * sg)).astype(bf16)
            d_gate = d_act * uv * _dsilu(gate, sg)
            dbdw_ref[:, cs] += _colsum8(d_gate)
            dwdw_ref[0, :, cs] += _colsum8(d_gate * e0)
            dwdw_ref[1, :, cs] += _colsum8(d_gate * e1)
            dwdw_ref[2, :, cs] += _colsum8(d_gate * ug)
            ext2_ref[0:tm, :] = d_gate
            ext2_ref[tm:tm + 8, :] = carry_ref[:, cs]
            carry_ref[:, cs] = d_gate[0:8, :]
            d_ug = w0 * ext2_ref[pl.ds(2, tm), :] + w1 * ext2_ref[pl.ds(1, tm), :] + w2 * d_gate
            dup_ref[:, cs] = d_ug.astype(bf16)
        dh = _dot(dup_ref[...], wupt_ref[...])
        gv = g_ref[...]
        _, xn, r = _rms_fwd(x_ref[...], gv)
        dx, dg = _rms_bwd(xn, r, gv, dh)
        dxi_ref[...] = dxo + dx
        dg_ref[...] += dg

    return pl.pallas_call(
        body, grid=(nt,), name="ffn_bwd",
        in_specs=[
            pl.BlockSpec((tm, D), rev), pl.BlockSpec((tm, D), rev), _const((1, D)),
            pl.BlockSpec((tm, 2 * DFF), rev),
            pl.BlockSpec((hb, 2 * DFF), lambda i: (jnp.maximum((nt - 1 - i) * (tm // hb) - 1, 0), 0)),
            _const((3, DFF)), _const((1, DFF)), _const((D, DFF)), _const((2 * DFF, D)),
        ],
        out_specs=[pl.BlockSpec((tm, D), rev), pl.BlockSpec((tm, 2 * DFF), rev), _acc((SUB, D)),
                   _acc((3, SUB, DFF)), _acc((SUB, DFF))],
        out_shape=[SDS((s, D), f32), SDS((s, 2 * DFF), bf16), SDS((SUB, D), f32), SDS((3, SUB, DFF), f32),
                   SDS((SUB, DFF), f32)],
        scratch_shapes=[pltpu.VMEM((8, DFF), f32), pltpu.VMEM((hb + tm, cw), f32), pltpu.VMEM((tm + 8, cw), f32)],
        compiler_params=_cp("arbitrary"),
    )(dxo, x, g, up, up, wdw, bdw, wdnt, wupt)


def pw1_fwd(x, g, w, b, tm):
    s = x.shape[0]

    def body(x_ref, g_ref, w_ref, b_ref, h_ref, a_ref, u_ref):
        h, _, _ = _rms_fwd(x_ref[...], g_ref[...])
        hb = h.astype(bf16)
        h_ref[...] = hb
        a = _dot(hb, w_ref[...]) + b_ref[...]
        a_ref[...] = a.astype(bf16)
        u_ref[...] = a[:, :D] * _sigmoid(a[:, D:])

    return pl.pallas_call(
        body, grid=(s // tm,), name="pw1_fwd",
        in_specs=[_row(tm, D), _const((1, D)), _const((D, 2 * D)), _const((1, 2 * D))],
        out_specs=[_row(tm, D), _row(tm, 2 * D), _row(tm, D)],
        out_shape=[SDS((s, D), bf16), SDS((s, 2 * D), bf16), SDS((s, D), f32)],
        compiler_params=_cp("parallel"),
    )(x, g, w, b)


def _ln_silu(c, lg, lb):
    mu = jnp.mean(c, axis=-1, keepdims=True)
    cc = c - mu
    var = jnp.mean(cc * cc, axis=-1, keepdims=True)
    rstd = lax.rsqrt(var + LN_EPS)
    xh = cc * rstd
    ln = xh * lg + lb
    sg = _sigmoid(ln)
    return xh, rstd, ln, sg


def _shifted_copies(ext_ref, sh_ref, cs, tm):
    n = CONV_HALO - SUB + tm
    for k in range(1, SUB):
        sh_ref[k - 1] = ext_ref[pl.ds(k, n), cs]


def _shifted_rows(ext_ref, sh_ref, cs, start, rows):
    q, k = divmod(start, SUB)
    if k == 0:
        return ext_ref[pl.ds(start, rows), cs]
    return sh_ref[k - 1, pl.ds(q * SUB, rows), :]


def conv_fwd(u, x, wdw, bdw, lg, lb, w2, b2, tm, rc):
    s = x.shape[0]
    hl = CONV_HALO
    off = hl - (CONV_W - 1)

    def body(u_ref, halo_ref, x_ref, wdw_ref, bdw_ref, lg_ref, lb_ref, w2_ref, b2_ref, c_ref, xo_ref, ext_ref, sh_ref):
        has_prev = (pl.program_id(0) > 0).astype(f32)
        ext_ref[0:hl, :] = halo_ref[...] * has_prev
        ext_ref[hl:hl + tm, :] = u_ref[...]
        for cc in range(D // LANES):
            cs = slice(cc * LANES, (cc + 1) * LANES)
            _shifted_copies(ext_ref, sh_ref, cs, tm)
            for rr in range(tm // rc):
                acc = jnp.zeros((rc, LANES), f32) + bdw_ref[:, cs]
                for j in range(CONV_W):
                    acc = acc + wdw_ref[j:j + 1, cs] * _shifted_rows(ext_ref, sh_ref, cs, rr * rc + off + j, rc)
                c_ref[rr * rc:(rr + 1) * rc, cs] = acc
        _, _, ln, sg = _ln_silu(c_ref[...], lg_ref[...], lb_ref[...])
        xo_ref[...] = x_ref[...] + _dot((ln * sg).astype(bf16), w2_ref[...]) + b2_ref[...]

    return pl.pallas_call(
        body, grid=(s // tm,), name="conv_fwd",
        in_specs=[_row(tm, D), pl.BlockSpec((hl, D), lambda i: (jnp.maximum(i * (tm // hl) - 1, 0), 0)), _row(tm, D),
                  _const((CONV_W, D)), _const((1, D)), _const((1, D)), _const((1, D)), _const((D, D)), _const((1, D))],
        out_specs=[_row(tm, D), _row(tm, D)],
        out_shape=[SDS((s, D), f32), SDS((s, D), f32)],
        scratch_shapes=[pltpu.VMEM((hl + tm, D), f32), pltpu.VMEM((SUB - 1, hl - SUB + tm, LANES), f32)],
        compiler_params=_cp("parallel"),
    )(u, u, x, wdw, bdw, lg, lb, w2, b2)


def conv_bwd_a(dy, c, lg, lb, w2t, tm):
    s = dy.shape[0]

    def body(dy_ref, c_ref, lg_ref, lb_ref, w2t_ref, dc_ref, u3_ref, dlg_ref, dlb_ref, db2_ref, dbdw_ref):
        @pl.when(pl.program_id(0) == 0)
        def _():
            for r in (dlg_ref, dlb_ref, db2_ref, dbdw_ref):
                r[...] = jnp.zeros_like(r)

        dy = dy_ref[...]
        lg = lg_ref[...]
        xh, rstd, ln, sg = _ln_silu(c_ref[...], lg, lb_ref[...])
        u3_ref[...] = (ln * sg).astype(bf16)
        du3 = _dot(dy.astype(bf16), w2t_ref[...])
        dln = du3 * _dsilu(ln, sg)
        dxh = dln * lg
        dc = rstd * (dxh - jnp.mean(dxh, axis=-1, keepdims=True) - xh * jnp.mean(dxh * xh, axis=-1, keepdims=True))
        dc_ref[...] = dc
        dlg_ref[...] += _colsum8(dln * xh)
        dlb_ref[...] += _colsum8(dln)
        db2_ref[...] += _colsum8(dy)
        dbdw_ref[...] += _colsum8(dc)

    return pl.pallas_call(
        body, grid=(s // tm,), name="conv_bwd_a",
        in_specs=[_row(tm, D), _row(tm, D), _const((1, D)), _const((1, D)), _const((D, D))],
        out_specs=[_row(tm, D), _row(tm, D)] + [_acc((SUB, D))] * 4,
        out_shape=[SDS((s, D), f32), SDS((s, D), bf16)] + [SDS((SUB, D), f32)] * 4,
        compiler_params=_cp("arbitrary"),
    )(dy, c, lg, lb, w2t)


def conv_bwd_b(dc, u, a, wdw, x, g, w1t, dres, tm, rc):
    s = x.shape[0]
    nt = s // tm
    hl = CONV_HALO
    off = hl - (CONV_W - 1)

    def body(dc_ref, dnext_ref, u_ref, uprev_ref, a_ref, wdw_ref, x_ref, g_ref, w1t_ref, dres_ref,
             dx_ref, da_ref, dwdw_ref, db1_ref, dg_ref, ext_ref, ext2_ref, du_ref, sh_ref, sh2_ref):
        i = pl.program_id(0)

        @pl.when(i == 0)
        def _():
            for r in (dwdw_ref, db1_ref, dg_ref):
                r[...] = jnp.zeros_like(r)

        ext_ref[0:hl, :] = uprev_ref[...] * (i > 0).astype(f32)
        ext_ref[hl:hl + tm, :] = u_ref[...]
        ext2_ref[0:tm, :] = dc_ref[...]
        ext2_ref[tm:tm + hl, :] = dnext_ref[...] * (i < nt - 1).astype(f32)
        for cc in range(D // LANES):
            cs = slice(cc * LANES, (cc + 1) * LANES)
            _shifted_copies(ext_ref, sh_ref, cs, tm)
            _shifted_copies(ext2_ref, sh2_ref, cs, tm)
            for rr in range(tm // rc):
                r0 = rr * rc
                dcb = ext2_ref[r0:r0 + rc, cs]
                acc = jnp.zeros((rc, LANES), f32)
                for j in range(CONV_W):
                    acc = acc + wdw_ref[j:j + 1, cs] * _shifted_rows(ext2_ref, sh2_ref, cs, r0 + CONV_W - 1 - j, rc)
                    dwdw_ref[j, :, cs] += _colsum8(dcb * _shifted_rows(ext_ref, sh_ref, cs, r0 + off + j, rc))
                du_ref[r0:r0 + rc, cs] = acc
        du = du_ref[...]
        a1 = a_ref[:, :D].astype(f32)
        sg = _sigmoid(a_ref[:, D:].astype(f32))
        da1 = du * sg
        da2 = du * a1 * sg * (1.0 - sg)
        da_ref[:, :D] = da1.astype(bf16)
        da_ref[:, D:] = da2.astype(bf16)
        db1_ref[:, :D] += _colsum8(da1)
        db1_ref[:, D:] += _colsum8(da2)
        dh = _dot(da_ref[...], w1t_ref[...])
        gv = g_ref[...]
        _, xn, r = _rms_fwd(x_ref[...], gv)
        dx, dg = _rms_bwd(xn, r, gv, dh)
        dx_ref[...] = dres_ref[...] + dx
        dg_ref[...] += dg

    blocks = tm // hl
    return pl.pallas_call(
        body, grid=(nt,), name="conv_bwd_b",
        in_specs=[
            _row(tm, D), pl.BlockSpec((hl, D), lambda i: (jnp.minimum((i + 1) * blocks, s // hl - 1), 0)),
            _row(tm, D), pl.BlockSpec((hl, D), lambda i: (jnp.maximum(i * blocks - 1, 0), 0)),
            _row(tm, 2 * D), _const((CONV_W, D)), _row(tm, D), _const((1, D)), _const((2 * D, D)), _row(tm, D),
        ],
        out_specs=[_row(tm, D), _row(tm, 2 * D), _acc((CONV_W, SUB, D)), _acc((SUB, 2 * D)), _acc((SUB, D))],
        out_shape=[SDS((s, D), f32), SDS((s, 2 * D), bf16), SDS((CONV_W, SUB, D), f32), SDS((SUB, 2 * D), f32),
                   SDS((SUB, D), f32)],
        scratch_shapes=[pltpu.VMEM((hl + tm, D), f32), pltpu.VMEM((tm + hl, D), f32), pltpu.VMEM((tm, D), f32),
                        pltpu.VMEM((SUB - 1, hl - SUB + tm, LANES), f32),
                        pltpu.VMEM((SUB - 1, hl - SUB + tm, LANES), f32)],
        compiler_params=_cp("arbitrary"),
    )(dc, dc, u, u, a, wdw, x, g, w1t, dres)


def loss_bwd(x, g, t, tm):
    s = x.shape[0]

    def body(x_ref, g_ref, t_ref, dx_ref, dg_ref, loss_ref):
        @pl.when(pl.program_id(0) == 0)
        def _():
            dg_ref[...] = jnp.zeros_like(dg_ref)
            loss_ref[...] = jnp.zeros_like(loss_ref)

        gv = g_ref[...]
        y, xn, r = _rms_fwd(x_ref[...], gv)
        e = y - t_ref[...]
        loss_ref[...] += 0.5 * jnp.sum(jnp.mean(e * e, axis=-1, keepdims=True), axis=0, keepdims=True)
        dx, dg = _rms_bwd(xn, r, gv, e / D)
        dx_ref[...] = dx
        dg_ref[...] += dg

    return pl.pallas_call(
        body, grid=(s // tm,), name="loss_bwd",
        in_specs=[_row(tm, D), _const((1, D)), _row(tm, D)],
        out_specs=[_row(tm, D), _acc((SUB, D)), _acc((1, 1))],
        out_shape=[SDS((s, D), f32), SDS((SUB, D), f32), SDS((1, 1), f32)],
        compiler_params=_cp("arbitrary"),
    )(x, g, t)


def wgrad(a, b, nb, tk, name):
    s, k1 = a.shape
    n = b.shape[1]

    def body(a_ref, b_ref, o_ref):
        @pl.when(pl.program_id(1) == 0)
        def _():
            o_ref[...] = jnp.zeros_like(o_ref)

        o_ref[...] += lax.dot_general(a_ref[...], b_ref[...].astype(bf16), _TN, preferred_element_type=f32)

    return pl.pallas_call(
        body, grid=(n // nb, s // tk), name=name,
        in_specs=[pl.BlockSpec((tk, k1), lambda j, k: (k, 0)), pl.BlockSpec((tk, nb), lambda j, k: (k, j))],
        out_specs=pl.BlockSpec((k1, nb), lambda j, k: (0, j)),
        out_shape=SDS((k1, n), f32),
        compiler_params=_cp("parallel", "arbitrary"),
    )(a, b)


def wgrad_cols(a, b, tk, name):
    s, k1 = a.shape
    w = b.shape[1] // N_CHIPS

    def body(a_ref, b_ref, o_ref):
        @pl.when(pl.program_id(1) == 0)
        def _():
            o_ref[...] = jnp.zeros_like(o_ref)

        acc = lax.dot_general(a_ref[...], b_ref[...].astype(bf16), _TN, preferred_element_type=f32)
        o_ref[:, 0] += acc.reshape(2, k1 // 2, w)

    return pl.pallas_call(
        body, grid=(N_CHIPS, s // tk), name=name,
        in_specs=[pl.BlockSpec((tk, k1), lambda j, k: (k, 0)), pl.BlockSpec((tk, w), lambda j, k: (k, j))],
        out_specs=pl.BlockSpec((2, 1, k1 // 2, w), lambda j, k: (0, j, 0, 0)),
        out_shape=SDS((2, N_CHIPS, k1 // 2, w), f32),
        compiler_params=_cp("parallel", "arbitrary"),
    )(a, b)


def wgrad_rows(a, b, nb, tk, name):
    s, k1 = a.shape
    n = b.shape[1]
    r = k1 // (2 * N_CHIPS)

    def body(a_ref, b_ref, o_ref):
        @pl.when(pl.program_id(1) == 0)
        def _():
            o_ref[...] = jnp.zeros_like(o_ref)

        acc = lax.dot_general(a_ref[...], b_ref[...].astype(bf16), _TN, preferred_element_type=f32)
        for j in range(N_CHIPS):
            for h in range(2):
                o_ref[h, j] += acc[(2 * j + h) * r:(2 * j + h + 1) * r, :]

    return pl.pallas_call(
        body, grid=(n // nb, s // tk), name=name,
        in_specs=[pl.BlockSpec((tk, k1), lambda j, k: (k, 0)), pl.BlockSpec((tk, nb), lambda j, k: (k, j))],
        out_specs=pl.BlockSpec((2, N_CHIPS, r, nb), lambda j, k: (0, 0, 0, j)),
        out_shape=SDS((2, N_CHIPS, r, n), f32),
        compiler_params=_cp("parallel", "arbitrary"),
    )(a, b)


def _adam_math(w, g, m, v):
    m = B1 * m + (1.0 - B1) * g
    v = B2 * v + (1.0 - B2) * (g * g)
    m_hat = m / (1.0 - B1 ** STEP)
    v_hat = v / (1.0 - B2 ** STEP)
    delta = -LR * (m_hat / (jnp.sqrt(v_hat) + ADAM_EPS) + WD * w)
    return delta, m, v


def _rows_tile(r, c, multiple=SUB):
    best = None
    for t in range(multiple, r + 1, multiple):
        if r % t == 0 and t * c * 4 <= ELEMENTWISE_BLOCK_BYTES:
            best = t
    return best if best is not None else r


def adamw(w, g, m, v, name):
    l, r, c = w.shape
    tr = _rows_tile(r, c)
    spec = pl.BlockSpec((1, tr, c), lambda i, j: (i, j, 0))

    def body(w_ref, g_ref, m_ref, v_ref, d_ref, mo_ref, vo_ref):
        d, mn, vn = _adam_math(w_ref[...], g_ref[...], m_ref[...], v_ref[...])
        d_ref[...] = d
        mo_ref[...] = mn
        vo_ref[...] = vn

    return pl.pallas_call(
        body, grid=(l, r // tr), name=name, in_specs=[spec] * 4, out_specs=[spec] * 3,
        out_shape=[SDS((l, r, c), f32)] * 3, compiler_params=_cp("parallel", "parallel"),
    )(w, g, m, v)


def local_grads(x, t, p, tm=512, tmf=256, cw=256, rc=128, tk=1024):
    s = x.shape[0]
    tm = min(tm, s)
    tk = min(tk, s)
    tmf = min(tmf, s)
    row = lambda v: v.reshape(1, -1)

    h0, qkv = qkv_fwd(x, row(p["norm_mix"][0]), p["w_qkv"], p["b_qkv"], tm)
    o, lse = attn_fwd(qkv, p["sinks"])
    x1 = attn_out_fwd(x, o, p["w_o"], p["b_o"], tm)
    x2, h1, up0, act0 = ffn_fwd(x1, row(p["norm_ffn"][0]), p["w_up"][0], p["ffn_w_dw"][0], row(p["ffn_b_dw"][0]),
                                p["w_down"][0], tmf, cw)
    h2, a, u = pw1_fwd(x2, row(p["norm_mix"][1]), p["w_pw1"], p["b_pw1"], tm)
    c, x3 = conv_fwd(u, x2, p["conv_w_dw"], p["conv_b_dw"], p["ln_g"], p["ln_b"], p["w_pw2"], p["b_pw2"], tm, rc)
    x4, h3, up1, act1 = ffn_fwd(x3, row(p["norm_ffn"][1]), p["w_up"][1], p["ffn_w_dw"][1], row(p["ffn_b_dw"][1]),
                                p["w_down"][1], tmf, cw)

    big, small = {}, {}
    dx4, small["final_norm"], loss = loss_bwd(x4, p["final_norm"], t, tm)
    dx3, dup1, small["norm_ffn1"], small["ffn_w_dw1"], small["ffn_b_dw1"] = ffn_bwd(
        dx4, x3, row(p["norm_ffn"][1]), up1, p["ffn_w_dw"][1], row(p["ffn_b_dw"][1]), p["w_down_t"][1], p["w_up_t"][1],
        tmf, cw)
    big["up1"] = wgrad_cols(h3, dup1, tk, "wgrad_up1")
    big["down1"] = wgrad_rows(act1, dx4, 512, tk, "wgrad_down1")
    dc, u3, small["conv_ln_g"], small["conv_ln_b"], small["conv_b_pw2"], small["conv_b_dw"] = conv_bwd_a(
        dx3, c, p["ln_g"], p["ln_b"], p["w_pw2_t"], tm)
    big["pw2"] = wgrad_rows(u3, dx3, 512, tk, "wgrad_pw2")
    dx2, da, small["conv_w_dw"], small["conv_b_pw1"], small["norm_mix1"] = conv_bwd_b(
        dc, u, a, p["conv_w_dw"], x2, row(p["norm_mix"][1]), p["w_pw1_t"], dx3, tm, rc)
    big["pw1"] = wgrad_cols(h2, da, tk, "wgrad_pw1")
    dx1, dup0, small["norm_ffn0"], small["ffn_w_dw0"], small["ffn_b_dw0"] = ffn_bwd(
        dx2, x1, row(p["norm_ffn"][0]), up0, p["ffn_w_dw"][0], row(p["ffn_b_dw"][0]), p["w_down_t"][0], p["w_up_t"][0],
        tmf, cw)
    big["up0"] = wgrad_cols(h1, dup0, tk, "wgrad_up0")
    big["down0"] = wgrad_rows(act0, dx2, 512, tk, "wgrad_down0")
    do, small["attn_b_o"] = attn_out_bwd(dx1, p["w_o_t"], tm)
    big["wo"] = wgrad_rows(o, dx1, 512, tk, "wgrad_o")
    dq, dkv, small["attn_sinks"] = attn_bwd(qkv, o, do, lse, p["sinks"])
    dx0, small["norm_mix0"], small["attn_b_qkv"], dkvb = qkv_bwd(dq, dkv, x, row(p["norm_mix"][0]), p["w_qkv_t"], dx1, tm)
    gqkv = jnp.concatenate([wgrad(h0, dq, 512, tk, "wgrad_q"), wgrad(h0, dkvb, 2 * N_KV * HD, tk, "wgrad_kv")], axis=1)
    big["qkv"] = jnp.transpose(gqkv.reshape(2, D // 2, N_CHIPS, QKV // N_CHIPS), (0, 2, 1, 3))
    small["loss"] = loss
    return dx0, big, small


_ANY = pl.BlockSpec(memory_space=pl.ANY)
_VMEM = pl.BlockSpec(memory_space=pltpu.VMEM)


def _place():
    return lax.axis_index("x"), lax.axis_index("y"), lax.axis_index("c")


def _chip_peer(xi, yi, r):
    px = 1 - xi if r & 2 else xi
    py = 1 - yi if r & 1 else yi
    return px, py


def _gv_qkv(src, dst, j, h):
    rows = pl.ds(h * (D // 2), D // 2)
    return src.at[rows, :], dst.at[j, rows, :]


def _gv_rows(src, dst, j, h):
    r = D // (2 * N_CHIPS)
    return src.at[pl.ds(h * r, r), :], dst.at[pl.ds(j * 2 * r + h * r, r), :]


def _gv_pw1(src, dst, j, h):
    rows = pl.ds(h * (D // 2), D // 2)
    w = 2 * D // N_CHIPS
    return src.at[rows, :], dst.at[rows, pl.ds(j * w, w)]


def _gv_up(src, dst, j, h):
    rows = pl.ds(h * (D // 2), D // 2)
    w = 2 * DFF // N_CHIPS
    return src.at[:, rows, :], dst.at[:, rows, pl.ds(j * w, w)]


def _gv_down(src, dst, j, h):
    r = DFF // (2 * N_CHIPS)
    return src.at[:, pl.ds(h * r, r), :], dst.at[:, pl.ds(j * 2 * r + h * r, r), :]


def _gv_whole(src, dst, j):
    return src, dst.at[j]


def gather_weights(big, small):
    views = (_gv_qkv, _gv_rows, _gv_pw1, _gv_rows, _gv_up, _gv_down)
    nb, ns = len(big), len(small)
    n = nb + ns
    out_shape = [
        SDS((N_CHIPS, D, QKV // N_CHIPS), bf16), SDS((D, D), bf16), SDS((D, 2 * D), bf16), SDS((D, D), bf16),
        SDS((2, D, 2 * DFF), bf16), SDS((2, DFF, D), bf16),
    ] + [SDS((N_CHIPS,) + a.shape, a.dtype) for a in small]

    def body(*refs):
        ins, outs = refs[:n], refs[n:2 * n]
        ici_send, ici_recv, d2d_send, d2d_recv, loc_sems = refs[2 * n:]
        xi, yi, ci = _place()
        me = 2 * xi + yi
        sib = (xi, yi, 1 - ci)
        local, sends = [], []

        def start_local(src, dst, k):
            cp = pltpu.make_async_copy(src, dst, loc_sems.at[k])
            cp.start()
            local.append(cp)

        for a in range(nb):
            for h in range(2):
                start_local(*views[a](ins[a], outs[a], me, h), 2 * a + h)
        for a in range(ns):
            start_local(*_gv_whole(ins[nb + a], outs[nb + a], me), 2 * nb + a)

        def ici(a, r, slot):
            px, py = _chip_peer(xi, yi, r)
            src, dst = views[a](ins[a], outs[a], slot, ci) if a < nb else _gv_whole(ins[a], outs[a], slot)
            k = 3 * a + r - 1
            return pltpu.make_async_remote_copy(src_ref=src, dst_ref=dst, send_sem=ici_send.at[k],
                                                recv_sem=ici_recv.at[k], device_id=(px, py, ci), device_id_type=MESH)

        def d2d(a, r, half):
            px, py = _chip_peer(xi, yi, r)
            _, dst = views[a](ins[a], outs[a], 2 * px + py, half)
            k = 3 * a + r - 1
            return pltpu.make_async_remote_copy(src_ref=dst, dst_ref=dst, send_sem=d2d_send.at[k],
                                                recv_sem=d2d_recv.at[k], device_id=sib, device_id_type=MESH)

        for a in range(n):
            for r in (1, 2, 3):
                cp = ici(a, r, me)
                cp.start()
                sends.append(cp)
        for a in range(n):
            for r in (1, 2, 3):
                px, py = _chip_peer(xi, yi, r)
                ici(a, r, 2 * px + py).wait_recv()
                if a < nb:
                    cp = d2d(a, r, ci)
                    cp.start()
                    sends.append(cp)
        for a in range(nb):
            for r in (1, 2, 3):
                d2d(a, r, 1 - ci).wait_recv()
        for cp in sends:
            cp.wait_send()
        for cp in local:
            cp.wait()

    return pl.pallas_call(
        body, name="gather_weights", in_specs=[_ANY] * n, out_specs=[_ANY] * n, out_shape=out_shape,
        scratch_shapes=[pltpu.SemaphoreType.DMA((3 * n,)), pltpu.SemaphoreType.DMA((3 * n,)),
                        pltpu.SemaphoreType.DMA((3 * nb,)), pltpu.SemaphoreType.DMA((3 * nb,)),
                        pltpu.SemaphoreType.DMA((2 * nb + ns,))],
    )(*big, *small)


def rs_pair_send(gs):
    n = len(gs)

    def body(*refs):
        ins, outs = refs[:n], refs[n:2 * n]
        send_sems, recv_sems = refs[2 * n:]
        xi, yi, ci = _place()
        cps = []
        for a in range(n):
            cp = pltpu.make_async_remote_copy(
                src_ref=ins[a].at[1 - ci], dst_ref=outs[a], send_sem=send_sems.at[a], recv_sem=recv_sems.at[a],
                device_id=(xi, yi, 1 - ci), device_id_type=MESH)
            cp.start()
            cps.append(cp)
        for cp in cps:
            cp.wait()

    return pl.pallas_call(
        body, name="rs_pair_send", in_specs=[_ANY] * n, out_specs=[_ANY] * n,
        out_shape=[SDS(g.shape[1:], g.dtype) for g in gs],
        scratch_shapes=[pltpu.SemaphoreType.DMA((n,)), pltpu.SemaphoreType.DMA((n,))],
    )(*gs)


def rs_pair_add(g, other, ci, name):
    _, nsh, r, w = g.shape
    tr = _rows_tile(r, w, 16)

    def body(c_ref, g_ref, o_ref, p_ref, pb_ref):
        p = g_ref[0] + o_ref[...]
        p_ref[...] = p
        pb_ref[...] = p.astype(bf16)

    blk = pl.BlockSpec((1, tr, w), lambda j, i, c: (j, i, 0))
    return pl.pallas_call(
        body, name=name,
        grid_spec=pltpu.PrefetchScalarGridSpec(
            num_scalar_prefetch=1, grid=(nsh, r // tr),
            in_specs=[pl.BlockSpec((1, 1, tr, w), lambda j, i, c: (c[0], j, i, 0)), blk], out_specs=[blk, blk]),
        out_shape=[SDS((nsh, r, w), f32), SDS((nsh, r, w), bf16)],
        compiler_params=_cp("parallel", "parallel"),
    )(ci.reshape(1).astype(jnp.int32), g, other)


def rs_chip_send(ps):
    n = len(ps)

    def body(*refs):
        ins, outs = refs[:n], refs[n:2 * n]
        send_sems, recv_sems, loc_sems = refs[2 * n:]
        xi, yi, ci = _place()
        me = 2 * xi + yi
        local, sends = [], []
        for a in range(n):
            cp = pltpu.make_async_copy(ins[a].at[me], outs[a].at[me], loc_sems.at[a])
            cp.start()
            local.append(cp)

        def copy(a, r, src_slot, dst_slot):
            px, py = _chip_peer(xi, yi, r)
            k = 3 * a + r - 1
            return pltpu.make_async_remote_copy(
                src_ref=ins[a].at[src_slot], dst_ref=outs[a].at[dst_slot], send_sem=send_sems.at[k],
                recv_sem=recv_sems.at[k], device_id=(px, py, ci), device_id_type=MESH)

        for a in range(n):
            for r in (1, 2, 3):
                px, py = _chip_peer(xi, yi, r)
                cp = copy(a, r, 2 * px + py, me)
                cp.start()
                sends.append(cp)
        for a in range(n):
            for r in (1, 2, 3):
                px, py = _chip_peer(xi, yi, r)
                copy(a, r, me, 2 * px + py).wait_recv()
        for cp in sends:
            cp.wait_send()
        for cp in local:
            cp.wait()

    return pl.pallas_call(
        body, name="rs_chip_send", in_specs=[_ANY] * n, out_specs=[_ANY] * n,
        out_shape=[SDS(p.shape, p.dtype) for p in ps],
        scratch_shapes=[pltpu.SemaphoreType.DMA((3 * n,)), pltpu.SemaphoreType.DMA((3 * n,)),
                        pltpu.SemaphoreType.DMA((n,))],
    )(*ps)


def rs_chip_add(p32, parts, chip, name):
    nsh, r, w = parts.shape
    tr = _rows_tile(r, w, 16)

    def body(c_ref, p_ref, parts_ref, o_ref):
        me = c_ref[0]
        acc = None
        for j in range(N_CHIPS):
            term = jnp.where(me == j, p_ref[0], parts_ref[j].astype(f32))
            acc = term if acc is None else acc + term
        o_ref[...] = acc

    return pl.pallas_call(
        body, name=name,
        grid_spec=pltpu.PrefetchScalarGridSpec(
            num_scalar_prefetch=1, grid=(r // tr,),
            in_specs=[pl.BlockSpec((1, tr, w), lambda i, c: (c[0], i, 0)),
                      pl.BlockSpec((nsh, tr, w), lambda i, c: (0, i, 0))],
            out_specs=pl.BlockSpec((tr, w), lambda i, c: (i, 0))),
        out_shape=SDS((r, w), f32),
        compiler_params=_cp("parallel"),
    )(chip.reshape(1).astype(jnp.int32), p32, parts)


def pair_exchange(halves):
    n = len(halves)

    def body(*refs):
        ins, outs = refs[:n], refs[n:2 * n]
        send_sems, recv_sems, loc_sems = refs[2 * n:]
        xi, yi, ci = _place()
        cps = []
        for a in range(n):
            loc = pltpu.make_async_copy(ins[a], outs[a].at[ci], loc_sems.at[a])
            loc.start()
            cps.append(loc)

        def copy(a, slot):
            return pltpu.make_async_remote_copy(
                src_ref=ins[a], dst_ref=outs[a].at[slot], send_sem=send_sems.at[a], recv_sem=recv_sems.at[a],
                device_id=(xi, yi, 1 - ci), device_id_type=MESH)

        sends = []
        for a in range(n):
            cp = copy(a, ci)
            cp.start()
            sends.append(cp)
        for a in range(n):
            copy(a, 1 - ci).wait_recv()
        for cp in sends:
            cp.wait_send()
        for cp in cps:
            cp.wait()

    return pl.pallas_call(
        body, name="pair_exchange", in_specs=[_ANY] * n, out_specs=[_ANY] * n,
        out_shape=[SDS((2,) + h.shape, h.dtype) for h in halves],
        scratch_shapes=[pltpu.SemaphoreType.DMA((n,)), pltpu.SemaphoreType.DMA((n,)), pltpu.SemaphoreType.DMA((n,))],
    )(*halves)


STAGE_W = DFF
_ST = {"norm_mix0": 0, "norm_mix1": 1, "attn_b_qkv": 2, "attn_sinks": 3, "attn_b_o": 4, "final_norm": 5, "loss": 6,
       "norm_ffn0": 8, "norm_ffn1": 9, "ffn_b_dw0": 10, "ffn_b_dw1": 11, "conv_b_pw1": 12, "conv_b_dw": 13,
       "conv_ln_g": 14, "conv_ln_b": 15, "conv_b_pw2": 16, "ffn_w_dw0": 17, "ffn_w_dw1": 20, "conv_w_dw": 24}
STAGE_ROWS = 56
SMALL_REP = ("norm_mix", "attn_b_qkv", "attn_sinks", "attn_b_o", "norm_ffn", "ffn_b_dw", "final_norm")
SMALL_SH = ("conv_b_pw1", "conv_w_dw", "conv_b_dw", "conv_ln_g", "conv_ln_b", "conv_b_pw2", "ffn_w_dw")
_SMALL_PARTS = ("norm_mix0", "norm_mix1", "attn_b_qkv", "attn_sinks", "attn_b_o", "norm_ffn0", "norm_ffn1", "ffn_b_dw0",
                "ffn_b_dw1", "final_norm", "conv_b_pw1", "conv_b_dw", "conv_ln_g", "conv_ln_b", "conv_b_pw2", "loss",
                "ffn_w_dw0", "ffn_w_dw1", "conv_w_dw")


def small_reduce_adamw(parts, w, m, v):
    names = SMALL_REP + SMALL_SH
    npart, nw = len(_SMALL_PARTS), len(names)

    def body(*refs):
        part = dict(zip(_SMALL_PARTS, refs[:npart]))
        off = npart
        w_ref = dict(zip(names, refs[off:off + nw]))
        m_ref = dict(zip(names, refs[off + nw:off + 2 * nw]))
        v_ref = dict(zip(names, refs[off + 2 * nw:off + 3 * nw]))
        off += 3 * nw
        loss_ref = refs[off]
        g_out = dict(zip(names, refs[off + 1:off + 1 + nw]))
        d_out = dict(zip(names, refs[off + 1 + nw:off + 1 + 2 * nw]))
        m_out = dict(zip(names, refs[off + 1 + 2 * nw:off + 1 + 3 * nw]))
        v_out = dict(zip(names, refs[off + 1 + 3 * nw:off + 1 + 4 * nw]))
        stage_ref, buf_ref, tot_ref, send_sems, recv_sems = refs[off + 1 + 4 * nw:]

        xi, yi, ci = _place()
        me = 4 * xi + 2 * yi + ci
        chip = 2 * xi + yi

        stage_ref[...] = jnp.zeros_like(stage_ref)
        for name in _SMALL_PARTS:
            ref, r0 = part[name], _ST[name]
            if name in ("attn_sinks", "loss"):
                val = ref[...]
            elif name in ("ffn_w_dw0", "ffn_w_dw1", "conv_w_dw"):
                val = jnp.sum(ref[...], axis=1)
            else:
                val = jnp.sum(ref[...], axis=0, keepdims=True)
            stage_ref[r0:r0 + val.shape[0], 0:val.shape[1]] = val

        buf_ref[me] = stage_ref[...]

        def peer(r):
            px, py = _chip_peer(xi, yi, r >> 1)
            return px, py, (1 - ci if r & 1 else ci)

        def copy(r, slot):
            return pltpu.make_async_remote_copy(
                src_ref=stage_ref, dst_ref=buf_ref.at[slot], send_sem=send_sems.at[r - 1], recv_sem=recv_sems.at[r - 1],
                device_id=peer(r), device_id_type=MESH)

        sends = []
        for r in range(1, N_DEV):
            cp = copy(r, me)
            cp.start()
            sends.append(cp)
        for r in range(1, N_DEV):
            px, py, pc = peer(r)
            copy(r, 4 * px + 2 * py + pc).wait_recv()
        for cp in sends:
            cp.wait_send()
        acc = buf_ref[0]
        for d in range(1, N_DEV):
            acc = acc + buf_ref[d]
        tot_ref[...] = acc

        def rows(name, n, width):
            r0 = _ST[name]
            return tot_ref[r0:r0 + n, 0:width]

        def mine(name, n, width):
            r0 = _ST[name]
            out = tot_ref[r0:r0 + n, 0:width]
            for j in range(1, N_CHIPS):
                out = jnp.where(chip == j, tot_ref[r0:r0 + n, j * width:(j + 1) * width], out)
            return out

        loss_ref[...] = rows("loss", 1, 1)
        grads = {
            "norm_mix": rows("norm_mix0", 2, D), "attn_b_qkv": rows("attn_b_qkv", 1, QKV),
            "attn_sinks": rows("attn_sinks", 1, N_HEADS), "attn_b_o": rows("attn_b_o", 1, D),
            "norm_ffn": rows("norm_ffn0", 2, D), "ffn_b_dw": rows("ffn_b_dw0", 2, DFF),
            "final_norm": rows("final_norm", 1, D),
            "conv_b_pw1": mine("conv_b_pw1", 1, 2 * D // N_CHIPS), "conv_w_dw": mine("conv_w_dw", CONV_W, D // N_CHIPS),
            "conv_b_dw": mine("conv_b_dw", 1, D // N_CHIPS), "conv_ln_g": mine("conv_ln_g", 1, D // N_CHIPS),
            "conv_ln_b": mine("conv_ln_b", 1, D // N_CHIPS), "conv_b_pw2": mine("conv_b_pw2", 1, D // N_CHIPS),
        }
        for name in names:
            if name == "ffn_w_dw":
                continue
            at = 0 if name == "conv_w_dw" else Ellipsis
            g = grads[name]
            d, mn, vn = _adam_math(w_ref[name][at], g, m_ref[name][at], v_ref[name][at])
            g_out[name][at] = g
            d_out[name][at] = d
            m_out[name][at] = mn
            v_out[name][at] = vn
        for layer, key in enumerate(("ffn_w_dw0", "ffn_w_dw1")):
            g = mine(key, 3, DFF // N_CHIPS)
            d, mn, vn = _adam_math(w_ref["ffn_w_dw"][layer], g, m_ref["ffn_w_dw"][layer], v_ref["ffn_w_dw"][layer])
            g_out["ffn_w_dw"][layer] = g
            d_out["ffn_w_dw"][layer] = d
            m_out["ffn_w_dw"][layer] = mn
            v_out["ffn_w_dw"][layer] = vn

    ins = [parts[k] for k in _SMALL_PARTS] + [src[k] for src in (w, m, v) for k in names]
    wshapes = [SDS(w[k].shape, f32) for k in names]
    outs = pl.pallas_call(
        body, name="small_reduce_adamw", in_specs=[_VMEM] * len(ins), out_specs=[_VMEM] * (1 + 4 * nw),
        out_shape=[SDS((1, 1), f32)] + wshapes * 4,
        scratch_shapes=[pltpu.VMEM((STAGE_ROWS, STAGE_W), f32), pltpu.VMEM((N_DEV, STAGE_ROWS, STAGE_W), f32),
                        pltpu.VMEM((STAGE_ROWS, STAGE_W), f32), pltpu.SemaphoreType.DMA((N_DEV - 1,)),
                        pltpu.SemaphoreType.DMA((N_DEV - 1,))],
        compiler_params=pltpu.CompilerParams(vmem_limit_bytes=VMEM_LIMIT),
    )(*ins)
    loss = outs[0]
    g, d, mn, vn = (dict(zip(names, outs[1 + k * nw:1 + (k + 1) * nw])) for k in range(4))
    return loss, g, d, mn, vn


def kernel(x, norm_mix, attn_w_qkv, attn_b_qkv, attn_sinks, attn_w_o, attn_b_o, conv_w_pw1, conv_b_pw1, conv_w_dw, conv_b_dw, conv_ln_g, conv_ln_b, conv_w_pw2, conv_b_pw2, norm_ffn, ffn_w_up, ffn_w_dw, ffn_b_dw, ffn_w_down, final_norm, loss_target, m_norm_mix, m_attn_w_qkv, m_attn_b_qkv, m_attn_sinks, m_attn_w_o, m_attn_b_o, m_conv_w_pw1, m_conv_b_pw1, m_conv_w_dw, m_conv_b_dw, m_conv_ln_g, m_conv_ln_b, m_conv_w_pw2, m_conv_b_pw2, m_norm_ffn, m_ffn_w_up, m_ffn_w_dw, m_ffn_b_dw, m_ffn_w_down, m_final_norm, v_norm_mix, v_attn_w_qkv, v_attn_b_qkv, v_attn_sinks, v_attn_w_o, v_attn_b_o, v_conv_w_pw1, v_conv_b_pw1, v_conv_w_dw, v_conv_b_dw, v_conv_ln_g, v_conv_ln_b, v_conv_w_pw2, v_conv_b_pw2, v_norm_ffn, v_ffn_w_up, v_ffn_w_dw, v_ffn_b_dw, v_ffn_w_down, v_final_norm):
    w = dict(norm_mix=norm_mix, attn_w_qkv=attn_w_qkv, attn_b_qkv=attn_b_qkv, attn_sinks=attn_sinks, attn_w_o=attn_w_o,
             attn_b_o=attn_b_o, conv_w_pw1=conv_w_pw1, conv_b_pw1=conv_b_pw1, conv_w_dw=conv_w_dw, conv_b_dw=conv_b_dw,
             conv_ln_g=conv_ln_g, conv_ln_b=conv_ln_b, conv_w_pw2=conv_w_pw2, conv_b_pw2=conv_b_pw2, norm_ffn=norm_ffn,
             ffn_w_up=ffn_w_up, ffn_w_dw=ffn_w_dw, ffn_b_dw=ffn_b_dw, ffn_w_down=ffn_w_down, final_norm=final_norm)
    mom = dict(norm_mix=m_norm_mix, attn_w_qkv=m_attn_w_qkv, attn_b_qkv=m_attn_b_qkv, attn_sinks=m_attn_sinks,
               attn_w_o=m_attn_w_o, attn_b_o=m_attn_b_o, conv_w_pw1=m_conv_w_pw1, conv_b_pw1=m_conv_b_pw1,
               conv_w_dw=m_conv_w_dw, conv_b_dw=m_conv_b_dw, conv_ln_g=m_conv_ln_g, conv_ln_b=m_conv_ln_b,
               conv_w_pw2=m_conv_w_pw2, conv_b_pw2=m_conv_b_pw2, norm_ffn=m_norm_ffn, ffn_w_up=m_ffn_w_up,
               ffn_w_dw=m_ffn_w_dw, ffn_b_dw=m_ffn_b_dw, ffn_w_down=m_ffn_w_down, final_norm=m_final_norm)
    vel = dict(norm_mix=v_norm_mix, attn_w_qkv=v_attn_w_qkv, attn_b_qkv=v_attn_b_qkv, attn_sinks=v_attn_sinks,
               attn_w_o=v_attn_w_o, attn_b_o=v_attn_b_o, conv_w_pw1=v_conv_w_pw1, conv_b_pw1=v_conv_b_pw1,
               conv_w_dw=v_conv_w_dw, conv_b_dw=v_conv_b_dw, conv_ln_g=v_conv_ln_g, conv_ln_b=v_conv_ln_b,
               conv_w_pw2=v_conv_w_pw2, conv_b_pw2=v_conv_b_pw2, norm_ffn=v_norm_ffn, ffn_w_up=v_ffn_w_up,
               ffn_w_dw=v_ffn_w_dw, ffn_b_dw=v_ffn_b_dw, ffn_w_down=v_ffn_w_down, final_norm=v_final_norm)
    order = ("norm_mix", "attn_w_qkv", "attn_b_qkv", "attn_sinks", "attn_w_o", "attn_b_o", "conv_w_pw1", "conv_b_pw1",
             "conv_w_dw", "conv_b_dw", "conv_ln_g", "conv_ln_b", "conv_w_pw2", "conv_b_pw2", "norm_ffn", "ffn_w_up",
             "ffn_w_dw", "ffn_b_dw", "ffn_w_down", "final_norm")
    xi, yi, ci = _place()
    chip = 2 * xi + yi

    big_in = [attn_w_qkv[0].astype(bf16), attn_w_o[0].astype(bf16), conv_w_pw1[0].astype(bf16),
              conv_w_pw2[0].astype(bf16), ffn_w_up.astype(bf16), ffn_w_down.astype(bf16)]
    small_in = [w[k] for k in SMALL_SH]
    gathered = gather_weights(big_in, small_in)
    qkv4, w_o, w_pw1, w_pw2, w_up, w_down = gathered[:6]
    sm = dict(zip(SMALL_SH, gathered[6:]))
    w_qkv = jnp.transpose(qkv4, (1, 0, 2)).reshape(D, QKV)
    join = lambda a, axis: jnp.concatenate([a[j] for j in range(N_CHIPS)], axis=axis)
    p = {
        "norm_mix": norm_mix, "norm_ffn": norm_ffn, "final_norm": final_norm.reshape(1, D),
        "w_qkv": w_qkv, "w_qkv_t": w_qkv.T, "b_qkv": attn_b_qkv, "sinks": attn_sinks.reshape(N_HEADS),
        "w_o": w_o, "w_o_t": w_o.T, "b_o": attn_b_o,
        "w_pw1": w_pw1, "w_pw1_t": w_pw1.T, "b_pw1": join(sm["conv_b_pw1"], 1),
        "conv_w_dw": join(sm["conv_w_dw"], 2)[0], "conv_b_dw": join(sm["conv_b_dw"], 1),
        "ln_g": join(sm["conv_ln_g"], 1), "ln_b": join(sm["conv_ln_b"], 1),
        "w_pw2": w_pw2, "w_pw2_t": w_pw2.T, "b_pw2": join(sm["conv_b_pw2"], 1),
        "w_up": w_up, "w_up_t": jnp.swapaxes(w_up, 1, 2), "ffn_w_dw": join(sm["ffn_w_dw"], 2), "ffn_b_dw": ffn_b_dw,
        "w_down": w_down, "w_down_t": jnp.swapaxes(w_down, 1, 2),
    }

    dx, big, small = local_grads(x[0], loss_target[0], p)

    keys = ("qkv", "wo", "pw1", "pw2", "up0", "up1", "down0", "down1")
    from_pair = rs_pair_send([big[k] for k in keys])
    p32, pb = zip(*[rs_pair_add(big[k], o, ci, "rs_pair_add_" + k) for k, o in zip(keys, from_pair)])
    parts = rs_chip_send(list(pb))
    halves = [rs_chip_add(a, b, chip, "rs_chip_add_" + k) for k, a, b in zip(keys, p32, parts)]
    red = dict(zip(keys, pair_exchange(halves)))
    gbig = {
        "attn_w_qkv": red["qkv"].reshape(1, D, QKV // N_CHIPS), "attn_w_o": red["wo"].reshape(1, D // N_CHIPS, D),
        "conv_w_pw1": red["pw1"].reshape(1, D, 2 * D // N_CHIPS), "conv_w_pw2": red["pw2"].reshape(1, D // N_CHIPS, D),
        "ffn_w_up": jnp.stack([red["up0"], red["up1"]]).reshape(2, D, 2 * DFF // N_CHIPS),
        "ffn_w_down": jnp.stack([red["down0"], red["down1"]]).reshape(2, DFF // N_CHIPS, D),
    }

    fix = lambda d: {**d, "final_norm": d["final_norm"].reshape(1, D)}
    loss, gs, ds, ms, vs = small_reduce_adamw(small, fix(w), fix(mom), fix(vel))
    unfix = lambda d: {**d, "final_norm": d["final_norm"].reshape(D)}
    gout, delta, new_m, new_v = unfix(gs), unfix(ds), unfix(ms), unfix(vs)

    for name, g in gbig.items():
        gout[name] = g
        delta[name], new_m[name], new_v[name] = adamw(w[name], g, mom[name], vel[name], "adamw_" + name)

    return (loss.reshape(()), dx[None], *[gout[n] for n in order], *[delta[n] for n in order],
            *[new_m[n] for n in order], *[new_v[n] for n in order])
```

```python
import math

import jax
import jax.numpy as jnp
from jax import lax
from jax.experimental import pallas as pl
from jax.experimental.pallas import tpu as pltpu

f32 = jnp.float32
bf16 = jnp.bfloat16
SDS = jax.ShapeDtypeStruct
MESH = pl.DeviceIdType.MESH

D = 1024
N_HEADS = 16
N_KV = 2
GROUP = 8
HD = 64
BLK = 128
QKV = (N_HEADS + 2 * N_KV) * HD
KV_COL_BLOCK = (N_HEADS * HD) // (2 * N_KV * HD)
CONV_W = 31
CONV_HALO = 32
DFF = 2816
FFN_HALO = 16
RMS_EPS = 1e-6
LN_EPS = 1e-5
LR, B1, B2, ADAM_EPS, WD, STEP = 0.001, 0.9, 0.999, 1e-08, 0.01, 10

N_CHIPS = 4
N_DEV = 8
VMEM_LIMIT = 56 * 1024 * 1024
LANES = 128
SUB = 8
ELEMENTWISE_BLOCK_BYTES = 1 << 20


def _cp(*sem):
    return pltpu.CompilerParams(dimension_semantics=sem, vmem_limit_bytes=VMEM_LIMIT)


def _row(tm, n):
    return pl.BlockSpec((tm, n), lambda i: (i, 0))


def _const(shape):
    return pl.BlockSpec(shape, lambda *_: (0,) * len(shape), pipeline_mode=pl.Buffered(1))


def _acc(shape):
    return pl.BlockSpec(shape, lambda *_: (0,) * len(shape))


def _rms_fwd(x, g):
    r = lax.rsqrt(jnp.mean(x * x, axis=-1, keepdims=True) + RMS_EPS)
    xn = x * r
    return xn * g, xn, r


def _colsum8(v):
    return jnp.sum(v.reshape(v.shape[0] // SUB, SUB, v.shape[1]), axis=0)


def _rms_bwd(xn, r, g, dh):
    dyn = dh * g
    dx = r * (dyn - xn * jnp.mean(dyn * xn, axis=-1, keepdims=True))
    return dx, _colsum8(dh * xn)


def _sigmoid(z):
    return 0.5 * jnp.tanh(0.5 * z) + 0.5


def _dsilu(z, sg):
    return sg * (1.0 + z * (1.0 - sg))


def _dot(a, b):
    return jnp.dot(a, b, preferred_element_type=f32)


def qkv_fwd(x, g, w, b, tm):
    s = x.shape[0]

    def body(x_ref, g_ref, w_ref, b_ref, h_ref, o_ref):
        h, _, _ = _rms_fwd(x_ref[...], g_ref[...])
        hb = h.astype(bf16)
        h_ref[...] = hb
        o_ref[...] = (_dot(hb, w_ref[...]) + b_ref[...]).astype(bf16)

    return pl.pallas_call(
        body, grid=(s // tm,), name="qkv_fwd",
        in_specs=[_row(tm, D), _const((1, D)), _const((D, QKV)), _const((1, QKV))],
        out_specs=[_row(tm, D), _row(tm, QKV)],
        out_shape=[SDS((s, D), bf16), SDS((s, QKV), bf16)],
        compiler_params=_cp("parallel"),
    )(x, g, w, b)


def _band_mask(i):
    qi = lax.broadcasted_iota(jnp.int32, (GROUP * BLK, 2 * BLK), 0) & (BLK - 1)
    ki = lax.broadcasted_iota(jnp.int32, (GROUP * BLK, 2 * BLK), 1)
    dist = qi + BLK - ki
    return (dist >= 0) & (dist < BLK) & ((ki >= BLK) | (i > 0))


_NEG = float(jnp.finfo(jnp.float32).min)
_NT = (((1,), (1,)), ((), ()))
_TN = (((0,), (0,)), ((), ()))


def _kv_heads(kvp_ref, kvc_ref, kvh):
    ks = slice(kvh * HD, (kvh + 1) * HD)
    vs = slice(N_KV * HD + kvh * HD, N_KV * HD + (kvh + 1) * HD)
    k = jnp.concatenate([kvp_ref[:, ks], kvc_ref[:, ks]], axis=0)
    v = jnp.concatenate([kvp_ref[:, vs], kvc_ref[:, vs]], axis=0)
    return k, v


def _stack_group(ref, kvh, width=HD):
    return jnp.concatenate([ref[:, (kvh * GROUP + gi) * width:(kvh * GROUP + gi + 1) * width] for gi in range(GROUP)],
                           axis=0)


def _group_sinks(sink_ref, kvh):
    row = lax.broadcasted_iota(jnp.int32, (GROUP * BLK, 1), 0)
    col = jnp.zeros((GROUP * BLK, 1), f32)
    for g in range(GROUP):
        col = jnp.where((row >= g * BLK) & (row < (g + 1) * BLK), sink_ref[kvh * GROUP + g], col)
    return col


def attn_fwd(qkv, sinks):
    s = qkv.shape[0]
    scale = 1.0 / math.sqrt(HD)

    def body(q_ref, kvc_ref, kvp_ref, sink_ref, o_ref, lse_ref):
        valid = _band_mask(pl.program_id(0))
        for kvh in range(N_KV):
            k, v = _kv_heads(kvp_ref, kvc_ref, kvh)
            sc = lax.dot_general(_stack_group(q_ref, kvh), k, _NT, preferred_element_type=f32) * scale
            sc = jnp.where(valid, sc, _NEG)
            sink = _group_sinks(sink_ref, kvh)
            m = jnp.maximum(jnp.max(sc, axis=-1, keepdims=True), sink)
            p = jnp.exp(sc - m)
            denom = jnp.sum(p, axis=-1, keepdims=True) + jnp.exp(sink - m)
            og = _dot((p / denom).astype(bf16), v).astype(bf16)
            lse = m + jnp.log(denom)
            for gi in range(GROUP):
                h = kvh * GROUP + gi
                o_ref[:, h * HD:(h + 1) * HD] = og[gi * BLK:(gi + 1) * BLK]
                lse_ref[:, h:h + 1] = lse[gi * BLK:(gi + 1) * BLK]

    return pl.pallas_call(
        body, grid=(s // BLK,), name="attn_fwd",
        in_specs=[
            pl.BlockSpec((BLK, N_HEADS * HD), lambda i: (i, 0)),
            pl.BlockSpec((BLK, 2 * N_KV * HD), lambda i: (i, KV_COL_BLOCK)),
            pl.BlockSpec((BLK, 2 * N_KV * HD), lambda i: (jnp.maximum(i - 1, 0), KV_COL_BLOCK)),
            pl.BlockSpec(memory_space=pltpu.SMEM),
        ],
        out_specs=[_row(BLK, D), _row(BLK, N_HEADS)],
        out_shape=[SDS((s, D), bf16), SDS((s, N_HEADS), f32)],
        compiler_params=_cp("parallel"),
    )(qkv, qkv, qkv, sinks)


def attn_out_fwd(x, o, w, b, tm):
    s = x.shape[0]

    def body(x_ref, o_ref, w_ref, b_ref, y_ref):
        y_ref[...] = x_ref[...] + _dot(o_ref[...], w_ref[...]) + b_ref[...]

    return pl.pallas_call(
        body, grid=(s // tm,), name="attn_out_fwd",
        in_specs=[_row(tm, D), _row(tm, D), _const((D, D)), _const((1, D))],
        out_specs=_row(tm, D), out_shape=SDS((s, D), f32),
        compiler_params=_cp("parallel"),
    )(x, o, w, b)


def attn_out_bwd(dy, wt, tm):
    s = dy.shape[0]

    def body(dy_ref, wt_ref, do_ref, db_ref):
        @pl.when(pl.program_id(0) == 0)
        def _():
            db_ref[...] = jnp.zeros_like(db_ref)

        dy = dy_ref[...]
        do_ref[...] = _dot(dy.astype(bf16), wt_ref[...]).astype(bf16)
        db_ref[...] += _colsum8(dy)

    return pl.pallas_call(
        body, grid=(s // tm,), name="attn_out_bwd",
        in_specs=[_row(tm, D), _const((D, D))],
        out_specs=[_row(tm, D), _acc((SUB, D))],
        out_shape=[SDS((s, D), bf16), SDS((SUB, D), f32)],
        compiler_params=_cp("arbitrary"),
    )(dy, wt)


def attn_bwd(qkv, o, do, lse, sinks):
    s = qkv.shape[0]
    nb = s // BLK
    scale = 1.0 / math.sqrt(HD)
    kvw = 2 * N_KV * HD

    def body(q_ref, kvc_ref, kvp_ref, o_ref, do_ref, lse_ref, sink_ref, dq_ref, dkv_ref, ds_ref, carry_ref):
        i = pl.program_id(0)

        @pl.when(i == 0)
        def _():
            ds_ref[...] = jnp.zeros_like(ds_ref)
            carry_ref[...] = jnp.zeros_like(carry_ref)

        @pl.when(i < nb)
        def _():
            valid = _band_mask(i)
            for kvh in range(N_KV):
                k, v = _kv_heads(kvp_ref, kvc_ref, kvh)
                qg = _stack_group(q_ref, kvh)
                dog = _stack_group(do_ref, kvh)
                lse = _stack_group(lse_ref, kvh, 1)
                sc = lax.dot_general(qg, k, _NT, preferred_element_type=f32) * scale
                sc = jnp.where(valid, sc, _NEG)
                p = jnp.exp(sc - lse)
                dp = lax.dot_general(dog, v, _NT, preferred_element_type=f32)
                dlt = jnp.sum(dog.astype(f32) * _stack_group(o_ref, kvh).astype(f32), axis=-1, keepdims=True)
                dsc = (p * (dp - dlt)).astype(bf16)
                dqg = (_dot(dsc, k) * scale).astype(bf16)
                dk = lax.dot_general(dsc, qg, _TN, preferred_element_type=f32) * scale
                dv = lax.dot_general(p.astype(bf16), dog, _TN, preferred_element_type=f32)
                dsink = jnp.exp(_group_sinks(sink_ref, kvh) - lse) * dlt
                for gi in range(GROUP):
                    h = kvh * GROUP + gi
                    dq_ref[:, h * HD:(h + 1) * HD] = dqg[gi * BLK:(gi + 1) * BLK]
                    ds_ref[:, h:h + 1] += -jnp.sum(dsink[gi * BLK:(gi + 1) * BLK], axis=0, keepdims=True)
                ks = slice(kvh * HD, (kvh + 1) * HD)
                vs = slice(N_KV * HD + kvh * HD, N_KV * HD + (kvh + 1) * HD)
                dkv_ref[:, ks] = carry_ref[:, ks] + dk[:BLK]
                dkv_ref[:, vs] = carry_ref[:, vs] + dv[:BLK]
                carry_ref[:, ks] = dk[BLK:]
                carry_ref[:, vs] = dv[BLK:]

        @pl.when(i == nb)
        def _():
            dkv_ref[...] = carry_ref[...]

    cur = lambda i: (jnp.minimum(i, nb - 1), 0)
    prev = lambda i: (jnp.clip(i - 1, 0, nb - 1), KV_COL_BLOCK)
    return pl.pallas_call(
        body, grid=(nb + 1,), name="attn_bwd",
        in_specs=[
            pl.BlockSpec((BLK, D), cur),
            pl.BlockSpec((BLK, kvw), lambda i: (jnp.minimum(i, nb - 1), KV_COL_BLOCK)),
            pl.BlockSpec((BLK, kvw), prev),
            pl.BlockSpec((BLK, D), cur),
            pl.BlockSpec((BLK, D), cur),
            pl.BlockSpec((BLK, N_HEADS), cur),
            pl.BlockSpec(memory_space=pltpu.SMEM),
        ],
        out_specs=[
            pl.BlockSpec((BLK, D), cur),
            pl.BlockSpec((BLK, kvw), lambda i: (jnp.maximum(i - 1, 0), 0)),
            _acc((1, N_HEADS)),
        ],
        out_shape=[SDS((s, D), bf16), SDS((s, kvw), f32), SDS((1, N_HEADS), f32)],
        scratch_shapes=[pltpu.VMEM((BLK, kvw), f32)],
        compiler_params=_cp("arbitrary"),
    )(qkv, qkv, qkv, o, do, lse, sinks)


def qkv_bwd(dq, dkv, x, g, wt, dres, tm):
    s = x.shape[0]
    qd = N_HEADS * HD
    kvw = 2 * N_KV * HD

    def body(dq_ref, dkv_ref, x_ref, g_ref, wt_ref, dres_ref, dx_ref, dg_ref, db_ref, dkvb_ref):
        @pl.when(pl.program_id(0) == 0)
        def _():
            dg_ref[...] = jnp.zeros_like(dg_ref)
            db_ref[...] = jnp.zeros_like(db_ref)

        dq = dq_ref[...]
        dkv = dkv_ref[...]
        dkvb = dkv.astype(bf16)
        dkvb_ref[...] = dkvb
        dh = _dot(dq, wt_ref[0:qd, :]) + _dot(dkvb, wt_ref[qd:QKV, :])
        g = g_ref[...]
        _, xn, r = _rms_fwd(x_ref[...], g)
        dx, dg = _rms_bwd(xn, r, g, dh)
        dx_ref[...] = dres_ref[...] + dx
        dg_ref[...] += dg
        db_ref[:, 0:qd] += _colsum8(dq.astype(f32))
        db_ref[:, qd:QKV] += _colsum8(dkv)

    return pl.pallas_call(
        body, grid=(s // tm,), name="qkv_bwd",
        in_specs=[_row(tm, qd), _row(tm, kvw), _row(tm, D), _const((1, D)), _const((QKV, D)), _row(tm, D)],
        out_specs=[_row(tm, D), _acc((SUB, D)), _acc((SUB, QKV)), _row(tm, kvw)],
        out_shape=[SDS((s, D), f32), SDS((SUB, D), f32), SDS((SUB, QKV), f32), SDS((s, kvw), bf16)],
        compiler_params=_cp("arbitrary"),
    )(dq, dkv, x, g, wt, dres)


def ffn_fwd(x, g, wup, wdw, bdw, wdn, tm, cw):
    s = x.shape[0]
    tail = 8

    def body(x_ref, g_ref, wup_ref, wdw_ref, bdw_ref, wdn_ref, xo_ref, h_ref, up_ref, act_ref, carry_ref, ext_ref):
        @pl.when(pl.program_id(0) == 0)
        def _():
            carry_ref[...] = jnp.zeros_like(carry_ref)

        x = x_ref[...]
        h, _, _ = _rms_fwd(x, g_ref[...])
        hb = h.astype(bf16)
        h_ref[...] = hb
        for c in range(DFF // cw):
            cs = slice(c * cw, (c + 1) * cw)
            vs = slice(DFF + c * cw, DFF + (c + 1) * cw)
            ug = _dot(hb, wup_ref[:, cs])
            uv = _dot(hb, wup_ref[:, vs])
            up_ref[:, cs] = ug.astype(bf16)
            up_ref[:, vs] = uv.astype(bf16)
            ext_ref[0:tail, :] = carry_ref[:, cs]
            ext_ref[tail:tail + tm, :] = ug
            carry_ref[:, cs] = ug[tm - tail:, :]
            gate = (wdw_ref[0:1, cs] * ext_ref[pl.ds(tail - 2, tm), :]
                    + wdw_ref[1:2, cs] * ext_ref[pl.ds(tail - 1, tm), :]
                    + wdw_ref[2:3, cs] * ug) + bdw_ref[:, cs]
            act_ref[:, cs] = (gate * _sigmoid(gate) * uv).astype(bf16)
        xo_ref[...] = x + _dot(act_ref[...], wdn_ref[...])

    return pl.pallas_call(
        body, grid=(s // tm,), name="ffn_fwd",
        in_specs=[_row(tm, D), _const((1, D)), _const((D, 2 * DFF)), _const((3, DFF)), _const((1, DFF)),
                  _const((DFF, D))],
        out_specs=[_row(tm, D), _row(tm, D), _row(tm, 2 * DFF), _row(tm, DFF)],
        out_shape=[SDS((s, D), f32), SDS((s, D), bf16), SDS((s, 2 * DFF), bf16), SDS((s, DFF), bf16)],
        scratch_shapes=[pltpu.VMEM((tail, DFF), f32), pltpu.VMEM((tail + tm, cw), f32)],
        compiler_params=_cp("arbitrary"),
    )(x, g, wup, wdw, bdw, wdn)


def ffn_bwd(dxo, x, g, up, wdw, bdw, wdnt, wupt, tm, cw):
    s = x.shape[0]
    nt = s // tm
    hb = FFN_HALO
    rev = lambda i: (nt - 1 - i, 0)

    def body(dxo_ref, x_ref, g_ref, up_ref, halo_ref, wdw_ref, bdw_ref, wdnt_ref, wupt_ref,
             dxi_ref, dup_ref, dg_ref, dwdw_ref, dbdw_ref, carry_ref, ext2_ref):
        i = pl.program_id(0)

        @pl.when(i == 0)
        def _():
            carry_ref[...] = jnp.zeros_like(carry_ref)
            dg_ref[...] = jnp.zeros_like(dg_ref)
            dwdw_ref[...] = jnp.zeros_like(dwdw_ref)
            dbdw_ref[...] = jnp.zeros_like(dbdw_ref)

        has_prev = (i < nt - 1).astype(bf16)
        dxo = dxo_ref[...]
        dxb = dxo.astype(bf16)
        tr = lax.broadcasted_iota(jnp.int32, (2 * tm, tm), 0)
        tc = lax.broadcasted_iota(jnp.int32, (2 * tm, tm), 1)
        back = jnp.where(tr < tm, 1, 2)
        pick = (tc == jnp.where(tr < tm, tr, tr - tm) - back).astype(bf16)
        hr = lax.broadcasted_iota(jnp.int32, (2 * tm, hb), 0)
        hc = lax.broadcasted_iota(jnp.int32, (2 * tm, hb), 1)
        pick_halo = (hc == jnp.where(hr < tm, hr + hb - 1, hr - tm + hb - 2)).astype(bf16)
        for c in range(DFF // cw):
            cs = slice(c * cw, (c + 1) * cw)
            vs = slice(DFF + c * cw, DFF + (c + 1) * cw)
            d_act = _dot(dxb, wdnt_ref[:, cs])
            ugb = up_ref[:, cs]
            ug = ugb.astype(f32)
            uv = up_ref[:, vs].astype(f32)
            e10 = _dot(pick, ugb) + _dot(pick_halo, halo_ref[:, cs] * has_prev)
            e1 = e10[:tm]
            e0 = e10[tm:]
            w0, w1, w2 = wdw_ref[0:1, cs], wdw_ref[1:2, cs], wdw_ref[2:3, cs]
            gate = (w0 * e0 + w1 * e1 + w2 * ug) + bdw_ref[:, cs]
            sg = _sigmoid(gate)
            dup_ref[:, vs] = (d_act * (gate * sg)).astype(bf16)
            d_gate = d_act * uv * _dsilu(gate, sg)
            dbdw_ref[:, cs] += _colsum8(d_gate)
            dwdw_ref[0, :, cs] += _colsum8(d_gate * e0)
            dwdw_ref[1, :, cs] += _colsum8(d_gate * e1)
            dwdw_ref[2, :, cs] += _colsum8(d_gate * ug)
            ext2_ref[0:tm, :] = d_gate
            ext2_ref[tm:tm + 8, :] = carry_ref[:, cs]
            carry_ref[:, cs] = d_gate[0:8, :]
            d_ug = w0 * ext2_ref[pl.ds(2, tm), :] + w1 * ext2_ref[pl.ds(1, tm), :] + w2 * d_gate
            dup_ref[:, cs] = d_ug.astype(bf16)
        dh = _dot(dup_ref[...], wupt_ref[...])
        gv = g_ref[...]
        _, xn, r = _rms_fwd(x_ref[...], gv)
        dx, dg = _rms_bwd(xn, r, gv, dh)
        dxi_ref[...] = dxo + dx
        dg_ref[...] += dg

    return pl.pallas_call(
        body, grid=(nt,), name="ffn_bwd",
        in_specs=[
            pl.BlockSpec((tm, D), rev), pl.BlockSpec((tm, D), rev), _const((1, D)),
            pl.BlockSpec((tm, 2 * DFF), rev),
            pl.BlockSpec((hb, 2 * DFF), lambda i: (jnp.maximum((nt - 1 - i) * (tm // hb) - 1, 0), 0)),
            _const((3, DFF)), _const((1, DFF)), _const((D, DFF)), _const((2 * DFF, D)),
        ],
        out_specs=[pl.BlockSpec((tm, D), rev), pl.BlockSpec((tm, 2 * DFF), rev), _acc((SUB, D)),
                   _acc((3, SUB, DFF)), _acc((SUB, DFF))],
        out_shape=[SDS((s, D), f32), SDS((s, 2 * DFF), bf16), SDS((SUB, D), f32), SDS((3, SUB, DFF), f32),
                   SDS((SUB, DFF), f32)],
        scratch_shapes=[pltpu.VMEM((8, DFF), f32), pltpu.VMEM((tm + 8, cw), f32)],
        compiler_params=_cp("arbitrary"),
    )(dxo, x, g, up, up, wdw, bdw, wdnt, wupt)


def pw1_fwd(x, g, w, b, tm):
    s = x.shape[0]

    def body(x_ref, g_ref, w_ref, b_ref, h_ref, a_ref, u_ref):
        h, _, _ = _rms_fwd(x_ref[...], g_ref[...])
        hb = h.astype(bf16)
        h_ref[...] = hb
        a = _dot(hb, w_ref[...]) + b_ref[...]
        a_ref[...] = a.astype(bf16)
        u_ref[...] = a[:, :D] * _sigmoid(a[:, D:])

    return pl.pallas_call(
        body, grid=(s // tm,), name="pw1_fwd",
        in_specs=[_row(tm, D), _const((1, D)), _const((D, 2 * D)), _const((1, 2 * D))],
        out_specs=[_row(tm, D), _row(tm, 2 * D), _row(tm, D)],
        out_shape=[SDS((s, D), bf16), SDS((s, 2 * D), bf16), SDS((s, D), f32)],
        compiler_params=_cp("parallel"),
    )(x, g, w, b)


def _ln_silu(c, lg, lb):
    mu = jnp.mean(c, axis=-1, keepdims=True)
    cc = c - mu
    var = jnp.mean(cc * cc, axis=-1, keepdims=True)
    rstd = lax.rsqrt(var + LN_EPS)
    xh = cc * rstd
    ln = xh * lg + lb
    sg = _sigmoid(ln)
    return xh, rstd, ln, sg


def _shifted_copies(ext_ref, sh_ref, cs, tm):
    n = CONV_HALO - SUB + tm
    for k in range(1, SUB):
        sh_ref[k - 1] = ext_ref[pl.ds(k, n), cs]


def _shifted_rows(ext_ref, sh_ref, cs, start, rows):
    q, k = divmod(start, SUB)
    if k == 0:
        return ext_ref[pl.ds(start, rows), cs]
    return sh_ref[k - 1, pl.ds(q * SUB, rows), :]


def conv_fwd(u, x, wdw, bdw, lg, lb, w2, b2, tm, rc):
    s = x.shape[0]
    hl = CONV_HALO
    off = hl - (CONV_W - 1)

    def body(u_ref, halo_ref, x_ref, wdw_ref, bdw_ref, lg_ref, lb_ref, w2_ref, b2_ref, c_ref, xo_ref, ext_ref, sh_ref):
        has_prev = (pl.program_id(0) > 0).astype(f32)
        ext_ref[0:hl, :] = halo_ref[...] * has_prev
        ext_ref[hl:hl + tm, :] = u_ref[...]
        for cc in range(D // LANES):
            cs = slice(cc * LANES, (cc + 1) * LANES)
            _shifted_copies(ext_ref, sh_ref, cs, tm)
            for rr in range(tm // rc):
                acc = jnp.zeros((rc, LANES), f32) + bdw_ref[:, cs]
                for j in range(CONV_W):
                    acc = acc + wdw_ref[j:j + 1, cs] * _shifted_rows(ext_ref, sh_ref, cs, rr * rc + off + j, rc)
                c_ref[rr * rc:(rr + 1) * rc, cs] = acc
        _, _, ln, sg = _ln_silu(c_ref[...], lg_ref[...], lb_ref[...])
        xo_ref[...] = x_ref[...] + _dot((ln * sg).astype(bf16), w2_ref[...]) + b2_ref[...]

    return pl.pallas_call(
        body, grid=(s // tm,), name="conv_fwd",
        in_specs=[_row(tm, D), pl.BlockSpec((hl, D), lambda i: (jnp.maximum(i * (tm // hl) - 1, 0), 0)), _row(tm, D),
                  _const((CONV_W, D)), _const((1, D)), _const((1, D)), _const((1, D)), _const((D, D)), _const((1, D))],
        out_specs=[_row(tm, D), _row(tm, D)],
        out_shape=[SDS((s, D), f32), SDS((s, D), f32)],
        scratch_shapes=[pltpu.VMEM((hl + tm, D), f32), pltpu.VMEM((SUB - 1, hl - SUB + tm, LANES), f32)],
        compiler_params=_cp("parallel"),
    )(u, u, x, wdw, bdw, lg, lb, w2, b2)


def conv_bwd_a(dy, c, lg, lb, w2t, tm):
    s = dy.shape[0]

    def body(dy_ref, c_ref, lg_ref, lb_ref, w2t_ref, dc_ref, u3_ref, dlg_ref, dlb_ref, db2_ref, dbdw_ref):
        @pl.when(pl.program_id(0) == 0)
        def _():
            for r in (dlg_ref, dlb_ref, db2_ref, dbdw_ref):
                r[...] = jnp.zeros_like(r)

        dy = dy_ref[...]
        lg = lg_ref[...]
        xh, rstd, ln, sg = _ln_silu(c_ref[...], lg, lb_ref[...])
        u3_ref[...] = (ln * sg).astype(bf16)
        du3 = _dot(dy.astype(bf16), w2t_ref[...])
        dln = du3 * _dsilu(ln, sg)
        dxh = dln * lg
        dc = rstd * (dxh - jnp.mean(dxh, axis=-1, keepdims=True) - xh * jnp.mean(dxh * xh, axis=-1, keepdims=True))
        dc_ref[...] = dc
        dlg_ref[...] += _colsum8(dln * xh)
        dlb_ref[...] += _colsum8(dln)
        db2_ref[...] += _colsum8(dy)
        dbdw_ref[...] += _colsum8(dc)

    return pl.pallas_call(
        body, grid=(s // tm,), name="conv_bwd_a",
        in_specs=[_row(tm, D), _row(tm, D), _const((1, D)), _const((1, D)), _const((D, D))],
        out_specs=[_row(tm, D), _row(tm, D)] + [_acc((SUB, D))] * 4,
        out_shape=[SDS((s, D), f32), SDS((s, D), bf16)] + [SDS((SUB, D), f32)] * 4,
        compiler_params=_cp("arbitrary"),
    )(dy, c, lg, lb, w2t)


def conv_bwd_b(dc, u, a, wdw, x, g, w1t, dres, tm, rc):
    s = x.shape[0]
    nt = s // tm
    hl = CONV_HALO
    off = hl - (CONV_W - 1)

    def body(dc_ref, dnext_ref, u_ref, uprev_ref, a_ref, wdw_ref, x_ref, g_ref, w1t_ref, dres_ref,
             dx_ref, da_ref, dwdw_ref, db1_ref, dg_ref, ext_ref, ext2_ref, du_ref, sh_ref, sh2_ref):
        i = pl.program_id(0)

        @pl.when(i == 0)
        def _():
            for r in (dwdw_ref, db1_ref, dg_ref):
                r[...] = jnp.zeros_like(r)

        ext_ref[0:hl, :] = uprev_ref[...] * (i > 0).astype(f32)
        ext_ref[hl:hl + tm, :] = u_ref[...]
        ext2_ref[0:tm, :] = dc_ref[...]
        ext2_ref[tm:tm + hl, :] = dnext_ref[...] * (i < nt - 1).astype(f32)
        for cc in range(D // LANES):
            cs = slice(cc * LANES, (cc + 1) * LANES)
            _shifted_copies(ext_ref, sh_ref, cs, tm)
            _shifted_copies(ext2_ref, sh2_ref, cs, tm)
            for rr in range(tm // rc):
                r0 = rr * rc
                dcb = ext2_ref[r0:r0 + rc, cs]
                acc = jnp.zeros((rc, LANES), f32)
                for j in range(CONV_W):
                    acc = acc + wdw_ref[j:j + 1, cs] * _shifted_rows(ext2_ref, sh2_ref, cs, r0 + CONV_W - 1 - j, rc)
                    dwdw_ref[j, :, cs] += _colsum8(dcb * _shifted_rows(ext_ref, sh_ref, cs, r0 + off + j, rc))
                du_ref[r0:r0 + rc, cs] = acc
        du = du_ref[...]
        a1 = a_ref[:, :D].astype(f32)
        sg = _sigmoid(a_ref[:, D:].astype(f32))
        da1 = du * sg
        da2 = du * a1 * sg * (1.0 - sg)
        da_ref[:, :D] = da1.astype(bf16)
        da_ref[:, D:] = da2.astype(bf16)
        db1_ref[:, :D] += _colsum8(da1)
        db1_ref[:, D:] += _colsum8(da2)
        dh = _dot(da_ref[...], w1t_ref[...])
        gv = g_ref[...]
        _, xn, r = _rms_fwd(x_ref[...], gv)
        dx, dg = _rms_bwd(xn, r, gv, dh)
        dx_ref[...] = dres_ref[...] + dx
        dg_ref[...] += dg

    blocks = tm // hl
    return pl.pallas_call(
        body, grid=(nt,), name="conv_bwd_b",
        in_specs=[
            _row(tm, D), pl.BlockSpec((hl, D), lambda i: (jnp.minimum((i + 1) * blocks, s // hl - 1), 0)),
            _row(tm, D), pl.BlockSpec((hl, D), lambda i: (jnp.maximum(i * blocks - 1, 0), 0)),
            _row(tm, 2 * D), _const((CONV_W, D)), _row(tm, D), _const((1, D)), _const((2 * D, D)), _row(tm, D),
        ],
        out_specs=[_row(tm, D), _row(tm, 2 * D), _acc((CONV_W, SUB, D)), _acc((SUB, 2 * D)), _acc((SUB, D))],
        out_shape=[SDS((s, D), f32), SDS((s, 2 * D), bf16), SDS((CONV_W, SUB, D), f32), SDS((SUB, 2 * D), f32),
                   SDS((SUB, D), f32)],
        scratch_shapes=[pltpu.VMEM((hl + tm, D), f32), pltpu.VMEM((tm + hl, D), f32), pltpu.VMEM((tm, D), f32),
                        pltpu.VMEM((SUB - 1, hl - SUB + tm, LANES), f32),
                        pltpu.VMEM((SUB - 1, hl - SUB + tm, LANES), f32)],
        compiler_params=_cp("arbitrary"),
    )(dc, dc, u, u, a, wdw, x, g, w1t, dres)


def loss_bwd(x, g, t, tm):
    s = x.shape[0]

    def body(x_ref, g_ref, t_ref, dx_ref, dg_ref, loss_ref):
        @pl.when(pl.program_id(0) == 0)
        def _():
            dg_ref[...] = jnp.zeros_like(dg_ref)
            loss_ref[...] = jnp.zeros_like(loss_ref)

        gv = g_ref[...]
        y, xn, r = _rms_fwd(x_ref[...], gv)
        e = y - t_ref[...]
        loss_ref[...] += 0.5 * jnp.sum(jnp.mean(e * e, axis=-1, keepdims=True), axis=0, keepdims=True)
        dx, dg = _rms_bwd(xn, r, gv, e / D)
        dx_ref[...] = dx
        dg_ref[...] += dg

    return pl.pallas_call(
        body, grid=(s // tm,), name="loss_bwd",
        in_specs=[_row(tm, D), _const((1, D)), _row(tm, D)],
        out_specs=[_row(tm, D), _acc((SUB, D)), _acc((1, 1))],
        out_shape=[SDS((s, D), f32), SDS((SUB, D), f32), SDS((1, 1), f32)],
        compiler_params=_cp("arbitrary"),
    )(x, g, t)


def wgrad(a, b, nb, tk, name):
    s, k1 = a.shape
    n = b.shape[1]

    def body(a_ref, b_ref, o_ref):
        @pl.when(pl.program_id(1) == 0)
        def _():
            o_ref[...] = jnp.zeros_like(o_ref)

        o_ref[...] += lax.dot_general(a_ref[...], b_ref[...].astype(bf16), _TN, preferred_element_type=f32)

    return pl.pallas_call(
        body, grid=(n // nb, s // tk), name=name,
        in_specs=[pl.BlockSpec((tk, k1), lambda j, k: (k, 0)), pl.BlockSpec((tk, nb), lambda j, k: (k, j))],
        out_specs=pl.BlockSpec((k1, nb), lambda j, k: (0, j)),
        out_shape=SDS((k1, n), f32),
        compiler_params=_cp("parallel", "arbitrary"),
    )(a, b)


def wgrad_cols(a, b, tk, name):
    s, k1 = a.shape
    w = b.shape[1] // N_CHIPS

    def body(a_ref, b_ref, o_ref):
        @pl.when(pl.program_id(1) == 0)
        def _():
            o_ref[...] = jnp.zeros_like(o_ref)

        acc = lax.dot_general(a_ref[...], b_ref[...].astype(bf16), _TN, preferred_element_type=f32)
        o_ref[:, 0] += acc.reshape(2, k1 // 2, w)

    return pl.pallas_call(
        body, grid=(N_CHIPS, s // tk), name=name,
        in_specs=[pl.BlockSpec((tk, k1), lambda j, k: (k, 0)), pl.BlockSpec((tk, w), lambda j, k: (k, j))],
        out_specs=pl.BlockSpec((2, 1, k1 // 2, w), lambda j, k: (0, j, 0, 0)),
        out_shape=SDS((2, N_CHIPS, k1 // 2, w), f32),
        compiler_params=_cp("parallel", "arbitrary"),
    )(a, b)


def wgrad_rows(a, b, nb, tk, name):
    s, k1 = a.shape
    n = b.shape[1]
    r = k1 // (2 * N_CHIPS)

    def body(a_ref, b_ref, o_ref):
        @pl.when(pl.program_id(1) == 0)
        def _():
            o_ref[...] = jnp.zeros_like(o_ref)

        acc = lax.dot_general(a_ref[...], b_ref[...].astype(bf16), _TN, preferred_element_type=f32)
        for j in range(N_CHIPS):
            for h in range(2):
                o_ref[h, j] += acc[(2 * j + h) * r:(2 * j + h + 1) * r, :]

    return pl.pallas_call(
        body, grid=(n // nb, s // tk), name=name,
        in_specs=[pl.BlockSpec((tk, k1), lambda j, k: (k, 0)), pl.BlockSpec((tk, nb), lambda j, k: (k, j))],
        out_specs=pl.BlockSpec((2, N_CHIPS, r, nb), lambda j, k: (0, 0, 0, j)),
        out_shape=SDS((2, N_CHIPS, r, n), f32),
        compiler_params=_cp("parallel", "arbitrary"),
    )(a, b)


def _adam_math(w, g, m, v):
    m = B1 * m + (1.0 - B1) * g
    v = B2 * v + (1.0 - B2) * (g * g)
    m_hat = m / (1.0 - B1 ** STEP)
    v_hat = v / (1.0 - B2 ** STEP)
    delta = -LR * (m_hat / (jnp.sqrt(v_hat) + ADAM_EPS) + WD * w)
    return delta, m, v


def _rows_tile(r, c, multiple=SUB):
    best = None
    for t in range(multiple, r + 1, multiple):
        if r % t == 0 and t * c * 4 <= ELEMENTWISE_BLOCK_BYTES:
            best = t
    return best if best is not None else r


def adamw(w, g, m, v, name):
    l, r, c = w.shape
    tr = _rows_tile(r, c)
    spec = pl.BlockSpec((1, tr, c), lambda i, j: (i, j, 0))

    def body(w_ref, g_ref, m_ref, v_ref, d_ref, mo_ref, vo_ref):
        d, mn, vn = _adam_math(w_ref[...], g_ref[...], m_ref[...], v_ref[...])
        d_ref[...] = d
        mo_ref[...] = mn
        vo_ref[...] = vn

    return pl.pallas_call(
        body, grid=(l, r // tr), name=name, in_specs=[spec] * 4, out_specs=[spec] * 3,
        out_shape=[SDS((l, r, c), f32)] * 3, compiler_params=_cp("parallel", "parallel"),
    )(w, g, m, v)


def local_grads(x, t, p, tm=512, tmf=256, cw=256, rc=128, tk=1024):
    s = x.shape[0]
    tm = min(tm, s)
    tk = min(tk, s)
    tmf = min(tmf, s)
    row = lambda v: v.reshape(1, -1)

    h0, qkv = qkv_fwd(x, row(p["norm_mix"][0]), p["w_qkv"], p["b_qkv"], tm)
    o, lse = attn_fwd(qkv, p["sinks"])
    x1 = attn_out_fwd(x, o, p["w_o"], p["b_o"], tm)
    x2, h1, up0, act0 = ffn_fwd(x1, row(p["norm_ffn"][0]), p["w_up"][0], p["ffn_w_dw"][0], row(p["ffn_b_dw"][0]),
                                p["w_down"][0], tmf, cw)
    h2, a, u = pw1_fwd(x2, row(p["norm_mix"][1]), p["w_pw1"], p["b_pw1"], tm)
    c, x3 = conv_fwd(u, x2, p["conv_w_dw"], p["conv_b_dw"], p["ln_g"], p["ln_b"], p["w_pw2"], p["b_pw2"], tm, rc)
    x4, h3, up1, act1 = ffn_fwd(x3, row(p["norm_ffn"][1]), p["w_up"][1], p["ffn_w_dw"][1], row(p["ffn_b_dw"][1]),
                                p["w_down"][1], tmf, cw)

    big, small = {}, {}
    dx4, small["final_norm"], loss = loss_bwd(x4, p["final_norm"], t, tm)
    dx3, dup1, small["norm_ffn1"], small["ffn_w_dw1"], small["ffn_b_dw1"] = ffn_bwd(
        dx4, x3, row(p["norm_ffn"][1]), up1, p["ffn_w_dw"][1], row(p["ffn_b_dw"][1]), p["w_down_t"][1], p["w_up_t"][1],
        tmf, cw)
    big["up1"] = wgrad_cols(h3, dup1, tk, "wgrad_up1")
    big["down1"] = wgrad_rows(act1, dx4, 512, tk, "wgrad_down1")
    dc, u3, small["conv_ln_g"], small["conv_ln_b"], small["conv_b_pw2"], small["conv_b_dw"] = conv_bwd_a(
        dx3, c, p["ln_g"], p["ln_b"], p["w_pw2_t"], tm)
    big["pw2"] = wgrad_rows(u3, dx3, 512, tk, "wgrad_pw2")
    dx2, da, small["conv_w_dw"], small["conv_b_pw1"], small["norm_mix1"] = conv_bwd_b(
        dc, u, a, p["conv_w_dw"], x2, row(p["norm_mix"][1]), p["w_pw1_t"], dx3, tm, rc)
    big["pw1"] = wgrad_cols(h2, da, tk, "wgrad_pw1")
    dx1, dup0, small["norm_ffn0"], small["ffn_w_dw0"], small["ffn_b_dw0"] = ffn_bwd(
        dx2, x1, row(p["norm_ffn"][0]), up0, p["ffn_w_dw"][0], row(p["ffn_b_dw"][0]), p["w_down_t"][0], p["w_up_t"][0],
        tmf, cw)
    big["up0"] = wgrad_cols(h1, dup0, tk, "wgrad_up0")
    big["down0"] = wgrad_rows(act0, dx2, 512, tk, "wgrad_down0")
    do, small["attn_b_o"] = attn_out_bwd(dx1, p["w_o_t"], tm)
    big["wo"] = wgrad_rows(o, dx1, 512, tk, "wgrad_o")
    dq, dkv, small["attn_sinks"] = attn_bwd(qkv, o, do, lse, p["sinks"])
    dx0, small["norm_mix0"], small["attn_b_qkv"], dkvb = qkv_bwd(dq, dkv, x, row(p["norm_mix"][0]), p["w_qkv_t"], dx1, tm)
    gqkv = jnp.concatenate([wgrad(h0, dq, 512, tk, "wgrad_q"), wgrad(h0, dkvb, 2 * N_KV * HD, tk, "wgrad_kv")], axis=1)
    big["qkv"] = jnp.transpose(gqkv.reshape(2, D // 2, N_CHIPS, QKV // N_CHIPS), (0, 2, 1, 3))
    small["loss"] = loss
    return dx0, big, small


_ANY = pl.BlockSpec(memory_space=pl.ANY)
_VMEM = pl.BlockSpec(memory_space=pltpu.VMEM)


def _place():
    return lax.axis_index("x"), lax.axis_index("y"), lax.axis_index("c")


def _chip_peer(xi, yi, r):
    px = 1 - xi if r & 2 else xi
    py = 1 - yi if r & 1 else yi
    return px, py


def _gv_qkv(src, dst, j, h):
    rows = pl.ds(h * (D // 2), D // 2)
    return src.at[rows, :], dst.at[j, rows, :]


def _gv_rows(src, dst, j, h):
    r = D // (2 * N_CHIPS)
    return src.at[pl.ds(h * r, r), :], dst.at[pl.ds(j * 2 * r + h * r, r), :]


def _gv_pw1(src, dst, j, h):
    rows = pl.ds(h * (D // 2), D // 2)
    w = 2 * D // N_CHIPS
    return src.at[rows, :], dst.at[rows, pl.ds(j * w, w)]


def _gv_up(src, dst, j, h):
    rows = pl.ds(h * (D // 2), D // 2)
    w = 2 * DFF // N_CHIPS
    return src.at[:, rows, :], dst.at[:, rows, pl.ds(j * w, w)]


def _gv_down(src, dst, j, h):
    r = DFF // (2 * N_CHIPS)
    return src.at[:, pl.ds(h * r, r), :], dst.at[:, pl.ds(j * 2 * r + h * r, r), :]


def _gv_whole(src, dst, j):
    return src, dst.at[j]


def gather_weights(big, small):
    views = (_gv_qkv, _gv_rows, _gv_pw1, _gv_rows, _gv_up, _gv_down)
    nb, ns = len(big), len(small)
    n = nb + ns
    out_shape = [
        SDS((N_CHIPS, D, QKV // N_CHIPS), bf16), SDS((D, D), bf16), SDS((D, 2 * D), bf16), SDS((D, D), bf16),
        SDS((2, D, 2 * DFF), bf16), SDS((2, DFF, D), bf16),
    ] + [SDS((N_CHIPS,) + a.shape, a.dtype) for a in small]

    def body(*refs):
        ins, outs = refs[:n], refs[n:2 * n]
        ici_send, ici_recv, d2d_send, d2d_recv, loc_sems = refs[2 * n:]
        xi, yi, ci = _place()
        me = 2 * xi + yi
        sib = (xi, yi, 1 - ci)
        local, sends = [], []

        def start_local(src, dst, k):
            cp = pltpu.make_async_copy(src, dst, loc_sems.at[k])
            cp.start()
            local.append(cp)

        for a in range(nb):
            for h in range(2):
                start_local(*views[a](ins[a], outs[a], me, h), 2 * a + h)
        for a in range(ns):
            start_local(*_gv_whole(ins[nb + a], outs[nb + a], me), 2 * nb + a)

        def ici(a, r, slot):
            px, py = _chip_peer(xi, yi, r)
            src, dst = views[a](ins[a], outs[a], slot, ci) if a < nb else _gv_whole(ins[a], outs[a], slot)
            k = 3 * a + r - 1
            return pltpu.make_async_remote_copy(src_ref=src, dst_ref=dst, send_sem=ici_send.at[k],
                                                recv_sem=ici_recv.at[k], device_id=(px, py, ci), device_id_type=MESH)

        def d2d(a, r, half):
            px, py = _chip_peer(xi, yi, r)
            _, dst = views[a](ins[a], outs[a], 2 * px + py, half)
            k = 3 * a + r - 1
            return pltpu.make_async_remote_copy(src_ref=dst, dst_ref=dst, send_sem=d2d_send.at[k],
                                                recv_sem=d2d_recv.at[k], device_id=sib, device_id_type=MESH)

        for a in range(n):
            for r in (1, 2, 3):
                cp = ici(a, r, me)
                cp.start()
                sends.append(cp)
        for a in range(n):
            for r in (1, 2, 3):
                px, py = _chip_peer(xi, yi, r)
                ici(a, r, 2 * px + py).wait_recv()
                if a < nb:
                    cp = d2d(a, r, ci)
                    cp.start()
                    sends.append(cp)
        for a in range(nb):
            for r in (1, 2, 3):
                d2d(a, r, 1 - ci).wait_recv()
        for cp in sends:
            cp.wait_send()
        for cp in local:
            cp.wait()

    return pl.pallas_call(
        body, name="gather_weights", in_specs=[_ANY] * n, out_specs=[_ANY] * n, out_shape=out_shape,
        scratch_shapes=[pltpu.SemaphoreType.DMA((3 * n,)), pltpu.SemaphoreType.DMA((3 * n,)),
                        pltpu.SemaphoreType.DMA((3 * nb,)), pltpu.SemaphoreType.DMA((3 * nb,)),
                        pltpu.SemaphoreType.DMA((2 * nb + ns,))],
    )(*big, *small)


def rs_pair_send(gs):
    n = len(gs)

    def body(*refs):
        ins, outs = refs[:n], refs[n:2 * n]
        send_sems, recv_sems = refs[2 * n:]
        xi, yi, ci = _place()
        cps = []
        for a in range(n):
            cp = pltpu.make_async_remote_copy(
                src_ref=ins[a].at[1 - ci], dst_ref=outs[a], send_sem=send_sems.at[a], recv_sem=recv_sems.at[a],
                device_id=(xi, yi, 1 - ci), device_id_type=MESH)
            cp.start()
            cps.append(cp)
        for cp in cps:
            cp.wait()

    return pl.pallas_call(
        body, name="rs_pair_send", in_specs=[_ANY] * n, out_specs=[_ANY] * n,
        out_shape=[SDS(g.shape[1:], g.dtype) for g in gs],
        scratch_shapes=[pltpu.SemaphoreType.DMA((n,)), pltpu.SemaphoreType.DMA((n,))],
    )(*gs)


def rs_pair_add(g, other, ci, name):
    _, nsh, r, w = g.shape
    tr = _rows_tile(r, w, 16)

    def body(c_ref, g_ref, o_ref, p_ref, pb_ref):
        p = g_ref[0] + o_ref[...]
        p_ref[...] = p
        pb_ref[...] = p.astype(bf16)

    blk = pl.BlockSpec((1, tr, w), lambda j, i, c: (j, i, 0))
    return pl.pallas_call(
        body, name=name,
        grid_spec=pltpu.PrefetchScalarGridSpec(
            num_scalar_prefetch=1, grid=(nsh, r // tr),
            in_specs=[pl.BlockSpec((1, 1, tr, w), lambda j, i, c: (c[0], j, i, 0)), blk], out_specs=[blk, blk]),
        out_shape=[SDS((nsh, r, w), f32), SDS((nsh, r, w), bf16)],
        compiler_params=_cp("parallel", "parallel"),
    )(ci.reshape(1).astype(jnp.int32), g, other)


def rs_chip_send(ps):
    n = len(ps)

    def body(*refs):
        ins, outs = refs[:n], refs[n:2 * n]
        send_sems, recv_sems, loc_sems = refs[2 * n:]
        xi, yi, ci = _place()
        me = 2 * xi + yi
        local, sends = [], []
        for a in range(n):
            cp = pltpu.make_async_copy(ins[a].at[me], outs[a].at[me], loc_sems.at[a])
            cp.start()
            local.append(cp)

        def copy(a, r, src_slot, dst_slot):
            px, py = _chip_peer(xi, yi, r)
            k = 3 * a + r - 1
            return pltpu.make_async_remote_copy(
                src_ref=ins[a].at[src_slot], dst_ref=outs[a].at[dst_slot], send_sem=send_sems.at[k],
                recv_sem=recv_sems.at[k], device_id=(px, py, ci), device_id_type=MESH)

        for a in range(n):
            for r in (1, 2, 3):
                px, py = _chip_peer(xi, yi, r)
                cp = copy(a, r, 2 * px + py, me)
                cp.start()
                sends.append(cp)
        for a in range(n):
            for r in (1, 2, 3):
                px, py = _chip_peer(xi, yi, r)
                copy(a, r, me, 2 * px + py).wait_recv()
        for cp in sends:
            cp.wait_send()
        for cp in local:
            cp.wait()

    return pl.pallas_call(
        body, name="rs_chip_send", in_specs=[_ANY] * n, out_specs=[_ANY] * n,
        out_shape=[SDS(p.shape, p.dtype) for p in ps],
        scratch_shapes=[pltpu.SemaphoreType.DMA((3 * n,)), pltpu.SemaphoreType.DMA((3 * n,)),
                        pltpu.SemaphoreType.DMA((n,))],
    )(*ps)


def rs_chip_add(p32, parts, chip, name):
    nsh, r, w = parts.shape
    tr = _rows_tile(r, w, 16)

    def body(c_ref, p_ref, parts_ref, o_ref):
        me = c_ref[0]
        acc = None
        for j in range(N_CHIPS):
            term = jnp.where(me == j, p_ref[0], parts_ref[j].astype(f32))
            acc = term if acc is None else acc + term
        o_ref[...] = acc

    return pl.pallas_call(
        body, name=name,
        grid_spec=pltpu.PrefetchScalarGridSpec(
            num_scalar_prefetch=1, grid=(r // tr,),
            in_specs=[pl.BlockSpec((1, tr, w), lambda i, c: (c[0], i, 0)),
                      pl.BlockSpec((nsh, tr, w), lambda i, c: (0, i, 0))],
            out_specs=pl.BlockSpec((tr, w), lambda i, c: (i, 0))),
        out_shape=SDS((r, w), f32),
        compiler_params=_cp("parallel"),
    )(chip.reshape(1).astype(jnp.int32), p32, parts)


def pair_exchange(halves):
    n = len(halves)

    def body(*refs):
        ins, outs = refs[:n], refs[n:2 * n]
        send_sems, recv_sems, loc_sems = refs[2 * n:]
        xi, yi, ci = _place()
        cps = []
        for a in range(n):
            loc = pltpu.make_async_copy(ins[a], outs[a].at[ci], loc_sems.at[a])
            loc.start()
            cps.append(loc)

        def copy(a, slot):
            return pltpu.make_async_remote_copy(
                src_ref=ins[a], dst_ref=outs[a].at[slot], send_sem=send_sems.at[a], recv_sem=recv_sems.at[a],
                device_id=(xi, yi, 1 - ci), device_id_type=MESH)

        sends = []
        for a in range(n):
            cp = copy(a, ci)
            cp.start()
            sends.append(cp)
        for a in range(n):
            copy(a, 1 - ci).wait_recv()
        for cp in sends:
            cp.wait_send()
        for cp in cps:
            cp.wait()

    return pl.pallas_call(
        body, name="pair_exchange", in_specs=[_ANY] * n, out_specs=[_ANY] * n,
        out_shape=[SDS((2,) + h.shape, h.dtype) for h in halves],
        scratch_shapes=[pltpu.SemaphoreType.DMA((n,)), pltpu.SemaphoreType.DMA((n,)), pltpu.SemaphoreType.DMA((n,))],
    )(*halves)


STAGE_W = DFF
_ST = {"norm_mix0": 0, "norm_mix1": 1, "attn_b_qkv": 2, "attn_sinks": 3, "attn_b_o": 4, "final_norm": 5, "loss": 6,
       "norm_ffn0": 8, "norm_ffn1": 9, "ffn_b_dw0": 10, "ffn_b_dw1": 11, "conv_b_pw1": 12, "conv_b_dw": 13,
       "conv_ln_g": 14, "conv_ln_b": 15, "conv_b_pw2": 16, "ffn_w_dw0": 17, "ffn_w_dw1": 20, "conv_w_dw": 24}
STAGE_ROWS = 56
SMALL_REP = ("norm_mix", "attn_b_qkv", "attn_sinks", "attn_b_o", "norm_ffn", "ffn_b_dw", "final_norm")
SMALL_SH = ("conv_b_pw1", "conv_w_dw", "conv_b_dw", "conv_ln_g", "conv_ln_b", "conv_b_pw2", "ffn_w_dw")
_SMALL_PARTS = ("norm_mix0", "norm_mix1", "attn_b_qkv", "attn_sinks", "attn_b_o", "norm_ffn0", "norm_ffn1", "ffn_b_dw0",
                "ffn_b_dw1", "final_norm", "conv_b_pw1", "conv_b_dw", "conv_ln_g", "conv_ln_b", "conv_b_pw2", "loss",
                "ffn_w_dw0", "ffn_w_dw1", "conv_w_dw")


def small_reduce_adamw(parts, w, m, v):
    names = SMALL_REP + SMALL_SH
    npart, nw = len(_SMALL_PARTS), len(names)

    def body(*refs):
        part = dict(zip(_SMALL_PARTS, refs[:npart]))
        off = npart
        w_ref = dict(zip(names, refs[off:off + nw]))
        m_ref = dict(zip(names, refs[off + nw:off + 2 * nw]))
        v_ref = dict(zip(names, refs[off + 2 * nw:off + 3 * nw]))
        off += 3 * nw
        loss_ref = refs[off]
        g_out = dict(zip(names, refs[off + 1:off + 1 + nw]))
        d_out = dict(zip(names, refs[off + 1 + nw:off + 1 + 2 * nw]))
        m_out = dict(zip(names, refs[off + 1 + 2 * nw:off + 1 + 3 * nw]))
        v_out = dict(zip(names, refs[off + 1 + 3 * nw:off + 1 + 4 * nw]))
        stage_ref, buf_ref, tot_ref, send_sems, recv_sems = refs[off + 1 + 4 * nw:]

        xi, yi, ci = _place()
        me = 4 * xi + 2 * yi + ci
        chip = 2 * xi + yi

        stage_ref[...] = jnp.zeros_like(stage_ref)
        for name in _SMALL_PARTS:
            ref, r0 = part[name], _ST[name]
            if name in ("attn_sinks", "loss"):
                val = ref[...]
            elif name in ("ffn_w_dw0", "ffn_w_dw1", "conv_w_dw"):
                val = jnp.sum(ref[...], axis=1)
            else:
                val = jnp.sum(ref[...], axis=0, keepdims=True)
            stage_ref[r0:r0 + val.shape[0], 0:val.shape[1]] = val

        buf_ref[me] = stage_ref[...]

        def peer(r):
            px, py = _chip_peer(xi, yi, r >> 1)
            return px, py, (1 - ci if r & 1 else ci)

        def copy(r, slot):
            return pltpu.make_async_remote_copy(
                src_ref=stage_ref, dst_ref=buf_ref.at[slot], send_sem=send_sems.at[r - 1], recv_sem=recv_sems.at[r - 1],
                device_id=peer(r), device_id_type=MESH)

        sends = []
        for r in range(1, N_DEV):
            cp = copy(r, me)
            cp.start()
            sends.append(cp)
        for r in range(1, N_DEV):
            px, py, pc = peer(r)
            copy(r, 4 * px + 2 * py + pc).wait_recv()
        for cp in sends:
            cp.wait_send()
        acc = buf_ref[0]
        for d in range(1, N_DEV):
            acc = acc + buf_ref[d]
        tot_ref[...] = acc

        def rows(name, n, width):
            r0 = _ST[name]
            return tot_ref[r0:r0 + n, 0:width]

        def mine(name, n, width):
            r0 = _ST[name]
            out = tot_ref[r0:r0 + n, 0:width]
            for j in range(1, N_CHIPS):
                out = jnp.where(chip == j, tot_ref[r0:r0 + n, j * width:(j + 1) * width], out)
            return out

        loss_ref[...] = rows("loss", 1, 1)
        grads = {
            "norm_mix": rows("norm_mix0", 2, D), "attn_b_qkv": rows("attn_b_qkv", 1, QKV),
            "attn_sinks": rows("attn_sinks", 1, N_HEADS), "attn_b_o": rows("attn_b_o", 1, D),
            "norm_ffn": rows("norm_ffn0", 2, D), "ffn_b_dw": rows("ffn_b_dw0", 2, DFF),
            "final_norm": rows("final_norm", 1, D),
            "conv_b_pw1": mine("conv_b_pw1", 1, 2 * D // N_CHIPS), "conv_w_dw": mine("conv_w_dw", CONV_W, D // N_CHIPS),
            "conv_b_dw": mine("conv_b_dw", 1, D // N_CHIPS), "conv_ln_g": mine("conv_ln_g", 1, D // N_CHIPS),
            "conv_ln_b": mine("conv_ln_b", 1, D // N_CHIPS), "conv_b_pw2": mine("conv_b_pw2", 1, D // N_CHIPS),
        }
        for name in names:
            if name == "ffn_w_dw":
                continue
            at = 0 if name == "conv_w_dw" else Ellipsis
            g = grads[name]
            d, mn, vn = _adam_math(w_ref[name][at], g, m_ref[name][at], v_ref[name][at])
            g_out[name][at] = g
            d_out[name][at] = d
            m_out[name][at] = mn
            v_out[name][at] = vn
        for layer, key in enumerate(("ffn_w_dw0", "ffn_w_dw1")):
            g = mine(key, 3, DFF // N_CHIPS)
            d, mn, vn = _adam_math(w_ref["ffn_w_dw"][layer], g, m_ref["ffn_w_dw"][layer], v_ref["ffn_w_dw"][layer])
            g_out["ffn_w_dw"][layer] = g
            d_out["ffn_w_dw"][layer] = d
            m_out["ffn_w_dw"][layer] = mn
            v_out["ffn_w_dw"][layer] = vn

    ins = [parts[k] for k in _SMALL_PARTS] + [src[k] for src in (w, m, v) for k in names]
    wshapes = [SDS(w[k].shape, f32) for k in names]
    outs = pl.pallas_call(
        body, name="small_reduce_adamw", in_specs=[_VMEM] * len(ins), out_specs=[_VMEM] * (1 + 4 * nw),
        out_shape=[SDS((1, 1), f32)] + wshapes * 4,
        scratch_shapes=[pltpu.VMEM((STAGE_ROWS, STAGE_W), f32), pltpu.VMEM((N_DEV, STAGE_ROWS, STAGE_W), f32),
                        pltpu.VMEM((STAGE_ROWS, STAGE_W), f32), pltpu.SemaphoreType.DMA((N_DEV - 1,)),
                        pltpu.SemaphoreType.DMA((N_DEV - 1,))],
        compiler_params=pltpu.CompilerParams(vmem_limit_bytes=VMEM_LIMIT),
    )(*ins)
    loss = outs[0]
    g, d, mn, vn = (dict(zip(names, outs[1 + k * nw:1 + (k + 1) * nw])) for k in range(4))
    return loss, g, d, mn, vn


def kernel(x, norm_mix, attn_w_qkv, attn_b_qkv, attn_sinks, attn_w_o, attn_b_o, conv_w_pw1, conv_b_pw1, conv_w_dw, conv_b_dw, conv_ln_g, conv_ln_b, conv_w_pw2, conv_b_pw2, norm_ffn, ffn_w_up, ffn_w_dw, ffn_b_dw, ffn_w_down, final_norm, loss_target, m_norm_mix, m_attn_w_qkv, m_attn_b_qkv, m_attn_sinks, m_attn_w_o, m_attn_b_o, m_conv_w_pw1, m_conv_b_pw1, m_conv_w_dw, m_conv_b_dw, m_conv_ln_g, m_conv_ln_b, m_conv_w_pw2, m_conv_b_pw2, m_norm_ffn, m_ffn_w_up, m_ffn_w_dw, m_ffn_b_dw, m_ffn_w_down, m_final_norm, v_norm_mix, v_attn_w_qkv, v_attn_b_qkv, v_attn_sinks, v_attn_w_o, v_attn_b_o, v_conv_w_pw1, v_conv_b_pw1, v_conv_w_dw, v_conv_b_dw, v_conv_ln_g, v_conv_ln_b, v_conv_w_pw2, v_conv_b_pw2, v_norm_ffn, v_ffn_w_up, v_ffn_w_dw, v_ffn_b_dw, v_ffn_w_down, v_final_norm):
    w = dict(norm_mix=norm_mix, attn_w_qkv=attn_w_qkv, attn_b_qkv=attn_b_qkv, attn_sinks=attn_sinks, attn_w_o=attn_w_o,
             attn_b_o=attn_b_o, conv_w_pw1=conv_w_pw1, conv_b_pw1=conv_b_pw1, conv_w_dw=conv_w_dw, conv_b_dw=conv_b_dw,
             conv_ln_g=conv_ln_g, conv_ln_b=conv_ln_b, conv_w_pw2=conv_w_pw2, conv_b_pw2=conv_b_pw2, norm_ffn=norm_ffn,
             ffn_w_up=ffn_w_up, ffn_w_dw=ffn_w_dw, ffn_b_dw=ffn_b_dw, ffn_w_down=ffn_w_down, final_norm=final_norm)
    mom = dict(norm_mix=m_norm_mix, attn_w_qkv=m_attn_w_qkv, attn_b_qkv=m_attn_b_qkv, attn_sinks=m_attn_sinks,
               attn_w_o=m_attn_w_o, attn_b_o=m_attn_b_o, conv_w_pw1=m_conv_w_pw1, conv_b_pw1=m_conv_b_pw1,
               conv_w_dw=m_conv_w_dw, conv_b_dw=m_conv_b_dw, conv_ln_g=m_conv_ln_g, conv_ln_b=m_conv_ln_b,
               conv_w_pw2=m_conv_w_pw2, conv_b_pw2=m_conv_b_pw2, norm_ffn=m_norm_ffn, ffn_w_up=m_ffn_w_up,
               ffn_w_dw=m_ffn_w_dw, ffn_b_dw=m_ffn_b_dw, ffn_w_down=m_ffn_w_down, final_norm=m_final_norm)
    vel = dict(norm_mix=v_norm_mix, attn_w_qkv=v_attn_w_qkv, attn_b_qkv=v_attn_b_qkv, attn_sinks=v_attn_sinks,
               attn_w_o=v_attn_w_o, attn_b_o=v_attn_b_o, conv_w_pw1=v_conv_w_pw1, conv_b_pw1=v_conv_b_pw1,
               conv_w_dw=v_conv_w_dw, conv_b_dw=v_conv_b_dw, conv_ln_g=v_conv_ln_g, conv_ln_b=v_conv_ln_b,
               conv_w_pw2=v_conv_w_pw2, conv_b_pw2=v_conv_b_pw2, norm_ffn=v_norm_ffn, ffn_w_up=v_ffn_w_up,
               ffn_w_dw=v_ffn_w_dw, ffn_b_dw=v_ffn_b_dw, ffn_w_down=v_ffn_w_down, final_norm=v_final_norm)
    order = ("norm_mix", "attn_w_qkv", "attn_b_qkv", "attn_sinks", "attn_w_o", "attn_b_o", "conv_w_pw1", "conv_b_pw1",
             "conv_w_dw", "conv_b_dw", "conv_ln_g", "conv_ln_b", "conv_w_pw2", "conv_b_pw2", "norm_ffn", "ffn_w_up",
             "ffn_w_dw", "ffn_b_dw", "ffn_w_down", "final_norm")
    xi, yi, ci = _place()
    chip = 2 * xi + yi

    big_in = [attn_w_qkv[0].astype(bf16), attn_w_o[0].astype(bf16), conv_w_pw1[0].astype(bf16),
              conv_w_pw2[0].astype(bf16), ffn_w_up.astype(bf16), ffn_w_down.astype(bf16)]
    small_in = [w[k] for k in SMALL_SH]
    gathered = gather_weights(big_in, small_in)
    qkv4, w_o, w_pw1, w_pw2, w_up, w_down = gathered[:6]
    sm = dict(zip(SMALL_SH, gathered[6:]))
    w_qkv = jnp.transpose(qkv4, (1, 0, 2)).reshape(D, QKV)
    join = lambda a, axis: jnp.concatenate([a[j] for j in range(N_CHIPS)], axis=axis)
    p = {
        "norm_mix": norm_mix, "norm_ffn": norm_ffn, "final_norm": final_norm.reshape(1, D),
        "w_qkv": w_qkv, "w_qkv_t": w_qkv.T, "b_qkv": attn_b_qkv, "sinks": attn_sinks.reshape(N_HEADS),
        "w_o": w_o, "w_o_t": w_o.T, "b_o": attn_b_o,
        "w_pw1": w_pw1, "w_pw1_t": w_pw1.T, "b_pw1": join(sm["conv_b_pw1"], 1),
        "conv_w_dw": join(sm["conv_w_dw"], 2)[0], "conv_b_dw": join(sm["conv_b_dw"], 1),
        "ln_g": join(sm["conv_ln_g"], 1), "ln_b": join(sm["conv_ln_b"], 1),
        "w_pw2": w_pw2, "w_pw2_t": w_pw2.T, "b_pw2": join(sm["conv_b_pw2"], 1),
        "w_up": w_up, "w_up_t": jnp.swapaxes(w_up, 1, 2), "ffn_w_dw": join(sm["ffn_w_dw"], 2), "ffn_b_dw": ffn_b_dw,
        "w_down": w_down, "w_down_t": jnp.swapaxes(w_down, 1, 2),
    }

    dx, big, small = local_grads(x[0], loss_target[0], p)

    keys = ("qkv", "wo", "pw1", "pw2", "up0", "up1", "down0", "down1")
    from_pair = rs_pair_send([big[k] for k in keys])
    p32, pb = zip(*[rs_pair_add(big[k], o, ci, "rs_pair_add_" + k) for k, o in zip(keys, from_pair)])
    parts = rs_chip_send(list(pb))
    halves = [rs_chip_add(a, b, chip, "rs_chip_add_" + k) for k, a, b in zip(keys, p32, parts)]
    red = dict(zip(keys, pair_exchange(halves)))
    gbig = {
        "attn_w_qkv": red["qkv"].reshape(1, D, QKV // N_CHIPS), "attn_w_o": red["wo"].reshape(1, D // N_CHIPS, D),
        "conv_w_pw1": red["pw1"].reshape(1, D, 2 * D // N_CHIPS), "conv_w_pw2": red["pw2"].reshape(1, D // N_CHIPS, D),
        "ffn_w_up": jnp.stack([red["up0"], red["up1"]]).reshape(2, D, 2 * DFF // N_CHIPS),
        "ffn_w_down": jnp.stack([red["down0"], red["down1"]]).reshape(2, DFF // N_CHIPS, D),
    }

    fix = lambda d: {**d, "final_norm": d["final_norm"].reshape(1, D)}
    loss, gs, ds, ms, vs = small_reduce_adamw(small, fix(w), fix(mom), fix(vel))
    unfix = lambda d: {**d, "final_norm": d["final_norm"].reshape(D)}
    gout, delta, new_m, new_v = unfix(gs), unfix(ds), unfix(ms), unfix(vs)

    for name, g in gbig.items():
        gout[name] = g
        delta[name], new_m[name], new_v[name] = adamw(w[name], g, mom[name], vel[name], "adamw_" + name)

    return (loss.reshape(()), dx[None], *[gout[n] for n in order], *[delta[n] for n in order],
            *[new_m[n] for n in order], *[new_v[n] for n in order])
```

```python
import math

import jax
import jax.numpy as jnp
from jax import lax
from jax.experimental import pallas as pl
from jax.experimental.pallas import tpu as pltpu

f32 = jnp.float32
bf16 = jnp.bfloat16
SDS = jax.ShapeDtypeStruct
MESH = pl.DeviceIdType.MESH

D = 1024
N_HEADS = 16
N_KV = 2
GROUP = 8
HD = 64
BLK = 128
QKV = (N_HEADS + 2 * N_KV) * HD
KV_COL_BLOCK = (N_HEADS * HD) // (2 * N_KV * HD)
CONV_W = 31
CONV_HALO = 32
DFF = 2816
FFN_HALO = 16
RMS_EPS = 1e-6
LN_EPS = 1e-5
LR, B1, B2, ADAM_EPS, WD, STEP = 0.001, 0.9, 0.999, 1e-08, 0.01, 10

N_CHIPS = 4
N_DEV = 8
VMEM_LIMIT = 56 * 1024 * 1024
LANES = 128
SUB = 8
ELEMENTWISE_BLOCK_BYTES = 1 << 20


def _cp(*sem):
    return pltpu.CompilerParams(dimension_semantics=sem, vmem_limit_bytes=VMEM_LIMIT)


def _row(tm, n):
    return pl.BlockSpec((tm, n), lambda i: (i, 0))


def _const(shape):
    return pl.BlockSpec(shape, lambda *_: (0,) * len(shape), pipeline_mode=pl.Buffered(1))


def _acc(shape):
    return pl.BlockSpec(shape, lambda *_: (0,) * len(shape))


def _rms_fwd(x, g):
    r = lax.rsqrt(jnp.mean(x * x, axis=-1, keepdims=True) + RMS_EPS)
    xn = x * r
    return xn * g, xn, r


def _colsum8(v):
    return jnp.sum(v.reshape(v.shape[0] // SUB, SUB, v.shape[1]), axis=0)


def _rms_bwd(xn, r, g, dh):
    dyn = dh * g
    dx = r * (dyn - xn * jnp.mean(dyn * xn, axis=-1, keepdims=True))
    return dx, _colsum8(dh * xn)


def _sigmoid(z):
    return 0.5 * jnp.tanh(0.5 * z) + 0.5


def _dsilu(z, sg):
    return sg * (1.0 + z * (1.0 - sg))


def _dot(a, b):
    return jnp.dot(a, b, preferred_element_type=f32)


_ANY = pl.BlockSpec(memory_space=pl.ANY)
_VMEM = pl.BlockSpec(memory_space=pltpu.VMEM)


class Rider:
    def __init__(self, ins, out_shape, sem_shapes, stages, final):
        self.ins, self.out_shape, self.sem_shapes, self.stages, self.final = ins, out_shape, sem_shapes, stages, final


def run_rider(rider, name):
    n_in, n_out = len(rider.ins), len(rider.out_shape)

    def body(*refs):
        parts = refs[:n_in], refs[n_in:n_in + n_out], refs[n_in + n_out:]
        for _, fn in rider.stages:
            fn(*parts)
        rider.final(*parts)

    return pl.pallas_call(
        body, name=name, in_specs=[_ANY] * n_in, out_specs=[_ANY] * n_out, out_shape=list(rider.out_shape),
        scratch_shapes=list(rider.sem_shapes),
    )(*rider.ins)


def _hosted(rider, body, *, grid, in_specs, out_specs, out_shape, name, compiler_params, scratch_shapes=()):
    if rider is None:
        return pl.pallas_call(body, grid=grid, in_specs=in_specs, out_specs=out_specs, out_shape=out_shape, name=name,
                              compiler_params=compiler_params, scratch_shapes=list(scratch_shapes))
    single = not isinstance(out_shape, (list, tuple))
    shapes = [out_shape] if single else list(out_shape)
    specs = [out_specs] if single else list(out_specs)
    n_in, n_out, n_sc = len(in_specs), len(shapes), len(scratch_shapes)
    r_in, r_out = len(rider.ins), len(rider.out_shape)
    total = math.prod(grid)

    def wrapped(*refs):
        own_in, refs = refs[:n_in], refs[n_in:]
        r_ins, refs = refs[:r_in], refs[r_in:]
        own_out, refs = refs[:n_out], refs[n_out:]
        r_outs, refs = refs[:r_out], refs[r_out:]
        own_sc, r_sems = refs[:n_sc], refs[n_sc:]
        step = 0
        for d, n in enumerate(grid):
            step = step * n + pl.program_id(d)
        for frac, fn in rider.stages:
            @pl.when(step == min(int(frac * total), total - 1))
            def _(fn=fn):
                fn(r_ins, r_outs, r_sems)

        body(*own_in, *own_out, *own_sc)

        @pl.when(step == total - 1)
        def _():
            rider.final(r_ins, r_outs, r_sems)

    call = pl.pallas_call(
        wrapped, grid=grid, in_specs=list(in_specs) + [_ANY] * r_in, out_specs=specs + [_ANY] * r_out,
        out_shape=shapes + list(rider.out_shape), scratch_shapes=list(scratch_shapes) + list(rider.sem_shapes),
        name=name, compiler_params=_cp(*(("arbitrary",) * len(grid))))

    def run(*args):
        res = call(*args, *rider.ins)
        own = res[:n_out]
        return (own[0] if single else own), res[n_out:]

    return run


def qkv_fwd(x, g, w, b, tm):
    s = x.shape[0]

    def body(x_ref, g_ref, w_ref, b_ref, h_ref, o_ref):
        h, _, _ = _rms_fwd(x_ref[...], g_ref[...])
        hb = h.astype(bf16)
        h_ref[...] = hb
        o_ref[...] = (_dot(hb, w_ref[...]) + b_ref[...]).astype(bf16)

    return pl.pallas_call(
        body, grid=(s // tm,), name="qkv_fwd",
        in_specs=[_row(tm, D), _const((1, D)), _const((D, QKV)), _const((1, QKV))],
        out_specs=[_row(tm, D), _row(tm, QKV)],
        out_shape=[SDS((s, D), bf16), SDS((s, QKV), bf16)],
        compiler_params=_cp("parallel"),
    )(x, g, w, b)


def _band_mask(i):
    qi = lax.broadcasted_iota(jnp.int32, (GROUP * BLK, 2 * BLK), 0) & (BLK - 1)
    ki = lax.broadcasted_iota(jnp.int32, (GROUP * BLK, 2 * BLK), 1)
    dist = qi + BLK - ki
    return (dist >= 0) & (dist < BLK) & ((ki >= BLK) | (i > 0))


_NEG = float(jnp.finfo(jnp.float32).min)
_NT = (((1,), (1,)), ((), ()))
_TN = (((0,), (0,)), ((), ()))


def _kv_heads(kvp_ref, kvc_ref, kvh):
    ks = slice(kvh * HD, (kvh + 1) * HD)
    vs = slice(N_KV * HD + kvh * HD, N_KV * HD + (kvh + 1) * HD)
    k = jnp.concatenate([kvp_ref[:, ks], kvc_ref[:, ks]], axis=0)
    v = jnp.concatenate([kvp_ref[:, vs], kvc_ref[:, vs]], axis=0)
    return k, v


def _stack_group(ref, kvh, width=HD):
    return jnp.concatenate([ref[:, (kvh * GROUP + gi) * width:(kvh * GROUP + gi + 1) * width] for gi in range(GROUP)],
                           axis=0)


def _group_sinks(sink_ref, kvh):
    row = lax.broadcasted_iota(jnp.int32, (GROUP * BLK, 1), 0)
    col = jnp.zeros((GROUP * BLK, 1), f32)
    for g in range(GROUP):
        col = jnp.where((row >= g * BLK) & (row < (g + 1) * BLK), sink_ref[kvh * GROUP + g], col)
    return col


def attn_fwd(qkv, sinks, rider=None):
    s = qkv.shape[0]
    scale = 1.0 / math.sqrt(HD)

    def body(q_ref, kvc_ref, kvp_ref, sink_ref, o_ref, lse_ref):
        valid = _band_mask(pl.program_id(0))
        for kvh in range(N_KV):
            k, v = _kv_heads(kvp_ref, kvc_ref, kvh)
            sc = lax.dot_general(_stack_group(q_ref, kvh), k, _NT, preferred_element_type=f32) * scale
            sc = jnp.where(valid, sc, _NEG)
            sink = _group_sinks(sink_ref, kvh)
            m = jnp.maximum(jnp.max(sc, axis=-1, keepdims=True), sink)
            p = jnp.exp(sc - m)
            denom = jnp.sum(p, axis=-1, keepdims=True) + jnp.exp(sink - m)
            og = _dot((p / denom).astype(bf16), v).astype(bf16)
            lse = m + jnp.log(denom)
            for gi in range(GROUP):
                h = kvh * GROUP + gi
                o_ref[:, h * HD:(h + 1) * HD] = og[gi * BLK:(gi + 1) * BLK]
                lse_ref[:, h:h + 1] = lse[gi * BLK:(gi + 1) * BLK]

    return _hosted(
        rider, body, grid=(s // BLK,), name="attn_fwd",
        in_specs=[
            pl.BlockSpec((BLK, N_HEADS * HD), lambda i: (i, 0)),
            pl.BlockSpec((BLK, 2 * N_KV * HD), lambda i: (i, KV_COL_BLOCK)),
            pl.BlockSpec((BLK, 2 * N_KV * HD), lambda i: (jnp.maximum(i - 1, 0), KV_COL_BLOCK)),
            pl.BlockSpec(memory_space=pltpu.SMEM),
        ],
        out_specs=[_row(BLK, D), _row(BLK, N_HEADS)],
        out_shape=[SDS((s, D), bf16), SDS((s, N_HEADS), f32)],
        compiler_params=_cp("parallel"),
    )(qkv, qkv, qkv, sinks)


def attn_out_fwd(x, o, w, b, tm):
    s = x.shape[0]

    def body(x_ref, o_ref, w_ref, b_ref, y_ref):
        y_ref[...] = x_ref[...] + _dot(o_ref[...], w_ref[...]) + b_ref[...]

    return pl.pallas_call(
        body, grid=(s // tm,), name="attn_out_fwd",
        in_specs=[_row(tm, D), _row(tm, D), _const((D, D)), _const((1, D))],
        out_specs=_row(tm, D), out_shape=SDS((s, D), f32),
        compiler_params=_cp("parallel"),
    )(x, o, w, b)


def attn_out_bwd(dy, wt, tm, rider=None):
    s = dy.shape[0]

    def body(dy_ref, wt_ref, do_ref, db_ref):
        @pl.when(pl.program_id(0) == 0)
        def _():
            db_ref[...] = jnp.zeros_like(db_ref)

        dy = dy_ref[...]
        do_ref[...] = _dot(dy.astype(bf16), wt_ref[...]).astype(bf16)
        db_ref[...] += _colsum8(dy)

    return _hosted(
        rider, body, grid=(s // tm,), name="attn_out_bwd",
        in_specs=[_row(tm, D), _const((D, D))],
        out_specs=[_row(tm, D), _acc((SUB, D))],
        out_shape=[SDS((s, D), bf16), SDS((SUB, D), f32)],
        compiler_params=_cp("arbitrary"),
    )(dy, wt)


def attn_bwd(qkv, o, do, lse, sinks, rider=None):
    s = qkv.shape[0]
    nb = s // BLK
    scale = 1.0 / math.sqrt(HD)
    kvw = 2 * N_KV * HD

    def body(q_ref, kvc_ref, kvp_ref, o_ref, do_ref, lse_ref, sink_ref, dq_ref, dkv_ref, ds_ref, carry_ref):
        i = pl.program_id(0)

        @pl.when(i == 0)
        def _():
            ds_ref[...] = jnp.zeros_like(ds_ref)
            carry_ref[...] = jnp.zeros_like(carry_ref)

        @pl.when(i < nb)
        def _():
            valid = _band_mask(i)
            for kvh in range(N_KV):
                k, v = _kv_heads(kvp_ref, kvc_ref, kvh)
                qg = _stack_group(q_ref, kvh)
                dog = _stack_group(do_ref, kvh)
                lse = _stack_group(lse_ref, kvh, 1)
                sc = lax.dot_general(qg, k, _NT, preferred_element_type=f32) * scale
                sc = jnp.where(valid, sc, _NEG)
                p = jnp.exp(sc - lse)
                dp = lax.dot_general(dog, v, _NT, preferred_element_type=f32)
                dlt = jnp.sum(dog.astype(f32) * _stack_group(o_ref, kvh).astype(f32), axis=-1, keepdims=True)
                dsc = (p * (dp - dlt)).astype(bf16)
                dqg = (_dot(dsc, k) * scale).astype(bf16)
                dk = lax.dot_general(dsc, qg, _TN, preferred_element_type=f32) * scale
                dv = lax.dot_general(p.astype(bf16), dog, _TN, preferred_element_type=f32)
                dsink = jnp.exp(_group_sinks(sink_ref, kvh) - lse) * dlt
                for gi in range(GROUP):
                    h = kvh * GROUP + gi
                    dq_ref[:, h * HD:(h + 1) * HD] = dqg[gi * BLK:(gi + 1) * BLK]
                    ds_ref[:, h:h + 1] += -jnp.sum(dsink[gi * BLK:(gi + 1) * BLK], axis=0, keepdims=True)
                ks = slice(kvh * HD, (kvh + 1) * HD)
                vs = slice(N_KV * HD + kvh * HD, N_KV * HD + (kvh + 1) * HD)
                dkv_ref[:, ks] = carry_ref[:, ks] + dk[:BLK]
                dkv_ref[:, vs] = carry_ref[:, vs] + dv[:BLK]
                carry_ref[:, ks] = dk[BLK:]
                carry_ref[:, vs] = dv[BLK:]

        @pl.when(i == nb)
        def _():
            dkv_ref[...] = carry_ref[...]

    cur = lambda i: (jnp.minimum(i, nb - 1), 0)
    prev = lambda i: (jnp.clip(i - 1, 0, nb - 1), KV_COL_BLOCK)
    return _hosted(
        rider, body, grid=(nb + 1,), name="attn_bwd",
        in_specs=[
            pl.BlockSpec((BLK, D), cur),
            pl.BlockSpec((BLK, kvw), lambda i: (jnp.minimum(i, nb - 1), KV_COL_BLOCK)),
            pl.BlockSpec((BLK, kvw), prev),
            pl.BlockSpec((BLK, D), cur),
            pl.BlockSpec((BLK, D), cur),
            pl.BlockSpec((BLK, N_HEADS), cur),
            pl.BlockSpec(memory_space=pltpu.SMEM),
        ],
        out_specs=[
            pl.BlockSpec((BLK, D), cur),
            pl.BlockSpec((BLK, kvw), lambda i: (jnp.maximum(i - 1, 0), 0)),
            _acc((1, N_HEADS)),
        ],
        out_shape=[SDS((s, D), bf16), SDS((s, kvw), f32), SDS((1, N_HEADS), f32)],
        scratch_shapes=[pltpu.VMEM((BLK, kvw), f32)],
        compiler_params=_cp("arbitrary"),
    )(qkv, qkv, qkv, o, do, lse, sinks)


def qkv_bwd(dq, dkv, x, g, wt, dres, tm):
    s = x.shape[0]
    qd = N_HEADS * HD
    kvw = 2 * N_KV * HD

    def body(dq_ref, dkv_ref, x_ref, g_ref, wt_ref, dres_ref, dx_ref, dg_ref, db_ref, dkvb_ref):
        @pl.when(pl.program_id(0) == 0)
        def _():
            dg_ref[...] = jnp.zeros_like(dg_ref)
            db_ref[...] = jnp.zeros_like(db_ref)

        dq = dq_ref[...]
        dkv = dkv_ref[...]
        dkvb = dkv.astype(bf16)
        dkvb_ref[...] = dkvb
        dh = _dot(dq, wt_ref[0:qd, :]) + _dot(dkvb, wt_ref[qd:QKV, :])
        g = g_ref[...]
        _, xn, r = _rms_fwd(x_ref[...], g)
        dx, dg = _rms_bwd(xn, r, g, dh)
        dx_ref[...] = dres_ref[...] + dx
        dg_ref[...] += dg
        db_ref[:, 0:qd] += _colsum8(dq.astype(f32))
        db_ref[:, qd:QKV] += _colsum8(dkv)

    return pl.pallas_call(
        body, grid=(s // tm,), name="qkv_bwd",
        in_specs=[_row(tm, qd), _row(tm, kvw), _row(tm, D), _const((1, D)), _const((QKV, D)), _row(tm, D)],
        out_specs=[_row(tm, D), _acc((SUB, D)), _acc((SUB, QKV)), _row(tm, kvw)],
        out_shape=[SDS((s, D), f32), SDS((SUB, D), f32), SDS((SUB, QKV), f32), SDS((s, kvw), bf16)],
        compiler_params=_cp("arbitrary"),
    )(dq, dkv, x, g, wt, dres)


def ffn_fwd(x, g, wup, wdw, bdw, wdn, tm, cw, rider=None):
    s = x.shape[0]
    tail = 8

    def body(x_ref, g_ref, wup_ref, wdw_ref, bdw_ref, wdn_ref, xo_ref, h_ref, up_ref, act_ref, carry_ref, ext_ref):
        @pl.when(pl.program_id(0) == 0)
        def _():
            carry_ref[...] = jnp.zeros_like(carry_ref)

        x = x_ref[...]
        h, _, _ = _rms_fwd(x, g_ref[...])
        hb = h.astype(bf16)
        h_ref[...] = hb
        for c in range(DFF // cw):
            cs = slice(c * cw, (c + 1) * cw)
            vs = slice(DFF + c * cw, DFF + (c + 1) * cw)
            ug = _dot(hb, wup_ref[:, cs])
            uv = _dot(hb, wup_ref[:, vs])
            up_ref[:, cs] = ug.astype(bf16)
            up_ref[:, vs] = uv.astype(bf16)
            ext_ref[0:tail, :] = carry_ref[:, cs]
            ext_ref[tail:tail + tm, :] = ug
            carry_ref[:, cs] = ug[tm - tail:, :]
            gate = (wdw_ref[0:1, cs] * ext_ref[pl.ds(tail - 2, tm), :]
                    + wdw_ref[1:2, cs] * ext_ref[pl.ds(tail - 1, tm), :]
                    + wdw_ref[2:3, cs] * ug) + bdw_ref[:, cs]
            act_ref[:, cs] = (gate * _sigmoid(gate) * uv).astype(bf16)
        xo_ref[...] = x + _dot(act_ref[...], wdn_ref[...])

    return _hosted(
        rider, body, grid=(s // tm,), name="ffn_fwd",
        in_specs=[_row(tm, D), _const((1, D)), _const((D, 2 * DFF)), _const((3, DFF)), _const((1, DFF)),
                  _const((DFF, D))],
        out_specs=[_row(tm, D), _row(tm, D), _row(tm, 2 * DFF), _row(tm, DFF)],
        out_shape=[SDS((s, D), f32), SDS((s, D), bf16), SDS((s, 2 * DFF), bf16), SDS((s, DFF), bf16)],
        scratch_shapes=[pltpu.VMEM((tail, DFF), f32), pltpu.VMEM((tail + tm, cw), f32)],
        compiler_params=_cp("arbitrary"),
    )(x, g, wup, wdw, bdw, wdn)


def ffn_bwd(dxo, x, g, up, wdw, bdw, wdnt, wupt, tm, cw, rider=None):
    s = x.shape[0]
    nt = s // tm
    hb = FFN_HALO
    rev = lambda i: (nt - 1 - i, 0)

    def body(dxo_ref, x_ref, g_ref, up_ref, halo_ref, wdw_ref, bdw_ref, wdnt_ref, wupt_ref,
             dxi_ref, dup_ref, dg_ref, dwdw_ref, dbdw_ref, carry_ref, ext2_ref):
        i = pl.program_id(0)

        @pl.when(i == 0)
        def _():
            carry_ref[...] = jnp.zeros_like(carry_ref)
            dg_ref[...] = jnp.zeros_like(dg_ref)
            dwdw_ref[...] = jnp.zeros_like(dwdw_ref)
            dbdw_ref[...] = jnp.zeros_like(dbdw_ref)

        has_prev = (i < nt - 1).astype(bf16)
        dxo = dxo_ref[...]
        dxb = dxo.astype(bf16)
        tr = lax.broadcasted_iota(jnp.int32, (2 * tm, tm), 0)
        tc = lax.broadcasted_iota(jnp.int32, (2 * tm, tm), 1)
        back = jnp.where(tr < tm, 1, 2)
        pick = (tc == jnp.where(tr < tm, tr, tr - tm) - back).astype(bf16)
        hr = lax.broadcasted_iota(jnp.int32, (2 * tm, hb), 0)
        hc = lax.broadcasted_iota(jnp.int32, (2 * tm, hb), 1)
        pick_halo = (hc == jnp.where(hr < tm, hr + hb - 1, hr - tm + hb - 2)).astype(bf16)
        for c in range(DFF // cw):
            cs = slice(c * cw, (c + 1) * cw)
            vs = slice(DFF + c * cw, DFF + (c + 1) * cw)
            d_act = _dot(dxb, wdnt_ref[:, cs])
            ugb = up_ref[:, cs]
            ug = ugb.astype(f32)
            uv = up_ref[:, vs].astype(f32)
            e10 = _dot(pick, ugb) + _dot(pick_halo, halo_ref[:, cs] * has_prev)
            e1 = e10[:tm]
            e0 = e10[tm:]
            w0, w1, w2 = wdw_ref[0:1, cs], wdw_ref[1:2, cs], wdw_ref[2:3, cs]
            gate = (w0 * e0 + w1 * e1 + w2 * ug) + bdw_ref[:, cs]
            sg = _sigmoid(gate)
            dup_ref[:, vs] = (d_act * (gate * sg)).astype(bf16)
            d_gate = d_act * uv * _dsilu(gate, sg)
            dbdw_ref[:, cs] += _colsum8(d_gate)
            dwdw_ref[0, :, cs] += _colsum8(d_gate * e0)
            dwdw_ref[1, :, cs] += _colsum8(d_gate * e1)
            dwdw_ref[2, :, cs] += _colsum8(d_gate * ug)
            ext2_ref[0:tm, :] = d_gate
            ext2_ref[tm:tm + 8, :] = carry_ref[:, cs]
            carry_ref[:, cs] = d_gate[0:8, :]
            d_ug = w0 * ext2_ref[pl.ds(2, tm), :] + w1 * ext2_ref[pl.ds(1, tm), :] + w2 * d_gate
            dup_ref[:, cs] = d_ug.astype(bf16)
        dh = _dot(dup_ref[...], wupt_ref[...])
        gv = g_ref[...]
        _, xn, r = _rms_fwd(x_ref[...], gv)
        dx, dg = _rms_bwd(xn, r, gv, dh)
        dxi_ref[...] = dxo + dx
        dg_ref[...] += dg

    return _hosted(
        rider, body, grid=(nt,), name="ffn_bwd",
        in_specs=[
            pl.BlockSpec((tm, D), rev), pl.BlockSpec((tm, D), rev), _const((1, D)),
            pl.BlockSpec((tm, 2 * DFF), rev),
            pl.BlockSpec((hb, 2 * DFF), lambda i: (jnp.maximum((nt - 1 - i) * (tm // hb) - 1, 0), 0)),
            _const((3, DFF)), _const((1, DFF)), _const((D, DFF)), _const((2 * DFF, D)),
        ],
        out_specs=[pl.BlockSpec((tm, D), rev), pl.BlockSpec((tm, 2 * DFF), rev), _acc((SUB, D)),
                   _acc((3, SUB, DFF)), _acc((SUB, DFF))],
        out_shape=[SDS((s, D), f32), SDS((s, 2 * DFF), bf16), SDS((SUB, D), f32), SDS((3, SUB, DFF), f32),
                   SDS((SUB, DFF), f32)],
        scratch_shapes=[pltpu.VMEM((8, DFF), f32), pltpu.VMEM((tm + 8, cw), f32)],
        compiler_params=_cp("arbitrary"),
    )(dxo, x, g, up, up, wdw, bdw, wdnt, wupt)


def pw1_fwd(x, g, w, b, tm):
    s = x.shape[0]

    def body(x_ref, g_ref, w_ref, b_ref, h_ref, a_ref, u_ref):
        h, _, _ = _rms_fwd(x_ref[...], g_ref[...])
        hb = h.astype(bf16)
        h_ref[...] = hb
        a = _dot(hb, w_ref[...]) + b_ref[...]
        a_ref[...] = a.astype(bf16)
        u_ref[...] = a[:, :D] * _sigmoid(a[:, D:])

    return pl.pallas_call(
        body, grid=(s // tm,), name="pw1_fwd",
        in_specs=[_row(tm, D), _const((1, D)), _const((D, 2 * D)), _const((1, 2 * D))],
        out_specs=[_row(tm, D), _row(tm, 2 * D), _row(tm, D)],
        out_shape=[SDS((s, D), bf16), SDS((s, 2 * D), bf16), SDS((s, D), f32)],
        compiler_params=_cp("parallel"),
    )(x, g, w, b)


def _ln_silu(c, lg, lb):
    mu = jnp.mean(c, axis=-1, keepdims=True)
    cc = c - mu
    var = jnp.mean(cc * cc, axis=-1, keepdims=True)
    rstd = lax.rsqrt(var + LN_EPS)
    xh = cc * rstd
    ln = xh * lg + lb
    sg = _sigmoid(ln)
    return xh, rstd, ln, sg


def _shifted_copies(ext_ref, sh_ref, cs, tm):
    n = CONV_HALO - SUB + tm
    for k in range(1, SUB):
        sh_ref[k - 1] = ext_ref[pl.ds(k, n), cs]


def _shifted_rows(ext_ref, sh_ref, cs, start, rows):
    q, k = divmod(start, SUB)
    if k == 0:
        return ext_ref[pl.ds(start, rows), cs]
    return sh_ref[k - 1, pl.ds(q * SUB, rows), :]


def conv_fwd(u, x, wdw, bdw, lg, lb, w2, b2, tm, rc):
    s = x.shape[0]
    hl = CONV_HALO
    off = hl - (CONV_W - 1)

    def body(u_ref, halo_ref, x_ref, wdw_ref, bdw_ref, lg_ref, lb_ref, w2_ref, b2_ref, c_ref, xo_ref, ext_ref, sh_ref):
        has_prev = (pl.program_id(0) > 0).astype(f32)
        ext_ref[0:hl, :] = halo_ref[...] * has_prev
        ext_ref[hl:hl + tm, :] = u_ref[...]
        for cc in range(D // LANES):
            cs = slice(cc * LANES, (cc + 1) * LANES)
            _shifted_copies(ext_ref, sh_ref, cs, tm)
            for rr in range(tm // rc):
                acc = jnp.zeros((rc, LANES), f32) + bdw_ref[:, cs]
                for j in range(CONV_W):
                    acc = acc + wdw_ref[j:j + 1, cs] * _shifted_rows(ext_ref, sh_ref, cs, rr * rc + off + j, rc)
                c_ref[rr * rc:(rr + 1) * rc, cs] = acc
        _, _, ln, sg = _ln_silu(c_ref[...], lg_ref[...], lb_ref[...])
        xo_ref[...] = x_ref[...] + _dot((ln * sg).astype(bf16), w2_ref[...]) + b2_ref[...]

    return pl.pallas_call(
        body, grid=(s // tm,), name="conv_fwd",
        in_specs=[_row(tm, D), pl.BlockSpec((hl, D), lambda i: (jnp.maximum(i * (tm // hl) - 1, 0), 0)), _row(tm, D),
                  _const((CONV_W, D)), _const((1, D)), _const((1, D)), _const((1, D)), _const((D, D)), _const((1, D))],
        out_specs=[_row(tm, D), _row(tm, D)],
        out_shape=[SDS((s, D), f32), SDS((s, D), f32)],
        scratch_shapes=[pltpu.VMEM((hl + tm, D), f32), pltpu.VMEM((SUB - 1, hl - SUB + tm, LANES), f32)],
        compiler_params=_cp("parallel"),
    )(u, u, x, wdw, bdw, lg, lb, w2, b2)


def conv_bwd_a(dy, c, lg, lb, w2t, tm, rider=None):
    s = dy.shape[0]

    def body(dy_ref, c_ref, lg_ref, lb_ref, w2t_ref, dc_ref, u3_ref, dlg_ref, dlb_ref, db2_ref, dbdw_ref):
        @pl.when(pl.program_id(0) == 0)
        def _():
            for r in (dlg_ref, dlb_ref, db2_ref, dbdw_ref):
                r[...] = jnp.zeros_like(r)

        dy = dy_ref[...]
        lg = lg_ref[...]
        xh, rstd, ln, sg = _ln_silu(c_ref[...], lg, lb_ref[...])
        u3_ref[...] = (ln * sg).astype(bf16)
        du3 = _dot(dy.astype(bf16), w2t_ref[...])
        dln = du3 * _dsilu(ln, sg)
        dxh = dln * lg
        dc = rstd * (dxh - jnp.mean(dxh, axis=-1, keepdims=True) - xh * jnp.mean(dxh * xh, axis=-1, keepdims=True))
        dc_ref[...] = dc
        dlg_ref[...] += _colsum8(dln * xh)
        dlb_ref[...] += _colsum8(dln)
        db2_ref[...] += _colsum8(dy)
        dbdw_ref[...] += _colsum8(dc)

    return _hosted(
        rider, body, grid=(s // tm,), name="conv_bwd_a",
        in_specs=[_row(tm, D), _row(tm, D), _const((1, D)), _const((1, D)), _const((D, D))],
        out_specs=[_row(tm, D), _row(tm, D)] + [_acc((SUB, D))] * 4,
        out_shape=[SDS((s, D), f32), SDS((s, D), bf16)] + [SDS((SUB, D), f32)] * 4,
        compiler_params=_cp("arbitrary"),
    )(dy, c, lg, lb, w2t)


def conv_bwd_b(dc, u, a, wdw, x, g, w1t, dres, tm, rc, rider=None):
    s = x.shape[0]
    nt = s // tm
    hl = CONV_HALO
    off = hl - (CONV_W - 1)

    def body(dc_ref, dnext_ref, u_ref, uprev_ref, a_ref, wdw_ref, x_ref, g_ref, w1t_ref, dres_ref,
             dx_ref, da_ref, dwdw_ref, db1_ref, dg_ref, ext_ref, ext2_ref, du_ref, sh_ref, sh2_ref):
        i = pl.program_id(0)

        @pl.when(i == 0)
        def _():
            for r in (dwdw_ref, db1_ref, dg_ref):
                r[...] = jnp.zeros_like(r)

        ext_ref[0:hl, :] = uprev_ref[...] * (i > 0).astype(f32)
        ext_ref[hl:hl + tm, :] = u_ref[...]
        ext2_ref[0:tm, :] = dc_ref[...]
        ext2_ref[tm:tm + hl, :] = dnext_ref[...] * (i < nt - 1).astype(f32)
        for cc in range(D // LANES):
            cs = slice(cc * LANES, (cc + 1) * LANES)
            _shifted_copies(ext_ref, sh_ref, cs, tm)
            _shifted_copies(ext2_ref, sh2_ref, cs, tm)
            for rr in range(tm // rc):
                r0 = rr * rc
                dcb = ext2_ref[r0:r0 + rc, cs]
                acc = jnp.zeros((rc, LANES), f32)
                for j in range(CONV_W):
                    acc = acc + wdw_ref[j:j + 1, cs] * _shifted_rows(ext2_ref, sh2_ref, cs, r0 + CONV_W - 1 - j, rc)
                    dwdw_ref[j, :, cs] += _colsum8(dcb * _shifted_rows(ext_ref, sh_ref, cs, r0 + off + j, rc))
                du_ref[r0:r0 + rc, cs] = acc
        du = du_ref[...]
        a1 = a_ref[:, :D].astype(f32)
        sg = _sigmoid(a_ref[:, D:].astype(f32))
        da1 = du * sg
        da2 = du * a1 * sg * (1.0 - sg)
        da_ref[:, :D] = da1.astype(bf16)
        da_ref[:, D:] = da2.astype(bf16)
        db1_ref[:, :D] += _colsum8(da1)
        db1_ref[:, D:] += _colsum8(da2)
        dh = _dot(da_ref[...], w1t_ref[...])
        gv = g_ref[...]
        _, xn, r = _rms_fwd(x_ref[...], gv)
        dx, dg = _rms_bwd(xn, r, gv, dh)
        dx_ref[...] = dres_ref[...] + dx
        dg_ref[...] += dg

    blocks = tm // hl
    return _hosted(
        rider, body, grid=(nt,), name="conv_bwd_b",
        in_specs=[
            _row(tm, D), pl.BlockSpec((hl, D), lambda i: (jnp.minimum((i + 1) * blocks, s // hl - 1), 0)),
            _row(tm, D), pl.BlockSpec((hl, D), lambda i: (jnp.maximum(i * blocks - 1, 0), 0)),
            _row(tm, 2 * D), _const((CONV_W, D)), _row(tm, D), _const((1, D)), _const((2 * D, D)), _row(tm, D),
        ],
        out_specs=[_row(tm, D), _row(tm, 2 * D), _acc((CONV_W, SUB, D)), _acc((SUB, 2 * D)), _acc((SUB, D))],
        out_shape=[SDS((s, D), f32), SDS((s, 2 * D), bf16), SDS((CONV_W, SUB, D), f32), SDS((SUB, 2 * D), f32),
                   SDS((SUB, D), f32)],
        scratch_shapes=[pltpu.VMEM((hl + tm, D), f32), pltpu.VMEM((tm + hl, D), f32), pltpu.VMEM((tm, D), f32),
                        pltpu.VMEM((SUB - 1, hl - SUB + tm, LANES), f32),
                        pltpu.VMEM((SUB - 1, hl - SUB + tm, LANES), f32)],
        compiler_params=_cp("arbitrary"),
    )(dc, dc, u, u, a, wdw, x, g, w1t, dres)


def loss_bwd(x, g, t, tm):
    s = x.shape[0]

    def body(x_ref, g_ref, t_ref, dx_ref, dg_ref, loss_ref):
        @pl.when(pl.program_id(0) == 0)
        def _():
            dg_ref[...] = jnp.zeros_like(dg_ref)
            loss_ref[...] = jnp.zeros_like(loss_ref)

        gv = g_ref[...]
        y, xn, r = _rms_fwd(x_ref[...], gv)
        e = y - t_ref[...]
        loss_ref[...] += 0.5 * jnp.sum(jnp.mean(e * e, axis=-1, keepdims=True), axis=0, keepdims=True)
        dx, dg = _rms_bwd(xn, r, gv, e / D)
        dx_ref[...] = dx
        dg_ref[...] += dg

    return pl.pallas_call(
        body, grid=(s // tm,), name="loss_bwd",
        in_specs=[_row(tm, D), _const((1, D)), _row(tm, D)],
        out_specs=[_row(tm, D), _acc((SUB, D)), _acc((1, 1))],
        out_shape=[SDS((s, D), f32), SDS((SUB, D), f32), SDS((1, 1), f32)],
        compiler_params=_cp("arbitrary"),
    )(x, g, t)


def wgrad(a, b, nb, tk, name, rider=None):
    s, k1 = a.shape
    n = b.shape[1]

    def body(a_ref, b_ref, o_ref):
        @pl.when(pl.program_id(1) == 0)
        def _():
            o_ref[...] = jnp.zeros_like(o_ref)

        o_ref[...] += lax.dot_general(a_ref[...], b_ref[...].astype(bf16), _TN, preferred_element_type=f32)

    return _hosted(
        rider, body, grid=(n // nb, s // tk), name=name,
        in_specs=[pl.BlockSpec((tk, k1), lambda j, k: (k, 0)), pl.BlockSpec((tk, nb), lambda j, k: (k, j))],
        out_specs=pl.BlockSpec((k1, nb), lambda j, k: (0, j)),
        out_shape=SDS((k1, n), f32),
        compiler_params=_cp("parallel", "arbitrary"),
    )(a, b)


def wgrad_cols(a, b, tk, name, rider=None):
    s, k1 = a.shape
    w = b.shape[1] // N_CHIPS

    def body(a_ref, b_ref, o_ref):
        @pl.when(pl.program_id(1) == 0)
        def _():
            o_ref[...] = jnp.zeros_like(o_ref)

        acc = lax.dot_general(a_ref[...], b_ref[...].astype(bf16), _TN, preferred_element_type=f32)
        o_ref[:, 0] += acc.reshape(2, k1 // 2, w)

    return _hosted(
        rider, body, grid=(N_CHIPS, s // tk), name=name,
        in_specs=[pl.BlockSpec((tk, k1), lambda j, k: (k, 0)), pl.BlockSpec((tk, w), lambda j, k: (k, j))],
        out_specs=pl.BlockSpec((2, 1, k1 // 2, w), lambda j, k: (0, j, 0, 0)),
        out_shape=SDS((2, N_CHIPS, k1 // 2, w), f32),
        compiler_params=_cp("parallel", "arbitrary"),
    )(a, b)


def wgrad_rows(a, b, nb, tk, name, rider=None):
    s, k1 = a.shape
    n = b.shape[1]
    r = k1 // (2 * N_CHIPS)

    def body(a_ref, b_ref, o_ref):
        @pl.when(pl.program_id(1) == 0)
        def _():
            o_ref[...] = jnp.zeros_like(o_ref)

        acc = lax.dot_general(a_ref[...], b_ref[...].astype(bf16), _TN, preferred_element_type=f32)
        for j in range(N_CHIPS):
            for h in range(2):
                o_ref[h, j] += acc[(2 * j + h) * r:(2 * j + h + 1) * r, :]

    return _hosted(
        rider, body, grid=(n // nb, s // tk), name=name,
        in_specs=[pl.BlockSpec((tk, k1), lambda j, k: (k, 0)), pl.BlockSpec((tk, nb), lambda j, k: (k, j))],
        out_specs=pl.BlockSpec((2, N_CHIPS, r, nb), lambda j, k: (0, 0, 0, j)),
        out_shape=SDS((2, N_CHIPS, r, n), f32),
        compiler_params=_cp("parallel", "arbitrary"),
    )(a, b)


def _adam_math(w, g, m, v):
    m = B1 * m + (1.0 - B1) * g
    v = B2 * v + (1.0 - B2) * (g * g)
    m_hat = m / (1.0 - B1 ** STEP)
    v_hat = v / (1.0 - B2 ** STEP)
    delta = -LR * (m_hat / (jnp.sqrt(v_hat) + ADAM_EPS) + WD * w)
    return delta, m, v


def _rows_tile(r, c, multiple=SUB):
    best = None
    for t in range(multiple, r + 1, multiple):
        if r % t == 0 and t * c * 4 <= ELEMENTWISE_BLOCK_BYTES:
            best = t
    return best if best is not None else r


def adamw(w, g, m, v, name):
    l, r, c = w.shape
    tr = _rows_tile(r, c)
    spec = pl.BlockSpec((1, tr, c), lambda i, j: (i, j, 0))

    def body(w_ref, g_ref, m_ref, v_ref, d_ref, mo_ref, vo_ref):
        d, mn, vn = _adam_math(w_ref[...], g_ref[...], m_ref[...], v_ref[...])
        d_ref[...] = d
        mo_ref[...] = mn
        vo_ref[...] = vn

    return pl.pallas_call(
        body, grid=(l, r // tr), name=name, in_specs=[spec] * 4, out_specs=[spec] * 3,
        out_shape=[SDS((l, r, c), f32)] * 3, compiler_params=_cp("parallel", "parallel"),
    )(w, g, m, v)


def _place():
    return lax.axis_index("x"), lax.axis_index("y"), lax.axis_index("c")


def _chip_peer(xi, yi, r):
    px = 1 - xi if r & 2 else xi
    py = 1 - yi if r & 1 else yi
    return px, py


def _gv_qkv(src, dst, j, h):
    rows = pl.ds(h * (D // 2), D // 2)
    return src.at[rows, :], dst.at[j, rows, :]


def _gv_rows(src, dst, j, h):
    r = src.shape[0] // 2
    return src.at[pl.ds(h * r, r), :], dst.at[pl.ds(j * 2 * r + h * r, r), :]


def _gv_cols(src, dst, j, h):
    r, w = src.shape[0] // 2, src.shape[1]
    return src.at[pl.ds(h * r, r), :], dst.at[pl.ds(h * r, r), pl.ds(j * w, w)]


def gather_rider(big, small, forward_at):
    nb, ns = len(big), len(small)
    n = nb + ns
    views = [v for _, v, _ in big]

    def env(ins, outs, sems):
        ici_send, ici_recv, d2d_send, d2d_recv, loc_sems = sems
        xi, yi, ci = _place()
        me = 2 * xi + yi

        def local(a, h):
            if a < nb:
                src, dst = views[a](ins[a], outs[a], me, h)
                return pltpu.make_async_copy(src, dst, loc_sems.at[2 * a + h])
            return pltpu.make_async_copy(ins[a], outs[a].at[me], loc_sems.at[nb + a])

        def ici(a, r, slot):
            px, py = _chip_peer(xi, yi, r)
            src, dst = views[a](ins[a], outs[a], slot, ci) if a < nb else (ins[a], outs[a].at[slot])
            k = 3 * a + r - 1
            return pltpu.make_async_remote_copy(src_ref=src, dst_ref=dst, send_sem=ici_send.at[k],
                                                recv_sem=ici_recv.at[k], device_id=(px, py, ci), device_id_type=MESH)

        def d2d(a, r, half):
            px, py = _chip_peer(xi, yi, r)
            _, dst = views[a](ins[a], outs[a], 2 * px + py, half)
            k = 3 * a + r - 1
            return pltpu.make_async_remote_copy(src_ref=dst, dst_ref=dst, send_sem=d2d_send.at[k],
                                                recv_sem=d2d_recv.at[k], device_id=(xi, yi, 1 - ci), device_id_type=MESH)

        return xi, yi, ci, me, local, ici, d2d

    def locals_of():
        return [(a, h) for a in range(nb) for h in range(2)] + [(a, 0) for a in range(nb, n)]

    def send(ins, outs, sems):
        _, _, _, me, local, ici, _ = env(ins, outs, sems)
        for a, h in locals_of():
            local(a, h).start()
        for a in range(n):
            for r in (1, 2, 3):
                ici(a, r, me).start()

    def forward(ins, outs, sems):
        xi, yi, ci, _, _, ici, d2d = env(ins, outs, sems)
        for a in range(n):
            for r in (1, 2, 3):
                px, py = _chip_peer(xi, yi, r)
                ici(a, r, 2 * px + py).wait_recv()
                if a < nb:
                    d2d(a, r, ci).start()

    def finish(ins, outs, sems):
        _, _, ci, me, local, ici, d2d = env(ins, outs, sems)
        for a in range(nb):
            for r in (1, 2, 3):
                d2d(a, r, 1 - ci).wait_recv()
        for a in range(n):
            for r in (1, 2, 3):
                ici(a, r, me).wait_send()
                if a < nb:
                    d2d(a, r, ci).wait_send()
        for a, h in locals_of():
            local(a, h).wait()

    dma = pltpu.SemaphoreType.DMA
    return Rider(
        ins=[b for b, _, _ in big] + list(small),
        out_shape=[SDS(shape, bf16) for _, _, shape in big] + [SDS((N_CHIPS,) + a.shape, a.dtype) for a in small],
        sem_shapes=[dma((3 * n,)), dma((3 * n,)), dma((max(3 * nb, 1),)), dma((max(3 * nb, 1),)), dma((2 * nb + ns,))],
        stages=[(0.0, send), (forward_at, forward)], final=finish)


def pair_send_rider(gs):
    n = len(gs)

    def copy(ins, outs, sems, a):
        xi, yi, ci = _place()
        return pltpu.make_async_remote_copy(
            src_ref=ins[a].at[1 - ci], dst_ref=outs[a], send_sem=sems[0].at[a], recv_sem=sems[1].at[a],
            device_id=(xi, yi, 1 - ci), device_id_type=MESH)

    def send(ins, outs, sems):
        for a in range(n):
            copy(ins, outs, sems, a).start()

    def finish(ins, outs, sems):
        for a in range(n):
            copy(ins, outs, sems, a).wait()

    dma = pltpu.SemaphoreType.DMA
    return Rider(ins=list(gs), out_shape=[SDS(g.shape[1:], g.dtype) for g in gs], sem_shapes=[dma((n,)), dma((n,))],
                 stages=[(0.0, send)], final=finish)


def chip_send_rider(ps):
    n = len(ps)

    def env(ins, outs, sems):
        xi, yi, ci = _place()
        me = 2 * xi + yi

        def local(a):
            return pltpu.make_async_copy(ins[a].at[me], outs[a].at[me], sems[2].at[a])

        def copy(a, r, src_slot, dst_slot):
            px, py = _chip_peer(xi, yi, r)
            k = 3 * a + r - 1
            return pltpu.make_async_remote_copy(
                src_ref=ins[a].at[src_slot], dst_ref=outs[a].at[dst_slot], send_sem=sems[0].at[k],
                recv_sem=sems[1].at[k], device_id=(px, py, ci), device_id_type=MESH)

        return xi, yi, me, local, copy

    def send(ins, outs, sems):
        xi, yi, me, local, copy = env(ins, outs, sems)
        for a in range(n):
            local(a).start()
            for r in (1, 2, 3):
                px, py = _chip_peer(xi, yi, r)
                copy(a, r, 2 * px + py, me).start()

    def finish(ins, outs, sems):
        xi, yi, me, local, copy = env(ins, outs, sems)
        for a in range(n):
            for r in (1, 2, 3):
                px, py = _chip_peer(xi, yi, r)
                copy(a, r, me, 2 * px + py).wait_recv()
        for a in range(n):
            for r in (1, 2, 3):
                px, py = _chip_peer(xi, yi, r)
                copy(a, r, 2 * px + py, me).wait_send()
            local(a).wait()

    dma = pltpu.SemaphoreType.DMA
    return Rider(ins=list(ps), out_shape=[SDS(p.shape, p.dtype) for p in ps],
                 sem_shapes=[dma((3 * n,)), dma((3 * n,)), dma((n,))], stages=[(0.0, send)], final=finish)


def exchange_rider(halves):
    n = len(halves)

    def env(ins, outs, sems):
        xi, yi, ci = _place()

        def local(a):
            return pltpu.make_async_copy(ins[a], outs[a].at[ci], sems[2].at[a])

        def copy(a, slot):
            return pltpu.make_async_remote_copy(
                src_ref=ins[a], dst_ref=outs[a].at[slot], send_sem=sems[0].at[a], recv_sem=sems[1].at[a],
                device_id=(xi, yi, 1 - ci), device_id_type=MESH)

        return ci, local, copy

    def send(ins, outs, sems):
        ci, local, copy = env(ins, outs, sems)
        for a in range(n):
            local(a).start()
            copy(a, ci).start()

    def finish(ins, outs, sems):
        ci, local, copy = env(ins, outs, sems)
        for a in range(n):
            copy(a, 1 - ci).wait_recv()
        for a in range(n):
            copy(a, ci).wait_send()
            local(a).wait()

    dma = pltpu.SemaphoreType.DMA
    return Rider(ins=list(halves), out_shape=[SDS((2,) + h.shape, h.dtype) for h in halves],
                 sem_shapes=[dma((n,)), dma((n,)), dma((n,))], stages=[(0.0, send)], final=finish)


def rs_pair_add(g, other, ci, name):
    _, nsh, r, w = g.shape
    tr = _rows_tile(r, w, 16)

    def body(c_ref, g_ref, o_ref, p_ref, pb_ref):
        p = g_ref[0] + o_ref[...]
        p_ref[...] = p
        pb_ref[...] = p.astype(bf16)

    blk = pl.BlockSpec((1, tr, w), lambda j, i, c: (j, i, 0))
    return pl.pallas_call(
        body, name=name,
        grid_spec=pltpu.PrefetchScalarGridSpec(
            num_scalar_prefetch=1, grid=(nsh, r // tr),
            in_specs=[pl.BlockSpec((1, 1, tr, w), lambda j, i, c: (c[0], j, i, 0)), blk], out_specs=[blk, blk]),
        out_shape=[SDS((nsh, r, w), f32), SDS((nsh, r, w), bf16)],
        compiler_params=_cp("parallel", "parallel"),
    )(ci.reshape(1).astype(jnp.int32), g, other)


def rs_chip_add(p32, parts, chip, name):
    nsh, r, w = parts.shape
    tr = _rows_tile(r, w, 16)

    def body(c_ref, p_ref, parts_ref, o_ref):
        me = c_ref[0]
        acc = None
        for j in range(N_CHIPS):
            term = jnp.where(me == j, p_ref[0], parts_ref[j].astype(f32))
            acc = term if acc is None else acc + term
        o_ref[...] = acc

    return pl.pallas_call(
        body, name=name,
        grid_spec=pltpu.PrefetchScalarGridSpec(
            num_scalar_prefetch=1, grid=(r // tr,),
            in_specs=[pl.BlockSpec((1, tr, w), lambda i, c: (c[0], i, 0)),
                      pl.BlockSpec((nsh, tr, w), lambda i, c: (0, i, 0))],
            out_specs=pl.BlockSpec((tr, w), lambda i, c: (i, 0))),
        out_shape=SDS((r, w), f32),
        compiler_params=_cp("parallel"),
    )(chip.reshape(1).astype(jnp.int32), p32, parts)


class GradReduction:
    def __init__(self, keys, grads, ci, chip):
        self.keys, self.grads, self.ci, self.chip = keys, grads, ci, chip

    def pair_rider(self):
        return pair_send_rider(self.grads)

    def chip_rider(self, from_pair):
        both = [rs_pair_add(g, o, self.ci, "rs_pair_add_" + k) for k, g, o in zip(self.keys, self.grads, from_pair)]
        self.p32 = [p for p, _ in both]
        return chip_send_rider([pb for _, pb in both])

    def exchange_rider(self, parts):
        return exchange_rider([rs_chip_add(p, q, self.chip, "rs_chip_add_" + k)
                               for k, p, q in zip(self.keys, self.p32, parts)])

    def alone(self):
        parts = run_rider(self.chip_rider(run_rider(self.pair_rider(), "rs_pair_" + self.keys[0])), "rs_chip_" + self.keys[0])
        return run_rider(self.exchange_rider(parts), "rs_swap_" + self.keys[0])


STAGE_W = DFF
_ST = {"norm_mix0": 0, "norm_mix1": 1, "attn_b_qkv": 2, "attn_sinks": 3, "attn_b_o": 4, "final_norm": 5, "loss": 6,
       "norm_ffn0": 8, "norm_ffn1": 9, "ffn_b_dw0": 10, "ffn_b_dw1": 11, "conv_b_pw1": 12, "conv_b_dw": 13,
       "conv_ln_g": 14, "conv_ln_b": 15, "conv_b_pw2": 16, "ffn_w_dw0": 17, "ffn_w_dw1": 20, "conv_w_dw": 24}
STAGE_ROWS = 56
SMALL_REP = ("norm_mix", "attn_b_qkv", "attn_sinks", "attn_b_o", "norm_ffn", "ffn_b_dw", "final_norm")
SMALL_SH = ("conv_b_pw1", "conv_w_dw", "conv_b_dw", "conv_ln_g", "conv_ln_b", "conv_b_pw2", "ffn_w_dw")
_SMALL_PARTS = ("norm_mix0", "norm_mix1", "attn_b_qkv", "attn_sinks", "attn_b_o", "norm_ffn0", "norm_ffn1", "ffn_b_dw0",
                "ffn_b_dw1", "final_norm", "conv_b_pw1", "conv_b_dw", "conv_ln_g", "conv_ln_b", "conv_b_pw2", "loss",
                "ffn_w_dw0", "ffn_w_dw1", "conv_w_dw")


def small_reduce_adamw(parts, w, m, v):
    names = SMALL_REP + SMALL_SH
    npart, nw = len(_SMALL_PARTS), len(names)

    def body(*refs):
        part = dict(zip(_SMALL_PARTS, refs[:npart]))
        off = npart
        w_ref = dict(zip(names, refs[off:off + nw]))
        m_ref = dict(zip(names, refs[off + nw:off + 2 * nw]))
        v_ref = dict(zip(names, refs[off + 2 * nw:off + 3 * nw]))
        off += 3 * nw
        loss_ref = refs[off]
        g_out = dict(zip(names, refs[off + 1:off + 1 + nw]))
        d_out = dict(zip(names, refs[off + 1 + nw:off + 1 + 2 * nw]))
        m_out = dict(zip(names, refs[off + 1 + 2 * nw:off + 1 + 3 * nw]))
        v_out = dict(zip(names, refs[off + 1 + 3 * nw:off + 1 + 4 * nw]))
        stage_ref, buf_ref, tot_ref, send_sems, recv_sems = refs[off + 1 + 4 * nw:]

        xi, yi, ci = _place()
        me = 4 * xi + 2 * yi + ci
        chip = 2 * xi + yi

        stage_ref[...] = jnp.zeros_like(stage_ref)
        for name in _SMALL_PARTS:
            ref, r0 = part[name], _ST[name]
            if name in ("attn_sinks", "loss"):
                val = ref[...]
            elif name in ("ffn_w_dw0", "ffn_w_dw1", "conv_w_dw"):
                val = jnp.sum(ref[...], axis=1)
            else:
                val = jnp.sum(ref[...], axis=0, keepdims=True)
            stage_ref[r0:r0 + val.shape[0], 0:val.shape[1]] = val

        buf_ref[me] = stage_ref[...]

        def peer(r):
            px, py = _chip_peer(xi, yi, r >> 1)
            return px, py, (1 - ci if r & 1 else ci)

        def copy(r, slot):
            return pltpu.make_async_remote_copy(
                src_ref=stage_ref, dst_ref=buf_ref.at[slot], send_sem=send_sems.at[r - 1], recv_sem=recv_sems.at[r - 1],
                device_id=peer(r), device_id_type=MESH)

        sends = []
        for r in range(1, N_DEV):
            cp = copy(r, me)
            cp.start()
            sends.append(cp)
        for r in range(1, N_DEV):
            px, py, pc = peer(r)
            copy(r, 4 * px + 2 * py + pc).wait_recv()
        for cp in sends:
            cp.wait_send()
        acc = buf_ref[0]
        for d in range(1, N_DEV):
            acc = acc + buf_ref[d]
        tot_ref[...] = acc

        def rows(name, n, width):
            r0 = _ST[name]
            return tot_ref[r0:r0 + n, 0:width]

        def mine(name, n, width):
            r0 = _ST[name]
            out = tot_ref[r0:r0 + n, 0:width]
            for j in range(1, N_CHIPS):
                out = jnp.where(chip == j, tot_ref[r0:r0 + n, j * width:(j + 1) * width], out)
            return out

        loss_ref[...] = rows("loss", 1, 1)
        grads = {
            "norm_mix": rows("norm_mix0", 2, D), "attn_b_qkv": rows("attn_b_qkv", 1, QKV),
            "attn_sinks": rows("attn_sinks", 1, N_HEADS), "attn_b_o": rows("attn_b_o", 1, D),
            "norm_ffn": rows("norm_ffn0", 2, D), "ffn_b_dw": rows("ffn_b_dw0", 2, DFF),
            "final_norm": rows("final_norm", 1, D),
            "conv_b_pw1": mine("conv_b_pw1", 1, 2 * D // N_CHIPS), "conv_w_dw": mine("conv_w_dw", CONV_W, D // N_CHIPS),
            "conv_b_dw": mine("conv_b_dw", 1, D // N_CHIPS), "conv_ln_g": mine("conv_ln_g", 1, D // N_CHIPS),
            "conv_ln_b": mine("conv_ln_b", 1, D // N_CHIPS), "conv_b_pw2": mine("conv_b_pw2", 1, D // N_CHIPS),
        }
        for name in names:
            if name == "ffn_w_dw":
                continue
            at = 0 if name == "conv_w_dw" else Ellipsis
            g = grads[name]
            d, mn, vn = _adam_math(w_ref[name][at], g, m_ref[name][at], v_ref[name][at])
            g_out[name][at] = g
            d_out[name][at] = d
            m_out[name][at] = mn
            v_out[name][at] = vn
        for layer, key in enumerate(("ffn_w_dw0", "ffn_w_dw1")):
            g = mine(key, 3, DFF // N_CHIPS)
            d, mn, vn = _adam_math(w_ref["ffn_w_dw"][layer], g, m_ref["ffn_w_dw"][layer], v_ref["ffn_w_dw"][layer])
            g_out["ffn_w_dw"][layer] = g
            d_out["ffn_w_dw"][layer] = d
            m_out["ffn_w_dw"][layer] = mn
            v_out["ffn_w_dw"][layer] = vn

    ins = [parts[k] for k in _SMALL_PARTS] + [src[k] for src in (w, m, v) for k in names]
    wshapes = [SDS(w[k].shape, f32) for k in names]
    outs = pl.pallas_call(
        body, name="small_reduce_adamw", in_specs=[_VMEM] * len(ins), out_specs=[_VMEM] * (1 + 4 * nw),
        out_shape=[SDS((1, 1), f32)] + wshapes * 4,
        scratch_shapes=[pltpu.VMEM((STAGE_ROWS, STAGE_W), f32), pltpu.VMEM((N_DEV, STAGE_ROWS, STAGE_W), f32),
                        pltpu.VMEM((STAGE_ROWS, STAGE_W), f32), pltpu.SemaphoreType.DMA((N_DEV - 1,)),
                        pltpu.SemaphoreType.DMA((N_DEV - 1,))],
        compiler_params=pltpu.CompilerParams(vmem_limit_bytes=VMEM_LIMIT),
    )(*ins)
    loss = outs[0]
    g, d, mn, vn = (dict(zip(names, outs[1 + k * nw:1 + (k + 1) * nw])) for k in range(4))
    return loss, g, d, mn, vn


TM = 512
TM_FFN = 256
FFN_CHUNK = 256
CONV_ROWS = 128
TK = 1024
FORWARD_AT = 0.6


def kernel(x, norm_mix, attn_w_qkv, attn_b_qkv, attn_sinks, attn_w_o, attn_b_o, conv_w_pw1, conv_b_pw1, conv_w_dw, conv_b_dw, conv_ln_g, conv_ln_b, conv_w_pw2, conv_b_pw2, norm_ffn, ffn_w_up, ffn_w_dw, ffn_b_dw, ffn_w_down, final_norm, loss_target, m_norm_mix, m_attn_w_qkv, m_attn_b_qkv, m_attn_sinks, m_attn_w_o, m_attn_b_o, m_conv_w_pw1, m_conv_b_pw1, m_conv_w_dw, m_conv_b_dw, m_conv_ln_g, m_conv_ln_b, m_conv_w_pw2, m_conv_b_pw2, m_norm_ffn, m_ffn_w_up, m_ffn_w_dw, m_ffn_b_dw, m_ffn_w_down, m_final_norm, v_norm_mix, v_attn_w_qkv, v_attn_b_qkv, v_attn_sinks, v_attn_w_o, v_attn_b_o, v_conv_w_pw1, v_conv_b_pw1, v_conv_w_dw, v_conv_b_dw, v_conv_ln_g, v_conv_ln_b, v_conv_w_pw2, v_conv_b_pw2, v_norm_ffn, v_ffn_w_up, v_ffn_w_dw, v_ffn_b_dw, v_ffn_w_down, v_final_norm):
    w = dict(norm_mix=norm_mix, attn_w_qkv=attn_w_qkv, attn_b_qkv=attn_b_qkv, attn_sinks=attn_sinks, attn_w_o=attn_w_o,
             attn_b_o=attn_b_o, conv_w_pw1=conv_w_pw1, conv_b_pw1=conv_b_pw1, conv_w_dw=conv_w_dw, conv_b_dw=conv_b_dw,
             conv_ln_g=conv_ln_g, conv_ln_b=conv_ln_b, conv_w_pw2=conv_w_pw2, conv_b_pw2=conv_b_pw2, norm_ffn=norm_ffn,
             ffn_w_up=ffn_w_up, ffn_w_dw=ffn_w_dw, ffn_b_dw=ffn_b_dw, ffn_w_down=ffn_w_down, final_norm=final_norm)
    mom = dict(norm_mix=m_norm_mix, attn_w_qkv=m_attn_w_qkv, attn_b_qkv=m_attn_b_qkv, attn_sinks=m_attn_sinks,
               attn_w_o=m_attn_w_o, attn_b_o=m_attn_b_o, conv_w_pw1=m_conv_w_pw1, conv_b_pw1=m_conv_b_pw1,
               conv_w_dw=m_conv_w_dw, conv_b_dw=m_conv_b_dw, conv_ln_g=m_conv_ln_g, conv_ln_b=m_conv_ln_b,
               conv_w_pw2=m_conv_w_pw2, conv_b_pw2=m_conv_b_pw2, norm_ffn=m_norm_ffn, ffn_w_up=m_ffn_w_up,
               ffn_w_dw=m_ffn_w_dw, ffn_b_dw=m_ffn_b_dw, ffn_w_down=m_ffn_w_down, final_norm=m_final_norm)
    vel = dict(norm_mix=v_norm_mix, attn_w_qkv=v_attn_w_qkv, attn_b_qkv=v_attn_b_qkv, attn_sinks=v_attn_sinks,
               attn_w_o=v_attn_w_o, attn_b_o=v_attn_b_o, conv_w_pw1=v_conv_w_pw1, conv_b_pw1=v_conv_b_pw1,
               conv_w_dw=v_conv_w_dw, conv_b_dw=v_conv_b_dw, conv_ln_g=v_conv_ln_g, conv_ln_b=v_conv_ln_b,
               conv_w_pw2=v_conv_w_pw2, conv_b_pw2=v_conv_b_pw2, norm_ffn=v_norm_ffn, ffn_w_up=v_ffn_w_up,
               ffn_w_dw=v_ffn_w_dw, ffn_b_dw=v_ffn_b_dw, ffn_w_down=v_ffn_w_down, final_norm=v_final_norm)
    order = ("norm_mix", "attn_w_qkv", "attn_b_qkv", "attn_sinks", "attn_w_o", "attn_b_o", "conv_w_pw1", "conv_b_pw1",
             "conv_w_dw", "conv_b_dw", "conv_ln_g", "conv_ln_b", "conv_w_pw2", "conv_b_pw2", "norm_ffn", "ffn_w_up",
             "ffn_w_dw", "ffn_b_dw", "ffn_w_down", "final_norm")
    xi, yi, ci = _place()
    chip = 2 * xi + yi
    xs, target = x[0], loss_target[0]
    s = xs.shape[0]
    tm, tmf, tk = min(TM, s), min(TM_FFN, s), min(TK, s)
    row = lambda v: v.reshape(1, -1)
    join = lambda a, axis: jnp.concatenate([a[j] for j in range(N_CHIPS)], axis=axis)
    cast = lambda a: a.astype(bf16)
    small, big = {}, {}

    got = run_rider(gather_rider(
        [(cast(attn_w_qkv[0]), _gv_qkv, (N_CHIPS, D, QKV // N_CHIPS)), (cast(attn_w_o[0]), _gv_rows, (D, D))],
        [w[k] for k in SMALL_SH], 0.0), "gather_attn")
    qkv4, w_o = got[:2]
    sm = dict(zip(SMALL_SH, got[2:]))
    w_qkv = jnp.transpose(qkv4, (1, 0, 2)).reshape(D, QKV)
    sinks = attn_sinks.reshape(N_HEADS)
    b_pw1, conv_dw, conv_bdw = join(sm["conv_b_pw1"], 1), join(sm["conv_w_dw"], 2)[0], join(sm["conv_b_dw"], 1)
    ln_g, ln_b, b_pw2, ffn_dw = (join(sm["conv_ln_g"], 1), join(sm["conv_ln_b"], 1), join(sm["conv_b_pw2"], 1),
                                 join(sm["ffn_w_dw"], 2))

    h0, qkv = qkv_fwd(xs, row(norm_mix[0]), w_qkv, attn_b_qkv, tm)
    (o, lse), (w_up0, w_dn0) = attn_fwd(qkv, sinks, rider=gather_rider(
        [(cast(ffn_w_up[0]), _gv_cols, (D, 2 * DFF)), (cast(ffn_w_down[0]), _gv_rows, (DFF, D))], [], FORWARD_AT))
    x1 = attn_out_fwd(xs, o, w_o, attn_b_o, tm)
    (x2, h1, up0, act0), (w_pw1, w_pw2, w_up1, w_dn1) = ffn_fwd(
        x1, row(norm_ffn[0]), w_up0, ffn_dw[0], row(ffn_b_dw[0]), w_dn0, tmf, FFN_CHUNK, rider=gather_rider(
            [(cast(conv_w_pw1[0]), _gv_cols, (D, 2 * D)), (cast(conv_w_pw2[0]), _gv_rows, (D, D)),
             (cast(ffn_w_up[1]), _gv_cols, (D, 2 * DFF)), (cast(ffn_w_down[1]), _gv_rows, (DFF, D))], [], FORWARD_AT))
    h2, a, u = pw1_fwd(x2, row(norm_mix[1]), w_pw1, b_pw1, tm)
    c, x3 = conv_fwd(u, x2, conv_dw, conv_bdw, ln_g, ln_b, w_pw2, b_pw2, tm, CONV_ROWS)
    x4, h3, up1, act1 = ffn_fwd(x3, row(norm_ffn[1]), w_up1, ffn_dw[1], row(ffn_b_dw[1]), w_dn1, tmf, FFN_CHUNK)

    dx4, small["final_norm"], small["loss"] = loss_bwd(x4, final_norm.reshape(1, D), target, tm)
    dx3, dup1, small["norm_ffn1"], small["ffn_w_dw1"], small["ffn_b_dw1"] = ffn_bwd(
        dx4, x3, row(norm_ffn[1]), up1, ffn_dw[1], row(ffn_b_dw[1]), w_dn1.T, w_up1.T, tmf, FFN_CHUNK)
    red1 = GradReduction(("up1", "down1"), [wgrad_cols(h3, dup1, tk, "wgrad_up1"),
                                           wgrad_rows(act1, dx4, 512, tk, "wgrad_down1")], ci, chip)
    (dc, u3, small["conv_ln_g"], small["conv_ln_b"], small["conv_b_pw2"], small["conv_b_dw"]), from_pair = conv_bwd_a(
        dx3, c, ln_g, ln_b, w_pw2.T, tm, rider=red1.pair_rider())
    g_pw2 = wgrad_rows(u3, dx3, 512, tk, "wgrad_pw2")
    (dx2, da, small["conv_w_dw"], small["conv_b_pw1"], small["norm_mix1"]), parts = conv_bwd_b(
        dc, u, a, conv_dw, x2, row(norm_mix[1]), w_pw1.T, dx3, tm, CONV_ROWS, rider=red1.chip_rider(from_pair))
    g_pw1, swapped = wgrad_cols(h2, da, tk, "wgrad_pw1", rider=red1.exchange_rider(parts))
    big.update(zip(red1.keys, swapped))

    red2 = GradReduction(("pw1", "pw2"), [g_pw1, g_pw2], ci, chip)
    (dx1, dup0, small["norm_ffn0"], small["ffn_w_dw0"], small["ffn_b_dw0"]), from_pair = ffn_bwd(
        dx2, x1, row(norm_ffn[0]), up0, ffn_dw[0], row(ffn_b_dw[0]), w_dn0.T, w_up0.T, tmf, FFN_CHUNK,
        rider=red2.pair_rider())
    g_up0, parts = wgrad_cols(h1, dup0, tk, "wgrad_up0", rider=red2.chip_rider(from_pair))
    g_dn0, swapped = wgrad_rows(act0, dx2, 512, tk, "wgrad_down0", rider=red2.exchange_rider(parts))
    big.update(zip(red2.keys, swapped))

    red3 = GradReduction(("up0", "down0"), [g_up0, g_dn0], ci, chip)
    (do, small["attn_b_o"]), from_pair = attn_out_bwd(dx1, w_o.T, tm, rider=red3.pair_rider())
    g_wo = wgrad_rows(o, dx1, 512, tk, "wgrad_o")
    (dq, dkv, small["attn_sinks"]), parts = attn_bwd(qkv, o, do, lse, sinks, rider=red3.chip_rider(from_pair))
    dx0, small["norm_mix0"], small["attn_b_qkv"], dkvb = qkv_bwd(dq, dkv, xs, row(norm_mix[0]), w_qkv.T, dx1, tm)
    g_q, swapped = wgrad(h0, dq, 512, tk, "wgrad_q", rider=red3.exchange_rider(parts))
    big.update(zip(red3.keys, swapped))
    g_qkv = jnp.concatenate([g_q, wgrad(h0, dkvb, 2 * N_KV * HD, tk, "wgrad_kv")], axis=1)
    g_qkv = jnp.transpose(g_qkv.reshape(2, D // 2, N_CHIPS, QKV // N_CHIPS), (0, 2, 1, 3))
    red4 = GradReduction(("qkv", "wo"), [g_qkv, g_wo], ci, chip)
    big.update(zip(red4.keys, red4.alone()))

    gbig = {
        "attn_w_qkv": big["qkv"].reshape(1, D, QKV // N_CHIPS), "attn_w_o": big["wo"].reshape(1, D // N_CHIPS, D),
        "conv_w_pw1": big["pw1"].reshape(1, D, 2 * D // N_CHIPS), "conv_w_pw2": big["pw2"].reshape(1, D // N_CHIPS, D),
        "ffn_w_up": jnp.stack([big["up0"], big["up1"]]).reshape(2, D, 2 * DFF // N_CHIPS),
        "ffn_w_down": jnp.stack([big["down0"], big["down1"]]).reshape(2, DFF // N_CHIPS, D),
    }

    fix = lambda d: {**d, "final_norm": d["final_norm"].reshape(1, D)}
    loss, gs, ds, ms, vs = small_reduce_adamw(small, fix(w), fix(mom), fix(vel))
    unfix = lambda d: {**d, "final_norm": d["final_norm"].reshape(D)}
    gout, delta, new_m, new_v = unfix(gs), unfix(ds), unfix(ms), unfix(vs)

    for name, g in gbig.items():
        gout[name] = g
        delta[name], new_m[name], new_v[name] = adamw(w[name], g, mom[name], vel[name], "adamw_" + name)

    return (loss.reshape(()), dx0[None], *[gout[n] for n in order], *[delta[n] for n in order],
            *[new_m[n] for n in order], *[new_v[n] for n in order])
```

```python
import math

import jax
import jax.numpy as jnp
from jax import lax
from jax.experimental import pallas as pl
from jax.experimental.pallas import tpu as pltpu

f32 = jnp.float32
bf16 = jnp.bfloat16
SDS = jax.ShapeDtypeStruct
MESH = pl.DeviceIdType.MESH

D = 1024
N_HEADS = 16
N_KV = 2
GROUP = 8
HD = 64
BLK = 128
QKV = (N_HEADS + 2 * N_KV) * HD
KV_COL_BLOCK = (N_HEADS * HD) // (2 * N_KV * HD)
CONV_W = 31
CONV_HALO = 32
DFF = 2816
RMS_EPS = 1e-6
LN_EPS = 1e-5
LR, B1, B2, ADAM_EPS, WD, STEP = 0.001, 0.9, 0.999, 1e-08, 0.01, 10

N_CHIPS = 4
N_DEV = 8
VMEM_LIMIT = 56 * 1024 * 1024
LANES = 128
SUB = 8
ELEMENTWISE_BLOCK_BYTES = 1 << 20


def _cp(*sem):
    return pltpu.CompilerParams(dimension_semantics=sem, vmem_limit_bytes=VMEM_LIMIT)


def _row(tm, n):
    return pl.BlockSpec((tm, n), lambda i: (i, 0))


def _const(shape):
    return pl.BlockSpec(shape, lambda *_: (0,) * len(shape), pipeline_mode=pl.Buffered(1))


def _acc(shape):
    return pl.BlockSpec(shape, lambda *_: (0,) * len(shape))


def _rms_fwd(x, g):
    r = lax.rsqrt(jnp.mean(x * x, axis=-1, keepdims=True) + RMS_EPS)
    xn = x * r
    return xn * g, xn, r


def _colsum8(v):
    return jnp.sum(v.reshape(v.shape[0] // SUB, SUB, v.shape[1]), axis=0)


def _rms_bwd(xn, r, g, dh):
    dyn = dh * g
    dx = r * (dyn - xn * jnp.mean(dyn * xn, axis=-1, keepdims=True))
    return dx, _colsum8(dh * xn)


def _sigmoid(z):
    return 0.5 * jnp.tanh(0.5 * z) + 0.5


def _dsilu(z, sg):
    return sg * (1.0 + z * (1.0 - sg))


def _dot(a, b):
    return jnp.dot(a, b, preferred_element_type=f32)


_ANY = pl.BlockSpec(memory_space=pl.ANY)
_VMEM = pl.BlockSpec(memory_space=pltpu.VMEM)


class Rider:
    def __init__(self, ins, out_shape, sem_shapes, stages, final):
        self.ins, self.out_shape, self.sem_shapes, self.stages, self.final = ins, out_shape, sem_shapes, stages, final


def run_rider(rider, name):
    n_in, n_out = len(rider.ins), len(rider.out_shape)

    def body(*refs):
        parts = refs[:n_in], refs[n_in:n_in + n_out], refs[n_in + n_out:]
        for _, fn in rider.stages:
            fn(*parts)
        rider.final(*parts)

    return pl.pallas_call(
        body, name=name, in_specs=[_ANY] * n_in, out_specs=[_ANY] * n_out, out_shape=list(rider.out_shape),
        scratch_shapes=list(rider.sem_shapes),
    )(*rider.ins)


def _hosted(rider, body, *, grid, in_specs, out_specs, out_shape, name, compiler_params, scratch_shapes=()):
    if rider is None:
        return pl.pallas_call(body, grid=grid, in_specs=in_specs, out_specs=out_specs, out_shape=out_shape, name=name,
                              compiler_params=compiler_params, scratch_shapes=list(scratch_shapes))
    single = not isinstance(out_shape, (list, tuple))
    shapes = [out_shape] if single else list(out_shape)
    specs = [out_specs] if single else list(out_specs)
    n_in, n_out, n_sc = len(in_specs), len(shapes), len(scratch_shapes)
    r_in, r_out = len(rider.ins), len(rider.out_shape)
    total = math.prod(grid)

    def wrapped(*refs):
        own_in, refs = refs[:n_in], refs[n_in:]
        r_ins, refs = refs[:r_in], refs[r_in:]
        own_out, refs = refs[:n_out], refs[n_out:]
        r_outs, refs = refs[:r_out], refs[r_out:]
        own_sc, r_sems = refs[:n_sc], refs[n_sc:]
        step = 0
        for d, n in enumerate(grid):
            step = step * n + pl.program_id(d)
        for frac, fn in rider.stages:
            @pl.when(step == min(int(frac * total), total - 1))
            def _(fn=fn):
                fn(r_ins, r_outs, r_sems)

        body(*own_in, *own_out, *own_sc)

        @pl.when(step == total - 1)
        def _():
            rider.final(r_ins, r_outs, r_sems)

    call = pl.pallas_call(
        wrapped, grid=grid, in_specs=list(in_specs) + [_ANY] * r_in, out_specs=specs + [_ANY] * r_out,
        out_shape=shapes + list(rider.out_shape), scratch_shapes=list(scratch_shapes) + list(rider.sem_shapes),
        name=name, compiler_params=_cp(*(("arbitrary",) * len(grid))))

    def run(*args):
        res = call(*args, *rider.ins)
        own = res[:n_out]
        return (own[0] if single else own), res[n_out:]

    return run


def qkv_fwd(x, g, w, b, tm):
    s = x.shape[0]

    def body(x_ref, g_ref, w_ref, b_ref, h_ref, o_ref):
        h, _, _ = _rms_fwd(x_ref[...], g_ref[...])
        hb = h.astype(bf16)
        h_ref[...] = hb
        o_ref[...] = (_dot(hb, w_ref[...]) + b_ref[...]).astype(bf16)

    return pl.pallas_call(
        body, grid=(s // tm,), name="qkv_fwd",
        in_specs=[_row(tm, D), _const((1, D)), _const((D, QKV)), _const((1, QKV))],
        out_specs=[_row(tm, D), _row(tm, QKV)],
        out_shape=[SDS((s, D), bf16), SDS((s, QKV), bf16)],
        compiler_params=_cp("parallel"),
    )(x, g, w, b)


def _band_mask(i):
    qi = lax.broadcasted_iota(jnp.int32, (GROUP * BLK, 2 * BLK), 0) & (BLK - 1)
    ki = lax.broadcasted_iota(jnp.int32, (GROUP * BLK, 2 * BLK), 1)
    dist = qi + BLK - ki
    return (dist >= 0) & (dist < BLK) & ((ki >= BLK) | (i > 0))


_NEG = float(jnp.finfo(jnp.float32).min)
_NT = (((1,), (1,)), ((), ()))
_TN = (((0,), (0,)), ((), ()))


def _kv_heads(kvp_ref, kvc_ref, kvh):
    ks = slice(kvh * HD, (kvh + 1) * HD)
    vs = slice(N_KV * HD + kvh * HD, N_KV * HD + (kvh + 1) * HD)
    k = jnp.concatenate([kvp_ref[:, ks], kvc_ref[:, ks]], axis=0)
    v = jnp.concatenate([kvp_ref[:, vs], kvc_ref[:, vs]], axis=0)
    return k, v


def _stack_group(ref, kvh, width=HD):
    return jnp.concatenate([ref[:, (kvh * GROUP + gi) * width:(kvh * GROUP + gi + 1) * width] for gi in range(GROUP)],
                           axis=0)


def _group_sinks(sink_ref, kvh):
    row = lax.broadcasted_iota(jnp.int32, (GROUP * BLK, 1), 0)
    col = jnp.zeros((GROUP * BLK, 1), f32)
    for g in range(GROUP):
        col = jnp.where((row >= g * BLK) & (row < (g + 1) * BLK), sink_ref[kvh * GROUP + g], col)
    return col


def attn_fwd(qkv, sinks, rider=None):
    s = qkv.shape[0]
    scale = 1.0 / math.sqrt(HD)

    def body(q_ref, kvc_ref, kvp_ref, sink_ref, o_ref, lse_ref):
        valid = _band_mask(pl.program_id(0))
        for kvh in range(N_KV):
            k, v = _kv_heads(kvp_ref, kvc_ref, kvh)
            sc = lax.dot_general(_stack_group(q_ref, kvh), k, _NT, preferred_element_type=f32) * scale
            sc = jnp.where(valid, sc, _NEG)
            sink = _group_sinks(sink_ref, kvh)
            m = jnp.maximum(jnp.max(sc, axis=-1, keepdims=True), sink)
            p = jnp.exp(sc - m)
            denom = jnp.sum(p, axis=-1, keepdims=True) + jnp.exp(sink - m)
            og = _dot((p / denom).astype(bf16), v).astype(bf16)
            lse = m + jnp.log(denom)
            for gi in range(GROUP):
                h = kvh * GROUP + gi
                o_ref[:, h * HD:(h + 1) * HD] = og[gi * BLK:(gi + 1) * BLK]
                lse_ref[:, h:h + 1] = lse[gi * BLK:(gi + 1) * BLK]

    return _hosted(
        rider, body, grid=(s // BLK,), name="attn_fwd",
        in_specs=[
            pl.BlockSpec((BLK, N_HEADS * HD), lambda i: (i, 0)),
            pl.BlockSpec((BLK, 2 * N_KV * HD), lambda i: (i, KV_COL_BLOCK)),
            pl.BlockSpec((BLK, 2 * N_KV * HD), lambda i: (jnp.maximum(i - 1, 0), KV_COL_BLOCK)),
            pl.BlockSpec(memory_space=pltpu.SMEM),
        ],
        out_specs=[_row(BLK, D), _row(BLK, N_HEADS)],
        out_shape=[SDS((s, D), bf16), SDS((s, N_HEADS), f32)],
        compiler_params=_cp("parallel"),
    )(qkv, qkv, qkv, sinks)


def attn_out_fwd(x, o, w, b, tm):
    s = x.shape[0]

    def body(x_ref, o_ref, w_ref, b_ref, y_ref):
        y_ref[...] = x_ref[...] + _dot(o_ref[...], w_ref[...]) + b_ref[...]

    return pl.pallas_call(
        body, grid=(s // tm,), name="attn_out_fwd",
        in_specs=[_row(tm, D), _row(tm, D), _const((D, D)), _const((1, D))],
        out_specs=_row(tm, D), out_shape=SDS((s, D), f32),
        compiler_params=_cp("parallel"),
    )(x, o, w, b)


def attn_out_bwd(dy, wt, tm, rider=None):
    s = dy.shape[0]

    def body(dy_ref, wt_ref, do_ref, db_ref):
        @pl.when(pl.program_id(0) == 0)
        def _():
            db_ref[...] = jnp.zeros_like(db_ref)

        dy = dy_ref[...]
        do_ref[...] = _dot(dy.astype(bf16), wt_ref[...]).astype(bf16)
        db_ref[...] += _colsum8(dy)

    return _hosted(
        rider, body, grid=(s // tm,), name="attn_out_bwd",
        in_specs=[_row(tm, D), _const((D, D))],
        out_specs=[_row(tm, D), _acc((SUB, D))],
        out_shape=[SDS((s, D), bf16), SDS((SUB, D), f32)],
        compiler_params=_cp("arbitrary"),
    )(dy, wt)


def attn_bwd(qkv, o, do, lse, sinks, rider=None):
    s = qkv.shape[0]
    nb = s // BLK
    scale = 1.0 / math.sqrt(HD)
    kvw = 2 * N_KV * HD

    def body(q_ref, kvc_ref, kvp_ref, o_ref, do_ref, lse_ref, sink_ref, dq_ref, dkv_ref, ds_ref, carry_ref):
        i = pl.program_id(0)

        @pl.when(i == 0)
        def _():
            ds_ref[...] = jnp.zeros_like(ds_ref)
            carry_ref[...] = jnp.zeros_like(carry_ref)

        @pl.when(i < nb)
        def _():
            valid = _band_mask(i)
            for kvh in range(N_KV):
                k, v = _kv_heads(kvp_ref, kvc_ref, kvh)
                qg = _stack_group(q_ref, kvh)
                dog = _stack_group(do_ref, kvh)
                lse = _stack_group(lse_ref, kvh, 1)
                sc = lax.dot_general(qg, k, _NT, preferred_element_type=f32) * scale
                sc = jnp.where(valid, sc, _NEG)
                p = jnp.exp(sc - lse)
                dp = lax.dot_general(dog, v, _NT, preferred_element_type=f32)
                dlt = jnp.sum(dog.astype(f32) * _stack_group(o_ref, kvh).astype(f32), axis=-1, keepdims=True)
                dsc = (p * (dp - dlt)).astype(bf16)
                dqg = (_dot(dsc, k) * scale).astype(bf16)
                dk = lax.dot_general(dsc, qg, _TN, preferred_element_type=f32) * scale
                dv = lax.dot_general(p.astype(bf16), dog, _TN, preferred_element_type=f32)
                dsink = jnp.exp(_group_sinks(sink_ref, kvh) - lse) * dlt
                for gi in range(GROUP):
                    h = kvh * GROUP + gi
                    dq_ref[:, h * HD:(h + 1) * HD] = dqg[gi * BLK:(gi + 1) * BLK]
                    ds_ref[:, h:h + 1] += -jnp.sum(dsink[gi * BLK:(gi + 1) * BLK], axis=0, keepdims=True)
                ks = slice(kvh * HD, (kvh + 1) * HD)
                vs = slice(N_KV * HD + kvh * HD, N_KV * HD + (kvh + 1) * HD)
                dkv_ref[:, ks] = carry_ref[:, ks] + dk[:BLK]
                dkv_ref[:, vs] = carry_ref[:, vs] + dv[:BLK]
                carry_ref[:, ks] = dk[BLK:]
                carry_ref[:, vs] = dv[BLK:]

        @pl.when(i == nb)
        def _():
            dkv_ref[...] = carry_ref[...]

    cur = lambda i: (jnp.minimum(i, nb - 1), 0)
    prev = lambda i: (jnp.clip(i - 1, 0, nb - 1), KV_COL_BLOCK)
    return _hosted(
        rider, body, grid=(nb + 1,), name="attn_bwd",
        in_specs=[
            pl.BlockSpec((BLK, D), cur),
            pl.BlockSpec((BLK, kvw), lambda i: (jnp.minimum(i, nb - 1), KV_COL_BLOCK)),
            pl.BlockSpec((BLK, kvw), prev),
            pl.BlockSpec((BLK, D), cur),
            pl.BlockSpec((BLK, D), cur),
            pl.BlockSpec((BLK, N_HEADS), cur),
            pl.BlockSpec(memory_space=pltpu.SMEM),
        ],
        out_specs=[
            pl.BlockSpec((BLK, D), cur),
            pl.BlockSpec((BLK, kvw), lambda i: (jnp.maximum(i - 1, 0), 0)),
            _acc((1, N_HEADS)),
        ],
        out_shape=[SDS((s, D), bf16), SDS((s, kvw), f32), SDS((1, N_HEADS), f32)],
        scratch_shapes=[pltpu.VMEM((BLK, kvw), f32)],
        compiler_params=_cp("arbitrary"),
    )(qkv, qkv, qkv, o, do, lse, sinks)


def qkv_bwd(dq, dkv, x, g, wt, dres, tm):
    s = x.shape[0]
    qd = N_HEADS * HD
    kvw = 2 * N_KV * HD

    def body(dq_ref, dkv_ref, x_ref, g_ref, wt_ref, dres_ref, dx_ref, dg_ref, db_ref, dkvb_ref):
        @pl.when(pl.program_id(0) == 0)
        def _():
            dg_ref[...] = jnp.zeros_like(dg_ref)
            db_ref[...] = jnp.zeros_like(db_ref)

        dq = dq_ref[...]
        dkv = dkv_ref[...]
        dkvb = dkv.astype(bf16)
        dkvb_ref[...] = dkvb
        dh = _dot(dq, wt_ref[0:qd, :]) + _dot(dkvb, wt_ref[qd:QKV, :])
        g = g_ref[...]
        _, xn, r = _rms_fwd(x_ref[...], g)
        dx, dg = _rms_bwd(xn, r, g, dh)
        dx_ref[...] = dres_ref[...] + dx
        dg_ref[...] += dg
        db_ref[:, 0:qd] += _colsum8(dq.astype(f32))
        db_ref[:, qd:QKV] += _colsum8(dkv)

    return pl.pallas_call(
        body, grid=(s // tm,), name="qkv_bwd",
        in_specs=[_row(tm, qd), _row(tm, kvw), _row(tm, D), _const((1, D)), _const((QKV, D)), _row(tm, D)],
        out_specs=[_row(tm, D), _acc((SUB, D)), _acc((SUB, QKV)), _row(tm, kvw)],
        out_shape=[SDS((s, D), f32), SDS((SUB, D), f32), SDS((SUB, QKV), f32), SDS((s, kvw), bf16)],
        compiler_params=_cp("arbitrary"),
    )(dq, dkv, x, g, wt, dres)


def ffn_fwd(x, g, wup, wdw, bdw, wdn, tm, cw, rider=None):
    s = x.shape[0]
    tail = 8

    def body(x_ref, g_ref, wup_ref, wdw_ref, bdw_ref, wdn_ref, xo_ref, h_ref, up_ref, gate_ref, act_ref, carry_ref,
             ext_ref):
        @pl.when(pl.program_id(0) == 0)
        def _():
            carry_ref[...] = jnp.zeros_like(carry_ref)

        x = x_ref[...]
        h, _, _ = _rms_fwd(x, g_ref[...])
        hb = h.astype(bf16)
        h_ref[...] = hb
        for c in range(DFF // cw):
            cs = slice(c * cw, (c + 1) * cw)
            vs = slice(DFF + c * cw, DFF + (c + 1) * cw)
            ug = _dot(hb, wup_ref[:, cs])
            uv = _dot(hb, wup_ref[:, vs])
            up_ref[:, cs] = ug.astype(bf16)
            up_ref[:, vs] = uv.astype(bf16)
            ext_ref[0:tail, :] = carry_ref[:, cs]
            ext_ref[tail:tail + tm, :] = ug
            carry_ref[:, cs] = ug[tm - tail:, :]
            gate = (wdw_ref[0:1, cs] * ext_ref[pl.ds(tail - 2, tm), :]
                    + wdw_ref[1:2, cs] * ext_ref[pl.ds(tail - 1, tm), :]
                    + wdw_ref[2:3, cs] * ug) + bdw_ref[:, cs]
            gate_ref[:, cs] = gate.astype(bf16)
            act_ref[:, cs] = (gate * _sigmoid(gate) * uv).astype(bf16)
        xo_ref[...] = x + _dot(act_ref[...], wdn_ref[...])

    return _hosted(
        rider, body, grid=(s // tm,), name="ffn_fwd",
        in_specs=[_row(tm, D), _const((1, D)), _const((D, 2 * DFF)), _const((3, DFF)), _const((1, DFF)),
                  _const((DFF, D))],
        out_specs=[_row(tm, D), _row(tm, D), _row(tm, 2 * DFF), _row(tm, DFF), _row(tm, DFF)],
        out_shape=[SDS((s, D), f32), SDS((s, D), bf16), SDS((s, 2 * DFF), bf16), SDS((s, DFF), bf16),
                   SDS((s, DFF), bf16)],
        scratch_shapes=[pltpu.VMEM((tail, DFF), f32), pltpu.VMEM((tail + tm, cw), f32)],
        compiler_params=_cp("arbitrary"),
    )(x, g, wup, wdw, bdw, wdn)


def ffn_bwd(dxo, x, g, up, gate, wdw, wdnt, wupt, tm, cw, rider=None):
    s = x.shape[0]
    nt = s // tm
    rev = lambda i: (nt - 1 - i, 0)

    def body(dxo_ref, x_ref, g_ref, up_ref, gate_ref, wdw_ref, wdnt_ref, wupt_ref,
             dxi_ref, dup_ref, dg_ref, dwdw_ref, dbdw_ref, carry_ref, ext2_ref):
        i = pl.program_id(0)

        @pl.when(i == 0)
        def _():
            carry_ref[...] = jnp.zeros_like(carry_ref)
            dg_ref[...] = jnp.zeros_like(dg_ref)
            dwdw_ref[...] = jnp.zeros_like(dwdw_ref)
            dbdw_ref[...] = jnp.zeros_like(dbdw_ref)

        dxo = dxo_ref[...]
        dxb = dxo.astype(bf16)
        for c in range(DFF // cw):
            cs = slice(c * cw, (c + 1) * cw)
            vs = slice(DFF + c * cw, DFF + (c + 1) * cw)
            d_act = _dot(dxb, wdnt_ref[:, cs])
            ug = up_ref[:, cs].astype(f32)
            uv = up_ref[:, vs].astype(f32)
            gate = gate_ref[:, cs].astype(f32)
            sg = _sigmoid(gate)
            dup_ref[:, vs] = (d_act * (gate * sg)).astype(bf16)
            d_gate = d_act * uv * _dsilu(gate, sg)
            ext2_ref[0:tm, :] = d_gate
            ext2_ref[tm:tm + 8, :] = carry_ref[:, cs]
            carry_ref[:, cs] = d_gate[0:8, :]
            ahead1 = ext2_ref[pl.ds(1, tm), :]
            ahead2 = ext2_ref[pl.ds(2, tm), :]
            dbdw_ref[:, cs] += _colsum8(d_gate)
            dwdw_ref[0, :, cs] += _colsum8(ahead2 * ug)
            dwdw_ref[1, :, cs] += _colsum8(ahead1 * ug)
            dwdw_ref[2, :, cs] += _colsum8(d_gate * ug)
            d_ug = wdw_ref[0:1, cs] * ahead2 + wdw_ref[1:2, cs] * ahead1 + wdw_ref[2:3, cs] * d_gate
            dup_ref[:, cs] = d_ug.astype(bf16)
        dh = _dot(dup_ref[...], wupt_ref[...])
        gv = g_ref[...]
        _, xn, r = _rms_fwd(x_ref[...], gv)
        dx, dg = _rms_bwd(xn, r, gv, dh)
        dxi_ref[...] = dxo + dx
        dg_ref[...] += dg

    return _hosted(
        rider, body, grid=(nt,), name="ffn_bwd",
        in_specs=[
            pl.BlockSpec((tm, D), rev), pl.BlockSpec((tm, D), rev), _const((1, D)),
            pl.BlockSpec((tm, 2 * DFF), rev), pl.BlockSpec((tm, DFF), rev),
            _const((3, DFF)), _const((D, DFF)), _const((2 * DFF, D)),
        ],
        out_specs=[pl.BlockSpec((tm, D), rev), pl.BlockSpec((tm, 2 * DFF), rev), _acc((SUB, D)),
                   _acc((3, SUB, DFF)), _acc((SUB, DFF))],
        out_shape=[SDS((s, D), f32), SDS((s, 2 * DFF), bf16), SDS((SUB, D), f32), SDS((3, SUB, DFF), f32),
                   SDS((SUB, DFF), f32)],
        scratch_shapes=[pltpu.VMEM((8, DFF), f32), pltpu.VMEM((tm + 8, cw), f32)],
        compiler_params=_cp("arbitrary"),
    )(dxo, x, g, up, gate, wdw, wdnt, wupt)


def pw1_fwd(x, g, w, b, tm):
    s = x.shape[0]

    def body(x_ref, g_ref, w_ref, b_ref, h_ref, a_ref, u_ref):
        h, _, _ = _rms_fwd(x_ref[...], g_ref[...])
        hb = h.astype(bf16)
        h_ref[...] = hb
        a = _dot(hb, w_ref[...]) + b_ref[...]
        a_ref[...] = a.astype(bf16)
        u_ref[...] = a[:, :D] * _sigmoid(a[:, D:])

    return pl.pallas_call(
        body, grid=(s // tm,), name="pw1_fwd",
        in_specs=[_row(tm, D), _const((1, D)), _const((D, 2 * D)), _const((1, 2 * D))],
        out_specs=[_row(tm, D), _row(tm, 2 * D), _row(tm, D)],
        out_shape=[SDS((s, D), bf16), SDS((s, 2 * D), bf16), SDS((s, D), f32)],
        compiler_params=_cp("parallel"),
    )(x, g, w, b)


def _ln_silu(c, lg, lb):
    mu = jnp.mean(c, axis=-1, keepdims=True)
    cc = c - mu
    var = jnp.mean(cc * cc, axis=-1, keepdims=True)
    rstd = lax.rsqrt(var + LN_EPS)
    xh = cc * rstd
    ln = xh * lg + lb
    sg = _sigmoid(ln)
    return xh, rstd, ln, sg


def _shifted_copies(ext_ref, sh_ref, cs, tm):
    n = CONV_HALO - SUB + tm
    for k in range(1, SUB):
        sh_ref[k - 1] = ext_ref[pl.ds(k, n), cs]


def _shifted_rows(ext_ref, sh_ref, cs, start, rows):
    q, k = divmod(start, SUB)
    if k == 0:
        return ext_ref[pl.ds(start, rows), cs]
    return sh_ref[k - 1, pl.ds(q * SUB, rows), :]


def conv_fwd(u, x, wdw, bdw, lg, lb, w2, b2, tm, rc):
    s = x.shape[0]
    hl = CONV_HALO
    off = hl - (CONV_W - 1)

    def body(u_ref, halo_ref, x_ref, wdw_ref, bdw_ref, lg_ref, lb_ref, w2_ref, b2_ref, c_ref, xo_ref, ext_ref, sh_ref):
        has_prev = (pl.program_id(0) > 0).astype(f32)
        ext_ref[0:hl, :] = halo_ref[...] * has_prev
        ext_ref[hl:hl + tm, :] = u_ref[...]
        for cc in range(D // LANES):
            cs = slice(cc * LANES, (cc + 1) * LANES)
            _shifted_copies(ext_ref, sh_ref, cs, tm)
            for rr in range(tm // rc):
                acc = jnp.zeros((rc, LANES), f32) + bdw_ref[:, cs]
                for j in range(CONV_W):
                    acc = acc + wdw_ref[j:j + 1, cs] * _shifted_rows(ext_ref, sh_ref, cs, rr * rc + off + j, rc)
                c_ref[rr * rc:(rr + 1) * rc, cs] = acc
        _, _, ln, sg = _ln_silu(c_ref[...], lg_ref[...], lb_ref[...])
        xo_ref[...] = x_ref[...] + _dot((ln * sg).astype(bf16), w2_ref[...]) + b2_ref[...]

    return pl.pallas_call(
        body, grid=(s // tm,), name="conv_fwd",
        in_specs=[_row(tm, D), pl.BlockSpec((hl, D), lambda i: (jnp.maximum(i * (tm // hl) - 1, 0), 0)), _row(tm, D),
                  _const((CONV_W, D)), _const((1, D)), _const((1, D)), _const((1, D)), _const((D, D)), _const((1, D))],
        out_specs=[_row(tm, D), _row(tm, D)],
        out_shape=[SDS((s, D), f32), SDS((s, D), f32)],
        scratch_shapes=[pltpu.VMEM((hl + tm, D), f32), pltpu.VMEM((SUB - 1, hl - SUB + tm, LANES), f32)],
        compiler_params=_cp("parallel"),
    )(u, u, x, wdw, bdw, lg, lb, w2, b2)


def conv_bwd_a(dy, c, lg, lb, w2t, tm, rider=None):
    s = dy.shape[0]

    def body(dy_ref, c_ref, lg_ref, lb_ref, w2t_ref, dc_ref, u3_ref, dlg_ref, dlb_ref, db2_ref, dbdw_ref):
        @pl.when(pl.program_id(0) == 0)
        def _():
            for r in (dlg_ref, dlb_ref, db2_ref, dbdw_ref):
                r[...] = jnp.zeros_like(r)

        dy = dy_ref[...]
        lg = lg_ref[...]
        xh, rstd, ln, sg = _ln_silu(c_ref[...], lg, lb_ref[...])
        u3_ref[...] = (ln * sg).astype(bf16)
        du3 = _dot(dy.astype(bf16), w2t_ref[...])
        dln = du3 * _dsilu(ln, sg)
        dxh = dln * lg
        dc = rstd * (dxh - jnp.mean(dxh, axis=-1, keepdims=True) - xh * jnp.mean(dxh * xh, axis=-1, keepdims=True))
        dc_ref[...] = dc
        dlg_ref[...] += _colsum8(dln * xh)
        dlb_ref[...] += _colsum8(dln)
        db2_ref[...] += _colsum8(dy)
        dbdw_ref[...] += _colsum8(dc)

    return _hosted(
        rider, body, grid=(s // tm,), name="conv_bwd_a",
        in_specs=[_row(tm, D), _row(tm, D), _const((1, D)), _const((1, D)), _const((D, D))],
        out_specs=[_row(tm, D), _row(tm, D)] + [_acc((SUB, D))] * 4,
        out_shape=[SDS((s, D), f32), SDS((s, D), bf16)] + [SDS((SUB, D), f32)] * 4,
        compiler_params=_cp("arbitrary"),
    )(dy, c, lg, lb, w2t)


def conv_bwd_b(dc, u, a, wdw, x, g, w1t, dres, tm, rc, rider=None):
    s = x.shape[0]
    nt = s // tm
    hl = CONV_HALO
    off = hl - (CONV_W - 1)

    def body(dc_ref, dnext_ref, u_ref, uprev_ref, a_ref, wdw_ref, x_ref, g_ref, w1t_ref, dres_ref,
             dx_ref, da_ref, dwdw_ref, db1_ref, dg_ref, ext_ref, ext2_ref, du_ref, sh_ref, sh2_ref):
        i = pl.program_id(0)

        @pl.when(i == 0)
        def _():
            for r in (dwdw_ref, db1_ref, dg_ref):
                r[...] = jnp.zeros_like(r)

        ext_ref[0:hl, :] = uprev_ref[...] * (i > 0).astype(f32)
        ext_ref[hl:hl + tm, :] = u_ref[...]
        ext2_ref[0:tm, :] = dc_ref[...]
        ext2_ref[tm:tm + hl, :] = dnext_ref[...] * (i < nt - 1).astype(f32)
        for cc in range(D // LANES):
            cs = slice(cc * LANES, (cc + 1) * LANES)
            _shifted_copies(ext_ref, sh_ref, cs, tm)
            _shifted_copies(ext2_ref, sh2_ref, cs, tm)
            for rr in range(tm // rc):
                r0 = rr * rc
                dcb = ext2_ref[r0:r0 + rc, cs]
                acc = jnp.zeros((rc, LANES), f32)
                for j in range(CONV_W):
                    acc = acc + wdw_ref[j:j + 1, cs] * _shifted_rows(ext2_ref, sh2_ref, cs, r0 + CONV_W - 1 - j, rc)
                    dwdw_ref[j, :, cs] += _colsum8(dcb * _shifted_rows(ext_ref, sh_ref, cs, r0 + off + j, rc))
                du_ref[r0:r0 + rc, cs] = acc
        du = du_ref[...]
        a1 = a_ref[:, :D].astype(f32)
        sg = _sigmoid(a_ref[:, D:].astype(f32))
        da1 = du * sg
        da2 = du * a1 * sg * (1.0 - sg)
        da_ref[:, :D] = da1.astype(bf16)
        da_ref[:, D:] = da2.astype(bf16)
        db1_ref[:, :D] += _colsum8(da1)
        db1_ref[:, D:] += _colsum8(da2)
        dh = _dot(da_ref[...], w1t_ref[...])
        gv = g_ref[...]
        _, xn, r = _rms_fwd(x_ref[...], gv)
        dx, dg = _rms_bwd(xn, r, gv, dh)
        dx_ref[...] = dres_ref[...] + dx
        dg_ref[...] += dg

    blocks = tm // hl
    return _hosted(
        rider, body, grid=(nt,), name="conv_bwd_b",
        in_specs=[
            _row(tm, D), pl.BlockSpec((hl, D), lambda i: (jnp.minimum((i + 1) * blocks, s // hl - 1), 0)),
            _row(tm, D), pl.BlockSpec((hl, D), lambda i: (jnp.maximum(i * blocks - 1, 0), 0)),
            _row(tm, 2 * D), _const((CONV_W, D)), _row(tm, D), _const((1, D)), _const((2 * D, D)), _row(tm, D),
        ],
        out_specs=[_row(tm, D), _row(tm, 2 * D), _acc((CONV_W, SUB, D)), _acc((SUB, 2 * D)), _acc((SUB, D))],
        out_shape=[SDS((s, D), f32), SDS((s, 2 * D), bf16), SDS((CONV_W, SUB, D), f32), SDS((SUB, 2 * D), f32),
                   SDS((SUB, D), f32)],
        scratch_shapes=[pltpu.VMEM((hl + tm, D), f32), pltpu.VMEM((tm + hl, D), f32), pltpu.VMEM((tm, D), f32),
                        pltpu.VMEM((SUB - 1, hl - SUB + tm, LANES), f32),
                        pltpu.VMEM((SUB - 1, hl - SUB + tm, LANES), f32)],
        compiler_params=_cp("arbitrary"),
    )(dc, dc, u, u, a, wdw, x, g, w1t, dres)


def loss_bwd(x, g, t, tm):
    s = x.shape[0]

    def body(x_ref, g_ref, t_ref, dx_ref, dg_ref, loss_ref):
        @pl.when(pl.program_id(0) == 0)
        def _():
            dg_ref[...] = jnp.zeros_like(dg_ref)
            loss_ref[...] = jnp.zeros_like(loss_ref)

        gv = g_ref[...]
        y, xn, r = _rms_fwd(x_ref[...], gv)
        e = y - t_ref[...]
        loss_ref[...] += 0.5 * jnp.sum(jnp.mean(e * e, axis=-1, keepdims=True), axis=0, keepdims=True)
        dx, dg = _rms_bwd(xn, r, gv, e / D)
        dx_ref[...] = dx
        dg_ref[...] += dg

    return pl.pallas_call(
        body, grid=(s // tm,), name="loss_bwd",
        in_specs=[_row(tm, D), _const((1, D)), _row(tm, D)],
        out_specs=[_row(tm, D), _acc((SUB, D)), _acc((1, 1))],
        out_shape=[SDS((s, D), f32), SDS((SUB, D), f32), SDS((1, 1), f32)],
        compiler_params=_cp("arbitrary"),
    )(x, g, t)


def wgrad(a, b, nb, tk, name, rider=None):
    s, k1 = a.shape
    n = b.shape[1]

    def body(a_ref, b_ref, o_ref):
        @pl.when(pl.program_id(1) == 0)
        def _():
            o_ref[...] = jnp.zeros_like(o_ref)

        o_ref[...] += lax.dot_general(a_ref[...], b_ref[...].astype(bf16), _TN, preferred_element_type=f32)

    return _hosted(
        rider, body, grid=(n // nb, s // tk), name=name,
        in_specs=[pl.BlockSpec((tk, k1), lambda j, k: (k, 0)), pl.BlockSpec((tk, nb), lambda j, k: (k, j))],
        out_specs=pl.BlockSpec((k1, nb), lambda j, k: (0, j)),
        out_shape=SDS((k1, n), f32),
        compiler_params=_cp("parallel", "arbitrary"),
    )(a, b)


def wgrad_cols(a, b, tk, name, rider=None):
    s, k1 = a.shape
    w = b.shape[1] // N_CHIPS

    def body(a_ref, b_ref, o_ref):
        @pl.when(pl.program_id(1) == 0)
        def _():
            o_ref[...] = jnp.zeros_like(o_ref)

        acc = lax.dot_general(a_ref[...], b_ref[...].astype(bf16), _TN, preferred_element_type=f32)
        o_ref[:, 0] += acc.reshape(2, k1 // 2, w)

    return _hosted(
        rider, body, grid=(N_CHIPS, s // tk), name=name,
        in_specs=[pl.BlockSpec((tk, k1), lambda j, k: (k, 0)), pl.BlockSpec((tk, w), lambda j, k: (k, j))],
        out_specs=pl.BlockSpec((2, 1, k1 // 2, w), lambda j, k: (0, j, 0, 0)),
        out_shape=SDS((2, N_CHIPS, k1 // 2, w), f32),
        compiler_params=_cp("parallel", "arbitrary"),
    )(a, b)


def wgrad_rows(a, b, nb, tk, name, rider=None):
    s, k1 = a.shape
    n = b.shape[1]
    r = k1 // (2 * N_CHIPS)

    def body(a_ref, b_ref, o_ref):
        @pl.when(pl.program_id(1) == 0)
        def _():
            o_ref[...] = jnp.zeros_like(o_ref)

        acc = lax.dot_general(a_ref[...], b_ref[...].astype(bf16), _TN, preferred_element_type=f32)
        for j in range(N_CHIPS):
            for h in range(2):
                o_ref[h, j] += acc[(2 * j + h) * r:(2 * j + h + 1) * r, :]

    return _hosted(
        rider, body, grid=(n // nb, s // tk), name=name,
        in_specs=[pl.BlockSpec((tk, k1), lambda j, k: (k, 0)), pl.BlockSpec((tk, nb), lambda j, k: (k, j))],
        out_specs=pl.BlockSpec((2, N_CHIPS, r, nb), lambda j, k: (0, 0, 0, j)),
        out_shape=SDS((2, N_CHIPS, r, n), f32),
        compiler_params=_cp("parallel", "arbitrary"),
    )(a, b)


def _adam_math(w, g, m, v):
    m = B1 * m + (1.0 - B1) * g
    v = B2 * v + (1.0 - B2) * (g * g)
    m_hat = m / (1.0 - B1 ** STEP)
    v_hat = v / (1.0 - B2 ** STEP)
    delta = -LR * (m_hat / (jnp.sqrt(v_hat) + ADAM_EPS) + WD * w)
    return delta, m, v


def _rows_tile(r, c, multiple=SUB):
    best = None
    for t in range(multiple, r + 1, multiple):
        if r % t == 0 and t * c * 4 <= ELEMENTWISE_BLOCK_BYTES:
            best = t
    return best if best is not None else r


def adamw(w, g, m, v, name):
    l, r, c = w.shape
    tr = _rows_tile(r, c)
    spec = pl.BlockSpec((1, tr, c), lambda i, j: (i, j, 0))

    def body(w_ref, g_ref, m_ref, v_ref, d_ref, mo_ref, vo_ref):
        d, mn, vn = _adam_math(w_ref[...], g_ref[...], m_ref[...], v_ref[...])
        d_ref[...] = d
        mo_ref[...] = mn
        vo_ref[...] = vn

    return pl.pallas_call(
        body, grid=(l, r // tr), name=name, in_specs=[spec] * 4, out_specs=[spec] * 3,
        out_shape=[SDS((l, r, c), f32)] * 3, compiler_params=_cp("parallel", "parallel"),
    )(w, g, m, v)


def _place():
    return lax.axis_index("x"), lax.axis_index("y"), lax.axis_index("c")


def _chip_peer(xi, yi, r):
    px = 1 - xi if r & 2 else xi
    py = 1 - yi if r & 1 else yi
    return px, py


def _gv_qkv(src, dst, j, h):
    rows = pl.ds(h * (D // 2), D // 2)
    return src.at[rows, :], dst.at[j, rows, :]


def _gv_rows(src, dst, j, h):
    r = src.shape[0] // 2
    return src.at[pl.ds(h * r, r), :], dst.at[pl.ds(j * 2 * r + h * r, r), :]


def _gv_cols(src, dst, j, h):
    r, w = src.shape[0] // 2, src.shape[1]
    return src.at[pl.ds(h * r, r), :], dst.at[pl.ds(h * r, r), pl.ds(j * w, w)]


def gather_rider(big, small, forward_at):
    nb, ns = len(big), len(small)
    n = nb + ns
    views = [v for _, v, _ in big]

    def env(ins, outs, sems):
        ici_send, ici_recv, d2d_send, d2d_recv, loc_sems = sems
        xi, yi, ci = _place()
        me = 2 * xi + yi

        def local(a, h):
            if a < nb:
                src, dst = views[a](ins[a], outs[a], me, h)
                return pltpu.make_async_copy(src, dst, loc_sems.at[2 * a + h])
            return pltpu.make_async_copy(ins[a], outs[a].at[me], loc_sems.at[nb + a])

        def ici(a, r, slot):
            px, py = _chip_peer(xi, yi, r)
            src, dst = views[a](ins[a], outs[a], slot, ci) if a < nb else (ins[a], outs[a].at[slot])
            k = 3 * a + r - 1
            return pltpu.make_async_remote_copy(src_ref=src, dst_ref=dst, send_sem=ici_send.at[k],
                                                recv_sem=ici_recv.at[k], device_id=(px, py, ci), device_id_type=MESH)

        def d2d(a, r, half):
            px, py = _chip_peer(xi, yi, r)
            _, dst = views[a](ins[a], outs[a], 2 * px + py, half)
            k = 3 * a + r - 1
            return pltpu.make_async_remote_copy(src_ref=dst, dst_ref=dst, send_sem=d2d_send.at[k],
                                                recv_sem=d2d_recv.at[k], device_id=(xi, yi, 1 - ci), device_id_type=MESH)

        return xi, yi, ci, me, local, ici, d2d

    def locals_of():
        return [(a, h) for a in range(nb) for h in range(2)] + [(a, 0) for a in range(nb, n)]

    def send(ins, outs, sems):
        _, _, _, me, local, ici, _ = env(ins, outs, sems)
        for a, h in locals_of():
            local(a, h).start()
        for a in range(n):
            for r in (1, 2, 3):
                ici(a, r, me).start()

    def forward(ins, outs, sems):
        xi, yi, ci, _, _, ici, d2d = env(ins, outs, sems)
        for a in range(n):
            for r in (1, 2, 3):
                px, py = _chip_peer(xi, yi, r)
                ici(a, r, 2 * px + py).wait_recv()
                if a < nb:
                    d2d(a, r, ci).start()

    def finish(ins, outs, sems):
        _, _, ci, me, local, ici, d2d = env(ins, outs, sems)
        for a in range(nb):
            for r in (1, 2, 3):
                d2d(a, r, 1 - ci).wait_recv()
        for a in range(n):
            for r in (1, 2, 3):
                ici(a, r, me).wait_send()
                if a < nb:
                    d2d(a, r, ci).wait_send()
        for a, h in locals_of():
            local(a, h).wait()

    dma = pltpu.SemaphoreType.DMA
    return Rider(
        ins=[b for b, _, _ in big] + list(small),
        out_shape=[SDS(shape, bf16) for _, _, shape in big] + [SDS((N_CHIPS,) + a.shape, a.dtype) for a in small],
        sem_shapes=[dma((3 * n,)), dma((3 * n,)), dma((max(3 * nb, 1),)), dma((max(3 * nb, 1),)), dma((2 * nb + ns,))],
        stages=[(0.0, send), (forward_at, forward)], final=finish)


def pair_send_rider(gs):
    n = len(gs)

    def copy(ins, outs, sems, a):
        xi, yi, ci = _place()
        return pltpu.make_async_remote_copy(
            src_ref=ins[a].at[1 - ci], dst_ref=outs[a], send_sem=sems[0].at[a], recv_sem=sems[1].at[a],
            device_id=(xi, yi, 1 - ci), device_id_type=MESH)

    def send(ins, outs, sems):
        for a in range(n):
            copy(ins, outs, sems, a).start()

    def finish(ins, outs, sems):
        for a in range(n):
            copy(ins, outs, sems, a).wait()

    dma = pltpu.SemaphoreType.DMA
    return Rider(ins=list(gs), out_shape=[SDS(g.shape[1:], g.dtype) for g in gs], sem_shapes=[dma((n,)), dma((n,))],
                 stages=[(0.0, send)], final=finish)


def chip_send_rider(ps):
    n = len(ps)

    def env(ins, outs, sems):
        xi, yi, ci = _place()
        me = 2 * xi + yi

        def local(a):
            return pltpu.make_async_copy(ins[a].at[me], outs[a].at[me], sems[2].at[a])

        def copy(a, r, src_slot, dst_slot):
            px, py = _chip_peer(xi, yi, r)
            k = 3 * a + r - 1
            return pltpu.make_async_remote_copy(
                src_ref=ins[a].at[src_slot], dst_ref=outs[a].at[dst_slot], send_sem=sems[0].at[k],
                recv_sem=sems[1].at[k], device_id=(px, py, ci), device_id_type=MESH)

        return xi, yi, me, local, copy

    def send(ins, outs, sems):
        xi, yi, me, local, copy = env(ins, outs, sems)
        for a in range(n):
            local(a).start()
            for r in (1, 2, 3):
                px, py = _chip_peer(xi, yi, r)
                copy(a, r, 2 * px + py, me).start()

    def finish(ins, outs, sems):
        xi, yi, me, local, copy = env(ins, outs, sems)
        for a in range(n):
            for r in (1, 2, 3):
                px, py = _chip_peer(xi, yi, r)
                copy(a, r, me, 2 * px + py).wait_recv()
        for a in range(n):
            for r in (1, 2, 3):
                px, py = _chip_peer(xi, yi, r)
                copy(a, r, 2 * px + py, me).wait_send()
            local(a).wait()

    dma = pltpu.SemaphoreType.DMA
    return Rider(ins=list(ps), out_shape=[SDS(p.shape, p.dtype) for p in ps],
                 sem_shapes=[dma((3 * n,)), dma((3 * n,)), dma((n,))], stages=[(0.0, send)], final=finish)


def exchange_rider(halves):
    n = len(halves)

    def env(ins, outs, sems):
        xi, yi, ci = _place()

        def local(a):
            return pltpu.make_async_copy(ins[a], outs[a].at[ci], sems[2].at[a])

        def copy(a, slot):
            return pltpu.make_async_remote_copy(
                src_ref=ins[a], dst_ref=outs[a].at[slot], send_sem=sems[0].at[a], recv_sem=sems[1].at[a],
                device_id=(xi, yi, 1 - ci), device_id_type=MESH)

        return ci, local, copy

    def send(ins, outs, sems):
        ci, local, copy = env(ins, outs, sems)
        for a in range(n):
            local(a).start()
            copy(a, ci).start()

    def finish(ins, outs, sems):
        ci, local, copy = env(ins, outs, sems)
        for a in range(n):
            copy(a, 1 - ci).wait_recv()
        for a in range(n):
            copy(a, ci).wait_send()
            local(a).wait()

    dma = pltpu.SemaphoreType.DMA
    return Rider(ins=list(halves), out_shape=[SDS((2,) + h.shape, h.dtype) for h in halves],
                 sem_shapes=[dma((n,)), dma((n,)), dma((n,))], stages=[(0.0, send)], final=finish)


def rs_pair_add(g, other, ci, name):
    _, nsh, r, w = g.shape
    tr = _rows_tile(r, w, 16)

    def body(c_ref, g_ref, o_ref, p_ref, pb_ref):
        p = g_ref[0] + o_ref[...]
        p_ref[...] = p
        pb_ref[...] = p.astype(bf16)

    blk = pl.BlockSpec((1, tr, w), lambda j, i, c: (j, i, 0))
    return pl.pallas_call(
        body, name=name,
        grid_spec=pltpu.PrefetchScalarGridSpec(
            num_scalar_prefetch=1, grid=(nsh, r // tr),
            in_specs=[pl.BlockSpec((1, 1, tr, w), lambda j, i, c: (c[0], j, i, 0)), blk], out_specs=[blk, blk]),
        out_shape=[SDS((nsh, r, w), f32), SDS((nsh, r, w), bf16)],
        compiler_params=_cp("parallel", "parallel"),
    )(ci.reshape(1).astype(jnp.int32), g, other)


def rs_chip_add(p32, parts, chip, name):
    nsh, r, w = parts.shape
    tr = _rows_tile(r, w, 16)

    def body(c_ref, p_ref, parts_ref, o_ref):
        me = c_ref[0]
        acc = None
        for j in range(N_CHIPS):
            term = jnp.where(me == j, p_ref[0], parts_ref[j].astype(f32))
            acc = term if acc is None else acc + term
        o_ref[...] = acc

    return pl.pallas_call(
        body, name=name,
        grid_spec=pltpu.PrefetchScalarGridSpec(
            num_scalar_prefetch=1, grid=(r // tr,),
            in_specs=[pl.BlockSpec((1, tr, w), lambda i, c: (c[0], i, 0)),
                      pl.BlockSpec((nsh, tr, w), lambda i, c: (0, i, 0))],
            out_specs=pl.BlockSpec((tr, w), lambda i, c: (i, 0))),
        out_shape=SDS((r, w), f32),
        compiler_params=_cp("parallel"),
    )(chip.reshape(1).astype(jnp.int32), p32, parts)


class GradReduction:
    def __init__(self, keys, grads, ci, chip):
        self.keys, self.grads, self.ci, self.chip = keys, grads, ci, chip

    def pair_rider(self):
        return pair_send_rider(self.grads)

    def chip_rider(self, from_pair):
        both = [rs_pair_add(g, o, self.ci, "rs_pair_add_" + k) for k, g, o in zip(self.keys, self.grads, from_pair)]
        self.p32 = [p for p, _ in both]
        return chip_send_rider([pb for _, pb in both])

    def exchange_rider(self, parts):
        return exchange_rider([rs_chip_add(p, q, self.chip, "rs_chip_add_" + k)
                               for k, p, q in zip(self.keys, self.p32, parts)])

    def alone(self):
        parts = run_rider(self.chip_rider(run_rider(self.pair_rider(), "rs_pair_" + self.keys[0])), "rs_chip_" + self.keys[0])
        return run_rider(self.exchange_rider(parts), "rs_swap_" + self.keys[0])


STAGE_W = DFF
_ST = {"norm_mix0": 0, "norm_mix1": 1, "attn_b_qkv": 2, "attn_sinks": 3, "attn_b_o": 4, "final_norm": 5, "loss": 6,
       "norm_ffn0": 8, "norm_ffn1": 9, "ffn_b_dw0": 10, "ffn_b_dw1": 11, "conv_b_pw1": 12, "conv_b_dw": 13,
       "conv_ln_g": 14, "conv_ln_b": 15, "conv_b_pw2": 16, "ffn_w_dw0": 17, "ffn_w_dw1": 20, "conv_w_dw": 24}
STAGE_ROWS = 56
SMALL_REP = ("norm_mix", "attn_b_qkv", "attn_sinks", "attn_b_o", "norm_ffn", "ffn_b_dw", "final_norm")
SMALL_SH = ("conv_b_pw1", "conv_w_dw", "conv_b_dw", "conv_ln_g", "conv_ln_b", "conv_b_pw2", "ffn_w_dw")
_SMALL_PARTS = ("norm_mix0", "norm_mix1", "attn_b_qkv", "attn_sinks", "attn_b_o", "norm_ffn0", "norm_ffn1", "ffn_b_dw0",
                "ffn_b_dw1", "final_norm", "conv_b_pw1", "conv_b_dw", "conv_ln_g", "conv_ln_b", "conv_b_pw2", "loss",
                "ffn_w_dw0", "ffn_w_dw1", "conv_w_dw")


def small_reduce_adamw(parts, w, m, v):
    names = SMALL_REP + SMALL_SH
    npart, nw = len(_SMALL_PARTS), len(names)

    def body(*refs):
        part = dict(zip(_SMALL_PARTS, refs[:npart]))
        off = npart
        w_ref = dict(zip(names, refs[off:off + nw]))
        m_ref = dict(zip(names, refs[off + nw:off + 2 * nw]))
        v_ref = dict(zip(names, refs[off + 2 * nw:off + 3 * nw]))
        off += 3 * nw
        loss_ref = refs[off]
        g_out = dict(zip(names, refs[off + 1:off + 1 + nw]))
        d_out = dict(zip(names, refs[off + 1 + nw:off + 1 + 2 * nw]))
        m_out = dict(zip(names, refs[off + 1 + 2 * nw:off + 1 + 3 * nw]))
        v_out = dict(zip(names, refs[off + 1 + 3 * nw:off + 1 + 4 * nw]))
        stage_ref, buf_ref, tot_ref, send_sems, recv_sems = refs[off + 1 + 4 * nw:]

        xi, yi, ci = _place()
        me = 4 * xi + 2 * yi + ci
        chip = 2 * xi + yi

        stage_ref[...] = jnp.zeros_like(stage_ref)
        for name in _SMALL_PARTS:
            ref, r0 = part[name], _ST[name]
            if name in ("attn_sinks", "loss"):
                val = ref[...]
            elif name in ("ffn_w_dw0", "ffn_w_dw1", "conv_w_dw"):
                val = jnp.sum(ref[...], axis=1)
            else:
                val = jnp.sum(ref[...], axis=0, keepdims=True)
            stage_ref[r0:r0 + val.shape[0], 0:val.shape[1]] = val

        buf_ref[me] = stage_ref[...]

        def peer(r):
            px, py = _chip_peer(xi, yi, r >> 1)
            return px, py, (1 - ci if r & 1 else ci)

        def copy(r, slot):
            return pltpu.make_async_remote_copy(
                src_ref=stage_ref, dst_ref=buf_ref.at[slot], send_sem=send_sems.at[r - 1], recv_sem=recv_sems.at[r - 1],
                device_id=peer(r), device_id_type=MESH)

        sends = []
        for r in range(1, N_DEV):
            cp = copy(r, me)
            cp.start()
            sends.append(cp)
        for r in range(1, N_DEV):
            px, py, pc = peer(r)
            copy(r, 4 * px + 2 * py + pc).wait_recv()
        for cp in sends:
            cp.wait_send()
        acc = buf_ref[0]
        for d in range(1, N_DEV):
            acc = acc + buf_ref[d]
        tot_ref[...] = acc

        def rows(name, n, width):
            r0 = _ST[name]
            return tot_ref[r0:r0 + n, 0:width]

        def mine(name, n, width):
            r0 = _ST[name]
            out = tot_ref[r0:r0 + n, 0:width]
            for j in range(1, N_CHIPS):
                out = jnp.where(chip == j, tot_ref[r0:r0 + n, j * width:(j + 1) * width], out)
            return out

        loss_ref[...] = rows("loss", 1, 1)
        grads = {
            "norm_mix": rows("norm_mix0", 2, D), "attn_b_qkv": rows("attn_b_qkv", 1, QKV),
            "attn_sinks": rows("attn_sinks", 1, N_HEADS), "attn_b_o": rows("attn_b_o", 1, D),
            "norm_ffn": rows("norm_ffn0", 2, D), "ffn_b_dw": rows("ffn_b_dw0", 2, DFF),
            "final_norm": rows("final_norm", 1, D),
            "conv_b_pw1": mine("conv_b_pw1", 1, 2 * D // N_CHIPS), "conv_w_dw": mine("conv_w_dw", CONV_W, D // N_CHIPS),
            "conv_b_dw": mine("conv_b_dw", 1, D // N_CHIPS), "conv_ln_g": mine("conv_ln_g", 1, D // N_CHIPS),
            "conv_ln_b": mine("conv_ln_b", 1, D // N_CHIPS), "conv_b_pw2": mine("conv_b_pw2", 1, D // N_CHIPS),
        }
        for name in names:
            if name == "ffn_w_dw":
                continue
            at = 0 if name == "conv_w_dw" else Ellipsis
            g = grads[name]
            d, mn, vn = _adam_math(w_ref[name][at], g, m_ref[name][at], v_ref[name][at])
            g_out[name][at] = g
            d_out[name][at] = d
            m_out[name][at] = mn
            v_out[name][at] = vn
        for layer, key in enumerate(("ffn_w_dw0", "ffn_w_dw1")):
            g = mine(key, 3, DFF // N_CHIPS)
            d, mn, vn = _adam_math(w_ref["ffn_w_dw"][layer], g, m_ref["ffn_w_dw"][layer], v_ref["ffn_w_dw"][layer])
            g_out["ffn_w_dw"][layer] = g
            d_out["ffn_w_dw"][layer] = d
            m_out["ffn_w_dw"][layer] = mn
            v_out["ffn_w_dw"][layer] = vn

    ins = [parts[k] for k in _SMALL_PARTS] + [src[k] for src in (w, m, v) for k in names]
    wshapes = [SDS(w[k].shape, f32) for k in names]
    outs = pl.pallas_call(
        body, name="small_reduce_adamw", in_specs=[_VMEM] * len(ins), out_specs=[_VMEM] * (1 + 4 * nw),
        out_shape=[SDS((1, 1), f32)] + wshapes * 4,
        scratch_shapes=[pltpu.VMEM((STAGE_ROWS, STAGE_W), f32), pltpu.VMEM((N_DEV, STAGE_ROWS, STAGE_W), f32),
                        pltpu.VMEM((STAGE_ROWS, STAGE_W), f32), pltpu.SemaphoreType.DMA((N_DEV - 1,)),
                        pltpu.SemaphoreType.DMA((N_DEV - 1,))],
        compiler_params=pltpu.CompilerParams(vmem_limit_bytes=VMEM_LIMIT),
    )(*ins)
    loss = outs[0]
    g, d, mn, vn = (dict(zip(names, outs[1 + k * nw:1 + (k + 1) * nw])) for k in range(4))
    return loss, g, d, mn, vn


TM = 512
TM_FFN = 256
FFN_CHUNK = 256
CONV_ROWS = 128
TK = 1024
FORWARD_AT = 0.6


def kernel(x, norm_mix, attn_w_qkv, attn_b_qkv, attn_sinks, attn_w_o, attn_b_o, conv_w_pw1, conv_b_pw1, conv_w_dw, conv_b_dw, conv_ln_g, conv_ln_b, conv_w_pw2, conv_b_pw2, norm_ffn, ffn_w_up, ffn_w_dw, ffn_b_dw, ffn_w_down, final_norm, loss_target, m_norm_mix, m_attn_w_qkv, m_attn_b_qkv, m_attn_sinks, m_attn_w_o, m_attn_b_o, m_conv_w_pw1, m_conv_b_pw1, m_conv_w_dw, m_conv_b_dw, m_conv_ln_g, m_conv_ln_b, m_conv_w_pw2, m_conv_b_pw2, m_norm_ffn, m_ffn_w_up, m_ffn_w_dw, m_ffn_b_dw, m_ffn_w_down, m_final_norm, v_norm_mix, v_attn_w_qkv, v_attn_b_qkv, v_attn_sinks, v_attn_w_o, v_attn_b_o, v_conv_w_pw1, v_conv_b_pw1, v_conv_w_dw, v_conv_b_dw, v_conv_ln_g, v_conv_ln_b, v_conv_w_pw2, v_conv_b_pw2, v_norm_ffn, v_ffn_w_up, v_ffn_w_dw, v_ffn_b_dw, v_ffn_w_down, v_final_norm):
    w = dict(norm_mix=norm_mix, attn_w_qkv=attn_w_qkv, attn_b_qkv=attn_b_qkv, attn_sinks=attn_sinks, attn_w_o=attn_w_o,
             attn_b_o=attn_b_o, conv_w_pw1=conv_w_pw1, conv_b_pw1=conv_b_pw1, conv_w_dw=conv_w_dw, conv_b_dw=conv_b_dw,
             conv_ln_g=conv_ln_g, conv_ln_b=conv_ln_b, conv_w_pw2=conv_w_pw2, conv_b_pw2=conv_b_pw2, norm_ffn=norm_ffn,
             ffn_w_up=ffn_w_up, ffn_w_dw=ffn_w_dw, ffn_b_dw=ffn_b_dw, ffn_w_down=ffn_w_down, final_norm=final_norm)
    mom = dict(norm_mix=m_norm_mix, attn_w_qkv=m_attn_w_qkv, attn_b_qkv=m_attn_b_qkv, attn_sinks=m_attn_sinks,
               attn_w_o=m_attn_w_o, attn_b_o=m_attn_b_o, conv_w_pw1=m_conv_w_pw1, conv_b_pw1=m_conv_b_pw1,
               conv_w_dw=m_conv_w_dw, conv_b_dw=m_conv_b_dw, conv_ln_g=m_conv_ln_g, conv_ln_b=m_conv_ln_b,
               conv_w_pw2=m_conv_w_pw2, conv_b_pw2=m_conv_b_pw2, norm_ffn=m_norm_ffn, ffn_w_up=m_ffn_w_up,
               ffn_w_dw=m_ffn_w_dw, ffn_b_dw=m_ffn_b_dw, ffn_w_down=m_ffn_w_down, final_norm=m_final_norm)
    vel = dict(norm_mix=v_norm_mix, attn_w_qkv=v_attn_w_qkv, attn_b_qkv=v_attn_b_qkv, attn_sinks=v_attn_sinks,
               attn_w_o=v_attn_w_o, attn_b_o=v_attn_b_o, conv_w_pw1=v_conv_w_pw1, conv_b_pw1=v_conv_b_pw1,
               conv_w_dw=v_conv_w_dw, conv_b_dw=v_conv_b_dw, conv_ln_g=v_conv_ln_g, conv_ln_b=v_conv_ln_b,
               conv_w_pw2=v_conv_w_pw2, conv_b_pw2=v_conv_b_pw2, norm_ffn=v_norm_ffn, ffn_w_up=v_ffn_w_up,
               ffn_w_dw=v_ffn_w_dw, ffn_b_dw=v_ffn_b_dw, ffn_w_down=v_ffn_w_down, final_norm=v_final_norm)
    order = ("norm_mix", "attn_w_qkv", "attn_b_qkv", "attn_sinks", "attn_w_o", "attn_b_o", "conv_w_pw1", "conv_b_pw1",
             "conv_w_dw", "conv_b_dw", "conv_ln_g", "conv_ln_b", "conv_w_pw2", "conv_b_pw2", "norm_ffn", "ffn_w_up",
             "ffn_w_dw", "ffn_b_dw", "ffn_w_down", "final_norm")
    xi, yi, ci = _place()
    chip = 2 * xi + yi
    xs, target = x[0], loss_target[0]
    s = xs.shape[0]
    tm, tmf, tk = min(TM, s), min(TM_FFN, s), min(TK, s)
    row = lambda v: v.reshape(1, -1)
    join = lambda a, axis: jnp.concatenate([a[j] for j in range(N_CHIPS)], axis=axis)
    cast = lambda a: a.astype(bf16)
    small, big = {}, {}

    got = run_rider(gather_rider(
        [(cast(attn_w_qkv[0]), _gv_qkv, (N_CHIPS, D, QKV // N_CHIPS)), (cast(attn_w_o[0]), _gv_rows, (D, D))],
        [w[k] for k in SMALL_SH], 0.0), "gather_attn")
    qkv4, w_o = got[:2]
    sm = dict(zip(SMALL_SH, got[2:]))
    w_qkv = jnp.transpose(qkv4, (1, 0, 2)).reshape(D, QKV)
    sinks = attn_sinks.reshape(N_HEADS)
    b_pw1, conv_dw, conv_bdw = join(sm["conv_b_pw1"], 1), join(sm["conv_w_dw"], 2)[0], join(sm["conv_b_dw"], 1)
    ln_g, ln_b, b_pw2, ffn_dw = (join(sm["conv_ln_g"], 1), join(sm["conv_ln_b"], 1), join(sm["conv_b_pw2"], 1),
                                 join(sm["ffn_w_dw"], 2))

    h0, qkv = qkv_fwd(xs, row(norm_mix[0]), w_qkv, attn_b_qkv, tm)
    (o, lse), (w_up0, w_dn0) = attn_fwd(qkv, sinks, rider=gather_rider(
        [(cast(ffn_w_up[0]), _gv_cols, (D, 2 * DFF)), (cast(ffn_w_down[0]), _gv_rows, (DFF, D))], [], FORWARD_AT))
    x1 = attn_out_fwd(xs, o, w_o, attn_b_o, tm)
    (x2, h1, up0, gate0, act0), (w_pw1, w_pw2, w_up1, w_dn1) = ffn_fwd(
        x1, row(norm_ffn[0]), w_up0, ffn_dw[0], row(ffn_b_dw[0]), w_dn0, tmf, FFN_CHUNK, rider=gather_rider(
            [(cast(conv_w_pw1[0]), _gv_cols, (D, 2 * D)), (cast(conv_w_pw2[0]), _gv_rows, (D, D)),
             (cast(ffn_w_up[1]), _gv_cols, (D, 2 * DFF)), (cast(ffn_w_down[1]), _gv_rows, (DFF, D))], [], FORWARD_AT))
    h2, a, u = pw1_fwd(x2, row(norm_mix[1]), w_pw1, b_pw1, tm)
    c, x3 = conv_fwd(u, x2, conv_dw, conv_bdw, ln_g, ln_b, w_pw2, b_pw2, tm, CONV_ROWS)
    x4, h3, up1, gate1, act1 = ffn_fwd(x3, row(norm_ffn[1]), w_up1, ffn_dw[1], row(ffn_b_dw[1]), w_dn1, tmf, FFN_CHUNK)

    dx4, small["final_norm"], small["loss"] = loss_bwd(x4, final_norm.reshape(1, D), target, tm)
    dx3, dup1, small["norm_ffn1"], small["ffn_w_dw1"], small["ffn_b_dw1"] = ffn_bwd(
        dx4, x3, row(norm_ffn[1]), up1, gate1, ffn_dw[1], w_dn1.T, w_up1.T, tmf, FFN_CHUNK)
    red1 = GradReduction(("up1", "down1"), [wgrad_cols(h3, dup1, tk, "wgrad_up1"),
                                           wgrad_rows(act1, dx4, 512, tk, "wgrad_down1")], ci, chip)
    (dc, u3, small["conv_ln_g"], small["conv_ln_b"], small["conv_b_pw2"], small["conv_b_dw"]), from_pair = conv_bwd_a(
        dx3, c, ln_g, ln_b, w_pw2.T, tm, rider=red1.pair_rider())
    g_pw2 = wgrad_rows(u3, dx3, 512, tk, "wgrad_pw2")
    (dx2, da, small["conv_w_dw"], small["conv_b_pw1"], small["norm_mix1"]), parts = conv_bwd_b(
        dc, u, a, conv_dw, x2, row(norm_mix[1]), w_pw1.T, dx3, tm, CONV_ROWS, rider=red1.chip_rider(from_pair))
    g_pw1, swapped = wgrad_cols(h2, da, tk, "wgrad_pw1", rider=red1.exchange_rider(parts))
    big.update(zip(red1.keys, swapped))

    red2 = GradReduction(("pw1", "pw2"), [g_pw1, g_pw2], ci, chip)
    (dx1, dup0, small["norm_ffn0"], small["ffn_w_dw0"], small["ffn_b_dw0"]), from_pair = ffn_bwd(
        dx2, x1, row(norm_ffn[0]), up0, gate0, ffn_dw[0], w_dn0.T, w_up0.T, tmf, FFN_CHUNK, rider=red2.pair_rider())
    g_up0, parts = wgrad_cols(h1, dup0, tk, "wgrad_up0", rider=red2.chip_rider(from_pair))
    g_dn0, swapped = wgrad_rows(act0, dx2, 512, tk, "wgrad_down0", rider=red2.exchange_rider(parts))
    big.update(zip(red2.keys, swapped))

    red3 = GradReduction(("up0", "down0", "wo"), [g_up0, g_dn0, wgrad_rows(o, dx1, 512, tk, "wgrad_o")], ci, chip)
    (do, small["attn_b_o"]), from_pair = attn_out_bwd(dx1, w_o.T, tm, rider=red3.pair_rider())
    (dq, dkv, small["attn_sinks"]), parts = attn_bwd(qkv, o, do, lse, sinks, rider=red3.chip_rider(from_pair))
    dx0, small["norm_mix0"], small["attn_b_qkv"], dkvb = qkv_bwd(dq, dkv, xs, row(norm_mix[0]), w_qkv.T, dx1, tm)
    g_q, swapped = wgrad(h0, dq, 512, tk, "wgrad_q", rider=red3.exchange_rider(parts))
    big.update(zip(red3.keys, swapped))
    g_qkv = jnp.concatenate([g_q, wgrad(h0, dkvb, 2 * N_KV * HD, tk, "wgrad_kv")], axis=1)
    g_qkv = jnp.transpose(g_qkv.reshape(2, D // 2, N_CHIPS, QKV // N_CHIPS), (0, 2, 1, 3))
    red4 = GradReduction(("qkv",), [g_qkv], ci, chip)
    big.update(zip(red4.keys, red4.alone()))

    gbig = {
        "attn_w_qkv": big["qkv"].reshape(1, D, QKV // N_CHIPS), "attn_w_o": big["wo"].reshape(1, D // N_CHIPS, D),
        "conv_w_pw1": big["pw1"].reshape(1, D, 2 * D // N_CHIPS), "conv_w_pw2": big["pw2"].reshape(1, D // N_CHIPS, D),
        "ffn_w_up": jnp.stack([big["up0"], big["up1"]]).reshape(2, D, 2 * DFF // N_CHIPS),
        "ffn_w_down": jnp.stack([big["down0"], big["down1"]]).reshape(2, DFF // N_CHIPS, D),
    }

    fix = lambda d: {**d, "final_norm": d["final_norm"].reshape(1, D)}
    loss, gs, ds, ms, vs = small_reduce_adamw(small, fix(w), fix(mom), fix(vel))
    unfix = lambda d: {**d, "final_norm": d["final_norm"].reshape(D)}
    gout, delta, new_m, new_v = unfix(gs), unfix(ds), unfix(ms), unfix(vs)

    for name, g in gbig.items():
        gout[name] = g
        delta[name], new_m[name], new_v[name] = adamw(w[name], g, mom[name], vel[name], "adamw_" + name)

    return (loss.reshape(()), dx0[None], *[gout[n] for n in order], *[delta[n] for n in order],
            *[new_m[n] for n in order], *[new_v[n] for n in order])
```

```python
import math

import jax
import jax.numpy as jnp
from jax import lax
from jax.experimental import pallas as pl
from jax.experimental.pallas import tpu as pltpu

f32 = jnp.float32
bf16 = jnp.bfloat16
SDS = jax.ShapeDtypeStruct
MESH = pl.DeviceIdType.MESH

D = 1024
N_HEADS = 16
N_KV = 2
GROUP = 8
HD = 64
BLK = 128
QKV = (N_HEADS + 2 * N_KV) * HD
KV_COL_BLOCK = (N_HEADS * HD) // (2 * N_KV * HD)
CONV_W = 31
CONV_HALO = 32
DFF = 2816
RMS_EPS = 1e-6
LN_EPS = 1e-5
LR, B1, B2, ADAM_EPS, WD, STEP = 0.001, 0.9, 0.999, 1e-08, 0.01, 10

N_CHIPS = 4
N_DEV = 8
VMEM_LIMIT = 56 * 1024 * 1024
LANES = 128
SUB = 8
ELEMENTWISE_BLOCK_BYTES = 1 << 20


def _cp(*sem):
    return pltpu.CompilerParams(dimension_semantics=sem, vmem_limit_bytes=VMEM_LIMIT)


def _row(tm, n):
    return pl.BlockSpec((tm, n), lambda i: (i, 0))


def _const(shape):
    return pl.BlockSpec(shape, lambda *_: (0,) * len(shape), pipeline_mode=pl.Buffered(1))


def _acc(shape):
    return pl.BlockSpec(shape, lambda *_: (0,) * len(shape))


def _rms_fwd(x, g):
    r = lax.rsqrt(jnp.mean(x * x, axis=-1, keepdims=True) + RMS_EPS)
    xn = x * r
    return xn * g, xn, r


def _colsum8(v):
    return jnp.sum(v.reshape(v.shape[0] // SUB, SUB, v.shape[1]), axis=0)


def _rms_bwd(xn, r, g, dh):
    dyn = dh * g
    dx = r * (dyn - xn * jnp.mean(dyn * xn, axis=-1, keepdims=True))
    return dx, _colsum8(dh * xn)


def _sigmoid(z):
    return 0.5 * jnp.tanh(0.5 * z) + 0.5


def _dsilu(z, sg):
    return sg * (1.0 + z * (1.0 - sg))


def _dot(a, b):
    return jnp.dot(a, b, preferred_element_type=f32)


_ANY = pl.BlockSpec(memory_space=pl.ANY)
_VMEM = pl.BlockSpec(memory_space=pltpu.VMEM)


class Rider:
    def __init__(self, ins, out_shape, sem_shapes, stages, final):
        self.ins, self.out_shape, self.sem_shapes, self.stages, self.final = ins, out_shape, sem_shapes, stages, final


def run_rider(rider, name):
    n_in, n_out = len(rider.ins), len(rider.out_shape)

    def body(*refs):
        parts = refs[:n_in], refs[n_in:n_in + n_out], refs[n_in + n_out:]
        for _, fn in rider.stages:
            fn(*parts)
        rider.final(*parts)

    return pl.pallas_call(
        body, name=name, in_specs=[_ANY] * n_in, out_specs=[_ANY] * n_out, out_shape=list(rider.out_shape),
        scratch_shapes=list(rider.sem_shapes),
    )(*rider.ins)


def _hosted(rider, body, *, grid, in_specs, out_specs, out_shape, name, compiler_params, scratch_shapes=()):
    if rider is None:
        return pl.pallas_call(body, grid=grid, in_specs=in_specs, out_specs=out_specs, out_shape=out_shape, name=name,
                              compiler_params=compiler_params, scratch_shapes=list(scratch_shapes))
    single = not isinstance(out_shape, (list, tuple))
    shapes = [out_shape] if single else list(out_shape)
    specs = [out_specs] if single else list(out_specs)
    n_in, n_out, n_sc = len(in_specs), len(shapes), len(scratch_shapes)
    r_in, r_out = len(rider.ins), len(rider.out_shape)
    total = math.prod(grid)

    def wrapped(*refs):
        own_in, refs = refs[:n_in], refs[n_in:]
        r_ins, refs = refs[:r_in], refs[r_in:]
        own_out, refs = refs[:n_out], refs[n_out:]
        r_outs, refs = refs[:r_out], refs[r_out:]
        own_sc, r_sems = refs[:n_sc], refs[n_sc:]
        step = 0
        for d, n in enumerate(grid):
            step = step * n + pl.program_id(d)
        for frac, fn in rider.stages:
            @pl.when(step == min(int(frac * total), total - 1))
            def _(fn=fn):
                fn(r_ins, r_outs, r_sems)

        body(*own_in, *own_out, *own_sc)

        @pl.when(step == total - 1)
        def _():
            rider.final(r_ins, r_outs, r_sems)

    call = pl.pallas_call(
        wrapped, grid=grid, in_specs=list(in_specs) + [_ANY] * r_in, out_specs=specs + [_ANY] * r_out,
        out_shape=shapes + list(rider.out_shape), scratch_shapes=list(scratch_shapes) + list(rider.sem_shapes),
        name=name, compiler_params=_cp(*(("arbitrary",) * len(grid))))

    def run(*args):
        res = call(*args, *rider.ins)
        own = res[:n_out]
        return (own[0] if single else own), res[n_out:]

    return run


def qkv_fwd(x, g, w, b, tm):
    s = x.shape[0]

    def body(x_ref, g_ref, w_ref, b_ref, h_ref, o_ref):
        h, _, _ = _rms_fwd(x_ref[...], g_ref[...])
        hb = h.astype(bf16)
        h_ref[...] = hb
        o_ref[...] = (_dot(hb, w_ref[...]) + b_ref[...]).astype(bf16)

    return pl.pallas_call(
        body, grid=(s // tm,), name="qkv_fwd",
        in_specs=[_row(tm, D), _const((1, D)), _const((D, QKV)), _const((1, QKV))],
        out_specs=[_row(tm, D), _row(tm, QKV)],
        out_shape=[SDS((s, D), bf16), SDS((s, QKV), bf16)],
        compiler_params=_cp("parallel"),
    )(x, g, w, b)


def _band_mask(i):
    qi = lax.broadcasted_iota(jnp.int32, (GROUP * BLK, 2 * BLK), 0) & (BLK - 1)
    ki = lax.broadcasted_iota(jnp.int32, (GROUP * BLK, 2 * BLK), 1)
    dist = qi + BLK - ki
    return (dist >= 0) & (dist < BLK) & ((ki >= BLK) | (i > 0))


_NEG = float(jnp.finfo(jnp.float32).min)
_NT = (((1,), (1,)), ((), ()))
_TN = (((0,), (0,)), ((), ()))


def _kv_heads(kvp_ref, kvc_ref, kvh):
    ks = slice(kvh * HD, (kvh + 1) * HD)
    vs = slice(N_KV * HD + kvh * HD, N_KV * HD + (kvh + 1) * HD)
    k = jnp.concatenate([kvp_ref[:, ks], kvc_ref[:, ks]], axis=0)
    v = jnp.concatenate([kvp_ref[:, vs], kvc_ref[:, vs]], axis=0)
    return k, v


def _stack_group(ref, kvh, width=HD):
    return jnp.concatenate([ref[:, (kvh * GROUP + gi) * width:(kvh * GROUP + gi + 1) * width] for gi in range(GROUP)],
                           axis=0)


def _group_sinks(sink_ref, kvh):
    row = lax.broadcasted_iota(jnp.int32, (GROUP * BLK, 1), 0)
    col = jnp.zeros((GROUP * BLK, 1), f32)
    for g in range(GROUP):
        col = jnp.where((row >= g * BLK) & (row < (g + 1) * BLK), sink_ref[kvh * GROUP + g], col)
    return col


def attn_fwd(qkv, sinks, rider=None):
    s = qkv.shape[0]
    scale = 1.0 / math.sqrt(HD)

    def body(q_ref, kvc_ref, kvp_ref, sink_ref, o_ref, lse_ref):
        valid = _band_mask(pl.program_id(0))
        for kvh in range(N_KV):
            k, v = _kv_heads(kvp_ref, kvc_ref, kvh)
            sc = lax.dot_general(_stack_group(q_ref, kvh), k, _NT, preferred_element_type=f32) * scale
            sc = jnp.where(valid, sc, _NEG)
            sink = _group_sinks(sink_ref, kvh)
            m = jnp.maximum(jnp.max(sc, axis=-1, keepdims=True), sink)
            p = jnp.exp(sc - m)
            denom = jnp.sum(p, axis=-1, keepdims=True) + jnp.exp(sink - m)
            og = _dot((p / denom).astype(bf16), v).astype(bf16)
            lse = m + jnp.log(denom)
            for gi in range(GROUP):
                h = kvh * GROUP + gi
                o_ref[:, h * HD:(h + 1) * HD] = og[gi * BLK:(gi + 1) * BLK]
                lse_ref[:, h:h + 1] = lse[gi * BLK:(gi + 1) * BLK]

    return _hosted(
        rider, body, grid=(s // BLK,), name="attn_fwd",
        in_specs=[
            pl.BlockSpec((BLK, N_HEADS * HD), lambda i: (i, 0)),
            pl.BlockSpec((BLK, 2 * N_KV * HD), lambda i: (i, KV_COL_BLOCK)),
            pl.BlockSpec((BLK, 2 * N_KV * HD), lambda i: (jnp.maximum(i - 1, 0), KV_COL_BLOCK)),
            pl.BlockSpec(memory_space=pltpu.SMEM),
        ],
        out_specs=[_row(BLK, D), _row(BLK, N_HEADS)],
        out_shape=[SDS((s, D), bf16), SDS((s, N_HEADS), f32)],
        compiler_params=_cp("parallel"),
    )(qkv, qkv, qkv, sinks)


def attn_out_fwd(x, o, w, b, tm):
    s = x.shape[0]

    def body(x_ref, o_ref, w_ref, b_ref, y_ref):
        y_ref[...] = x_ref[...] + _dot(o_ref[...], w_ref[...]) + b_ref[...]

    return pl.pallas_call(
        body, grid=(s // tm,), name="attn_out_fwd",
        in_specs=[_row(tm, D), _row(tm, D), _const((D, D)), _const((1, D))],
        out_specs=_row(tm, D), out_shape=SDS((s, D), f32),
        compiler_params=_cp("parallel"),
    )(x, o, w, b)


def attn_out_bwd(dy, wt, tm, rider=None):
    s = dy.shape[0]

    def body(dy_ref, wt_ref, do_ref, db_ref):
        @pl.when(pl.program_id(0) == 0)
        def _():
            db_ref[...] = jnp.zeros_like(db_ref)

        dy = dy_ref[...]
        do_ref[...] = _dot(dy.astype(bf16), wt_ref[...]).astype(bf16)
        db_ref[...] += _colsum8(dy)

    return _hosted(
        rider, body, grid=(s // tm,), name="attn_out_bwd",
        in_specs=[_row(tm, D), _const((D, D))],
        out_specs=[_row(tm, D), _acc((SUB, D))],
        out_shape=[SDS((s, D), bf16), SDS((SUB, D), f32)],
        compiler_params=_cp("arbitrary"),
    )(dy, wt)


def attn_bwd(qkv, o, do, lse, sinks, rider=None):
    s = qkv.shape[0]
    nb = s // BLK
    scale = 1.0 / math.sqrt(HD)
    kvw = 2 * N_KV * HD

    def body(q_ref, kvc_ref, kvp_ref, o_ref, do_ref, lse_ref, sink_ref, dq_ref, dkv_ref, ds_ref, carry_ref):
        i = pl.program_id(0)

        @pl.when(i == 0)
        def _():
            ds_ref[...] = jnp.zeros_like(ds_ref)
            carry_ref[...] = jnp.zeros_like(carry_ref)

        @pl.when(i < nb)
        def _():
            valid = _band_mask(i)
            for kvh in range(N_KV):
                k, v = _kv_heads(kvp_ref, kvc_ref, kvh)
                qg = _stack_group(q_ref, kvh)
                dog = _stack_group(do_ref, kvh)
                lse = _stack_group(lse_ref, kvh, 1)
                sc = lax.dot_general(qg, k, _NT, preferred_element_type=f32) * scale
                sc = jnp.where(valid, sc, _NEG)
                p = jnp.exp(sc - lse)
                dp = lax.dot_general(dog, v, _NT, preferred_element_type=f32)
                dlt = jnp.sum(dog.astype(f32) * _stack_group(o_ref, kvh).astype(f32), axis=-1, keepdims=True)
                dsc = (p * (dp - dlt)).astype(bf16)
                dqg = (_dot(dsc, k) * scale).astype(bf16)
                dk = lax.dot_general(dsc, qg, _TN, preferred_element_type=f32) * scale
                dv = lax.dot_general(p.astype(bf16), dog, _TN, preferred_element_type=f32)
                dsink = jnp.exp(_group_sinks(sink_ref, kvh) - lse) * dlt
                for gi in range(GROUP):
                    h = kvh * GROUP + gi
                    dq_ref[:, h * HD:(h + 1) * HD] = dqg[gi * BLK:(gi + 1) * BLK]
                    ds_ref[:, h:h + 1] += -jnp.sum(dsink[gi * BLK:(gi + 1) * BLK], axis=0, keepdims=True)
                ks = slice(kvh * HD, (kvh + 1) * HD)
                vs = slice(N_KV * HD + kvh * HD, N_KV * HD + (kvh + 1) * HD)
                dkv_ref[:, ks] = carry_ref[:, ks] + dk[:BLK]
                dkv_ref[:, vs] = carry_ref[:, vs] + dv[:BLK]
                carry_ref[:, ks] = dk[BLK:]
                carry_ref[:, vs] = dv[BLK:]

        @pl.when(i == nb)
        def _():
            dkv_ref[...] = carry_ref[...]

    cur = lambda i: (jnp.minimum(i, nb - 1), 0)
    prev = lambda i: (jnp.clip(i - 1, 0, nb - 1), KV_COL_BLOCK)
    return _hosted(
        rider, body, grid=(nb + 1,), name="attn_bwd",
        in_specs=[
            pl.BlockSpec((BLK, D), cur),
            pl.BlockSpec((BLK, kvw), lambda i: (jnp.minimum(i, nb - 1), KV_COL_BLOCK)),
            pl.BlockSpec((BLK, kvw), prev),
            pl.BlockSpec((BLK, D), cur),
            pl.BlockSpec((BLK, D), cur),
            pl.BlockSpec((BLK, N_HEADS), cur),
            pl.BlockSpec(memory_space=pltpu.SMEM),
        ],
        out_specs=[
            pl.BlockSpec((BLK, D), cur),
            pl.BlockSpec((BLK, kvw), lambda i: (jnp.maximum(i - 1, 0), 0)),
            _acc((1, N_HEADS)),
        ],
        out_shape=[SDS((s, D), bf16), SDS((s, kvw), f32), SDS((1, N_HEADS), f32)],
        scratch_shapes=[pltpu.VMEM((BLK, kvw), f32)],
        compiler_params=_cp("arbitrary"),
    )(qkv, qkv, qkv, o, do, lse, sinks)


def qkv_bwd(dq, dkv, x, g, wt, dres, tm):
    s = x.shape[0]
    qd = N_HEADS * HD
    kvw = 2 * N_KV * HD

    def body(dq_ref, dkv_ref, x_ref, g_ref, wt_ref, dres_ref, dx_ref, dg_ref, db_ref, dkvb_ref):
        @pl.when(pl.program_id(0) == 0)
        def _():
            dg_ref[...] = jnp.zeros_like(dg_ref)
            db_ref[...] = jnp.zeros_like(db_ref)

        dq = dq_ref[...]
        dkv = dkv_ref[...]
        dkvb = dkv.astype(bf16)
        dkvb_ref[...] = dkvb
        dh = _dot(dq, wt_ref[0:qd, :]) + _dot(dkvb, wt_ref[qd:QKV, :])
        g = g_ref[...]
        _, xn, r = _rms_fwd(x_ref[...], g)
        dx, dg = _rms_bwd(xn, r, g, dh)
        dx_ref[...] = dres_ref[...] + dx
        dg_ref[...] += dg
        db_ref[:, 0:qd] += _colsum8(dq.astype(f32))
        db_ref[:, qd:QKV] += _colsum8(dkv)

    return pl.pallas_call(
        body, grid=(s // tm,), name="qkv_bwd",
        in_specs=[_row(tm, qd), _row(tm, kvw), _row(tm, D), _const((1, D)), _const((QKV, D)), _row(tm, D)],
        out_specs=[_row(tm, D), _acc((SUB, D)), _acc((SUB, QKV)), _row(tm, kvw)],
        out_shape=[SDS((s, D), f32), SDS((SUB, D), f32), SDS((SUB, QKV), f32), SDS((s, kvw), bf16)],
        compiler_params=_cp("arbitrary"),
    )(dq, dkv, x, g, wt, dres)


def ffn_fwd(x, g, wup, wdw, bdw, wdn, tm, cw, rider=None):
    s = x.shape[0]
    tail = 8

    def body(x_ref, g_ref, wup_ref, wdw_ref, bdw_ref, wdn_ref, xo_ref, h_ref, up_ref, gate_ref, act_ref, carry_ref,
             ext_ref):
        @pl.when(pl.program_id(0) == 0)
        def _():
            carry_ref[...] = jnp.zeros_like(carry_ref)

        x = x_ref[...]
        h, _, _ = _rms_fwd(x, g_ref[...])
        hb = h.astype(bf16)
        h_ref[...] = hb
        for c in range(DFF // cw):
            cs = slice(c * cw, (c + 1) * cw)
            vs = slice(DFF + c * cw, DFF + (c + 1) * cw)
            ug = _dot(hb, wup_ref[:, cs])
            uv = _dot(hb, wup_ref[:, vs])
            up_ref[:, cs] = ug.astype(bf16)
            up_ref[:, vs] = uv.astype(bf16)
            ext_ref[0:tail, :] = carry_ref[:, cs]
            ext_ref[tail:tail + tm, :] = ug
            carry_ref[:, cs] = ug[tm - tail:, :]
            ext = ext_ref[...]
            gate = (wdw_ref[0:1, cs] * pltpu.roll(ext, 2, 0)[tail:] + wdw_ref[1:2, cs] * pltpu.roll(ext, 1, 0)[tail:]
                    + wdw_ref[2:3, cs] * ug) + bdw_ref[:, cs]
            gate_ref[:, cs] = gate.astype(bf16)
            act_ref[:, cs] = (gate * _sigmoid(gate) * uv).astype(bf16)
        xo_ref[...] = x + _dot(act_ref[...], wdn_ref[...])

    return _hosted(
        rider, body, grid=(s // tm,), name="ffn_fwd",
        in_specs=[_row(tm, D), _const((1, D)), _const((D, 2 * DFF)), _const((3, DFF)), _const((1, DFF)),
                  _const((DFF, D))],
        out_specs=[_row(tm, D), _row(tm, D), _row(tm, 2 * DFF), _row(tm, DFF), _row(tm, DFF)],
        out_shape=[SDS((s, D), f32), SDS((s, D), bf16), SDS((s, 2 * DFF), bf16), SDS((s, DFF), bf16),
                   SDS((s, DFF), bf16)],
        scratch_shapes=[pltpu.VMEM((tail, DFF), f32), pltpu.VMEM((tail + tm, cw), f32)],
        compiler_params=_cp("arbitrary"),
    )(x, g, wup, wdw, bdw, wdn)


def ffn_bwd(dxo, x, g, up, gate, wdw, wdnt, wupt, tm, cw, rider=None):
    s = x.shape[0]
    nt = s // tm
    rev = lambda i: (nt - 1 - i, 0)

    def body(dxo_ref, x_ref, g_ref, up_ref, gate_ref, wdw_ref, wdnt_ref, wupt_ref,
             dxi_ref, dup_ref, dg_ref, dwdw_ref, dbdw_ref, carry_ref, ext2_ref):
        i = pl.program_id(0)

        @pl.when(i == 0)
        def _():
            carry_ref[...] = jnp.zeros_like(carry_ref)
            dg_ref[...] = jnp.zeros_like(dg_ref)
            dwdw_ref[...] = jnp.zeros_like(dwdw_ref)
            dbdw_ref[...] = jnp.zeros_like(dbdw_ref)

        dxo = dxo_ref[...]
        dxb = dxo.astype(bf16)
        for c in range(DFF // cw):
            cs = slice(c * cw, (c + 1) * cw)
            vs = slice(DFF + c * cw, DFF + (c + 1) * cw)
            d_act = _dot(dxb, wdnt_ref[:, cs])
            ug = up_ref[:, cs].astype(f32)
            uv = up_ref[:, vs].astype(f32)
            gate = gate_ref[:, cs].astype(f32)
            sg = _sigmoid(gate)
            dup_ref[:, vs] = (d_act * (gate * sg)).astype(bf16)
            d_gate = d_act * uv * _dsilu(gate, sg)
            ext2_ref[0:tm, :] = d_gate
            ext2_ref[tm:tm + 8, :] = carry_ref[:, cs]
            carry_ref[:, cs] = d_gate[0:8, :]
            ext = ext2_ref[...]
            ahead1 = pltpu.roll(ext, tm + 8 - 1, 0)[:tm]
            ahead2 = pltpu.roll(ext, tm + 8 - 2, 0)[:tm]
            dbdw_ref[:, cs] += _colsum8(d_gate)
            dwdw_ref[0, :, cs] += _colsum8(ahead2 * ug)
            dwdw_ref[1, :, cs] += _colsum8(ahead1 * ug)
            dwdw_ref[2, :, cs] += _colsum8(d_gate * ug)
            d_ug = wdw_ref[0:1, cs] * ahead2 + wdw_ref[1:2, cs] * ahead1 + wdw_ref[2:3, cs] * d_gate
            dup_ref[:, cs] = d_ug.astype(bf16)
        dh = _dot(dup_ref[...], wupt_ref[...])
        gv = g_ref[...]
        _, xn, r = _rms_fwd(x_ref[...], gv)
        dx, dg = _rms_bwd(xn, r, gv, dh)
        dxi_ref[...] = dxo + dx
        dg_ref[...] += dg

    return _hosted(
        rider, body, grid=(nt,), name="ffn_bwd",
        in_specs=[
            pl.BlockSpec((tm, D), rev), pl.BlockSpec((tm, D), rev), _const((1, D)),
            pl.BlockSpec((tm, 2 * DFF), rev), pl.BlockSpec((tm, DFF), rev),
            _const((3, DFF)), _const((D, DFF)), _const((2 * DFF, D)),
        ],
        out_specs=[pl.BlockSpec((tm, D), rev), pl.BlockSpec((tm, 2 * DFF), rev), _acc((SUB, D)),
                   _acc((3, SUB, DFF)), _acc((SUB, DFF))],
        out_shape=[SDS((s, D), f32), SDS((s, 2 * DFF), bf16), SDS((SUB, D), f32), SDS((3, SUB, DFF), f32),
                   SDS((SUB, DFF), f32)],
        scratch_shapes=[pltpu.VMEM((8, DFF), f32), pltpu.VMEM((tm + 8, cw), f32)],
        compiler_params=_cp("arbitrary"),
    )(dxo, x, g, up, gate, wdw, wdnt, wupt)


def pw1_fwd(x, g, w, b, tm):
    s = x.shape[0]

    def body(x_ref, g_ref, w_ref, b_ref, h_ref, a_ref, u_ref):
        h, _, _ = _rms_fwd(x_ref[...], g_ref[...])
        hb = h.astype(bf16)
        h_ref[...] = hb
        a = _dot(hb, w_ref[...]) + b_ref[...]
        a_ref[...] = a.astype(bf16)
        u_ref[...] = a[:, :D] * _sigmoid(a[:, D:])

    return pl.pallas_call(
        body, grid=(s // tm,), name="pw1_fwd",
        in_specs=[_row(tm, D), _const((1, D)), _const((D, 2 * D)), _const((1, 2 * D))],
        out_specs=[_row(tm, D), _row(tm, 2 * D), _row(tm, D)],
        out_shape=[SDS((s, D), bf16), SDS((s, 2 * D), bf16), SDS((s, D), f32)],
        compiler_params=_cp("parallel"),
    )(x, g, w, b)


def _ln_silu(c, lg, lb):
    mu = jnp.mean(c, axis=-1, keepdims=True)
    cc = c - mu
    var = jnp.mean(cc * cc, axis=-1, keepdims=True)
    rstd = lax.rsqrt(var + LN_EPS)
    xh = cc * rstd
    ln = xh * lg + lb
    sg = _sigmoid(ln)
    return xh, rstd, ln, sg


def _shifted_copies(ext_ref, sh_ref, cs, tm):
    n = CONV_HALO - SUB + tm
    for k in range(1, SUB):
        sh_ref[k - 1] = ext_ref[pl.ds(k, n), cs]


def _shifted_rows(ext_ref, sh_ref, cs, start, rows):
    q, k = divmod(start, SUB)
    if k == 0:
        return ext_ref[pl.ds(start, rows), cs]
    return sh_ref[k - 1, pl.ds(q * SUB, rows), :]


def conv_fwd(u, x, wdw, bdw, lg, lb, w2, b2, tm, rc):
    s = x.shape[0]
    hl = CONV_HALO
    off = hl - (CONV_W - 1)

    def body(u_ref, halo_ref, x_ref, wdw_ref, bdw_ref, lg_ref, lb_ref, w2_ref, b2_ref, c_ref, xo_ref, ext_ref, sh_ref):
        has_prev = (pl.program_id(0) > 0).astype(f32)
        ext_ref[0:hl, :] = halo_ref[...] * has_prev
        ext_ref[hl:hl + tm, :] = u_ref[...]
        for cc in range(D // LANES):
            cs = slice(cc * LANES, (cc + 1) * LANES)
            _shifted_copies(ext_ref, sh_ref, cs, tm)
            for rr in range(tm // rc):
                acc = jnp.zeros((rc, LANES), f32) + bdw_ref[:, cs]
                for j in range(CONV_W):
                    acc = acc + wdw_ref[j:j + 1, cs] * _shifted_rows(ext_ref, sh_ref, cs, rr * rc + off + j, rc)
                c_ref[rr * rc:(rr + 1) * rc, cs] = acc
        _, _, ln, sg = _ln_silu(c_ref[...], lg_ref[...], lb_ref[...])
        xo_ref[...] = x_ref[...] + _dot((ln * sg).astype(bf16), w2_ref[...]) + b2_ref[...]

    return pl.pallas_call(
        body, grid=(s // tm,), name="conv_fwd",
        in_specs=[_row(tm, D), pl.BlockSpec((hl, D), lambda i: (jnp.maximum(i * (tm // hl) - 1, 0), 0)), _row(tm, D),
                  _const((CONV_W, D)), _const((1, D)), _const((1, D)), _const((1, D)), _const((D, D)), _const((1, D))],
        out_specs=[_row(tm, D), _row(tm, D)],
        out_shape=[SDS((s, D), f32), SDS((s, D), f32)],
        scratch_shapes=[pltpu.VMEM((hl + tm, D), f32), pltpu.VMEM((SUB - 1, hl - SUB + tm, LANES), f32)],
        compiler_params=_cp("parallel"),
    )(u, u, x, wdw, bdw, lg, lb, w2, b2)


def conv_bwd_a(dy, c, lg, lb, w2t, tm, rider=None):
    s = dy.shape[0]

    def body(dy_ref, c_ref, lg_ref, lb_ref, w2t_ref, dc_ref, u3_ref, dlg_ref, dlb_ref, db2_ref, dbdw_ref):
        @pl.when(pl.program_id(0) == 0)
        def _():
            for r in (dlg_ref, dlb_ref, db2_ref, dbdw_ref):
                r[...] = jnp.zeros_like(r)

        dy = dy_ref[...]
        lg = lg_ref[...]
        xh, rstd, ln, sg = _ln_silu(c_ref[...], lg, lb_ref[...])
        u3_ref[...] = (ln * sg).astype(bf16)
        du3 = _dot(dy.astype(bf16), w2t_ref[...])
        dln = du3 * _dsilu(ln, sg)
        dxh = dln * lg
        dc = rstd * (dxh - jnp.mean(dxh, axis=-1, keepdims=True) - xh * jnp.mean(dxh * xh, axis=-1, keepdims=True))
        dc_ref[...] = dc
        dlg_ref[...] += _colsum8(dln * xh)
        dlb_ref[...] += _colsum8(dln)
        db2_ref[...] += _colsum8(dy)
        dbdw_ref[...] += _colsum8(dc)

    return _hosted(
        rider, body, grid=(s // tm,), name="conv_bwd_a",
        in_specs=[_row(tm, D), _row(tm, D), _const((1, D)), _const((1, D)), _const((D, D))],
        out_specs=[_row(tm, D), _row(tm, D)] + [_acc((SUB, D))] * 4,
        out_shape=[SDS((s, D), f32), SDS((s, D), bf16)] + [SDS((SUB, D), f32)] * 4,
        compiler_params=_cp("arbitrary"),
    )(dy, c, lg, lb, w2t)


def conv_bwd_b(dc, u, a, wdw, x, g, w1t, dres, tm, rc, rider=None):
    s = x.shape[0]
    nt = s // tm
    hl = CONV_HALO
    off = hl - (CONV_W - 1)

    def body(dc_ref, dnext_ref, u_ref, uprev_ref, a_ref, wdw_ref, x_ref, g_ref, w1t_ref, dres_ref,
             dx_ref, da_ref, dwdw_ref, db1_ref, dg_ref, ext_ref, ext2_ref, du_ref, sh_ref, sh2_ref):
        i = pl.program_id(0)

        @pl.when(i == 0)
        def _():
            for r in (dwdw_ref, db1_ref, dg_ref):
                r[...] = jnp.zeros_like(r)

        ext_ref[0:hl, :] = uprev_ref[...] * (i > 0).astype(f32)
        ext_ref[hl:hl + tm, :] = u_ref[...]
        ext2_ref[0:tm, :] = dc_ref[...]
        ext2_ref[tm:tm + hl, :] = dnext_ref[...] * (i < nt - 1).astype(f32)
        for cc in range(D // LANES):
            cs = slice(cc * LANES, (cc + 1) * LANES)
            _shifted_copies(ext_ref, sh_ref, cs, tm)
            _shifted_copies(ext2_ref, sh2_ref, cs, tm)
            for rr in range(tm // rc):
                r0 = rr * rc
                dcb = ext2_ref[r0:r0 + rc, cs]
                acc = jnp.zeros((rc, LANES), f32)
                for j in range(CONV_W):
                    acc = acc + wdw_ref[j:j + 1, cs] * _shifted_rows(ext2_ref, sh2_ref, cs, r0 + CONV_W - 1 - j, rc)
                    dwdw_ref[j, :, cs] += _colsum8(dcb * _shifted_rows(ext_ref, sh_ref, cs, r0 + off + j, rc))
                du_ref[r0:r0 + rc, cs] = acc
        du = du_ref[...]
        a1 = a_ref[:, :D].astype(f32)
        sg = _sigmoid(a_ref[:, D:].astype(f32))
        da1 = du * sg
        da2 = du * a1 * sg * (1.0 - sg)
        da_ref[:, :D] = da1.astype(bf16)
        da_ref[:, D:] = da2.astype(bf16)
        db1_ref[:, :D] += _colsum8(da1)
        db1_ref[:, D:] += _colsum8(da2)
        dh = _dot(da_ref[...], w1t_ref[...])
        gv = g_ref[...]
        _, xn, r = _rms_fwd(x_ref[...], gv)
        dx, dg = _rms_bwd(xn, r, gv, dh)
        dx_ref[...] = dres_ref[...] + dx
        dg_ref[...] += dg

    blocks = tm // hl
    return _hosted(
        rider, body, grid=(nt,), name="conv_bwd_b",
        in_specs=[
            _row(tm, D), pl.BlockSpec((hl, D), lambda i: (jnp.minimum((i + 1) * blocks, s // hl - 1), 0)),
            _row(tm, D), pl.BlockSpec((hl, D), lambda i: (jnp.maximum(i * blocks - 1, 0), 0)),
            _row(tm, 2 * D), _const((CONV_W, D)), _row(tm, D), _const((1, D)), _const((2 * D, D)), _row(tm, D),
        ],
        out_specs=[_row(tm, D), _row(tm, 2 * D), _acc((CONV_W, SUB, D)), _acc((SUB, 2 * D)), _acc((SUB, D))],
        out_shape=[SDS((s, D), f32), SDS((s, 2 * D), bf16), SDS((CONV_W, SUB, D), f32), SDS((SUB, 2 * D), f32),
                   SDS((SUB, D), f32)],
        scratch_shapes=[pltpu.VMEM((hl + tm, D), f32), pltpu.VMEM((tm + hl, D), f32), pltpu.VMEM((tm, D), f32),
                        pltpu.VMEM((SUB - 1, hl - SUB + tm, LANES), f32),
                        pltpu.VMEM((SUB - 1, hl - SUB + tm, LANES), f32)],
        compiler_params=_cp("arbitrary"),
    )(dc, dc, u, u, a, wdw, x, g, w1t, dres)


def loss_bwd(x, g, t, tm):
    s = x.shape[0]

    def body(x_ref, g_ref, t_ref, dx_ref, dg_ref, loss_ref):
        @pl.when(pl.program_id(0) == 0)
        def _():
            dg_ref[...] = jnp.zeros_like(dg_ref)
            loss_ref[...] = jnp.zeros_like(loss_ref)

        gv = g_ref[...]
        y, xn, r = _rms_fwd(x_ref[...], gv)
        e = y - t_ref[...]
        loss_ref[...] += 0.5 * jnp.sum(jnp.mean(e * e, axis=-1, keepdims=True), axis=0, keepdims=True)
        dx, dg = _rms_bwd(xn, r, gv, e / D)
        dx_ref[...] = dx
        dg_ref[...] += dg

    return pl.pallas_call(
        body, grid=(s // tm,), name="loss_bwd",
        in_specs=[_row(tm, D), _const((1, D)), _row(tm, D)],
        out_specs=[_row(tm, D), _acc((SUB, D)), _acc((1, 1))],
        out_shape=[SDS((s, D), f32), SDS((SUB, D), f32), SDS((1, 1), f32)],
        compiler_params=_cp("arbitrary"),
    )(x, g, t)


def wgrad(a, b, nb, tk, name, rider=None):
    s, k1 = a.shape
    n = b.shape[1]

    def body(a_ref, b_ref, o_ref):
        @pl.when(pl.program_id(1) == 0)
        def _():
            o_ref[...] = jnp.zeros_like(o_ref)

        o_ref[...] += lax.dot_general(a_ref[...], b_ref[...].astype(bf16), _TN, preferred_element_type=f32)

    return _hosted(
        rider, body, grid=(n // nb, s // tk), name=name,
        in_specs=[pl.BlockSpec((tk, k1), lambda j, k: (k, 0)), pl.BlockSpec((tk, nb), lambda j, k: (k, j))],
        out_specs=pl.BlockSpec((k1, nb), lambda j, k: (0, j)),
        out_shape=SDS((k1, n), f32),
        compiler_params=_cp("parallel", "arbitrary"),
    )(a, b)


def wgrad_cols(a, b, tk, name, rider=None):
    s, k1 = a.shape
    w = b.shape[1] // N_CHIPS

    def body(a_ref, b_ref, o_ref):
        @pl.when(pl.program_id(1) == 0)
        def _():
            o_ref[...] = jnp.zeros_like(o_ref)

        acc = lax.dot_general(a_ref[...], b_ref[...].astype(bf16), _TN, preferred_element_type=f32)
        o_ref[:, 0] += acc.reshape(2, k1 // 2, w)

    return _hosted(
        rider, body, grid=(N_CHIPS, s // tk), name=name,
        in_specs=[pl.BlockSpec((tk, k1), lambda j, k: (k, 0)), pl.BlockSpec((tk, w), lambda j, k: (k, j))],
        out_specs=pl.BlockSpec((2, 1, k1 // 2, w), lambda j, k: (0, j, 0, 0)),
        out_shape=SDS((2, N_CHIPS, k1 // 2, w), f32),
        compiler_params=_cp("parallel", "arbitrary"),
    )(a, b)


def wgrad_rows(a, b, nb, tk, name, rider=None):
    s, k1 = a.shape
    n = b.shape[1]
    r = k1 // (2 * N_CHIPS)

    def body(a_ref, b_ref, o_ref):
        @pl.when(pl.program_id(1) == 0)
        def _():
            o_ref[...] = jnp.zeros_like(o_ref)

        acc = lax.dot_general(a_ref[...], b_ref[...].astype(bf16), _TN, preferred_element_type=f32)
        for j in range(N_CHIPS):
            for h in range(2):
                o_ref[h, j] += acc[(2 * j + h) * r:(2 * j + h + 1) * r, :]

    return _hosted(
        rider, body, grid=(n // nb, s // tk), name=name,
        in_specs=[pl.BlockSpec((tk, k1), lambda j, k: (k, 0)), pl.BlockSpec((tk, nb), lambda j, k: (k, j))],
        out_specs=pl.BlockSpec((2, N_CHIPS, r, nb), lambda j, k: (0, 0, 0, j)),
        out_shape=SDS((2, N_CHIPS, r, n), f32),
        compiler_params=_cp("parallel", "arbitrary"),
    )(a, b)


def _adam_math(w, g, m, v):
    m = B1 * m + (1.0 - B1) * g
    v = B2 * v + (1.0 - B2) * (g * g)
    m_hat = m / (1.0 - B1 ** STEP)
    v_hat = v / (1.0 - B2 ** STEP)
    delta = -LR * (m_hat / (jnp.sqrt(v_hat) + ADAM_EPS) + WD * w)
    return delta, m, v


def _rows_tile(r, c, multiple=SUB):
    best = None
    for t in range(multiple, r + 1, multiple):
        if r % t == 0 and t * c * 4 <= ELEMENTWISE_BLOCK_BYTES:
            best = t
    return best if best is not None else r


def adamw(w, g, m, v, name):
    l, r, c = w.shape
    tr = _rows_tile(r, c)
    spec = pl.BlockSpec((1, tr, c), lambda i, j: (i, j, 0))

    def body(w_ref, g_ref, m_ref, v_ref, d_ref, mo_ref, vo_ref):
        d, mn, vn = _adam_math(w_ref[...], g_ref[...], m_ref[...], v_ref[...])
        d_ref[...] = d
        mo_ref[...] = mn
        vo_ref[...] = vn

    return pl.pallas_call(
        body, grid=(l, r // tr), name=name, in_specs=[spec] * 4, out_specs=[spec] * 3,
        out_shape=[SDS((l, r, c), f32)] * 3, compiler_params=_cp("parallel", "parallel"),
    )(w, g, m, v)


def _place():
    return lax.axis_index("x"), lax.axis_index("y"), lax.axis_index("c")


def _chip_peer(xi, yi, r):
    px = 1 - xi if r & 2 else xi
    py = 1 - yi if r & 1 else yi
    return px, py


def _gv_qkv(src, dst, j, h):
    rows = pl.ds(h * (D // 2), D // 2)
    return src.at[rows, :], dst.at[j, rows, :]


def _gv_rows(src, dst, j, h):
    r = src.shape[0] // 2
    return src.at[pl.ds(h * r, r), :], dst.at[pl.ds(j * 2 * r + h * r, r), :]


def _gv_cols(src, dst, j, h):
    r, w = src.shape[0] // 2, src.shape[1]
    return src.at[pl.ds(h * r, r), :], dst.at[pl.ds(h * r, r), pl.ds(j * w, w)]


def gather_rider(big, small, forward_at):
    nb, ns = len(big), len(small)
    n = nb + ns
    views = [v for _, v, _ in big]

    def env(ins, outs, sems):
        ici_send, ici_recv, d2d_send, d2d_recv, loc_sems = sems
        xi, yi, ci = _place()
        me = 2 * xi + yi

        def local(a, h):
            if a < nb:
                src, dst = views[a](ins[a], outs[a], me, h)
                return pltpu.make_async_copy(src, dst, loc_sems.at[2 * a + h])
            return pltpu.make_async_copy(ins[a], outs[a].at[me], loc_sems.at[nb + a])

        def ici(a, r, slot):
            px, py = _chip_peer(xi, yi, r)
            src, dst = views[a](ins[a], outs[a], slot, ci) if a < nb else (ins[a], outs[a].at[slot])
            k = 3 * a + r - 1
            return pltpu.make_async_remote_copy(src_ref=src, dst_ref=dst, send_sem=ici_send.at[k],
                                                recv_sem=ici_recv.at[k], device_id=(px, py, ci), device_id_type=MESH)

        def d2d(a, r, half):
            px, py = _chip_peer(xi, yi, r)
            _, dst = views[a](ins[a], outs[a], 2 * px + py, half)
            k = 3 * a + r - 1
            return pltpu.make_async_remote_copy(src_ref=dst, dst_ref=dst, send_sem=d2d_send.at[k],
                                                recv_sem=d2d_recv.at[k], device_id=(xi, yi, 1 - ci), device_id_type=MESH)

        return xi, yi, ci, me, local, ici, d2d

    def locals_of():
        return [(a, h) for a in range(nb) for h in range(2)] + [(a, 0) for a in range(nb, n)]

    def send(ins, outs, sems):
        _, _, _, me, local, ici, _ = env(ins, outs, sems)
        for a, h in locals_of():
            local(a, h).start()
        for a in range(n):
            for r in (1, 2, 3):
                ici(a, r, me).start()

    def forward(ins, outs, sems):
        xi, yi, ci, _, _, ici, d2d = env(ins, outs, sems)
        for a in range(n):
            for r in (1, 2, 3):
                px, py = _chip_peer(xi, yi, r)
                ici(a, r, 2 * px + py).wait_recv()
                if a < nb:
                    d2d(a, r, ci).start()

    def finish(ins, outs, sems):
        _, _, ci, me, local, ici, d2d = env(ins, outs, sems)
        for a in range(nb):
            for r in (1, 2, 3):
                d2d(a, r, 1 - ci).wait_recv()
        for a in range(n):
            for r in (1, 2, 3):
                ici(a, r, me).wait_send()
                if a < nb:
                    d2d(a, r, ci).wait_send()
        for a, h in locals_of():
            local(a, h).wait()

    dma = pltpu.SemaphoreType.DMA
    return Rider(
        ins=[b for b, _, _ in big] + list(small),
        out_shape=[SDS(shape, bf16) for _, _, shape in big] + [SDS((N_CHIPS,) + a.shape, a.dtype) for a in small],
        sem_shapes=[dma((3 * n,)), dma((3 * n,)), dma((max(3 * nb, 1),)), dma((max(3 * nb, 1),)), dma((2 * nb + ns,))],
        stages=[(0.0, send), (forward_at, forward)], final=finish)


def pair_send_rider(gs):
    n = len(gs)

    def copy(ins, outs, sems, a):
        xi, yi, ci = _place()
        return pltpu.make_async_remote_copy(
            src_ref=ins[a].at[1 - ci], dst_ref=outs[a], send_sem=sems[0].at[a], recv_sem=sems[1].at[a],
            device_id=(xi, yi, 1 - ci), device_id_type=MESH)

    def send(ins, outs, sems):
        for a in range(n):
            copy(ins, outs, sems, a).start()

    def finish(ins, outs, sems):
        for a in range(n):
            copy(ins, outs, sems, a).wait()

    dma = pltpu.SemaphoreType.DMA
    return Rider(ins=list(gs), out_shape=[SDS(g.shape[1:], g.dtype) for g in gs], sem_shapes=[dma((n,)), dma((n,))],
                 stages=[(0.0, send)], final=finish)


def chip_send_rider(ps):
    n = len(ps)

    def env(ins, outs, sems):
        xi, yi, ci = _place()
        me = 2 * xi + yi

        def local(a):
            return pltpu.make_async_copy(ins[a].at[me], outs[a].at[me], sems[2].at[a])

        def copy(a, r, src_slot, dst_slot):
            px, py = _chip_peer(xi, yi, r)
            k = 3 * a + r - 1
            return pltpu.make_async_remote_copy(
                src_ref=ins[a].at[src_slot], dst_ref=outs[a].at[dst_slot], send_sem=sems[0].at[k],
                recv_sem=sems[1].at[k], device_id=(px, py, ci), device_id_type=MESH)

        return xi, yi, me, local, copy

    def send(ins, outs, sems):
        xi, yi, me, local, copy = env(ins, outs, sems)
        for a in range(n):
            local(a).start()
            for r in (1, 2, 3):
                px, py = _chip_peer(xi, yi, r)
                copy(a, r, 2 * px + py, me).start()

    def finish(ins, outs, sems):
        xi, yi, me, local, copy = env(ins, outs, sems)
        for a in range(n):
            for r in (1, 2, 3):
                px, py = _chip_peer(xi, yi, r)
                copy(a, r, me, 2 * px + py).wait_recv()
        for a in range(n):
            for r in (1, 2, 3):
                px, py = _chip_peer(xi, yi, r)
                copy(a, r, 2 * px + py, me).wait_send()
            local(a).wait()

    dma = pltpu.SemaphoreType.DMA
    return Rider(ins=list(ps), out_shape=[SDS(p.shape, p.dtype) for p in ps],
                 sem_shapes=[dma((3 * n,)), dma((3 * n,)), dma((n,))], stages=[(0.0, send)], final=finish)


def exchange_rider(halves):
    n = len(halves)

    def env(ins, outs, sems):
        xi, yi, ci = _place()

        def local(a):
            return pltpu.make_async_copy(ins[a], outs[a].at[ci], sems[2].at[a])

        def copy(a, slot):
            return pltpu.make_async_remote_copy(
                src_ref=ins[a], dst_ref=outs[a].at[slot], send_sem=sems[0].at[a], recv_sem=sems[1].at[a],
                device_id=(xi, yi, 1 - ci), device_id_type=MESH)

        return ci, local, copy

    def send(ins, outs, sems):
        ci, local, copy = env(ins, outs, sems)
        for a in range(n):
            local(a).start()
            copy(a, ci).start()

    def finish(ins, outs, sems):
        ci, local, copy = env(ins, outs, sems)
        for a in range(n):
            copy(a, 1 - ci).wait_recv()
        for a in range(n):
            copy(a, ci).wait_send()
            local(a).wait()

    dma = pltpu.SemaphoreType.DMA
    return Rider(ins=list(halves), out_shape=[SDS((2,) + h.shape, h.dtype) for h in halves],
                 sem_shapes=[dma((n,)), dma((n,)), dma((n,))], stages=[(0.0, send)], final=finish)


def rs_pair_add(g, other, ci, name):
    _, nsh, r, w = g.shape
    tr = _rows_tile(r, w, 16)

    def body(c_ref, g_ref, o_ref, p_ref, pb_ref):
        p = g_ref[0] + o_ref[...]
        p_ref[...] = p
        pb_ref[...] = p.astype(bf16)

    blk = pl.BlockSpec((1, tr, w), lambda j, i, c: (j, i, 0))
    return pl.pallas_call(
        body, name=name,
        grid_spec=pltpu.PrefetchScalarGridSpec(
            num_scalar_prefetch=1, grid=(nsh, r // tr),
            in_specs=[pl.BlockSpec((1, 1, tr, w), lambda j, i, c: (c[0], j, i, 0)), blk], out_specs=[blk, blk]),
        out_shape=[SDS((nsh, r, w), f32), SDS((nsh, r, w), bf16)],
        compiler_params=_cp("parallel", "parallel"),
    )(ci.reshape(1).astype(jnp.int32), g, other)


def rs_chip_add(p32, parts, chip, name):
    nsh, r, w = parts.shape
    tr = _rows_tile(r, w, 16)

    def body(c_ref, p_ref, parts_ref, o_ref):
        me = c_ref[0]
        acc = None
        for j in range(N_CHIPS):
            term = jnp.where(me == j, p_ref[0], parts_ref[j].astype(f32))
            acc = term if acc is None else acc + term
        o_ref[...] = acc

    return pl.pallas_call(
        body, name=name,
        grid_spec=pltpu.PrefetchScalarGridSpec(
            num_scalar_prefetch=1, grid=(r // tr,),
            in_specs=[pl.BlockSpec((1, tr, w), lambda i, c: (c[0], i, 0)),
                      pl.BlockSpec((nsh, tr, w), lambda i, c: (0, i, 0))],
            out_specs=pl.BlockSpec((tr, w), lambda i, c: (i, 0))),
        out_shape=SDS((r, w), f32),
        compiler_params=_cp("parallel"),
    )(chip.reshape(1).astype(jnp.int32), p32, parts)


class GradReduction:
    def __init__(self, keys, grads, ci, chip):
        self.keys, self.grads, self.ci, self.chip = keys, grads, ci, chip

    def pair_rider(self):
        return pair_send_rider(self.grads)

    def chip_rider(self, from_pair):
        both = [rs_pair_add(g, o, self.ci, "rs_pair_add_" + k) for k, g, o in zip(self.keys, self.grads, from_pair)]
        self.p32 = [p for p, _ in both]
        return chip_send_rider([pb for _, pb in both])

    def exchange_rider(self, parts):
        return exchange_rider([rs_chip_add(p, q, self.chip, "rs_chip_add_" + k)
                               for k, p, q in zip(self.keys, self.p32, parts)])

    def alone(self):
        parts = run_rider(self.chip_rider(run_rider(self.pair_rider(), "rs_pair_" + self.keys[0])), "rs_chip_" + self.keys[0])
        return run_rider(self.exchange_rider(parts), "rs_swap_" + self.keys[0])


STAGE_W = DFF
_ST = {"norm_mix0": 0, "norm_mix1": 1, "attn_b_qkv": 2, "attn_sinks": 3, "attn_b_o": 4, "final_norm": 5, "loss": 6,
       "norm_ffn0": 8, "norm_ffn1": 9, "ffn_b_dw0": 10, "ffn_b_dw1": 11, "conv_b_pw1": 12, "conv_b_dw": 13,
       "conv_ln_g": 14, "conv_ln_b": 15, "conv_b_pw2": 16, "ffn_w_dw0": 17, "ffn_w_dw1": 20, "conv_w_dw": 24}
STAGE_ROWS = 56
SMALL_REP = ("norm_mix", "attn_b_qkv", "attn_sinks", "attn_b_o", "norm_ffn", "ffn_b_dw", "final_norm")
SMALL_SH = ("conv_b_pw1", "conv_w_dw", "conv_b_dw", "conv_ln_g", "conv_ln_b", "conv_b_pw2", "ffn_w_dw")
_SMALL_PARTS = ("norm_mix0", "norm_mix1", "attn_b_qkv", "attn_sinks", "attn_b_o", "norm_ffn0", "norm_ffn1", "ffn_b_dw0",
                "ffn_b_dw1", "final_norm", "conv_b_pw1", "conv_b_dw", "conv_ln_g", "conv_ln_b", "conv_b_pw2", "loss",
                "ffn_w_dw0", "ffn_w_dw1", "conv_w_dw")


def small_reduce_adamw(parts, w, m, v):
    names = SMALL_REP + SMALL_SH
    npart, nw = len(_SMALL_PARTS), len(names)

    def body(*refs):
        part = dict(zip(_SMALL_PARTS, refs[:npart]))
        off = npart
        w_ref = dict(zip(names, refs[off:off + nw]))
        m_ref = dict(zip(names, refs[off + nw:off + 2 * nw]))
        v_ref = dict(zip(names, refs[off + 2 * nw:off + 3 * nw]))
        off += 3 * nw
        loss_ref = refs[off]
        g_out = dict(zip(names, refs[off + 1:off + 1 + nw]))
        d_out = dict(zip(names, refs[off + 1 + nw:off + 1 + 2 * nw]))
        m_out = dict(zip(names, refs[off + 1 + 2 * nw:off + 1 + 3 * nw]))
        v_out = dict(zip(names, refs[off + 1 + 3 * nw:off + 1 + 4 * nw]))
        stage_ref, buf_ref, tot_ref, send_sems, recv_sems = refs[off + 1 + 4 * nw:]

        xi, yi, ci = _place()
        me = 4 * xi + 2 * yi + ci
        chip = 2 * xi + yi

        stage_ref[...] = jnp.zeros_like(stage_ref)
        for name in _SMALL_PARTS:
            ref, r0 = part[name], _ST[name]
            if name in ("attn_sinks", "loss"):
                val = ref[...]
            elif name in ("ffn_w_dw0", "ffn_w_dw1", "conv_w_dw"):
                val = jnp.sum(ref[...], axis=1)
            else:
                val = jnp.sum(ref[...], axis=0, keepdims=True)
            stage_ref[r0:r0 + val.shape[0], 0:val.shape[1]] = val

        buf_ref[me] = stage_ref[...]

        def peer(r):
            px, py = _chip_peer(xi, yi, r >> 1)
            return px, py, (1 - ci if r & 1 else ci)

        def copy(r, slot):
            return pltpu.make_async_remote_copy(
                src_ref=stage_ref, dst_ref=buf_ref.at[slot], send_sem=send_sems.at[r - 1], recv_sem=recv_sems.at[r - 1],
                device_id=peer(r), device_id_type=MESH)

        sends = []
        for r in range(1, N_DEV):
            cp = copy(r, me)
            cp.start()
            sends.append(cp)
        for r in range(1, N_DEV):
            px, py, pc = peer(r)
            copy(r, 4 * px + 2 * py + pc).wait_recv()
        for cp in sends:
            cp.wait_send()
        acc = buf_ref[0]
        for d in range(1, N_DEV):
            acc = acc + buf_ref[d]
        tot_ref[...] = acc

        def rows(name, n, width):
            r0 = _ST[name]
            return tot_ref[r0:r0 + n, 0:width]

        def mine(name, n, width):
            r0 = _ST[name]
            out = tot_ref[r0:r0 + n, 0:width]
            for j in range(1, N_CHIPS):
                out = jnp.where(chip == j, tot_ref[r0:r0 + n, j * width:(j + 1) * width], out)
            return out

        loss_ref[...] = rows("loss", 1, 1)
        grads = {
            "norm_mix": rows("norm_mix0", 2, D), "attn_b_qkv": rows("attn_b_qkv", 1, QKV),
            "attn_sinks": rows("attn_sinks", 1, N_HEADS), "attn_b_o": rows("attn_b_o", 1, D),
            "norm_ffn": rows("norm_ffn0", 2, D), "ffn_b_dw": rows("ffn_b_dw0", 2, DFF),
            "final_norm": rows("final_norm", 1, D),
            "conv_b_pw1": mine("conv_b_pw1", 1, 2 * D // N_CHIPS), "conv_w_dw": mine("conv_w_dw", CONV_W, D // N_CHIPS),
            "conv_b_dw": mine("conv_b_dw", 1, D // N_CHIPS), "conv_ln_g": mine("conv_ln_g", 1, D // N_CHIPS),
            "conv_ln_b": mine("conv_ln_b", 1, D // N_CHIPS), "conv_b_pw2": mine("conv_b_pw2", 1, D // N_CHIPS),
        }
        for name in names:
            if name == "ffn_w_dw":
                continue
            at = 0 if name == "conv_w_dw" else Ellipsis
            g = grads[name]
            d, mn, vn = _adam_math(w_ref[name][at], g, m_ref[name][at], v_ref[name][at])
            g_out[name][at] = g
            d_out[name][at] = d
            m_out[name][at] = mn
            v_out[name][at] = vn
        for layer, key in enumerate(("ffn_w_dw0", "ffn_w_dw1")):
            g = mine(key, 3, DFF // N_CHIPS)
            d, mn, vn = _adam_math(w_ref["ffn_w_dw"][layer], g, m_ref["ffn_w_dw"][layer], v_ref["ffn_w_dw"][layer])
            g_out["ffn_w_dw"][layer] = g
            d_out["ffn_w_dw"][layer] = d
            m_out["ffn_w_dw"][layer] = mn
            v_out["ffn_w_dw"][layer] = vn

    ins = [parts[k] for k in _SMALL_PARTS] + [src[k] for src in (w, m, v) for k in names]
    wshapes = [SDS(w[k].shape, f32) for k in names]
    outs = pl.pallas_call(
        body, name="small_reduce_adamw", in_specs=[_VMEM] * len(ins), out_specs=[_VMEM] * (1 + 4 * nw),
        out_shape=[SDS((1, 1), f32)] + wshapes * 4,
        scratch_shapes=[pltpu.VMEM((STAGE_ROWS, STAGE_W), f32), pltpu.VMEM((N_DEV, STAGE_ROWS, STAGE_W), f32),
                        pltpu.VMEM((STAGE_ROWS, STAGE_W), f32), pltpu.SemaphoreType.DMA((N_DEV - 1,)),
                        pltpu.SemaphoreType.DMA((N_DEV - 1,))],
        compiler_params=pltpu.CompilerParams(vmem_limit_bytes=VMEM_LIMIT),
    )(*ins)
    loss = outs[0]
    g, d, mn, vn = (dict(zip(names, outs[1 + k * nw:1 + (k + 1) * nw])) for k in range(4))
    return loss, g, d, mn, vn


TM = 512
TM_FFN = 256
FFN_CHUNK = 256
CONV_ROWS = 128
TK = 1024
FORWARD_AT = 0.6


def kernel(x, norm_mix, attn_w_qkv, attn_b_qkv, attn_sinks, attn_w_o, attn_b_o, conv_w_pw1, conv_b_pw1, conv_w_dw, conv_b_dw, conv_ln_g, conv_ln_b, conv_w_pw2, conv_b_pw2, norm_ffn, ffn_w_up, ffn_w_dw, ffn_b_dw, ffn_w_down, final_norm, loss_target, m_norm_mix, m_attn_w_qkv, m_attn_b_qkv, m_attn_sinks, m_attn_w_o, m_attn_b_o, m_conv_w_pw1, m_conv_b_pw1, m_conv_w_dw, m_conv_b_dw, m_conv_ln_g, m_conv_ln_b, m_conv_w_pw2, m_conv_b_pw2, m_norm_ffn, m_ffn_w_up, m_ffn_w_dw, m_ffn_b_dw, m_ffn_w_down, m_final_norm, v_norm_mix, v_attn_w_qkv, v_attn_b_qkv, v_attn_sinks, v_attn_w_o, v_attn_b_o, v_conv_w_pw1, v_conv_b_pw1, v_conv_w_dw, v_conv_b_dw, v_conv_ln_g, v_conv_ln_b, v_conv_w_pw2, v_conv_b_pw2, v_norm_ffn, v_ffn_w_up, v_ffn_w_dw, v_ffn_b_dw, v_ffn_w_down, v_final_norm):
    w = dict(norm_mix=norm_mix, attn_w_qkv=attn_w_qkv, attn_b_qkv=attn_b_qkv, attn_sinks=attn_sinks, attn_w_o=attn_w_o,
             attn_b_o=attn_b_o, conv_w_pw1=conv_w_pw1, conv_b_pw1=conv_b_pw1, conv_w_dw=conv_w_dw, conv_b_dw=conv_b_dw,
             conv_ln_g=conv_ln_g, conv_ln_b=conv_ln_b, conv_w_pw2=conv_w_pw2, conv_b_pw2=conv_b_pw2, norm_ffn=norm_ffn,
             ffn_w_up=ffn_w_up, ffn_w_dw=ffn_w_dw, ffn_b_dw=ffn_b_dw, ffn_w_down=ffn_w_down, final_norm=final_norm)
    mom = dict(norm_mix=m_norm_mix, attn_w_qkv=m_attn_w_qkv, attn_b_qkv=m_attn_b_qkv, attn_sinks=m_attn_sinks,
               attn_w_o=m_attn_w_o, attn_b_o=m_attn_b_o, conv_w_pw1=m_conv_w_pw1, conv_b_pw1=m_conv_b_pw1,
               conv_w_dw=m_conv_w_dw, conv_b_dw=m_conv_b_dw, conv_ln_g=m_conv_ln_g, conv_ln_b=m_conv_ln_b,
               conv_w_pw2=m_conv_w_pw2, conv_b_pw2=m_conv_b_pw2, norm_ffn=m_norm_ffn, ffn_w_up=m_ffn_w_up,
               ffn_w_dw=m_ffn_w_dw, ffn_b_dw=m_ffn_b_dw, ffn_w_down=m_ffn_w_down, final_norm=m_final_norm)
    vel = dict(norm_mix=v_norm_mix, attn_w_qkv=v_attn_w_qkv, attn_b_qkv=v_attn_b_qkv, attn_sinks=v_attn_sinks,
               attn_w_o=v_attn_w_o, attn_b_o=v_attn_b_o, conv_w_pw1=v_conv_w_pw1, conv_b_pw1=v_conv_b_pw1,
               conv_w_dw=v_conv_w_dw, conv_b_dw=v_conv_b_dw, conv_ln_g=v_conv_ln_g, conv_ln_b=v_conv_ln_b,
               conv_w_pw2=v_conv_w_pw2, conv_b_pw2=v_conv_b_pw2, norm_ffn=v_norm_ffn, ffn_w_up=v_ffn_w_up,
               ffn_w_dw=v_ffn_w_dw, ffn_b_dw=v_ffn_b_dw, ffn_w_down=v_ffn_w_down, final_norm=v_final_norm)
    order = ("norm_mix", "attn_w_qkv", "attn_b_qkv", "attn_sinks", "attn_w_o", "attn_b_o", "conv_w_pw1", "conv_b_pw1",
             "conv_w_dw", "conv_b_dw", "conv_ln_g", "conv_ln_b", "conv_w_pw2", "conv_b_pw2", "norm_ffn", "ffn_w_up",
             "ffn_w_dw", "ffn_b_dw", "ffn_w_down", "final_norm")
    xi, yi, ci = _place()
    chip = 2 * xi + yi
    xs, target = x[0], loss_target[0]
    s = xs.shape[0]
    tm, tmf, tk = min(TM, s), min(TM_FFN, s), min(TK, s)
    row = lambda v: v.reshape(1, -1)
    join = lambda a, axis: jnp.concatenate([a[j] for j in range(N_CHIPS)], axis=axis)
    cast = lambda a: a.astype(bf16)
    small, big = {}, {}

    got = run_rider(gather_rider(
        [(cast(attn_w_qkv[0]), _gv_qkv, (N_CHIPS, D, QKV // N_CHIPS)), (cast(attn_w_o[0]), _gv_rows, (D, D))],
        [w[k] for k in SMALL_SH], 0.0), "gather_attn")
    qkv4, w_o = got[:2]
    sm = dict(zip(SMALL_SH, got[2:]))
    w_qkv = jnp.transpose(qkv4, (1, 0, 2)).reshape(D, QKV)
    sinks = attn_sinks.reshape(N_HEADS)
    b_pw1, conv_dw, conv_bdw = join(sm["conv_b_pw1"], 1), join(sm["conv_w_dw"], 2)[0], join(sm["conv_b_dw"], 1)
    ln_g, ln_b, b_pw2, ffn_dw = (join(sm["conv_ln_g"], 1), join(sm["conv_ln_b"], 1), join(sm["conv_b_pw2"], 1),
                                 join(sm["ffn_w_dw"], 2))

    h0, qkv = qkv_fwd(xs, row(norm_mix[0]), w_qkv, attn_b_qkv, tm)
    (o, lse), (w_up0, w_dn0) = attn_fwd(qkv, sinks, rider=gather_rider(
        [(cast(ffn_w_up[0]), _gv_cols, (D, 2 * DFF)), (cast(ffn_w_down[0]), _gv_rows, (DFF, D))], [], FORWARD_AT))
    x1 = attn_out_fwd(xs, o, w_o, attn_b_o, tm)
    (x2, h1, up0, gate0, act0), (w_pw1, w_pw2, w_up1, w_dn1) = ffn_fwd(
        x1, row(norm_ffn[0]), w_up0, ffn_dw[0], row(ffn_b_dw[0]), w_dn0, tmf, FFN_CHUNK, rider=gather_rider(
            [(cast(conv_w_pw1[0]), _gv_cols, (D, 2 * D)), (cast(conv_w_pw2[0]), _gv_rows, (D, D)),
             (cast(ffn_w_up[1]), _gv_cols, (D, 2 * DFF)), (cast(ffn_w_down[1]), _gv_rows, (DFF, D))], [], FORWARD_AT))
    h2, a, u = pw1_fwd(x2, row(norm_mix[1]), w_pw1, b_pw1, tm)
    c, x3 = conv_fwd(u, x2, conv_dw, conv_bdw, ln_g, ln_b, w_pw2, b_pw2, tm, CONV_ROWS)
    x4, h3, up1, gate1, act1 = ffn_fwd(x3, row(norm_ffn[1]), w_up1, ffn_dw[1], row(ffn_b_dw[1]), w_dn1, tmf, FFN_CHUNK)

    dx4, small["final_norm"], small["loss"] = loss_bwd(x4, final_norm.reshape(1, D), target, tm)
    dx3, dup1, small["norm_ffn1"], small["ffn_w_dw1"], small["ffn_b_dw1"] = ffn_bwd(
        dx4, x3, row(norm_ffn[1]), up1, gate1, ffn_dw[1], w_dn1.T, w_up1.T, tmf, FFN_CHUNK)
    red1 = GradReduction(("up1", "down1"), [wgrad_cols(h3, dup1, tk, "wgrad_up1"),
                                           wgrad_rows(act1, dx4, 512, tk, "wgrad_down1")], ci, chip)
    (dc, u3, small["conv_ln_g"], small["conv_ln_b"], small["conv_b_pw2"], small["conv_b_dw"]), from_pair = conv_bwd_a(
        dx3, c, ln_g, ln_b, w_pw2.T, tm, rider=red1.pair_rider())
    g_pw2 = wgrad_rows(u3, dx3, 512, tk, "wgrad_pw2")
    (dx2, da, small["conv_w_dw"], small["conv_b_pw1"], small["norm_mix1"]), parts = conv_bwd_b(
        dc, u, a, conv_dw, x2, row(norm_mix[1]), w_pw1.T, dx3, tm, CONV_ROWS, rider=red1.chip_rider(from_pair))
    g_pw1, swapped = wgrad_cols(h2, da, tk, "wgrad_pw1", rider=red1.exchange_rider(parts))
    big.update(zip(red1.keys, swapped))

    red2 = GradReduction(("pw1", "pw2"), [g_pw1, g_pw2], ci, chip)
    (dx1, dup0, small["norm_ffn0"], small["ffn_w_dw0"], small["ffn_b_dw0"]), from_pair = ffn_bwd(
        dx2, x1, row(norm_ffn[0]), up0, gate0, ffn_dw[0], w_dn0.T, w_up0.T, tmf, FFN_CHUNK, rider=red2.pair_rider())
    g_up0, parts = wgrad_cols(h1, dup0, tk, "wgrad_up0", rider=red2.chip_rider(from_pair))
    g_dn0, swapped = wgrad_rows(act0, dx2, 512, tk, "wgrad_down0", rider=red2.exchange_rider(parts))
    big.update(zip(red2.keys, swapped))

    red3 = GradReduction(("up0", "down0", "wo"), [g_up0, g_dn0, wgrad_rows(o, dx1, 512, tk, "wgrad_o")], ci, chip)
    (do, small["attn_b_o"]), from_pair = attn_out_bwd(dx1, w_o.T, tm, rider=red3.pair_rider())
    (dq, dkv, small["attn_sinks"]), parts = attn_bwd(qkv, o, do, lse, sinks, rider=red3.chip_rider(from_pair))
    dx0, small["norm_mix0"], small["attn_b_qkv"], dkvb = qkv_bwd(dq, dkv, xs, row(norm_mix[0]), w_qkv.T, dx1, tm)
    g_q, swapped = wgrad(h0, dq, 512, tk, "wgrad_q", rider=red3.exchange_rider(parts))
    big.update(zip(red3.keys, swapped))
    g_qkv = jnp.concatenate([g_q, wgrad(h0, dkvb, 2 * N_KV * HD, tk, "wgrad_kv")], axis=1)
    g_qkv = jnp.transpose(g_qkv.reshape(2, D // 2, N_CHIPS, QKV // N_CHIPS), (0, 2, 1, 3))
    red4 = GradReduction(("qkv",), [g_qkv], ci, chip)
    big.update(zip(red4.keys, red4.alone()))

    gbig = {
        "attn_w_qkv": big["qkv"].reshape(1, D, QKV // N_CHIPS), "attn_w_o": big["wo"].reshape(1, D // N_CHIPS, D),
        "conv_w_pw1": big["pw1"].reshape(1, D, 2 * D // N_CHIPS), "conv_w_pw2": big["pw2"].reshape(1, D // N_CHIPS, D),
        "ffn_w_up": jnp.stack([big["up0"], big["up1"]]).reshape(2, D, 2 * DFF // N_CHIPS),
        "ffn_w_down": jnp.stack([big["down0"], big["down1"]]).reshape(2, DFF // N_CHIPS, D),
    }

    fix = lambda d: {**d, "final_norm": d["final_norm"].reshape(1, D)}
    loss, gs, ds, ms, vs = small_reduce_adamw(small, fix(w), fix(mom), fix(vel))
    unfix = lambda d: {**d, "final_norm": d["final_norm"].reshape(D)}
    gout, delta, new_m, new_v = unfix(gs), unfix(ds), unfix(ms), unfix(vs)

    for name, g in gbig.items():
        gout[name] = g
        delta[name], new_m[name], new_v[name] = adamw(w[name], g, mom[name], vel[name], "adamw_" + name)

    return (loss.reshape(()), dx0[None], *[gout[n] for n in order], *[delta[n] for n in order],
            *[new_m[n] for n in order], *[new_v[n] for n in order])
```

```python
import math

import jax
import jax.numpy as jnp
from jax import lax
from jax.experimental import pallas as pl
from jax.experimental.pallas import tpu as pltpu

f32 = jnp.float32
bf16 = jnp.bfloat16
SDS = jax.ShapeDtypeStruct
MESH = pl.DeviceIdType.MESH

D = 1024
N_HEADS = 16
N_KV = 2
GROUP = 8
HD = 64
BLK = 128
QKV = (N_HEADS + 2 * N_KV) * HD
KV_COL_BLOCK = (N_HEADS * HD) // (2 * N_KV * HD)
CONV_W = 31
CONV_HALO = 32
DFF = 2816
RMS_EPS = 1e-6
LN_EPS = 1e-5
LR, B1, B2, ADAM_EPS, WD, STEP = 0.001, 0.9, 0.999, 1e-08, 0.01, 10

N_CHIPS = 4
N_DEV = 8
VMEM_LIMIT = 56 * 1024 * 1024
LANES = 128
SUB = 8
ELEMENTWISE_BLOCK_BYTES = 1 << 20


def _cp(*sem):
    return pltpu.CompilerParams(dimension_semantics=sem, vmem_limit_bytes=VMEM_LIMIT)


def _row(tm, n):
    return pl.BlockSpec((tm, n), lambda i: (i, 0))


def _const(shape):
    return pl.BlockSpec(shape, lambda *_: (0,) * len(shape), pipeline_mode=pl.Buffered(1))


def _acc(shape):
    return pl.BlockSpec(shape, lambda *_: (0,) * len(shape))


def _rms_fwd(x, g):
    r = lax.rsqrt(jnp.mean(x * x, axis=-1, keepdims=True) + RMS_EPS)
    xn = x * r
    return xn * g, xn, r


def _colsum8(v):
    return jnp.sum(v.reshape(v.shape[0] // SUB, SUB, v.shape[1]), axis=0)


def _rms_bwd(xn, r, g, dh):
    dyn = dh * g
    dx = r * (dyn - xn * jnp.mean(dyn * xn, axis=-1, keepdims=True))
    return dx, _colsum8(dh * xn)


def _sigmoid(z):
    return 0.5 * jnp.tanh(0.5 * z) + 0.5


def _dsilu(z, sg):
    return sg * (1.0 + z * (1.0 - sg))


def _dot(a, b):
    return jnp.dot(a, b, preferred_element_type=f32)


_ANY = pl.BlockSpec(memory_space=pl.ANY)
_VMEM = pl.BlockSpec(memory_space=pltpu.VMEM)


class Rider:
    def __init__(self, ins, out_shape, sem_shapes, stages, final):
        self.ins, self.out_shape, self.sem_shapes, self.stages, self.final = ins, out_shape, sem_shapes, stages, final


def run_rider(rider, name):
    n_in, n_out = len(rider.ins), len(rider.out_shape)

    def body(*refs):
        parts = refs[:n_in], refs[n_in:n_in + n_out], refs[n_in + n_out:]
        for _, fn in rider.stages:
            fn(*parts)
        rider.final(*parts)

    return pl.pallas_call(
        body, name=name, in_specs=[_ANY] * n_in, out_specs=[_ANY] * n_out, out_shape=list(rider.out_shape),
        scratch_shapes=list(rider.sem_shapes),
    )(*rider.ins)


def _hosted(rider, body, *, grid, in_specs, out_specs, out_shape, name, compiler_params, scratch_shapes=()):
    if rider is None:
        return pl.pallas_call(body, grid=grid, in_specs=in_specs, out_specs=out_specs, out_shape=out_shape, name=name,
                              compiler_params=compiler_params, scratch_shapes=list(scratch_shapes))
    single = not isinstance(out_shape, (list, tuple))
    shapes = [out_shape] if single else list(out_shape)
    specs = [out_specs] if single else list(out_specs)
    n_in, n_out, n_sc = len(in_specs), len(shapes), len(scratch_shapes)
    r_in, r_out = len(rider.ins), len(rider.out_shape)
    total = math.prod(grid)

    def wrapped(*refs):
        own_in, refs = refs[:n_in], refs[n_in:]
        r_ins, refs = refs[:r_in], refs[r_in:]
        own_out, refs = refs[:n_out], refs[n_out:]
        r_outs, refs = refs[:r_out], refs[r_out:]
        own_sc, r_sems = refs[:n_sc], refs[n_sc:]
        step = 0
        for d, n in enumerate(grid):
            step = step * n + pl.program_id(d)
        for frac, fn in rider.stages:
            @pl.when(step == min(int(frac * total), total - 1))
            def _(fn=fn):
                fn(r_ins, r_outs, r_sems)

        body(*own_in, *own_out, *own_sc)

        @pl.when(step == total - 1)
        def _():
            rider.final(r_ins, r_outs, r_sems)

    call = pl.pallas_call(
        wrapped, grid=grid, in_specs=list(in_specs) + [_ANY] * r_in, out_specs=specs + [_ANY] * r_out,
        out_shape=shapes + list(rider.out_shape), scratch_shapes=list(scratch_shapes) + list(rider.sem_shapes),
        name=name, compiler_params=_cp(*(("arbitrary",) * len(grid))))

    def run(*args):
        res = call(*args, *rider.ins)
        own = res[:n_out]
        return (own[0] if single else own), res[n_out:]

    return run


def qkv_fwd(x, g, w, b, tm):
    s = x.shape[0]

    def body(x_ref, g_ref, w_ref, b_ref, h_ref, o_ref):
        h, _, _ = _rms_fwd(x_ref[...], g_ref[...])
        hb = h.astype(bf16)
        h_ref[...] = hb
        o_ref[...] = (_dot(hb, w_ref[...]) + b_ref[...]).astype(bf16)

    return pl.pallas_call(
        body, grid=(s // tm,), name="qkv_fwd",
        in_specs=[_row(tm, D), _const((1, D)), _const((D, QKV)), _const((1, QKV))],
        out_specs=[_row(tm, D), _row(tm, QKV)],
        out_shape=[SDS((s, D), bf16), SDS((s, QKV), bf16)],
        compiler_params=_cp("parallel"),
    )(x, g, w, b)


def _band_mask(i):
    qi = lax.broadcasted_iota(jnp.int32, (GROUP * BLK, 2 * BLK), 0) & (BLK - 1)
    ki = lax.broadcasted_iota(jnp.int32, (GROUP * BLK, 2 * BLK), 1)
    dist = qi + BLK - ki
    return (dist >= 0) & (dist < BLK) & ((ki >= BLK) | (i > 0))


_NEG = float(jnp.finfo(jnp.float32).min)
_NT = (((1,), (1,)), ((), ()))
_TN = (((0,), (0,)), ((), ()))


def _kv_heads(kvp_ref, kvc_ref, kvh):
    ks = slice(kvh * HD, (kvh + 1) * HD)
    vs = slice(N_KV * HD + kvh * HD, N_KV * HD + (kvh + 1) * HD)
    k = jnp.concatenate([kvp_ref[:, ks], kvc_ref[:, ks]], axis=0)
    v = jnp.concatenate([kvp_ref[:, vs], kvc_ref[:, vs]], axis=0)
    return k, v


def _stack_group(ref, kvh, width=HD):
    return jnp.concatenate([ref[:, (kvh * GROUP + gi) * width:(kvh * GROUP + gi + 1) * width] for gi in range(GROUP)],
                           axis=0)


def _group_sinks(sink_ref, kvh):
    row = lax.broadcasted_iota(jnp.int32, (GROUP * BLK, 1), 0)
    col = jnp.zeros((GROUP * BLK, 1), f32)
    for g in range(GROUP):
        col = jnp.where((row >= g * BLK) & (row < (g + 1) * BLK), sink_ref[kvh * GROUP + g], col)
    return col


def attn_fwd(qkv, sinks, rider=None):
    s = qkv.shape[0]
    scale = 1.0 / math.sqrt(HD)

    def body(q_ref, kvc_ref, kvp_ref, sink_ref, o_ref, lse_ref):
        valid = _band_mask(pl.program_id(0))
        for kvh in range(N_KV):
            k, v = _kv_heads(kvp_ref, kvc_ref, kvh)
            sc = lax.dot_general(_stack_group(q_ref, kvh), k, _NT, preferred_element_type=f32) * scale
            sc = jnp.where(valid, sc, _NEG)
            sink = _group_sinks(sink_ref, kvh)
            m = jnp.maximum(jnp.max(sc, axis=-1, keepdims=True), sink)
            p = jnp.exp(sc - m)
            denom = jnp.sum(p, axis=-1, keepdims=True) + jnp.exp(sink - m)
            og = _dot((p / denom).astype(bf16), v).astype(bf16)
            lse = m + jnp.log(denom)
            for gi in range(GROUP):
                h = kvh * GROUP + gi
                o_ref[:, h * HD:(h + 1) * HD] = og[gi * BLK:(gi + 1) * BLK]
                lse_ref[:, h:h + 1] = lse[gi * BLK:(gi + 1) * BLK]

    return _hosted(
        rider, body, grid=(s // BLK,), name="attn_fwd",
        in_specs=[
            pl.BlockSpec((BLK, N_HEADS * HD), lambda i: (i, 0)),
            pl.BlockSpec((BLK, 2 * N_KV * HD), lambda i: (i, KV_COL_BLOCK)),
            pl.BlockSpec((BLK, 2 * N_KV * HD), lambda i: (jnp.maximum(i - 1, 0), KV_COL_BLOCK)),
            pl.BlockSpec(memory_space=pltpu.SMEM),
        ],
        out_specs=[_row(BLK, D), _row(BLK, N_HEADS)],
        out_shape=[SDS((s, D), bf16), SDS((s, N_HEADS), f32)],
        compiler_params=_cp("parallel"),
    )(qkv, qkv, qkv, sinks)


def attn_out_fwd(x, o, w, b, tm):
    s = x.shape[0]

    def body(x_ref, o_ref, w_ref, b_ref, y_ref):
        y_ref[...] = x_ref[...] + _dot(o_ref[...], w_ref[...]) + b_ref[...]

    return pl.pallas_call(
        body, grid=(s // tm,), name="attn_out_fwd",
        in_specs=[_row(tm, D), _row(tm, D), _const((D, D)), _const((1, D))],
        out_specs=_row(tm, D), out_shape=SDS((s, D), f32),
        compiler_params=_cp("parallel"),
    )(x, o, w, b)


def attn_out_bwd(dy, wt, tm, rider=None):
    s = dy.shape[0]

    def body(dy_ref, wt_ref, do_ref, db_ref):
        @pl.when(pl.program_id(0) == 0)
        def _():
            db_ref[...] = jnp.zeros_like(db_ref)

        dy = dy_ref[...]
        do_ref[...] = _dot(dy.astype(bf16), wt_ref[...]).astype(bf16)
        db_ref[...] += _colsum8(dy)

    return _hosted(
        rider, body, grid=(s // tm,), name="attn_out_bwd",
        in_specs=[_row(tm, D), _const((D, D))],
        out_specs=[_row(tm, D), _acc((SUB, D))],
        out_shape=[SDS((s, D), bf16), SDS((SUB, D), f32)],
        compiler_params=_cp("arbitrary"),
    )(dy, wt)


def attn_bwd(qkv, o, do, lse, sinks, rider=None):
    s = qkv.shape[0]
    nb = s // BLK
    scale = 1.0 / math.sqrt(HD)
    kvw = 2 * N_KV * HD

    def body(q_ref, kvc_ref, kvp_ref, o_ref, do_ref, lse_ref, sink_ref, dq_ref, dkv_ref, ds_ref, carry_ref):
        i = pl.program_id(0)

        @pl.when(i == 0)
        def _():
            ds_ref[...] = jnp.zeros_like(ds_ref)
            carry_ref[...] = jnp.zeros_like(carry_ref)

        @pl.when(i < nb)
        def _():
            valid = _band_mask(i)
            for kvh in range(N_KV):
                k, v = _kv_heads(kvp_ref, kvc_ref, kvh)
                qg = _stack_group(q_ref, kvh)
                dog = _stack_group(do_ref, kvh)
                lse = _stack_group(lse_ref, kvh, 1)
                sc = lax.dot_general(qg, k, _NT, preferred_element_type=f32) * scale
                sc = jnp.where(valid, sc, _NEG)
                p = jnp.exp(sc - lse)
                dp = lax.dot_general(dog, v, _NT, preferred_element_type=f32)
                dlt = jnp.sum(dog.astype(f32) * _stack_group(o_ref, kvh).astype(f32), axis=-1, keepdims=True)
                dsc = (p * (dp - dlt)).astype(bf16)
                dqg = (_dot(dsc, k) * scale).astype(bf16)
                dk = lax.dot_general(dsc, qg, _TN, preferred_element_type=f32) * scale
                dv = lax.dot_general(p.astype(bf16), dog, _TN, preferred_element_type=f32)
                dsink = jnp.exp(_group_sinks(sink_ref, kvh) - lse) * dlt
                for gi in range(GROUP):
                    h = kvh * GROUP + gi
                    dq_ref[:, h * HD:(h + 1) * HD] = dqg[gi * BLK:(gi + 1) * BLK]
                    ds_ref[:, h:h + 1] += -jnp.sum(dsink[gi * BLK:(gi + 1) * BLK], axis=0, keepdims=True)
                ks = slice(kvh * HD, (kvh + 1) * HD)
                vs = slice(N_KV * HD + kvh * HD, N_KV * HD + (kvh + 1) * HD)
                dkv_ref[:, ks] = carry_ref[:, ks] + dk[:BLK]
                dkv_ref[:, vs] = carry_ref[:, vs] + dv[:BLK]
                carry_ref[:, ks] = dk[BLK:]
                carry_ref[:, vs] = dv[BLK:]

        @pl.when(i == nb)
        def _():
            dkv_ref[...] = carry_ref[...]

    cur = lambda i: (jnp.minimum(i, nb - 1), 0)
    prev = lambda i: (jnp.clip(i - 1, 0, nb - 1), KV_COL_BLOCK)
    return _hosted(
        rider, body, grid=(nb + 1,), name="attn_bwd",
        in_specs=[
            pl.BlockSpec((BLK, D), cur),
            pl.BlockSpec((BLK, kvw), lambda i: (jnp.minimum(i, nb - 1), KV_COL_BLOCK)),
            pl.BlockSpec((BLK, kvw), prev),
            pl.BlockSpec((BLK, D), cur),
            pl.BlockSpec((BLK, D), cur),
            pl.BlockSpec((BLK, N_HEADS), cur),
            pl.BlockSpec(memory_space=pltpu.SMEM),
        ],
        out_specs=[
            pl.BlockSpec((BLK, D), cur),
            pl.BlockSpec((BLK, kvw), lambda i: (jnp.maximum(i - 1, 0), 0)),
            _acc((1, N_HEADS)),
        ],
        out_shape=[SDS((s, D), bf16), SDS((s, kvw), f32), SDS((1, N_HEADS), f32)],
        scratch_shapes=[pltpu.VMEM((BLK, kvw), f32)],
        compiler_params=_cp("arbitrary"),
    )(qkv, qkv, qkv, o, do, lse, sinks)


def qkv_bwd(dq, dkv, x, g, wt, dres, tm):
    s = x.shape[0]
    qd = N_HEADS * HD
    kvw = 2 * N_KV * HD

    def body(dq_ref, dkv_ref, x_ref, g_ref, wt_ref, dres_ref, dx_ref, dg_ref, db_ref, dkvb_ref):
        @pl.when(pl.program_id(0) == 0)
        def _():
            dg_ref[...] = jnp.zeros_like(dg_ref)
            db_ref[...] = jnp.zeros_like(db_ref)

        dq = dq_ref[...]
        dkv = dkv_ref[...]
        dkvb = dkv.astype(bf16)
        dkvb_ref[...] = dkvb
        dh = _dot(dq, wt_ref[0:qd, :]) + _dot(dkvb, wt_ref[qd:QKV, :])
        g = g_ref[...]
        _, xn, r = _rms_fwd(x_ref[...], g)
        dx, dg = _rms_bwd(xn, r, g, dh)
        dx_ref[...] = dres_ref[...] + dx
        dg_ref[...] += dg
        db_ref[:, 0:qd] += _colsum8(dq.astype(f32))
        db_ref[:, qd:QKV] += _colsum8(dkv)

    return pl.pallas_call(
        body, grid=(s // tm,), name="qkv_bwd",
        in_specs=[_row(tm, qd), _row(tm, kvw), _row(tm, D), _const((1, D)), _const((QKV, D)), _row(tm, D)],
        out_specs=[_row(tm, D), _acc((SUB, D)), _acc((SUB, QKV)), _row(tm, kvw)],
        out_shape=[SDS((s, D), f32), SDS((SUB, D), f32), SDS((SUB, QKV), f32), SDS((s, kvw), bf16)],
        compiler_params=_cp("arbitrary"),
    )(dq, dkv, x, g, wt, dres)


def ffn_fwd(x, g, wup, wdw, bdw, wdn, tm, cw, rider=None):
    s = x.shape[0]
    tail = 8

    def body(x_ref, g_ref, wup_ref, wdw_ref, bdw_ref, wdn_ref, xo_ref, h_ref, up_ref, gate_ref, act_ref, carry_ref,
             ext_ref):
        @pl.when(pl.program_id(0) == 0)
        def _():
            carry_ref[...] = jnp.zeros_like(carry_ref)

        x = x_ref[...]
        h, _, _ = _rms_fwd(x, g_ref[...])
        hb = h.astype(bf16)
        h_ref[...] = hb
        for c in range(DFF // cw):
            cs = slice(c * cw, (c + 1) * cw)
            vs = slice(DFF + c * cw, DFF + (c + 1) * cw)
            ug = _dot(hb, wup_ref[:, cs])
            uv = _dot(hb, wup_ref[:, vs])
            up_ref[:, cs] = ug.astype(bf16)
            up_ref[:, vs] = uv.astype(bf16)
            ext_ref[0:tail, :] = carry_ref[:, cs]
            ext_ref[tail:tail + tm, :] = ug
            carry_ref[:, cs] = ug[tm - tail:, :]
            ext = ext_ref[...]
            gate = (wdw_ref[0:1, cs] * pltpu.roll(ext, 2, 0)[tail:] + wdw_ref[1:2, cs] * pltpu.roll(ext, 1, 0)[tail:]
                    + wdw_ref[2:3, cs] * ug) + bdw_ref[:, cs]
            gate_ref[:, cs] = gate.astype(bf16)
            act_ref[:, cs] = (gate * _sigmoid(gate) * uv).astype(bf16)
        xo_ref[...] = x + _dot(act_ref[...], wdn_ref[...])

    return _hosted(
        rider, body, grid=(s // tm,), name="ffn_fwd",
        in_specs=[_row(tm, D), _const((1, D)), _const((D, 2 * DFF)), _const((3, DFF)), _const((1, DFF)),
                  _const((DFF, D))],
        out_specs=[_row(tm, D), _row(tm, D), _row(tm, 2 * DFF), _row(tm, DFF), _row(tm, DFF)],
        out_shape=[SDS((s, D), f32), SDS((s, D), bf16), SDS((s, 2 * DFF), bf16), SDS((s, DFF), bf16),
                   SDS((s, DFF), bf16)],
        scratch_shapes=[pltpu.VMEM((tail, DFF), f32), pltpu.VMEM((tail + tm, cw), f32)],
        compiler_params=_cp("arbitrary"),
    )(x, g, wup, wdw, bdw, wdn)


def ffn_bwd(dxo, x, g, up, gate, wdw, wdnt, wupt, tm, cw, rider=None):
    s = x.shape[0]
    nt = s // tm
    rev = lambda i: (nt - 1 - i, 0)

    def body(dxo_ref, x_ref, g_ref, up_ref, gate_ref, wdw_ref, wdnt_ref, wupt_ref,
             dxi_ref, dup_ref, dg_ref, dwdw_ref, dbdw_ref, carry_ref, ext2_ref):
        i = pl.program_id(0)

        @pl.when(i == 0)
        def _():
            carry_ref[...] = jnp.zeros_like(carry_ref)
            dg_ref[...] = jnp.zeros_like(dg_ref)
            dwdw_ref[...] = jnp.zeros_like(dwdw_ref)
            dbdw_ref[...] = jnp.zeros_like(dbdw_ref)

        dxo = dxo_ref[...]
        dxb = dxo.astype(bf16)
        for c in range(DFF // cw):
            cs = slice(c * cw, (c + 1) * cw)
            vs = slice(DFF + c * cw, DFF + (c + 1) * cw)
            d_act = _dot(dxb, wdnt_ref[:, cs])
            ug = up_ref[:, cs].astype(f32)
            uv = up_ref[:, vs].astype(f32)
            gate = gate_ref[:, cs].astype(f32)
            sg = _sigmoid(gate)
            dup_ref[:, vs] = (d_act * (gate * sg)).astype(bf16)
            d_gate = d_act * uv * _dsilu(gate, sg)
            ext2_ref[0:tm, :] = d_gate
            ext2_ref[tm:tm + 8, :] = carry_ref[:, cs]
            carry_ref[:, cs] = d_gate[0:8, :]
            ext = ext2_ref[...]
            ahead1 = pltpu.roll(ext, tm + 8 - 1, 0)[:tm]
            ahead2 = pltpu.roll(ext, tm + 8 - 2, 0)[:tm]
            dbdw_ref[:, cs] += _colsum8(d_gate)
            dwdw_ref[0, :, cs] += _colsum8(ahead2 * ug)
            dwdw_ref[1, :, cs] += _colsum8(ahead1 * ug)
            dwdw_ref[2, :, cs] += _colsum8(d_gate * ug)
            d_ug = wdw_ref[0:1, cs] * ahead2 + wdw_ref[1:2, cs] * ahead1 + wdw_ref[2:3, cs] * d_gate
            dup_ref[:, cs] = d_ug.astype(bf16)
        dh = _dot(dup_ref[...], wupt_ref[...])
        gv = g_ref[...]
        _, xn, r = _rms_fwd(x_ref[...], gv)
        dx, dg = _rms_bwd(xn, r, gv, dh)
        dxi_ref[...] = dxo + dx
        dg_ref[...] += dg

    return _hosted(
        rider, body, grid=(nt,), name="ffn_bwd",
        in_specs=[
            pl.BlockSpec((tm, D), rev), pl.BlockSpec((tm, D), rev), _const((1, D)),
            pl.BlockSpec((tm, 2 * DFF), rev), pl.BlockSpec((tm, DFF), rev),
            _const((3, DFF)), _const((D, DFF)), _const((2 * DFF, D)),
        ],
        out_specs=[pl.BlockSpec((tm, D), rev), pl.BlockSpec((tm, 2 * DFF), rev), _acc((SUB, D)),
                   _acc((3, SUB, DFF)), _acc((SUB, DFF))],
        out_shape=[SDS((s, D), f32), SDS((s, 2 * DFF), bf16), SDS((SUB, D), f32), SDS((3, SUB, DFF), f32),
                   SDS((SUB, DFF), f32)],
        scratch_shapes=[pltpu.VMEM((8, DFF), f32), pltpu.VMEM((tm + 8, cw), f32)],
        compiler_params=_cp("arbitrary"),
    )(dxo, x, g, up, gate, wdw, wdnt, wupt)


def pw1_fwd(x, g, w, b, tm):
    s = x.shape[0]

    def body(x_ref, g_ref, w_ref, b_ref, h_ref, a_ref, u_ref):
        h, _, _ = _rms_fwd(x_ref[...], g_ref[...])
        hb = h.astype(bf16)
        h_ref[...] = hb
        a = _dot(hb, w_ref[...]) + b_ref[...]
        a_ref[...] = a.astype(bf16)
        u_ref[...] = a[:, :D] * _sigmoid(a[:, D:])

    return pl.pallas_call(
        body, grid=(s // tm,), name="pw1_fwd",
        in_specs=[_row(tm, D), _const((1, D)), _const((D, 2 * D)), _const((1, 2 * D))],
        out_specs=[_row(tm, D), _row(tm, 2 * D), _row(tm, D)],
        out_shape=[SDS((s, D), bf16), SDS((s, 2 * D), bf16), SDS((s, D), f32)],
        compiler_params=_cp("parallel"),
    )(x, g, w, b)


def _ln_silu(c, lg, lb):
    mu = jnp.mean(c, axis=-1, keepdims=True)
    cc = c - mu
    var = jnp.mean(cc * cc, axis=-1, keepdims=True)
    rstd = lax.rsqrt(var + LN_EPS)
    xh = cc * rstd
    ln = xh * lg + lb
    sg = _sigmoid(ln)
    return xh, rstd, ln, sg


def _shifted_copies(ext_ref, sh_ref, cs, tm):
    n = CONV_HALO - SUB + tm
    for k in range(1, SUB):
        sh_ref[k - 1] = ext_ref[pl.ds(k, n), cs]


def _shifted_rows(ext_ref, sh_ref, cs, start, rows):
    q, k = divmod(start, SUB)
    if k == 0:
        return ext_ref[pl.ds(start, rows), cs]
    return sh_ref[k - 1, pl.ds(q * SUB, rows), :]


def conv_fwd(u, x, wdw, bdw, lg, lb, w2, b2, tm, rc):
    s = x.shape[0]
    hl = CONV_HALO
    off = hl - (CONV_W - 1)

    def body(u_ref, halo_ref, x_ref, wdw_ref, bdw_ref, lg_ref, lb_ref, w2_ref, b2_ref, c_ref, xo_ref, ext_ref, sh_ref):
        has_prev = (pl.program_id(0) > 0).astype(f32)
        ext_ref[0:hl, :] = halo_ref[...] * has_prev
        ext_ref[hl:hl + tm, :] = u_ref[...]
        for cc in range(D // LANES):
            cs = slice(cc * LANES, (cc + 1) * LANES)
            _shifted_copies(ext_ref, sh_ref, cs, tm)
            for rr in range(tm // rc):
                acc = jnp.zeros((rc, LANES), f32) + bdw_ref[:, cs]
                for j in range(CONV_W):
                    acc = acc + wdw_ref[j:j + 1, cs] * _shifted_rows(ext_ref, sh_ref, cs, rr * rc + off + j, rc)
                c_ref[rr * rc:(rr + 1) * rc, cs] = acc
        _, _, ln, sg = _ln_silu(c_ref[...], lg_ref[...], lb_ref[...])
        xo_ref[...] = x_ref[...] + _dot((ln * sg).astype(bf16), w2_ref[...]) + b2_ref[...]

    return pl.pallas_call(
        body, grid=(s // tm,), name="conv_fwd",
        in_specs=[_row(tm, D), pl.BlockSpec((hl, D), lambda i: (jnp.maximum(i * (tm // hl) - 1, 0), 0)), _row(tm, D),
                  _const((CONV_W, D)), _const((1, D)), _const((1, D)), _const((1, D)), _const((D, D)), _const((1, D))],
        out_specs=[_row(tm, D), _row(tm, D)],
        out_shape=[SDS((s, D), f32), SDS((s, D), f32)],
        scratch_shapes=[pltpu.VMEM((hl + tm, D), f32), pltpu.VMEM((SUB - 1, hl - SUB + tm, LANES), f32)],
        compiler_params=_cp("parallel"),
    )(u, u, x, wdw, bdw, lg, lb, w2, b2)


def conv_bwd_a(dy, c, lg, lb, w2t, tm, rider=None):
    s = dy.shape[0]

    def body(dy_ref, c_ref, lg_ref, lb_ref, w2t_ref, dc_ref, u3_ref, dlg_ref, dlb_ref, db2_ref, dbdw_ref):
        @pl.when(pl.program_id(0) == 0)
        def _():
            for r in (dlg_ref, dlb_ref, db2_ref, dbdw_ref):
                r[...] = jnp.zeros_like(r)

        dy = dy_ref[...]
        lg = lg_ref[...]
        xh, rstd, ln, sg = _ln_silu(c_ref[...], lg, lb_ref[...])
        u3_ref[...] = (ln * sg).astype(bf16)
        du3 = _dot(dy.astype(bf16), w2t_ref[...])
        dln = du3 * _dsilu(ln, sg)
        dxh = dln * lg
        dc = rstd * (dxh - jnp.mean(dxh, axis=-1, keepdims=True) - xh * jnp.mean(dxh * xh, axis=-1, keepdims=True))
        dc_ref[...] = dc
        dlg_ref[...] += _colsum8(dln * xh)
        dlb_ref[...] += _colsum8(dln)
        db2_ref[...] += _colsum8(dy)
        dbdw_ref[...] += _colsum8(dc)

    return _hosted(
        rider, body, grid=(s // tm,), name="conv_bwd_a",
        in_specs=[_row(tm, D), _row(tm, D), _const((1, D)), _const((1, D)), _const((D, D))],
        out_specs=[_row(tm, D), _row(tm, D)] + [_acc((SUB, D))] * 4,
        out_shape=[SDS((s, D), f32), SDS((s, D), bf16)] + [SDS((SUB, D), f32)] * 4,
        compiler_params=_cp("arbitrary"),
    )(dy, c, lg, lb, w2t)


def conv_bwd_b(dc, u, a, wdw, x, g, w1t, dres, tm, rc, rider=None):
    s = x.shape[0]
    nt = s // tm
    hl = CONV_HALO
    off = hl - (CONV_W - 1)

    def body(dc_ref, dnext_ref, u_ref, uprev_ref, a_ref, wdw_ref, x_ref, g_ref, w1t_ref, dres_ref,
             dx_ref, da_ref, dwdw_ref, db1_ref, dg_ref, ext_ref, ext2_ref, du_ref, sh_ref, sh2_ref):
        i = pl.program_id(0)

        @pl.when(i == 0)
        def _():
            for r in (dwdw_ref, db1_ref, dg_ref):
                r[...] = jnp.zeros_like(r)

        ext_ref[0:hl, :] = uprev_ref[...] * (i > 0).astype(f32)
        ext_ref[hl:hl + tm, :] = u_ref[...]
        ext2_ref[0:tm, :] = dc_ref[...]
        ext2_ref[tm:tm + hl, :] = dnext_ref[...] * (i < nt - 1).astype(f32)
        for cc in range(D // LANES):
            cs = slice(cc * LANES, (cc + 1) * LANES)
            _shifted_copies(ext_ref, sh_ref, cs, tm)
            _shifted_copies(ext2_ref, sh2_ref, cs, tm)
            for rr in range(tm // rc):
                r0 = rr * rc
                dcb = ext2_ref[r0:r0 + rc, cs]
                acc = jnp.zeros((rc, LANES), f32)
                for j in range(CONV_W):
                    acc = acc + wdw_ref[j:j + 1, cs] * _shifted_rows(ext2_ref, sh2_ref, cs, r0 + CONV_W - 1 - j, rc)
                    dwdw_ref[j, :, cs] += _colsum8(dcb * _shifted_rows(ext_ref, sh_ref, cs, r0 + off + j, rc))
                du_ref[r0:r0 + rc, cs] = acc
        du = du_ref[...]
        a1 = a_ref[:, :D].astype(f32)
        sg = _sigmoid(a_ref[:, D:].astype(f32))
        da1 = du * sg
        da2 = du * a1 * sg * (1.0 - sg)
        da_ref[:, :D] = da1.astype(bf16)
        da_ref[:, D:] = da2.astype(bf16)
        db1_ref[:, :D] += _colsum8(da1)
        db1_ref[:, D:] += _colsum8(da2)
        dh = _dot(da_ref[...], w1t_ref[...])
        gv = g_ref[...]
        _, xn, r = _rms_fwd(x_ref[...], gv)
        dx, dg = _rms_bwd(xn, r, gv, dh)
        dx_ref[...] = dres_ref[...] + dx
        dg_ref[...] += dg

    blocks = tm // hl
    return _hosted(
        rider, body, grid=(nt,), name="conv_bwd_b",
        in_specs=[
            _row(tm, D), pl.BlockSpec((hl, D), lambda i: (jnp.minimum((i + 1) * blocks, s // hl - 1), 0)),
            _row(tm, D), pl.BlockSpec((hl, D), lambda i: (jnp.maximum(i * blocks - 1, 0), 0)),
            _row(tm, 2 * D), _const((CONV_W, D)), _row(tm, D), _const((1, D)), _const((2 * D, D)), _row(tm, D),
        ],
        out_specs=[_row(tm, D), _row(tm, 2 * D), _acc((CONV_W, SUB, D)), _acc((SUB, 2 * D)), _acc((SUB, D))],
        out_shape=[SDS((s, D), f32), SDS((s, 2 * D), bf16), SDS((CONV_W, SUB, D), f32), SDS((SUB, 2 * D), f32),
                   SDS((SUB, D), f32)],
        scratch_shapes=[pltpu.VMEM((hl + tm, D), f32), pltpu.VMEM((tm + hl, D), f32), pltpu.VMEM((tm, D), f32),
                        pltpu.VMEM((SUB - 1, hl - SUB + tm, LANES), f32),
                        pltpu.VMEM((SUB - 1, hl - SUB + tm, LANES), f32)],
        compiler_params=_cp("arbitrary"),
    )(dc, dc, u, u, a, wdw, x, g, w1t, dres)


def loss_bwd(x, g, t, tm):
    s = x.shape[0]

    def body(x_ref, g_ref, t_ref, dx_ref, dg_ref, loss_ref):
        @pl.when(pl.program_id(0) == 0)
        def _():
            dg_ref[...] = jnp.zeros_like(dg_ref)
            loss_ref[...] = jnp.zeros_like(loss_ref)

        gv = g_ref[...]
        y, xn, r = _rms_fwd(x_ref[...], gv)
        e = y - t_ref[...]
        loss_ref[...] += 0.5 * jnp.sum(jnp.mean(e * e, axis=-1, keepdims=True), axis=0, keepdims=True)
        dx, dg = _rms_bwd(xn, r, gv, e / D)
        dx_ref[...] = dx
        dg_ref[...] += dg

    return pl.pallas_call(
        body, grid=(s // tm,), name="loss_bwd",
        in_specs=[_row(tm, D), _const((1, D)), _row(tm, D)],
        out_specs=[_row(tm, D), _acc((SUB, D)), _acc((1, 1))],
        out_shape=[SDS((s, D), f32), SDS((SUB, D), f32), SDS((1, 1), f32)],
        compiler_params=_cp("arbitrary"),
    )(x, g, t)


def wgrad(a, b, nb, tk, name, rider=None):
    s, k1 = a.shape
    n = b.shape[1]

    def body(a_ref, b_ref, o_ref):
        @pl.when(pl.program_id(1) == 0)
        def _():
            o_ref[...] = jnp.zeros_like(o_ref)

        o_ref[...] += lax.dot_general(a_ref[...], b_ref[...].astype(bf16), _TN, preferred_element_type=f32)

    return _hosted(
        rider, body, grid=(n // nb, s // tk), name=name,
        in_specs=[pl.BlockSpec((tk, k1), lambda j, k: (k, 0)), pl.BlockSpec((tk, nb), lambda j, k: (k, j))],
        out_specs=pl.BlockSpec((k1, nb), lambda j, k: (0, j)),
        out_shape=SDS((k1, n), f32),
        compiler_params=_cp("parallel", "arbitrary"),
    )(a, b)


def wgrad_cols(a, b, tk, name, rider=None):
    s, k1 = a.shape
    w = b.shape[1] // N_CHIPS

    def body(a_ref, b_ref, o_ref):
        @pl.when(pl.program_id(1) == 0)
        def _():
            o_ref[...] = jnp.zeros_like(o_ref)

        acc = lax.dot_general(a_ref[...], b_ref[...].astype(bf16), _TN, preferred_element_type=f32)
        o_ref[:, 0] += acc.reshape(2, k1 // 2, w)

    return _hosted(
        rider, body, grid=(N_CHIPS, s // tk), name=name,
        in_specs=[pl.BlockSpec((tk, k1), lambda j, k: (k, 0)), pl.BlockSpec((tk, w), lambda j, k: (k, j))],
        out_specs=pl.BlockSpec((2, 1, k1 // 2, w), lambda j, k: (0, j, 0, 0)),
        out_shape=SDS((2, N_CHIPS, k1 // 2, w), f32),
        compiler_params=_cp("parallel", "arbitrary"),
    )(a, b)


def wgrad_rows(a, b, nb, tk, name, rider=None):
    s, k1 = a.shape
    n = b.shape[1]
    r = k1 // (2 * N_CHIPS)

    def body(a_ref, b_ref, o_ref):
        @pl.when(pl.program_id(1) == 0)
        def _():
            o_ref[...] = jnp.zeros_like(o_ref)

        acc = lax.dot_general(a_ref[...], b_ref[...].astype(bf16), _TN, preferred_element_type=f32)
        for j in range(N_CHIPS):
            for h in range(2):
                o_ref[h, j] += acc[(2 * j + h) * r:(2 * j + h + 1) * r, :]

    return _hosted(
        rider, body, grid=(n // nb, s // tk), name=name,
        in_specs=[pl.BlockSpec((tk, k1), lambda j, k: (k, 0)), pl.BlockSpec((tk, nb), lambda j, k: (k, j))],
        out_specs=pl.BlockSpec((2, N_CHIPS, r, nb), lambda j, k: (0, 0, 0, j)),
        out_shape=SDS((2, N_CHIPS, r, n), f32),
        compiler_params=_cp("parallel", "arbitrary"),
    )(a, b)


def _adam_math(w, g, m, v):
    m = B1 * m + (1.0 - B1) * g
    v = B2 * v + (1.0 - B2) * (g * g)
    m_hat = m / (1.0 - B1 ** STEP)
    v_hat = v / (1.0 - B2 ** STEP)
    delta = -LR * (m_hat / (jnp.sqrt(v_hat) + ADAM_EPS) + WD * w)
    return delta, m, v


def _rows_tile(r, c, multiple=SUB):
    best = None
    for t in range(multiple, r + 1, multiple):
        if r % t == 0 and t * c * 4 <= ELEMENTWISE_BLOCK_BYTES:
            best = t
    return best if best is not None else r


def adamw(w, g, m, v, name):
    l, r, c = w.shape
    tr = _rows_tile(r, c)
    spec = pl.BlockSpec((1, tr, c), lambda i, j: (i, j, 0))

    def body(w_ref, g_ref, m_ref, v_ref, d_ref, mo_ref, vo_ref):
        d, mn, vn = _adam_math(w_ref[...], g_ref[...], m_ref[...], v_ref[...])
        d_ref[...] = d
        mo_ref[...] = mn
        vo_ref[...] = vn

    return pl.pallas_call(
        body, grid=(l, r // tr), name=name, in_specs=[spec] * 4, out_specs=[spec] * 3,
        out_shape=[SDS((l, r, c), f32)] * 3, compiler_params=_cp("parallel", "parallel"),
    )(w, g, m, v)


def _place():
    return lax.axis_index("x"), lax.axis_index("y"), lax.axis_index("c")


def _chip_peer(xi, yi, r):
    px = 1 - xi if r & 2 else xi
    py = 1 - yi if r & 1 else yi
    return px, py


def _gv_qkv(src, dst, j, h):
    rows = pl.ds(h * (D // 2), D // 2)
    return src.at[rows, :], dst.at[j, rows, :]


def _gv_rows(src, dst, j, h):
    r = src.shape[0] // 2
    return src.at[pl.ds(h * r, r), :], dst.at[pl.ds(j * 2 * r + h * r, r), :]


def _gv_cols(src, dst, j, h):
    r, w = src.shape[0] // 2, src.shape[1]
    return src.at[pl.ds(h * r, r), :], dst.at[pl.ds(h * r, r), pl.ds(j * w, w)]


def gather_rider(big, small, forward_at):
    nb, ns = len(big), len(small)
    n = nb + ns
    views = [v for _, v, _ in big]

    def env(ins, outs, sems):
        ici_send, ici_recv, d2d_send, d2d_recv, loc_sems = sems
        xi, yi, ci = _place()
        me = 2 * xi + yi

        def local(a, h):
            if a < nb:
                src, dst = views[a](ins[a], outs[a], me, h)
                return pltpu.make_async_copy(src, dst, loc_sems.at[2 * a + h])
            return pltpu.make_async_copy(ins[a], outs[a].at[me], loc_sems.at[nb + a])

        def ici(a, r, slot):
            px, py = _chip_peer(xi, yi, r)
            src, dst = views[a](ins[a], outs[a], slot, ci) if a < nb else (ins[a], outs[a].at[slot])
            k = 3 * a + r - 1
            return pltpu.make_async_remote_copy(src_ref=src, dst_ref=dst, send_sem=ici_send.at[k],
                                                recv_sem=ici_recv.at[k], device_id=(px, py, ci), device_id_type=MESH)

        def d2d(a, r, half):
            px, py = _chip_peer(xi, yi, r)
            _, dst = views[a](ins[a], outs[a], 2 * px + py, half)
            k = 3 * a + r - 1
            return pltpu.make_async_remote_copy(src_ref=dst, dst_ref=dst, send_sem=d2d_send.at[k],
                                                recv_sem=d2d_recv.at[k], device_id=(xi, yi, 1 - ci), device_id_type=MESH)

        return xi, yi, ci, me, local, ici, d2d

    def locals_of():
        return [(a, h) for a in range(nb) for h in range(2)] + [(a, 0) for a in range(nb, n)]

    def send(ins, outs, sems):
        _, _, _, me, local, ici, _ = env(ins, outs, sems)
        for a, h in locals_of():
            local(a, h).start()
        for a in range(n):
            for r in (1, 2, 3):
                ici(a, r, me).start()

    def forward(ins, outs, sems):
        xi, yi, ci, _, _, ici, d2d = env(ins, outs, sems)
        for a in range(n):
            for r in (1, 2, 3):
                px, py = _chip_peer(xi, yi, r)
                ici(a, r, 2 * px + py).wait_recv()
                if a < nb:
                    d2d(a, r, ci).start()

    def finish(ins, outs, sems):
        _, _, ci, me, local, ici, d2d = env(ins, outs, sems)
        for a in range(nb):
            for r in (1, 2, 3):
                d2d(a, r, 1 - ci).wait_recv()
        for a in range(n):
            for r in (1, 2, 3):
                ici(a, r, me).wait_send()
                if a < nb:
                    d2d(a, r, ci).wait_send()
        for a, h in locals_of():
            local(a, h).wait()

    dma = pltpu.SemaphoreType.DMA
    return Rider(
        ins=[b for b, _, _ in big] + list(small),
        out_shape=[SDS(shape, bf16) for _, _, shape in big] + [SDS((N_CHIPS,) + a.shape, a.dtype) for a in small],
        sem_shapes=[dma((3 * n,)), dma((3 * n,)), dma((max(3 * nb, 1),)), dma((max(3 * nb, 1),)), dma((2 * nb + ns,))],
        stages=[(0.0, send), (forward_at, forward)], final=finish)


def pair_send_rider(gs):
    n = len(gs)

    def copy(ins, outs, sems, a):
        xi, yi, ci = _place()
        return pltpu.make_async_remote_copy(
            src_ref=ins[a].at[1 - ci], dst_ref=outs[a], send_sem=sems[0].at[a], recv_sem=sems[1].at[a],
            device_id=(xi, yi, 1 - ci), device_id_type=MESH)

    def send(ins, outs, sems):
        for a in range(n):
            copy(ins, outs, sems, a).start()

    def finish(ins, outs, sems):
        for a in range(n):
            copy(ins, outs, sems, a).wait()

    dma = pltpu.SemaphoreType.DMA
    return Rider(ins=list(gs), out_shape=[SDS(g.shape[1:], g.dtype) for g in gs], sem_shapes=[dma((n,)), dma((n,))],
                 stages=[(0.0, send)], final=finish)


def chip_rider(pbs, p32s, forward_at):
    n = len(pbs)

    def env(ins, outs, sems):
        ici_send, ici_recv, d2d_send, d2d_recv, own_send, own_recv, loc_sems = sems
        pb, p32, recv, own = ins[:n], ins[n:], outs[:n], outs[n:]
        xi, yi, ci = _place()
        me = 2 * xi + yi
        sib = (xi, yi, 1 - ci)

        def ici(a, r, src_slot, dst_slot):
            px, py = _chip_peer(xi, yi, r)
            k = 3 * a + r - 1
            return pltpu.make_async_remote_copy(
                src_ref=pb[a].at[src_slot], dst_ref=recv[a].at[ci, dst_slot], send_sem=ici_send.at[k],
                recv_sem=ici_recv.at[k], device_id=(px, py, ci), device_id_type=MESH)

        def d2d(a, r, half):
            px, py = _chip_peer(xi, yi, r)
            blk = recv[a].at[half, 2 * px + py]
            k = 3 * a + r - 1
            return pltpu.make_async_remote_copy(src_ref=blk, dst_ref=blk, send_sem=d2d_send.at[k],
                                                recv_sem=d2d_recv.at[k], device_id=sib, device_id_type=MESH)

        def mine(a, half):
            return pltpu.make_async_remote_copy(src_ref=p32[a].at[me], dst_ref=own[a].at[half], send_sem=own_send.at[a],
                                                recv_sem=own_recv.at[a], device_id=sib, device_id_type=MESH)

        def local(a):
            return pltpu.make_async_copy(p32[a].at[me], own[a].at[ci], loc_sems.at[a])

        return xi, yi, ci, me, ici, d2d, mine, local

    def send(ins, outs, sems):
        xi, yi, ci, me, ici, _, mine, local = env(ins, outs, sems)
        for a in range(n):
            local(a).start()
            mine(a, ci).start()
            for r in (1, 2, 3):
                px, py = _chip_peer(xi, yi, r)
                ici(a, r, 2 * px + py, me).start()

    def forward(ins, outs, sems):
        xi, yi, ci, me, ici, d2d, _, _ = env(ins, outs, sems)
        for a in range(n):
            for r in (1, 2, 3):
                px, py = _chip_peer(xi, yi, r)
                ici(a, r, me, 2 * px + py).wait_recv()
                d2d(a, r, ci).start()

    def finish(ins, outs, sems):
        xi, yi, ci, me, ici, d2d, mine, local = env(ins, outs, sems)
        for a in range(n):
            mine(a, 1 - ci).wait_recv()
            for r in (1, 2, 3):
                d2d(a, r, 1 - ci).wait_recv()
        for a in range(n):
            mine(a, ci).wait_send()
            local(a).wait()
            for r in (1, 2, 3):
                px, py = _chip_peer(xi, yi, r)
                ici(a, r, 2 * px + py, me).wait_send()
                d2d(a, r, ci).wait_send()

    dma = pltpu.SemaphoreType.DMA
    return Rider(
        ins=list(pbs) + list(p32s),
        out_shape=[SDS((2,) + p.shape, bf16) for p in pbs] + [SDS((2,) + p.shape[1:], f32) for p in p32s],
        sem_shapes=[dma((3 * n,)), dma((3 * n,)), dma((3 * n,)), dma((3 * n,)), dma((n,)), dma((n,)), dma((n,))],
        stages=[(0.0, send), (forward_at, forward)], final=finish)


def rs_pair_add(g, other, ci, name):
    _, nsh, r, w = g.shape
    tr = _rows_tile(r, w, 16)

    def body(c_ref, g_ref, o_ref, p_ref, pb_ref):
        p = g_ref[0] + o_ref[...]
        p_ref[...] = p
        pb_ref[...] = p.astype(bf16)

    blk = pl.BlockSpec((1, tr, w), lambda j, i, c: (j, i, 0))
    return pl.pallas_call(
        body, name=name,
        grid_spec=pltpu.PrefetchScalarGridSpec(
            num_scalar_prefetch=1, grid=(nsh, r // tr),
            in_specs=[pl.BlockSpec((1, 1, tr, w), lambda j, i, c: (c[0], j, i, 0)), blk], out_specs=[blk, blk]),
        out_shape=[SDS((nsh, r, w), f32), SDS((nsh, r, w), bf16)],
        compiler_params=_cp("parallel", "parallel"),
    )(ci.reshape(1).astype(jnp.int32), g, other)


def rs_chip_add(own, recv, chip, name):
    _, nsh, r, w = recv.shape
    tr = _rows_tile(r, w, 16)

    def body(c_ref, own_ref, recv_ref, o_ref):
        me = c_ref[0]
        acc = None
        for j in range(N_CHIPS):
            term = jnp.where(me == j, own_ref[0], recv_ref[0, j].astype(f32))
            acc = term if acc is None else acc + term
        o_ref[0] = acc

    return pl.pallas_call(
        body, name=name,
        grid_spec=pltpu.PrefetchScalarGridSpec(
            num_scalar_prefetch=1, grid=(2, r // tr),
            in_specs=[pl.BlockSpec((1, tr, w), lambda h, i, c: (h, i, 0)),
                      pl.BlockSpec((1, nsh, tr, w), lambda h, i, c: (h, 0, i, 0))],
            out_specs=pl.BlockSpec((1, tr, w), lambda h, i, c: (h, i, 0))),
        out_shape=SDS((2, r, w), f32),
        compiler_params=_cp("parallel", "parallel"),
    )(chip.reshape(1).astype(jnp.int32), own, recv)


class GradReduction:
    def __init__(self, keys, grads, ci, chip):
        self.keys, self.grads, self.ci, self.chip = keys, grads, ci, chip

    def pair_rider(self):
        return pair_send_rider(self.grads)

    def chip_rider(self, from_pair, forward_at):
        both = [rs_pair_add(g, o, self.ci, "rs_pair_add_" + k) for k, g, o in zip(self.keys, self.grads, from_pair)]
        return chip_rider([pb for _, pb in both], [p for p, _ in both], forward_at)

    def result(self, landed):
        n = len(self.keys)
        return {k: rs_chip_add(own, recv, self.chip, "rs_chip_add_" + k)
                for k, recv, own in zip(self.keys, landed[:n], landed[n:])}

    def alone(self):
        from_pair = run_rider(self.pair_rider(), "rs_pair_" + self.keys[0])
        return self.result(run_rider(self.chip_rider(from_pair, 0.0), "rs_chip_" + self.keys[0]))


STAGE_W = DFF
_ST = {"norm_mix0": 0, "norm_mix1": 1, "attn_b_qkv": 2, "attn_sinks": 3, "attn_b_o": 4, "final_norm": 5, "loss": 6,
       "norm_ffn0": 8, "norm_ffn1": 9, "ffn_b_dw0": 10, "ffn_b_dw1": 11, "conv_b_pw1": 12, "conv_b_dw": 13,
       "conv_ln_g": 14, "conv_ln_b": 15, "conv_b_pw2": 16, "ffn_w_dw0": 17, "ffn_w_dw1": 20, "conv_w_dw": 24}
STAGE_ROWS = 56
SMALL_REP = ("norm_mix", "attn_b_qkv", "attn_sinks", "attn_b_o", "norm_ffn", "ffn_b_dw", "final_norm")
SMALL_SH = ("conv_b_pw1", "conv_w_dw", "conv_b_dw", "conv_ln_g", "conv_ln_b", "conv_b_pw2", "ffn_w_dw")
_SMALL_PARTS = ("norm_mix0", "norm_mix1", "attn_b_qkv", "attn_sinks", "attn_b_o", "norm_ffn0", "norm_ffn1", "ffn_b_dw0",
                "ffn_b_dw1", "final_norm", "conv_b_pw1", "conv_b_dw", "conv_ln_g", "conv_ln_b", "conv_b_pw2", "loss",
                "ffn_w_dw0", "ffn_w_dw1", "conv_w_dw")


def small_reduce_adamw(parts, w, m, v):
    names = SMALL_REP + SMALL_SH
    npart, nw = len(_SMALL_PARTS), len(names)

    def body(*refs):
        part = dict(zip(_SMALL_PARTS, refs[:npart]))
        off = npart
        w_ref = dict(zip(names, refs[off:off + nw]))
        m_ref = dict(zip(names, refs[off + nw:off + 2 * nw]))
        v_ref = dict(zip(names, refs[off + 2 * nw:off + 3 * nw]))
        off += 3 * nw
        loss_ref = refs[off]
        g_out = dict(zip(names, refs[off + 1:off + 1 + nw]))
        d_out = dict(zip(names, refs[off + 1 + nw:off + 1 + 2 * nw]))
        m_out = dict(zip(names, refs[off + 1 + 2 * nw:off + 1 + 3 * nw]))
        v_out = dict(zip(names, refs[off + 1 + 3 * nw:off + 1 + 4 * nw]))
        stage_ref, buf_ref, tot_ref, send_sems, recv_sems = refs[off + 1 + 4 * nw:]

        xi, yi, ci = _place()
        me = 4 * xi + 2 * yi + ci
        chip = 2 * xi + yi

        stage_ref[...] = jnp.zeros_like(stage_ref)
        for name in _SMALL_PARTS:
            ref, r0 = part[name], _ST[name]
            if name in ("attn_sinks", "loss"):
                val = ref[...]
            elif name in ("ffn_w_dw0", "ffn_w_dw1", "conv_w_dw"):
                val = jnp.sum(ref[...], axis=1)
            else:
                val = jnp.sum(ref[...], axis=0, keepdims=True)
            stage_ref[r0:r0 + val.shape[0], 0:val.shape[1]] = val

        buf_ref[me] = stage_ref[...]

        def peer(r):
            px, py = _chip_peer(xi, yi, r >> 1)
            return px, py, (1 - ci if r & 1 else ci)

        def copy(r, slot):
            return pltpu.make_async_remote_copy(
                src_ref=stage_ref, dst_ref=buf_ref.at[slot], send_sem=send_sems.at[r - 1], recv_sem=recv_sems.at[r - 1],
                device_id=peer(r), device_id_type=MESH)

        sends = []
        for r in range(1, N_DEV):
            cp = copy(r, me)
            cp.start()
            sends.append(cp)
        for r in range(1, N_DEV):
            px, py, pc = peer(r)
            copy(r, 4 * px + 2 * py + pc).wait_recv()
        for cp in sends:
            cp.wait_send()
        acc = buf_ref[0]
        for d in range(1, N_DEV):
            acc = acc + buf_ref[d]
        tot_ref[...] = acc

        def rows(name, n, width):
            r0 = _ST[name]
            return tot_ref[r0:r0 + n, 0:width]

        def mine(name, n, width):
            r0 = _ST[name]
            out = tot_ref[r0:r0 + n, 0:width]
            for j in range(1, N_CHIPS):
                out = jnp.where(chip == j, tot_ref[r0:r0 + n, j * width:(j + 1) * width], out)
            return out

        loss_ref[...] = rows("loss", 1, 1)
        grads = {
            "norm_mix": rows("norm_mix0", 2, D), "attn_b_qkv": rows("attn_b_qkv", 1, QKV),
            "attn_sinks": rows("attn_sinks", 1, N_HEADS), "attn_b_o": rows("attn_b_o", 1, D),
            "norm_ffn": rows("norm_ffn0", 2, D), "ffn_b_dw": rows("ffn_b_dw0", 2, DFF),
            "final_norm": rows("final_norm", 1, D),
            "conv_b_pw1": mine("conv_b_pw1", 1, 2 * D // N_CHIPS), "conv_w_dw": mine("conv_w_dw", CONV_W, D // N_CHIPS),
            "conv_b_dw": mine("conv_b_dw", 1, D // N_CHIPS), "conv_ln_g": mine("conv_ln_g", 1, D // N_CHIPS),
            "conv_ln_b": mine("conv_ln_b", 1, D // N_CHIPS), "conv_b_pw2": mine("conv_b_pw2", 1, D // N_CHIPS),
        }
        for name in names:
            if name == "ffn_w_dw":
                continue
            at = 0 if name == "conv_w_dw" else Ellipsis
            g = grads[name]
            d, mn, vn = _adam_math(w_ref[name][at], g, m_ref[name][at], v_ref[name][at])
            g_out[name][at] = g
            d_out[name][at] = d
            m_out[name][at] = mn
            v_out[name][at] = vn
        for layer, key in enumerate(("ffn_w_dw0", "ffn_w_dw1")):
            g = mine(key, 3, DFF // N_CHIPS)
            d, mn, vn = _adam_math(w_ref["ffn_w_dw"][layer], g, m_ref["ffn_w_dw"][layer], v_ref["ffn_w_dw"][layer])
            g_out["ffn_w_dw"][layer] = g
            d_out["ffn_w_dw"][layer] = d
            m_out["ffn_w_dw"][layer] = mn
            v_out["ffn_w_dw"][layer] = vn

    ins = [parts[k] for k in _SMALL_PARTS] + [src[k] for src in (w, m, v) for k in names]
    wshapes = [SDS(w[k].shape, f32) for k in names]
    outs = pl.pallas_call(
        body, name="small_reduce_adamw", in_specs=[_VMEM] * len(ins), out_specs=[_VMEM] * (1 + 4 * nw),
        out_shape=[SDS((1, 1), f32)] + wshapes * 4,
        scratch_shapes=[pltpu.VMEM((STAGE_ROWS, STAGE_W), f32), pltpu.VMEM((N_DEV, STAGE_ROWS, STAGE_W), f32),
                        pltpu.VMEM((STAGE_ROWS, STAGE_W), f32), pltpu.SemaphoreType.DMA((N_DEV - 1,)),
                        pltpu.SemaphoreType.DMA((N_DEV - 1,))],
        compiler_params=pltpu.CompilerParams(vmem_limit_bytes=VMEM_LIMIT),
    )(*ins)
    loss = outs[0]
    g, d, mn, vn = (dict(zip(names, outs[1 + k * nw:1 + (k + 1) * nw])) for k in range(4))
    return loss, g, d, mn, vn


TM = 512
TM_FFN = 256
FFN_CHUNK = 256
CONV_ROWS = 128
TK = 1024
FORWARD_AT = 0.6


def kernel(x, norm_mix, attn_w_qkv, attn_b_qkv, attn_sinks, attn_w_o, attn_b_o, conv_w_pw1, conv_b_pw1, conv_w_dw, conv_b_dw, conv_ln_g, conv_ln_b, conv_w_pw2, conv_b_pw2, norm_ffn, ffn_w_up, ffn_w_dw, ffn_b_dw, ffn_w_down, final_norm, loss_target, m_norm_mix, m_attn_w_qkv, m_attn_b_qkv, m_attn_sinks, m_attn_w_o, m_attn_b_o, m_conv_w_pw1, m_conv_b_pw1, m_conv_w_dw, m_conv_b_dw, m_conv_ln_g, m_conv_ln_b, m_conv_w_pw2, m_conv_b_pw2, m_norm_ffn, m_ffn_w_up, m_ffn_w_dw, m_ffn_b_dw, m_ffn_w_down, m_final_norm, v_norm_mix, v_attn_w_qkv, v_attn_b_qkv, v_attn_sinks, v_attn_w_o, v_attn_b_o, v_conv_w_pw1, v_conv_b_pw1, v_conv_w_dw, v_conv_b_dw, v_conv_ln_g, v_conv_ln_b, v_conv_w_pw2, v_conv_b_pw2, v_norm_ffn, v_ffn_w_up, v_ffn_w_dw, v_ffn_b_dw, v_ffn_w_down, v_final_norm):
    w = dict(norm_mix=norm_mix, attn_w_qkv=attn_w_qkv, attn_b_qkv=attn_b_qkv, attn_sinks=attn_sinks, attn_w_o=attn_w_o,
             attn_b_o=attn_b_o, conv_w_pw1=conv_w_pw1, conv_b_pw1=conv_b_pw1, conv_w_dw=conv_w_dw, conv_b_dw=conv_b_dw,
             conv_ln_g=conv_ln_g, conv_ln_b=conv_ln_b, conv_w_pw2=conv_w_pw2, conv_b_pw2=conv_b_pw2, norm_ffn=norm_ffn,
             ffn_w_up=ffn_w_up, ffn_w_dw=ffn_w_dw, ffn_b_dw=ffn_b_dw, ffn_w_down=ffn_w_down, final_norm=final_norm)
    mom = dict(norm_mix=m_norm_mix, attn_w_qkv=m_attn_w_qkv, attn_b_qkv=m_attn_b_qkv, attn_sinks=m_attn_sinks,
               attn_w_o=m_attn_w_o, attn_b_o=m_attn_b_o, conv_w_pw1=m_conv_w_pw1, conv_b_pw1=m_conv_b_pw1,
               conv_w_dw=m_conv_w_dw, conv_b_dw=m_conv_b_dw, conv_ln_g=m_conv_ln_g, conv_ln_b=m_conv_ln_b,
               conv_w_pw2=m_conv_w_pw2, conv_b_pw2=m_conv_b_pw2, norm_ffn=m_norm_ffn, ffn_w_up=m_ffn_w_up,
               ffn_w_dw=m_ffn_w_dw, ffn_b_dw=m_ffn_b_dw, ffn_w_down=m_ffn_w_down, final_norm=m_final_norm)
    vel = dict(norm_mix=v_norm_mix, attn_w_qkv=v_attn_w_qkv, attn_b_qkv=v_attn_b_qkv, attn_sinks=v_attn_sinks,
               attn_w_o=v_attn_w_o, attn_b_o=v_attn_b_o, conv_w_pw1=v_conv_w_pw1, conv_b_pw1=v_conv_b_pw1,
               conv_w_dw=v_conv_w_dw, conv_b_dw=v_conv_b_dw, conv_ln_g=v_conv_ln_g, conv_ln_b=v_conv_ln_b,
               conv_w_pw2=v_conv_w_pw2, conv_b_pw2=v_conv_b_pw2, norm_ffn=v_norm_ffn, ffn_w_up=v_ffn_w_up,
               ffn_w_dw=v_ffn_w_dw, ffn_b_dw=v_ffn_b_dw, ffn_w_down=v_ffn_w_down, final_norm=v_final_norm)
    order = ("norm_mix", "attn_w_qkv", "attn_b_qkv", "attn_sinks", "attn_w_o", "attn_b_o", "conv_w_pw1", "conv_b_pw1",
             "conv_w_dw", "conv_b_dw", "conv_ln_g", "conv_ln_b", "conv_w_pw2", "conv_b_pw2", "norm_ffn", "ffn_w_up",
             "ffn_w_dw", "ffn_b_dw", "ffn_w_down", "final_norm")
    xi, yi, ci = _place()
    chip = 2 * xi + yi
    xs, target = x[0], loss_target[0]
    s = xs.shape[0]
    tm, tmf, tk = min(TM, s), min(TM_FFN, s), min(TK, s)
    row = lambda v: v.reshape(1, -1)
    join = lambda a, axis: jnp.concatenate([a[j] for j in range(N_CHIPS)], axis=axis)
    cast = lambda a: a.astype(bf16)
    small, big = {}, {}

    got = run_rider(gather_rider(
        [(cast(attn_w_qkv[0]), _gv_qkv, (N_CHIPS, D, QKV // N_CHIPS)), (cast(attn_w_o[0]), _gv_rows, (D, D))],
        [w[k] for k in SMALL_SH], 0.0), "gather_attn")
    qkv4, w_o = got[:2]
    sm = dict(zip(SMALL_SH, got[2:]))
    w_qkv = jnp.transpose(qkv4, (1, 0, 2)).reshape(D, QKV)
    sinks = attn_sinks.reshape(N_HEADS)
    b_pw1, conv_dw, conv_bdw = join(sm["conv_b_pw1"], 1), join(sm["conv_w_dw"], 2)[0], join(sm["conv_b_dw"], 1)
    ln_g, ln_b, b_pw2, ffn_dw = (join(sm["conv_ln_g"], 1), join(sm["conv_ln_b"], 1), join(sm["conv_b_pw2"], 1),
                                 join(sm["ffn_w_dw"], 2))

    h0, qkv = qkv_fwd(xs, row(norm_mix[0]), w_qkv, attn_b_qkv, tm)
    (o, lse), (w_up0, w_dn0) = attn_fwd(qkv, sinks, rider=gather_rider(
        [(cast(ffn_w_up[0]), _gv_cols, (D, 2 * DFF)), (cast(ffn_w_down[0]), _gv_rows, (DFF, D))], [], FORWARD_AT))
    x1 = attn_out_fwd(xs, o, w_o, attn_b_o, tm)
    (x2, h1, up0, gate0, act0), (w_pw1, w_pw2, w_up1, w_dn1) = ffn_fwd(
        x1, row(norm_ffn[0]), w_up0, ffn_dw[0], row(ffn_b_dw[0]), w_dn0, tmf, FFN_CHUNK, rider=gather_rider(
            [(cast(conv_w_pw1[0]), _gv_cols, (D, 2 * D)), (cast(conv_w_pw2[0]), _gv_rows, (D, D)),
             (cast(ffn_w_up[1]), _gv_cols, (D, 2 * DFF)), (cast(ffn_w_down[1]), _gv_rows, (DFF, D))], [], FORWARD_AT))
    h2, a, u = pw1_fwd(x2, row(norm_mix[1]), w_pw1, b_pw1, tm)
    c, x3 = conv_fwd(u, x2, conv_dw, conv_bdw, ln_g, ln_b, w_pw2, b_pw2, tm, CONV_ROWS)
    x4, h3, up1, gate1, act1 = ffn_fwd(x3, row(norm_ffn[1]), w_up1, ffn_dw[1], row(ffn_b_dw[1]), w_dn1, tmf, FFN_CHUNK)

    dx4, small["final_norm"], small["loss"] = loss_bwd(x4, final_norm.reshape(1, D), target, tm)
    dx3, dup1, small["norm_ffn1"], small["ffn_w_dw1"], small["ffn_b_dw1"] = ffn_bwd(
        dx4, x3, row(norm_ffn[1]), up1, gate1, ffn_dw[1], w_dn1.T, w_up1.T, tmf, FFN_CHUNK)
    red1 = GradReduction(("up1", "down1"), [wgrad_cols(h3, dup1, tk, "wgrad_up1"),
                                           wgrad_rows(act1, dx4, 512, tk, "wgrad_down1")], ci, chip)
    (dc, u3, small["conv_ln_g"], small["conv_ln_b"], small["conv_b_pw2"], small["conv_b_dw"]), from_pair = conv_bwd_a(
        dx3, c, ln_g, ln_b, w_pw2.T, tm, rider=red1.pair_rider())
    g_pw2 = wgrad_rows(u3, dx3, 512, tk, "wgrad_pw2")
    (dx2, da, small["conv_w_dw"], small["conv_b_pw1"], small["norm_mix1"]), landed = conv_bwd_b(
        dc, u, a, conv_dw, x2, row(norm_mix[1]), w_pw1.T, dx3, tm, CONV_ROWS,
        rider=red1.chip_rider(from_pair, FORWARD_AT))
    big.update(red1.result(landed))
    g_pw1 = wgrad_cols(h2, da, tk, "wgrad_pw1")

    red2 = GradReduction(("pw1", "pw2"), [g_pw1, g_pw2], ci, chip)
    (dx1, dup0, small["norm_ffn0"], small["ffn_w_dw0"], small["ffn_b_dw0"]), from_pair = ffn_bwd(
        dx2, x1, row(norm_ffn[0]), up0, gate0, ffn_dw[0], w_dn0.T, w_up0.T, tmf, FFN_CHUNK, rider=red2.pair_rider())
    g_up0, landed = wgrad_cols(h1, dup0, tk, "wgrad_up0", rider=red2.chip_rider(from_pair, FORWARD_AT))
    big.update(red2.result(landed))
    g_dn0 = wgrad_rows(act0, dx2, 512, tk, "wgrad_down0")

    red3 = GradReduction(("up0", "down0", "wo"), [g_up0, g_dn0, wgrad_rows(o, dx1, 512, tk, "wgrad_o")], ci, chip)
    (do, small["attn_b_o"]), from_pair = attn_out_bwd(dx1, w_o.T, tm, rider=red3.pair_rider())
    (dq, dkv, small["attn_sinks"]), landed = attn_bwd(qkv, o, do, lse, sinks,
                                                      rider=red3.chip_rider(from_pair, FORWARD_AT))
    big.update(red3.result(landed))
    dx0, small["norm_mix0"], small["attn_b_qkv"], dkvb = qkv_bwd(dq, dkv, xs, row(norm_mix[0]), w_qkv.T, dx1, tm)
    g_qkv = jnp.concatenate([wgrad(h0, dq, 512, tk, "wgrad_q"), wgrad(h0, dkvb, 2 * N_KV * HD, tk, "wgrad_kv")], axis=1)
    g_qkv = jnp.transpose(g_qkv.reshape(2, D // 2, N_CHIPS, QKV // N_CHIPS), (0, 2, 1, 3))
    red4 = GradReduction(("qkv",), [g_qkv], ci, chip)
    big.update(red4.alone())

    gbig = {
        "attn_w_qkv": big["qkv"].reshape(1, D, QKV // N_CHIPS), "attn_w_o": big["wo"].reshape(1, D // N_CHIPS, D),
        "conv_w_pw1": big["pw1"].reshape(1, D, 2 * D // N_CHIPS), "conv_w_pw2": big["pw2"].reshape(1, D // N_CHIPS, D),
        "ffn_w_up": jnp.stack([big["up0"], big["up1"]]).reshape(2, D, 2 * DFF // N_CHIPS),
        "ffn_w_down": jnp.stack([big["down0"], big["down1"]]).reshape(2, DFF // N_CHIPS, D),
    }

    fix = lambda d: {**d, "final_norm": d["final_norm"].reshape(1, D)}
    loss, gs, ds, ms, vs = small_reduce_adamw(small, fix(w), fix(mom), fix(vel))
    unfix = lambda d: {**d, "final_norm": d["final_norm"].reshape(D)}
    gout, delta, new_m, new_v = unfix(gs), unfix(ds), unfix(ms), unfix(vs)

    for name, g in gbig.items():
        gout[name] = g
        delta[name], new_m[name], new_v[name] = adamw(w[name], g, mom[name], vel[name], "adamw_" + name)

    return (loss.reshape(()), dx0[None], *[gout[n] for n in order], *[delta[n] for n in order],
            *[new_m[n] for n in order], *[new_v[n] for n in order])
```

```python
import math

import jax
import jax.numpy as jnp
from jax import lax
from jax.experimental import pallas as pl
from jax.experimental.pallas import tpu as pltpu

f32 = jnp.float32
bf16 = jnp.bfloat16
SDS = jax.ShapeDtypeStruct
MESH = pl.DeviceIdType.MESH

D = 1024
N_HEADS = 16
N_KV = 2
GROUP = 8
HD = 64
BLK = 128
QKV = (N_HEADS + 2 * N_KV) * HD
KV_COL_BLOCK = (N_HEADS * HD) // (2 * N_KV * HD)
CONV_W = 31
CONV_HALO = 32
DFF = 2816
RMS_EPS = 1e-6
LN_EPS = 1e-5
LR, B1, B2, ADAM_EPS, WD, STEP = 0.001, 0.9, 0.999, 1e-08, 0.01, 10

N_CHIPS = 4
N_DEV = 8
VMEM_LIMIT = 56 * 1024 * 1024
LANES = 128
SUB = 8
ELEMENTWISE_BLOCK_BYTES = 1 << 20


def _cp(*sem):
    return pltpu.CompilerParams(dimension_semantics=sem, vmem_limit_bytes=VMEM_LIMIT)


def _row(tm, n):
    return pl.BlockSpec((tm, n), lambda i: (i, 0))


def _const(shape):
    return pl.BlockSpec(shape, lambda *_: (0,) * len(shape), pipeline_mode=pl.Buffered(1))


def _acc(shape):
    return pl.BlockSpec(shape, lambda *_: (0,) * len(shape))


def _rms_fwd(x, g):
    r = lax.rsqrt(jnp.mean(x * x, axis=-1, keepdims=True) + RMS_EPS)
    xn = x * r
    return xn * g, xn, r


def _colsum8(v):
    return jnp.sum(v.reshape(v.shape[0] // SUB, SUB, v.shape[1]), axis=0)


def _rms_bwd(xn, r, g, dh):
    dyn = dh * g
    dx = r * (dyn - xn * jnp.mean(dyn * xn, axis=-1, keepdims=True))
    return dx, _colsum8(dh * xn)


def _sigmoid(z):
    return 0.5 * jnp.tanh(0.5 * z) + 0.5


def _dsilu(z, sg):
    return sg * (1.0 + z * (1.0 - sg))


def _dot(a, b):
    return jnp.dot(a, b, preferred_element_type=f32)


_ANY = pl.BlockSpec(memory_space=pl.ANY)
_VMEM = pl.BlockSpec(memory_space=pltpu.VMEM)


class Rider:
    def __init__(self, ins, out_shape, sem_shapes, stages, final):
        self.ins, self.out_shape, self.sem_shapes, self.stages, self.final = ins, out_shape, sem_shapes, stages, final


def run_rider(rider, name):
    n_in, n_out = len(rider.ins), len(rider.out_shape)

    def body(*refs):
        parts = refs[:n_in], refs[n_in:n_in + n_out], refs[n_in + n_out:]
        for _, fn in rider.stages:
            fn(*parts)
        rider.final(*parts)

    return pl.pallas_call(
        body, name=name, in_specs=[_ANY] * n_in, out_specs=[_ANY] * n_out, out_shape=list(rider.out_shape),
        scratch_shapes=list(rider.sem_shapes),
    )(*rider.ins)


def _hosted(rider, body, *, grid, in_specs, out_specs, out_shape, name, compiler_params, scratch_shapes=()):
    if rider is None:
        return pl.pallas_call(body, grid=grid, in_specs=in_specs, out_specs=out_specs, out_shape=out_shape, name=name,
                              compiler_params=compiler_params, scratch_shapes=list(scratch_shapes))
    single = not isinstance(out_shape, (list, tuple))
    shapes = [out_shape] if single else list(out_shape)
    specs = [out_specs] if single else list(out_specs)
    n_in, n_out, n_sc = len(in_specs), len(shapes), len(scratch_shapes)
    r_in, r_out = len(rider.ins), len(rider.out_shape)
    total = math.prod(grid)

    def wrapped(*refs):
        own_in, refs = refs[:n_in], refs[n_in:]
        r_ins, refs = refs[:r_in], refs[r_in:]
        own_out, refs = refs[:n_out], refs[n_out:]
        r_outs, refs = refs[:r_out], refs[r_out:]
        own_sc, r_sems = refs[:n_sc], refs[n_sc:]
        step = 0
        for d, n in enumerate(grid):
            step = step * n + pl.program_id(d)
        for frac, fn in rider.stages:
            @pl.when(step == min(int(frac * total), total - 1))
            def _(fn=fn):
                fn(r_ins, r_outs, r_sems)

        body(*own_in, *own_out, *own_sc)

        @pl.when(step == total - 1)
        def _():
            rider.final(r_ins, r_outs, r_sems)

    call = pl.pallas_call(
        wrapped, grid=grid, in_specs=list(in_specs) + [_ANY] * r_in, out_specs=specs + [_ANY] * r_out,
        out_shape=shapes + list(rider.out_shape), scratch_shapes=list(scratch_shapes) + list(rider.sem_shapes),
        name=name, compiler_params=_cp(*(("arbitrary",) * len(grid))))

    def run(*args):
        res = call(*args, *rider.ins)
        own = res[:n_out]
        return (own[0] if single else own), res[n_out:]

    return run


def qkv_fwd(x, g, w, b, tm):
    s = x.shape[0]

    def body(x_ref, g_ref, w_ref, b_ref, h_ref, o_ref):
        h, _, _ = _rms_fwd(x_ref[...], g_ref[...])
        hb = h.astype(bf16)
        h_ref[...] = hb
        o_ref[...] = (_dot(hb, w_ref[...]) + b_ref[...]).astype(bf16)

    return pl.pallas_call(
        body, grid=(s // tm,), name="qkv_fwd",
        in_specs=[_row(tm, D), _const((1, D)), _const((D, QKV)), _const((1, QKV))],
        out_specs=[_row(tm, D), _row(tm, QKV)],
        out_shape=[SDS((s, D), bf16), SDS((s, QKV), bf16)],
        compiler_params=_cp("parallel"),
    )(x, g, w, b)


def _band_mask(i):
    qi = lax.broadcasted_iota(jnp.int32, (GROUP * BLK, 2 * BLK), 0) & (BLK - 1)
    ki = lax.broadcasted_iota(jnp.int32, (GROUP * BLK, 2 * BLK), 1)
    dist = qi + BLK - ki
    return (dist >= 0) & (dist < BLK) & ((ki >= BLK) | (i > 0))


_NEG = float(jnp.finfo(jnp.float32).min)
_NT = (((1,), (1,)), ((), ()))
_TN = (((0,), (0,)), ((), ()))


def _kv_heads(kvp_ref, kvc_ref, kvh):
    ks = slice(kvh * HD, (kvh + 1) * HD)
    vs = slice(N_KV * HD + kvh * HD, N_KV * HD + (kvh + 1) * HD)
    k = jnp.concatenate([kvp_ref[:, ks], kvc_ref[:, ks]], axis=0)
    v = jnp.concatenate([kvp_ref[:, vs], kvc_ref[:, vs]], axis=0)
    return k, v


def _stack_group(ref, kvh, width=HD):
    return jnp.concatenate([ref[:, (kvh * GROUP + gi) * width:(kvh * GROUP + gi + 1) * width] for gi in range(GROUP)],
                           axis=0)


def _group_sinks(sink_ref, kvh):
    row = lax.broadcasted_iota(jnp.int32, (GROUP * BLK, 1), 0)
    col = jnp.zeros((GROUP * BLK, 1), f32)
    for g in range(GROUP):
        col = jnp.where((row >= g * BLK) & (row < (g + 1) * BLK), sink_ref[kvh * GROUP + g], col)
    return col


def attn_fwd(qkv, sinks, rider=None):
    s = qkv.shape[0]
    scale = 1.0 / math.sqrt(HD)

    def body(q_ref, kvc_ref, kvp_ref, sink_ref, o_ref, lse_ref):
        valid = _band_mask(pl.program_id(0))
        for kvh in range(N_KV):
            k, v = _kv_heads(kvp_ref, kvc_ref, kvh)
            sc = lax.dot_general(_stack_group(q_ref, kvh), k, _NT, preferred_element_type=f32) * scale
            sc = jnp.where(valid, sc, _NEG)
            sink = _group_sinks(sink_ref, kvh)
            m = jnp.maximum(jnp.max(sc, axis=-1, keepdims=True), sink)
            p = jnp.exp(sc - m)
            denom = jnp.sum(p, axis=-1, keepdims=True) + jnp.exp(sink - m)
            og = _dot((p / denom).astype(bf16), v).astype(bf16)
            lse = m + jnp.log(denom)
            for gi in range(GROUP):
                h = kvh * GROUP + gi
                o_ref[:, h * HD:(h + 1) * HD] = og[gi * BLK:(gi + 1) * BLK]
                lse_ref[:, h:h + 1] = lse[gi * BLK:(gi + 1) * BLK]

    return _hosted(
        rider, body, grid=(s // BLK,), name="attn_fwd",
        in_specs=[
            pl.BlockSpec((BLK, N_HEADS * HD), lambda i: (i, 0)),
            pl.BlockSpec((BLK, 2 * N_KV * HD), lambda i: (i, KV_COL_BLOCK)),
            pl.BlockSpec((BLK, 2 * N_KV * HD), lambda i: (jnp.maximum(i - 1, 0), KV_COL_BLOCK)),
            pl.BlockSpec(memory_space=pltpu.SMEM),
        ],
        out_specs=[_row(BLK, D), _row(BLK, N_HEADS)],
        out_shape=[SDS((s, D), bf16), SDS((s, N_HEADS), f32)],
        compiler_params=_cp("parallel"),
    )(qkv, qkv, qkv, sinks)


def attn_out_fwd(x, o, w, b, tm):
    s = x.shape[0]

    def body(x_ref, o_ref, w_ref, b_ref, y_ref):
        y_ref[...] = x_ref[...] + _dot(o_ref[...], w_ref[...]) + b_ref[...]

    return pl.pallas_call(
        body, grid=(s // tm,), name="attn_out_fwd",
        in_specs=[_row(tm, D), _row(tm, D), _const((D, D)), _const((1, D))],
        out_specs=_row(tm, D), out_shape=SDS((s, D), f32),
        compiler_params=_cp("parallel"),
    )(x, o, w, b)


def attn_out_bwd(dy, wt, tm, rider=None):
    s = dy.shape[0]

    def body(dy_ref, wt_ref, do_ref, db_ref):
        @pl.when(pl.program_id(0) == 0)
        def _():
            db_ref[...] = jnp.zeros_like(db_ref)

        dy = dy_ref[...]
        do_ref[...] = _dot(dy.astype(bf16), wt_ref[...]).astype(bf16)
        db_ref[...] += _colsum8(dy)

    return _hosted(
        rider, body, grid=(s // tm,), name="attn_out_bwd",
        in_specs=[_row(tm, D), _const((D, D))],
        out_specs=[_row(tm, D), _acc((SUB, D))],
        out_shape=[SDS((s, D), bf16), SDS((SUB, D), f32)],
        compiler_params=_cp("arbitrary"),
    )(dy, wt)


def attn_bwd(qkv, o, do, lse, sinks, rider=None):
    s = qkv.shape[0]
    nb = s // BLK
    scale = 1.0 / math.sqrt(HD)
    kvw = 2 * N_KV * HD

    def body(q_ref, kvc_ref, kvp_ref, o_ref, do_ref, lse_ref, sink_ref, dq_ref, dkv_ref, ds_ref, carry_ref):
        i = pl.program_id(0)

        @pl.when(i == 0)
        def _():
            ds_ref[...] = jnp.zeros_like(ds_ref)
            carry_ref[...] = jnp.zeros_like(carry_ref)

        @pl.when(i < nb)
        def _():
            valid = _band_mask(i)
            for kvh in range(N_KV):
                k, v = _kv_heads(kvp_ref, kvc_ref, kvh)
                qg = _stack_group(q_ref, kvh)
                dog = _stack_group(do_ref, kvh)
                lse = _stack_group(lse_ref, kvh, 1)
                sc = lax.dot_general(qg, k, _NT, preferred_element_type=f32) * scale
                sc = jnp.where(valid, sc, _NEG)
                p = jnp.exp(sc - lse)
                dp = lax.dot_general(dog, v, _NT, preferred_element_type=f32)
                dlt = jnp.sum(dog.astype(f32) * _stack_group(o_ref, kvh).astype(f32), axis=-1, keepdims=True)
                dsc = (p * (dp - dlt)).astype(bf16)
                dqg = (_dot(dsc, k) * scale).astype(bf16)
                dk = lax.dot_general(dsc, qg, _TN, preferred_element_type=f32) * scale
                dv = lax.dot_general(p.astype(bf16), dog, _TN, preferred_element_type=f32)
                dsink = jnp.exp(_group_sinks(sink_ref, kvh) - lse) * dlt
                for gi in range(GROUP):
                    h = kvh * GROUP + gi
                    dq_ref[:, h * HD:(h + 1) * HD] = dqg[gi * BLK:(gi + 1) * BLK]
                    ds_ref[:, h:h + 1] += -jnp.sum(dsink[gi * BLK:(gi + 1) * BLK], axis=0, keepdims=True)
                ks = slice(kvh * HD, (kvh + 1) * HD)
                vs = slice(N_KV * HD + kvh * HD, N_KV * HD + (kvh + 1) * HD)
                dkv_ref[:, ks] = carry_ref[:, ks] + dk[:BLK]
                dkv_ref[:, vs] = carry_ref[:, vs] + dv[:BLK]
                carry_ref[:, ks] = dk[BLK:]
                carry_ref[:, vs] = dv[BLK:]

        @pl.when(i == nb)
        def _():
            dkv_ref[...] = carry_ref[...]

    cur = lambda i: (jnp.minimum(i, nb - 1), 0)
    prev = lambda i: (jnp.clip(i - 1, 0, nb - 1), KV_COL_BLOCK)
    return _hosted(
        rider, body, grid=(nb + 1,), name="attn_bwd",
        in_specs=[
            pl.BlockSpec((BLK, D), cur),
            pl.BlockSpec((BLK, kvw), lambda i: (jnp.minimum(i, nb - 1), KV_COL_BLOCK)),
            pl.BlockSpec((BLK, kvw), prev),
            pl.BlockSpec((BLK, D), cur),
            pl.BlockSpec((BLK, D), cur),
            pl.BlockSpec((BLK, N_HEADS), cur),
            pl.BlockSpec(memory_space=pltpu.SMEM),
        ],
        out_specs=[
            pl.BlockSpec((BLK, D), cur),
            pl.BlockSpec((BLK, kvw), lambda i: (jnp.maximum(i - 1, 0), 0)),
            _acc((1, N_HEADS)),
        ],
        out_shape=[SDS((s, D), bf16), SDS((s, kvw), f32), SDS((1, N_HEADS), f32)],
        scratch_shapes=[pltpu.VMEM((BLK, kvw), f32)],
        compiler_params=_cp("arbitrary"),
    )(qkv, qkv, qkv, o, do, lse, sinks)


def qkv_bwd(dq, dkv, x, g, wt, dres, tm):
    s = x.shape[0]
    qd = N_HEADS * HD
    kvw = 2 * N_KV * HD

    def body(dq_ref, dkv_ref, x_ref, g_ref, wt_ref, dres_ref, dx_ref, dg_ref, db_ref, dkvb_ref):
        @pl.when(pl.program_id(0) == 0)
        def _():
            dg_ref[...] = jnp.zeros_like(dg_ref)
            db_ref[...] = jnp.zeros_like(db_ref)

        dq = dq_ref[...]
        dkv = dkv_ref[...]
        dkvb = dkv.astype(bf16)
        dkvb_ref[...] = dkvb
        dh = _dot(dq, wt_ref[0:qd, :]) + _dot(dkvb, wt_ref[qd:QKV, :])
        g = g_ref[...]
        _, xn, r = _rms_fwd(x_ref[...], g)
        dx, dg = _rms_bwd(xn, r, g, dh)
        dx_ref[...] = dres_ref[...] + dx
        dg_ref[...] += dg
        db_ref[:, 0:qd] += _colsum8(dq.astype(f32))
        db_ref[:, qd:QKV] += _colsum8(dkv)

    return pl.pallas_call(
        body, grid=(s // tm,), name="qkv_bwd",
        in_specs=[_row(tm, qd), _row(tm, kvw), _row(tm, D), _const((1, D)), _const((QKV, D)), _row(tm, D)],
        out_specs=[_row(tm, D), _acc((SUB, D)), _acc((SUB, QKV)), _row(tm, kvw)],
        out_shape=[SDS((s, D), f32), SDS((SUB, D), f32), SDS((SUB, QKV), f32), SDS((s, kvw), bf16)],
        compiler_params=_cp("arbitrary"),
    )(dq, dkv, x, g, wt, dres)


def ffn_fwd(x, g, wup, wdw, bdw, wdn, tm, cw, rider=None):
    s = x.shape[0]
    tail = 8

    def body(x_ref, g_ref, wup_ref, wdw_ref, bdw_ref, wdn_ref, xo_ref, h_ref, up_ref, gate_ref, act_ref, carry_ref,
             ext_ref):
        @pl.when(pl.program_id(0) == 0)
        def _():
            carry_ref[...] = jnp.zeros_like(carry_ref)

        x = x_ref[...]
        h, _, _ = _rms_fwd(x, g_ref[...])
        hb = h.astype(bf16)
        h_ref[...] = hb
        for c in range(DFF // cw):
            cs = slice(c * cw, (c + 1) * cw)
            vs = slice(DFF + c * cw, DFF + (c + 1) * cw)
            ug = _dot(hb, wup_ref[:, cs])
            uv = _dot(hb, wup_ref[:, vs])
            up_ref[:, cs] = ug.astype(bf16)
            up_ref[:, vs] = uv.astype(bf16)
            ext_ref[0:tail, :] = carry_ref[:, cs]
            ext_ref[tail:tail + tm, :] = ug
            carry_ref[:, cs] = ug[tm - tail:, :]
            ext = ext_ref[...]
            gate = (wdw_ref[0:1, cs] * pltpu.roll(ext, 2, 0)[tail:] + wdw_ref[1:2, cs] * pltpu.roll(ext, 1, 0)[tail:]
                    + wdw_ref[2:3, cs] * ug) + bdw_ref[:, cs]
            gate_ref[:, cs] = gate.astype(bf16)
            act_ref[:, cs] = (gate * _sigmoid(gate) * uv).astype(bf16)
        xo_ref[...] = x + _dot(act_ref[...], wdn_ref[...])

    return _hosted(
        rider, body, grid=(s // tm,), name="ffn_fwd",
        in_specs=[_row(tm, D), _const((1, D)), _const((D, 2 * DFF)), _const((3, DFF)), _const((1, DFF)),
                  _const((DFF, D))],
        out_specs=[_row(tm, D), _row(tm, D), _row(tm, 2 * DFF), _row(tm, DFF), _row(tm, DFF)],
        out_shape=[SDS((s, D), f32), SDS((s, D), bf16), SDS((s, 2 * DFF), bf16), SDS((s, DFF), bf16),
                   SDS((s, DFF), bf16)],
        scratch_shapes=[pltpu.VMEM((tail, DFF), f32), pltpu.VMEM((tail + tm, cw), f32)],
        compiler_params=_cp("arbitrary"),
    )(x, g, wup, wdw, bdw, wdn)


def ffn_bwd(dxo, x, g, up, gate, wdw, wdnt, wupt, tm, cw, rider=None):
    s = x.shape[0]
    nt = s // tm
    rev = lambda i: (nt - 1 - i, 0)

    def body(dxo_ref, x_ref, g_ref, up_ref, gate_ref, wdw_ref, wdnt_ref, wupt_ref,
             dxi_ref, dup_ref, dg_ref, dwdw_ref, dbdw_ref, carry_ref, ext2_ref):
        i = pl.program_id(0)

        @pl.when(i == 0)
        def _():
            carry_ref[...] = jnp.zeros_like(carry_ref)
            dg_ref[...] = jnp.zeros_like(dg_ref)
            dwdw_ref[...] = jnp.zeros_like(dwdw_ref)
            dbdw_ref[...] = jnp.zeros_like(dbdw_ref)

        dxo = dxo_ref[...]
        dxb = dxo.astype(bf16)
        for c in range(DFF // cw):
            cs = slice(c * cw, (c + 1) * cw)
            vs = slice(DFF + c * cw, DFF + (c + 1) * cw)
            d_act = _dot(dxb, wdnt_ref[:, cs])
            ug = up_ref[:, cs].astype(f32)
            uv = up_ref[:, vs].astype(f32)
            gate = gate_ref[:, cs].astype(f32)
            sg = _sigmoid(gate)
            dup_ref[:, vs] = (d_act * (gate * sg)).astype(bf16)
            d_gate = d_act * uv * _dsilu(gate, sg)
            ext2_ref[0:tm, :] = d_gate
            ext2_ref[tm:tm + 8, :] = carry_ref[:, cs]
            carry_ref[:, cs] = d_gate[0:8, :]
            ext = ext2_ref[...]
            ahead1 = pltpu.roll(ext, tm + 8 - 1, 0)[:tm]
            ahead2 = pltpu.roll(ext, tm + 8 - 2, 0)[:tm]
            dbdw_ref[:, cs] += _colsum8(d_gate)
            dwdw_ref[0, :, cs] += _colsum8(ahead2 * ug)
            dwdw_ref[1, :, cs] += _colsum8(ahead1 * ug)
            dwdw_ref[2, :, cs] += _colsum8(d_gate * ug)
            d_ug = wdw_ref[0:1, cs] * ahead2 + wdw_ref[1:2, cs] * ahead1 + wdw_ref[2:3, cs] * d_gate
            dup_ref[:, cs] = d_ug.astype(bf16)
        dh = _dot(dup_ref[...], wupt_ref[...])
        gv = g_ref[...]
        _, xn, r = _rms_fwd(x_ref[...], gv)
        dx, dg = _rms_bwd(xn, r, gv, dh)
        dxi_ref[...] = dxo + dx
        dg_ref[...] += dg

    return _hosted(
        rider, body, grid=(nt,), name="ffn_bwd",
        in_specs=[
            pl.BlockSpec((tm, D), rev), pl.BlockSpec((tm, D), rev), _const((1, D)),
            pl.BlockSpec((tm, 2 * DFF), rev), pl.BlockSpec((tm, DFF), rev),
            _const((3, DFF)), _const((D, DFF)), _const((2 * DFF, D)),
        ],
        out_specs=[pl.BlockSpec((tm, D), rev), pl.BlockSpec((tm, 2 * DFF), rev), _acc((SUB, D)),
                   _acc((3, SUB, DFF)), _acc((SUB, DFF))],
        out_shape=[SDS((s, D), f32), SDS((s, 2 * DFF), bf16), SDS((SUB, D), f32), SDS((3, SUB, DFF), f32),
                   SDS((SUB, DFF), f32)],
        scratch_shapes=[pltpu.VMEM((8, DFF), f32), pltpu.VMEM((tm + 8, cw), f32)],
        compiler_params=_cp("arbitrary"),
    )(dxo, x, g, up, gate, wdw, wdnt, wupt)


def pw1_fwd(x, g, w, b, tm):
    s = x.shape[0]

    def body(x_ref, g_ref, w_ref, b_ref, h_ref, a_ref, u_ref):
        h, _, _ = _rms_fwd(x_ref[...], g_ref[...])
        hb = h.astype(bf16)
        h_ref[...] = hb
        a = _dot(hb, w_ref[...]) + b_ref[...]
        a_ref[...] = a.astype(bf16)
        u_ref[...] = a[:, :D] * _sigmoid(a[:, D:])

    return pl.pallas_call(
        body, grid=(s // tm,), name="pw1_fwd",
        in_specs=[_row(tm, D), _const((1, D)), _const((D, 2 * D)), _const((1, 2 * D))],
        out_specs=[_row(tm, D), _row(tm, 2 * D), _row(tm, D)],
        out_shape=[SDS((s, D), bf16), SDS((s, 2 * D), bf16), SDS((s, D), f32)],
        compiler_params=_cp("parallel"),
    )(x, g, w, b)


def _ln_silu(c, lg, lb):
    mu = jnp.mean(c, axis=-1, keepdims=True)
    cc = c - mu
    var = jnp.mean(cc * cc, axis=-1, keepdims=True)
    rstd = lax.rsqrt(var + LN_EPS)
    xh = cc * rstd
    ln = xh * lg + lb
    sg = _sigmoid(ln)
    return xh, rstd, ln, sg


def _shifted_copies(ext_ref, sh_ref, cs, tm):
    n = CONV_HALO - SUB + tm
    for k in range(1, SUB):
        sh_ref[k - 1] = ext_ref[pl.ds(k, n), cs]


def _shifted_rows(ext_ref, sh_ref, cs, start, rows):
    q, k = divmod(start, SUB)
    if k == 0:
        return ext_ref[pl.ds(start, rows), cs]
    return sh_ref[k - 1, pl.ds(q * SUB, rows), :]


def conv_fwd(u, x, wdw, bdw, lg, lb, w2, b2, tm, rc):
    s = x.shape[0]
    hl = CONV_HALO
    off = hl - (CONV_W - 1)

    def body(u_ref, halo_ref, x_ref, wdw_ref, bdw_ref, lg_ref, lb_ref, w2_ref, b2_ref, c_ref, xo_ref, ext_ref, sh_ref):
        has_prev = (pl.program_id(0) > 0).astype(f32)
        ext_ref[0:hl, :] = halo_ref[...] * has_prev
        ext_ref[hl:hl + tm, :] = u_ref[...]
        for cc in range(D // LANES):
            cs = slice(cc * LANES, (cc + 1) * LANES)
            _shifted_copies(ext_ref, sh_ref, cs, tm)
            for rr in range(tm // rc):
                acc = jnp.zeros((rc, LANES), f32) + bdw_ref[:, cs]
                for j in range(CONV_W):
                    acc = acc + wdw_ref[j:j + 1, cs] * _shifted_rows(ext_ref, sh_ref, cs, rr * rc + off + j, rc)
                c_ref[rr * rc:(rr + 1) * rc, cs] = acc
        _, _, ln, sg = _ln_silu(c_ref[...], lg_ref[...], lb_ref[...])
        xo_ref[...] = x_ref[...] + _dot((ln * sg).astype(bf16), w2_ref[...]) + b2_ref[...]

    return pl.pallas_call(
        body, grid=(s // tm,), name="conv_fwd",
        in_specs=[_row(tm, D), pl.BlockSpec((hl, D), lambda i: (jnp.maximum(i * (tm // hl) - 1, 0), 0)), _row(tm, D),
                  _const((CONV_W, D)), _const((1, D)), _const((1, D)), _const((1, D)), _const((D, D)), _const((1, D))],
        out_specs=[_row(tm, D), _row(tm, D)],
        out_shape=[SDS((s, D), f32), SDS((s, D), f32)],
        scratch_shapes=[pltpu.VMEM((hl + tm, D), f32), pltpu.VMEM((SUB - 1, hl - SUB + tm, LANES), f32)],
        compiler_params=_cp("parallel"),
    )(u, u, x, wdw, bdw, lg, lb, w2, b2)


def conv_bwd_a(dy, c, lg, lb, w2t, tm, rider=None):
    s = dy.shape[0]

    def body(dy_ref, c_ref, lg_ref, lb_ref, w2t_ref, dc_ref, u3_ref, dlg_ref, dlb_ref, db2_ref, dbdw_ref):
        @pl.when(pl.program_id(0) == 0)
        def _():
            for r in (dlg_ref, dlb_ref, db2_ref, dbdw_ref):
                r[...] = jnp.zeros_like(r)

        dy = dy_ref[...]
        lg = lg_ref[...]
        xh, rstd, ln, sg = _ln_silu(c_ref[...], lg, lb_ref[...])
        u3_ref[...] = (ln * sg).astype(bf16)
        du3 = _dot(dy.astype(bf16), w2t_ref[...])
        dln = du3 * _dsilu(ln, sg)
        dxh = dln * lg
        dc = rstd * (dxh - jnp.mean(dxh, axis=-1, keepdims=True) - xh * jnp.mean(dxh * xh, axis=-1, keepdims=True))
        dc_ref[...] = dc
        dlg_ref[...] += _colsum8(dln * xh)
        dlb_ref[...] += _colsum8(dln)
        db2_ref[...] += _colsum8(dy)
        dbdw_ref[...] += _colsum8(dc)

    return _hosted(
        rider, body, grid=(s // tm,), name="conv_bwd_a",
        in_specs=[_row(tm, D), _row(tm, D), _const((1, D)), _const((1, D)), _const((D, D))],
        out_specs=[_row(tm, D), _row(tm, D)] + [_acc((SUB, D))] * 4,
        out_shape=[SDS((s, D), f32), SDS((s, D), bf16)] + [SDS((SUB, D), f32)] * 4,
        compiler_params=_cp("arbitrary"),
    )(dy, c, lg, lb, w2t)


def conv_bwd_b(dc, u, a, wdw, x, g, w1t, dres, tm, rc, rider=None):
    s = x.shape[0]
    nt = s // tm
    hl = CONV_HALO
    off = hl - (CONV_W - 1)

    def body(dc_ref, dnext_ref, u_ref, uprev_ref, a_ref, wdw_ref, x_ref, g_ref, w1t_ref, dres_ref,
             dx_ref, da_ref, dwdw_ref, db1_ref, dg_ref, ext_ref, ext2_ref, du_ref, sh_ref, sh2_ref):
        i = pl.program_id(0)

        @pl.when(i == 0)
        def _():
            for r in (dwdw_ref, db1_ref, dg_ref):
                r[...] = jnp.zeros_like(r)

        ext_ref[0:hl, :] = uprev_ref[...] * (i > 0).astype(f32)
        ext_ref[hl:hl + tm, :] = u_ref[...]
        ext2_ref[0:tm, :] = dc_ref[...]
        ext2_ref[tm:tm + hl, :] = dnext_ref[...] * (i < nt - 1).astype(f32)
        for cc in range(D // LANES):
            cs = slice(cc * LANES, (cc + 1) * LANES)
            _shifted_copies(ext_ref, sh_ref, cs, tm)
            _shifted_copies(ext2_ref, sh2_ref, cs, tm)
            for rr in range(tm // rc):
                r0 = rr * rc
                dcb = ext2_ref[r0:r0 + rc, cs]
                acc = jnp.zeros((rc, LANES), f32)
                for j in range(CONV_W):
                    acc = acc + wdw_ref[j:j + 1, cs] * _shifted_rows(ext2_ref, sh2_ref, cs, r0 + CONV_W - 1 - j, rc)
                    dwdw_ref[j, :, cs] += _colsum8(dcb * _shifted_rows(ext_ref, sh_ref, cs, r0 + off + j, rc))
                du_ref[r0:r0 + rc, cs] = acc
        du = du_ref[...]
        a1 = a_ref[:, :D].astype(f32)
        sg = _sigmoid(a_ref[:, D:].astype(f32))
        da1 = du * sg
        da2 = du * a1 * sg * (1.0 - sg)
        da_ref[:, :D] = da1.astype(bf16)
        da_ref[:, D:] = da2.astype(bf16)
        db1_ref[:, :D] += _colsum8(da1)
        db1_ref[:, D:] += _colsum8(da2)
        dh = _dot(da_ref[...], w1t_ref[...])
        gv = g_ref[...]
        _, xn, r = _rms_fwd(x_ref[...], gv)
        dx, dg = _rms_bwd(xn, r, gv, dh)
        dx_ref[...] = dres_ref[...] + dx
        dg_ref[...] += dg

    blocks = tm // hl
    return _hosted(
        rider, body, grid=(nt,), name="conv_bwd_b",
        in_specs=[
            _row(tm, D), pl.BlockSpec((hl, D), lambda i: (jnp.minimum((i + 1) * blocks, s // hl - 1), 0)),
            _row(tm, D), pl.BlockSpec((hl, D), lambda i: (jnp.maximum(i * blocks - 1, 0), 0)),
            _row(tm, 2 * D), _const((CONV_W, D)), _row(tm, D), _const((1, D)), _const((2 * D, D)), _row(tm, D),
        ],
        out_specs=[_row(tm, D), _row(tm, 2 * D), _acc((CONV_W, SUB, D)), _acc((SUB, 2 * D)), _acc((SUB, D))],
        out_shape=[SDS((s, D), f32), SDS((s, 2 * D), bf16), SDS((CONV_W, SUB, D), f32), SDS((SUB, 2 * D), f32),
                   SDS((SUB, D), f32)],
        scratch_shapes=[pltpu.VMEM((hl + tm, D), f32), pltpu.VMEM((tm + hl, D), f32), pltpu.VMEM((tm, D), f32),
                        pltpu.VMEM((SUB - 1, hl - SUB + tm, LANES), f32),
                        pltpu.VMEM((SUB - 1, hl - SUB + tm, LANES), f32)],
        compiler_params=_cp("arbitrary"),
    )(dc, dc, u, u, a, wdw, x, g, w1t, dres)


def loss_bwd(x, g, t, tm):
    s = x.shape[0]

    def body(x_ref, g_ref, t_ref, dx_ref, dg_ref, loss_ref):
        @pl.when(pl.program_id(0) == 0)
        def _():
            dg_ref[...] = jnp.zeros_like(dg_ref)
            loss_ref[...] = jnp.zeros_like(loss_ref)

        gv = g_ref[...]
        y, xn, r = _rms_fwd(x_ref[...], gv)
        e = y - t_ref[...]
        loss_ref[...] += 0.5 * jnp.sum(jnp.mean(e * e, axis=-1, keepdims=True), axis=0, keepdims=True)
        dx, dg = _rms_bwd(xn, r, gv, e / D)
        dx_ref[...] = dx
        dg_ref[...] += dg

    return pl.pallas_call(
        body, grid=(s // tm,), name="loss_bwd",
        in_specs=[_row(tm, D), _const((1, D)), _row(tm, D)],
        out_specs=[_row(tm, D), _acc((SUB, D)), _acc((1, 1))],
        out_shape=[SDS((s, D), f32), SDS((SUB, D), f32), SDS((1, 1), f32)],
        compiler_params=_cp("arbitrary"),
    )(x, g, t)


def wgrad(a, b, nb, tk, name, rider=None):
    s, k1 = a.shape
    n = b.shape[1]

    def body(a_ref, b_ref, o_ref):
        @pl.when(pl.program_id(1) == 0)
        def _():
            o_ref[...] = jnp.zeros_like(o_ref)

        o_ref[...] += lax.dot_general(a_ref[...], b_ref[...].astype(bf16), _TN, preferred_element_type=f32)

    return _hosted(
        rider, body, grid=(n // nb, s // tk), name=name,
        in_specs=[pl.BlockSpec((tk, k1), lambda j, k: (k, 0)), pl.BlockSpec((tk, nb), lambda j, k: (k, j))],
        out_specs=pl.BlockSpec((k1, nb), lambda j, k: (0, j)),
        out_shape=SDS((k1, n), f32),
        compiler_params=_cp("parallel", "arbitrary"),
    )(a, b)


def wgrad_cols(a, b, tk, name, rider=None):
    s, k1 = a.shape
    w = b.shape[1] // N_CHIPS

    def body(a_ref, b_ref, o_ref):
        @pl.when(pl.program_id(1) == 0)
        def _():
            o_ref[...] = jnp.zeros_like(o_ref)

        acc = lax.dot_general(a_ref[...], b_ref[...].astype(bf16), _TN, preferred_element_type=f32)
        o_ref[:, 0] += acc.reshape(2, k1 // 2, w)

    return _hosted(
        rider, body, grid=(N_CHIPS, s // tk), name=name,
        in_specs=[pl.BlockSpec((tk, k1), lambda j, k: (k, 0)), pl.BlockSpec((tk, w), lambda j, k: (k, j))],
        out_specs=pl.BlockSpec((2, 1, k1 // 2, w), lambda j, k: (0, j, 0, 0)),
        out_shape=SDS((2, N_CHIPS, k1 // 2, w), f32),
        compiler_params=_cp("parallel", "arbitrary"),
    )(a, b)


def wgrad_rows(a, b, nb, tk, name, rider=None):
    s, k1 = a.shape
    n = b.shape[1]
    r = k1 // (2 * N_CHIPS)

    def body(a_ref, b_ref, o_ref):
        @pl.when(pl.program_id(1) == 0)
        def _():
            o_ref[...] = jnp.zeros_like(o_ref)

        acc = lax.dot_general(a_ref[...], b_ref[...].astype(bf16), _TN, preferred_element_type=f32)
        for j in range(N_CHIPS):
            for h in range(2):
                o_ref[h, j] += acc[(2 * j + h) * r:(2 * j + h + 1) * r, :]

    return _hosted(
        rider, body, grid=(n // nb, s // tk), name=name,
        in_specs=[pl.BlockSpec((tk, k1), lambda j, k: (k, 0)), pl.BlockSpec((tk, nb), lambda j, k: (k, j))],
        out_specs=pl.BlockSpec((2, N_CHIPS, r, nb), lambda j, k: (0, 0, 0, j)),
        out_shape=SDS((2, N_CHIPS, r, n), f32),
        compiler_params=_cp("parallel", "arbitrary"),
    )(a, b)


def _adam_math(w, g, m, v):
    m = B1 * m + (1.0 - B1) * g
    v = B2 * v + (1.0 - B2) * (g * g)
    m_hat = m / (1.0 - B1 ** STEP)
    v_hat = v / (1.0 - B2 ** STEP)
    delta = -LR * (m_hat / (jnp.sqrt(v_hat) + ADAM_EPS) + WD * w)
    return delta, m, v


def _rows_tile(r, c, multiple=SUB):
    best = None
    for t in range(multiple, r + 1, multiple):
        if r % t == 0 and t * c * 4 <= ELEMENTWISE_BLOCK_BYTES:
            best = t
    return best if best is not None else r


def adamw(w, g, m, v, name):
    l, r, c = w.shape
    tr = _rows_tile(r, c)
    spec = pl.BlockSpec((1, tr, c), lambda i, j: (i, j, 0))

    def body(w_ref, g_ref, m_ref, v_ref, d_ref, mo_ref, vo_ref):
        d, mn, vn = _adam_math(w_ref[...], g_ref[...], m_ref[...], v_ref[...])
        d_ref[...] = d
        mo_ref[...] = mn
        vo_ref[...] = vn

    return pl.pallas_call(
        body, grid=(l, r // tr), name=name, in_specs=[spec] * 4, out_specs=[spec] * 3,
        out_shape=[SDS((l, r, c), f32)] * 3, compiler_params=_cp("parallel", "parallel"),
    )(w, g, m, v)


def _place():
    return lax.axis_index("x"), lax.axis_index("y"), lax.axis_index("c")


def _chip_peer(xi, yi, r):
    px = 1 - xi if r & 2 else xi
    py = 1 - yi if r & 1 else yi
    return px, py


def _gv_qkv(src, dst, j, h):
    rows = pl.ds(h * (D // 2), D // 2)
    return src.at[rows, :], dst.at[j, rows, :]


def _gv_rows(src, dst, j, h):
    r = src.shape[0] // 2
    return src.at[pl.ds(h * r, r), :], dst.at[pl.ds(j * 2 * r + h * r, r), :]


def _gv_cols(src, dst, j, h):
    r, w = src.shape[0] // 2, src.shape[1]
    return src.at[pl.ds(h * r, r), :], dst.at[pl.ds(h * r, r), pl.ds(j * w, w)]


def gather_rider(big, small, forward_at):
    nb, ns = len(big), len(small)
    n = nb + ns
    views = [v for _, v, _ in big]

    def env(ins, outs, sems):
        ici_send, ici_recv, d2d_send, d2d_recv, loc_sems = sems
        xi, yi, ci = _place()
        me = 2 * xi + yi

        def local(a, h):
            if a < nb:
                src, dst = views[a](ins[a], outs[a], me, h)
                return pltpu.make_async_copy(src, dst, loc_sems.at[2 * a + h])
            return pltpu.make_async_copy(ins[a], outs[a].at[me], loc_sems.at[nb + a])

        def ici(a, r, slot):
            px, py = _chip_peer(xi, yi, r)
            src, dst = views[a](ins[a], outs[a], slot, ci) if a < nb else (ins[a], outs[a].at[slot])
            k = 3 * a + r - 1
            return pltpu.make_async_remote_copy(src_ref=src, dst_ref=dst, send_sem=ici_send.at[k],
                                                recv_sem=ici_recv.at[k], device_id=(px, py, ci), device_id_type=MESH)

        def d2d(a, r, half):
            px, py = _chip_peer(xi, yi, r)
            _, dst = views[a](ins[a], outs[a], 2 * px + py, half)
            k = 3 * a + r - 1
            return pltpu.make_async_remote_copy(src_ref=dst, dst_ref=dst, send_sem=d2d_send.at[k],
                                                recv_sem=d2d_recv.at[k], device_id=(xi, yi, 1 - ci), device_id_type=MESH)

        return xi, yi, ci, me, local, ici, d2d

    def locals_of():
        return [(a, h) for a in range(nb) for h in range(2)] + [(a, 0) for a in range(nb, n)]

    def send(ins, outs, sems):
        _, _, _, me, local, ici, _ = env(ins, outs, sems)
        for a, h in locals_of():
            local(a, h).start()
        for a in range(n):
            for r in (1, 2, 3):
                ici(a, r, me).start()

    def forward(ins, outs, sems):
        xi, yi, ci, _, _, ici, d2d = env(ins, outs, sems)
        for a in range(n):
            for r in (1, 2, 3):
                px, py = _chip_peer(xi, yi, r)
                ici(a, r, 2 * px + py).wait_recv()
                if a < nb:
                    d2d(a, r, ci).start()

    def finish(ins, outs, sems):
        _, _, ci, me, local, ici, d2d = env(ins, outs, sems)
        for a in range(nb):
            for r in (1, 2, 3):
                d2d(a, r, 1 - ci).wait_recv()
        for a in range(n):
            for r in (1, 2, 3):
                ici(a, r, me).wait_send()
                if a < nb:
                    d2d(a, r, ci).wait_send()
        for a, h in locals_of():
            local(a, h).wait()

    dma = pltpu.SemaphoreType.DMA
    return Rider(
        ins=[b for b, _, _ in big] + list(small),
        out_shape=[SDS(shape, bf16) for _, _, shape in big] + [SDS((N_CHIPS,) + a.shape, a.dtype) for a in small],
        sem_shapes=[dma((3 * n,)), dma((3 * n,)), dma((max(3 * nb, 1),)), dma((max(3 * nb, 1),)), dma((2 * nb + ns,))],
        stages=[(0.0, send), (forward_at, forward)], final=finish)


def pair_send_rider(gs):
    n = len(gs)

    def copy(ins, outs, sems, a):
        xi, yi, ci = _place()
        return pltpu.make_async_remote_copy(
            src_ref=ins[a].at[1 - ci], dst_ref=outs[a], send_sem=sems[0].at[a], recv_sem=sems[1].at[a],
            device_id=(xi, yi, 1 - ci), device_id_type=MESH)

    def send(ins, outs, sems):
        for a in range(n):
            copy(ins, outs, sems, a).start()

    def finish(ins, outs, sems):
        for a in range(n):
            copy(ins, outs, sems, a).wait()

    dma = pltpu.SemaphoreType.DMA
    return Rider(ins=list(gs), out_shape=[SDS(g.shape[1:], g.dtype) for g in gs], sem_shapes=[dma((n,)), dma((n,))],
                 stages=[(0.0, send)], final=finish)


def chip_rider(pbs, p32s, forward_at):
    n = len(pbs)

    def env(ins, outs, sems):
        ici_send, ici_recv, d2d_send, d2d_recv, own_send, own_recv, loc_sems = sems
        pb, p32, recv, own = ins[:n], ins[n:], outs[:n], outs[n:]
        xi, yi, ci = _place()
        me = 2 * xi + yi
        sib = (xi, yi, 1 - ci)

        def ici(a, r, src_slot, dst_slot):
            px, py = _chip_peer(xi, yi, r)
            k = 3 * a + r - 1
            return pltpu.make_async_remote_copy(
                src_ref=pb[a].at[src_slot], dst_ref=recv[a].at[ci, dst_slot], send_sem=ici_send.at[k],
                recv_sem=ici_recv.at[k], device_id=(px, py, ci), device_id_type=MESH)

        def d2d(a, r, half):
            px, py = _chip_peer(xi, yi, r)
            blk = recv[a].at[half, 2 * px + py]
            k = 3 * a + r - 1
            return pltpu.make_async_remote_copy(src_ref=blk, dst_ref=blk, send_sem=d2d_send.at[k],
                                                recv_sem=d2d_recv.at[k], device_id=sib, device_id_type=MESH)

        def mine(a, half):
            return pltpu.make_async_remote_copy(src_ref=p32[a].at[me], dst_ref=own[a].at[half], send_sem=own_send.at[a],
                                                recv_sem=own_recv.at[a], device_id=sib, device_id_type=MESH)

        def local(a):
            return pltpu.make_async_copy(p32[a].at[me], own[a].at[ci], loc_sems.at[a])

        return xi, yi, ci, me, ici, d2d, mine, local

    def send(ins, outs, sems):
        xi, yi, ci, me, ici, _, mine, local = env(ins, outs, sems)
        for a in range(n):
            local(a).start()
            mine(a, ci).start()
            for r in (1, 2, 3):
                px, py = _chip_peer(xi, yi, r)
                ici(a, r, 2 * px + py, me).start()

    def forward(ins, outs, sems):
        xi, yi, ci, me, ici, d2d, _, _ = env(ins, outs, sems)
        for a in range(n):
            for r in (1, 2, 3):
                px, py = _chip_peer(xi, yi, r)
                ici(a, r, me, 2 * px + py).wait_recv()
                d2d(a, r, ci).start()

    def finish(ins, outs, sems):
        xi, yi, ci, me, ici, d2d, mine, local = env(ins, outs, sems)
        for a in range(n):
            mine(a, 1 - ci).wait_recv()
            for r in (1, 2, 3):
                d2d(a, r, 1 - ci).wait_recv()
        for a in range(n):
            mine(a, ci).wait_send()
            local(a).wait()
            for r in (1, 2, 3):
                px, py = _chip_peer(xi, yi, r)
                ici(a, r, 2 * px + py, me).wait_send()
                d2d(a, r, ci).wait_send()

    dma = pltpu.SemaphoreType.DMA
    return Rider(
        ins=list(pbs) + list(p32s),
        out_shape=[SDS((2,) + p.shape, bf16) for p in pbs] + [SDS((2,) + p.shape[1:], f32) for p in p32s],
        sem_shapes=[dma((3 * n,)), dma((3 * n,)), dma((3 * n,)), dma((3 * n,)), dma((n,)), dma((n,)), dma((n,))],
        stages=[(0.0, send), (forward_at, forward)], final=finish)


def rs_pair_add(g, other, ci, name):
    _, nsh, r, w = g.shape
    tr = _rows_tile(r, w, 16)

    def body(c_ref, g_ref, o_ref, p_ref, pb_ref):
        p = g_ref[0] + o_ref[...]
        p_ref[...] = p
        pb_ref[...] = p.astype(bf16)

    blk = pl.BlockSpec((1, tr, w), lambda j, i, c: (j, i, 0))
    return pl.pallas_call(
        body, name=name,
        grid_spec=pltpu.PrefetchScalarGridSpec(
            num_scalar_prefetch=1, grid=(nsh, r // tr),
            in_specs=[pl.BlockSpec((1, 1, tr, w), lambda j, i, c: (c[0], j, i, 0)), blk], out_specs=[blk, blk]),
        out_shape=[SDS((nsh, r, w), f32), SDS((nsh, r, w), bf16)],
        compiler_params=_cp("parallel", "parallel"),
    )(ci.reshape(1).astype(jnp.int32), g, other)


def rs_chip_add(own, recv, chip, name):
    _, nsh, r, w = recv.shape
    tr = _rows_tile(r, w, 16)

    def body(c_ref, own_ref, recv_ref, o_ref):
        me = c_ref[0]
        acc = None
        for j in range(N_CHIPS):
            term = jnp.where(me == j, own_ref[0], recv_ref[0, j].astype(f32))
            acc = term if acc is None else acc + term
        o_ref[0] = acc

    return pl.pallas_call(
        body, name=name,
        grid_spec=pltpu.PrefetchScalarGridSpec(
            num_scalar_prefetch=1, grid=(2, r // tr),
            in_specs=[pl.BlockSpec((1, tr, w), lambda h, i, c: (h, i, 0)),
                      pl.BlockSpec((1, nsh, tr, w), lambda h, i, c: (h, 0, i, 0))],
            out_specs=pl.BlockSpec((1, tr, w), lambda h, i, c: (h, i, 0))),
        out_shape=SDS((2, r, w), f32),
        compiler_params=_cp("parallel", "parallel"),
    )(chip.reshape(1).astype(jnp.int32), own, recv)


class GradReduction:
    def __init__(self, keys, grads, ci, chip):
        self.keys, self.grads, self.ci, self.chip = keys, grads, ci, chip

    def pair_rider(self):
        return pair_send_rider(self.grads)

    def chip_rider(self, from_pair, forward_at):
        both = [rs_pair_add(g, o, self.ci, "rs_pair_add_" + k) for k, g, o in zip(self.keys, self.grads, from_pair)]
        return chip_rider([pb for _, pb in both], [p for p, _ in both], forward_at)

    def result(self, landed):
        n = len(self.keys)
        return {k: rs_chip_add(own, recv, self.chip, "rs_chip_add_" + k)
                for k, recv, own in zip(self.keys, landed[:n], landed[n:])}

    def alone(self):
        from_pair = run_rider(self.pair_rider(), "rs_pair_" + self.keys[0])
        return self.result(run_rider(self.chip_rider(from_pair, 0.0), "rs_chip_" + self.keys[0]))


STAGE_W = DFF
_ST = {"norm_mix0": 0, "norm_mix1": 1, "attn_b_qkv": 2, "attn_sinks": 3, "attn_b_o": 4, "final_norm": 5, "loss": 6,
       "norm_ffn0": 8, "norm_ffn1": 9, "ffn_b_dw0": 10, "ffn_b_dw1": 11, "conv_b_pw1": 12, "conv_b_dw": 13,
       "conv_ln_g": 14, "conv_ln_b": 15, "conv_b_pw2": 16, "ffn_w_dw0": 17, "ffn_w_dw1": 20, "conv_w_dw": 24}
STAGE_ROWS = 56
SMALL_REP = ("norm_mix", "attn_b_qkv", "attn_sinks", "attn_b_o", "norm_ffn", "ffn_b_dw", "final_norm")
SMALL_SH = ("conv_b_pw1", "conv_w_dw", "conv_b_dw", "conv_ln_g", "conv_ln_b", "conv_b_pw2", "ffn_w_dw")
_SMALL_PARTS = ("norm_mix0", "norm_mix1", "attn_b_qkv", "attn_sinks", "attn_b_o", "norm_ffn0", "norm_ffn1", "ffn_b_dw0",
                "ffn_b_dw1", "final_norm", "conv_b_pw1", "conv_b_dw", "conv_ln_g", "conv_ln_b", "conv_b_pw2", "loss",
                "ffn_w_dw0", "ffn_w_dw1", "conv_w_dw")


def small_reduce_adamw(parts, w, m, v):
    names = SMALL_REP + SMALL_SH
    npart, nw = len(_SMALL_PARTS), len(names)

    def body(*refs):
        part = dict(zip(_SMALL_PARTS, refs[:npart]))
        off = npart
        w_ref = dict(zip(names, refs[off:off + nw]))
        m_ref = dict(zip(names, refs[off + nw:off + 2 * nw]))
        v_ref = dict(zip(names, refs[off + 2 * nw:off + 3 * nw]))
        off += 3 * nw
        loss_ref = refs[off]
        g_out = dict(zip(names, refs[off + 1:off + 1 + nw]))
        d_out = dict(zip(names, refs[off + 1 + nw:off + 1 + 2 * nw]))
        m_out = dict(zip(names, refs[off + 1 + 2 * nw:off + 1 + 3 * nw]))
        v_out = dict(zip(names, refs[off + 1 + 3 * nw:off + 1 + 4 * nw]))
        stage_ref, buf_ref, tot_ref, send_sems, recv_sems = refs[off + 1 + 4 * nw:]

        xi, yi, ci = _place()
        me = 4 * xi + 2 * yi + ci
        chip = 2 * xi + yi

        stage_ref[...] = jnp.zeros_like(stage_ref)
        for name in _SMALL_PARTS:
            ref, r0 = part[name], _ST[name]
            if name in ("attn_sinks", "loss"):
                val = ref[...]
            elif name in ("ffn_w_dw0", "ffn_w_dw1", "conv_w_dw"):
                val = jnp.sum(ref[...], axis=1)
            else:
                val = jnp.sum(ref[...], axis=0, keepdims=True)
            stage_ref[r0:r0 + val.shape[0], 0:val.shape[1]] = val

        buf_ref[me] = stage_ref[...]

        def peer(r):
            px, py = _chip_peer(xi, yi, r >> 1)
            return px, py, (1 - ci if r & 1 else ci)

        def copy(r, slot):
            return pltpu.make_async_remote_copy(
                src_ref=stage_ref, dst_ref=buf_ref.at[slot], send_sem=send_sems.at[r - 1], recv_sem=recv_sems.at[r - 1],
                device_id=peer(r), device_id_type=MESH)

        sends = []
        for r in range(1, N_DEV):
            cp = copy(r, me)
            cp.start()
            sends.append(cp)
        for r in range(1, N_DEV):
            px, py, pc = peer(r)
            copy(r, 4 * px + 2 * py + pc).wait_recv()
        for cp in sends:
            cp.wait_send()
        acc = buf_ref[0]
        for d in range(1, N_DEV):
            acc = acc + buf_ref[d]
        tot_ref[...] = acc

        def rows(name, n, width):
            r0 = _ST[name]
            return tot_ref[r0:r0 + n, 0:width]

        def mine(name, n, width):
            r0 = _ST[name]
            out = tot_ref[r0:r0 + n, 0:width]
            for j in range(1, N_CHIPS):
                out = jnp.where(chip == j, tot_ref[r0:r0 + n, j * width:(j + 1) * width], out)
            return out

        loss_ref[...] = rows("loss", 1, 1)
        grads = {
            "norm_mix": rows("norm_mix0", 2, D), "attn_b_qkv": rows("attn_b_qkv", 1, QKV),
            "attn_sinks": rows("attn_sinks", 1, N_HEADS), "attn_b_o": rows("attn_b_o", 1, D),
            "norm_ffn": rows("norm_ffn0", 2, D), "ffn_b_dw": rows("ffn_b_dw0", 2, DFF),
            "final_norm": rows("final_norm", 1, D),
            "conv_b_pw1": mine("conv_b_pw1", 1, 2 * D // N_CHIPS), "conv_w_dw": mine("conv_w_dw", CONV_W, D // N_CHIPS),
            "conv_b_dw": mine("conv_b_dw", 1, D // N_CHIPS), "conv_ln_g": mine("conv_ln_g", 1, D // N_CHIPS),
            "conv_ln_b": mine("conv_ln_b", 1, D // N_CHIPS), "conv_b_pw2": mine("conv_b_pw2", 1, D // N_CHIPS),
        }
        for name in names:
            if name == "ffn_w_dw":
                continue
            at = 0 if name == "conv_w_dw" else Ellipsis
            g = grads[name]
            d, mn, vn = _adam_math(w_ref[name][at], g, m_ref[name][at], v_ref[name][at])
            g_out[name][at] = g
            d_out[name][at] = d
            m_out[name][at] = mn
            v_out[name][at] = vn
        for layer, key in enumerate(("ffn_w_dw0", "ffn_w_dw1")):
            g = mine(key, 3, DFF // N_CHIPS)
            d, mn, vn = _adam_math(w_ref["ffn_w_dw"][layer], g, m_ref["ffn_w_dw"][layer], v_ref["ffn_w_dw"][layer])
            g_out["ffn_w_dw"][layer] = g
            d_out["ffn_w_dw"][layer] = d
            m_out["ffn_w_dw"][layer] = mn
            v_out["ffn_w_dw"][layer] = vn

    ins = [parts[k] for k in _SMALL_PARTS] + [src[k] for src in (w, m, v) for k in names]
    wshapes = [SDS(w[k].shape, f32) for k in names]
    outs = pl.pallas_call(
        body, name="small_reduce_adamw", in_specs=[_VMEM] * len(ins), out_specs=[_VMEM] * (1 + 4 * nw),
        out_shape=[SDS((1, 1), f32)] + wshapes * 4,
        scratch_shapes=[pltpu.VMEM((STAGE_ROWS, STAGE_W), f32), pltpu.VMEM((N_DEV, STAGE_ROWS, STAGE_W), f32),
                        pltpu.VMEM((STAGE_ROWS, STAGE_W), f32), pltpu.SemaphoreType.DMA((N_DEV - 1,)),
                        pltpu.SemaphoreType.DMA((N_DEV - 1,))],
        compiler_params=pltpu.CompilerParams(vmem_limit_bytes=VMEM_LIMIT),
    )(*ins)
    loss = outs[0]
    g, d, mn, vn = (dict(zip(names, outs[1 + k * nw:1 + (k + 1) * nw])) for k in range(4))
    return loss, g, d, mn, vn


TM = 512
TM_FFN = 256
FFN_CHUNK = 256
CONV_ROWS = 128
CONV_BWD_ROWS = 64
TK = 2048
FORWARD_AT = 0.6


def kernel(x, norm_mix, attn_w_qkv, attn_b_qkv, attn_sinks, attn_w_o, attn_b_o, conv_w_pw1, conv_b_pw1, conv_w_dw, conv_b_dw, conv_ln_g, conv_ln_b, conv_w_pw2, conv_b_pw2, norm_ffn, ffn_w_up, ffn_w_dw, ffn_b_dw, ffn_w_down, final_norm, loss_target, m_norm_mix, m_attn_w_qkv, m_attn_b_qkv, m_attn_sinks, m_attn_w_o, m_attn_b_o, m_conv_w_pw1, m_conv_b_pw1, m_conv_w_dw, m_conv_b_dw, m_conv_ln_g, m_conv_ln_b, m_conv_w_pw2, m_conv_b_pw2, m_norm_ffn, m_ffn_w_up, m_ffn_w_dw, m_ffn_b_dw, m_ffn_w_down, m_final_norm, v_norm_mix, v_attn_w_qkv, v_attn_b_qkv, v_attn_sinks, v_attn_w_o, v_attn_b_o, v_conv_w_pw1, v_conv_b_pw1, v_conv_w_dw, v_conv_b_dw, v_conv_ln_g, v_conv_ln_b, v_conv_w_pw2, v_conv_b_pw2, v_norm_ffn, v_ffn_w_up, v_ffn_w_dw, v_ffn_b_dw, v_ffn_w_down, v_final_norm):
    w = dict(norm_mix=norm_mix, attn_w_qkv=attn_w_qkv, attn_b_qkv=attn_b_qkv, attn_sinks=attn_sinks, attn_w_o=attn_w_o,
             attn_b_o=attn_b_o, conv_w_pw1=conv_w_pw1, conv_b_pw1=conv_b_pw1, conv_w_dw=conv_w_dw, conv_b_dw=conv_b_dw,
             conv_ln_g=conv_ln_g, conv_ln_b=conv_ln_b, conv_w_pw2=conv_w_pw2, conv_b_pw2=conv_b_pw2, norm_ffn=norm_ffn,
             ffn_w_up=ffn_w_up, ffn_w_dw=ffn_w_dw, ffn_b_dw=ffn_b_dw, ffn_w_down=ffn_w_down, final_norm=final_norm)
    mom = dict(norm_mix=m_norm_mix, attn_w_qkv=m_attn_w_qkv, attn_b_qkv=m_attn_b_qkv, attn_sinks=m_attn_sinks,
               attn_w_o=m_attn_w_o, attn_b_o=m_attn_b_o, conv_w_pw1=m_conv_w_pw1, conv_b_pw1=m_conv_b_pw1,
               conv_w_dw=m_conv_w_dw, conv_b_dw=m_conv_b_dw, conv_ln_g=m_conv_ln_g, conv_ln_b=m_conv_ln_b,
               conv_w_pw2=m_conv_w_pw2, conv_b_pw2=m_conv_b_pw2, norm_ffn=m_norm_ffn, ffn_w_up=m_ffn_w_up,
               ffn_w_dw=m_ffn_w_dw, ffn_b_dw=m_ffn_b_dw, ffn_w_down=m_ffn_w_down, final_norm=m_final_norm)
    vel = dict(norm_mix=v_norm_mix, attn_w_qkv=v_attn_w_qkv, attn_b_qkv=v_attn_b_qkv, attn_sinks=v_attn_sinks,
               attn_w_o=v_attn_w_o, attn_b_o=v_attn_b_o, conv_w_pw1=v_conv_w_pw1, conv_b_pw1=v_conv_b_pw1,
               conv_w_dw=v_conv_w_dw, conv_b_dw=v_conv_b_dw, conv_ln_g=v_conv_ln_g, conv_ln_b=v_conv_ln_b,
               conv_w_pw2=v_conv_w_pw2, conv_b_pw2=v_conv_b_pw2, norm_ffn=v_norm_ffn, ffn_w_up=v_ffn_w_up,
               ffn_w_dw=v_ffn_w_dw, ffn_b_dw=v_ffn_b_dw, ffn_w_down=v_ffn_w_down, final_norm=v_final_norm)
    order = ("norm_mix", "attn_w_qkv", "attn_b_qkv", "attn_sinks", "attn_w_o", "attn_b_o", "conv_w_pw1", "conv_b_pw1",
             "conv_w_dw", "conv_b_dw", "conv_ln_g", "conv_ln_b", "conv_w_pw2", "conv_b_pw2", "norm_ffn", "ffn_w_up",
             "ffn_w_dw", "ffn_b_dw", "ffn_w_down", "final_norm")
    xi, yi, ci = _place()
    chip = 2 * xi + yi
    xs, target = x[0], loss_target[0]
    s = xs.shape[0]
    tm, tmf, tk = min(TM, s), min(TM_FFN, s), min(TK, s)
    row = lambda v: v.reshape(1, -1)
    join = lambda a, axis: jnp.concatenate([a[j] for j in range(N_CHIPS)], axis=axis)
    cast = lambda a: a.astype(bf16)
    small, big = {}, {}

    got = run_rider(gather_rider(
        [(cast(attn_w_qkv[0]), _gv_qkv, (N_CHIPS, D, QKV // N_CHIPS)), (cast(attn_w_o[0]), _gv_rows, (D, D))],
        [w[k] for k in SMALL_SH], 0.0), "gather_attn")
    qkv4, w_o = got[:2]
    sm = dict(zip(SMALL_SH, got[2:]))
    w_qkv = jnp.transpose(qkv4, (1, 0, 2)).reshape(D, QKV)
    sinks = attn_sinks.reshape(N_HEADS)
    b_pw1, conv_dw, conv_bdw = join(sm["conv_b_pw1"], 1), join(sm["conv_w_dw"], 2)[0], join(sm["conv_b_dw"], 1)
    ln_g, ln_b, b_pw2, ffn_dw = (join(sm["conv_ln_g"], 1), join(sm["conv_ln_b"], 1), join(sm["conv_b_pw2"], 1),
                                 join(sm["ffn_w_dw"], 2))

    h0, qkv = qkv_fwd(xs, row(norm_mix[0]), w_qkv, attn_b_qkv, tm)
    (o, lse), (w_up0, w_dn0) = attn_fwd(qkv, sinks, rider=gather_rider(
        [(cast(ffn_w_up[0]), _gv_cols, (D, 2 * DFF)), (cast(ffn_w_down[0]), _gv_rows, (DFF, D))], [], FORWARD_AT))
    x1 = attn_out_fwd(xs, o, w_o, attn_b_o, tm)
    (x2, h1, up0, gate0, act0), (w_pw1, w_pw2, w_up1, w_dn1) = ffn_fwd(
        x1, row(norm_ffn[0]), w_up0, ffn_dw[0], row(ffn_b_dw[0]), w_dn0, tmf, FFN_CHUNK, rider=gather_rider(
            [(cast(conv_w_pw1[0]), _gv_cols, (D, 2 * D)), (cast(conv_w_pw2[0]), _gv_rows, (D, D)),
             (cast(ffn_w_up[1]), _gv_cols, (D, 2 * DFF)), (cast(ffn_w_down[1]), _gv_rows, (DFF, D))], [], FORWARD_AT))
    h2, a, u = pw1_fwd(x2, row(norm_mix[1]), w_pw1, b_pw1, tm)
    c, x3 = conv_fwd(u, x2, conv_dw, conv_bdw, ln_g, ln_b, w_pw2, b_pw2, tm, CONV_ROWS)
    x4, h3, up1, gate1, act1 = ffn_fwd(x3, row(norm_ffn[1]), w_up1, ffn_dw[1], row(ffn_b_dw[1]), w_dn1, tmf, FFN_CHUNK)

    dx4, small["final_norm"], small["loss"] = loss_bwd(x4, final_norm.reshape(1, D), target, tm)
    dx3, dup1, small["norm_ffn1"], small["ffn_w_dw1"], small["ffn_b_dw1"] = ffn_bwd(
        dx4, x3, row(norm_ffn[1]), up1, gate1, ffn_dw[1], w_dn1.T, w_up1.T, tmf, FFN_CHUNK)
    red1 = GradReduction(("up1", "down1"), [wgrad_cols(h3, dup1, tk, "wgrad_up1"),
                                           wgrad_rows(act1, dx4, 512, tk, "wgrad_down1")], ci, chip)
    (dc, u3, small["conv_ln_g"], small["conv_ln_b"], small["conv_b_pw2"], small["conv_b_dw"]), from_pair = conv_bwd_a(
        dx3, c, ln_g, ln_b, w_pw2.T, tm, rider=red1.pair_rider())
    g_pw2 = wgrad_rows(u3, dx3, 512, tk, "wgrad_pw2")
    (dx2, da, small["conv_w_dw"], small["conv_b_pw1"], small["norm_mix1"]), landed = conv_bwd_b(
        dc, u, a, conv_dw, x2, row(norm_mix[1]), w_pw1.T, dx3, tm, CONV_BWD_ROWS,
        rider=red1.chip_rider(from_pair, FORWARD_AT))
    big.update(red1.result(landed))
    g_pw1 = wgrad_cols(h2, da, tk, "wgrad_pw1")

    red2 = GradReduction(("pw1", "pw2"), [g_pw1, g_pw2], ci, chip)
    (dx1, dup0, small["norm_ffn0"], small["ffn_w_dw0"], small["ffn_b_dw0"]), from_pair = ffn_bwd(
        dx2, x1, row(norm_ffn[0]), up0, gate0, ffn_dw[0], w_dn0.T, w_up0.T, tmf, FFN_CHUNK, rider=red2.pair_rider())
    g_up0, landed = wgrad_cols(h1, dup0, tk, "wgrad_up0", rider=red2.chip_rider(from_pair, FORWARD_AT))
    big.update(red2.result(landed))
    g_dn0 = wgrad_rows(act0, dx2, 512, tk, "wgrad_down0")

    red3 = GradReduction(("up0", "down0", "wo"), [g_up0, g_dn0, wgrad_rows(o, dx1, 512, tk, "wgrad_o")], ci, chip)
    (do, small["attn_b_o"]), from_pair = attn_out_bwd(dx1, w_o.T, tm, rider=red3.pair_rider())
    (dq, dkv, small["attn_sinks"]), landed = attn_bwd(qkv, o, do, lse, sinks,
                                                      rider=red3.chip_rider(from_pair, FORWARD_AT))
    big.update(red3.result(landed))
    dx0, small["norm_mix0"], small["attn_b_qkv"], dkvb = qkv_bwd(dq, dkv, xs, row(norm_mix[0]), w_qkv.T, dx1, tm)
    g_qkv = jnp.concatenate([wgrad(h0, dq, 512, tk, "wgrad_q"), wgrad(h0, dkvb, 2 * N_KV * HD, tk, "wgrad_kv")], axis=1)
    g_qkv = jnp.transpose(g_qkv.reshape(2, D // 2, N_CHIPS, QKV // N_CHIPS), (0, 2, 1, 3))
    red4 = GradReduction(("qkv",), [g_qkv], ci, chip)
    big.update(red4.alone())

    gbig = {
        "attn_w_qkv": big["qkv"].reshape(1, D, QKV // N_CHIPS), "attn_w_o": big["wo"].reshape(1, D // N_CHIPS, D),
        "conv_w_pw1": big["pw1"].reshape(1, D, 2 * D // N_CHIPS), "conv_w_pw2": big["pw2"].reshape(1, D // N_CHIPS, D),
        "ffn_w_up": jnp.stack([big["up0"], big["up1"]]).reshape(2, D, 2 * DFF // N_CHIPS),
        "ffn_w_down": jnp.stack([big["down0"], big["down1"]]).reshape(2, DFF // N_CHIPS, D),
    }

    fix = lambda d: {**d, "final_norm": d["final_norm"].reshape(1, D)}
    loss, gs, ds, ms, vs = small_reduce_adamw(small, fix(w), fix(mom), fix(vel))
    unfix = lambda d: {**d, "final_norm": d["final_norm"].reshape(D)}
    gout, delta, new_m, new_v = unfix(gs), unfix(ds), unfix(ms), unfix(vs)

    for name, g in gbig.items():
        gout[name] = g
        delta[name], new_m[name], new_v[name] = adamw(w[name], g, mom[name], vel[name], "adamw_" + name)

    return (loss.reshape(()), dx0[None], *[gout[n] for n in order], *[delta[n] for n in order],
            *[new_m[n] for n in order], *[new_v[n] for n in order])
```

```python
import math

import jax
import jax.numpy as jnp
from jax import lax
from jax.experimental import pallas as pl
from jax.experimental.pallas import tpu as pltpu

f32 = jnp.float32
bf16 = jnp.bfloat16
SDS = jax.ShapeDtypeStruct
MESH = pl.DeviceIdType.MESH

D = 1024
N_HEADS = 16
N_KV = 2
GROUP = 8
HD = 64
BLK = 128
QKV = (N_HEADS + 2 * N_KV) * HD
KV_COL_BLOCK = (N_HEADS * HD) // (2 * N_KV * HD)
CONV_W = 31
CONV_HALO = 32
DFF = 2816
RMS_EPS = 1e-6
LN_EPS = 1e-5
LR, B1, B2, ADAM_EPS, WD, STEP = 0.001, 0.9, 0.999, 1e-08, 0.01, 10

N_CHIPS = 4
N_DEV = 8
VMEM_LIMIT = 56 * 1024 * 1024
LANES = 128
SUB = 8
ELEMENTWISE_BLOCK_BYTES = 1 << 20


def _cp(*sem):
    return pltpu.CompilerParams(dimension_semantics=sem, vmem_limit_bytes=VMEM_LIMIT)


def _row(tm, n):
    return pl.BlockSpec((tm, n), lambda i: (i, 0))


def _const(shape):
    return pl.BlockSpec(shape, lambda *_: (0,) * len(shape), pipeline_mode=pl.Buffered(1))


def _acc(shape):
    return pl.BlockSpec(shape, lambda *_: (0,) * len(shape))


def _rms_fwd(x, g):
    r = lax.rsqrt(jnp.mean(x * x, axis=-1, keepdims=True) + RMS_EPS)
    xn = x * r
    return xn * g, xn, r


def _colsum8(v):
    return jnp.sum(v.reshape(v.shape[0] // SUB, SUB, v.shape[1]), axis=0)


def _rms_bwd(xn, r, g, dh):
    dyn = dh * g
    dx = r * (dyn - xn * jnp.mean(dyn * xn, axis=-1, keepdims=True))
    return dx, _colsum8(dh * xn)


def _sigmoid(z):
    return 0.5 * jnp.tanh(0.5 * z) + 0.5


def _dsilu(z, sg):
    return sg * (1.0 + z * (1.0 - sg))


def _dot(a, b):
    return jnp.dot(a, b, preferred_element_type=f32)


_ANY = pl.BlockSpec(memory_space=pl.ANY)
_VMEM = pl.BlockSpec(memory_space=pltpu.VMEM)


class Rider:
    def __init__(self, ins, out_shape, sem_shapes, stages, final):
        self.ins, self.out_shape, self.sem_shapes, self.stages, self.final = ins, out_shape, sem_shapes, stages, final


def run_rider(rider, name):
    n_in, n_out = len(rider.ins), len(rider.out_shape)

    def body(*refs):
        parts = refs[:n_in], refs[n_in:n_in + n_out], refs[n_in + n_out:]
        for _, fn in rider.stages:
            fn(*parts)
        rider.final(*parts)

    return pl.pallas_call(
        body, name=name, in_specs=[_ANY] * n_in, out_specs=[_ANY] * n_out, out_shape=list(rider.out_shape),
        scratch_shapes=list(rider.sem_shapes),
    )(*rider.ins)


def _hosted(rider, body, *, grid, in_specs, out_specs, out_shape, name, compiler_params, scratch_shapes=()):
    if rider is None:
        return pl.pallas_call(body, grid=grid, in_specs=in_specs, out_specs=out_specs, out_shape=out_shape, name=name,
                              compiler_params=compiler_params, scratch_shapes=list(scratch_shapes))
    single = not isinstance(out_shape, (list, tuple))
    shapes = [out_shape] if single else list(out_shape)
    specs = [out_specs] if single else list(out_specs)
    n_in, n_out, n_sc = len(in_specs), len(shapes), len(scratch_shapes)
    r_in, r_out = len(rider.ins), len(rider.out_shape)
    total = math.prod(grid)

    def wrapped(*refs):
        own_in, refs = refs[:n_in], refs[n_in:]
        r_ins, refs = refs[:r_in], refs[r_in:]
        own_out, refs = refs[:n_out], refs[n_out:]
        r_outs, refs = refs[:r_out], refs[r_out:]
        own_sc, r_sems = refs[:n_sc], refs[n_sc:]
        step = 0
        for d, n in enumerate(grid):
            step = step * n + pl.program_id(d)
        for frac, fn in rider.stages:
            @pl.when(step == min(int(frac * total), total - 1))
            def _(fn=fn):
                fn(r_ins, r_outs, r_sems)

        body(*own_in, *own_out, *own_sc)

        @pl.when(step == total - 1)
        def _():
            rider.final(r_ins, r_outs, r_sems)

    call = pl.pallas_call(
        wrapped, grid=grid, in_specs=list(in_specs) + [_ANY] * r_in, out_specs=specs + [_ANY] * r_out,
        out_shape=shapes + list(rider.out_shape), scratch_shapes=list(scratch_shapes) + list(rider.sem_shapes),
        name=name, compiler_params=_cp(*(("arbitrary",) * len(grid))))

    def run(*args):
        res = call(*args, *rider.ins)
        own = res[:n_out]
        return (own[0] if single else own), res[n_out:]

    return run


def qkv_fwd(x, g, w, b, tm):
    s = x.shape[0]

    def body(x_ref, g_ref, w_ref, b_ref, h_ref, o_ref):
        h, _, _ = _rms_fwd(x_ref[...], g_ref[...])
        hb = h.astype(bf16)
        h_ref[...] = hb
        o_ref[...] = (_dot(hb, w_ref[...]) + b_ref[...]).astype(bf16)

    return pl.pallas_call(
        body, grid=(s // tm,), name="qkv_fwd",
        in_specs=[_row(tm, D), _const((1, D)), _const((D, QKV)), _const((1, QKV))],
        out_specs=[_row(tm, D), _row(tm, QKV)],
        out_shape=[SDS((s, D), bf16), SDS((s, QKV), bf16)],
        compiler_params=_cp("parallel"),
    )(x, g, w, b)


def _band_mask(i):
    qi = lax.broadcasted_iota(jnp.int32, (GROUP * BLK, 2 * BLK), 0) & (BLK - 1)
    ki = lax.broadcasted_iota(jnp.int32, (GROUP * BLK, 2 * BLK), 1)
    dist = qi + BLK - ki
    return (dist >= 0) & (dist < BLK) & ((ki >= BLK) | (i > 0))


_NEG = float(jnp.finfo(jnp.float32).min)
_NT = (((1,), (1,)), ((), ()))
_TN = (((0,), (0,)), ((), ()))


def _kv_heads(kvp_ref, kvc_ref, kvh):
    ks = slice(kvh * HD, (kvh + 1) * HD)
    vs = slice(N_KV * HD + kvh * HD, N_KV * HD + (kvh + 1) * HD)
    k = jnp.concatenate([kvp_ref[:, ks], kvc_ref[:, ks]], axis=0)
    v = jnp.concatenate([kvp_ref[:, vs], kvc_ref[:, vs]], axis=0)
    return k, v


def _stack_group(ref, kvh, width=HD):
    return jnp.concatenate([ref[:, (kvh * GROUP + gi) * width:(kvh * GROUP + gi + 1) * width] for gi in range(GROUP)],
                           axis=0)


def _group_sinks(sink_ref, kvh):
    row = lax.broadcasted_iota(jnp.int32, (GROUP * BLK, 1), 0)
    col = jnp.zeros((GROUP * BLK, 1), f32)
    for g in range(GROUP):
        col = jnp.where((row >= g * BLK) & (row < (g + 1) * BLK), sink_ref[kvh * GROUP + g], col)
    return col


def attn_fwd(qkv, sinks, rider=None):
    s = qkv.shape[0]
    scale = 1.0 / math.sqrt(HD)

    def body(q_ref, kvc_ref, kvp_ref, sink_ref, o_ref, lse_ref):
        valid = _band_mask(pl.program_id(0))
        for kvh in range(N_KV):
            k, v = _kv_heads(kvp_ref, kvc_ref, kvh)
            sc = lax.dot_general(_stack_group(q_ref, kvh), k, _NT, preferred_element_type=f32) * scale
            sc = jnp.where(valid, sc, _NEG)
            sink = _group_sinks(sink_ref, kvh)
            m = jnp.maximum(jnp.max(sc, axis=-1, keepdims=True), sink)
            p = jnp.exp(sc - m)
            denom = jnp.sum(p, axis=-1, keepdims=True) + jnp.exp(sink - m)
            og = _dot((p / denom).astype(bf16), v).astype(bf16)
            lse = m + jnp.log(denom)
            for gi in range(GROUP):
                h = kvh * GROUP + gi
                o_ref[:, h * HD:(h + 1) * HD] = og[gi * BLK:(gi + 1) * BLK]
                lse_ref[:, h:h + 1] = lse[gi * BLK:(gi + 1) * BLK]

    return _hosted(
        rider, body, grid=(s // BLK,), name="attn_fwd",
        in_specs=[
            pl.BlockSpec((BLK, N_HEADS * HD), lambda i: (i, 0)),
            pl.BlockSpec((BLK, 2 * N_KV * HD), lambda i: (i, KV_COL_BLOCK)),
            pl.BlockSpec((BLK, 2 * N_KV * HD), lambda i: (jnp.maximum(i - 1, 0), KV_COL_BLOCK)),
            pl.BlockSpec(memory_space=pltpu.SMEM),
        ],
        out_specs=[_row(BLK, D), _row(BLK, N_HEADS)],
        out_shape=[SDS((s, D), bf16), SDS((s, N_HEADS), f32)],
        compiler_params=_cp("parallel"),
    )(qkv, qkv, qkv, sinks)


def attn_out_fwd(x, o, w, b, tm):
    s = x.shape[0]

    def body(x_ref, o_ref, w_ref, b_ref, y_ref):
        y_ref[...] = x_ref[...] + _dot(o_ref[...], w_ref[...]) + b_ref[...]

    return pl.pallas_call(
        body, grid=(s // tm,), name="attn_out_fwd",
        in_specs=[_row(tm, D), _row(tm, D), _const((D, D)), _const((1, D))],
        out_specs=_row(tm, D), out_shape=SDS((s, D), f32),
        compiler_params=_cp("parallel"),
    )(x, o, w, b)


def attn_out_bwd(dy, wt, tm, rider=None):
    s = dy.shape[0]

    def body(dy_ref, wt_ref, do_ref, db_ref):
        @pl.when(pl.program_id(0) == 0)
        def _():
            db_ref[...] = jnp.zeros_like(db_ref)

        dy = dy_ref[...]
        do_ref[...] = _dot(dy.astype(bf16), wt_ref[...]).astype(bf16)
        db_ref[...] += _colsum8(dy)

    return _hosted(
        rider, body, grid=(s // tm,), name="attn_out_bwd",
        in_specs=[_row(tm, D), _const((D, D))],
        out_specs=[_row(tm, D), _acc((SUB, D))],
        out_shape=[SDS((s, D), bf16), SDS((SUB, D), f32)],
        compiler_params=_cp("arbitrary"),
    )(dy, wt)


def attn_bwd(qkv, o, do, lse, sinks, rider=None):
    s = qkv.shape[0]
    nb = s // BLK
    scale = 1.0 / math.sqrt(HD)
    kvw = 2 * N_KV * HD

    def body(q_ref, kvc_ref, kvp_ref, o_ref, do_ref, lse_ref, sink_ref, dq_ref, dkv_ref, ds_ref, carry_ref):
        i = pl.program_id(0)

        @pl.when(i == 0)
        def _():
            ds_ref[...] = jnp.zeros_like(ds_ref)
            carry_ref[...] = jnp.zeros_like(carry_ref)

        @pl.when(i < nb)
        def _():
            valid = _band_mask(i)
            for kvh in range(N_KV):
                k, v = _kv_heads(kvp_ref, kvc_ref, kvh)
                qg = _stack_group(q_ref, kvh)
                dog = _stack_group(do_ref, kvh)
                lse = _stack_group(lse_ref, kvh, 1)
                sc = lax.dot_general(qg, k, _NT, preferred_element_type=f32) * scale
                sc = jnp.where(valid, sc, _NEG)
                p = jnp.exp(sc - lse)
                dp = lax.dot_general(dog, v, _NT, preferred_element_type=f32)
                dlt = jnp.sum(dog.astype(f32) * _stack_group(o_ref, kvh).astype(f32), axis=-1, keepdims=True)
                dsc = (p * (dp - dlt)).astype(bf16)
                dqg = (_dot(dsc, k) * scale).astype(bf16)
                dk = lax.dot_general(dsc, qg, _TN, preferred_element_type=f32) * scale
                dv = lax.dot_general(p.astype(bf16), dog, _TN, preferred_element_type=f32)
                dsink = jnp.exp(_group_sinks(sink_ref, kvh) - lse) * dlt
                for gi in range(GROUP):
                    h = kvh * GROUP + gi
                    dq_ref[:, h * HD:(h + 1) * HD] = dqg[gi * BLK:(gi + 1) * BLK]
                    ds_ref[:, h:h + 1] += -jnp.sum(dsink[gi * BLK:(gi + 1) * BLK], axis=0, keepdims=True)
                ks = slice(kvh * HD, (kvh + 1) * HD)
                vs = slice(N_KV * HD + kvh * HD, N_KV * HD + (kvh + 1) * HD)
                dkv_ref[:, ks] = carry_ref[:, ks] + dk[:BLK]
                dkv_ref[:, vs] = carry_ref[:, vs] + dv[:BLK]
                carry_ref[:, ks] = dk[BLK:]
                carry_ref[:, vs] = dv[BLK:]

        @pl.when(i == nb)
        def _():
            dkv_ref[...] = carry_ref[...]

    cur = lambda i: (jnp.minimum(i, nb - 1), 0)
    prev = lambda i: (jnp.clip(i - 1, 0, nb - 1), KV_COL_BLOCK)
    return _hosted(
        rider, body, grid=(nb + 1,), name="attn_bwd",
        in_specs=[
            pl.BlockSpec((BLK, D), cur),
            pl.BlockSpec((BLK, kvw), lambda i: (jnp.minimum(i, nb - 1), KV_COL_BLOCK)),
            pl.BlockSpec((BLK, kvw), prev),
            pl.BlockSpec((BLK, D), cur),
            pl.BlockSpec((BLK, D), cur),
            pl.BlockSpec((BLK, N_HEADS), cur),
            pl.BlockSpec(memory_space=pltpu.SMEM),
        ],
        out_specs=[
            pl.BlockSpec((BLK, D), cur),
            pl.BlockSpec((BLK, kvw), lambda i: (jnp.maximum(i - 1, 0), 0)),
            _acc((1, N_HEADS)),
        ],
        out_shape=[SDS((s, D), bf16), SDS((s, kvw), f32), SDS((1, N_HEADS), f32)],
        scratch_shapes=[pltpu.VMEM((BLK, kvw), f32)],
        compiler_params=_cp("arbitrary"),
    )(qkv, qkv, qkv, o, do, lse, sinks)


def qkv_bwd(dq, dkv, x, g, wt, dres, tm):
    s = x.shape[0]
    qd = N_HEADS * HD
    kvw = 2 * N_KV * HD

    def body(dq_ref, dkv_ref, x_ref, g_ref, wt_ref, dres_ref, dx_ref, dg_ref, db_ref, dkvb_ref):
        @pl.when(pl.program_id(0) == 0)
        def _():
            dg_ref[...] = jnp.zeros_like(dg_ref)
            db_ref[...] = jnp.zeros_like(db_ref)

        dq = dq_ref[...]
        dkv = dkv_ref[...]
        dkvb = dkv.astype(bf16)
        dkvb_ref[...] = dkvb
        dh = _dot(dq, wt_ref[0:qd, :]) + _dot(dkvb, wt_ref[qd:QKV, :])
        g = g_ref[...]
        _, xn, r = _rms_fwd(x_ref[...], g)
        dx, dg = _rms_bwd(xn, r, g, dh)
        dx_ref[...] = dres_ref[...] + dx
        dg_ref[...] += dg
        db_ref[:, 0:qd] += _colsum8(dq.astype(f32))
        db_ref[:, qd:QKV] += _colsum8(dkv)

    return pl.pallas_call(
        body, grid=(s // tm,), name="qkv_bwd",
        in_specs=[_row(tm, qd), _row(tm, kvw), _row(tm, D), _const((1, D)), _const((QKV, D)), _row(tm, D)],
        out_specs=[_row(tm, D), _acc((SUB, D)), _acc((SUB, QKV)), _row(tm, kvw)],
        out_shape=[SDS((s, D), f32), SDS((SUB, D), f32), SDS((SUB, QKV), f32), SDS((s, kvw), bf16)],
        compiler_params=_cp("arbitrary"),
    )(dq, dkv, x, g, wt, dres)


def ffn_fwd(x, g, wup, wdw, bdw, wdn, tm, cw, rider=None, head=None):
    s = x.shape[0]
    tail = 8

    def body(x_ref, g_ref, wup_ref, wdw_ref, bdw_ref, wdn_ref, *refs):
        if head is None:
            xo_ref, h_ref, up_ref, gate_ref, act_ref, carry_ref, ext_ref = refs
        else:
            gf_ref, t_ref, xo_ref, h_ref, up_ref, gate_ref, act_ref, dgf_ref, loss_ref, carry_ref, ext_ref = refs

        @pl.when(pl.program_id(0) == 0)
        def _():
            carry_ref[...] = jnp.zeros_like(carry_ref)
            if head is not None:
                dgf_ref[...] = jnp.zeros_like(dgf_ref)
                loss_ref[...] = jnp.zeros_like(loss_ref)

        x = x_ref[...]
        h, _, _ = _rms_fwd(x, g_ref[...])
        hb = h.astype(bf16)
        h_ref[...] = hb
        for c in range(DFF // cw):
            cs = slice(c * cw, (c + 1) * cw)
            vs = slice(DFF + c * cw, DFF + (c + 1) * cw)
            ug = _dot(hb, wup_ref[:, cs])
            uv = _dot(hb, wup_ref[:, vs])
            up_ref[:, cs] = ug.astype(bf16)
            up_ref[:, vs] = uv.astype(bf16)
            ext_ref[0:tail, :] = carry_ref[:, cs]
            ext_ref[tail:tail + tm, :] = ug
            carry_ref[:, cs] = ug[tm - tail:, :]
            ext = ext_ref[...]
            gate = (wdw_ref[0:1, cs] * pltpu.roll(ext, 2, 0)[tail:] + wdw_ref[1:2, cs] * pltpu.roll(ext, 1, 0)[tail:]
                    + wdw_ref[2:3, cs] * ug) + bdw_ref[:, cs]
            gate_ref[:, cs] = gate.astype(bf16)
            act_ref[:, cs] = (gate * _sigmoid(gate) * uv).astype(bf16)
        xo = x + _dot(act_ref[...], wdn_ref[...])
        if head is None:
            xo_ref[...] = xo
        else:
            gf = gf_ref[...]
            y, xn, r = _rms_fwd(xo, gf)
            e = y - t_ref[...]
            loss_ref[...] += 0.5 * jnp.sum(jnp.mean(e * e, axis=-1, keepdims=True), axis=0, keepdims=True)
            dx, dgf = _rms_bwd(xn, r, gf, e / D)
            xo_ref[...] = dx
            dgf_ref[...] += dgf

    extra_in = [] if head is None else [_const((1, D)), _row(tm, D)]
    extra_out = [] if head is None else [_acc((SUB, D)), _acc((1, 1))]
    extra_shape = [] if head is None else [SDS((SUB, D), f32), SDS((1, 1), f32)]
    return _hosted(
        rider, body, grid=(s // tm,), name="ffn_fwd" if head is None else "ffn_fwd_loss",
        in_specs=[_row(tm, D), _const((1, D)), _const((D, 2 * DFF)), _const((3, DFF)), _const((1, DFF)),
                  _const((DFF, D))] + extra_in,
        out_specs=[_row(tm, D), _row(tm, D), _row(tm, 2 * DFF), _row(tm, DFF), _row(tm, DFF)] + extra_out,
        out_shape=[SDS((s, D), f32), SDS((s, D), bf16), SDS((s, 2 * DFF), bf16), SDS((s, DFF), bf16),
                   SDS((s, DFF), bf16)] + extra_shape,
        scratch_shapes=[pltpu.VMEM((tail, DFF), f32), pltpu.VMEM((tail + tm, cw), f32)],
        compiler_params=_cp("arbitrary"),
    )(x, g, wup, wdw, bdw, wdn, *(head or ()))


def ffn_bwd(dxo, x, g, up, gate, wdw, wdnt, wupt, tm, cw, rider=None):
    s = x.shape[0]
    nt = s // tm
    rev = lambda i: (nt - 1 - i, 0)

    def body(dxo_ref, x_ref, g_ref, up_ref, gate_ref, wdw_ref, wdnt_ref, wupt_ref,
             dxi_ref, dup_ref, dg_ref, dwdw_ref, dbdw_ref, carry_ref, ext2_ref):
        i = pl.program_id(0)

        @pl.when(i == 0)
        def _():
            carry_ref[...] = jnp.zeros_like(carry_ref)
            dg_ref[...] = jnp.zeros_like(dg_ref)
            dwdw_ref[...] = jnp.zeros_like(dwdw_ref)
            dbdw_ref[...] = jnp.zeros_like(dbdw_ref)

        dxo = dxo_ref[...]
        dxb = dxo.astype(bf16)
        for c in range(DFF // cw):
            cs = slice(c * cw, (c + 1) * cw)
            vs = slice(DFF + c * cw, DFF + (c + 1) * cw)
            d_act = _dot(dxb, wdnt_ref[:, cs])
            ug = up_ref[:, cs].astype(f32)
            uv = up_ref[:, vs].astype(f32)
            gate = gate_ref[:, cs].astype(f32)
            sg = _sigmoid(gate)
            dup_ref[:, vs] = (d_act * (gate * sg)).astype(bf16)
            d_gate = d_act * uv * _dsilu(gate, sg)
            ext2_ref[0:tm, :] = d_gate
            ext2_ref[tm:tm + 8, :] = carry_ref[:, cs]
            carry_ref[:, cs] = d_gate[0:8, :]
            ext = ext2_ref[...]
            ahead1 = pltpu.roll(ext, tm + 8 - 1, 0)[:tm]
            ahead2 = pltpu.roll(ext, tm + 8 - 2, 0)[:tm]
            dbdw_ref[:, cs] += _colsum8(d_gate)
            dwdw_ref[0, :, cs] += _colsum8(ahead2 * ug)
            dwdw_ref[1, :, cs] += _colsum8(ahead1 * ug)
            dwdw_ref[2, :, cs] += _colsum8(d_gate * ug)
            d_ug = wdw_ref[0:1, cs] * ahead2 + wdw_ref[1:2, cs] * ahead1 + wdw_ref[2:3, cs] * d_gate
            dup_ref[:, cs] = d_ug.astype(bf16)
        dh = _dot(dup_ref[...], wupt_ref[...])
        gv = g_ref[...]
        _, xn, r = _rms_fwd(x_ref[...], gv)
        dx, dg = _rms_bwd(xn, r, gv, dh)
        dxi_ref[...] = dxo + dx
        dg_ref[...] += dg

    return _hosted(
        rider, body, grid=(nt,), name="ffn_bwd",
        in_specs=[
            pl.BlockSpec((tm, D), rev), pl.BlockSpec((tm, D), rev), _const((1, D)),
            pl.BlockSpec((tm, 2 * DFF), rev), pl.BlockSpec((tm, DFF), rev),
            _const((3, DFF)), _const((D, DFF)), _const((2 * DFF, D)),
        ],
        out_specs=[pl.BlockSpec((tm, D), rev), pl.BlockSpec((tm, 2 * DFF), rev), _acc((SUB, D)),
                   _acc((3, SUB, DFF)), _acc((SUB, DFF))],
        out_shape=[SDS((s, D), f32), SDS((s, 2 * DFF), bf16), SDS((SUB, D), f32), SDS((3, SUB, DFF), f32),
                   SDS((SUB, DFF), f32)],
        scratch_shapes=[pltpu.VMEM((8, DFF), f32), pltpu.VMEM((tm + 8, cw), f32)],
        compiler_params=_cp("arbitrary"),
    )(dxo, x, g, up, gate, wdw, wdnt, wupt)


def pw1_fwd(x, g, w, b, tm):
    s = x.shape[0]

    def body(x_ref, g_ref, w_ref, b_ref, h_ref, a_ref, u_ref):
        h, _, _ = _rms_fwd(x_ref[...], g_ref[...])
        hb = h.astype(bf16)
        h_ref[...] = hb
        a = _dot(hb, w_ref[...]) + b_ref[...]
        a_ref[...] = a.astype(bf16)
        u_ref[...] = a[:, :D] * _sigmoid(a[:, D:])

    return pl.pallas_call(
        body, grid=(s // tm,), name="pw1_fwd",
        in_specs=[_row(tm, D), _const((1, D)), _const((D, 2 * D)), _const((1, 2 * D))],
        out_specs=[_row(tm, D), _row(tm, 2 * D), _row(tm, D)],
        out_shape=[SDS((s, D), bf16), SDS((s, 2 * D), bf16), SDS((s, D), f32)],
        compiler_params=_cp("parallel"),
    )(x, g, w, b)


def _ln_silu(c, lg, lb):
    mu = jnp.mean(c, axis=-1, keepdims=True)
    cc = c - mu
    var = jnp.mean(cc * cc, axis=-1, keepdims=True)
    rstd = lax.rsqrt(var + LN_EPS)
    xh = cc * rstd
    ln = xh * lg + lb
    sg = _sigmoid(ln)
    return xh, rstd, ln, sg


def _shifted_copies(ext_ref, sh_ref, cs, tm):
    n = CONV_HALO - SUB + tm
    for k in range(1, SUB):
        sh_ref[k - 1] = ext_ref[pl.ds(k, n), cs]


def _shifted_rows(ext_ref, sh_ref, cs, start, rows):
    q, k = divmod(start, SUB)
    if k == 0:
        return ext_ref[pl.ds(start, rows), cs]
    return sh_ref[k - 1, pl.ds(q * SUB, rows), :]


def conv_fwd(u, x, wdw, bdw, lg, lb, w2, b2, tm, rc):
    s = x.shape[0]
    hl = CONV_HALO
    off = hl - (CONV_W - 1)

    def body(u_ref, halo_ref, x_ref, wdw_ref, bdw_ref, lg_ref, lb_ref, w2_ref, b2_ref, c_ref, xo_ref, ext_ref, sh_ref):
        has_prev = (pl.program_id(0) > 0).astype(f32)
        ext_ref[0:hl, :] = halo_ref[...] * has_prev
        ext_ref[hl:hl + tm, :] = u_ref[...]
        for cc in range(D // LANES):
            cs = slice(cc * LANES, (cc + 1) * LANES)
            _shifted_copies(ext_ref, sh_ref, cs, tm)
            for rr in range(tm // rc):
                acc = jnp.zeros((rc, LANES), f32) + bdw_ref[:, cs]
                for j in range(CONV_W):
                    acc = acc + wdw_ref[j:j + 1, cs] * _shifted_rows(ext_ref, sh_ref, cs, rr * rc + off + j, rc)
                c_ref[rr * rc:(rr + 1) * rc, cs] = acc
        _, _, ln, sg = _ln_silu(c_ref[...], lg_ref[...], lb_ref[...])
        xo_ref[...] = x_ref[...] + _dot((ln * sg).astype(bf16), w2_ref[...]) + b2_ref[...]

    return pl.pallas_call(
        body, grid=(s // tm,), name="conv_fwd",
        in_specs=[_row(tm, D), pl.BlockSpec((hl, D), lambda i: (jnp.maximum(i * (tm // hl) - 1, 0), 0)), _row(tm, D),
                  _const((CONV_W, D)), _const((1, D)), _const((1, D)), _const((1, D)), _const((D, D)), _const((1, D))],
        out_specs=[_row(tm, D), _row(tm, D)],
        out_shape=[SDS((s, D), f32), SDS((s, D), f32)],
        scratch_shapes=[pltpu.VMEM((hl + tm, D), f32), pltpu.VMEM((SUB - 1, hl - SUB + tm, LANES), f32)],
        compiler_params=_cp("parallel"),
    )(u, u, x, wdw, bdw, lg, lb, w2, b2)


def conv_bwd_a(dy, c, lg, lb, w2t, tm, rider=None):
    s = dy.shape[0]

    def body(dy_ref, c_ref, lg_ref, lb_ref, w2t_ref, dc_ref, u3_ref, dlg_ref, dlb_ref, db2_ref, dbdw_ref):
        @pl.when(pl.program_id(0) == 0)
        def _():
            for r in (dlg_ref, dlb_ref, db2_ref, dbdw_ref):
                r[...] = jnp.zeros_like(r)

        dy = dy_ref[...]
        lg = lg_ref[...]
        xh, rstd, ln, sg = _ln_silu(c_ref[...], lg, lb_ref[...])
        u3_ref[...] = (ln * sg).astype(bf16)
        du3 = _dot(dy.astype(bf16), w2t_ref[...])
        dln = du3 * _dsilu(ln, sg)
        dxh = dln * lg
        dc = rstd * (dxh - jnp.mean(dxh, axis=-1, keepdims=True) - xh * jnp.mean(dxh * xh, axis=-1, keepdims=True))
        dc_ref[...] = dc
        dlg_ref[...] += _colsum8(dln * xh)
        dlb_ref[...] += _colsum8(dln)
        db2_ref[...] += _colsum8(dy)
        dbdw_ref[...] += _colsum8(dc)

    return _hosted(
        rider, body, grid=(s // tm,), name="conv_bwd_a",
        in_specs=[_row(tm, D), _row(tm, D), _const((1, D)), _const((1, D)), _const((D, D))],
        out_specs=[_row(tm, D), _row(tm, D)] + [_acc((SUB, D))] * 4,
        out_shape=[SDS((s, D), f32), SDS((s, D), bf16)] + [SDS((SUB, D), f32)] * 4,
        compiler_params=_cp("arbitrary"),
    )(dy, c, lg, lb, w2t)


def conv_bwd_b(dc, u, a, wdw, x, g, w1t, dres, tm, rc, rider=None):
    s = x.shape[0]
    nt = s // tm
    hl = CONV_HALO
    off = hl - (CONV_W - 1)

    def body(dc_ref, dnext_ref, u_ref, uprev_ref, a_ref, wdw_ref, x_ref, g_ref, w1t_ref, dres_ref,
             dx_ref, da_ref, dwdw_ref, db1_ref, dg_ref, ext_ref, ext2_ref, du_ref, sh_ref, sh2_ref):
        i = pl.program_id(0)

        @pl.when(i == 0)
        def _():
            for r in (dwdw_ref, db1_ref, dg_ref):
                r[...] = jnp.zeros_like(r)

        ext_ref[0:hl, :] = uprev_ref[...] * (i > 0).astype(f32)
        ext_ref[hl:hl + tm, :] = u_ref[...]
        ext2_ref[0:tm, :] = dc_ref[...]
        ext2_ref[tm:tm + hl, :] = dnext_ref[...] * (i < nt - 1).astype(f32)
        for cc in range(D // LANES):
            cs = slice(cc * LANES, (cc + 1) * LANES)
            _shifted_copies(ext_ref, sh_ref, cs, tm)
            _shifted_copies(ext2_ref, sh2_ref, cs, tm)
            for rr in range(tm // rc):
                r0 = rr * rc
                dcb = ext2_ref[r0:r0 + rc, cs]
                acc = jnp.zeros((rc, LANES), f32)
                for j in range(CONV_W):
                    acc = acc + wdw_ref[j:j + 1, cs] * _shifted_rows(ext2_ref, sh2_ref, cs, r0 + CONV_W - 1 - j, rc)
                    dwdw_ref[j, :, cs] += _colsum8(dcb * _shifted_rows(ext_ref, sh_ref, cs, r0 + off + j, rc))
                du_ref[r0:r0 + rc, cs] = acc
        du = du_ref[...]
        a1 = a_ref[:, :D].astype(f32)
        sg = _sigmoid(a_ref[:, D:].astype(f32))
        da1 = du * sg
        da2 = du * a1 * sg * (1.0 - sg)
        da_ref[:, :D] = da1.astype(bf16)
        da_ref[:, D:] = da2.astype(bf16)
        db1_ref[:, :D] += _colsum8(da1)
        db1_ref[:, D:] += _colsum8(da2)
        dh = _dot(da_ref[...], w1t_ref[...])
        gv = g_ref[...]
        _, xn, r = _rms_fwd(x_ref[...], gv)
        dx, dg = _rms_bwd(xn, r, gv, dh)
        dx_ref[...] = dres_ref[...] + dx
        dg_ref[...] += dg

    blocks = tm // hl
    return _hosted(
        rider, body, grid=(nt,), name="conv_bwd_b",
        in_specs=[
            _row(tm, D), pl.BlockSpec((hl, D), lambda i: (jnp.minimum((i + 1) * blocks, s // hl - 1), 0)),
            _row(tm, D), pl.BlockSpec((hl, D), lambda i: (jnp.maximum(i * blocks - 1, 0), 0)),
            _row(tm, 2 * D), _const((CONV_W, D)), _row(tm, D), _const((1, D)), _const((2 * D, D)), _row(tm, D),
        ],
        out_specs=[_row(tm, D), _row(tm, 2 * D), _acc((CONV_W, SUB, D)), _acc((SUB, 2 * D)), _acc((SUB, D))],
        out_shape=[SDS((s, D), f32), SDS((s, 2 * D), bf16), SDS((CONV_W, SUB, D), f32), SDS((SUB, 2 * D), f32),
                   SDS((SUB, D), f32)],
        scratch_shapes=[pltpu.VMEM((hl + tm, D), f32), pltpu.VMEM((tm + hl, D), f32), pltpu.VMEM((tm, D), f32),
                        pltpu.VMEM((SUB - 1, hl - SUB + tm, LANES), f32),
                        pltpu.VMEM((SUB - 1, hl - SUB + tm, LANES), f32)],
        compiler_params=_cp("arbitrary"),
    )(dc, dc, u, u, a, wdw, x, g, w1t, dres)


def wgrad(a, b, nb, tk, name, rider=None):
    s, k1 = a.shape
    n = b.shape[1]

    def body(a_ref, b_ref, o_ref):
        @pl.when(pl.program_id(1) == 0)
        def _():
            o_ref[...] = jnp.zeros_like(o_ref)

        o_ref[...] += lax.dot_general(a_ref[...], b_ref[...].astype(bf16), _TN, preferred_element_type=f32)

    return _hosted(
        rider, body, grid=(n // nb, s // tk), name=name,
        in_specs=[pl.BlockSpec((tk, k1), lambda j, k: (k, 0)), pl.BlockSpec((tk, nb), lambda j, k: (k, j))],
        out_specs=pl.BlockSpec((k1, nb), lambda j, k: (0, j)),
        out_shape=SDS((k1, n), f32),
        compiler_params=_cp("parallel", "arbitrary"),
    )(a, b)


def wgrad_cols(a, b, tk, name, rider=None):
    s, k1 = a.shape
    w = b.shape[1] // N_CHIPS

    def body(a_ref, b_ref, o_ref):
        @pl.when(pl.program_id(1) == 0)
        def _():
            o_ref[...] = jnp.zeros_like(o_ref)

        acc = lax.dot_general(a_ref[...], b_ref[...].astype(bf16), _TN, preferred_element_type=f32)
        o_ref[:, 0] += acc.reshape(2, k1 // 2, w)

    return _hosted(
        rider, body, grid=(N_CHIPS, s // tk), name=name,
        in_specs=[pl.BlockSpec((tk, k1), lambda j, k: (k, 0)), pl.BlockSpec((tk, w), lambda j, k: (k, j))],
        out_specs=pl.BlockSpec((2, 1, k1 // 2, w), lambda j, k: (0, j, 0, 0)),
        out_shape=SDS((2, N_CHIPS, k1 // 2, w), f32),
        compiler_params=_cp("parallel", "arbitrary"),
    )(a, b)


def wgrad_rows(a, b, nb, tk, name, rider=None):
    s, k1 = a.shape
    n = b.shape[1]
    r = k1 // (2 * N_CHIPS)

    def body(a_ref, b_ref, o_ref):
        @pl.when(pl.program_id(1) == 0)
        def _():
            o_ref[...] = jnp.zeros_like(o_ref)

        acc = lax.dot_general(a_ref[...], b_ref[...].astype(bf16), _TN, preferred_element_type=f32)
        for j in range(N_CHIPS):
            for h in range(2):
                o_ref[h, j] += acc[(2 * j + h) * r:(2 * j + h + 1) * r, :]

    return _hosted(
        rider, body, grid=(n // nb, s // tk), name=name,
        in_specs=[pl.BlockSpec((tk, k1), lambda j, k: (k, 0)), pl.BlockSpec((tk, nb), lambda j, k: (k, j))],
        out_specs=pl.BlockSpec((2, N_CHIPS, r, nb), lambda j, k: (0, 0, 0, j)),
        out_shape=SDS((2, N_CHIPS, r, n), f32),
        compiler_params=_cp("parallel", "arbitrary"),
    )(a, b)


def _adam_math(w, g, m, v):
    m = B1 * m + (1.0 - B1) * g
    v = B2 * v + (1.0 - B2) * (g * g)
    m_hat = m / (1.0 - B1 ** STEP)
    v_hat = v / (1.0 - B2 ** STEP)
    delta = -LR * (m_hat / (jnp.sqrt(v_hat) + ADAM_EPS) + WD * w)
    return delta, m, v


def _rows_tile(r, c, multiple=SUB):
    best = None
    for t in range(multiple, r + 1, multiple):
        if r % t == 0 and t * c * 4 <= ELEMENTWISE_BLOCK_BYTES:
            best = t
    return best if best is not None else r


def adamw(w, g, m, v, name):
    l, r, c = w.shape
    tr = _rows_tile(r, c)
    spec = pl.BlockSpec((1, tr, c), lambda i, j: (i, j, 0))

    def body(w_ref, g_ref, m_ref, v_ref, d_ref, mo_ref, vo_ref):
        d, mn, vn = _adam_math(w_ref[...], g_ref[...], m_ref[...], v_ref[...])
        d_ref[...] = d
        mo_ref[...] = mn
        vo_ref[...] = vn

    return pl.pallas_call(
        body, grid=(l, r // tr), name=name, in_specs=[spec] * 4, out_specs=[spec] * 3,
        out_shape=[SDS((l, r, c), f32)] * 3, compiler_params=_cp("parallel", "parallel"),
    )(w, g, m, v)


def _place():
    return lax.axis_index("x"), lax.axis_index("y"), lax.axis_index("c")


def _chip_peer(xi, yi, r):
    px = 1 - xi if r & 2 else xi
    py = 1 - yi if r & 1 else yi
    return px, py


def _gv_qkv(src, dst, j, h):
    rows = pl.ds(h * (D // 2), D // 2)
    return src.at[rows, :], dst.at[j, rows, :]


def _gv_rows(src, dst, j, h):
    r = src.shape[0] // 2
    return src.at[pl.ds(h * r, r), :], dst.at[pl.ds(j * 2 * r + h * r, r), :]


def _gv_cols(src, dst, j, h):
    r, w = src.shape[0] // 2, src.shape[1]
    return src.at[pl.ds(h * r, r), :], dst.at[pl.ds(h * r, r), pl.ds(j * w, w)]


def gather_rider(big, small, forward_at):
    nb, ns = len(big), len(small)
    n = nb + ns
    views = [v for _, v, _ in big]

    def env(ins, outs, sems):
        ici_send, ici_recv, d2d_send, d2d_recv, loc_sems = sems
        xi, yi, ci = _place()
        me = 2 * xi + yi

        def local(a, h):
            if a < nb:
                src, dst = views[a](ins[a], outs[a], me, h)
                return pltpu.make_async_copy(src, dst, loc_sems.at[2 * a + h])
            return pltpu.make_async_copy(ins[a], outs[a].at[me], loc_sems.at[nb + a])

        def ici(a, r, slot):
            px, py = _chip_peer(xi, yi, r)
            src, dst = views[a](ins[a], outs[a], slot, ci) if a < nb else (ins[a], outs[a].at[slot])
            k = 3 * a + r - 1
            return pltpu.make_async_remote_copy(src_ref=src, dst_ref=dst, send_sem=ici_send.at[k],
                                                recv_sem=ici_recv.at[k], device_id=(px, py, ci), device_id_type=MESH)

        def d2d(a, r, half):
            px, py = _chip_peer(xi, yi, r)
            _, dst = views[a](ins[a], outs[a], 2 * px + py, half)
            k = 3 * a + r - 1
            return pltpu.make_async_remote_copy(src_ref=dst, dst_ref=dst, send_sem=d2d_send.at[k],
                                                recv_sem=d2d_recv.at[k], device_id=(xi, yi, 1 - ci), device_id_type=MESH)

        return xi, yi, ci, me, local, ici, d2d

    def locals_of():
        return [(a, h) for a in range(nb) for h in range(2)] + [(a, 0) for a in range(nb, n)]

    def send(ins, outs, sems):
        _, _, _, me, local, ici, _ = env(ins, outs, sems)
        for a, h in locals_of():
            local(a, h).start()
        for a in range(n):
            for r in (1, 2, 3):
                ici(a, r, me).start()

    def forward(ins, outs, sems):
        xi, yi, ci, _, _, ici, d2d = env(ins, outs, sems)
        for a in range(n):
            for r in (1, 2, 3):
                px, py = _chip_peer(xi, yi, r)
                ici(a, r, 2 * px + py).wait_recv()
                if a < nb:
                    d2d(a, r, ci).start()

    def finish(ins, outs, sems):
        _, _, ci, me, local, ici, d2d = env(ins, outs, sems)
        for a in range(nb):
            for r in (1, 2, 3):
                d2d(a, r, 1 - ci).wait_recv()
        for a in range(n):
            for r in (1, 2, 3):
                ici(a, r, me).wait_send()
                if a < nb:
                    d2d(a, r, ci).wait_send()
        for a, h in locals_of():
            local(a, h).wait()

    dma = pltpu.SemaphoreType.DMA
    return Rider(
        ins=[b for b, _, _ in big] + list(small),
        out_shape=[SDS(shape, bf16) for _, _, shape in big] + [SDS((N_CHIPS,) + a.shape, a.dtype) for a in small],
        sem_shapes=[dma((3 * n,)), dma((3 * n,)), dma((max(3 * nb, 1),)), dma((max(3 * nb, 1),)), dma((2 * nb + ns,))],
        stages=[(0.0, send), (forward_at, forward)], final=finish)


def pair_send_rider(gs):
    n = len(gs)

    def copy(ins, outs, sems, a):
        xi, yi, ci = _place()
        return pltpu.make_async_remote_copy(
            src_ref=ins[a].at[1 - ci], dst_ref=outs[a], send_sem=sems[0].at[a], recv_sem=sems[1].at[a],
            device_id=(xi, yi, 1 - ci), device_id_type=MESH)

    def send(ins, outs, sems):
        for a in range(n):
            copy(ins, outs, sems, a).start()

    def finish(ins, outs, sems):
        for a in range(n):
            copy(ins, outs, sems, a).wait()

    dma = pltpu.SemaphoreType.DMA
    return Rider(ins=list(gs), out_shape=[SDS(g.shape[1:], g.dtype) for g in gs], sem_shapes=[dma((n,)), dma((n,))],
                 stages=[(0.0, send)], final=finish)


def chip_rider(pbs, p32s, forward_at):
    n = len(pbs)

    def env(ins, outs, sems):
        ici_send, ici_recv, d2d_send, d2d_recv, own_send, own_recv, loc_sems = sems
        pb, p32, recv, own = ins[:n], ins[n:], outs[:n], outs[n:]
        xi, yi, ci = _place()
        me = 2 * xi + yi
        sib = (xi, yi, 1 - ci)

        def ici(a, r, src_slot, dst_slot):
            px, py = _chip_peer(xi, yi, r)
            k = 3 * a + r - 1
            return pltpu.make_async_remote_copy(
                src_ref=pb[a].at[src_slot], dst_ref=recv[a].at[ci, dst_slot], send_sem=ici_send.at[k],
                recv_sem=ici_recv.at[k], device_id=(px, py, ci), device_id_type=MESH)

        def d2d(a, r, half):
            px, py = _chip_peer(xi, yi, r)
            blk = recv[a].at[half, 2 * px + py]
            k = 3 * a + r - 1
            return pltpu.make_async_remote_copy(src_ref=blk, dst_ref=blk, send_sem=d2d_send.at[k],
                                                recv_sem=d2d_recv.at[k], device_id=sib, device_id_type=MESH)

        def mine(a, half):
            return pltpu.make_async_remote_copy(src_ref=p32[a].at[me], dst_ref=own[a].at[half], send_sem=own_send.at[a],
                                                recv_sem=own_recv.at[a], device_id=sib, device_id_type=MESH)

        def local(a):
            return pltpu.make_async_copy(p32[a].at[me], own[a].at[ci], loc_sems.at[a])

        return xi, yi, ci, me, ici, d2d, mine, local

    def send(ins, outs, sems):
        xi, yi, ci, me, ici, _, mine, local = env(ins, outs, sems)
        for a in range(n):
            local(a).start()
            mine(a, ci).start()
            for r in (1, 2, 3):
                px, py = _chip_peer(xi, yi, r)
                ici(a, r, 2 * px + py, me).start()

    def forward(ins, outs, sems):
        xi, yi, ci, me, ici, d2d, _, _ = env(ins, outs, sems)
        for a in range(n):
            for r in (1, 2, 3):
                px, py = _chip_peer(xi, yi, r)
                ici(a, r, me, 2 * px + py).wait_recv()
                d2d(a, r, ci).start()

    def finish(ins, outs, sems):
        xi, yi, ci, me, ici, d2d, mine, local = env(ins, outs, sems)
        for a in range(n):
            mine(a, 1 - ci).wait_recv()
            for r in (1, 2, 3):
                d2d(a, r, 1 - ci).wait_recv()
        for a in range(n):
            mine(a, ci).wait_send()
            local(a).wait()
            for r in (1, 2, 3):
                px, py = _chip_peer(xi, yi, r)
                ici(a, r, 2 * px + py, me).wait_send()
                d2d(a, r, ci).wait_send()

    dma = pltpu.SemaphoreType.DMA
    return Rider(
        ins=list(pbs) + list(p32s),
        out_shape=[SDS((2,) + p.shape, bf16) for p in pbs] + [SDS((2,) + p.shape[1:], f32) for p in p32s],
        sem_shapes=[dma((3 * n,)), dma((3 * n,)), dma((3 * n,)), dma((3 * n,)), dma((n,)), dma((n,)), dma((n,))],
        stages=[(0.0, send), (forward_at, forward)], final=finish)


def rs_pair_add(g, other, ci, name):
    _, nsh, r, w = g.shape
    tr = _rows_tile(r, w, 16)

    def body(c_ref, g_ref, o_ref, p_ref, pb_ref):
        p = g_ref[0] + o_ref[...]
        p_ref[...] = p
        pb_ref[...] = p.astype(bf16)

    blk = pl.BlockSpec((1, tr, w), lambda j, i, c: (j, i, 0))
    return pl.pallas_call(
        body, name=name,
        grid_spec=pltpu.PrefetchScalarGridSpec(
            num_scalar_prefetch=1, grid=(nsh, r // tr),
            in_specs=[pl.BlockSpec((1, 1, tr, w), lambda j, i, c: (c[0], j, i, 0)), blk], out_specs=[blk, blk]),
        out_shape=[SDS((nsh, r, w), f32), SDS((nsh, r, w), bf16)],
        compiler_params=_cp("parallel", "parallel"),
    )(ci.reshape(1).astype(jnp.int32), g, other)


def rs_chip_add(own, recv, chip, name):
    _, nsh, r, w = recv.shape
    tr = _rows_tile(r, w, 16)

    def body(c_ref, own_ref, recv_ref, o_ref):
        me = c_ref[0]
        acc = None
        for j in range(N_CHIPS):
            term = jnp.where(me == j, own_ref[0], recv_ref[0, j].astype(f32))
            acc = term if acc is None else acc + term
        o_ref[0] = acc

    return pl.pallas_call(
        body, name=name,
        grid_spec=pltpu.PrefetchScalarGridSpec(
            num_scalar_prefetch=1, grid=(2, r // tr),
            in_specs=[pl.BlockSpec((1, tr, w), lambda h, i, c: (h, i, 0)),
                      pl.BlockSpec((1, nsh, tr, w), lambda h, i, c: (h, 0, i, 0))],
            out_specs=pl.BlockSpec((1, tr, w), lambda h, i, c: (h, i, 0))),
        out_shape=SDS((2, r, w), f32),
        compiler_params=_cp("parallel", "parallel"),
    )(chip.reshape(1).astype(jnp.int32), own, recv)


class GradReduction:
    def __init__(self, keys, grads, ci, chip):
        self.keys, self.grads, self.ci, self.chip = keys, grads, ci, chip

    def pair_rider(self):
        return pair_send_rider(self.grads)

    def chip_rider(self, from_pair, forward_at):
        both = [rs_pair_add(g, o, self.ci, "rs_pair_add_" + k) for k, g, o in zip(self.keys, self.grads, from_pair)]
        return chip_rider([pb for _, pb in both], [p for p, _ in both], forward_at)

    def result(self, landed):
        n = len(self.keys)
        return {k: rs_chip_add(own, recv, self.chip, "rs_chip_add_" + k)
                for k, recv, own in zip(self.keys, landed[:n], landed[n:])}

    def alone(self):
        from_pair = run_rider(self.pair_rider(), "rs_pair_" + self.keys[0])
        return self.result(run_rider(self.chip_rider(from_pair, 0.0), "rs_chip_" + self.keys[0]))


STAGE_W = DFF
_ST = {"norm_mix0": 0, "norm_mix1": 1, "attn_b_qkv": 2, "attn_sinks": 3, "attn_b_o": 4, "final_norm": 5, "loss": 6,
       "norm_ffn0": 8, "norm_ffn1": 9, "ffn_b_dw0": 10, "ffn_b_dw1": 11, "conv_b_pw1": 12, "conv_b_dw": 13,
       "conv_ln_g": 14, "conv_ln_b": 15, "conv_b_pw2": 16, "ffn_w_dw0": 17, "ffn_w_dw1": 20, "conv_w_dw": 24}
STAGE_ROWS = 56
SMALL_REP = ("norm_mix", "attn_b_qkv", "attn_sinks", "attn_b_o", "norm_ffn", "ffn_b_dw", "final_norm")
SMALL_SH = ("conv_b_pw1", "conv_w_dw", "conv_b_dw", "conv_ln_g", "conv_ln_b", "conv_b_pw2", "ffn_w_dw")
_SMALL_PARTS = ("norm_mix0", "norm_mix1", "attn_b_qkv", "attn_sinks", "attn_b_o", "norm_ffn0", "norm_ffn1", "ffn_b_dw0",
                "ffn_b_dw1", "final_norm", "conv_b_pw1", "conv_b_dw", "conv_ln_g", "conv_ln_b", "conv_b_pw2", "loss",
                "ffn_w_dw0", "ffn_w_dw1", "conv_w_dw")


def small_reduce_adamw(parts, w, m, v):
    names = SMALL_REP + SMALL_SH
    npart, nw = len(_SMALL_PARTS), len(names)

    def body(*refs):
        part = dict(zip(_SMALL_PARTS, refs[:npart]))
        off = npart
        w_ref = dict(zip(names, refs[off:off + nw]))
        m_ref = dict(zip(names, refs[off + nw:off + 2 * nw]))
        v_ref = dict(zip(names, refs[off + 2 * nw:off + 3 * nw]))
        off += 3 * nw
        loss_ref = refs[off]
        g_out = dict(zip(names, refs[off + 1:off + 1 + nw]))
        d_out = dict(zip(names, refs[off + 1 + nw:off + 1 + 2 * nw]))
        m_out = dict(zip(names, refs[off + 1 + 2 * nw:off + 1 + 3 * nw]))
        v_out = dict(zip(names, refs[off + 1 + 3 * nw:off + 1 + 4 * nw]))
        stage_ref, buf_ref, tot_ref, send_sems, recv_sems = refs[off + 1 + 4 * nw:]

        xi, yi, ci = _place()
        me = 4 * xi + 2 * yi + ci
        chip = 2 * xi + yi

        stage_ref[...] = jnp.zeros_like(stage_ref)
        for name in _SMALL_PARTS:
            ref, r0 = part[name], _ST[name]
            if name in ("attn_sinks", "loss"):
                val = ref[...]
            elif name in ("ffn_w_dw0", "ffn_w_dw1", "conv_w_dw"):
                val = jnp.sum(ref[...], axis=1)
            else:
                val = jnp.sum(ref[...], axis=0, keepdims=True)
            stage_ref[r0:r0 + val.shape[0], 0:val.shape[1]] = val

        buf_ref[me] = stage_ref[...]

        def peer(r):
            px, py = _chip_peer(xi, yi, r >> 1)
            return px, py, (1 - ci if r & 1 else ci)

        def copy(r, slot):
            return pltpu.make_async_remote_copy(
                src_ref=stage_ref, dst_ref=buf_ref.at[slot], send_sem=send_sems.at[r - 1], recv_sem=recv_sems.at[r - 1],
                device_id=peer(r), device_id_type=MESH)

        sends = []
        for r in range(1, N_DEV):
            cp = copy(r, me)
            cp.start()
            sends.append(cp)
        for r in range(1, N_DEV):
            px, py, pc = peer(r)
            copy(r, 4 * px + 2 * py + pc).wait_recv()
        for cp in sends:
            cp.wait_send()
        acc = buf_ref[0]
        for d in range(1, N_DEV):
            acc = acc + buf_ref[d]
        tot_ref[...] = acc

        def rows(name, n, width):
            r0 = _ST[name]
            return tot_ref[r0:r0 + n, 0:width]

        def mine(name, n, width):
            r0 = _ST[name]
            out = tot_ref[r0:r0 + n, 0:width]
            for j in range(1, N_CHIPS):
                out = jnp.where(chip == j, tot_ref[r0:r0 + n, j * width:(j + 1) * width], out)
            return out

        loss_ref[...] = rows("loss", 1, 1)
        grads = {
            "norm_mix": rows("norm_mix0", 2, D), "attn_b_qkv": rows("attn_b_qkv", 1, QKV),
            "attn_sinks": rows("attn_sinks", 1, N_HEADS), "attn_b_o": rows("attn_b_o", 1, D),
            "norm_ffn": rows("norm_ffn0", 2, D), "ffn_b_dw": rows("ffn_b_dw0", 2, DFF),
            "final_norm": rows("final_norm", 1, D),
            "conv_b_pw1": mine("conv_b_pw1", 1, 2 * D // N_CHIPS), "conv_w_dw": mine("conv_w_dw", CONV_W, D // N_CHIPS),
            "conv_b_dw": mine("conv_b_dw", 1, D // N_CHIPS), "conv_ln_g": mine("conv_ln_g", 1, D // N_CHIPS),
            "conv_ln_b": mine("conv_ln_b", 1, D // N_CHIPS), "conv_b_pw2": mine("conv_b_pw2", 1, D // N_CHIPS),
        }
        for name in names:
            if name == "ffn_w_dw":
                continue
            at = 0 if name == "conv_w_dw" else Ellipsis
            g = grads[name]
            d, mn, vn = _adam_math(w_ref[name][at], g, m_ref[name][at], v_ref[name][at])
            g_out[name][at] = g
            d_out[name][at] = d
            m_out[name][at] = mn
            v_out[name][at] = vn
        for layer, key in enumerate(("ffn_w_dw0", "ffn_w_dw1")):
            g = mine(key, 3, DFF // N_CHIPS)
            d, mn, vn = _adam_math(w_ref["ffn_w_dw"][layer], g, m_ref["ffn_w_dw"][layer], v_ref["ffn_w_dw"][layer])
            g_out["ffn_w_dw"][layer] = g
            d_out["ffn_w_dw"][layer] = d
            m_out["ffn_w_dw"][layer] = mn
            v_out["ffn_w_dw"][layer] = vn

    ins = [parts[k] for k in _SMALL_PARTS] + [src[k] for src in (w, m, v) for k in names]
    wshapes = [SDS(w[k].shape, f32) for k in names]
    outs = pl.pallas_call(
        body, name="small_reduce_adamw", in_specs=[_VMEM] * len(ins), out_specs=[_VMEM] * (1 + 4 * nw),
        out_shape=[SDS((1, 1), f32)] + wshapes * 4,
        scratch_shapes=[pltpu.VMEM((STAGE_ROWS, STAGE_W), f32), pltpu.VMEM((N_DEV, STAGE_ROWS, STAGE_W), f32),
                        pltpu.VMEM((STAGE_ROWS, STAGE_W), f32), pltpu.SemaphoreType.DMA((N_DEV - 1,)),
                        pltpu.SemaphoreType.DMA((N_DEV - 1,))],
        compiler_params=pltpu.CompilerParams(vmem_limit_bytes=VMEM_LIMIT),
    )(*ins)
    loss = outs[0]
    g, d, mn, vn = (dict(zip(names, outs[1 + k * nw:1 + (k + 1) * nw])) for k in range(4))
    return loss, g, d, mn, vn


TM = 512
TM_FFN = 256
FFN_CHUNK = 256
CONV_ROWS = 128
CONV_BWD_ROWS = 64
TK = 2048
TK_LIGHT = 4096
FORWARD_AT = 0.6


def kernel(x, norm_mix, attn_w_qkv, attn_b_qkv, attn_sinks, attn_w_o, attn_b_o, conv_w_pw1, conv_b_pw1, conv_w_dw, conv_b_dw, conv_ln_g, conv_ln_b, conv_w_pw2, conv_b_pw2, norm_ffn, ffn_w_up, ffn_w_dw, ffn_b_dw, ffn_w_down, final_norm, loss_target, m_norm_mix, m_attn_w_qkv, m_attn_b_qkv, m_attn_sinks, m_attn_w_o, m_attn_b_o, m_conv_w_pw1, m_conv_b_pw1, m_conv_w_dw, m_conv_b_dw, m_conv_ln_g, m_conv_ln_b, m_conv_w_pw2, m_conv_b_pw2, m_norm_ffn, m_ffn_w_up, m_ffn_w_dw, m_ffn_b_dw, m_ffn_w_down, m_final_norm, v_norm_mix, v_attn_w_qkv, v_attn_b_qkv, v_attn_sinks, v_attn_w_o, v_attn_b_o, v_conv_w_pw1, v_conv_b_pw1, v_conv_w_dw, v_conv_b_dw, v_conv_ln_g, v_conv_ln_b, v_conv_w_pw2, v_conv_b_pw2, v_norm_ffn, v_ffn_w_up, v_ffn_w_dw, v_ffn_b_dw, v_ffn_w_down, v_final_norm):
    w = dict(norm_mix=norm_mix, attn_w_qkv=attn_w_qkv, attn_b_qkv=attn_b_qkv, attn_sinks=attn_sinks, attn_w_o=attn_w_o,
             attn_b_o=attn_b_o, conv_w_pw1=conv_w_pw1, conv_b_pw1=conv_b_pw1, conv_w_dw=conv_w_dw, conv_b_dw=conv_b_dw,
             conv_ln_g=conv_ln_g, conv_ln_b=conv_ln_b, conv_w_pw2=conv_w_pw2, conv_b_pw2=conv_b_pw2, norm_ffn=norm_ffn,
             ffn_w_up=ffn_w_up, ffn_w_dw=ffn_w_dw, ffn_b_dw=ffn_b_dw, ffn_w_down=ffn_w_down, final_norm=final_norm)
    mom = dict(norm_mix=m_norm_mix, attn_w_qkv=m_attn_w_qkv, attn_b_qkv=m_attn_b_qkv, attn_sinks=m_attn_sinks,
               attn_w_o=m_attn_w_o, attn_b_o=m_attn_b_o, conv_w_pw1=m_conv_w_pw1, conv_b_pw1=m_conv_b_pw1,
               conv_w_dw=m_conv_w_dw, conv_b_dw=m_conv_b_dw, conv_ln_g=m_conv_ln_g, conv_ln_b=m_conv_ln_b,
               conv_w_pw2=m_conv_w_pw2, conv_b_pw2=m_conv_b_pw2, norm_ffn=m_norm_ffn, ffn_w_up=m_ffn_w_up,
               ffn_w_dw=m_ffn_w_dw, ffn_b_dw=m_ffn_b_dw, ffn_w_down=m_ffn_w_down, final_norm=m_final_norm)
    vel = dict(norm_mix=v_norm_mix, attn_w_qkv=v_attn_w_qkv, attn_b_qkv=v_attn_b_qkv, attn_sinks=v_attn_sinks,
               attn_w_o=v_attn_w_o, attn_b_o=v_attn_b_o, conv_w_pw1=v_conv_w_pw1, conv_b_pw1=v_conv_b_pw1,
               conv_w_dw=v_conv_w_dw, conv_b_dw=v_conv_b_dw, conv_ln_g=v_conv_ln_g, conv_ln_b=v_conv_ln_b,
               conv_w_pw2=v_conv_w_pw2, conv_b_pw2=v_conv_b_pw2, norm_ffn=v_norm_ffn, ffn_w_up=v_ffn_w_up,
               ffn_w_dw=v_ffn_w_dw, ffn_b_dw=v_ffn_b_dw, ffn_w_down=v_ffn_w_down, final_norm=v_final_norm)
    order = ("norm_mix", "attn_w_qkv", "attn_b_qkv", "attn_sinks", "attn_w_o", "attn_b_o", "conv_w_pw1", "conv_b_pw1",
             "conv_w_dw", "conv_b_dw", "conv_ln_g", "conv_ln_b", "conv_w_pw2", "conv_b_pw2", "norm_ffn", "ffn_w_up",
             "ffn_w_dw", "ffn_b_dw", "ffn_w_down", "final_norm")
    xi, yi, ci = _place()
    chip = 2 * xi + yi
    xs, target = x[0], loss_target[0]
    s = xs.shape[0]
    tm, tmf, tk, tkl = min(TM, s), min(TM_FFN, s), min(TK, s), min(TK_LIGHT, s)
    row = lambda v: v.reshape(1, -1)
    join = lambda a, axis: jnp.concatenate([a[j] for j in range(N_CHIPS)], axis=axis)
    cast = lambda a: a.astype(bf16)
    small, big = {}, {}

    got = run_rider(gather_rider(
        [(cast(attn_w_qkv[0]), _gv_qkv, (N_CHIPS, D, QKV // N_CHIPS)), (cast(attn_w_o[0]), _gv_rows, (D, D))],
        [w[k] for k in SMALL_SH], 0.0), "gather_attn")
    qkv4, w_o = got[:2]
    sm = dict(zip(SMALL_SH, got[2:]))
    w_qkv = jnp.transpose(qkv4, (1, 0, 2)).reshape(D, QKV)
    sinks = attn_sinks.reshape(N_HEADS)
    b_pw1, conv_dw, conv_bdw = join(sm["conv_b_pw1"], 1), join(sm["conv_w_dw"], 2)[0], join(sm["conv_b_dw"], 1)
    ln_g, ln_b, b_pw2, ffn_dw = (join(sm["conv_ln_g"], 1), join(sm["conv_ln_b"], 1), join(sm["conv_b_pw2"], 1),
                                 join(sm["ffn_w_dw"], 2))

    h0, qkv = qkv_fwd(xs, row(norm_mix[0]), w_qkv, attn_b_qkv, tm)
    (o, lse), (w_up0, w_dn0) = attn_fwd(qkv, sinks, rider=gather_rider(
        [(cast(ffn_w_up[0]), _gv_cols, (D, 2 * DFF)), (cast(ffn_w_down[0]), _gv_rows, (DFF, D))], [], FORWARD_AT))
    x1 = attn_out_fwd(xs, o, w_o, attn_b_o, tm)
    (x2, h1, up0, gate0, act0), (w_pw1, w_pw2, w_up1, w_dn1) = ffn_fwd(
        x1, row(norm_ffn[0]), w_up0, ffn_dw[0], row(ffn_b_dw[0]), w_dn0, tmf, FFN_CHUNK, rider=gather_rider(
            [(cast(conv_w_pw1[0]), _gv_cols, (D, 2 * D)), (cast(conv_w_pw2[0]), _gv_rows, (D, D)),
             (cast(ffn_w_up[1]), _gv_cols, (D, 2 * DFF)), (cast(ffn_w_down[1]), _gv_rows, (DFF, D))], [], FORWARD_AT))
    h2, a, u = pw1_fwd(x2, row(norm_mix[1]), w_pw1, b_pw1, tm)
    c, x3 = conv_fwd(u, x2, conv_dw, conv_bdw, ln_g, ln_b, w_pw2, b_pw2, tm, CONV_ROWS)
    dx4, h3, up1, gate1, act1, small["final_norm"], small["loss"] = ffn_fwd(
        x3, row(norm_ffn[1]), w_up1, ffn_dw[1], row(ffn_b_dw[1]), w_dn1, tmf, FFN_CHUNK,
        head=(final_norm.reshape(1, D), target))

    dx3, dup1, small["norm_ffn1"], small["ffn_w_dw1"], small["ffn_b_dw1"] = ffn_bwd(
        dx4, x3, row(norm_ffn[1]), up1, gate1, ffn_dw[1], w_dn1.T, w_up1.T, tmf, FFN_CHUNK)
    red1 = GradReduction(("up1", "down1"), [wgrad_cols(h3, dup1, tk, "wgrad_up1"),
                                           wgrad_rows(act1, dx4, 512, tk, "wgrad_down1")], ci, chip)
    (dc, u3, small["conv_ln_g"], small["conv_ln_b"], small["conv_b_pw2"], small["conv_b_dw"]), from_pair = conv_bwd_a(
        dx3, c, ln_g, ln_b, w_pw2.T, tm, rider=red1.pair_rider())
    g_pw2 = wgrad_rows(u3, dx3, 512, tkl, "wgrad_pw2")
    (dx2, da, small["conv_w_dw"], small["conv_b_pw1"], small["norm_mix1"]), landed = conv_bwd_b(
        dc, u, a, conv_dw, x2, row(norm_mix[1]), w_pw1.T, dx3, tm, CONV_BWD_ROWS,
        rider=red1.chip_rider(from_pair, FORWARD_AT))
    big.update(red1.result(landed))
    g_pw1 = wgrad_cols(h2, da, tkl, "wgrad_pw1")

    red2 = GradReduction(("pw1", "pw2"), [g_pw1, g_pw2], ci, chip)
    (dx1, dup0, small["norm_ffn0"], small["ffn_w_dw0"], small["ffn_b_dw0"]), from_pair = ffn_bwd(
        dx2, x1, row(norm_ffn[0]), up0, gate0, ffn_dw[0], w_dn0.T, w_up0.T, tmf, FFN_CHUNK, rider=red2.pair_rider())
    g_up0, landed = wgrad_cols(h1, dup0, tk, "wgrad_up0", rider=red2.chip_rider(from_pair, FORWARD_AT))
    big.update(red2.result(landed))
    g_dn0 = wgrad_rows(act0, dx2, 512, tk, "wgrad_down0")

    red3 = GradReduction(("up0", "down0", "wo"), [g_up0, g_dn0, wgrad_rows(o, dx1, 512, tkl, "wgrad_o")], ci, chip)
    (do, small["attn_b_o"]), from_pair = attn_out_bwd(dx1, w_o.T, tm, rider=red3.pair_rider())
    (dq, dkv, small["attn_sinks"]), landed = attn_bwd(qkv, o, do, lse, sinks,
                                                      rider=red3.chip_rider(from_pair, FORWARD_AT))
    big.update(red3.result(landed))
    dx0, small["norm_mix0"], small["attn_b_qkv"], dkvb = qkv_bwd(dq, dkv, xs, row(norm_mix[0]), w_qkv.T, dx1, tm)
    g_qkv = jnp.concatenate([wgrad(h0, dq, 512, tkl, "wgrad_q"), wgrad(h0, dkvb, 2 * N_KV * HD, tkl, "wgrad_kv")], axis=1)
    g_qkv = jnp.transpose(g_qkv.reshape(2, D // 2, N_CHIPS, QKV // N_CHIPS), (0, 2, 1, 3))
    red4 = GradReduction(("qkv",), [g_qkv], ci, chip)
    big.update(red4.alone())

    gbig = {
        "attn_w_qkv": big["qkv"].reshape(1, D, QKV // N_CHIPS), "attn_w_o": big["wo"].reshape(1, D // N_CHIPS, D),
        "conv_w_pw1": big["pw1"].reshape(1, D, 2 * D // N_CHIPS), "conv_w_pw2": big["pw2"].reshape(1, D // N_CHIPS, D),
        "ffn_w_up": jnp.stack([big["up0"], big["up1"]]).reshape(2, D, 2 * DFF // N_CHIPS),
        "ffn_w_down": jnp.stack([big["down0"], big["down1"]]).reshape(2, DFF // N_CHIPS, D),
    }

    fix = lambda d: {**d, "final_norm": d["final_norm"].reshape(1, D)}
    loss, gs, ds, ms, vs = small_reduce_adamw(small, fix(w), fix(mom), fix(vel))
    unfix = lambda d: {**d, "final_norm": d["final_norm"].reshape(D)}
    gout, delta, new_m, new_v = unfix(gs), unfix(ds), unfix(ms), unfix(vs)

    for name, g in gbig.items():
        gout[name] = g
        delta[name], new_m[name], new_v[name] = adamw(w[name], g, mom[name], vel[name], "adamw_" + name)

    return (loss.reshape(()), dx0[None], *[gout[n] for n in order], *[delta[n] for n in order],
            *[new_m[n] for n in order], *[new_v[n] for n in order])
```

```python
import math

import jax
import jax.numpy as jnp
from jax import lax
from jax.experimental import pallas as pl
from jax.experimental.pallas import tpu as pltpu

f32 = jnp.float32
bf16 = jnp.bfloat16
SDS = jax.ShapeDtypeStruct
MESH = pl.DeviceIdType.MESH

D = 1024
N_HEADS = 16
N_KV = 2
GROUP = 8
HD = 64
BLK = 128
QKV = (N_HEADS + 2 * N_KV) * HD
KV_COL_BLOCK = (N_HEADS * HD) // (2 * N_KV * HD)
CONV_W = 31
CONV_HALO = 32
DFF = 2816
RMS_EPS = 1e-6
LN_EPS = 1e-5
LR, B1, B2, ADAM_EPS, WD, STEP = 0.001, 0.9, 0.999, 1e-08, 0.01, 10

N_CHIPS = 4
N_DEV = 8
VMEM_LIMIT = 56 * 1024 * 1024
LANES = 128
SUB = 8
ELEMENTWISE_BLOCK_BYTES = 1 << 20


def _cp(*sem):
    return pltpu.CompilerParams(dimension_semantics=sem, vmem_limit_bytes=VMEM_LIMIT)


def _row(tm, n):
    return pl.BlockSpec((tm, n), lambda i: (i, 0))


def _const(shape):
    return pl.BlockSpec(shape, lambda *_: (0,) * len(shape), pipeline_mode=pl.Buffered(1))


def _acc(shape):
    return pl.BlockSpec(shape, lambda *_: (0,) * len(shape))


def _rms_fwd(x, g):
    r = lax.rsqrt(jnp.mean(x * x, axis=-1, keepdims=True) + RMS_EPS)
    xn = x * r
    return xn * g, xn, r


def _colsum8(v):
    return jnp.sum(v.reshape(v.shape[0] // SUB, SUB, v.shape[1]), axis=0)


def _rms_bwd(xn, r, g, dh):
    dyn = dh * g
    dx = r * (dyn - xn * jnp.mean(dyn * xn, axis=-1, keepdims=True))
    return dx, _colsum8(dh * xn)


def _sigmoid(z):
    return 0.5 * jnp.tanh(0.5 * z) + 0.5


def _dsilu(z, sg):
    return sg * (1.0 + z * (1.0 - sg))


def _dot(a, b):
    return jnp.dot(a, b, preferred_element_type=f32)


_ANY = pl.BlockSpec(memory_space=pl.ANY)
_VMEM = pl.BlockSpec(memory_space=pltpu.VMEM)


class Rider:
    def __init__(self, ins, out_shape, sem_shapes, stages, final):
        self.ins, self.out_shape, self.sem_shapes, self.stages, self.final = ins, out_shape, sem_shapes, stages, final


def run_rider(rider, name):
    n_in, n_out = len(rider.ins), len(rider.out_shape)

    def body(*refs):
        parts = refs[:n_in], refs[n_in:n_in + n_out], refs[n_in + n_out:]
        for _, fn in rider.stages:
            fn(*parts)
        rider.final(*parts)

    return pl.pallas_call(
        body, name=name, in_specs=[_ANY] * n_in, out_specs=[_ANY] * n_out, out_shape=list(rider.out_shape),
        scratch_shapes=list(rider.sem_shapes),
    )(*rider.ins)


def _hosted(rider, body, *, grid, in_specs, out_specs, out_shape, name, compiler_params, scratch_shapes=()):
    if rider is None:
        return pl.pallas_call(body, grid=grid, in_specs=in_specs, out_specs=out_specs, out_shape=out_shape, name=name,
                              compiler_params=compiler_params, scratch_shapes=list(scratch_shapes))
    single = not isinstance(out_shape, (list, tuple))
    shapes = [out_shape] if single else list(out_shape)
    specs = [out_specs] if single else list(out_specs)
    n_in, n_out, n_sc = len(in_specs), len(shapes), len(scratch_shapes)
    r_in, r_out = len(rider.ins), len(rider.out_shape)
    total = math.prod(grid)

    def wrapped(*refs):
        own_in, refs = refs[:n_in], refs[n_in:]
        r_ins, refs = refs[:r_in], refs[r_in:]
        own_out, refs = refs[:n_out], refs[n_out:]
        r_outs, refs = refs[:r_out], refs[r_out:]
        own_sc, r_sems = refs[:n_sc], refs[n_sc:]
        step = 0
        for d, n in enumerate(grid):
            step = step * n + pl.program_id(d)
        for frac, fn in rider.stages:
            @pl.when(step == min(int(frac * total), total - 1))
            def _(fn=fn):
                fn(r_ins, r_outs, r_sems)

        body(*own_in, *own_out, *own_sc)

        @pl.when(step == total - 1)
        def _():
            rider.final(r_ins, r_outs, r_sems)

    call = pl.pallas_call(
        wrapped, grid=grid, in_specs=list(in_specs) + [_ANY] * r_in, out_specs=specs + [_ANY] * r_out,
        out_shape=shapes + list(rider.out_shape), scratch_shapes=list(scratch_shapes) + list(rider.sem_shapes),
        name=name, compiler_params=_cp(*(("arbitrary",) * len(grid))))

    def run(*args):
        res = call(*args, *rider.ins)
        own = res[:n_out]
        return (own[0] if single else own), res[n_out:]

    return run


def qkv_fwd(x, g, w, b, tm):
    s = x.shape[0]

    def body(x_ref, g_ref, w_ref, b_ref, h_ref, o_ref):
        h, _, _ = _rms_fwd(x_ref[...], g_ref[...])
        hb = h.astype(bf16)
        h_ref[...] = hb
        o_ref[...] = (_dot(hb, w_ref[...]) + b_ref[...]).astype(bf16)

    return pl.pallas_call(
        body, grid=(s // tm,), name="qkv_fwd",
        in_specs=[_row(tm, D), _const((1, D)), _const((D, QKV)), _const((1, QKV))],
        out_specs=[_row(tm, D), _row(tm, QKV)],
        out_shape=[SDS((s, D), bf16), SDS((s, QKV), bf16)],
        compiler_params=_cp("parallel"),
    )(x, g, w, b)


def _band_mask(i):
    qi = lax.broadcasted_iota(jnp.int32, (GROUP * BLK, 2 * BLK), 0) & (BLK - 1)
    ki = lax.broadcasted_iota(jnp.int32, (GROUP * BLK, 2 * BLK), 1)
    dist = qi + BLK - ki
    return (dist >= 0) & (dist < BLK) & ((ki >= BLK) | (i > 0))


_NEG = float(jnp.finfo(jnp.float32).min)
_NT = (((1,), (1,)), ((), ()))
_TN = (((0,), (0,)), ((), ()))


def _kv_heads(kvp_ref, kvc_ref, kvh):
    ks = slice(kvh * HD, (kvh + 1) * HD)
    vs = slice(N_KV * HD + kvh * HD, N_KV * HD + (kvh + 1) * HD)
    k = jnp.concatenate([kvp_ref[:, ks], kvc_ref[:, ks]], axis=0)
    v = jnp.concatenate([kvp_ref[:, vs], kvc_ref[:, vs]], axis=0)
    return k, v


def _stack_group(ref, kvh, width=HD):
    return jnp.concatenate([ref[:, (kvh * GROUP + gi) * width:(kvh * GROUP + gi + 1) * width] for gi in range(GROUP)],
                           axis=0)


def _group_sinks(sink_ref, kvh):
    row = lax.broadcasted_iota(jnp.int32, (GROUP * BLK, 1), 0)
    col = jnp.zeros((GROUP * BLK, 1), f32)
    for g in range(GROUP):
        col = jnp.where((row >= g * BLK) & (row < (g + 1) * BLK), sink_ref[kvh * GROUP + g], col)
    return col


def attn_fwd(qkv, sinks, rider=None):
    s = qkv.shape[0]
    scale = 1.0 / math.sqrt(HD)

    def body(q_ref, kvc_ref, kvp_ref, sink_ref, o_ref, lse_ref):
        valid = _band_mask(pl.program_id(0))
        for kvh in range(N_KV):
            k, v = _kv_heads(kvp_ref, kvc_ref, kvh)
            sc = lax.dot_general(_stack_group(q_ref, kvh), k, _NT, preferred_element_type=f32) * scale
            sc = jnp.where(valid, sc, _NEG)
            sink = _group_sinks(sink_ref, kvh)
            m = jnp.maximum(jnp.max(sc, axis=-1, keepdims=True), sink)
            p = jnp.exp(sc - m)
            denom = jnp.sum(p, axis=-1, keepdims=True) + jnp.exp(sink - m)
            og = _dot((p / denom).astype(bf16), v).astype(bf16)
            lse = m + jnp.log(denom)
            for gi in range(GROUP):
                h = kvh * GROUP + gi
                o_ref[:, h * HD:(h + 1) * HD] = og[gi * BLK:(gi + 1) * BLK]
                lse_ref[:, h:h + 1] = lse[gi * BLK:(gi + 1) * BLK]

    return _hosted(
        rider, body, grid=(s // BLK,), name="attn_fwd",
        in_specs=[
            pl.BlockSpec((BLK, N_HEADS * HD), lambda i: (i, 0)),
            pl.BlockSpec((BLK, 2 * N_KV * HD), lambda i: (i, KV_COL_BLOCK)),
            pl.BlockSpec((BLK, 2 * N_KV * HD), lambda i: (jnp.maximum(i - 1, 0), KV_COL_BLOCK)),
            pl.BlockSpec(memory_space=pltpu.SMEM),
        ],
        out_specs=[_row(BLK, D), _row(BLK, N_HEADS)],
        out_shape=[SDS((s, D), bf16), SDS((s, N_HEADS), f32)],
        compiler_params=_cp("parallel"),
    )(qkv, qkv, qkv, sinks)


def attn_out_fwd(x, o, w, b, tm):
    s = x.shape[0]

    def body(x_ref, o_ref, w_ref, b_ref, y_ref):
        y_ref[...] = x_ref[...] + _dot(o_ref[...], w_ref[...]) + b_ref[...]

    return pl.pallas_call(
        body, grid=(s // tm,), name="attn_out_fwd",
        in_specs=[_row(tm, D), _row(tm, D), _const((D, D)), _const((1, D))],
        out_specs=_row(tm, D), out_shape=SDS((s, D), f32),
        compiler_params=_cp("parallel"),
    )(x, o, w, b)


def attn_out_bwd(dy, wt, tm, rider=None):
    s = dy.shape[0]

    def body(dy_ref, wt_ref, do_ref, db_ref):
        @pl.when(pl.program_id(0) == 0)
        def _():
            db_ref[...] = jnp.zeros_like(db_ref)

        dy = dy_ref[...]
        do_ref[...] = _dot(dy.astype(bf16), wt_ref[...]).astype(bf16)
        db_ref[...] += _colsum8(dy)

    return _hosted(
        rider, body, grid=(s // tm,), name="attn_out_bwd",
        in_specs=[_row(tm, D), _const((D, D))],
        out_specs=[_row(tm, D), _acc((SUB, D))],
        out_shape=[SDS((s, D), bf16), SDS((SUB, D), f32)],
        compiler_params=_cp("arbitrary"),
    )(dy, wt)


def attn_bwd(qkv, o, do, lse, sinks, rider=None):
    s = qkv.shape[0]
    nb = s // BLK
    scale = 1.0 / math.sqrt(HD)
    kvw = 2 * N_KV * HD

    def body(q_ref, kvc_ref, kvp_ref, o_ref, do_ref, lse_ref, sink_ref, dq_ref, dkv_ref, ds_ref, carry_ref):
        i = pl.program_id(0)

        @pl.when(i == 0)
        def _():
            ds_ref[...] = jnp.zeros_like(ds_ref)
            carry_ref[...] = jnp.zeros_like(carry_ref)

        @pl.when(i < nb)
        def _():
            valid = _band_mask(i)
            for kvh in range(N_KV):
                k, v = _kv_heads(kvp_ref, kvc_ref, kvh)
                qg = _stack_group(q_ref, kvh)
                dog = _stack_group(do_ref, kvh)
                lse = _stack_group(lse_ref, kvh, 1)
                sc = lax.dot_general(qg, k, _NT, preferred_element_type=f32) * scale
                sc = jnp.where(valid, sc, _NEG)
                p = jnp.exp(sc - lse)
                dp = lax.dot_general(dog, v, _NT, preferred_element_type=f32)
                dlt = jnp.sum(dog.astype(f32) * _stack_group(o_ref, kvh).astype(f32), axis=-1, keepdims=True)
                dsc = (p * (dp - dlt)).astype(bf16)
                dqg = (_dot(dsc, k) * scale).astype(bf16)
                dk = lax.dot_general(dsc, qg, _TN, preferred_element_type=f32) * scale
                dv = lax.dot_general(p.astype(bf16), dog, _TN, preferred_element_type=f32)
                dsink = jnp.exp(_group_sinks(sink_ref, kvh) - lse) * dlt
                for gi in range(GROUP):
                    h = kvh * GROUP + gi
                    dq_ref[:, h * HD:(h + 1) * HD] = dqg[gi * BLK:(gi + 1) * BLK]
                    ds_ref[:, h:h + 1] += -jnp.sum(dsink[gi * BLK:(gi + 1) * BLK], axis=0, keepdims=True)
                ks = slice(kvh * HD, (kvh + 1) * HD)
                vs = slice(N_KV * HD + kvh * HD, N_KV * HD + (kvh + 1) * HD)
                dkv_ref[:, ks] = carry_ref[:, ks] + dk[:BLK]
                dkv_ref[:, vs] = carry_ref[:, vs] + dv[:BLK]
                carry_ref[:, ks] = dk[BLK:]
                carry_ref[:, vs] = dv[BLK:]

        @pl.when(i == nb)
        def _():
            dkv_ref[...] = carry_ref[...]

    cur = lambda i: (jnp.minimum(i, nb - 1), 0)
    prev = lambda i: (jnp.clip(i - 1, 0, nb - 1), KV_COL_BLOCK)
    return _hosted(
        rider, body, grid=(nb + 1,), name="attn_bwd",
        in_specs=[
            pl.BlockSpec((BLK, D), cur),
            pl.BlockSpec((BLK, kvw), lambda i: (jnp.minimum(i, nb - 1), KV_COL_BLOCK)),
            pl.BlockSpec((BLK, kvw), prev),
            pl.BlockSpec((BLK, D), cur),
            pl.BlockSpec((BLK, D), cur),
            pl.BlockSpec((BLK, N_HEADS), cur),
            pl.BlockSpec(memory_space=pltpu.SMEM),
        ],
        out_specs=[
            pl.BlockSpec((BLK, D), cur),
            pl.BlockSpec((BLK, kvw), lambda i: (jnp.maximum(i - 1, 0), 0)),
            _acc((1, N_HEADS)),
        ],
        out_shape=[SDS((s, D), bf16), SDS((s, kvw), f32), SDS((1, N_HEADS), f32)],
        scratch_shapes=[pltpu.VMEM((BLK, kvw), f32)],
        compiler_params=_cp("arbitrary"),
    )(qkv, qkv, qkv, o, do, lse, sinks)


def qkv_bwd(dq, dkv, x, g, wt, dres, tm):
    s = x.shape[0]
    qd = N_HEADS * HD
    kvw = 2 * N_KV * HD

    def body(dq_ref, dkv_ref, x_ref, g_ref, wt_ref, dres_ref, dx_ref, dg_ref, db_ref, dkvb_ref):
        @pl.when(pl.program_id(0) == 0)
        def _():
            dg_ref[...] = jnp.zeros_like(dg_ref)
            db_ref[...] = jnp.zeros_like(db_ref)

        dq = dq_ref[...]
        dkv = dkv_ref[...]
        dkvb = dkv.astype(bf16)
        dkvb_ref[...] = dkvb
        dh = _dot(dq, wt_ref[0:qd, :]) + _dot(dkvb, wt_ref[qd:QKV, :])
        g = g_ref[...]
        _, xn, r = _rms_fwd(x_ref[...], g)
        dx, dg = _rms_bwd(xn, r, g, dh)
        dx_ref[...] = dres_ref[...] + dx
        dg_ref[...] += dg
        db_ref[:, 0:qd] += _colsum8(dq.astype(f32))
        db_ref[:, qd:QKV] += _colsum8(dkv)

    return pl.pallas_call(
        body, grid=(s // tm,), name="qkv_bwd",
        in_specs=[_row(tm, qd), _row(tm, kvw), _row(tm, D), _const((1, D)), _const((QKV, D)), _row(tm, D)],
        out_specs=[_row(tm, D), _acc((SUB, D)), _acc((SUB, QKV)), _row(tm, kvw)],
        out_shape=[SDS((s, D), f32), SDS((SUB, D), f32), SDS((SUB, QKV), f32), SDS((s, kvw), bf16)],
        compiler_params=_cp("arbitrary"),
    )(dq, dkv, x, g, wt, dres)


def ffn_fwd(x, g, wup, wdw, bdw, wdn, tm, cw, rider=None, head=None):
    s = x.shape[0]
    tail = 8

    def body(x_ref, g_ref, wup_ref, wdw_ref, bdw_ref, wdn_ref, *refs):
        if head is None:
            xo_ref, h_ref, up_ref, gate_ref, act_ref, carry_ref, ext_ref = refs
        else:
            gf_ref, t_ref, xo_ref, h_ref, up_ref, gate_ref, act_ref, dgf_ref, loss_ref, carry_ref, ext_ref = refs

        @pl.when(pl.program_id(0) == 0)
        def _():
            carry_ref[...] = jnp.zeros_like(carry_ref)
            if head is not None:
                dgf_ref[...] = jnp.zeros_like(dgf_ref)
                loss_ref[...] = jnp.zeros_like(loss_ref)

        x = x_ref[...]
        h, _, _ = _rms_fwd(x, g_ref[...])
        hb = h.astype(bf16)
        h_ref[...] = hb
        for c in range(DFF // cw):
            cs = slice(c * cw, (c + 1) * cw)
            vs = slice(DFF + c * cw, DFF + (c + 1) * cw)
            ug = _dot(hb, wup_ref[:, cs])
            uv = _dot(hb, wup_ref[:, vs])
            up_ref[:, cs] = ug.astype(bf16)
            up_ref[:, vs] = uv.astype(bf16)
            ext_ref[0:tail, :] = carry_ref[:, cs]
            ext_ref[tail:tail + tm, :] = ug
            carry_ref[:, cs] = ug[tm - tail:, :]
            ext = ext_ref[...]
            gate = (wdw_ref[0:1, cs] * pltpu.roll(ext, 2, 0)[tail:] + wdw_ref[1:2, cs] * pltpu.roll(ext, 1, 0)[tail:]
                    + wdw_ref[2:3, cs] * ug) + bdw_ref[:, cs]
            gate_ref[:, cs] = gate.astype(bf16)
            act_ref[:, cs] = (gate * _sigmoid(gate) * uv).astype(bf16)
        xo = x + _dot(act_ref[...], wdn_ref[...])
        if head is None:
            xo_ref[...] = xo
        else:
            gf = gf_ref[...]
            y, xn, r = _rms_fwd(xo, gf)
            e = y - t_ref[...]
            loss_ref[...] += 0.5 * jnp.sum(jnp.mean(e * e, axis=-1, keepdims=True), axis=0, keepdims=True)
            dx, dgf = _rms_bwd(xn, r, gf, e / D)
            xo_ref[...] = dx
            dgf_ref[...] += dgf

    extra_in = [] if head is None else [_const((1, D)), _row(tm, D)]
    extra_out = [] if head is None else [_acc((SUB, D)), _acc((1, 1))]
    extra_shape = [] if head is None else [SDS((SUB, D), f32), SDS((1, 1), f32)]
    return _hosted(
        rider, body, grid=(s // tm,), name="ffn_fwd" if head is None else "ffn_fwd_loss",
        in_specs=[_row(tm, D), _const((1, D)), _const((D, 2 * DFF)), _const((3, DFF)), _const((1, DFF)),
                  _const((DFF, D))] + extra_in,
        out_specs=[_row(tm, D), _row(tm, D), _row(tm, 2 * DFF), _row(tm, DFF), _row(tm, DFF)] + extra_out,
        out_shape=[SDS((s, D), f32), SDS((s, D), bf16), SDS((s, 2 * DFF), bf16), SDS((s, DFF), bf16),
                   SDS((s, DFF), bf16)] + extra_shape,
        scratch_shapes=[pltpu.VMEM((tail, DFF), f32), pltpu.VMEM((tail + tm, cw), f32)],
        compiler_params=_cp("arbitrary"),
    )(x, g, wup, wdw, bdw, wdn, *(head or ()))


def ffn_bwd(dxo, x, g, up, gate, wdw, wdnt, wupt, tm, cw, rider=None):
    s = x.shape[0]
    nt = s // tm
    rev = lambda i: (nt - 1 - i, 0)

    def body(dxo_ref, x_ref, g_ref, up_ref, gate_ref, wdw_ref, wdnt_ref, wupt_ref,
             dxi_ref, dup_ref, dg_ref, dwdw_ref, dbdw_ref, carry_ref, ext2_ref):
        i = pl.program_id(0)

        @pl.when(i == 0)
        def _():
            carry_ref[...] = jnp.zeros_like(carry_ref)
            dg_ref[...] = jnp.zeros_like(dg_ref)
            dwdw_ref[...] = jnp.zeros_like(dwdw_ref)
            dbdw_ref[...] = jnp.zeros_like(dbdw_ref)

        dxo = dxo_ref[...]
        dxb = dxo.astype(bf16)
        for c in range(DFF // cw):
            cs = slice(c * cw, (c + 1) * cw)
            vs = slice(DFF + c * cw, DFF + (c + 1) * cw)
            d_act = _dot(dxb, wdnt_ref[:, cs])
            ug = up_ref[:, cs].astype(f32)
            uv = up_ref[:, vs].astype(f32)
            gate = gate_ref[:, cs].astype(f32)
            sg = _sigmoid(gate)
            dup_ref[:, vs] = (d_act * (gate * sg)).astype(bf16)
            d_gate = d_act * uv * _dsilu(gate, sg)
            ext2_ref[0:tm, :] = d_gate
            ext2_ref[tm:tm + 8, :] = carry_ref[:, cs]
            carry_ref[:, cs] = d_gate[0:8, :]
            ext = ext2_ref[...]
            ahead1 = pltpu.roll(ext, tm + 8 - 1, 0)[:tm]
            ahead2 = pltpu.roll(ext, tm + 8 - 2, 0)[:tm]
            dbdw_ref[:, cs] += _colsum8(d_gate)
            dwdw_ref[0, :, cs] += _colsum8(ahead2 * ug)
            dwdw_ref[1, :, cs] += _colsum8(ahead1 * ug)
            dwdw_ref[2, :, cs] += _colsum8(d_gate * ug)
            d_ug = wdw_ref[0:1, cs] * ahead2 + wdw_ref[1:2, cs] * ahead1 + wdw_ref[2:3, cs] * d_gate
            dup_ref[:, cs] = d_ug.astype(bf16)
        dh = _dot(dup_ref[...], wupt_ref[...])
        gv = g_ref[...]
        _, xn, r = _rms_fwd(x_ref[...], gv)
        dx, dg = _rms_bwd(xn, r, gv, dh)
        dxi_ref[...] = dxo + dx
        dg_ref[...] += dg

    return _hosted(
        rider, body, grid=(nt,), name="ffn_bwd",
        in_specs=[
            pl.BlockSpec((tm, D), rev), pl.BlockSpec((tm, D), rev), _const((1, D)),
            pl.BlockSpec((tm, 2 * DFF), rev), pl.BlockSpec((tm, DFF), rev),
            _const((3, DFF)), _const((D, DFF)), _const((2 * DFF, D)),
        ],
        out_specs=[pl.BlockSpec((tm, D), rev), pl.BlockSpec((tm, 2 * DFF), rev), _acc((SUB, D)),
                   _acc((3, SUB, DFF)), _acc((SUB, DFF))],
        out_shape=[SDS((s, D), f32), SDS((s, 2 * DFF), bf16), SDS((SUB, D), f32), SDS((3, SUB, DFF), f32),
                   SDS((SUB, DFF), f32)],
        scratch_shapes=[pltpu.VMEM((8, DFF), f32), pltpu.VMEM((tm + 8, cw), f32)],
        compiler_params=_cp("arbitrary"),
    )(dxo, x, g, up, gate, wdw, wdnt, wupt)


def pw1_fwd(x, g, w, b, tm):
    s = x.shape[0]

    def body(x_ref, g_ref, w_ref, b_ref, h_ref, a_ref, u_ref):
        h, _, _ = _rms_fwd(x_ref[...], g_ref[...])
        hb = h.astype(bf16)
        h_ref[...] = hb
        a = _dot(hb, w_ref[...]) + b_ref[...]
        a_ref[...] = a.astype(bf16)
        u_ref[...] = a[:, :D] * _sigmoid(a[:, D:])

    return pl.pallas_call(
        body, grid=(s // tm,), name="pw1_fwd",
        in_specs=[_row(tm, D), _const((1, D)), _const((D, 2 * D)), _const((1, 2 * D))],
        out_specs=[_row(tm, D), _row(tm, 2 * D), _row(tm, D)],
        out_shape=[SDS((s, D), bf16), SDS((s, 2 * D), bf16), SDS((s, D), f32)],
        compiler_params=_cp("parallel"),
    )(x, g, w, b)


def _ln_silu(c, lg, lb):
    mu = jnp.mean(c, axis=-1, keepdims=True)
    cc = c - mu
    var = jnp.mean(cc * cc, axis=-1, keepdims=True)
    rstd = lax.rsqrt(var + LN_EPS)
    xh = cc * rstd
    ln = xh * lg + lb
    sg = _sigmoid(ln)
    return xh, rstd, ln, sg


def _shifted_copies(ext_ref, sh_ref, cs, tm):
    n = CONV_HALO - SUB + tm
    for k in range(1, SUB):
        sh_ref[k - 1] = ext_ref[pl.ds(k, n), cs]


def _shifted_rows(ext_ref, sh_ref, cs, start, rows):
    q, k = divmod(start, SUB)
    if k == 0:
        return ext_ref[pl.ds(start, rows), cs]
    return sh_ref[k - 1, pl.ds(q * SUB, rows), :]


def conv_fwd(u, x, wdw, bdw, lg, lb, w2, b2, tm, rc):
    s = x.shape[0]
    hl = CONV_HALO
    off = hl - (CONV_W - 1)

    def body(u_ref, halo_ref, x_ref, wdw_ref, bdw_ref, lg_ref, lb_ref, w2_ref, b2_ref, c_ref, xo_ref, ext_ref, sh_ref):
        has_prev = (pl.program_id(0) > 0).astype(f32)
        ext_ref[0:hl, :] = halo_ref[...] * has_prev
        ext_ref[hl:hl + tm, :] = u_ref[...]
        for cc in range(D // LANES):
            cs = slice(cc * LANES, (cc + 1) * LANES)
            _shifted_copies(ext_ref, sh_ref, cs, tm)
            for rr in range(tm // rc):
                acc = jnp.zeros((rc, LANES), f32) + bdw_ref[:, cs]
                for j in range(CONV_W):
                    acc = acc + wdw_ref[j:j + 1, cs] * _shifted_rows(ext_ref, sh_ref, cs, rr * rc + off + j, rc)
                c_ref[rr * rc:(rr + 1) * rc, cs] = acc
        _, _, ln, sg = _ln_silu(c_ref[...], lg_ref[...], lb_ref[...])
        xo_ref[...] = x_ref[...] + _dot((ln * sg).astype(bf16), w2_ref[...]) + b2_ref[...]

    return pl.pallas_call(
        body, grid=(s // tm,), name="conv_fwd",
        in_specs=[_row(tm, D), pl.BlockSpec((hl, D), lambda i: (jnp.maximum(i * (tm // hl) - 1, 0), 0)), _row(tm, D),
                  _const((CONV_W, D)), _const((1, D)), _const((1, D)), _const((1, D)), _const((D, D)), _const((1, D))],
        out_specs=[_row(tm, D), _row(tm, D)],
        out_shape=[SDS((s, D), f32), SDS((s, D), f32)],
        scratch_shapes=[pltpu.VMEM((hl + tm, D), f32), pltpu.VMEM((SUB - 1, hl - SUB + tm, LANES), f32)],
        compiler_params=_cp("parallel"),
    )(u, u, x, wdw, bdw, lg, lb, w2, b2)


def conv_bwd_a(dy, c, lg, lb, w2t, tm, rider=None):
    s = dy.shape[0]

    def body(dy_ref, c_ref, lg_ref, lb_ref, w2t_ref, dc_ref, u3_ref, dlg_ref, dlb_ref, db2_ref, dbdw_ref):
        @pl.when(pl.program_id(0) == 0)
        def _():
            for r in (dlg_ref, dlb_ref, db2_ref, dbdw_ref):
                r[...] = jnp.zeros_like(r)

        dy = dy_ref[...]
        lg = lg_ref[...]
        xh, rstd, ln, sg = _ln_silu(c_ref[...], lg, lb_ref[...])
        u3_ref[...] = (ln * sg).astype(bf16)
        du3 = _dot(dy.astype(bf16), w2t_ref[...])
        dln = du3 * _dsilu(ln, sg)
        dxh = dln * lg
        dc = rstd * (dxh - jnp.mean(dxh, axis=-1, keepdims=True) - xh * jnp.mean(dxh * xh, axis=-1, keepdims=True))
        dc_ref[...] = dc
        dlg_ref[...] += _colsum8(dln * xh)
        dlb_ref[...] += _colsum8(dln)
        db2_ref[...] += _colsum8(dy)
        dbdw_ref[...] += _colsum8(dc)

    return _hosted(
        rider, body, grid=(s // tm,), name="conv_bwd_a",
        in_specs=[_row(tm, D), _row(tm, D), _const((1, D)), _const((1, D)), _const((D, D))],
        out_specs=[_row(tm, D), _row(tm, D)] + [_acc((SUB, D))] * 4,
        out_shape=[SDS((s, D), f32), SDS((s, D), bf16)] + [SDS((SUB, D), f32)] * 4,
        compiler_params=_cp("arbitrary"),
    )(dy, c, lg, lb, w2t)


def conv_bwd_b(dc, u, a, wdw, x, g, w1t, dres, tm, rc, rider=None):
    s = x.shape[0]
    nt = s // tm
    hl = CONV_HALO
    off = hl - (CONV_W - 1)

    def body(dc_ref, dnext_ref, u_ref, uprev_ref, a_ref, wdw_ref, x_ref, g_ref, w1t_ref, dres_ref,
             dx_ref, da_ref, dwdw_ref, db1_ref, dg_ref, ext_ref, ext2_ref, du_ref, sh_ref, sh2_ref):
        i = pl.program_id(0)

        @pl.when(i == 0)
        def _():
            for r in (dwdw_ref, db1_ref, dg_ref):
                r[...] = jnp.zeros_like(r)

        ext_ref[0:hl, :] = uprev_ref[...] * (i > 0).astype(f32)
        ext_ref[hl:hl + tm, :] = u_ref[...]
        ext2_ref[0:tm, :] = dc_ref[...]
        ext2_ref[tm:tm + hl, :] = dnext_ref[...] * (i < nt - 1).astype(f32)
        for cc in range(D // LANES):
            cs = slice(cc * LANES, (cc + 1) * LANES)
            _shifted_copies(ext_ref, sh_ref, cs, tm)
            _shifted_copies(ext2_ref, sh2_ref, cs, tm)
            for rr in range(tm // rc):
                r0 = rr * rc
                dcb = ext2_ref[r0:r0 + rc, cs]
                acc = jnp.zeros((rc, LANES), f32)
                for j in range(CONV_W):
                    acc = acc + wdw_ref[j:j + 1, cs] * _shifted_rows(ext2_ref, sh2_ref, cs, r0 + CONV_W - 1 - j, rc)
                    dwdw_ref[j, :, cs] += _colsum8(dcb * _shifted_rows(ext_ref, sh_ref, cs, r0 + off + j, rc))
                du_ref[r0:r0 + rc, cs] = acc
        du = du_ref[...]
        a1 = a_ref[:, :D].astype(f32)
        sg = _sigmoid(a_ref[:, D:].astype(f32))
        da1 = du * sg
        da2 = du * a1 * sg * (1.0 - sg)
        da_ref[:, :D] = da1.astype(bf16)
        da_ref[:, D:] = da2.astype(bf16)
        db1_ref[:, :D] += _colsum8(da1)
        db1_ref[:, D:] += _colsum8(da2)
        dh = _dot(da_ref[...], w1t_ref[...])
        gv = g_ref[...]
        _, xn, r = _rms_fwd(x_ref[...], gv)
        dx, dg = _rms_bwd(xn, r, gv, dh)
        dx_ref[...] = dres_ref[...] + dx
        dg_ref[...] += dg

    blocks = tm // hl
    return _hosted(
        rider, body, grid=(nt,), name="conv_bwd_b",
        in_specs=[
            _row(tm, D), pl.BlockSpec((hl, D), lambda i: (jnp.minimum((i + 1) * blocks, s // hl - 1), 0)),
            _row(tm, D), pl.BlockSpec((hl, D), lambda i: (jnp.maximum(i * blocks - 1, 0), 0)),
            _row(tm, 2 * D), _const((CONV_W, D)), _row(tm, D), _const((1, D)), _const((2 * D, D)), _row(tm, D),
        ],
        out_specs=[_row(tm, D), _row(tm, 2 * D), _acc((CONV_W, SUB, D)), _acc((SUB, 2 * D)), _acc((SUB, D))],
        out_shape=[SDS((s, D), f32), SDS((s, 2 * D), bf16), SDS((CONV_W, SUB, D), f32), SDS((SUB, 2 * D), f32),
                   SDS((SUB, D), f32)],
        scratch_shapes=[pltpu.VMEM((hl + tm, D), f32), pltpu.VMEM((tm + hl, D), f32), pltpu.VMEM((tm, D), f32),
                        pltpu.VMEM((SUB - 1, hl - SUB + tm, LANES), f32),
                        pltpu.VMEM((SUB - 1, hl - SUB + tm, LANES), f32)],
        compiler_params=_cp("arbitrary"),
    )(dc, dc, u, u, a, wdw, x, g, w1t, dres)


def wgrad(a, b, nb, tk, name, rider=None):
    s, k1 = a.shape
    n = b.shape[1]

    def body(a_ref, b_ref, o_ref):
        @pl.when(pl.program_id(1) == 0)
        def _():
            o_ref[...] = jnp.zeros_like(o_ref)

        o_ref[...] += lax.dot_general(a_ref[...], b_ref[...].astype(bf16), _TN, preferred_element_type=f32)

    return _hosted(
        rider, body, grid=(n // nb, s // tk), name=name,
        in_specs=[pl.BlockSpec((tk, k1), lambda j, k: (k, 0)), pl.BlockSpec((tk, nb), lambda j, k: (k, j))],
        out_specs=pl.BlockSpec((k1, nb), lambda j, k: (0, j)),
        out_shape=SDS((k1, n), f32),
        compiler_params=_cp("parallel", "arbitrary"),
    )(a, b)


def wgrad_cols(a, b, tk, name, rider=None):
    s, k1 = a.shape
    w = b.shape[1] // N_CHIPS

    def body(a_ref, b_ref, o_ref):
        @pl.when(pl.program_id(1) == 0)
        def _():
            o_ref[...] = jnp.zeros_like(o_ref)

        acc = lax.dot_general(a_ref[...], b_ref[...].astype(bf16), _TN, preferred_element_type=f32)
        o_ref[:, 0] += acc.reshape(2, k1 // 2, w)

    return _hosted(
        rider, body, grid=(N_CHIPS, s // tk), name=name,
        in_specs=[pl.BlockSpec((tk, k1), lambda j, k: (k, 0)), pl.BlockSpec((tk, w), lambda j, k: (k, j))],
        out_specs=pl.BlockSpec((2, 1, k1 // 2, w), lambda j, k: (0, j, 0, 0)),
        out_shape=SDS((2, N_CHIPS, k1 // 2, w), f32),
        compiler_params=_cp("parallel", "arbitrary"),
    )(a, b)


def wgrad_rows(a, b, nb, tk, name, rider=None):
    s, k1 = a.shape
    n = b.shape[1]
    r = k1 // (2 * N_CHIPS)

    def body(a_ref, b_ref, o_ref):
        @pl.when(pl.program_id(1) == 0)
        def _():
            o_ref[...] = jnp.zeros_like(o_ref)

        acc = lax.dot_general(a_ref[...], b_ref[...].astype(bf16), _TN, preferred_element_type=f32)
        for j in range(N_CHIPS):
            for h in range(2):
                o_ref[h, j] += acc[(2 * j + h) * r:(2 * j + h + 1) * r, :]

    return _hosted(
        rider, body, grid=(n // nb, s // tk), name=name,
        in_specs=[pl.BlockSpec((tk, k1), lambda j, k: (k, 0)), pl.BlockSpec((tk, nb), lambda j, k: (k, j))],
        out_specs=pl.BlockSpec((2, N_CHIPS, r, nb), lambda j, k: (0, 0, 0, j)),
        out_shape=SDS((2, N_CHIPS, r, n), f32),
        compiler_params=_cp("parallel", "arbitrary"),
    )(a, b)


def _adam_math(w, g, m, v):
    m = B1 * m + (1.0 - B1) * g
    v = B2 * v + (1.0 - B2) * (g * g)
    m_hat = m / (1.0 - B1 ** STEP)
    v_hat = v / (1.0 - B2 ** STEP)
    delta = -LR * (m_hat / (jnp.sqrt(v_hat) + ADAM_EPS) + WD * w)
    return delta, m, v


def _rows_tile(r, c, multiple=SUB):
    best = None
    for t in range(multiple, r + 1, multiple):
        if r % t == 0 and t * c * 4 <= ELEMENTWISE_BLOCK_BYTES:
            best = t
    return best if best is not None else r


def adamw(w, g, m, v, name):
    l, r, c = w.shape
    tr = _rows_tile(r, c)
    spec = pl.BlockSpec((1, tr, c), lambda i, j: (i, j, 0))

    def body(w_ref, g_ref, m_ref, v_ref, d_ref, mo_ref, vo_ref):
        d, mn, vn = _adam_math(w_ref[...], g_ref[...], m_ref[...], v_ref[...])
        d_ref[...] = d
        mo_ref[...] = mn
        vo_ref[...] = vn

    return pl.pallas_call(
        body, grid=(l, r // tr), name=name, in_specs=[spec] * 4, out_specs=[spec] * 3,
        out_shape=[SDS((l, r, c), f32)] * 3, compiler_params=_cp("parallel", "parallel"),
    )(w, g, m, v)


def _place():
    return lax.axis_index("x"), lax.axis_index("y"), lax.axis_index("c")


def _chip_peer(xi, yi, r):
    px = 1 - xi if r & 2 else xi
    py = 1 - yi if r & 1 else yi
    return px, py


def _gv_qkv(src, dst, j, h):
    rows = pl.ds(h * (D // 2), D // 2)
    return src.at[rows, :], dst.at[j, rows, :]


def _gv_rows(src, dst, j, h):
    r = src.shape[0] // 2
    return src.at[pl.ds(h * r, r), :], dst.at[pl.ds(j * 2 * r + h * r, r), :]


def _gv_cols(src, dst, j, h):
    r, w = src.shape[0] // 2, src.shape[1]
    return src.at[pl.ds(h * r, r), :], dst.at[pl.ds(h * r, r), pl.ds(j * w, w)]


def gather_rider(big, small, forward_at):
    nb, ns = len(big), len(small)
    n = nb + ns
    views = [v for _, v, _ in big]

    def env(ins, outs, sems):
        ici_send, ici_recv, d2d_send, d2d_recv, loc_sems = sems
        xi, yi, ci = _place()
        me = 2 * xi + yi

        def local(a, h):
            if a < nb:
                src, dst = views[a](ins[a], outs[a], me, h)
                return pltpu.make_async_copy(src, dst, loc_sems.at[2 * a + h])
            return pltpu.make_async_copy(ins[a], outs[a].at[me], loc_sems.at[nb + a])

        def ici(a, r, slot):
            px, py = _chip_peer(xi, yi, r)
            src, dst = views[a](ins[a], outs[a], slot, ci) if a < nb else (ins[a], outs[a].at[slot])
            k = 3 * a + r - 1
            return pltpu.make_async_remote_copy(src_ref=src, dst_ref=dst, send_sem=ici_send.at[k],
                                                recv_sem=ici_recv.at[k], device_id=(px, py, ci), device_id_type=MESH)

        def d2d(a, r, half):
            px, py = _chip_peer(xi, yi, r)
            _, dst = views[a](ins[a], outs[a], 2 * px + py, half)
            k = 3 * a + r - 1
            return pltpu.make_async_remote_copy(src_ref=dst, dst_ref=dst, send_sem=d2d_send.at[k],
                                                recv_sem=d2d_recv.at[k], device_id=(xi, yi, 1 - ci), device_id_type=MESH)

        return xi, yi, ci, me, local, ici, d2d

    def locals_of():
        return [(a, h) for a in range(nb) for h in range(2)] + [(a, 0) for a in range(nb, n)]

    def send(ins, outs, sems):
        _, _, _, me, local, ici, _ = env(ins, outs, sems)
        for a, h in locals_of():
            local(a, h).start()
        for a in range(n):
            for r in (1, 2, 3):
                ici(a, r, me).start()

    def forward(ins, outs, sems):
        xi, yi, ci, _, _, ici, d2d = env(ins, outs, sems)
        for a in range(n):
            for r in (1, 2, 3):
                px, py = _chip_peer(xi, yi, r)
                ici(a, r, 2 * px + py).wait_recv()
                if a < nb:
                    d2d(a, r, ci).start()

    def finish(ins, outs, sems):
        _, _, ci, me, local, ici, d2d = env(ins, outs, sems)
        for a in range(nb):
            for r in (1, 2, 3):
                d2d(a, r, 1 - ci).wait_recv()
        for a in range(n):
            for r in (1, 2, 3):
                ici(a, r, me).wait_send()
                if a < nb:
                    d2d(a, r, ci).wait_send()
        for a, h in locals_of():
            local(a, h).wait()

    dma = pltpu.SemaphoreType.DMA
    return Rider(
        ins=[b for b, _, _ in big] + list(small),
        out_shape=[SDS(shape, bf16) for _, _, shape in big] + [SDS((N_CHIPS,) + a.shape, a.dtype) for a in small],
        sem_shapes=[dma((3 * n,)), dma((3 * n,)), dma((max(3 * nb, 1),)), dma((max(3 * nb, 1),)), dma((2 * nb + ns,))],
        stages=[(0.0, send), (forward_at, forward)], final=finish)


def pair_send_rider(gs):
    n = len(gs)

    def copy(ins, outs, sems, a):
        xi, yi, ci = _place()
        return pltpu.make_async_remote_copy(
            src_ref=ins[a].at[1 - ci], dst_ref=outs[a], send_sem=sems[0].at[a], recv_sem=sems[1].at[a],
            device_id=(xi, yi, 1 - ci), device_id_type=MESH)

    def send(ins, outs, sems):
        for a in range(n):
            copy(ins, outs, sems, a).start()

    def finish(ins, outs, sems):
        for a in range(n):
            copy(ins, outs, sems, a).wait()

    dma = pltpu.SemaphoreType.DMA
    return Rider(ins=list(gs), out_shape=[SDS(g.shape[1:], g.dtype) for g in gs], sem_shapes=[dma((n,)), dma((n,))],
                 stages=[(0.0, send)], final=finish)


def chip_rider(pbs, p32s, forward_at):
    n = len(pbs)

    def env(ins, outs, sems):
        ici_send, ici_recv, d2d_send, d2d_recv, own_send, own_recv, loc_sems = sems
        pb, p32, recv, own = ins[:n], ins[n:], outs[:n], outs[n:]
        xi, yi, ci = _place()
        me = 2 * xi + yi
        sib = (xi, yi, 1 - ci)

        def ici(a, r, src_slot, dst_slot):
            px, py = _chip_peer(xi, yi, r)
            k = 3 * a + r - 1
            return pltpu.make_async_remote_copy(
                src_ref=pb[a].at[src_slot], dst_ref=recv[a].at[ci, dst_slot], send_sem=ici_send.at[k],
                recv_sem=ici_recv.at[k], device_id=(px, py, ci), device_id_type=MESH)

        def d2d(a, r, half):
            px, py = _chip_peer(xi, yi, r)
            blk = recv[a].at[half, 2 * px + py]
            k = 3 * a + r - 1
            return pltpu.make_async_remote_copy(src_ref=blk, dst_ref=blk, send_sem=d2d_send.at[k],
                                                recv_sem=d2d_recv.at[k], device_id=sib, device_id_type=MESH)

        def mine(a, half):
            return pltpu.make_async_remote_copy(src_ref=p32[a].at[me], dst_ref=own[a].at[half], send_sem=own_send.at[a],
                                                recv_sem=own_recv.at[a], device_id=sib, device_id_type=MESH)

        def local(a):
            return pltpu.make_async_copy(p32[a].at[me], own[a].at[ci], loc_sems.at[a])

        return xi, yi, ci, me, ici, d2d, mine, local

    def send(ins, outs, sems):
        xi, yi, ci, me, ici, _, mine, local = env(ins, outs, sems)
        for a in range(n):
            local(a).start()
            mine(a, ci).start()
            for r in (1, 2, 3):
                px, py = _chip_peer(xi, yi, r)
                ici(a, r, 2 * px + py, me).start()

    def forward(ins, outs, sems):
        xi, yi, ci, me, ici, d2d, _, _ = env(ins, outs, sems)
        for a in range(n):
            for r in (1, 2, 3):
                px, py = _chip_peer(xi, yi, r)
                ici(a, r, me, 2 * px + py).wait_recv()
                d2d(a, r, ci).start()

    def finish(ins, outs, sems):
        xi, yi, ci, me, ici, d2d, mine, local = env(ins, outs, sems)
        for a in range(n):
            mine(a, 1 - ci).wait_recv()
            for r in (1, 2, 3):
                d2d(a, r, 1 - ci).wait_recv()
        for a in range(n):
            mine(a, ci).wait_send()
            local(a).wait()
            for r in (1, 2, 3):
                px, py = _chip_peer(xi, yi, r)
                ici(a, r, 2 * px + py, me).wait_send()
                d2d(a, r, ci).wait_send()

    dma = pltpu.SemaphoreType.DMA
    return Rider(
        ins=list(pbs) + list(p32s),
        out_shape=[SDS((2,) + p.shape, bf16) for p in pbs] + [SDS((2,) + p.shape[1:], f32) for p in p32s],
        sem_shapes=[dma((3 * n,)), dma((3 * n,)), dma((3 * n,)), dma((3 * n,)), dma((n,)), dma((n,)), dma((n,))],
        stages=[(0.0, send), (forward_at, forward)], final=finish)


def rs_pair_add(g, other, ci, name):
    _, nsh, r, w = g.shape
    tr = _rows_tile(r, w, 16)

    def body(c_ref, g_ref, o_ref, p_ref, pb_ref):
        p = g_ref[0] + o_ref[...]
        p_ref[...] = p
        pb_ref[...] = p.astype(bf16)

    blk = pl.BlockSpec((1, tr, w), lambda j, i, c: (j, i, 0))
    return pl.pallas_call(
        body, name=name,
        grid_spec=pltpu.PrefetchScalarGridSpec(
            num_scalar_prefetch=1, grid=(nsh, r // tr),
            in_specs=[pl.BlockSpec((1, 1, tr, w), lambda j, i, c: (c[0], j, i, 0)), blk], out_specs=[blk, blk]),
        out_shape=[SDS((nsh, r, w), f32), SDS((nsh, r, w), bf16)],
        compiler_params=_cp("parallel", "parallel"),
    )(ci.reshape(1).astype(jnp.int32), g, other)


def rs_chip_add(own, recv, chip, name, layer=0, layers=1, into=None):
    _, nsh, r, w = recv.shape
    tr = _rows_tile(r, w, 16)

    def body(c_ref, own_ref, recv_ref, *refs):
        o_ref = refs[-1]
        me = c_ref[0]
        acc = None
        for j in range(N_CHIPS):
            term = jnp.where(me == j, own_ref[0], recv_ref[0, j].astype(f32))
            acc = term if acc is None else acc + term
        o_ref[0, 0] = acc

    args = (chip.reshape(1).astype(jnp.int32), own, recv) + (() if into is None else (into,))
    return pl.pallas_call(
        body, name=name,
        grid_spec=pltpu.PrefetchScalarGridSpec(
            num_scalar_prefetch=1, grid=(2, r // tr),
            in_specs=[pl.BlockSpec((1, tr, w), lambda h, i, c: (h, i, 0)),
                      pl.BlockSpec((1, nsh, tr, w), lambda h, i, c: (h, 0, i, 0))] + ([] if into is None else [_ANY]),
            out_specs=pl.BlockSpec((1, 1, tr, w), lambda h, i, c: (layer, h, i, 0))),
        out_shape=SDS((layers, 2, r, w), f32),
        input_output_aliases={} if into is None else {3: 0},
        compiler_params=_cp("parallel", "parallel"),
    )(*args)


class GradReduction:
    def __init__(self, keys, grads, ci, chip):
        self.keys, self.grads, self.ci, self.chip = keys, grads, ci, chip

    def pair_rider(self):
        return pair_send_rider(self.grads)

    def chip_rider(self, from_pair, forward_at):
        both = [rs_pair_add(g, o, self.ci, "rs_pair_add_" + k) for k, g, o in zip(self.keys, self.grads, from_pair)]
        return chip_rider([pb for _, pb in both], [p for p, _ in both], forward_at)

    def result(self, landed, place=None):
        n = len(self.keys)
        return {k: rs_chip_add(own, recv, self.chip, "rs_chip_add_" + k, *(place or {}).get(k, ()))
                for k, recv, own in zip(self.keys, landed[:n], landed[n:])}

    def alone(self):
        from_pair = run_rider(self.pair_rider(), "rs_pair_" + self.keys[0])
        return self.result(run_rider(self.chip_rider(from_pair, 0.0), "rs_chip_" + self.keys[0]))


STAGE_W = DFF
_ST = {"norm_mix0": 0, "norm_mix1": 1, "attn_b_qkv": 2, "attn_sinks": 3, "attn_b_o": 4, "final_norm": 5, "loss": 6,
       "norm_ffn0": 8, "norm_ffn1": 9, "ffn_b_dw0": 10, "ffn_b_dw1": 11, "conv_b_pw1": 12, "conv_b_dw": 13,
       "conv_ln_g": 14, "conv_ln_b": 15, "conv_b_pw2": 16, "ffn_w_dw0": 17, "ffn_w_dw1": 20, "conv_w_dw": 24}
STAGE_ROWS = 56
SMALL_REP = ("norm_mix", "attn_b_qkv", "attn_sinks", "attn_b_o", "norm_ffn", "ffn_b_dw", "final_norm")
SMALL_SH = ("conv_b_pw1", "conv_w_dw", "conv_b_dw", "conv_ln_g", "conv_ln_b", "conv_b_pw2", "ffn_w_dw")
_SMALL_PARTS = ("norm_mix0", "norm_mix1", "attn_b_qkv", "attn_sinks", "attn_b_o", "norm_ffn0", "norm_ffn1", "ffn_b_dw0",
                "ffn_b_dw1", "final_norm", "conv_b_pw1", "conv_b_dw", "conv_ln_g", "conv_ln_b", "conv_b_pw2", "loss",
                "ffn_w_dw0", "ffn_w_dw1", "conv_w_dw")


def small_reduce_adamw(parts, w, m, v):
    names = SMALL_REP + SMALL_SH
    npart, nw = len(_SMALL_PARTS), len(names)

    def body(*refs):
        part = dict(zip(_SMALL_PARTS, refs[:npart]))
        off = npart
        w_ref = dict(zip(names, refs[off:off + nw]))
        m_ref = dict(zip(names, refs[off + nw:off + 2 * nw]))
        v_ref = dict(zip(names, refs[off + 2 * nw:off + 3 * nw]))
        off += 3 * nw
        loss_ref = refs[off]
        g_out = dict(zip(names, refs[off + 1:off + 1 + nw]))
        d_out = dict(zip(names, refs[off + 1 + nw:off + 1 + 2 * nw]))
        m_out = dict(zip(names, refs[off + 1 + 2 * nw:off + 1 + 3 * nw]))
        v_out = dict(zip(names, refs[off + 1 + 3 * nw:off + 1 + 4 * nw]))
        stage_ref, buf_ref, tot_ref, send_sems, recv_sems = refs[off + 1 + 4 * nw:]

        xi, yi, ci = _place()
        me = 4 * xi + 2 * yi + ci
        chip = 2 * xi + yi

        stage_ref[...] = jnp.zeros_like(stage_ref)
        for name in _SMALL_PARTS:
            ref, r0 = part[name], _ST[name]
            if name in ("attn_sinks", "loss"):
                val = ref[...]
            elif name in ("ffn_w_dw0", "ffn_w_dw1", "conv_w_dw"):
                val = jnp.sum(ref[...], axis=1)
            else:
                val = jnp.sum(ref[...], axis=0, keepdims=True)
            stage_ref[r0:r0 + val.shape[0], 0:val.shape[1]] = val

        buf_ref[me] = stage_ref[...]

        def peer(r):
            px, py = _chip_peer(xi, yi, r >> 1)
            return px, py, (1 - ci if r & 1 else ci)

        def copy(r, slot):
            return pltpu.make_async_remote_copy(
                src_ref=stage_ref, dst_ref=buf_ref.at[slot], send_sem=send_sems.at[r - 1], recv_sem=recv_sems.at[r - 1],
                device_id=peer(r), device_id_type=MESH)

        sends = []
        for r in range(1, N_DEV):
            cp = copy(r, me)
            cp.start()
            sends.append(cp)
        for r in range(1, N_DEV):
            px, py, pc = peer(r)
            copy(r, 4 * px + 2 * py + pc).wait_recv()
        for cp in sends:
            cp.wait_send()
        acc = buf_ref[0]
        for d in range(1, N_DEV):
            acc = acc + buf_ref[d]
        tot_ref[...] = acc

        def rows(name, n, width):
            r0 = _ST[name]
            return tot_ref[r0:r0 + n, 0:width]

        def mine(name, n, width):
            r0 = _ST[name]
            out = tot_ref[r0:r0 + n, 0:width]
            for j in range(1, N_CHIPS):
                out = jnp.where(chip == j, tot_ref[r0:r0 + n, j * width:(j + 1) * width], out)
            return out

        loss_ref[...] = rows("loss", 1, 1)
        grads = {
            "norm_mix": rows("norm_mix0", 2, D), "attn_b_qkv": rows("attn_b_qkv", 1, QKV),
            "attn_sinks": rows("attn_sinks", 1, N_HEADS), "attn_b_o": rows("attn_b_o", 1, D),
            "norm_ffn": rows("norm_ffn0", 2, D), "ffn_b_dw": rows("ffn_b_dw0", 2, DFF),
            "final_norm": rows("final_norm", 1, D),
            "conv_b_pw1": mine("conv_b_pw1", 1, 2 * D // N_CHIPS), "conv_w_dw": mine("conv_w_dw", CONV_W, D // N_CHIPS),
            "conv_b_dw": mine("conv_b_dw", 1, D // N_CHIPS), "conv_ln_g": mine("conv_ln_g", 1, D // N_CHIPS),
            "conv_ln_b": mine("conv_ln_b", 1, D // N_CHIPS), "conv_b_pw2": mine("conv_b_pw2", 1, D // N_CHIPS),
        }
        for name in names:
            if name == "ffn_w_dw":
                continue
            at = 0 if name == "conv_w_dw" else Ellipsis
            g = grads[name]
            d, mn, vn = _adam_math(w_ref[name][at], g, m_ref[name][at], v_ref[name][at])
            g_out[name][at] = g
            d_out[name][at] = d
            m_out[name][at] = mn
            v_out[name][at] = vn
        for layer, key in enumerate(("ffn_w_dw0", "ffn_w_dw1")):
            g = mine(key, 3, DFF // N_CHIPS)
            d, mn, vn = _adam_math(w_ref["ffn_w_dw"][layer], g, m_ref["ffn_w_dw"][layer], v_ref["ffn_w_dw"][layer])
            g_out["ffn_w_dw"][layer] = g
            d_out["ffn_w_dw"][layer] = d
            m_out["ffn_w_dw"][layer] = mn
            v_out["ffn_w_dw"][layer] = vn

    ins = [parts[k] for k in _SMALL_PARTS] + [src[k] for src in (w, m, v) for k in names]
    wshapes = [SDS(w[k].shape, f32) for k in names]
    outs = pl.pallas_call(
        body, name="small_reduce_adamw", in_specs=[_VMEM] * len(ins), out_specs=[_VMEM] * (1 + 4 * nw),
        out_shape=[SDS((1, 1), f32)] + wshapes * 4,
        scratch_shapes=[pltpu.VMEM((STAGE_ROWS, STAGE_W), f32), pltpu.VMEM((N_DEV, STAGE_ROWS, STAGE_W), f32),
                        pltpu.VMEM((STAGE_ROWS, STAGE_W), f32), pltpu.SemaphoreType.DMA((N_DEV - 1,)),
                        pltpu.SemaphoreType.DMA((N_DEV - 1,))],
        compiler_params=pltpu.CompilerParams(vmem_limit_bytes=VMEM_LIMIT),
    )(*ins)
    loss = outs[0]
    g, d, mn, vn = (dict(zip(names, outs[1 + k * nw:1 + (k + 1) * nw])) for k in range(4))
    return loss, g, d, mn, vn


TM = 512
TM_FFN = 256
FFN_CHUNK = 256
CONV_ROWS = 128
CONV_BWD_ROWS = 64
TK = 2048
TK_LIGHT = 4096
FORWARD_AT = 0.6
FORWARD_AT_LATE = 0.85


def kernel(x, norm_mix, attn_w_qkv, attn_b_qkv, attn_sinks, attn_w_o, attn_b_o, conv_w_pw1, conv_b_pw1, conv_w_dw, conv_b_dw, conv_ln_g, conv_ln_b, conv_w_pw2, conv_b_pw2, norm_ffn, ffn_w_up, ffn_w_dw, ffn_b_dw, ffn_w_down, final_norm, loss_target, m_norm_mix, m_attn_w_qkv, m_attn_b_qkv, m_attn_sinks, m_attn_w_o, m_attn_b_o, m_conv_w_pw1, m_conv_b_pw1, m_conv_w_dw, m_conv_b_dw, m_conv_ln_g, m_conv_ln_b, m_conv_w_pw2, m_conv_b_pw2, m_norm_ffn, m_ffn_w_up, m_ffn_w_dw, m_ffn_b_dw, m_ffn_w_down, m_final_norm, v_norm_mix, v_attn_w_qkv, v_attn_b_qkv, v_attn_sinks, v_attn_w_o, v_attn_b_o, v_conv_w_pw1, v_conv_b_pw1, v_conv_w_dw, v_conv_b_dw, v_conv_ln_g, v_conv_ln_b, v_conv_w_pw2, v_conv_b_pw2, v_norm_ffn, v_ffn_w_up, v_ffn_w_dw, v_ffn_b_dw, v_ffn_w_down, v_final_norm):
    w = dict(norm_mix=norm_mix, attn_w_qkv=attn_w_qkv, attn_b_qkv=attn_b_qkv, attn_sinks=attn_sinks, attn_w_o=attn_w_o,
             attn_b_o=attn_b_o, conv_w_pw1=conv_w_pw1, conv_b_pw1=conv_b_pw1, conv_w_dw=conv_w_dw, conv_b_dw=conv_b_dw,
             conv_ln_g=conv_ln_g, conv_ln_b=conv_ln_b, conv_w_pw2=conv_w_pw2, conv_b_pw2=conv_b_pw2, norm_ffn=norm_ffn,
             ffn_w_up=ffn_w_up, ffn_w_dw=ffn_w_dw, ffn_b_dw=ffn_b_dw, ffn_w_down=ffn_w_down, final_norm=final_norm)
    mom = dict(norm_mix=m_norm_mix, attn_w_qkv=m_attn_w_qkv, attn_b_qkv=m_attn_b_qkv, attn_sinks=m_attn_sinks,
               attn_w_o=m_attn_w_o, attn_b_o=m_attn_b_o, conv_w_pw1=m_conv_w_pw1, conv_b_pw1=m_conv_b_pw1,
               conv_w_dw=m_conv_w_dw, conv_b_dw=m_conv_b_dw, conv_ln_g=m_conv_ln_g, conv_ln_b=m_conv_ln_b,
               conv_w_pw2=m_conv_w_pw2, conv_b_pw2=m_conv_b_pw2, norm_ffn=m_norm_ffn, ffn_w_up=m_ffn_w_up,
               ffn_w_dw=m_ffn_w_dw, ffn_b_dw=m_ffn_b_dw, ffn_w_down=m_ffn_w_down, final_norm=m_final_norm)
    vel = dict(norm_mix=v_norm_mix, attn_w_qkv=v_attn_w_qkv, attn_b_qkv=v_attn_b_qkv, attn_sinks=v_attn_sinks,
               attn_w_o=v_attn_w_o, attn_b_o=v_attn_b_o, conv_w_pw1=v_conv_w_pw1, conv_b_pw1=v_conv_b_pw1,
               conv_w_dw=v_conv_w_dw, conv_b_dw=v_conv_b_dw, conv_ln_g=v_conv_ln_g, conv_ln_b=v_conv_ln_b,
               conv_w_pw2=v_conv_w_pw2, conv_b_pw2=v_conv_b_pw2, norm_ffn=v_norm_ffn, ffn_w_up=v_ffn_w_up,
               ffn_w_dw=v_ffn_w_dw, ffn_b_dw=v_ffn_b_dw, ffn_w_down=v_ffn_w_down, final_norm=v_final_norm)
    order = ("norm_mix", "attn_w_qkv", "attn_b_qkv", "attn_sinks", "attn_w_o", "attn_b_o", "conv_w_pw1", "conv_b_pw1",
             "conv_w_dw", "conv_b_dw", "conv_ln_g", "conv_ln_b", "conv_w_pw2", "conv_b_pw2", "norm_ffn", "ffn_w_up",
             "ffn_w_dw", "ffn_b_dw", "ffn_w_down", "final_norm")
    xi, yi, ci = _place()
    chip = 2 * xi + yi
    xs, target = x[0], loss_target[0]
    s = xs.shape[0]
    tm, tmf, tk, tkl = min(TM, s), min(TM_FFN, s), min(TK, s), min(TK_LIGHT, s)
    row = lambda v: v.reshape(1, -1)
    join = lambda a, axis: jnp.concatenate([a[j] for j in range(N_CHIPS)], axis=axis)
    cast = lambda a: a.astype(bf16)
    small, big = {}, {}

    got = run_rider(gather_rider(
        [(cast(attn_w_qkv[0]), _gv_qkv, (N_CHIPS, D, QKV // N_CHIPS)), (cast(attn_w_o[0]), _gv_rows, (D, D))],
        [w[k] for k in SMALL_SH], 0.0), "gather_attn")
    qkv4, w_o = got[:2]
    sm = dict(zip(SMALL_SH, got[2:]))
    w_qkv = jnp.transpose(qkv4, (1, 0, 2)).reshape(D, QKV)
    sinks = attn_sinks.reshape(N_HEADS)
    b_pw1, conv_dw, conv_bdw = join(sm["conv_b_pw1"], 1), join(sm["conv_w_dw"], 2)[0], join(sm["conv_b_dw"], 1)
    ln_g, ln_b, b_pw2, ffn_dw = (join(sm["conv_ln_g"], 1), join(sm["conv_ln_b"], 1), join(sm["conv_b_pw2"], 1),
                                 join(sm["ffn_w_dw"], 2))

    h0, qkv = qkv_fwd(xs, row(norm_mix[0]), w_qkv, attn_b_qkv, tm)
    (o, lse), (w_up0, w_dn0) = attn_fwd(qkv, sinks, rider=gather_rider(
        [(cast(ffn_w_up[0]), _gv_cols, (D, 2 * DFF)), (cast(ffn_w_down[0]), _gv_rows, (DFF, D))], [], FORWARD_AT))
    x1 = attn_out_fwd(xs, o, w_o, attn_b_o, tm)
    (x2, h1, up0, gate0, act0), (w_pw1, w_pw2, w_up1, w_dn1) = ffn_fwd(
        x1, row(norm_ffn[0]), w_up0, ffn_dw[0], row(ffn_b_dw[0]), w_dn0, tmf, FFN_CHUNK, rider=gather_rider(
            [(cast(conv_w_pw1[0]), _gv_cols, (D, 2 * D)), (cast(conv_w_pw2[0]), _gv_rows, (D, D)),
             (cast(ffn_w_up[1]), _gv_cols, (D, 2 * DFF)), (cast(ffn_w_down[1]), _gv_rows, (DFF, D))], [],
            FORWARD_AT_LATE))
    h2, a, u = pw1_fwd(x2, row(norm_mix[1]), w_pw1, b_pw1, tm)
    c, x3 = conv_fwd(u, x2, conv_dw, conv_bdw, ln_g, ln_b, w_pw2, b_pw2, tm, CONV_ROWS)
    dx4, h3, up1, gate1, act1, small["final_norm"], small["loss"] = ffn_fwd(
        x3, row(norm_ffn[1]), w_up1, ffn_dw[1], row(ffn_b_dw[1]), w_dn1, tmf, FFN_CHUNK,
        head=(final_norm.reshape(1, D), target))

    dx3, dup1, small["norm_ffn1"], small["ffn_w_dw1"], small["ffn_b_dw1"] = ffn_bwd(
        dx4, x3, row(norm_ffn[1]), up1, gate1, ffn_dw[1], w_dn1.T, w_up1.T, tmf, FFN_CHUNK)
    red1 = GradReduction(("up1", "down1"), [wgrad_cols(h3, dup1, tk, "wgrad_up1"),
                                           wgrad_rows(act1, dx4, 512, tk, "wgrad_down1")], ci, chip)
    (dc, u3, small["conv_ln_g"], small["conv_ln_b"], small["conv_b_pw2"], small["conv_b_dw"]), from_pair = conv_bwd_a(
        dx3, c, ln_g, ln_b, w_pw2.T, tm, rider=red1.pair_rider())
    g_pw2 = wgrad_rows(u3, dx3, 512, tkl, "wgrad_pw2")
    (dx2, da, small["conv_w_dw"], small["conv_b_pw1"], small["norm_mix1"]), landed = conv_bwd_b(
        dc, u, a, conv_dw, x2, row(norm_mix[1]), w_pw1.T, dx3, tm, CONV_BWD_ROWS,
        rider=red1.chip_rider(from_pair, FORWARD_AT))
    big.update(red1.result(landed, {"up1": (1, 2, None), "down1": (1, 2, None)}))
    g_pw1 = wgrad_cols(h2, da, tkl, "wgrad_pw1")

    red2 = GradReduction(("pw1", "pw2"), [g_pw1, g_pw2], ci, chip)
    (dx1, dup0, small["norm_ffn0"], small["ffn_w_dw0"], small["ffn_b_dw0"]), from_pair = ffn_bwd(
        dx2, x1, row(norm_ffn[0]), up0, gate0, ffn_dw[0], w_dn0.T, w_up0.T, tmf, FFN_CHUNK, rider=red2.pair_rider())
    g_up0, landed = wgrad_cols(h1, dup0, tk, "wgrad_up0", rider=red2.chip_rider(from_pair, FORWARD_AT))
    big.update(red2.result(landed))
    g_dn0 = wgrad_rows(act0, dx2, 512, tk, "wgrad_down0")

    red3 = GradReduction(("up0", "down0", "wo"), [g_up0, g_dn0, wgrad_rows(o, dx1, 512, tkl, "wgrad_o")], ci, chip)
    (do, small["attn_b_o"]), from_pair = attn_out_bwd(dx1, w_o.T, tm, rider=red3.pair_rider())
    (dq, dkv, small["attn_sinks"]), landed = attn_bwd(qkv, o, do, lse, sinks,
                                                      rider=red3.chip_rider(from_pair, FORWARD_AT))
    both = red3.result(landed, {"up0": (0, 2, big["up1"]), "down0": (0, 2, big["down1"])})
    big.update({"up": both.pop("up0"), "down": both.pop("down0"), **both})
    dx0, small["norm_mix0"], small["attn_b_qkv"], dkvb = qkv_bwd(dq, dkv, xs, row(norm_mix[0]), w_qkv.T, dx1, tm)
    g_qkv = jnp.concatenate([wgrad(h0, dq, 512, tkl, "wgrad_q"), wgrad(h0, dkvb, 2 * N_KV * HD, tkl, "wgrad_kv")], axis=1)
    g_qkv = jnp.transpose(g_qkv.reshape(2, D // 2, N_CHIPS, QKV // N_CHIPS), (0, 2, 1, 3))
    red4 = GradReduction(("qkv",), [g_qkv], ci, chip)
    big.update(red4.alone())

    gbig = {
        "attn_w_qkv": big["qkv"].reshape(1, D, QKV // N_CHIPS), "attn_w_o": big["wo"].reshape(1, D // N_CHIPS, D),
        "conv_w_pw1": big["pw1"].reshape(1, D, 2 * D // N_CHIPS), "conv_w_pw2": big["pw2"].reshape(1, D // N_CHIPS, D),
        "ffn_w_up": big["up"].reshape(2, D, 2 * DFF // N_CHIPS),
        "ffn_w_down": big["down"].reshape(2, DFF // N_CHIPS, D),
    }

    fix = lambda d: {**d, "final_norm": d["final_norm"].reshape(1, D)}
    loss, gs, ds, ms, vs = small_reduce_adamw(small, fix(w), fix(mom), fix(vel))
    unfix = lambda d: {**d, "final_norm": d["final_norm"].reshape(D)}
    gout, delta, new_m, new_v = unfix(gs), unfix(ds), unfix(ms), unfix(vs)

    for name, g in gbig.items():
        gout[name] = g
        delta[name], new_m[name], new_v[name] = adamw(w[name], g, mom[name], vel[name], "adamw_" + name)

    return (loss.reshape(()), dx0[None], *[gout[n] for n in order], *[delta[n] for n in order],
            *[new_m[n] for n in order], *[new_v[n] for n in order])
```

```python
import math

import jax
import jax.numpy as jnp
from jax import lax
from jax.experimental import pallas as pl
from jax.experimental.pallas import tpu as pltpu

f32 = jnp.float32
bf16 = jnp.bfloat16
SDS = jax.ShapeDtypeStruct
MESH = pl.DeviceIdType.MESH

D = 1024
N_HEADS = 16
N_KV = 2
GROUP = 8
HD = 64
BLK = 128
QKV = (N_HEADS + 2 * N_KV) * HD
KV_COL_BLOCK = (N_HEADS * HD) // (2 * N_KV * HD)
CONV_W = 31
CONV_HALO = 32
DFF = 2816
RMS_EPS = 1e-6
LN_EPS = 1e-5
LR, B1, B2, ADAM_EPS, WD, STEP = 0.001, 0.9, 0.999, 1e-08, 0.01, 10

N_CHIPS = 4
N_DEV = 8
VMEM_LIMIT = 56 * 1024 * 1024
LANES = 128
SUB = 8
ELEMENTWISE_BLOCK_BYTES = 1 << 20


def _cp(*sem):
    return pltpu.CompilerParams(dimension_semantics=sem, vmem_limit_bytes=VMEM_LIMIT)


def _row(tm, n):
    return pl.BlockSpec((tm, n), lambda i: (i, 0))


def _const(shape):
    return pl.BlockSpec(shape, lambda *_: (0,) * len(shape), pipeline_mode=pl.Buffered(1))


def _acc(shape):
    return pl.BlockSpec(shape, lambda *_: (0,) * len(shape))


def _rms_fwd(x, g):
    r = lax.rsqrt(jnp.mean(x * x, axis=-1, keepdims=True) + RMS_EPS)
    xn = x * r
    return xn * g, xn, r


def _colsum8(v):
    return jnp.sum(v.reshape(v.shape[0] // SUB, SUB, v.shape[1]), axis=0)


def _rms_bwd(xn, r, g, dh):
    dyn = dh * g
    dx = r * (dyn - xn * jnp.mean(dyn * xn, axis=-1, keepdims=True))
    return dx, _colsum8(dh * xn)


def _sigmoid(z):
    return 0.5 * jnp.tanh(0.5 * z) + 0.5


def _dsilu(z, sg):
    return sg * (1.0 + z * (1.0 - sg))


def _dot(a, b):
    return jnp.dot(a, b, preferred_element_type=f32)


_ANY = pl.BlockSpec(memory_space=pl.ANY)
_VMEM = pl.BlockSpec(memory_space=pltpu.VMEM)


class Rider:
    def __init__(self, ins, out_shape, sem_shapes, stages, final):
        self.ins, self.out_shape, self.sem_shapes, self.stages, self.final = ins, out_shape, sem_shapes, stages, final


def run_rider(rider, name):
    n_in, n_out = len(rider.ins), len(rider.out_shape)

    def body(*refs):
        parts = refs[:n_in], refs[n_in:n_in + n_out], refs[n_in + n_out:]
        for _, fn in rider.stages:
            fn(*parts)
        rider.final(*parts)

    return pl.pallas_call(
        body, name=name, in_specs=[_ANY] * n_in, out_specs=[_ANY] * n_out, out_shape=list(rider.out_shape),
        scratch_shapes=list(rider.sem_shapes),
    )(*rider.ins)


def _hosted(rider, body, *, grid, in_specs, out_specs, out_shape, name, compiler_params, scratch_shapes=()):
    if rider is None:
        return pl.pallas_call(body, grid=grid, in_specs=in_specs, out_specs=out_specs, out_shape=out_shape, name=name,
                              compiler_params=compiler_params, scratch_shapes=list(scratch_shapes))
    single = not isinstance(out_shape, (list, tuple))
    shapes = [out_shape] if single else list(out_shape)
    specs = [out_specs] if single else list(out_specs)
    n_in, n_out, n_sc = len(in_specs), len(shapes), len(scratch_shapes)
    r_in, r_out = len(rider.ins), len(rider.out_shape)
    total = math.prod(grid)

    def wrapped(*refs):
        own_in, refs = refs[:n_in], refs[n_in:]
        r_ins, refs = refs[:r_in], refs[r_in:]
        own_out, refs = refs[:n_out], refs[n_out:]
        r_outs, refs = refs[:r_out], refs[r_out:]
        own_sc, r_sems = refs[:n_sc], refs[n_sc:]
        step = 0
        for d, n in enumerate(grid):
            step = step * n + pl.program_id(d)
        for frac, fn in rider.stages:
            @pl.when(step == min(int(frac * total), total - 1))
            def _(fn=fn):
                fn(r_ins, r_outs, r_sems)

        body(*own_in, *own_out, *own_sc)

        @pl.when(step == total - 1)
        def _():
            rider.final(r_ins, r_outs, r_sems)

    call = pl.pallas_call(
        wrapped, grid=grid, in_specs=list(in_specs) + [_ANY] * r_in, out_specs=specs + [_ANY] * r_out,
        out_shape=shapes + list(rider.out_shape), scratch_shapes=list(scratch_shapes) + list(rider.sem_shapes),
        name=name, compiler_params=_cp(*(("arbitrary",) * len(grid))))

    def run(*args):
        res = call(*args, *rider.ins)
        own = res[:n_out]
        return (own[0] if single else own), res[n_out:]

    return run


def qkv_fwd(x, g, w, b, tm):
    s = x.shape[0]

    def body(x_ref, g_ref, w_ref, b_ref, h_ref, o_ref):
        h, _, _ = _rms_fwd(x_ref[...], g_ref[...])
        hb = h.astype(bf16)
        h_ref[...] = hb
        o_ref[...] = (_dot(hb, w_ref[...]) + b_ref[...]).astype(bf16)

    return pl.pallas_call(
        body, grid=(s // tm,), name="qkv_fwd",
        in_specs=[_row(tm, D), _const((1, D)), _const((D, QKV)), _const((1, QKV))],
        out_specs=[_row(tm, D), _row(tm, QKV)],
        out_shape=[SDS((s, D), bf16), SDS((s, QKV), bf16)],
        compiler_params=_cp("parallel"),
    )(x, g, w, b)


def _band_mask(i):
    qi = lax.broadcasted_iota(jnp.int32, (GROUP * BLK, 2 * BLK), 0) & (BLK - 1)
    ki = lax.broadcasted_iota(jnp.int32, (GROUP * BLK, 2 * BLK), 1)
    dist = qi + BLK - ki
    return (dist >= 0) & (dist < BLK) & ((ki >= BLK) | (i > 0))


_NEG = float(jnp.finfo(jnp.float32).min)
_NT = (((1,), (1,)), ((), ()))
_TN = (((0,), (0,)), ((), ()))


def _kv_heads(kvp_ref, kvc_ref, kvh):
    ks = slice(kvh * HD, (kvh + 1) * HD)
    vs = slice(N_KV * HD + kvh * HD, N_KV * HD + (kvh + 1) * HD)
    k = jnp.concatenate([kvp_ref[:, ks], kvc_ref[:, ks]], axis=0)
    v = jnp.concatenate([kvp_ref[:, vs], kvc_ref[:, vs]], axis=0)
    return k, v


def _stack_group(ref, kvh, width=HD):
    return jnp.concatenate([ref[:, (kvh * GROUP + gi) * width:(kvh * GROUP + gi + 1) * width] for gi in range(GROUP)],
                           axis=0)


def _group_sinks(sink_ref, kvh):
    row = lax.broadcasted_iota(jnp.int32, (GROUP * BLK, 1), 0)
    col = jnp.zeros((GROUP * BLK, 1), f32)
    for g in range(GROUP):
        col = jnp.where((row >= g * BLK) & (row < (g + 1) * BLK), sink_ref[kvh * GROUP + g], col)
    return col


def attn_fwd(qkv, sinks, rider=None):
    s = qkv.shape[0]
    scale = 1.0 / math.sqrt(HD)

    def body(q_ref, kvc_ref, kvp_ref, sink_ref, o_ref, lse_ref):
        valid = _band_mask(pl.program_id(0))
        for kvh in range(N_KV):
            k, v = _kv_heads(kvp_ref, kvc_ref, kvh)
            sc = lax.dot_general(_stack_group(q_ref, kvh), k, _NT, preferred_element_type=f32) * scale
            sc = jnp.where(valid, sc, _NEG)
            sink = _group_sinks(sink_ref, kvh)
            m = jnp.maximum(jnp.max(sc, axis=-1, keepdims=True), sink)
            p = jnp.exp(sc - m)
            denom = jnp.sum(p, axis=-1, keepdims=True) + jnp.exp(sink - m)
            og = _dot((p / denom).astype(bf16), v).astype(bf16)
            lse = m + jnp.log(denom)
            for gi in range(GROUP):
                h = kvh * GROUP + gi
                o_ref[:, h * HD:(h + 1) * HD] = og[gi * BLK:(gi + 1) * BLK]
                lse_ref[:, h:h + 1] = lse[gi * BLK:(gi + 1) * BLK]

    return _hosted(
        rider, body, grid=(s // BLK,), name="attn_fwd",
        in_specs=[
            pl.BlockSpec((BLK, N_HEADS * HD), lambda i: (i, 0)),
            pl.BlockSpec((BLK, 2 * N_KV * HD), lambda i: (i, KV_COL_BLOCK)),
            pl.BlockSpec((BLK, 2 * N_KV * HD), lambda i: (jnp.maximum(i - 1, 0), KV_COL_BLOCK)),
            pl.BlockSpec(memory_space=pltpu.SMEM),
        ],
        out_specs=[_row(BLK, D), _row(BLK, N_HEADS)],
        out_shape=[SDS((s, D), bf16), SDS((s, N_HEADS), f32)],
        compiler_params=_cp("parallel"),
    )(qkv, qkv, qkv, sinks)


def attn_out_fwd(x, o, w, b, tm):
    s = x.shape[0]

    def body(x_ref, o_ref, w_ref, b_ref, y_ref):
        y_ref[...] = x_ref[...] + _dot(o_ref[...], w_ref[...]) + b_ref[...]

    return pl.pallas_call(
        body, grid=(s // tm,), name="attn_out_fwd",
        in_specs=[_row(tm, D), _row(tm, D), _const((D, D)), _const((1, D))],
        out_specs=_row(tm, D), out_shape=SDS((s, D), f32),
        compiler_params=_cp("parallel"),
    )(x, o, w, b)


def attn_out_bwd(dy, wt, tm, rider=None):
    s = dy.shape[0]

    def body(dy_ref, wt_ref, do_ref, db_ref):
        @pl.when(pl.program_id(0) == 0)
        def _():
            db_ref[...] = jnp.zeros_like(db_ref)

        dy = dy_ref[...]
        do_ref[...] = _dot(dy.astype(bf16), wt_ref[...]).astype(bf16)
        db_ref[...] += _colsum8(dy)

    return _hosted(
        rider, body, grid=(s // tm,), name="attn_out_bwd",
        in_specs=[_row(tm, D), _const((D, D))],
        out_specs=[_row(tm, D), _acc((SUB, D))],
        out_shape=[SDS((s, D), bf16), SDS((SUB, D), f32)],
        compiler_params=_cp("arbitrary"),
    )(dy, wt)


def attn_bwd(qkv, o, do, lse, sinks, rider=None):
    s = qkv.shape[0]
    nb = s // BLK
    scale = 1.0 / math.sqrt(HD)
    kvw = 2 * N_KV * HD

    def body(q_ref, kvc_ref, kvp_ref, o_ref, do_ref, lse_ref, sink_ref, dq_ref, dkv_ref, ds_ref, carry_ref):
        i = pl.program_id(0)

        @pl.when(i == 0)
        def _():
            ds_ref[...] = jnp.zeros_like(ds_ref)
            carry_ref[...] = jnp.zeros_like(carry_ref)

        @pl.when(i < nb)
        def _():
            valid = _band_mask(i)
            for kvh in range(N_KV):
                k, v = _kv_heads(kvp_ref, kvc_ref, kvh)
                qg = _stack_group(q_ref, kvh)
                dog = _stack_group(do_ref, kvh)
                lse = _stack_group(lse_ref, kvh, 1)
                sc = lax.dot_general(qg, k, _NT, preferred_element_type=f32) * scale
                sc = jnp.where(valid, sc, _NEG)
                p = jnp.exp(sc - lse)
                dp = lax.dot_general(dog, v, _NT, preferred_element_type=f32)
                dlt = jnp.sum(dog.astype(f32) * _stack_group(o_ref, kvh).astype(f32), axis=-1, keepdims=True)
                dsc = (p * (dp - dlt)).astype(bf16)
                dqg = (_dot(dsc, k) * scale).astype(bf16)
                dk = lax.dot_general(dsc, qg, _TN, preferred_element_type=f32) * scale
                dv = lax.dot_general(p.astype(bf16), dog, _TN, preferred_element_type=f32)
                dsink = jnp.exp(_group_sinks(sink_ref, kvh) - lse) * dlt
                for gi in range(GROUP):
                    h = kvh * GROUP + gi
                    dq_ref[:, h * HD:(h + 1) * HD] = dqg[gi * BLK:(gi + 1) * BLK]
                    ds_ref[:, h:h + 1] += -jnp.sum(dsink[gi * BLK:(gi + 1) * BLK], axis=0, keepdims=True)
                ks = slice(kvh * HD, (kvh + 1) * HD)
                vs = slice(N_KV * HD + kvh * HD, N_KV * HD + (kvh + 1) * HD)
                dkv_ref[:, ks] = carry_ref[:, ks] + dk[:BLK]
                dkv_ref[:, vs] = carry_ref[:, vs] + dv[:BLK]
                carry_ref[:, ks] = dk[BLK:]
                carry_ref[:, vs] = dv[BLK:]

        @pl.when(i == nb)
        def _():
            dkv_ref[...] = carry_ref[...]

    cur = lambda i: (jnp.minimum(i, nb - 1), 0)
    prev = lambda i: (jnp.clip(i - 1, 0, nb - 1), KV_COL_BLOCK)
    return _hosted(
        rider, body, grid=(nb + 1,), name="attn_bwd",
        in_specs=[
            pl.BlockSpec((BLK, D), cur),
            pl.BlockSpec((BLK, kvw), lambda i: (jnp.minimum(i, nb - 1), KV_COL_BLOCK)),
            pl.BlockSpec((BLK, kvw), prev),
            pl.BlockSpec((BLK, D), cur),
            pl.BlockSpec((BLK, D), cur),
            pl.BlockSpec((BLK, N_HEADS), cur),
            pl.BlockSpec(memory_space=pltpu.SMEM),
        ],
        out_specs=[
            pl.BlockSpec((BLK, D), cur),
            pl.BlockSpec((BLK, kvw), lambda i: (jnp.maximum(i - 1, 0), 0)),
            _acc((1, N_HEADS)),
        ],
        out_shape=[SDS((s, D), bf16), SDS((s, kvw), f32), SDS((1, N_HEADS), f32)],
        scratch_shapes=[pltpu.VMEM((BLK, kvw), f32)],
        compiler_params=_cp("arbitrary"),
    )(qkv, qkv, qkv, o, do, lse, sinks)


def qkv_bwd(dq, dkv, x, g, wt, dres, tm):
    s = x.shape[0]
    qd = N_HEADS * HD
    kvw = 2 * N_KV * HD

    def body(dq_ref, dkv_ref, x_ref, g_ref, wt_ref, dres_ref, dx_ref, dg_ref, db_ref, dkvb_ref):
        @pl.when(pl.program_id(0) == 0)
        def _():
            dg_ref[...] = jnp.zeros_like(dg_ref)
            db_ref[...] = jnp.zeros_like(db_ref)

        dq = dq_ref[...]
        dkv = dkv_ref[...]
        dkvb = dkv.astype(bf16)
        dkvb_ref[...] = dkvb
        dh = _dot(dq, wt_ref[0:qd, :]) + _dot(dkvb, wt_ref[qd:QKV, :])
        g = g_ref[...]
        _, xn, r = _rms_fwd(x_ref[...], g)
        dx, dg = _rms_bwd(xn, r, g, dh)
        dx_ref[...] = dres_ref[...] + dx
        dg_ref[...] += dg
        db_ref[:, 0:qd] += _colsum8(dq.astype(f32))
        db_ref[:, qd:QKV] += _colsum8(dkv)

    return pl.pallas_call(
        body, grid=(s // tm,), name="qkv_bwd",
        in_specs=[_row(tm, qd), _row(tm, kvw), _row(tm, D), _const((1, D)), _const((QKV, D)), _row(tm, D)],
        out_specs=[_row(tm, D), _acc((SUB, D)), _acc((SUB, QKV)), _row(tm, kvw)],
        out_shape=[SDS((s, D), f32), SDS((SUB, D), f32), SDS((SUB, QKV), f32), SDS((s, kvw), bf16)],
        compiler_params=_cp("arbitrary"),
    )(dq, dkv, x, g, wt, dres)


def ffn_fwd(x, g, wup, wdw, bdw, wdn, tm, cw, rider=None, head=None):
    s = x.shape[0]
    tail = 8

    def body(x_ref, g_ref, wup_ref, wdw_ref, bdw_ref, wdn_ref, *refs):
        if head is None:
            xo_ref, h_ref, up_ref, gate_ref, act_ref, carry_ref, ext_ref = refs
        else:
            gf_ref, t_ref, xo_ref, h_ref, up_ref, gate_ref, act_ref, dgf_ref, loss_ref, carry_ref, ext_ref = refs

        @pl.when(pl.program_id(0) == 0)
        def _():
            carry_ref[...] = jnp.zeros_like(carry_ref)
            if head is not None:
                dgf_ref[...] = jnp.zeros_like(dgf_ref)
                loss_ref[...] = jnp.zeros_like(loss_ref)

        x = x_ref[...]
        h, _, _ = _rms_fwd(x, g_ref[...])
        hb = h.astype(bf16)
        h_ref[...] = hb
        for c in range(DFF // cw):
            cs = slice(c * cw, (c + 1) * cw)
            vs = slice(DFF + c * cw, DFF + (c + 1) * cw)
            ug = _dot(hb, wup_ref[:, cs])
            uv = _dot(hb, wup_ref[:, vs])
            up_ref[:, cs] = ug.astype(bf16)
            up_ref[:, vs] = uv.astype(bf16)
            ext_ref[0:tail, :] = carry_ref[:, cs]
            ext_ref[tail:tail + tm, :] = ug
            carry_ref[:, cs] = ug[tm - tail:, :]
            ext = ext_ref[...]
            gate = (wdw_ref[0:1, cs] * pltpu.roll(ext, 2, 0)[tail:] + wdw_ref[1:2, cs] * pltpu.roll(ext, 1, 0)[tail:]
                    + wdw_ref[2:3, cs] * ug) + bdw_ref[:, cs]
            gate_ref[:, cs] = gate.astype(bf16)
            act_ref[:, cs] = (gate * _sigmoid(gate) * uv).astype(bf16)
        xo = x + _dot(act_ref[...], wdn_ref[...])
        if head is None:
            xo_ref[...] = xo
        else:
            gf = gf_ref[...]
            y, xn, r = _rms_fwd(xo, gf)
            e = y - t_ref[...]
            loss_ref[...] += 0.5 * jnp.sum(jnp.mean(e * e, axis=-1, keepdims=True), axis=0, keepdims=True)
            dx, dgf = _rms_bwd(xn, r, gf, e / D)
            xo_ref[...] = dx
            dgf_ref[...] += dgf

    extra_in = [] if head is None else [_const((1, D)), _row(tm, D)]
    extra_out = [] if head is None else [_acc((SUB, D)), _acc((1, 1))]
    extra_shape = [] if head is None else [SDS((SUB, D), f32), SDS((1, 1), f32)]
    return _hosted(
        rider, body, grid=(s // tm,), name="ffn_fwd" if head is None else "ffn_fwd_loss",
        in_specs=[_row(tm, D), _const((1, D)), _const((D, 2 * DFF)), _const((3, DFF)), _const((1, DFF)),
                  _const((DFF, D))] + extra_in,
        out_specs=[_row(tm, D), _row(tm, D), _row(tm, 2 * DFF), _row(tm, DFF), _row(tm, DFF)] + extra_out,
        out_shape=[SDS((s, D), f32), SDS((s, D), bf16), SDS((s, 2 * DFF), bf16), SDS((s, DFF), bf16),
                   SDS((s, DFF), bf16)] + extra_shape,
        scratch_shapes=[pltpu.VMEM((tail, DFF), f32), pltpu.VMEM((tail + tm, cw), f32)],
        compiler_params=_cp("arbitrary"),
    )(x, g, wup, wdw, bdw, wdn, *(head or ()))


def ffn_bwd(dxo, x, g, up, gate, wdw, wdn, wup, tm, cw, rider=None):
    s = x.shape[0]
    nt = s // tm
    rev = lambda i: (nt - 1 - i, 0)

    def body(dxo_ref, x_ref, g_ref, up_ref, gate_ref, wdw_ref, wdn_ref, wup_ref,
             dxi_ref, dup_ref, dg_ref, dwdw_ref, dbdw_ref, carry_ref, ext2_ref):
        i = pl.program_id(0)

        @pl.when(i == 0)
        def _():
            carry_ref[...] = jnp.zeros_like(carry_ref)
            dg_ref[...] = jnp.zeros_like(dg_ref)
            dwdw_ref[...] = jnp.zeros_like(dwdw_ref)
            dbdw_ref[...] = jnp.zeros_like(dbdw_ref)

        dxo = dxo_ref[...]
        dxb = dxo.astype(bf16)
        for c in range(DFF // cw):
            cs = slice(c * cw, (c + 1) * cw)
            vs = slice(DFF + c * cw, DFF + (c + 1) * cw)
            d_act = lax.dot_general(dxb, wdn_ref[cs, :], _NT, preferred_element_type=f32)
            ug = up_ref[:, cs].astype(f32)
            uv = up_ref[:, vs].astype(f32)
            gate = gate_ref[:, cs].astype(f32)
            sg = _sigmoid(gate)
            dup_ref[:, vs] = (d_act * (gate * sg)).astype(bf16)
            d_gate = d_act * uv * _dsilu(gate, sg)
            ext2_ref[0:tm, :] = d_gate
            ext2_ref[tm:tm + 8, :] = carry_ref[:, cs]
            carry_ref[:, cs] = d_gate[0:8, :]
            ext = ext2_ref[...]
            ahead1 = pltpu.roll(ext, tm + 8 - 1, 0)[:tm]
            ahead2 = pltpu.roll(ext, tm + 8 - 2, 0)[:tm]
            dbdw_ref[:, cs] += _colsum8(d_gate)
            dwdw_ref[0, :, cs] += _colsum8(ahead2 * ug)
            dwdw_ref[1, :, cs] += _colsum8(ahead1 * ug)
            dwdw_ref[2, :, cs] += _colsum8(d_gate * ug)
            d_ug = wdw_ref[0:1, cs] * ahead2 + wdw_ref[1:2, cs] * ahead1 + wdw_ref[2:3, cs] * d_gate
            dup_ref[:, cs] = d_ug.astype(bf16)
        dh = lax.dot_general(dup_ref[...], wup_ref[...], _NT, preferred_element_type=f32)
        gv = g_ref[...]
        _, xn, r = _rms_fwd(x_ref[...], gv)
        dx, dg = _rms_bwd(xn, r, gv, dh)
        dxi_ref[...] = dxo + dx
        dg_ref[...] += dg

    return _hosted(
        rider, body, grid=(nt,), name="ffn_bwd",
        in_specs=[
            pl.BlockSpec((tm, D), rev), pl.BlockSpec((tm, D), rev), _const((1, D)),
            pl.BlockSpec((tm, 2 * DFF), rev), pl.BlockSpec((tm, DFF), rev),
            _const((3, DFF)), _const((DFF, D)), _const((D, 2 * DFF)),
        ],
        out_specs=[pl.BlockSpec((tm, D), rev), pl.BlockSpec((tm, 2 * DFF), rev), _acc((SUB, D)),
                   _acc((3, SUB, DFF)), _acc((SUB, DFF))],
        out_shape=[SDS((s, D), f32), SDS((s, 2 * DFF), bf16), SDS((SUB, D), f32), SDS((3, SUB, DFF), f32),
                   SDS((SUB, DFF), f32)],
        scratch_shapes=[pltpu.VMEM((8, DFF), f32), pltpu.VMEM((tm + 8, cw), f32)],
        compiler_params=_cp("arbitrary"),
    )(dxo, x, g, up, gate, wdw, wdn, wup)


def pw1_fwd(x, g, w, b, tm):
    s = x.shape[0]

    def body(x_ref, g_ref, w_ref, b_ref, h_ref, a_ref, u_ref):
        h, _, _ = _rms_fwd(x_ref[...], g_ref[...])
        hb = h.astype(bf16)
        h_ref[...] = hb
        a = _dot(hb, w_ref[...]) + b_ref[...]
        a_ref[...] = a.astype(bf16)
        u_ref[...] = a[:, :D] * _sigmoid(a[:, D:])

    return pl.pallas_call(
        body, grid=(s // tm,), name="pw1_fwd",
        in_specs=[_row(tm, D), _const((1, D)), _const((D, 2 * D)), _const((1, 2 * D))],
        out_specs=[_row(tm, D), _row(tm, 2 * D), _row(tm, D)],
        out_shape=[SDS((s, D), bf16), SDS((s, 2 * D), bf16), SDS((s, D), f32)],
        compiler_params=_cp("parallel"),
    )(x, g, w, b)


def _ln_silu(c, lg, lb):
    mu = jnp.mean(c, axis=-1, keepdims=True)
    cc = c - mu
    var = jnp.mean(cc * cc, axis=-1, keepdims=True)
    rstd = lax.rsqrt(var + LN_EPS)
    xh = cc * rstd
    ln = xh * lg + lb
    sg = _sigmoid(ln)
    return xh, rstd, ln, sg


def _shifted_copies(ext_ref, sh_ref, cs, tm):
    n = CONV_HALO - SUB + tm
    for k in range(1, SUB):
        sh_ref[k - 1] = ext_ref[pl.ds(k, n), cs]


def _shifted_rows(ext_ref, sh_ref, cs, start, rows):
    q, k = divmod(start, SUB)
    if k == 0:
        return ext_ref[pl.ds(start, rows), cs]
    return sh_ref[k - 1, pl.ds(q * SUB, rows), :]


def conv_fwd(u, x, wdw, bdw, lg, lb, w2, b2, tm, rc):
    s = x.shape[0]
    hl = CONV_HALO
    off = hl - (CONV_W - 1)

    def body(u_ref, halo_ref, x_ref, wdw_ref, bdw_ref, lg_ref, lb_ref, w2_ref, b2_ref, c_ref, xo_ref, ext_ref, sh_ref):
        has_prev = (pl.program_id(0) > 0).astype(f32)
        ext_ref[0:hl, :] = halo_ref[...] * has_prev
        ext_ref[hl:hl + tm, :] = u_ref[...]
        for cc in range(D // LANES):
            cs = slice(cc * LANES, (cc + 1) * LANES)
            _shifted_copies(ext_ref, sh_ref, cs, tm)
            for rr in range(tm // rc):
                acc = jnp.zeros((rc, LANES), f32) + bdw_ref[:, cs]
                for j in range(CONV_W):
                    acc = acc + wdw_ref[j:j + 1, cs] * _shifted_rows(ext_ref, sh_ref, cs, rr * rc + off + j, rc)
                c_ref[rr * rc:(rr + 1) * rc, cs] = acc
        _, _, ln, sg = _ln_silu(c_ref[...], lg_ref[...], lb_ref[...])
        xo_ref[...] = x_ref[...] + _dot((ln * sg).astype(bf16), w2_ref[...]) + b2_ref[...]

    return pl.pallas_call(
        body, grid=(s // tm,), name="conv_fwd",
        in_specs=[_row(tm, D), pl.BlockSpec((hl, D), lambda i: (jnp.maximum(i * (tm // hl) - 1, 0), 0)), _row(tm, D),
                  _const((CONV_W, D)), _const((1, D)), _const((1, D)), _const((1, D)), _const((D, D)), _const((1, D))],
        out_specs=[_row(tm, D), _row(tm, D)],
        out_shape=[SDS((s, D), f32), SDS((s, D), f32)],
        scratch_shapes=[pltpu.VMEM((hl + tm, D), f32), pltpu.VMEM((SUB - 1, hl - SUB + tm, LANES), f32)],
        compiler_params=_cp("parallel"),
    )(u, u, x, wdw, bdw, lg, lb, w2, b2)


def conv_bwd_a(dy, c, lg, lb, w2t, tm, rider=None):
    s = dy.shape[0]

    def body(dy_ref, c_ref, lg_ref, lb_ref, w2t_ref, dc_ref, u3_ref, dlg_ref, dlb_ref, db2_ref, dbdw_ref):
        @pl.when(pl.program_id(0) == 0)
        def _():
            for r in (dlg_ref, dlb_ref, db2_ref, dbdw_ref):
                r[...] = jnp.zeros_like(r)

        dy = dy_ref[...]
        lg = lg_ref[...]
        xh, rstd, ln, sg = _ln_silu(c_ref[...], lg, lb_ref[...])
        u3_ref[...] = (ln * sg).astype(bf16)
        du3 = _dot(dy.astype(bf16), w2t_ref[...])
        dln = du3 * _dsilu(ln, sg)
        dxh = dln * lg
        dc = rstd * (dxh - jnp.mean(dxh, axis=-1, keepdims=True) - xh * jnp.mean(dxh * xh, axis=-1, keepdims=True))
        dc_ref[...] = dc
        dlg_ref[...] += _colsum8(dln * xh)
        dlb_ref[...] += _colsum8(dln)
        db2_ref[...] += _colsum8(dy)
        dbdw_ref[...] += _colsum8(dc)

    return _hosted(
        rider, body, grid=(s // tm,), name="conv_bwd_a",
        in_specs=[_row(tm, D), _row(tm, D), _const((1, D)), _const((1, D)), _const((D, D))],
        out_specs=[_row(tm, D), _row(tm, D)] + [_acc((SUB, D))] * 4,
        out_shape=[SDS((s, D), f32), SDS((s, D), bf16)] + [SDS((SUB, D), f32)] * 4,
        compiler_params=_cp("arbitrary"),
    )(dy, c, lg, lb, w2t)


def conv_bwd_b(dc, u, a, wdw, x, g, w1t, dres, tm, rc, rider=None):
    s = x.shape[0]
    nt = s // tm
    hl = CONV_HALO
    off = hl - (CONV_W - 1)

    def body(dc_ref, dnext_ref, u_ref, uprev_ref, a_ref, wdw_ref, x_ref, g_ref, w1t_ref, dres_ref,
             dx_ref, da_ref, dwdw_ref, db1_ref, dg_ref, ext_ref, ext2_ref, du_ref, sh_ref, sh2_ref):
        i = pl.program_id(0)

        @pl.when(i == 0)
        def _():
            for r in (dwdw_ref, db1_ref, dg_ref):
                r[...] = jnp.zeros_like(r)

        ext_ref[0:hl, :] = uprev_ref[...] * (i > 0).astype(f32)
        ext_ref[hl:hl + tm, :] = u_ref[...]
        ext2_ref[0:tm, :] = dc_ref[...]
        ext2_ref[tm:tm + hl, :] = dnext_ref[...] * (i < nt - 1).astype(f32)
        for cc in range(D // LANES):
            cs = slice(cc * LANES, (cc + 1) * LANES)
            _shifted_copies(ext_ref, sh_ref, cs, tm)
            _shifted_copies(ext2_ref, sh2_ref, cs, tm)
            for rr in range(tm // rc):
                r0 = rr * rc
                dcb = ext2_ref[r0:r0 + rc, cs]
                acc = jnp.zeros((rc, LANES), f32)
                for j in range(CONV_W):
                    acc = acc + wdw_ref[j:j + 1, cs] * _shifted_rows(ext2_ref, sh2_ref, cs, r0 + CONV_W - 1 - j, rc)
                    dwdw_ref[j, :, cs] += _colsum8(dcb * _shifted_rows(ext_ref, sh_ref, cs, r0 + off + j, rc))
                du_ref[r0:r0 + rc, cs] = acc
        du = du_ref[...]
        a1 = a_ref[:, :D].astype(f32)
        sg = _sigmoid(a_ref[:, D:].astype(f32))
        da1 = du * sg
        da2 = du * a1 * sg * (1.0 - sg)
        da_ref[:, :D] = da1.astype(bf16)
        da_ref[:, D:] = da2.astype(bf16)
        db1_ref[:, :D] += _colsum8(da1)
        db1_ref[:, D:] += _colsum8(da2)
        dh = _dot(da_ref[...], w1t_ref[...])
        gv = g_ref[...]
        _, xn, r = _rms_fwd(x_ref[...], gv)
        dx, dg = _rms_bwd(xn, r, gv, dh)
        dx_ref[...] = dres_ref[...] + dx
        dg_ref[...] += dg

    blocks = tm // hl
    return _hosted(
        rider, body, grid=(nt,), name="conv_bwd_b",
        in_specs=[
            _row(tm, D), pl.BlockSpec((hl, D), lambda i: (jnp.minimum((i + 1) * blocks, s // hl - 1), 0)),
            _row(tm, D), pl.BlockSpec((hl, D), lambda i: (jnp.maximum(i * blocks - 1, 0), 0)),
            _row(tm, 2 * D), _const((CONV_W, D)), _row(tm, D), _const((1, D)), _const((2 * D, D)), _row(tm, D),
        ],
        out_specs=[_row(tm, D), _row(tm, 2 * D), _acc((CONV_W, SUB, D)), _acc((SUB, 2 * D)), _acc((SUB, D))],
        out_shape=[SDS((s, D), f32), SDS((s, 2 * D), bf16), SDS((CONV_W, SUB, D), f32), SDS((SUB, 2 * D), f32),
                   SDS((SUB, D), f32)],
        scratch_shapes=[pltpu.VMEM((hl + tm, D), f32), pltpu.VMEM((tm + hl, D), f32), pltpu.VMEM((tm, D), f32),
                        pltpu.VMEM((SUB - 1, hl - SUB + tm, LANES), f32),
                        pltpu.VMEM((SUB - 1, hl - SUB + tm, LANES), f32)],
        compiler_params=_cp("arbitrary"),
    )(dc, dc, u, u, a, wdw, x, g, w1t, dres)


def wgrad(a, b, nb, tk, name, rider=None):
    s, k1 = a.shape
    n = b.shape[1]

    def body(a_ref, b_ref, o_ref):
        @pl.when(pl.program_id(1) == 0)
        def _():
            o_ref[...] = jnp.zeros_like(o_ref)

        o_ref[...] += lax.dot_general(a_ref[...], b_ref[...].astype(bf16), _TN, preferred_element_type=f32)

    return _hosted(
        rider, body, grid=(n // nb, s // tk), name=name,
        in_specs=[pl.BlockSpec((tk, k1), lambda j, k: (k, 0)), pl.BlockSpec((tk, nb), lambda j, k: (k, j))],
        out_specs=pl.BlockSpec((k1, nb), lambda j, k: (0, j)),
        out_shape=SDS((k1, n), f32),
        compiler_params=_cp("parallel", "arbitrary"),
    )(a, b)


def wgrad_cols(a, b, tk, name, rider=None):
    s, k1 = a.shape
    w = b.shape[1] // N_CHIPS

    def body(a_ref, b_ref, o_ref):
        @pl.when(pl.program_id(1) == 0)
        def _():
            o_ref[...] = jnp.zeros_like(o_ref)

        acc = lax.dot_general(a_ref[...], b_ref[...].astype(bf16), _TN, preferred_element_type=f32)
        o_ref[:, 0] += acc.reshape(2, k1 // 2, w)

    return _hosted(
        rider, body, grid=(N_CHIPS, s // tk), name=name,
        in_specs=[pl.BlockSpec((tk, k1), lambda j, k: (k, 0)), pl.BlockSpec((tk, w), lambda j, k: (k, j))],
        out_specs=pl.BlockSpec((2, 1, k1 // 2, w), lambda j, k: (0, j, 0, 0)),
        out_shape=SDS((2, N_CHIPS, k1 // 2, w), f32),
        compiler_params=_cp("parallel", "arbitrary"),
    )(a, b)


def wgrad_rows(a, b, nb, tk, name, rider=None):
    s, k1 = a.shape
    n = b.shape[1]
    r = k1 // (2 * N_CHIPS)

    def body(a_ref, b_ref, o_ref):
        @pl.when(pl.program_id(1) == 0)
        def _():
            o_ref[...] = jnp.zeros_like(o_ref)

        acc = lax.dot_general(a_ref[...], b_ref[...].astype(bf16), _TN, preferred_element_type=f32)
        for j in range(N_CHIPS):
            for h in range(2):
                o_ref[h, j] += acc[(2 * j + h) * r:(2 * j + h + 1) * r, :]

    return _hosted(
        rider, body, grid=(n // nb, s // tk), name=name,
        in_specs=[pl.BlockSpec((tk, k1), lambda j, k: (k, 0)), pl.BlockSpec((tk, nb), lambda j, k: (k, j))],
        out_specs=pl.BlockSpec((2, N_CHIPS, r, nb), lambda j, k: (0, 0, 0, j)),
        out_shape=SDS((2, N_CHIPS, r, n), f32),
        compiler_params=_cp("parallel", "arbitrary"),
    )(a, b)


def _adam_math(w, g, m, v):
    m = B1 * m + (1.0 - B1) * g
    v = B2 * v + (1.0 - B2) * (g * g)
    m_hat = m / (1.0 - B1 ** STEP)
    v_hat = v / (1.0 - B2 ** STEP)
    delta = -LR * (m_hat / (jnp.sqrt(v_hat) + ADAM_EPS) + WD * w)
    return delta, m, v


def _rows_tile(r, c, multiple=SUB):
    best = None
    for t in range(multiple, r + 1, multiple):
        if r % t == 0 and t * c * 4 <= ELEMENTWISE_BLOCK_BYTES:
            best = t
    return best if best is not None else r


def adamw(w, g, m, v, name):
    l, r, c = w.shape
    tr = _rows_tile(r, c)
    spec = pl.BlockSpec((1, tr, c), lambda i, j: (i, j, 0))

    def body(w_ref, g_ref, m_ref, v_ref, d_ref, mo_ref, vo_ref):
        d, mn, vn = _adam_math(w_ref[...], g_ref[...], m_ref[...], v_ref[...])
        d_ref[...] = d
        mo_ref[...] = mn
        vo_ref[...] = vn

    return pl.pallas_call(
        body, grid=(l, r // tr), name=name, in_specs=[spec] * 4, out_specs=[spec] * 3,
        out_shape=[SDS((l, r, c), f32)] * 3, compiler_params=_cp("parallel", "parallel"),
    )(w, g, m, v)


def _place():
    return lax.axis_index("x"), lax.axis_index("y"), lax.axis_index("c")


def _chip_peer(xi, yi, r):
    px = 1 - xi if r & 2 else xi
    py = 1 - yi if r & 1 else yi
    return px, py


def _gv_qkv(src, dst, j, h):
    rows = pl.ds(h * (D // 2), D // 2)
    return src.at[rows, :], dst.at[j, rows, :]


def _gv_rows(src, dst, j, h):
    r = src.shape[0] // 2
    return src.at[pl.ds(h * r, r), :], dst.at[pl.ds(j * 2 * r + h * r, r), :]


def _gv_cols(src, dst, j, h):
    r, w = src.shape[0] // 2, src.shape[1]
    return src.at[pl.ds(h * r, r), :], dst.at[pl.ds(h * r, r), pl.ds(j * w, w)]


def gather_rider(big, small, forward_at):
    nb, ns = len(big), len(small)
    n = nb + ns
    views = [v for _, v, _ in big]

    def env(ins, outs, sems):
        ici_send, ici_recv, d2d_send, d2d_recv, loc_sems = sems
        xi, yi, ci = _place()
        me = 2 * xi + yi

        def local(a, h):
            if a < nb:
                src, dst = views[a](ins[a], outs[a], me, h)
                return pltpu.make_async_copy(src, dst, loc_sems.at[2 * a + h])
            return pltpu.make_async_copy(ins[a], outs[a].at[me], loc_sems.at[nb + a])

        def ici(a, r, slot):
            px, py = _chip_peer(xi, yi, r)
            src, dst = views[a](ins[a], outs[a], slot, ci) if a < nb else (ins[a], outs[a].at[slot])
            k = 3 * a + r - 1
            return pltpu.make_async_remote_copy(src_ref=src, dst_ref=dst, send_sem=ici_send.at[k],
                                                recv_sem=ici_recv.at[k], device_id=(px, py, ci), device_id_type=MESH)

        def d2d(a, r, half):
            px, py = _chip_peer(xi, yi, r)
            _, dst = views[a](ins[a], outs[a], 2 * px + py, half)
            k = 3 * a + r - 1
            return pltpu.make_async_remote_copy(src_ref=dst, dst_ref=dst, send_sem=d2d_send.at[k],
                                                recv_sem=d2d_recv.at[k], device_id=(xi, yi, 1 - ci), device_id_type=MESH)

        return xi, yi, ci, me, local, ici, d2d

    def locals_of():
        return [(a, h) for a in range(nb) for h in range(2)] + [(a, 0) for a in range(nb, n)]

    def send(ins, outs, sems):
        _, _, _, me, local, ici, _ = env(ins, outs, sems)
        for a, h in locals_of():
            local(a, h).start()
        for a in range(n):
            for r in (1, 2, 3):
                ici(a, r, me).start()

    def forward(ins, outs, sems):
        xi, yi, ci, _, _, ici, d2d = env(ins, outs, sems)
        for a in range(n):
            for r in (1, 2, 3):
                px, py = _chip_peer(xi, yi, r)
                ici(a, r, 2 * px + py).wait_recv()
                if a < nb:
                    d2d(a, r, ci).start()

    def finish(ins, outs, sems):
        _, _, ci, me, local, ici, d2d = env(ins, outs, sems)
        for a in range(nb):
            for r in (1, 2, 3):
                d2d(a, r, 1 - ci).wait_recv()
        for a in range(n):
            for r in (1, 2, 3):
                ici(a, r, me).wait_send()
                if a < nb:
                    d2d(a, r, ci).wait_send()
        for a, h in locals_of():
            local(a, h).wait()

    dma = pltpu.SemaphoreType.DMA
    return Rider(
        ins=[b for b, _, _ in big] + list(small),
        out_shape=[SDS(shape, bf16) for _, _, shape in big] + [SDS((N_CHIPS,) + a.shape, a.dtype) for a in small],
        sem_shapes=[dma((3 * n,)), dma((3 * n,)), dma((max(3 * nb, 1),)), dma((max(3 * nb, 1),)), dma((2 * nb + ns,))],
        stages=[(0.0, send), (forward_at, forward)], final=finish)


def pair_send_rider(gs):
    n = len(gs)

    def copy(ins, outs, sems, a):
        xi, yi, ci = _place()
        return pltpu.make_async_remote_copy(
            src_ref=ins[a].at[1 - ci], dst_ref=outs[a], send_sem=sems[0].at[a], recv_sem=sems[1].at[a],
            device_id=(xi, yi, 1 - ci), device_id_type=MESH)

    def send(ins, outs, sems):
        for a in range(n):
            copy(ins, outs, sems, a).start()

    def finish(ins, outs, sems):
        for a in range(n):
            copy(ins, outs, sems, a).wait()

    dma = pltpu.SemaphoreType.DMA
    return Rider(ins=list(gs), out_shape=[SDS(g.shape[1:], g.dtype) for g in gs], sem_shapes=[dma((n,)), dma((n,))],
                 stages=[(0.0, send)], final=finish)


def chip_rider(pbs, p32s, forward_at):
    n = len(pbs)

    def env(ins, outs, sems):
        ici_send, ici_recv, d2d_send, d2d_recv, own_send, own_recv, loc_sems = sems
        pb, p32, recv, own = ins[:n], ins[n:], outs[:n], outs[n:]
        xi, yi, ci = _place()
        me = 2 * xi + yi
        sib = (xi, yi, 1 - ci)

        def ici(a, r, src_slot, dst_slot):
            px, py = _chip_peer(xi, yi, r)
            k = 3 * a + r - 1
            return pltpu.make_async_remote_copy(
                src_ref=pb[a].at[src_slot], dst_ref=recv[a].at[ci, dst_slot], send_sem=ici_send.at[k],
                recv_sem=ici_recv.at[k], device_id=(px, py, ci), device_id_type=MESH)

        def d2d(a, r, half):
            px, py = _chip_peer(xi, yi, r)
            blk = recv[a].at[half, 2 * px + py]
            k = 3 * a + r - 1
            return pltpu.make_async_remote_copy(src_ref=blk, dst_ref=blk, send_sem=d2d_send.at[k],
                                                recv_sem=d2d_recv.at[k], device_id=sib, device_id_type=MESH)

        def mine(a, half):
            return pltpu.make_async_remote_copy(src_ref=p32[a].at[me], dst_ref=own[a].at[half], send_sem=own_send.at[a],
                                                recv_sem=own_recv.at[a], device_id=sib, device_id_type=MESH)

        def local(a):
            return pltpu.make_async_copy(p32[a].at[me], own[a].at[ci], loc_sems.at[a])

        return xi, yi, ci, me, ici, d2d, mine, local

    def send(ins, outs, sems):
        xi, yi, ci, me, ici, _, mine, local = env(ins, outs, sems)
        for a in range(n):
            local(a).start()
            mine(a, ci).start()
            for r in (1, 2, 3):
                px, py = _chip_peer(xi, yi, r)
                ici(a, r, 2 * px + py, me).start()

    def forward(ins, outs, sems):
        xi, yi, ci, me, ici, d2d, _, _ = env(ins, outs, sems)
        for a in range(n):
            for r in (1, 2, 3):
                px, py = _chip_peer(xi, yi, r)
                ici(a, r, me, 2 * px + py).wait_recv()
                d2d(a, r, ci).start()

    def finish(ins, outs, sems):
        xi, yi, ci, me, ici, d2d, mine, local = env(ins, outs, sems)
        for a in range(n):
            mine(a, 1 - ci).wait_recv()
            for r in (1, 2, 3):
                d2d(a, r, 1 - ci).wait_recv()
        for a in range(n):
            mine(a, ci).wait_send()
            local(a).wait()
            for r in (1, 2, 3):
                px, py = _chip_peer(xi, yi, r)
                ici(a, r, 2 * px + py, me).wait_send()
                d2d(a, r, ci).wait_send()

    dma = pltpu.SemaphoreType.DMA
    return Rider(
        ins=list(pbs) + list(p32s),
        out_shape=[SDS((2,) + p.shape, bf16) for p in pbs] + [SDS((2,) + p.shape[1:], f32) for p in p32s],
        sem_shapes=[dma((3 * n,)), dma((3 * n,)), dma((3 * n,)), dma((3 * n,)), dma((n,)), dma((n,)), dma((n,))],
        stages=[(0.0, send), (forward_at, forward)], final=finish)


def rs_pair_add(g, other, ci, name):
    _, nsh, r, w = g.shape
    tr = _rows_tile(r, w, 16)

    def body(c_ref, g_ref, o_ref, p_ref, pb_ref):
        p = g_ref[0] + o_ref[...]
        p_ref[...] = p
        pb_ref[...] = p.astype(bf16)

    blk = pl.BlockSpec((1, tr, w), lambda j, i, c: (j, i, 0))
    return pl.pallas_call(
        body, name=name,
        grid_spec=pltpu.PrefetchScalarGridSpec(
            num_scalar_prefetch=1, grid=(nsh, r // tr),
            in_specs=[pl.BlockSpec((1, 1, tr, w), lambda j, i, c: (c[0], j, i, 0)), blk], out_specs=[blk, blk]),
        out_shape=[SDS((nsh, r, w), f32), SDS((nsh, r, w), bf16)],
        compiler_params=_cp("parallel", "parallel"),
    )(ci.reshape(1).astype(jnp.int32), g, other)


def rs_chip_add(own, recv, chip, name, layer=0, layers=1, into=None):
    _, nsh, r, w = recv.shape
    tr = _rows_tile(r, w, 16)

    def body(c_ref, own_ref, recv_ref, *refs):
        o_ref = refs[-1]
        me = c_ref[0]
        acc = None
        for j in range(N_CHIPS):
            term = jnp.where(me == j, own_ref[0], recv_ref[0, j].astype(f32))
            acc = term if acc is None else acc + term
        o_ref[0, 0] = acc

    args = (chip.reshape(1).astype(jnp.int32), own, recv) + (() if into is None else (into,))
    return pl.pallas_call(
        body, name=name,
        grid_spec=pltpu.PrefetchScalarGridSpec(
            num_scalar_prefetch=1, grid=(2, r // tr),
            in_specs=[pl.BlockSpec((1, tr, w), lambda h, i, c: (h, i, 0)),
                      pl.BlockSpec((1, nsh, tr, w), lambda h, i, c: (h, 0, i, 0))] + ([] if into is None else [_ANY]),
            out_specs=pl.BlockSpec((1, 1, tr, w), lambda h, i, c: (layer, h, i, 0))),
        out_shape=SDS((layers, 2, r, w), f32),
        input_output_aliases={} if into is None else {3: 0},
        compiler_params=_cp("parallel", "parallel"),
    )(*args)


class GradReduction:
    def __init__(self, keys, grads, ci, chip):
        self.keys, self.grads, self.ci, self.chip = keys, grads, ci, chip

    def pair_rider(self):
        return pair_send_rider(self.grads)

    def chip_rider(self, from_pair, forward_at):
        both = [rs_pair_add(g, o, self.ci, "rs_pair_add_" + k) for k, g, o in zip(self.keys, self.grads, from_pair)]
        return chip_rider([pb for _, pb in both], [p for p, _ in both], forward_at)

    def result(self, landed, place=None):
        n = len(self.keys)
        return {k: rs_chip_add(own, recv, self.chip, "rs_chip_add_" + k, *(place or {}).get(k, ()))
                for k, recv, own in zip(self.keys, landed[:n], landed[n:])}

    def alone(self):
        from_pair = run_rider(self.pair_rider(), "rs_pair_" + self.keys[0])
        return self.result(run_rider(self.chip_rider(from_pair, 0.0), "rs_chip_" + self.keys[0]))


STAGE_W = DFF
_ST = {"norm_mix0": 0, "norm_mix1": 1, "attn_b_qkv": 2, "attn_sinks": 3, "attn_b_o": 4, "final_norm": 5, "loss": 6,
       "norm_ffn0": 8, "norm_ffn1": 9, "ffn_b_dw0": 10, "ffn_b_dw1": 11, "conv_b_pw1": 12, "conv_b_dw": 13,
       "conv_ln_g": 14, "conv_ln_b": 15, "conv_b_pw2": 16, "ffn_w_dw0": 17, "ffn_w_dw1": 20, "conv_w_dw": 24}
STAGE_ROWS = 56
SMALL_REP = ("norm_mix", "attn_b_qkv", "attn_sinks", "attn_b_o", "norm_ffn", "ffn_b_dw", "final_norm")
SMALL_SH = ("conv_b_pw1", "conv_w_dw", "conv_b_dw", "conv_ln_g", "conv_ln_b", "conv_b_pw2", "ffn_w_dw")
_SMALL_PARTS = ("norm_mix0", "norm_mix1", "attn_b_qkv", "attn_sinks", "attn_b_o", "norm_ffn0", "norm_ffn1", "ffn_b_dw0",
                "ffn_b_dw1", "final_norm", "conv_b_pw1", "conv_b_dw", "conv_ln_g", "conv_ln_b", "conv_b_pw2", "loss",
                "ffn_w_dw0", "ffn_w_dw1", "conv_w_dw")


def small_reduce_adamw(parts, w, m, v):
    names = SMALL_REP + SMALL_SH
    npart, nw = len(_SMALL_PARTS), len(names)

    def body(*refs):
        part = dict(zip(_SMALL_PARTS, refs[:npart]))
        off = npart
        w_ref = dict(zip(names, refs[off:off + nw]))
        m_ref = dict(zip(names, refs[off + nw:off + 2 * nw]))
        v_ref = dict(zip(names, refs[off + 2 * nw:off + 3 * nw]))
        off += 3 * nw
        loss_ref = refs[off]
        g_out = dict(zip(names, refs[off + 1:off + 1 + nw]))
        d_out = dict(zip(names, refs[off + 1 + nw:off + 1 + 2 * nw]))
        m_out = dict(zip(names, refs[off + 1 + 2 * nw:off + 1 + 3 * nw]))
        v_out = dict(zip(names, refs[off + 1 + 3 * nw:off + 1 + 4 * nw]))
        stage_ref, buf_ref, tot_ref, send_sems, recv_sems = refs[off + 1 + 4 * nw:]

        xi, yi, ci = _place()
        me = 4 * xi + 2 * yi + ci
        chip = 2 * xi + yi

        stage_ref[...] = jnp.zeros_like(stage_ref)
        for name in _SMALL_PARTS:
            ref, r0 = part[name], _ST[name]
            if name in ("attn_sinks", "loss"):
                val = ref[...]
            elif name in ("ffn_w_dw0", "ffn_w_dw1", "conv_w_dw"):
                val = jnp.sum(ref[...], axis=1)
            else:
                val = jnp.sum(ref[...], axis=0, keepdims=True)
            stage_ref[r0:r0 + val.shape[0], 0:val.shape[1]] = val

        buf_ref[me] = stage_ref[...]

        def peer(r):
            px, py = _chip_peer(xi, yi, r >> 1)
            return px, py, (1 - ci if r & 1 else ci)

        def copy(r, slot):
            return pltpu.make_async_remote_copy(
                src_ref=stage_ref, dst_ref=buf_ref.at[slot], send_sem=send_sems.at[r - 1], recv_sem=recv_sems.at[r - 1],
                device_id=peer(r), device_id_type=MESH)

        sends = []
        for r in range(1, N_DEV):
            cp = copy(r, me)
            cp.start()
            sends.append(cp)
        for r in range(1, N_DEV):
            px, py, pc = peer(r)
            copy(r, 4 * px + 2 * py + pc).wait_recv()
        for cp in sends:
            cp.wait_send()
        acc = buf_ref[0]
        for d in range(1, N_DEV):
            acc = acc + buf_ref[d]
        tot_ref[...] = acc

        def rows(name, n, width):
            r0 = _ST[name]
            return tot_ref[r0:r0 + n, 0:width]

        def mine(name, n, width):
            r0 = _ST[name]
            out = tot_ref[r0:r0 + n, 0:width]
            for j in range(1, N_CHIPS):
                out = jnp.where(chip == j, tot_ref[r0:r0 + n, j * width:(j + 1) * width], out)
            return out

        loss_ref[...] = rows("loss", 1, 1)
        grads = {
            "norm_mix": rows("norm_mix0", 2, D), "attn_b_qkv": rows("attn_b_qkv", 1, QKV),
            "attn_sinks": rows("attn_sinks", 1, N_HEADS), "attn_b_o": rows("attn_b_o", 1, D),
            "norm_ffn": rows("norm_ffn0", 2, D), "ffn_b_dw": rows("ffn_b_dw0", 2, DFF),
            "final_norm": rows("final_norm", 1, D),
            "conv_b_pw1": mine("conv_b_pw1", 1, 2 * D // N_CHIPS), "conv_w_dw": mine("conv_w_dw", CONV_W, D // N_CHIPS),
            "conv_b_dw": mine("conv_b_dw", 1, D // N_CHIPS), "conv_ln_g": mine("conv_ln_g", 1, D // N_CHIPS),
            "conv_ln_b": mine("conv_ln_b", 1, D // N_CHIPS), "conv_b_pw2": mine("conv_b_pw2", 1, D // N_CHIPS),
        }
        for name in names:
            if name == "ffn_w_dw":
                continue
            at = 0 if name == "conv_w_dw" else Ellipsis
            g = grads[name]
            d, mn, vn = _adam_math(w_ref[name][at], g, m_ref[name][at], v_ref[name][at])
            g_out[name][at] = g
            d_out[name][at] = d
            m_out[name][at] = mn
            v_out[name][at] = vn
        for layer, key in enumerate(("ffn_w_dw0", "ffn_w_dw1")):
            g = mine(key, 3, DFF // N_CHIPS)
            d, mn, vn = _adam_math(w_ref["ffn_w_dw"][layer], g, m_ref["ffn_w_dw"][layer], v_ref["ffn_w_dw"][layer])
            g_out["ffn_w_dw"][layer] = g
            d_out["ffn_w_dw"][layer] = d
            m_out["ffn_w_dw"][layer] = mn
            v_out["ffn_w_dw"][layer] = vn

    ins = [parts[k] for k in _SMALL_PARTS] + [src[k] for src in (w, m, v) for k in names]
    wshapes = [SDS(w[k].shape, f32) for k in names]
    outs = pl.pallas_call(
        body, name="small_reduce_adamw", in_specs=[_VMEM] * len(ins), out_specs=[_VMEM] * (1 + 4 * nw),
        out_shape=[SDS((1, 1), f32)] + wshapes * 4,
        scratch_shapes=[pltpu.VMEM((STAGE_ROWS, STAGE_W), f32), pltpu.VMEM((N_DEV, STAGE_ROWS, STAGE_W), f32),
                        pltpu.VMEM((STAGE_ROWS, STAGE_W), f32), pltpu.SemaphoreType.DMA((N_DEV - 1,)),
                        pltpu.SemaphoreType.DMA((N_DEV - 1,))],
        compiler_params=pltpu.CompilerParams(vmem_limit_bytes=VMEM_LIMIT),
    )(*ins)
    loss = outs[0]
    g, d, mn, vn = (dict(zip(names, outs[1 + k * nw:1 + (k + 1) * nw])) for k in range(4))
    return loss, g, d, mn, vn


TM = 512
TM_FFN = 256
FFN_CHUNK = 256
CONV_ROWS = 128
CONV_BWD_ROWS = 64
TK = 2048
TK_LIGHT = 4096
FORWARD_AT = 0.6
FORWARD_AT_LATE = 0.85


def kernel(x, norm_mix, attn_w_qkv, attn_b_qkv, attn_sinks, attn_w_o, attn_b_o, conv_w_pw1, conv_b_pw1, conv_w_dw, conv_b_dw, conv_ln_g, conv_ln_b, conv_w_pw2, conv_b_pw2, norm_ffn, ffn_w_up, ffn_w_dw, ffn_b_dw, ffn_w_down, final_norm, loss_target, m_norm_mix, m_attn_w_qkv, m_attn_b_qkv, m_attn_sinks, m_attn_w_o, m_attn_b_o, m_conv_w_pw1, m_conv_b_pw1, m_conv_w_dw, m_conv_b_dw, m_conv_ln_g, m_conv_ln_b, m_conv_w_pw2, m_conv_b_pw2, m_norm_ffn, m_ffn_w_up, m_ffn_w_dw, m_ffn_b_dw, m_ffn_w_down, m_final_norm, v_norm_mix, v_attn_w_qkv, v_attn_b_qkv, v_attn_sinks, v_attn_w_o, v_attn_b_o, v_conv_w_pw1, v_conv_b_pw1, v_conv_w_dw, v_conv_b_dw, v_conv_ln_g, v_conv_ln_b, v_conv_w_pw2, v_conv_b_pw2, v_norm_ffn, v_ffn_w_up, v_ffn_w_dw, v_ffn_b_dw, v_ffn_w_down, v_final_norm):
    w = dict(norm_mix=norm_mix, attn_w_qkv=attn_w_qkv, attn_b_qkv=attn_b_qkv, attn_sinks=attn_sinks, attn_w_o=attn_w_o,
             attn_b_o=attn_b_o, conv_w_pw1=conv_w_pw1, conv_b_pw1=conv_b_pw1, conv_w_dw=conv_w_dw, conv_b_dw=conv_b_dw,
             conv_ln_g=conv_ln_g, conv_ln_b=conv_ln_b, conv_w_pw2=conv_w_pw2, conv_b_pw2=conv_b_pw2, norm_ffn=norm_ffn,
             ffn_w_up=ffn_w_up, ffn_w_dw=ffn_w_dw, ffn_b_dw=ffn_b_dw, ffn_w_down=ffn_w_down, final_norm=final_norm)
    mom = dict(norm_mix=m_norm_mix, attn_w_qkv=m_attn_w_qkv, attn_b_qkv=m_attn_b_qkv, attn_sinks=m_attn_sinks,
               attn_w_o=m_attn_w_o, attn_b_o=m_attn_b_o, conv_w_pw1=m_conv_w_pw1, conv_b_pw1=m_conv_b_pw1,
               conv_w_dw=m_conv_w_dw, conv_b_dw=m_conv_b_dw, conv_ln_g=m_conv_ln_g, conv_ln_b=m_conv_ln_b,
               conv_w_pw2=m_conv_w_pw2, conv_b_pw2=m_conv_b_pw2, norm_ffn=m_norm_ffn, ffn_w_up=m_ffn_w_up,
               ffn_w_dw=m_ffn_w_dw, ffn_b_dw=m_ffn_b_dw, ffn_w_down=m_ffn_w_down, final_norm=m_final_norm)
    vel = dict(norm_mix=v_norm_mix, attn_w_qkv=v_attn_w_qkv, attn_b_qkv=v_attn_b_qkv, attn_sinks=v_attn_sinks,
               attn_w_o=v_attn_w_o, attn_b_o=v_attn_b_o, conv_w_pw1=v_conv_w_pw1, conv_b_pw1=v_conv_b_pw1,
               conv_w_dw=v_conv_w_dw, conv_b_dw=v_conv_b_dw, conv_ln_g=v_conv_ln_g, conv_ln_b=v_conv_ln_b,
               conv_w_pw2=v_conv_w_pw2, conv_b_pw2=v_conv_b_pw2, norm_ffn=v_norm_ffn, ffn_w_up=v_ffn_w_up,
               ffn_w_dw=v_ffn_w_dw, ffn_b_dw=v_ffn_b_dw, ffn_w_down=v_ffn_w_down, final_norm=v_final_norm)
    order = ("norm_mix", "attn_w_qkv", "attn_b_qkv", "attn_sinks", "attn_w_o", "attn_b_o", "conv_w_pw1", "conv_b_pw1",
             "conv_w_dw", "conv_b_dw", "conv_ln_g", "conv_ln_b", "conv_w_pw2", "conv_b_pw2", "norm_ffn", "ffn_w_up",
             "ffn_w_dw", "ffn_b_dw", "ffn_w_down", "final_norm")
    xi, yi, ci = _place()
    chip = 2 * xi + yi
    xs, target = x[0], loss_target[0]
    s = xs.shape[0]
    tm, tmf, tk, tkl = min(TM, s), min(TM_FFN, s), min(TK, s), min(TK_LIGHT, s)
    row = lambda v: v.reshape(1, -1)
    join = lambda a, axis: jnp.concatenate([a[j] for j in range(N_CHIPS)], axis=axis)
    cast = lambda a: a.astype(bf16)
    small, big = {}, {}

    got = run_rider(gather_rider(
        [(cast(attn_w_qkv[0]), _gv_qkv, (N_CHIPS, D, QKV // N_CHIPS)), (cast(attn_w_o[0]), _gv_rows, (D, D))],
        [w[k] for k in SMALL_SH], 0.0), "gather_attn")
    qkv4, w_o = got[:2]
    sm = dict(zip(SMALL_SH, got[2:]))
    w_qkv = jnp.transpose(qkv4, (1, 0, 2)).reshape(D, QKV)
    sinks = attn_sinks.reshape(N_HEADS)
    b_pw1, conv_dw, conv_bdw = join(sm["conv_b_pw1"], 1), join(sm["conv_w_dw"], 2)[0], join(sm["conv_b_dw"], 1)
    ln_g, ln_b, b_pw2, ffn_dw = (join(sm["conv_ln_g"], 1), join(sm["conv_ln_b"], 1), join(sm["conv_b_pw2"], 1),
                                 join(sm["ffn_w_dw"], 2))

    h0, qkv = qkv_fwd(xs, row(norm_mix[0]), w_qkv, attn_b_qkv, tm)
    (o, lse), (w_up0, w_dn0) = attn_fwd(qkv, sinks, rider=gather_rider(
        [(cast(ffn_w_up[0]), _gv_cols, (D, 2 * DFF)), (cast(ffn_w_down[0]), _gv_rows, (DFF, D))], [], FORWARD_AT))
    x1 = attn_out_fwd(xs, o, w_o, attn_b_o, tm)
    (x2, h1, up0, gate0, act0), (w_pw1, w_pw2, w_up1, w_dn1) = ffn_fwd(
        x1, row(norm_ffn[0]), w_up0, ffn_dw[0], row(ffn_b_dw[0]), w_dn0, tmf, FFN_CHUNK, rider=gather_rider(
            [(cast(conv_w_pw1[0]), _gv_cols, (D, 2 * D)), (cast(conv_w_pw2[0]), _gv_rows, (D, D)),
             (cast(ffn_w_up[1]), _gv_cols, (D, 2 * DFF)), (cast(ffn_w_down[1]), _gv_rows, (DFF, D))], [],
            FORWARD_AT_LATE))
    h2, a, u = pw1_fwd(x2, row(norm_mix[1]), w_pw1, b_pw1, tm)
    c, x3 = conv_fwd(u, x2, conv_dw, conv_bdw, ln_g, ln_b, w_pw2, b_pw2, tm, CONV_ROWS)
    dx4, h3, up1, gate1, act1, small["final_norm"], small["loss"] = ffn_fwd(
        x3, row(norm_ffn[1]), w_up1, ffn_dw[1], row(ffn_b_dw[1]), w_dn1, tmf, FFN_CHUNK,
        head=(final_norm.reshape(1, D), target))

    dx3, dup1, small["norm_ffn1"], small["ffn_w_dw1"], small["ffn_b_dw1"] = ffn_bwd(
        dx4, x3, row(norm_ffn[1]), up1, gate1, ffn_dw[1], w_dn1, w_up1, tmf, FFN_CHUNK)
    red1 = GradReduction(("up1", "down1"), [wgrad_cols(h3, dup1, tk, "wgrad_up1"),
                                           wgrad_rows(act1, dx4, 512, tk, "wgrad_down1")], ci, chip)
    (dc, u3, small["conv_ln_g"], small["conv_ln_b"], small["conv_b_pw2"], small["conv_b_dw"]), from_pair = conv_bwd_a(
        dx3, c, ln_g, ln_b, w_pw2.T, tm, rider=red1.pair_rider())
    g_pw2 = wgrad_rows(u3, dx3, 512, tkl, "wgrad_pw2")
    (dx2, da, small["conv_w_dw"], small["conv_b_pw1"], small["norm_mix1"]), landed = conv_bwd_b(
        dc, u, a, conv_dw, x2, row(norm_mix[1]), w_pw1.T, dx3, tm, CONV_BWD_ROWS,
        rider=red1.chip_rider(from_pair, FORWARD_AT))
    big.update(red1.result(landed, {"up1": (1, 2, None), "down1": (1, 2, None)}))
    g_pw1 = wgrad_cols(h2, da, tkl, "wgrad_pw1")

    red2 = GradReduction(("pw1", "pw2"), [g_pw1, g_pw2], ci, chip)
    (dx1, dup0, small["norm_ffn0"], small["ffn_w_dw0"], small["ffn_b_dw0"]), from_pair = ffn_bwd(
        dx2, x1, row(norm_ffn[0]), up0, gate0, ffn_dw[0], w_dn0, w_up0, tmf, FFN_CHUNK, rider=red2.pair_rider())
    g_up0, landed = wgrad_cols(h1, dup0, tk, "wgrad_up0", rider=red2.chip_rider(from_pair, FORWARD_AT))
    big.update(red2.result(landed))
    g_dn0 = wgrad_rows(act0, dx2, 512, tk, "wgrad_down0")

    red3 = GradReduction(("up0", "down0", "wo"), [g_up0, g_dn0, wgrad_rows(o, dx1, 512, tkl, "wgrad_o")], ci, chip)
    (do, small["attn_b_o"]), from_pair = attn_out_bwd(dx1, w_o.T, tm, rider=red3.pair_rider())
    (dq, dkv, small["attn_sinks"]), landed = attn_bwd(qkv, o, do, lse, sinks,
                                                      rider=red3.chip_rider(from_pair, FORWARD_AT))
    both = red3.result(landed, {"up0": (0, 2, big["up1"]), "down0": (0, 2, big["down1"])})
    big.update({"up": both.pop("up0"), "down": both.pop("down0"), **both})
    dx0, small["norm_mix0"], small["attn_b_qkv"], dkvb = qkv_bwd(dq, dkv, xs, row(norm_mix[0]), w_qkv.T, dx1, tm)
    g_qkv = jnp.concatenate([wgrad(h0, dq, 512, tkl, "wgrad_q"), wgrad(h0, dkvb, 2 * N_KV * HD, tkl, "wgrad_kv")], axis=1)
    g_qkv = jnp.transpose(g_qkv.reshape(2, D // 2, N_CHIPS, QKV // N_CHIPS), (0, 2, 1, 3))
    red4 = GradReduction(("qkv",), [g_qkv], ci, chip)
    big.update(red4.alone())

    gbig = {
        "attn_w_qkv": big["qkv"].reshape(1, D, QKV // N_CHIPS), "attn_w_o": big["wo"].reshape(1, D // N_CHIPS, D),
        "conv_w_pw1": big["pw1"].reshape(1, D, 2 * D // N_CHIPS), "conv_w_pw2": big["pw2"].reshape(1, D // N_CHIPS, D),
        "ffn_w_up": big["up"].reshape(2, D, 2 * DFF // N_CHIPS),
        "ffn_w_down": big["down"].reshape(2, DFF // N_CHIPS, D),
    }

    fix = lambda d: {**d, "final_norm": d["final_norm"].reshape(1, D)}
    loss, gs, ds, ms, vs = small_reduce_adamw(small, fix(w), fix(mom), fix(vel))
    unfix = lambda d: {**d, "final_norm": d["final_norm"].reshape(D)}
    gout, delta, new_m, new_v = unfix(gs), unfix(ds), unfix(ms), unfix(vs)

    for name, g in gbig.items():
        gout[name] = g
        delta[name], new_m[name], new_v[name] = adamw(w[name], g, mom[name], vel[name], "adamw_" + name)

    return (loss.reshape(()), dx0[None], *[gout[n] for n in order], *[delta[n] for n in order],
            *[new_m[n] for n in order], *[new_v[n] for n in order])
```

```python
import math

import jax
import jax.numpy as jnp
from jax import lax
from jax.experimental import pallas as pl
from jax.experimental.pallas import tpu as pltpu

f32 = jnp.float32
bf16 = jnp.bfloat16
SDS = jax.ShapeDtypeStruct
MESH = pl.DeviceIdType.MESH

D = 1024
N_HEADS = 16
N_KV = 2
GROUP = 8
HD = 64
BLK = 128
QKV = (N_HEADS + 2 * N_KV) * HD
KV_COL_BLOCK = (N_HEADS * HD) // (2 * N_KV * HD)
CONV_W = 31
CONV_HALO = 32
DFF = 2816
RMS_EPS = 1e-6
LN_EPS = 1e-5
LR, B1, B2, ADAM_EPS, WD, STEP = 0.001, 0.9, 0.999, 1e-08, 0.01, 10

N_CHIPS = 4
N_DEV = 8
VMEM_LIMIT = 56 * 1024 * 1024
LANES = 128
SUB = 8
ELEMENTWISE_BLOCK_BYTES = 1 << 20


def _cp(*sem):
    return pltpu.CompilerParams(dimension_semantics=sem, vmem_limit_bytes=VMEM_LIMIT)


def _row(tm, n):
    return pl.BlockSpec((tm, n), lambda i: (i, 0))


def _const(shape):
    return pl.BlockSpec(shape, lambda *_: (0,) * len(shape), pipeline_mode=pl.Buffered(1))


def _acc(shape):
    return pl.BlockSpec(shape, lambda *_: (0,) * len(shape))


def _rms_fwd(x, g):
    r = lax.rsqrt(jnp.mean(x * x, axis=-1, keepdims=True) + RMS_EPS)
    xn = x * r
    return xn * g, xn, r


def _colsum8(v):
    return jnp.sum(v.reshape(v.shape[0] // SUB, SUB, v.shape[1]), axis=0)


def _rms_bwd(xn, r, g, dh):
    dyn = dh * g
    dx = r * (dyn - xn * jnp.mean(dyn * xn, axis=-1, keepdims=True))
    return dx, _colsum8(dh * xn)


def _sigmoid(z):
    return 0.5 * jnp.tanh(0.5 * z) + 0.5


def _dsilu(z, sg):
    return sg * (1.0 + z * (1.0 - sg))


def _dot(a, b):
    return jnp.dot(a, b, preferred_element_type=f32)


_ANY = pl.BlockSpec(memory_space=pl.ANY)
_VMEM = pl.BlockSpec(memory_space=pltpu.VMEM)


class Rider:
    def __init__(self, ins, out_shape, sem_shapes, stages, final):
        self.ins, self.out_shape, self.sem_shapes, self.stages, self.final = ins, out_shape, sem_shapes, stages, final


def run_rider(rider, name):
    n_in, n_out = len(rider.ins), len(rider.out_shape)

    def body(*refs):
        parts = refs[:n_in], refs[n_in:n_in + n_out], refs[n_in + n_out:]
        for _, fn in rider.stages:
            fn(*parts)
        rider.final(*parts)

    return pl.pallas_call(
        body, name=name, in_specs=[_ANY] * n_in, out_specs=[_ANY] * n_out, out_shape=list(rider.out_shape),
        scratch_shapes=list(rider.sem_shapes),
    )(*rider.ins)


def _hosted(rider, body, *, grid, in_specs, out_specs, out_shape, name, compiler_params, scratch_shapes=()):
    if rider is None:
        return pl.pallas_call(body, grid=grid, in_specs=in_specs, out_specs=out_specs, out_shape=out_shape, name=name,
                              compiler_params=compiler_params, scratch_shapes=list(scratch_shapes))
    single = not isinstance(out_shape, (list, tuple))
    shapes = [out_shape] if single else list(out_shape)
    specs = [out_specs] if single else list(out_specs)
    n_in, n_out, n_sc = len(in_specs), len(shapes), len(scratch_shapes)
    r_in, r_out = len(rider.ins), len(rider.out_shape)
    total = math.prod(grid)

    def wrapped(*refs):
        own_in, refs = refs[:n_in], refs[n_in:]
        r_ins, refs = refs[:r_in], refs[r_in:]
        own_out, refs = refs[:n_out], refs[n_out:]
        r_outs, refs = refs[:r_out], refs[r_out:]
        own_sc, r_sems = refs[:n_sc], refs[n_sc:]
        step = 0
        for d, n in enumerate(grid):
            step = step * n + pl.program_id(d)
        for frac, fn in rider.stages:
            @pl.when(step == min(int(frac * total), total - 1))
            def _(fn=fn):
                fn(r_ins, r_outs, r_sems)

        body(*own_in, *own_out, *own_sc)

        @pl.when(step == total - 1)
        def _():
            rider.final(r_ins, r_outs, r_sems)

    call = pl.pallas_call(
        wrapped, grid=grid, in_specs=list(in_specs) + [_ANY] * r_in, out_specs=specs + [_ANY] * r_out,
        out_shape=shapes + list(rider.out_shape), scratch_shapes=list(scratch_shapes) + list(rider.sem_shapes),
        name=name, compiler_params=_cp(*(("arbitrary",) * len(grid))))

    def run(*args):
        res = call(*args, *rider.ins)
        own = res[:n_out]
        return (own[0] if single else own), res[n_out:]

    return run


def qkv_fwd(x, g, w, b, tm):
    s = x.shape[0]

    def body(x_ref, g_ref, w_ref, b_ref, h_ref, o_ref):
        h, _, _ = _rms_fwd(x_ref[...], g_ref[...])
        hb = h.astype(bf16)
        h_ref[...] = hb
        o_ref[...] = (_dot(hb, w_ref[...]) + b_ref[...]).astype(bf16)

    return pl.pallas_call(
        body, grid=(s // tm,), name="qkv_fwd",
        in_specs=[_row(tm, D), _const((1, D)), _const((D, QKV)), _const((1, QKV))],
        out_specs=[_row(tm, D), _row(tm, QKV)],
        out_shape=[SDS((s, D), bf16), SDS((s, QKV), bf16)],
        compiler_params=_cp("parallel"),
    )(x, g, w, b)


def _band_mask(i):
    qi = lax.broadcasted_iota(jnp.int32, (GROUP * BLK, 2 * BLK), 0) & (BLK - 1)
    ki = lax.broadcasted_iota(jnp.int32, (GROUP * BLK, 2 * BLK), 1)
    dist = qi + BLK - ki
    return (dist >= 0) & (dist < BLK) & ((ki >= BLK) | (i > 0))


_NEG = float(jnp.finfo(jnp.float32).min)
_NT = (((1,), (1,)), ((), ()))
_TN = (((0,), (0,)), ((), ()))


def _kv_heads(kvp_ref, kvc_ref, kvh):
    ks = slice(kvh * HD, (kvh + 1) * HD)
    vs = slice(N_KV * HD + kvh * HD, N_KV * HD + (kvh + 1) * HD)
    k = jnp.concatenate([kvp_ref[:, ks], kvc_ref[:, ks]], axis=0)
    v = jnp.concatenate([kvp_ref[:, vs], kvc_ref[:, vs]], axis=0)
    return k, v


def _stack_group(ref, kvh, width=HD):
    return jnp.concatenate([ref[:, (kvh * GROUP + gi) * width:(kvh * GROUP + gi + 1) * width] for gi in range(GROUP)],
                           axis=0)


def _group_sinks(sink_ref, kvh):
    row = lax.broadcasted_iota(jnp.int32, (GROUP * BLK, 1), 0)
    col = jnp.zeros((GROUP * BLK, 1), f32)
    for g in range(GROUP):
        col = jnp.where((row >= g * BLK) & (row < (g + 1) * BLK), sink_ref[kvh * GROUP + g], col)
    return col


def attn_fwd(qkv, sinks, rider=None):
    s = qkv.shape[0]
    scale = 1.0 / math.sqrt(HD)

    def body(q_ref, kvc_ref, kvp_ref, sink_ref, o_ref, lse_ref):
        valid = _band_mask(pl.program_id(0))
        for kvh in range(N_KV):
            k, v = _kv_heads(kvp_ref, kvc_ref, kvh)
            sc = lax.dot_general(_stack_group(q_ref, kvh), k, _NT, preferred_element_type=f32) * scale
            sc = jnp.where(valid, sc, _NEG)
            sink = _group_sinks(sink_ref, kvh)
            m = jnp.maximum(jnp.max(sc, axis=-1, keepdims=True), sink)
            p = jnp.exp(sc - m)
            denom = jnp.sum(p, axis=-1, keepdims=True) + jnp.exp(sink - m)
            og = _dot((p / denom).astype(bf16), v).astype(bf16)
            lse = m + jnp.log(denom)
            for gi in range(GROUP):
                h = kvh * GROUP + gi
                o_ref[:, h * HD:(h + 1) * HD] = og[gi * BLK:(gi + 1) * BLK]
                lse_ref[:, h:h + 1] = lse[gi * BLK:(gi + 1) * BLK]

    return _hosted(
        rider, body, grid=(s // BLK,), name="attn_fwd",
        in_specs=[
            pl.BlockSpec((BLK, N_HEADS * HD), lambda i: (i, 0)),
            pl.BlockSpec((BLK, 2 * N_KV * HD), lambda i: (i, KV_COL_BLOCK)),
            pl.BlockSpec((BLK, 2 * N_KV * HD), lambda i: (jnp.maximum(i - 1, 0), KV_COL_BLOCK)),
            pl.BlockSpec(memory_space=pltpu.SMEM),
        ],
        out_specs=[_row(BLK, D), _row(BLK, N_HEADS)],
        out_shape=[SDS((s, D), bf16), SDS((s, N_HEADS), f32)],
        compiler_params=_cp("parallel"),
    )(qkv, qkv, qkv, sinks)


def attn_out_fwd(x, o, w, b, tm):
    s = x.shape[0]

    def body(x_ref, o_ref, w_ref, b_ref, y_ref):
        y_ref[...] = x_ref[...] + _dot(o_ref[...], w_ref[...]) + b_ref[...]

    return pl.pallas_call(
        body, grid=(s // tm,), name="attn_out_fwd",
        in_specs=[_row(tm, D), _row(tm, D), _const((D, D)), _const((1, D))],
        out_specs=_row(tm, D), out_shape=SDS((s, D), f32),
        compiler_params=_cp("parallel"),
    )(x, o, w, b)


def attn_out_bwd(dy, wt, tm, rider=None):
    s = dy.shape[0]

    def body(dy_ref, wt_ref, do_ref, db_ref):
        @pl.when(pl.program_id(0) == 0)
        def _():
            db_ref[...] = jnp.zeros_like(db_ref)

        dy = dy_ref[...]
        do_ref[...] = _dot(dy.astype(bf16), wt_ref[...]).astype(bf16)
        db_ref[...] += _colsum8(dy)

    return _hosted(
        rider, body, grid=(s // tm,), name="attn_out_bwd",
        in_specs=[_row(tm, D), _const((D, D))],
        out_specs=[_row(tm, D), _acc((SUB, D))],
        out_shape=[SDS((s, D), bf16), SDS((SUB, D), f32)],
        compiler_params=_cp("arbitrary"),
    )(dy, wt)


def attn_bwd(qkv, o, do, lse, sinks, rider=None):
    s = qkv.shape[0]
    nb = s // BLK
    scale = 1.0 / math.sqrt(HD)
    kvw = 2 * N_KV * HD

    def body(q_ref, kvc_ref, kvp_ref, o_ref, do_ref, lse_ref, sink_ref, dq_ref, dkv_ref, ds_ref, carry_ref):
        i = pl.program_id(0)

        @pl.when(i == 0)
        def _():
            ds_ref[...] = jnp.zeros_like(ds_ref)
            carry_ref[...] = jnp.zeros_like(carry_ref)

        @pl.when(i < nb)
        def _():
            valid = _band_mask(i)
            for kvh in range(N_KV):
                k, v = _kv_heads(kvp_ref, kvc_ref, kvh)
                qg = _stack_group(q_ref, kvh)
                dog = _stack_group(do_ref, kvh)
                lse = _stack_group(lse_ref, kvh, 1)
                sc = lax.dot_general(qg, k, _NT, preferred_element_type=f32) * scale
                sc = jnp.where(valid, sc, _NEG)
                p = jnp.exp(sc - lse)
                dp = lax.dot_general(dog, v, _NT, preferred_element_type=f32)
                dlt = jnp.sum(dog.astype(f32) * _stack_group(o_ref, kvh).astype(f32), axis=-1, keepdims=True)
                dsc = (p * (dp - dlt)).astype(bf16)
                dqg = (_dot(dsc, k) * scale).astype(bf16)
                dk = lax.dot_general(dsc, qg, _TN, preferred_element_type=f32) * scale
                dv = lax.dot_general(p.astype(bf16), dog, _TN, preferred_element_type=f32)
                dsink = jnp.exp(_group_sinks(sink_ref, kvh) - lse) * dlt
                for gi in range(GROUP):
                    h = kvh * GROUP + gi
                    dq_ref[:, h * HD:(h + 1) * HD] = dqg[gi * BLK:(gi + 1) * BLK]
                    ds_ref[:, h:h + 1] += -jnp.sum(dsink[gi * BLK:(gi + 1) * BLK], axis=0, keepdims=True)
                ks = slice(kvh * HD, (kvh + 1) * HD)
                vs = slice(N_KV * HD + kvh * HD, N_KV * HD + (kvh + 1) * HD)
                dkv_ref[:, ks] = carry_ref[:, ks] + dk[:BLK]
                dkv_ref[:, vs] = carry_ref[:, vs] + dv[:BLK]
                carry_ref[:, ks] = dk[BLK:]
                carry_ref[:, vs] = dv[BLK:]

        @pl.when(i == nb)
        def _():
            dkv_ref[...] = carry_ref[...]

    cur = lambda i: (jnp.minimum(i, nb - 1), 0)
    prev = lambda i: (jnp.clip(i - 1, 0, nb - 1), KV_COL_BLOCK)
    return _hosted(
        rider, body, grid=(nb + 1,), name="attn_bwd",
        in_specs=[
            pl.BlockSpec((BLK, D), cur),
            pl.BlockSpec((BLK, kvw), lambda i: (jnp.minimum(i, nb - 1), KV_COL_BLOCK)),
            pl.BlockSpec((BLK, kvw), prev),
            pl.BlockSpec((BLK, D), cur),
            pl.BlockSpec((BLK, D), cur),
            pl.BlockSpec((BLK, N_HEADS), cur),
            pl.BlockSpec(memory_space=pltpu.SMEM),
        ],
        out_specs=[
            pl.BlockSpec((BLK, D), cur),
            pl.BlockSpec((BLK, kvw), lambda i: (jnp.maximum(i - 1, 0), 0)),
            _acc((1, N_HEADS)),
        ],
        out_shape=[SDS((s, D), bf16), SDS((s, kvw), f32), SDS((1, N_HEADS), f32)],
        scratch_shapes=[pltpu.VMEM((BLK, kvw), f32)],
        compiler_params=_cp("arbitrary"),
    )(qkv, qkv, qkv, o, do, lse, sinks)


def qkv_bwd(dq, dkv, x, g, wt, dres, tm):
    s = x.shape[0]
    qd = N_HEADS * HD
    kvw = 2 * N_KV * HD

    def body(dq_ref, dkv_ref, x_ref, g_ref, wt_ref, dres_ref, dx_ref, dg_ref, db_ref, dkvb_ref):
        @pl.when(pl.program_id(0) == 0)
        def _():
            dg_ref[...] = jnp.zeros_like(dg_ref)
            db_ref[...] = jnp.zeros_like(db_ref)

        dq = dq_ref[...]
        dkv = dkv_ref[...]
        dkvb = dkv.astype(bf16)
        dkvb_ref[...] = dkvb
        dh = _dot(dq, wt_ref[0:qd, :]) + _dot(dkvb, wt_ref[qd:QKV, :])
        g = g_ref[...]
        _, xn, r = _rms_fwd(x_ref[...], g)
        dx, dg = _rms_bwd(xn, r, g, dh)
        dx_ref[...] = dres_ref[...] + dx
        dg_ref[...] += dg
        db_ref[:, 0:qd] += _colsum8(dq.astype(f32))
        db_ref[:, qd:QKV] += _colsum8(dkv)

    return pl.pallas_call(
        body, grid=(s // tm,), name="qkv_bwd",
        in_specs=[_row(tm, qd), _row(tm, kvw), _row(tm, D), _const((1, D)), _const((QKV, D)), _row(tm, D)],
        out_specs=[_row(tm, D), _acc((SUB, D)), _acc((SUB, QKV)), _row(tm, kvw)],
        out_shape=[SDS((s, D), f32), SDS((SUB, D), f32), SDS((SUB, QKV), f32), SDS((s, kvw), bf16)],
        compiler_params=_cp("arbitrary"),
    )(dq, dkv, x, g, wt, dres)


def ffn_fwd(x, g, wup, wdw, bdw, wdn, tm, cw, rider=None, head=None):
    s = x.shape[0]
    tail = 8

    def body(x_ref, g_ref, wup_ref, wdw_ref, bdw_ref, wdn_ref, *refs):
        if head is None:
            xo_ref, h_ref, up_ref, gate_ref, act_ref, carry_ref, ext_ref = refs
        else:
            gf_ref, t_ref, xo_ref, h_ref, up_ref, gate_ref, act_ref, dgf_ref, loss_ref, carry_ref, ext_ref = refs

        @pl.when(pl.program_id(0) == 0)
        def _():
            carry_ref[...] = jnp.zeros_like(carry_ref)
            if head is not None:
                dgf_ref[...] = jnp.zeros_like(dgf_ref)
                loss_ref[...] = jnp.zeros_like(loss_ref)

        x = x_ref[...]
        h, _, _ = _rms_fwd(x, g_ref[...])
        hb = h.astype(bf16)
        h_ref[...] = hb
        for c in range(DFF // cw):
            cs = slice(c * cw, (c + 1) * cw)
            vs = slice(DFF + c * cw, DFF + (c + 1) * cw)
            ug = _dot(hb, wup_ref[:, cs])
            uv = _dot(hb, wup_ref[:, vs])
            up_ref[:, cs] = ug.astype(bf16)
            up_ref[:, vs] = uv.astype(bf16)
            ext_ref[0:tail, :] = carry_ref[:, cs]
            ext_ref[tail:tail + tm, :] = ug
            carry_ref[:, cs] = ug[tm - tail:, :]
            ext = ext_ref[...]
            gate = (wdw_ref[0:1, cs] * pltpu.roll(ext, 2, 0)[tail:] + wdw_ref[1:2, cs] * pltpu.roll(ext, 1, 0)[tail:]
                    + wdw_ref[2:3, cs] * ug) + bdw_ref[:, cs]
            gate_ref[:, cs] = gate.astype(bf16)
            act_ref[:, cs] = (gate * _sigmoid(gate) * uv).astype(bf16)
        xo = x + _dot(act_ref[...], wdn_ref[...])
        if head is None:
            xo_ref[...] = xo
        else:
            gf = gf_ref[...]
            y, xn, r = _rms_fwd(xo, gf)
            e = y - t_ref[...]
            loss_ref[...] += 0.5 * jnp.sum(jnp.mean(e * e, axis=-1, keepdims=True), axis=0, keepdims=True)
            dx, dgf = _rms_bwd(xn, r, gf, e / D)
            xo_ref[...] = dx
            dgf_ref[...] += dgf

    extra_in = [] if head is None else [_const((1, D)), _row(tm, D)]
    extra_out = [] if head is None else [_acc((SUB, D)), _acc((1, 1))]
    extra_shape = [] if head is None else [SDS((SUB, D), f32), SDS((1, 1), f32)]
    return _hosted(
        rider, body, grid=(s // tm,), name="ffn_fwd" if head is None else "ffn_fwd_loss",
        in_specs=[_row(tm, D), _const((1, D)), _const((D, 2 * DFF)), _const((3, DFF)), _const((1, DFF)),
                  _const((DFF, D))] + extra_in,
        out_specs=[_row(tm, D), _row(tm, D), _row(tm, 2 * DFF), _row(tm, DFF), _row(tm, DFF)] + extra_out,
        out_shape=[SDS((s, D), f32), SDS((s, D), bf16), SDS((s, 2 * DFF), bf16), SDS((s, DFF), bf16),
                   SDS((s, DFF), bf16)] + extra_shape,
        scratch_shapes=[pltpu.VMEM((tail, DFF), f32), pltpu.VMEM((tail + tm, cw), f32)],
        compiler_params=_cp("arbitrary"),
    )(x, g, wup, wdw, bdw, wdn, *(head or ()))


def ffn_bwd(dxo, x, g, up, gate, wdw, wdn, wup, tm, cw, rider=None):
    s = x.shape[0]
    nt = s // tm
    rev = lambda i: (nt - 1 - i, 0)

    def body(dxo_ref, x_ref, g_ref, up_ref, gate_ref, wdw_ref, wdn_ref, wup_ref,
             dxi_ref, dup_ref, dg_ref, dwdw_ref, dbdw_ref, carry_ref, ext2_ref):
        i = pl.program_id(0)

        @pl.when(i == 0)
        def _():
            carry_ref[...] = jnp.zeros_like(carry_ref)
            dg_ref[...] = jnp.zeros_like(dg_ref)
            dwdw_ref[...] = jnp.zeros_like(dwdw_ref)
            dbdw_ref[...] = jnp.zeros_like(dbdw_ref)

        dxo = dxo_ref[...]
        dxb = dxo.astype(bf16)
        for c in range(DFF // cw):
            cs = slice(c * cw, (c + 1) * cw)
            vs = slice(DFF + c * cw, DFF + (c + 1) * cw)
            d_act = lax.dot_general(dxb, wdn_ref[cs, :], _NT, preferred_element_type=f32)
            ug = up_ref[:, cs].astype(f32)
            uv = up_ref[:, vs].astype(f32)
            gate = gate_ref[:, cs].astype(f32)
            sg = _sigmoid(gate)
            dup_ref[:, vs] = (d_act * (gate * sg)).astype(bf16)
            d_gate = d_act * uv * _dsilu(gate, sg)
            ext2_ref[0:tm, :] = d_gate
            ext2_ref[tm:tm + 8, :] = carry_ref[:, cs]
            carry_ref[:, cs] = d_gate[0:8, :]
            ext = ext2_ref[...]
            ahead1 = pltpu.roll(ext, tm + 8 - 1, 0)[:tm]
            ahead2 = pltpu.roll(ext, tm + 8 - 2, 0)[:tm]
            dbdw_ref[:, cs] += _colsum8(d_gate)
            dwdw_ref[0, :, cs] += _colsum8(ahead2 * ug)
            dwdw_ref[1, :, cs] += _colsum8(ahead1 * ug)
            dwdw_ref[2, :, cs] += _colsum8(d_gate * ug)
            d_ug = wdw_ref[0:1, cs] * ahead2 + wdw_ref[1:2, cs] * ahead1 + wdw_ref[2:3, cs] * d_gate
            dup_ref[:, cs] = d_ug.astype(bf16)
        dh = lax.dot_general(dup_ref[...], wup_ref[...], _NT, preferred_element_type=f32)
        gv = g_ref[...]
        _, xn, r = _rms_fwd(x_ref[...], gv)
        dx, dg = _rms_bwd(xn, r, gv, dh)
        dxi_ref[...] = dxo + dx
        dg_ref[...] += dg

    return _hosted(
        rider, body, grid=(nt,), name="ffn_bwd",
        in_specs=[
            pl.BlockSpec((tm, D), rev), pl.BlockSpec((tm, D), rev), _const((1, D)),
            pl.BlockSpec((tm, 2 * DFF), rev), pl.BlockSpec((tm, DFF), rev),
            _const((3, DFF)), _const((DFF, D)), _const((D, 2 * DFF)),
        ],
        out_specs=[pl.BlockSpec((tm, D), rev), pl.BlockSpec((tm, 2 * DFF), rev), _acc((SUB, D)),
                   _acc((3, SUB, DFF)), _acc((SUB, DFF))],
        out_shape=[SDS((s, D), f32), SDS((s, 2 * DFF), bf16), SDS((SUB, D), f32), SDS((3, SUB, DFF), f32),
                   SDS((SUB, DFF), f32)],
        scratch_shapes=[pltpu.VMEM((8, DFF), f32), pltpu.VMEM((tm + 8, cw), f32)],
        compiler_params=_cp("arbitrary"),
    )(dxo, x, g, up, gate, wdw, wdn, wup)


def pw1_fwd(x, g, w, b, tm):
    s = x.shape[0]

    def body(x_ref, g_ref, w_ref, b_ref, h_ref, a_ref, u_ref):
        h, _, _ = _rms_fwd(x_ref[...], g_ref[...])
        hb = h.astype(bf16)
        h_ref[...] = hb
        a = _dot(hb, w_ref[...]) + b_ref[...]
        a_ref[...] = a.astype(bf16)
        u_ref[...] = a[:, :D] * _sigmoid(a[:, D:])

    return pl.pallas_call(
        body, grid=(s // tm,), name="pw1_fwd",
        in_specs=[_row(tm, D), _const((1, D)), _const((D, 2 * D)), _const((1, 2 * D))],
        out_specs=[_row(tm, D), _row(tm, 2 * D), _row(tm, D)],
        out_shape=[SDS((s, D), bf16), SDS((s, 2 * D), bf16), SDS((s, D), f32)],
        compiler_params=_cp("parallel"),
    )(x, g, w, b)


def _ln_silu(c, lg, lb):
    mu = jnp.mean(c, axis=-1, keepdims=True)
    cc = c - mu
    var = jnp.mean(cc * cc, axis=-1, keepdims=True)
    rstd = lax.rsqrt(var + LN_EPS)
    xh = cc * rstd
    ln = xh * lg + lb
    sg = _sigmoid(ln)
    return xh, rstd, ln, sg


def _shifted_copies(ext_ref, sh_ref, cs, tm):
    n = CONV_HALO - SUB + tm
    for k in range(1, SUB):
        sh_ref[k - 1] = ext_ref[pl.ds(k, n), cs]


def _shifted_rows(ext_ref, sh_ref, cs, start, rows):
    q, k = divmod(start, SUB)
    if k == 0:
        return ext_ref[pl.ds(start, rows), cs]
    return sh_ref[k - 1, pl.ds(q * SUB, rows), :]


def conv_fwd(u, x, wdw, bdw, lg, lb, w2, b2, tm, rc):
    s = x.shape[0]
    hl = CONV_HALO
    off = hl - (CONV_W - 1)

    def body(u_ref, halo_ref, x_ref, wdw_ref, bdw_ref, lg_ref, lb_ref, w2_ref, b2_ref, c_ref, xo_ref, ext_ref, sh_ref):
        has_prev = (pl.program_id(0) > 0).astype(f32)
        ext_ref[0:hl, :] = halo_ref[...] * has_prev
        ext_ref[hl:hl + tm, :] = u_ref[...]
        for cc in range(D // LANES):
            cs = slice(cc * LANES, (cc + 1) * LANES)
            _shifted_copies(ext_ref, sh_ref, cs, tm)
            for rr in range(tm // rc):
                acc = jnp.zeros((rc, LANES), f32) + bdw_ref[:, cs]
                for j in range(CONV_W):
                    acc = acc + wdw_ref[j:j + 1, cs] * _shifted_rows(ext_ref, sh_ref, cs, rr * rc + off + j, rc)
                c_ref[rr * rc:(rr + 1) * rc, cs] = acc
        _, _, ln, sg = _ln_silu(c_ref[...], lg_ref[...], lb_ref[...])
        xo_ref[...] = x_ref[...] + _dot((ln * sg).astype(bf16), w2_ref[...]) + b2_ref[...]

    return pl.pallas_call(
        body, grid=(s // tm,), name="conv_fwd",
        in_specs=[_row(tm, D), pl.BlockSpec((hl, D), lambda i: (jnp.maximum(i * (tm // hl) - 1, 0), 0)), _row(tm, D),
                  _const((CONV_W, D)), _const((1, D)), _const((1, D)), _const((1, D)), _const((D, D)), _const((1, D))],
        out_specs=[_row(tm, D), _row(tm, D)],
        out_shape=[SDS((s, D), f32), SDS((s, D), f32)],
        scratch_shapes=[pltpu.VMEM((hl + tm, D), f32), pltpu.VMEM((SUB - 1, hl - SUB + tm, LANES), f32)],
        compiler_params=_cp("parallel"),
    )(u, u, x, wdw, bdw, lg, lb, w2, b2)


def conv_bwd_a(dy, c, lg, lb, w2t, tm, rider=None):
    s = dy.shape[0]

    def body(dy_ref, c_ref, lg_ref, lb_ref, w2t_ref, dc_ref, u3_ref, dlg_ref, dlb_ref, db2_ref, dbdw_ref):
        @pl.when(pl.program_id(0) == 0)
        def _():
            for r in (dlg_ref, dlb_ref, db2_ref, dbdw_ref):
                r[...] = jnp.zeros_like(r)

        dy = dy_ref[...]
        lg = lg_ref[...]
        xh, rstd, ln, sg = _ln_silu(c_ref[...], lg, lb_ref[...])
        u3_ref[...] = (ln * sg).astype(bf16)
        du3 = _dot(dy.astype(bf16), w2t_ref[...])
        dln = du3 * _dsilu(ln, sg)
        dxh = dln * lg
        dc = rstd * (dxh - jnp.mean(dxh, axis=-1, keepdims=True) - xh * jnp.mean(dxh * xh, axis=-1, keepdims=True))
        dc_ref[...] = dc
        dlg_ref[...] += _colsum8(dln * xh)
        dlb_ref[...] += _colsum8(dln)
        db2_ref[...] += _colsum8(dy)
        dbdw_ref[...] += _colsum8(dc)

    return _hosted(
        rider, body, grid=(s // tm,), name="conv_bwd_a",
        in_specs=[_row(tm, D), _row(tm, D), _const((1, D)), _const((1, D)), _const((D, D))],
        out_specs=[_row(tm, D), _row(tm, D)] + [_acc((SUB, D))] * 4,
        out_shape=[SDS((s, D), f32), SDS((s, D), bf16)] + [SDS((SUB, D), f32)] * 4,
        compiler_params=_cp("arbitrary"),
    )(dy, c, lg, lb, w2t)


def conv_bwd_b(dc, u, a, wdw, x, g, w1t, dres, tm, rc, rider=None):
    s = x.shape[0]
    nt = s // tm
    hl = CONV_HALO
    off = hl - (CONV_W - 1)

    def body(dc_ref, dnext_ref, u_ref, uprev_ref, a_ref, wdw_ref, x_ref, g_ref, w1t_ref, dres_ref,
             dx_ref, da_ref, dwdw_ref, db1_ref, dg_ref, ext_ref, ext2_ref, du_ref, sh_ref, sh2_ref):
        i = pl.program_id(0)

        @pl.when(i == 0)
        def _():
            for r in (dwdw_ref, db1_ref, dg_ref):
                r[...] = jnp.zeros_like(r)

        ext_ref[0:hl, :] = uprev_ref[...] * (i > 0).astype(f32)
        ext_ref[hl:hl + tm, :] = u_ref[...]
        ext2_ref[0:tm, :] = dc_ref[...]
        ext2_ref[tm:tm + hl, :] = dnext_ref[...] * (i < nt - 1).astype(f32)
        for cc in range(D // LANES):
            cs = slice(cc * LANES, (cc + 1) * LANES)
            _shifted_copies(ext_ref, sh_ref, cs, tm)
            _shifted_copies(ext2_ref, sh2_ref, cs, tm)
            for rr in range(tm // rc):
                r0 = rr * rc
                dcb = ext2_ref[r0:r0 + rc, cs]
                acc = jnp.zeros((rc, LANES), f32)
                for j in range(CONV_W):
                    acc = acc + wdw_ref[j:j + 1, cs] * _shifted_rows(ext2_ref, sh2_ref, cs, r0 + CONV_W - 1 - j, rc)
                    dwdw_ref[j, :, cs] += _colsum8(dcb * _shifted_rows(ext_ref, sh_ref, cs, r0 + off + j, rc))
                du_ref[r0:r0 + rc, cs] = acc
        du = du_ref[...]
        a1 = a_ref[:, :D].astype(f32)
        sg = _sigmoid(a_ref[:, D:].astype(f32))
        da1 = du * sg
        da2 = du * a1 * sg * (1.0 - sg)
        da_ref[:, :D] = da1.astype(bf16)
        da_ref[:, D:] = da2.astype(bf16)
        db1_ref[:, :D] += _colsum8(da1)
        db1_ref[:, D:] += _colsum8(da2)
        dh = _dot(da_ref[...], w1t_ref[...])
        gv = g_ref[...]
        _, xn, r = _rms_fwd(x_ref[...], gv)
        dx, dg = _rms_bwd(xn, r, gv, dh)
        dx_ref[...] = dres_ref[...] + dx
        dg_ref[...] += dg

    blocks = tm // hl
    return _hosted(
        rider, body, grid=(nt,), name="conv_bwd_b",
        in_specs=[
            _row(tm, D), pl.BlockSpec((hl, D), lambda i: (jnp.minimum((i + 1) * blocks, s // hl - 1), 0)),
            _row(tm, D), pl.BlockSpec((hl, D), lambda i: (jnp.maximum(i * blocks - 1, 0), 0)),
            _row(tm, 2 * D), _const((CONV_W, D)), _row(tm, D), _const((1, D)), _const((2 * D, D)), _row(tm, D),
        ],
        out_specs=[_row(tm, D), _row(tm, 2 * D), _acc((CONV_W, SUB, D)), _acc((SUB, 2 * D)), _acc((SUB, D))],
        out_shape=[SDS((s, D), f32), SDS((s, 2 * D), bf16), SDS((CONV_W, SUB, D), f32), SDS((SUB, 2 * D), f32),
                   SDS((SUB, D), f32)],
        scratch_shapes=[pltpu.VMEM((hl + tm, D), f32), pltpu.VMEM((tm + hl, D), f32), pltpu.VMEM((tm, D), f32),
                        pltpu.VMEM((SUB - 1, hl - SUB + tm, LANES), f32),
                        pltpu.VMEM((SUB - 1, hl - SUB + tm, LANES), f32)],
        compiler_params=_cp("arbitrary"),
    )(dc, dc, u, u, a, wdw, x, g, w1t, dres)


def wgrad(a, b, nb, tk, name, rider=None):
    s, k1 = a.shape
    n = b.shape[1]

    def body(a_ref, b_ref, o_ref):
        @pl.when(pl.program_id(1) == 0)
        def _():
            o_ref[...] = jnp.zeros_like(o_ref)

        o_ref[...] += lax.dot_general(a_ref[...], b_ref[...].astype(bf16), _TN, preferred_element_type=f32)

    return _hosted(
        rider, body, grid=(n // nb, s // tk), name=name,
        in_specs=[pl.BlockSpec((tk, k1), lambda j, k: (k, 0)), pl.BlockSpec((tk, nb), lambda j, k: (k, j))],
        out_specs=pl.BlockSpec((k1, nb), lambda j, k: (0, j)),
        out_shape=SDS((k1, n), f32),
        compiler_params=_cp("parallel", "arbitrary"),
    )(a, b)


def wgrad_cols(a, b, tk, name, rider=None):
    s, k1 = a.shape
    w = b.shape[1] // N_CHIPS

    def body(a_ref, b_ref, o_ref):
        @pl.when(pl.program_id(1) == 0)
        def _():
            o_ref[...] = jnp.zeros_like(o_ref)

        acc = lax.dot_general(a_ref[...], b_ref[...].astype(bf16), _TN, preferred_element_type=f32)
        o_ref[:, 0] += acc.reshape(2, k1 // 2, w)

    return _hosted(
        rider, body, grid=(N_CHIPS, s // tk), name=name,
        in_specs=[pl.BlockSpec((tk, k1), lambda j, k: (k, 0)), pl.BlockSpec((tk, w), lambda j, k: (k, j))],
        out_specs=pl.BlockSpec((2, 1, k1 // 2, w), lambda j, k: (0, j, 0, 0)),
        out_shape=SDS((2, N_CHIPS, k1 // 2, w), f32),
        compiler_params=_cp("parallel", "arbitrary"),
    )(a, b)


def wgrad_rows(a, b, nb, tk, name, rider=None):
    s, k1 = a.shape
    n = b.shape[1]
    r = k1 // (2 * N_CHIPS)

    def body(a_ref, b_ref, o_ref):
        @pl.when(pl.program_id(1) == 0)
        def _():
            o_ref[...] = jnp.zeros_like(o_ref)

        acc = lax.dot_general(a_ref[...], b_ref[...].astype(bf16), _TN, preferred_element_type=f32)
        for j in range(N_CHIPS):
            for h in range(2):
                o_ref[h, j] += acc[(2 * j + h) * r:(2 * j + h + 1) * r, :]

    return _hosted(
        rider, body, grid=(n // nb, s // tk), name=name,
        in_specs=[pl.BlockSpec((tk, k1), lambda j, k: (k, 0)), pl.BlockSpec((tk, nb), lambda j, k: (k, j))],
        out_specs=pl.BlockSpec((2, N_CHIPS, r, nb), lambda j, k: (0, 0, 0, j)),
        out_shape=SDS((2, N_CHIPS, r, n), f32),
        compiler_params=_cp("parallel", "arbitrary"),
    )(a, b)


def _adam_math(w, g, m, v):
    m = B1 * m + (1.0 - B1) * g
    v = B2 * v + (1.0 - B2) * (g * g)
    m_hat = m / (1.0 - B1 ** STEP)
    v_hat = v / (1.0 - B2 ** STEP)
    delta = -LR * (m_hat / (jnp.sqrt(v_hat) + ADAM_EPS) + WD * w)
    return delta, m, v


def _rows_tile(r, c, multiple=SUB):
    best = None
    for t in range(multiple, r + 1, multiple):
        if r % t == 0 and t * c * 4 <= ELEMENTWISE_BLOCK_BYTES:
            best = t
    return best if best is not None else r


def adamw(w, g, m, v, name):
    l, r, c = w.shape
    tr = _rows_tile(r, c)
    spec = pl.BlockSpec((1, tr, c), lambda i, j: (i, j, 0))

    def body(w_ref, g_ref, m_ref, v_ref, d_ref, mo_ref, vo_ref):
        d, mn, vn = _adam_math(w_ref[...], g_ref[...], m_ref[...], v_ref[...])
        d_ref[...] = d
        mo_ref[...] = mn
        vo_ref[...] = vn

    return pl.pallas_call(
        body, grid=(l, r // tr), name=name, in_specs=[spec] * 4, out_specs=[spec] * 3,
        out_shape=[SDS((l, r, c), f32)] * 3, compiler_params=_cp("parallel", "parallel"),
    )(w, g, m, v)


def _place():
    return lax.axis_index("x"), lax.axis_index("y"), lax.axis_index("c")


def _chip_peer(xi, yi, r):
    px = 1 - xi if r & 2 else xi
    py = 1 - yi if r & 1 else yi
    return px, py


def _gv_qkv(src, dst, j, h):
    rows = pl.ds(h * (D // 2), D // 2)
    return src.at[rows, :], dst.at[j, rows, :]


def _gv_rows(src, dst, j, h):
    r = src.shape[0] // 2
    return src.at[pl.ds(h * r, r), :], dst.at[pl.ds(j * 2 * r + h * r, r), :]


def _gv_cols(src, dst, j, h):
    r, w = src.shape[0] // 2, src.shape[1]
    return src.at[pl.ds(h * r, r), :], dst.at[pl.ds(h * r, r), pl.ds(j * w, w)]


def gather_rider(big, small, forward_at):
    nb, ns = len(big), len(small)
    n = nb + ns
    views = [v for _, v, _ in big]

    def env(ins, outs, sems):
        ici_send, ici_recv, d2d_send, d2d_recv, loc_sems = sems
        xi, yi, ci = _place()
        me = 2 * xi + yi

        def local(a, h):
            if a < nb:
                src, dst = views[a](ins[a], outs[a], me, h)
                return pltpu.make_async_copy(src, dst, loc_sems.at[2 * a + h])
            return pltpu.make_async_copy(ins[a], outs[a].at[me], loc_sems.at[nb + a])

        def ici(a, r, slot):
            px, py = _chip_peer(xi, yi, r)
            src, dst = views[a](ins[a], outs[a], slot, ci) if a < nb else (ins[a], outs[a].at[slot])
            k = 3 * a + r - 1
            return pltpu.make_async_remote_copy(src_ref=src, dst_ref=dst, send_sem=ici_send.at[k],
                                                recv_sem=ici_recv.at[k], device_id=(px, py, ci), device_id_type=MESH)

        def d2d(a, r, half):
            px, py = _chip_peer(xi, yi, r)
            _, dst = views[a](ins[a], outs[a], 2 * px + py, half)
            k = 3 * a + r - 1
            return pltpu.make_async_remote_copy(src_ref=dst, dst_ref=dst, send_sem=d2d_send.at[k],
                                                recv_sem=d2d_recv.at[k], device_id=(xi, yi, 1 - ci), device_id_type=MESH)

        return xi, yi, ci, me, local, ici, d2d

    def locals_of():
        return [(a, h) for a in range(nb) for h in range(2)] + [(a, 0) for a in range(nb, n)]

    def send(ins, outs, sems):
        _, _, _, me, local, ici, _ = env(ins, outs, sems)
        for a, h in locals_of():
            local(a, h).start()
        for a in range(n):
            for r in (1, 2, 3):
                ici(a, r, me).start()

    def forward(ins, outs, sems):
        xi, yi, ci, _, _, ici, d2d = env(ins, outs, sems)
        for a in range(n):
            for r in (1, 2, 3):
                px, py = _chip_peer(xi, yi, r)
                ici(a, r, 2 * px + py).wait_recv()
                if a < nb:
                    d2d(a, r, ci).start()

    def finish(ins, outs, sems):
        _, _, ci, me, local, ici, d2d = env(ins, outs, sems)
        for a in range(nb):
            for r in (1, 2, 3):
                d2d(a, r, 1 - ci).wait_recv()
        for a in range(n):
            for r in (1, 2, 3):
                ici(a, r, me).wait_send()
                if a < nb:
                    d2d(a, r, ci).wait_send()
        for a, h in locals_of():
            local(a, h).wait()

    dma = pltpu.SemaphoreType.DMA
    return Rider(
        ins=[b for b, _, _ in big] + list(small),
        out_shape=[SDS(shape, bf16) for _, _, shape in big] + [SDS((N_CHIPS,) + a.shape, a.dtype) for a in small],
        sem_shapes=[dma((3 * n,)), dma((3 * n,)), dma((max(3 * nb, 1),)), dma((max(3 * nb, 1),)), dma((2 * nb + ns,))],
        stages=[(0.0, send), (forward_at, forward)], final=finish)


def pair_send_rider(gs):
    n = len(gs)

    def copy(ins, outs, sems, a):
        xi, yi, ci = _place()
        return pltpu.make_async_remote_copy(
            src_ref=ins[a].at[1 - ci], dst_ref=outs[a], send_sem=sems[0].at[a], recv_sem=sems[1].at[a],
            device_id=(xi, yi, 1 - ci), device_id_type=MESH)

    def send(ins, outs, sems):
        for a in range(n):
            copy(ins, outs, sems, a).start()

    def finish(ins, outs, sems):
        for a in range(n):
            copy(ins, outs, sems, a).wait()

    dma = pltpu.SemaphoreType.DMA
    return Rider(ins=list(gs), out_shape=[SDS(g.shape[1:], g.dtype) for g in gs], sem_shapes=[dma((n,)), dma((n,))],
                 stages=[(0.0, send)], final=finish)


def chip_rider(pbs, p32s, forward_at):
    n = len(pbs)

    def env(ins, outs, sems):
        ici_send, ici_recv, d2d_send, d2d_recv, own_send, own_recv, loc_sems = sems
        pb, p32, recv, own = ins[:n], ins[n:], outs[:n], outs[n:]
        xi, yi, ci = _place()
        me = 2 * xi + yi
        sib = (xi, yi, 1 - ci)

        def ici(a, r, src_slot, dst_slot):
            px, py = _chip_peer(xi, yi, r)
            k = 3 * a + r - 1
            return pltpu.make_async_remote_copy(
                src_ref=pb[a].at[src_slot], dst_ref=recv[a].at[ci, dst_slot], send_sem=ici_send.at[k],
                recv_sem=ici_recv.at[k], device_id=(px, py, ci), device_id_type=MESH)

        def d2d(a, r, half):
            px, py = _chip_peer(xi, yi, r)
            blk = recv[a].at[half, 2 * px + py]
            k = 3 * a + r - 1
            return pltpu.make_async_remote_copy(src_ref=blk, dst_ref=blk, send_sem=d2d_send.at[k],
                                                recv_sem=d2d_recv.at[k], device_id=sib, device_id_type=MESH)

        def mine(a, half):
            return pltpu.make_async_remote_copy(src_ref=p32[a].at[me], dst_ref=own[a].at[half], send_sem=own_send.at[a],
                                                recv_sem=own_recv.at[a], device_id=sib, device_id_type=MESH)

        def local(a):
            return pltpu.make_async_copy(p32[a].at[me], own[a].at[ci], loc_sems.at[a])

        return xi, yi, ci, me, ici, d2d, mine, local

    def send(ins, outs, sems):
        xi, yi, ci, me, ici, _, mine, local = env(ins, outs, sems)
        for a in range(n):
            local(a).start()
            mine(a, ci).start()
            for r in (1, 2, 3):
                px, py = _chip_peer(xi, yi, r)
                ici(a, r, 2 * px + py, me).start()

    def forward(ins, outs, sems):
        xi, yi, ci, me, ici, d2d, _, _ = env(ins, outs, sems)
        for a in range(n):
            for r in (1, 2, 3):
                px, py = _chip_peer(xi, yi, r)
                ici(a, r, me, 2 * px + py).wait_recv()
                d2d(a, r, ci).start()

    def finish(ins, outs, sems):
        xi, yi, ci, me, ici, d2d, mine, local = env(ins, outs, sems)
        for a in range(n):
            mine(a, 1 - ci).wait_recv()
            for r in (1, 2, 3):
                d2d(a, r, 1 - ci).wait_recv()
        for a in range(n):
            mine(a, ci).wait_send()
            local(a).wait()
            for r in (1, 2, 3):
                px, py = _chip_peer(xi, yi, r)
                ici(a, r, 2 * px + py, me).wait_send()
                d2d(a, r, ci).wait_send()

    dma = pltpu.SemaphoreType.DMA
    return Rider(
        ins=list(pbs) + list(p32s),
        out_shape=[SDS((2,) + p.shape, bf16) for p in pbs] + [SDS((2,) + p.shape[1:], f32) for p in p32s],
        sem_shapes=[dma((3 * n,)), dma((3 * n,)), dma((3 * n,)), dma((3 * n,)), dma((n,)), dma((n,)), dma((n,))],
        stages=[(0.0, send), (forward_at, forward)], final=finish)


def rs_pair_add(g, other, ci, name):
    _, nsh, r, w = g.shape
    tr = _rows_tile(r, w, 16)

    def body(c_ref, g_ref, o_ref, p_ref, pb_ref):
        p = g_ref[0] + o_ref[...]
        p_ref[...] = p
        pb_ref[...] = p.astype(bf16)

    blk = pl.BlockSpec((1, tr, w), lambda j, i, c: (j, i, 0))
    return pl.pallas_call(
        body, name=name,
        grid_spec=pltpu.PrefetchScalarGridSpec(
            num_scalar_prefetch=1, grid=(nsh, r // tr),
            in_specs=[pl.BlockSpec((1, 1, tr, w), lambda j, i, c: (c[0], j, i, 0)), blk], out_specs=[blk, blk]),
        out_shape=[SDS((nsh, r, w), f32), SDS((nsh, r, w), bf16)],
        compiler_params=_cp("parallel", "parallel"),
    )(ci.reshape(1).astype(jnp.int32), g, other)


def rs_chip_add(own, recv, chip, name, layer=0, layers=1, into=None):
    _, nsh, r, w = recv.shape
    tr = _rows_tile(r, w, 16)

    def body(c_ref, own_ref, recv_ref, *refs):
        o_ref = refs[-1]
        me = c_ref[0]
        acc = None
        for j in range(N_CHIPS):
            term = jnp.where(me == j, own_ref[0], recv_ref[0, j].astype(f32))
            acc = term if acc is None else acc + term
        o_ref[0, 0] = acc

    args = (chip.reshape(1).astype(jnp.int32), own, recv) + (() if into is None else (into,))
    return pl.pallas_call(
        body, name=name,
        grid_spec=pltpu.PrefetchScalarGridSpec(
            num_scalar_prefetch=1, grid=(2, r // tr),
            in_specs=[pl.BlockSpec((1, tr, w), lambda h, i, c: (h, i, 0)),
                      pl.BlockSpec((1, nsh, tr, w), lambda h, i, c: (h, 0, i, 0))] + ([] if into is None else [_ANY]),
            out_specs=pl.BlockSpec((1, 1, tr, w), lambda h, i, c: (layer, h, i, 0))),
        out_shape=SDS((layers, 2, r, w), f32),
        input_output_aliases={} if into is None else {3: 0},
        compiler_params=_cp("parallel", "parallel"),
    )(*args)


class GradReduction:
    def __init__(self, keys, grads, ci, chip):
        self.keys, self.grads, self.ci, self.chip = keys, grads, ci, chip

    def pair_rider(self):
        return pair_send_rider(self.grads)

    def chip_rider(self, from_pair, forward_at):
        both = [rs_pair_add(g, o, self.ci, "rs_pair_add_" + k) for k, g, o in zip(self.keys, self.grads, from_pair)]
        return chip_rider([pb for _, pb in both], [p for p, _ in both], forward_at)

    def result(self, landed, place=None):
        n = len(self.keys)
        return {k: rs_chip_add(own, recv, self.chip, "rs_chip_add_" + k, *(place or {}).get(k, ()))
                for k, recv, own in zip(self.keys, landed[:n], landed[n:])}

    def alone(self):
        from_pair = run_rider(self.pair_rider(), "rs_pair_" + self.keys[0])
        return self.result(run_rider(self.chip_rider(from_pair, 0.0), "rs_chip_" + self.keys[0]))


STAGE_W = DFF
_ST = {"norm_mix0": 0, "norm_mix1": 1, "attn_b_qkv": 2, "attn_sinks": 3, "attn_b_o": 4, "final_norm": 5, "loss": 6,
       "norm_ffn0": 8, "norm_ffn1": 9, "ffn_b_dw0": 10, "ffn_b_dw1": 11, "conv_b_pw1": 12, "conv_b_dw": 13,
       "conv_ln_g": 14, "conv_ln_b": 15, "conv_b_pw2": 16, "ffn_w_dw0": 17, "ffn_w_dw1": 20, "conv_w_dw": 24}
STAGE_ROWS = 56
SMALL_REP = ("norm_mix", "attn_b_qkv", "attn_sinks", "attn_b_o", "norm_ffn", "ffn_b_dw", "final_norm")
SMALL_SH = ("conv_b_pw1", "conv_w_dw", "conv_b_dw", "conv_ln_g", "conv_ln_b", "conv_b_pw2", "ffn_w_dw")
_SMALL_PARTS = ("norm_mix0", "norm_mix1", "attn_b_qkv", "attn_sinks", "attn_b_o", "norm_ffn0", "norm_ffn1", "ffn_b_dw0",
                "ffn_b_dw1", "final_norm", "conv_b_pw1", "conv_b_dw", "conv_ln_g", "conv_ln_b", "conv_b_pw2", "loss",
                "ffn_w_dw0", "ffn_w_dw1", "conv_w_dw")


def small_reduce_adamw(parts, w, m, v):
    names = SMALL_REP + SMALL_SH
    npart, nw = len(_SMALL_PARTS), len(names)

    def body(*refs):
        part = dict(zip(_SMALL_PARTS, refs[:npart]))
        off = npart
        w_ref = dict(zip(names, refs[off:off + nw]))
        m_ref = dict(zip(names, refs[off + nw:off + 2 * nw]))
        v_ref = dict(zip(names, refs[off + 2 * nw:off + 3 * nw]))
        off += 3 * nw
        loss_ref = refs[off]
        g_out = dict(zip(names, refs[off + 1:off + 1 + nw]))
        d_out = dict(zip(names, refs[off + 1 + nw:off + 1 + 2 * nw]))
        m_out = dict(zip(names, refs[off + 1 + 2 * nw:off + 1 + 3 * nw]))
        v_out = dict(zip(names, refs[off + 1 + 3 * nw:off + 1 + 4 * nw]))
        stage_ref, buf_ref, tot_ref, send_sems, recv_sems = refs[off + 1 + 4 * nw:]

        xi, yi, ci = _place()
        me = 4 * xi + 2 * yi + ci
        chip = 2 * xi + yi

        stage_ref[...] = jnp.zeros_like(stage_ref)
        for name in _SMALL_PARTS:
            ref, r0 = part[name], _ST[name]
            if name in ("attn_sinks", "loss"):
                val = ref[...]
            elif name in ("ffn_w_dw0", "ffn_w_dw1", "conv_w_dw"):
                val = jnp.sum(ref[...], axis=1)
            else:
                val = jnp.sum(ref[...], axis=0, keepdims=True)
            stage_ref[r0:r0 + val.shape[0], 0:val.shape[1]] = val

        buf_ref[me] = stage_ref[...]

        def peer(r):
            px, py = _chip_peer(xi, yi, r >> 1)
            return px, py, (1 - ci if r & 1 else ci)

        def copy(r, slot):
            return pltpu.make_async_remote_copy(
                src_ref=stage_ref, dst_ref=buf_ref.at[slot], send_sem=send_sems.at[r - 1], recv_sem=recv_sems.at[r - 1],
                device_id=peer(r), device_id_type=MESH)

        sends = []
        for r in range(1, N_DEV):
            cp = copy(r, me)
            cp.start()
            sends.append(cp)
        for r in range(1, N_DEV):
            px, py, pc = peer(r)
            copy(r, 4 * px + 2 * py + pc).wait_recv()
        for cp in sends:
            cp.wait_send()
        acc = buf_ref[0]
        for d in range(1, N_DEV):
            acc = acc + buf_ref[d]
        tot_ref[...] = acc

        def rows(name, n, width):
            r0 = _ST[name]
            return tot_ref[r0:r0 + n, 0:width]

        def mine(name, n, width):
            r0 = _ST[name]
            out = tot_ref[r0:r0 + n, 0:width]
            for j in range(1, N_CHIPS):
                out = jnp.where(chip == j, tot_ref[r0:r0 + n, j * width:(j + 1) * width], out)
            return out

        loss_ref[...] = rows("loss", 1, 1)
        grads = {
            "norm_mix": rows("norm_mix0", 2, D), "attn_b_qkv": rows("attn_b_qkv", 1, QKV),
            "attn_sinks": rows("attn_sinks", 1, N_HEADS), "attn_b_o": rows("attn_b_o", 1, D),
            "norm_ffn": rows("norm_ffn0", 2, D), "ffn_b_dw": rows("ffn_b_dw0", 2, DFF),
            "final_norm": rows("final_norm", 1, D),
            "conv_b_pw1": mine("conv_b_pw1", 1, 2 * D // N_CHIPS), "conv_w_dw": mine("conv_w_dw", CONV_W, D // N_CHIPS),
            "conv_b_dw": mine("conv_b_dw", 1, D // N_CHIPS), "conv_ln_g": mine("conv_ln_g", 1, D // N_CHIPS),
            "conv_ln_b": mine("conv_ln_b", 1, D // N_CHIPS), "conv_b_pw2": mine("conv_b_pw2", 1, D // N_CHIPS),
        }
        for name in names:
            if name == "ffn_w_dw":
                continue
            at = 0 if name == "conv_w_dw" else Ellipsis
            g = grads[name]
            d, mn, vn = _adam_math(w_ref[name][at], g, m_ref[name][at], v_ref[name][at])
            g_out[name][at] = g
            d_out[name][at] = d
            m_out[name][at] = mn
            v_out[name][at] = vn
        for layer, key in enumerate(("ffn_w_dw0", "ffn_w_dw1")):
            g = mine(key, 3, DFF // N_CHIPS)
            d, mn, vn = _adam_math(w_ref["ffn_w_dw"][layer], g, m_ref["ffn_w_dw"][layer], v_ref["ffn_w_dw"][layer])
            g_out["ffn_w_dw"][layer] = g
            d_out["ffn_w_dw"][layer] = d
            m_out["ffn_w_dw"][layer] = mn
            v_out["ffn_w_dw"][layer] = vn

    ins = [parts[k] for k in _SMALL_PARTS] + [src[k] for src in (w, m, v) for k in names]
    wshapes = [SDS(w[k].shape, f32) for k in names]
    outs = pl.pallas_call(
        body, name="small_reduce_adamw", in_specs=[_VMEM] * len(ins), out_specs=[_VMEM] * (1 + 4 * nw),
        out_shape=[SDS((1, 1), f32)] + wshapes * 4,
        scratch_shapes=[pltpu.VMEM((STAGE_ROWS, STAGE_W), f32), pltpu.VMEM((N_DEV, STAGE_ROWS, STAGE_W), f32),
                        pltpu.VMEM((STAGE_ROWS, STAGE_W), f32), pltpu.SemaphoreType.DMA((N_DEV - 1,)),
                        pltpu.SemaphoreType.DMA((N_DEV - 1,))],
        compiler_params=pltpu.CompilerParams(vmem_limit_bytes=VMEM_LIMIT),
    )(*ins)
    loss = outs[0]
    g, d, mn, vn = (dict(zip(names, outs[1 + k * nw:1 + (k + 1) * nw])) for k in range(4))
    return loss, g, d, mn, vn


TM = 512
TM_LIGHT = 1024
TM_FFN = 256
FFN_CHUNK = 256
CONV_ROWS = 128
CONV_BWD_ROWS = 64
TK = 2048
TK_LIGHT = 4096
FORWARD_AT = 0.6
FORWARD_AT_LATE = 0.85


def kernel(x, norm_mix, attn_w_qkv, attn_b_qkv, attn_sinks, attn_w_o, attn_b_o, conv_w_pw1, conv_b_pw1, conv_w_dw, conv_b_dw, conv_ln_g, conv_ln_b, conv_w_pw2, conv_b_pw2, norm_ffn, ffn_w_up, ffn_w_dw, ffn_b_dw, ffn_w_down, final_norm, loss_target, m_norm_mix, m_attn_w_qkv, m_attn_b_qkv, m_attn_sinks, m_attn_w_o, m_attn_b_o, m_conv_w_pw1, m_conv_b_pw1, m_conv_w_dw, m_conv_b_dw, m_conv_ln_g, m_conv_ln_b, m_conv_w_pw2, m_conv_b_pw2, m_norm_ffn, m_ffn_w_up, m_ffn_w_dw, m_ffn_b_dw, m_ffn_w_down, m_final_norm, v_norm_mix, v_attn_w_qkv, v_attn_b_qkv, v_attn_sinks, v_attn_w_o, v_attn_b_o, v_conv_w_pw1, v_conv_b_pw1, v_conv_w_dw, v_conv_b_dw, v_conv_ln_g, v_conv_ln_b, v_conv_w_pw2, v_conv_b_pw2, v_norm_ffn, v_ffn_w_up, v_ffn_w_dw, v_ffn_b_dw, v_ffn_w_down, v_final_norm):
    w = dict(norm_mix=norm_mix, attn_w_qkv=attn_w_qkv, attn_b_qkv=attn_b_qkv, attn_sinks=attn_sinks, attn_w_o=attn_w_o,
             attn_b_o=attn_b_o, conv_w_pw1=conv_w_pw1, conv_b_pw1=conv_b_pw1, conv_w_dw=conv_w_dw, conv_b_dw=conv_b_dw,
             conv_ln_g=conv_ln_g, conv_ln_b=conv_ln_b, conv_w_pw2=conv_w_pw2, conv_b_pw2=conv_b_pw2, norm_ffn=norm_ffn,
             ffn_w_up=ffn_w_up, ffn_w_dw=ffn_w_dw, ffn_b_dw=ffn_b_dw, ffn_w_down=ffn_w_down, final_norm=final_norm)
    mom = dict(norm_mix=m_norm_mix, attn_w_qkv=m_attn_w_qkv, attn_b_qkv=m_attn_b_qkv, attn_sinks=m_attn_sinks,
               attn_w_o=m_attn_w_o, attn_b_o=m_attn_b_o, conv_w_pw1=m_conv_w_pw1, conv_b_pw1=m_conv_b_pw1,
               conv_w_dw=m_conv_w_dw, conv_b_dw=m_conv_b_dw, conv_ln_g=m_conv_ln_g, conv_ln_b=m_conv_ln_b,
               conv_w_pw2=m_conv_w_pw2, conv_b_pw2=m_conv_b_pw2, norm_ffn=m_norm_ffn, ffn_w_up=m_ffn_w_up,
               ffn_w_dw=m_ffn_w_dw, ffn_b_dw=m_ffn_b_dw, ffn_w_down=m_ffn_w_down, final_norm=m_final_norm)
    vel = dict(norm_mix=v_norm_mix, attn_w_qkv=v_attn_w_qkv, attn_b_qkv=v_attn_b_qkv, attn_sinks=v_attn_sinks,
               attn_w_o=v_attn_w_o, attn_b_o=v_attn_b_o, conv_w_pw1=v_conv_w_pw1, conv_b_pw1=v_conv_b_pw1,
               conv_w_dw=v_conv_w_dw, conv_b_dw=v_conv_b_dw, conv_ln_g=v_conv_ln_g, conv_ln_b=v_conv_ln_b,
               conv_w_pw2=v_conv_w_pw2, conv_b_pw2=v_conv_b_pw2, norm_ffn=v_norm_ffn, ffn_w_up=v_ffn_w_up,
               ffn_w_dw=v_ffn_w_dw, ffn_b_dw=v_ffn_b_dw, ffn_w_down=v_ffn_w_down, final_norm=v_final_norm)
    order = ("norm_mix", "attn_w_qkv", "attn_b_qkv", "attn_sinks", "attn_w_o", "attn_b_o", "conv_w_pw1", "conv_b_pw1",
             "conv_w_dw", "conv_b_dw", "conv_ln_g", "conv_ln_b", "conv_w_pw2", "conv_b_pw2", "norm_ffn", "ffn_w_up",
             "ffn_w_dw", "ffn_b_dw", "ffn_w_down", "final_norm")
    xi, yi, ci = _place()
    chip = 2 * xi + yi
    xs, target = x[0], loss_target[0]
    s = xs.shape[0]
    tm, tmf, tk, tkl, tml = min(TM, s), min(TM_FFN, s), min(TK, s), min(TK_LIGHT, s), min(TM_LIGHT, s)
    row = lambda v: v.reshape(1, -1)
    join = lambda a, axis: jnp.concatenate([a[j] for j in range(N_CHIPS)], axis=axis)
    cast = lambda a: a.astype(bf16)
    small, big = {}, {}

    got = run_rider(gather_rider(
        [(cast(attn_w_qkv[0]), _gv_qkv, (N_CHIPS, D, QKV // N_CHIPS)), (cast(attn_w_o[0]), _gv_rows, (D, D))],
        [w[k] for k in SMALL_SH], 0.0), "gather_attn")
    qkv4, w_o = got[:2]
    sm = dict(zip(SMALL_SH, got[2:]))
    w_qkv = jnp.transpose(qkv4, (1, 0, 2)).reshape(D, QKV)
    sinks = attn_sinks.reshape(N_HEADS)
    b_pw1, conv_dw, conv_bdw = join(sm["conv_b_pw1"], 1), join(sm["conv_w_dw"], 2)[0], join(sm["conv_b_dw"], 1)
    ln_g, ln_b, b_pw2, ffn_dw = (join(sm["conv_ln_g"], 1), join(sm["conv_ln_b"], 1), join(sm["conv_b_pw2"], 1),
                                 join(sm["ffn_w_dw"], 2))

    h0, qkv = qkv_fwd(xs, row(norm_mix[0]), w_qkv, attn_b_qkv, tml)
    (o, lse), (w_up0, w_dn0) = attn_fwd(qkv, sinks, rider=gather_rider(
        [(cast(ffn_w_up[0]), _gv_cols, (D, 2 * DFF)), (cast(ffn_w_down[0]), _gv_rows, (DFF, D))], [], FORWARD_AT))
    x1 = attn_out_fwd(xs, o, w_o, attn_b_o, tml)
    (x2, h1, up0, gate0, act0), (w_pw1, w_pw2, w_up1, w_dn1) = ffn_fwd(
        x1, row(norm_ffn[0]), w_up0, ffn_dw[0], row(ffn_b_dw[0]), w_dn0, tmf, FFN_CHUNK, rider=gather_rider(
            [(cast(conv_w_pw1[0]), _gv_cols, (D, 2 * D)), (cast(conv_w_pw2[0]), _gv_rows, (D, D)),
             (cast(ffn_w_up[1]), _gv_cols, (D, 2 * DFF)), (cast(ffn_w_down[1]), _gv_rows, (DFF, D))], [],
            FORWARD_AT_LATE))
    h2, a, u = pw1_fwd(x2, row(norm_mix[1]), w_pw1, b_pw1, tml)
    c, x3 = conv_fwd(u, x2, conv_dw, conv_bdw, ln_g, ln_b, w_pw2, b_pw2, tm, CONV_ROWS)
    dx4, h3, up1, gate1, act1, small["final_norm"], small["loss"] = ffn_fwd(
        x3, row(norm_ffn[1]), w_up1, ffn_dw[1], row(ffn_b_dw[1]), w_dn1, tmf, FFN_CHUNK,
        head=(final_norm.reshape(1, D), target))

    dx3, dup1, small["norm_ffn1"], small["ffn_w_dw1"], small["ffn_b_dw1"] = ffn_bwd(
        dx4, x3, row(norm_ffn[1]), up1, gate1, ffn_dw[1], w_dn1, w_up1, tmf, FFN_CHUNK)
    red1 = GradReduction(("up1", "down1"), [wgrad_cols(h3, dup1, tk, "wgrad_up1"),
                                           wgrad_rows(act1, dx4, 512, tk, "wgrad_down1")], ci, chip)
    (dc, u3, small["conv_ln_g"], small["conv_ln_b"], small["conv_b_pw2"], small["conv_b_dw"]), from_pair = conv_bwd_a(
        dx3, c, ln_g, ln_b, w_pw2.T, tm, rider=red1.pair_rider())
    g_pw2 = wgrad_rows(u3, dx3, 512, tkl, "wgrad_pw2")
    (dx2, da, small["conv_w_dw"], small["conv_b_pw1"], small["norm_mix1"]), landed = conv_bwd_b(
        dc, u, a, conv_dw, x2, row(norm_mix[1]), w_pw1.T, dx3, tm, CONV_BWD_ROWS,
        rider=red1.chip_rider(from_pair, FORWARD_AT))
    big.update(red1.result(landed, {"up1": (1, 2, None), "down1": (1, 2, None)}))
    g_pw1 = wgrad_cols(h2, da, tkl, "wgrad_pw1")

    red2 = GradReduction(("pw1", "pw2"), [g_pw1, g_pw2], ci, chip)
    (dx1, dup0, small["norm_ffn0"], small["ffn_w_dw0"], small["ffn_b_dw0"]), from_pair = ffn_bwd(
        dx2, x1, row(norm_ffn[0]), up0, gate0, ffn_dw[0], w_dn0, w_up0, tmf, FFN_CHUNK, rider=red2.pair_rider())
    g_up0, landed = wgrad_cols(h1, dup0, tk, "wgrad_up0", rider=red2.chip_rider(from_pair, FORWARD_AT))
    big.update(red2.result(landed))
    g_dn0 = wgrad_rows(act0, dx2, 512, tk, "wgrad_down0")

    red3 = GradReduction(("up0", "down0", "wo"), [g_up0, g_dn0, wgrad_rows(o, dx1, 512, tkl, "wgrad_o")], ci, chip)
    (do, small["attn_b_o"]), from_pair = attn_out_bwd(dx1, w_o.T, tml, rider=red3.pair_rider())
    (dq, dkv, small["attn_sinks"]), landed = attn_bwd(qkv, o, do, lse, sinks,
                                                      rider=red3.chip_rider(from_pair, FORWARD_AT))
    both = red3.result(landed, {"up0": (0, 2, big["up1"]), "down0": (0, 2, big["down1"])})
    big.update({"up": both.pop("up0"), "down": both.pop("down0"), **both})
    dx0, small["norm_mix0"], small["attn_b_qkv"], dkvb = qkv_bwd(dq, dkv, xs, row(norm_mix[0]), w_qkv.T, dx1, tml)
    g_qkv = jnp.concatenate([wgrad(h0, dq, 512, tkl, "wgrad_q"), wgrad(h0, dkvb, 2 * N_KV * HD, tkl, "wgrad_kv")], axis=1)
    g_qkv = jnp.transpose(g_qkv.reshape(2, D // 2, N_CHIPS, QKV // N_CHIPS), (0, 2, 1, 3))
    red4 = GradReduction(("qkv",), [g_qkv], ci, chip)
    big.update(red4.alone())

    gbig = {
        "attn_w_qkv": big["qkv"].reshape(1, D, QKV // N_CHIPS), "attn_w_o": big["wo"].reshape(1, D // N_CHIPS, D),
        "conv_w_pw1": big["pw1"].reshape(1, D, 2 * D // N_CHIPS), "conv_w_pw2": big["pw2"].reshape(1, D // N_CHIPS, D),
        "ffn_w_up": big["up"].reshape(2, D, 2 * DFF // N_CHIPS),
        "ffn_w_down": big["down"].reshape(2, DFF // N_CHIPS, D),
    }

    fix = lambda d: {**d, "final_norm": d["final_norm"].reshape(1, D)}
    loss, gs, ds, ms, vs = small_reduce_adamw(small, fix(w), fix(mom), fix(vel))
    unfix = lambda d: {**d, "final_norm": d["final_norm"].reshape(D)}
    gout, delta, new_m, new_v = unfix(gs), unfix(ds), unfix(ms), unfix(vs)

    for name, g in gbig.items():
        gout[name] = g
        delta[name], new_m[name], new_v[name] = adamw(w[name], g, mom[name], vel[name], "adamw_" + name)

    return (loss.reshape(()), dx0[None], *[gout[n] for n in order], *[delta[n] for n in order],
            *[new_m[n] for n in order], *[new_v[n] for n in order])
```

```python
import math

import jax
import jax.numpy as jnp
from jax import lax
from jax.experimental import pallas as pl
from jax.experimental.pallas import tpu as pltpu

f32 = jnp.float32
bf16 = jnp.bfloat16
SDS = jax.ShapeDtypeStruct
MESH = pl.DeviceIdType.MESH

D = 1024
N_HEADS = 16
N_KV = 2
GROUP = 8
HD = 64
BLK = 128
QKV = (N_HEADS + 2 * N_KV) * HD
KV_COL_BLOCK = (N_HEADS * HD) // (2 * N_KV * HD)
CONV_W = 31
CONV_HALO = 32
DFF = 2816
RMS_EPS = 1e-6
LN_EPS = 1e-5
LR, B1, B2, ADAM_EPS, WD, STEP = 0.001, 0.9, 0.999, 1e-08, 0.01, 10

N_CHIPS = 4
N_DEV = 8
VMEM_LIMIT = 56 * 1024 * 1024
LANES = 128
SUB = 8
ELEMENTWISE_BLOCK_BYTES = 1 << 20


def _cp(*sem):
    return pltpu.CompilerParams(dimension_semantics=sem, vmem_limit_bytes=VMEM_LIMIT)


def _row(tm, n):
    return pl.BlockSpec((tm, n), lambda i: (i, 0))


def _const(shape):
    return pl.BlockSpec(shape, lambda *_: (0,) * len(shape), pipeline_mode=pl.Buffered(1))


def _acc(shape):
    return pl.BlockSpec(shape, lambda *_: (0,) * len(shape))


def _rms_fwd(x, g):
    r = lax.rsqrt(jnp.mean(x * x, axis=-1, keepdims=True) + RMS_EPS)
    xn = x * r
    return xn * g, xn, r


def _colsum8(v):
    return jnp.sum(v.reshape(v.shape[0] // SUB, SUB, v.shape[1]), axis=0)


def _rms_bwd(xn, r, g, dh):
    dyn = dh * g
    dx = r * (dyn - xn * jnp.mean(dyn * xn, axis=-1, keepdims=True))
    return dx, _colsum8(dh * xn)


def _sigmoid(z):
    return 0.5 * jnp.tanh(0.5 * z) + 0.5


def _dsilu(z, sg):
    return sg * (1.0 + z * (1.0 - sg))


def _dot(a, b):
    return jnp.dot(a, b, preferred_element_type=f32)


_ANY = pl.BlockSpec(memory_space=pl.ANY)
_VMEM = pl.BlockSpec(memory_space=pltpu.VMEM)


class Rider:
    def __init__(self, ins, out_shape, sem_shapes, stages, final):
        self.ins, self.out_shape, self.sem_shapes, self.stages, self.final = ins, out_shape, sem_shapes, stages, final


def run_rider(rider, name):
    n_in, n_out = len(rider.ins), len(rider.out_shape)

    def body(*refs):
        parts = refs[:n_in], refs[n_in:n_in + n_out], refs[n_in + n_out:]
        for _, fn in rider.stages:
            fn(*parts)
        rider.final(*parts)

    return pl.pallas_call(
        body, name=name, in_specs=[_ANY] * n_in, out_specs=[_ANY] * n_out, out_shape=list(rider.out_shape),
        scratch_shapes=list(rider.sem_shapes),
    )(*rider.ins)


def _hosted(rider, body, *, grid, in_specs, out_specs, out_shape, name, compiler_params, scratch_shapes=()):
    if rider is None:
        return pl.pallas_call(body, grid=grid, in_specs=in_specs, out_specs=out_specs, out_shape=out_shape, name=name,
                              compiler_params=compiler_params, scratch_shapes=list(scratch_shapes))
    single = not isinstance(out_shape, (list, tuple))
    shapes = [out_shape] if single else list(out_shape)
    specs = [out_specs] if single else list(out_specs)
    n_in, n_out, n_sc = len(in_specs), len(shapes), len(scratch_shapes)
    r_in, r_out = len(rider.ins), len(rider.out_shape)
    total = math.prod(grid)

    def wrapped(*refs):
        own_in, refs = refs[:n_in], refs[n_in:]
        r_ins, refs = refs[:r_in], refs[r_in:]
        own_out, refs = refs[:n_out], refs[n_out:]
        r_outs, refs = refs[:r_out], refs[r_out:]
        own_sc, r_sems = refs[:n_sc], refs[n_sc:]
        step = 0
        for d, n in enumerate(grid):
            step = step * n + pl.program_id(d)
        for frac, fn in rider.stages:
            @pl.when(step == min(int(frac * total), total - 1))
            def _(fn=fn):
                fn(r_ins, r_outs, r_sems)

        body(*own_in, *own_out, *own_sc)

        @pl.when(step == total - 1)
        def _():
            rider.final(r_ins, r_outs, r_sems)

    call = pl.pallas_call(
        wrapped, grid=grid, in_specs=list(in_specs) + [_ANY] * r_in, out_specs=specs + [_ANY] * r_out,
        out_shape=shapes + list(rider.out_shape), scratch_shapes=list(scratch_shapes) + list(rider.sem_shapes),
        name=name, compiler_params=_cp(*(("arbitrary",) * len(grid))))

    def run(*args):
        res = call(*args, *rider.ins)
        own = res[:n_out]
        return (own[0] if single else own), res[n_out:]

    return run


def qkv_fwd(x, g, w, b, tm):
    s = x.shape[0]

    def body(x_ref, g_ref, w_ref, b_ref, h_ref, o_ref):
        h, _, _ = _rms_fwd(x_ref[...], g_ref[...])
        hb = h.astype(bf16)
        h_ref[...] = hb
        o_ref[...] = (_dot(hb, w_ref[...]) + b_ref[...]).astype(bf16)

    return pl.pallas_call(
        body, grid=(s // tm,), name="qkv_fwd",
        in_specs=[_row(tm, D), _const((1, D)), _const((D, QKV)), _const((1, QKV))],
        out_specs=[_row(tm, D), _row(tm, QKV)],
        out_shape=[SDS((s, D), bf16), SDS((s, QKV), bf16)],
        compiler_params=_cp("parallel"),
    )(x, g, w, b)


def _band_mask(i):
    qi = lax.broadcasted_iota(jnp.int32, (GROUP * BLK, 2 * BLK), 0) & (BLK - 1)
    ki = lax.broadcasted_iota(jnp.int32, (GROUP * BLK, 2 * BLK), 1)
    dist = qi + BLK - ki
    return (dist >= 0) & (dist < BLK) & ((ki >= BLK) | (i > 0))


_NEG = float(jnp.finfo(jnp.float32).min)
_NT = (((1,), (1,)), ((), ()))
_TN = (((0,), (0,)), ((), ()))


def _kv_heads(kvp_ref, kvc_ref, kvh):
    ks = slice(kvh * HD, (kvh + 1) * HD)
    vs = slice(N_KV * HD + kvh * HD, N_KV * HD + (kvh + 1) * HD)
    k = jnp.concatenate([kvp_ref[:, ks], kvc_ref[:, ks]], axis=0)
    v = jnp.concatenate([kvp_ref[:, vs], kvc_ref[:, vs]], axis=0)
    return k, v


def _stack_group(ref, kvh, width=HD):
    return jnp.concatenate([ref[:, (kvh * GROUP + gi) * width:(kvh * GROUP + gi + 1) * width] for gi in range(GROUP)],
                           axis=0)


def _group_sinks(sink_ref, kvh):
    row = lax.broadcasted_iota(jnp.int32, (GROUP * BLK, 1), 0)
    col = jnp.zeros((GROUP * BLK, 1), f32)
    for g in range(GROUP):
        col = jnp.where((row >= g * BLK) & (row < (g + 1) * BLK), sink_ref[kvh * GROUP + g], col)
    return col


def attn_fwd(qkv, sinks, rider=None):
    s = qkv.shape[0]
    scale = 1.0 / math.sqrt(HD)

    def body(q_ref, kvc_ref, kvp_ref, sink_ref, o_ref, lse_ref):
        valid = _band_mask(pl.program_id(0))
        for kvh in range(N_KV):
            k, v = _kv_heads(kvp_ref, kvc_ref, kvh)
            sc = lax.dot_general(_stack_group(q_ref, kvh), k, _NT, preferred_element_type=f32) * scale
            sc = jnp.where(valid, sc, _NEG)
            sink = _group_sinks(sink_ref, kvh)
            m = jnp.maximum(jnp.max(sc, axis=-1, keepdims=True), sink)
            p = jnp.exp(sc - m)
            denom = jnp.sum(p, axis=-1, keepdims=True) + jnp.exp(sink - m)
            og = _dot((p / denom).astype(bf16), v).astype(bf16)
            lse = m + jnp.log(denom)
            for gi in range(GROUP):
                h = kvh * GROUP + gi
                o_ref[:, h * HD:(h + 1) * HD] = og[gi * BLK:(gi + 1) * BLK]
                lse_ref[:, h:h + 1] = lse[gi * BLK:(gi + 1) * BLK]

    return _hosted(
        rider, body, grid=(s // BLK,), name="attn_fwd",
        in_specs=[
            pl.BlockSpec((BLK, N_HEADS * HD), lambda i: (i, 0)),
            pl.BlockSpec((BLK, 2 * N_KV * HD), lambda i: (i, KV_COL_BLOCK)),
            pl.BlockSpec((BLK, 2 * N_KV * HD), lambda i: (jnp.maximum(i - 1, 0), KV_COL_BLOCK)),
            pl.BlockSpec(memory_space=pltpu.SMEM),
        ],
        out_specs=[_row(BLK, D), _row(BLK, N_HEADS)],
        out_shape=[SDS((s, D), bf16), SDS((s, N_HEADS), f32)],
        compiler_params=_cp("parallel"),
    )(qkv, qkv, qkv, sinks)


def attn_out_fwd(x, o, w, b, tm):
    s = x.shape[0]

    def body(x_ref, o_ref, w_ref, b_ref, y_ref):
        y_ref[...] = x_ref[...] + _dot(o_ref[...], w_ref[...]) + b_ref[...]

    return pl.pallas_call(
        body, grid=(s // tm,), name="attn_out_fwd",
        in_specs=[_row(tm, D), _row(tm, D), _const((D, D)), _const((1, D))],
        out_specs=_row(tm, D), out_shape=SDS((s, D), f32),
        compiler_params=_cp("parallel"),
    )(x, o, w, b)


def attn_out_bwd(dy, wt, tm, rider=None):
    s = dy.shape[0]

    def body(dy_ref, wt_ref, do_ref, db_ref):
        @pl.when(pl.program_id(0) == 0)
        def _():
            db_ref[...] = jnp.zeros_like(db_ref)

        dy = dy_ref[...]
        do_ref[...] = _dot(dy.astype(bf16), wt_ref[...]).astype(bf16)
        db_ref[...] += _colsum8(dy)

    return _hosted(
        rider, body, grid=(s // tm,), name="attn_out_bwd",
        in_specs=[_row(tm, D), _const((D, D))],
        out_specs=[_row(tm, D), _acc((SUB, D))],
        out_shape=[SDS((s, D), bf16), SDS((SUB, D), f32)],
        compiler_params=_cp("arbitrary"),
    )(dy, wt)


def attn_bwd(qkv, o, do, lse, sinks, rider=None):
    s = qkv.shape[0]
    nb = s // BLK
    scale = 1.0 / math.sqrt(HD)
    kvw = 2 * N_KV * HD

    def body(q_ref, kvc_ref, kvp_ref, o_ref, do_ref, lse_ref, sink_ref, dq_ref, dkv_ref, ds_ref, carry_ref):
        i = pl.program_id(0)

        @pl.when(i == 0)
        def _():
            ds_ref[...] = jnp.zeros_like(ds_ref)
            carry_ref[...] = jnp.zeros_like(carry_ref)

        @pl.when(i < nb)
        def _():
            valid = _band_mask(i)
            for kvh in range(N_KV):
                k, v = _kv_heads(kvp_ref, kvc_ref, kvh)
                qg = _stack_group(q_ref, kvh)
                dog = _stack_group(do_ref, kvh)
                lse = _stack_group(lse_ref, kvh, 1)
                sc = lax.dot_general(qg, k, _NT, preferred_element_type=f32) * scale
                sc = jnp.where(valid, sc, _NEG)
                p = jnp.exp(sc - lse)
                dp = lax.dot_general(dog, v, _NT, preferred_element_type=f32)
                dlt = jnp.sum(dog.astype(f32) * _stack_group(o_ref, kvh).astype(f32), axis=-1, keepdims=True)
                dsc = (p * (dp - dlt)).astype(bf16)
                dqg = (_dot(dsc, k) * scale).astype(bf16)
                dk = lax.dot_general(dsc, qg, _TN, preferred_element_type=f32) * scale
                dv = lax.dot_general(p.astype(bf16), dog, _TN, preferred_element_type=f32)
                dsink = jnp.exp(_group_sinks(sink_ref, kvh) - lse) * dlt
                for gi in range(GROUP):
                    h = kvh * GROUP + gi
                    dq_ref[:, h * HD:(h + 1) * HD] = dqg[gi * BLK:(gi + 1) * BLK]
                    ds_ref[:, h:h + 1] += -jnp.sum(dsink[gi * BLK:(gi + 1) * BLK], axis=0, keepdims=True)
                ks = slice(kvh * HD, (kvh + 1) * HD)
                vs = slice(N_KV * HD + kvh * HD, N_KV * HD + (kvh + 1) * HD)
                dkv_ref[:, ks] = carry_ref[:, ks] + dk[:BLK]
                dkv_ref[:, vs] = carry_ref[:, vs] + dv[:BLK]
                carry_ref[:, ks] = dk[BLK:]
                carry_ref[:, vs] = dv[BLK:]

        @pl.when(i == nb)
        def _():
            dkv_ref[...] = carry_ref[...]

    cur = lambda i: (jnp.minimum(i, nb - 1), 0)
    prev = lambda i: (jnp.clip(i - 1, 0, nb - 1), KV_COL_BLOCK)
    return _hosted(
        rider, body, grid=(nb + 1,), name="attn_bwd",
        in_specs=[
            pl.BlockSpec((BLK, D), cur),
            pl.BlockSpec((BLK, kvw), lambda i: (jnp.minimum(i, nb - 1), KV_COL_BLOCK)),
            pl.BlockSpec((BLK, kvw), prev),
            pl.BlockSpec((BLK, D), cur),
            pl.BlockSpec((BLK, D), cur),
            pl.BlockSpec((BLK, N_HEADS), cur),
            pl.BlockSpec(memory_space=pltpu.SMEM),
        ],
        out_specs=[
            pl.BlockSpec((BLK, D), cur),
            pl.BlockSpec((BLK, kvw), lambda i: (jnp.maximum(i - 1, 0), 0)),
            _acc((1, N_HEADS)),
        ],
        out_shape=[SDS((s, D), bf16), SDS((s, kvw), f32), SDS((1, N_HEADS), f32)],
        scratch_shapes=[pltpu.VMEM((BLK, kvw), f32)],
        compiler_params=_cp("arbitrary"),
    )(qkv, qkv, qkv, o, do, lse, sinks)


def qkv_bwd(dq, dkv, x, g, wt, dres, tm):
    s = x.shape[0]
    qd = N_HEADS * HD
    kvw = 2 * N_KV * HD

    def body(dq_ref, dkv_ref, x_ref, g_ref, wt_ref, dres_ref, dx_ref, dg_ref, db_ref, dkvb_ref):
        @pl.when(pl.program_id(0) == 0)
        def _():
            dg_ref[...] = jnp.zeros_like(dg_ref)
            db_ref[...] = jnp.zeros_like(db_ref)

        dq = dq_ref[...]
        dkv = dkv_ref[...]
        dkvb = dkv.astype(bf16)
        dkvb_ref[...] = dkvb
        dh = _dot(dq, wt_ref[0:qd, :]) + _dot(dkvb, wt_ref[qd:QKV, :])
        g = g_ref[...]
        _, xn, r = _rms_fwd(x_ref[...], g)
        dx, dg = _rms_bwd(xn, r, g, dh)
        dx_ref[...] = dres_ref[...] + dx
        dg_ref[...] += dg
        db_ref[:, 0:qd] += _colsum8(dq.astype(f32))
        db_ref[:, qd:QKV] += _colsum8(dkv)

    return pl.pallas_call(
        body, grid=(s // tm,), name="qkv_bwd",
        in_specs=[_row(tm, qd), _row(tm, kvw), _row(tm, D), _const((1, D)), _const((QKV, D)), _row(tm, D)],
        out_specs=[_row(tm, D), _acc((SUB, D)), _acc((SUB, QKV)), _row(tm, kvw)],
        out_shape=[SDS((s, D), f32), SDS((SUB, D), f32), SDS((SUB, QKV), f32), SDS((s, kvw), bf16)],
        compiler_params=_cp("arbitrary"),
    )(dq, dkv, x, g, wt, dres)


def ffn_fwd(x, g, wup, wdw, bdw, wdn, tm, cw, rider=None, head=None):
    s = x.shape[0]
    tail = 8

    def body(x_ref, g_ref, wup_ref, wdw_ref, bdw_ref, wdn_ref, *refs):
        if head is None:
            xo_ref, h_ref, up_ref, gate_ref, act_ref, carry_ref, ext_ref = refs
        else:
            gf_ref, t_ref, xo_ref, h_ref, up_ref, gate_ref, act_ref, dgf_ref, loss_ref, carry_ref, ext_ref = refs

        @pl.when(pl.program_id(0) == 0)
        def _():
            carry_ref[...] = jnp.zeros_like(carry_ref)
            if head is not None:
                dgf_ref[...] = jnp.zeros_like(dgf_ref)
                loss_ref[...] = jnp.zeros_like(loss_ref)

        x = x_ref[...]
        h, _, _ = _rms_fwd(x, g_ref[...])
        hb = h.astype(bf16)
        h_ref[...] = hb
        for c in range(DFF // cw):
            cs = slice(c * cw, (c + 1) * cw)
            vs = slice(DFF + c * cw, DFF + (c + 1) * cw)
            ug = _dot(hb, wup_ref[:, cs])
            uv = _dot(hb, wup_ref[:, vs])
            up_ref[:, cs] = ug.astype(bf16)
            up_ref[:, vs] = uv.astype(bf16)
            ext_ref[0:tail, :] = carry_ref[:, cs]
            ext_ref[tail:tail + tm, :] = ug
            carry_ref[:, cs] = ug[tm - tail:, :]
            ext = ext_ref[...]
            gate = (wdw_ref[0:1, cs] * pltpu.roll(ext, 2, 0)[tail:] + wdw_ref[1:2, cs] * pltpu.roll(ext, 1, 0)[tail:]
                    + wdw_ref[2:3, cs] * ug) + bdw_ref[:, cs]
            gate_ref[:, cs] = gate.astype(bf16)
            act_ref[:, cs] = (gate * _sigmoid(gate) * uv).astype(bf16)
        xo = x + _dot(act_ref[...], wdn_ref[...])
        if head is None:
            xo_ref[...] = xo
        else:
            gf = gf_ref[...]
            y, xn, r = _rms_fwd(xo, gf)
            e = y - t_ref[...]
            loss_ref[...] += 0.5 * jnp.sum(jnp.mean(e * e, axis=-1, keepdims=True), axis=0, keepdims=True)
            dx, dgf = _rms_bwd(xn, r, gf, e / D)
            xo_ref[...] = dx
            dgf_ref[...] += dgf

    extra_in = [] if head is None else [_const((1, D)), _row(tm, D)]
    extra_out = [] if head is None else [_acc((SUB, D)), _acc((1, 1))]
    extra_shape = [] if head is None else [SDS((SUB, D), f32), SDS((1, 1), f32)]
    return _hosted(
        rider, body, grid=(s // tm,), name="ffn_fwd" if head is None else "ffn_fwd_loss",
        in_specs=[_row(tm, D), _const((1, D)), _const((D, 2 * DFF)), _const((3, DFF)), _const((1, DFF)),
                  _const((DFF, D))] + extra_in,
        out_specs=[_row(tm, D), _row(tm, D), _row(tm, 2 * DFF), _row(tm, DFF), _row(tm, DFF)] + extra_out,
        out_shape=[SDS((s, D), f32), SDS((s, D), bf16), SDS((s, 2 * DFF), bf16), SDS((s, DFF), bf16),
                   SDS((s, DFF), bf16)] + extra_shape,
        scratch_shapes=[pltpu.VMEM((tail, DFF), f32), pltpu.VMEM((tail + tm, cw), f32)],
        compiler_params=_cp("arbitrary"),
    )(x, g, wup, wdw, bdw, wdn, *(head or ()))


def ffn_bwd(dxo, x, g, up, gate, wdw, wdn, wup, tm, cw, rider=None):
    s = x.shape[0]
    nt = s // tm
    rev = lambda i: (nt - 1 - i, 0)

    def body(dxo_ref, x_ref, g_ref, up_ref, gate_ref, wdw_ref, wdn_ref, wup_ref,
             dxi_ref, dup_ref, dg_ref, dwdw_ref, dbdw_ref, carry_ref, ext2_ref):
        i = pl.program_id(0)

        @pl.when(i == 0)
        def _():
            carry_ref[...] = jnp.zeros_like(carry_ref)
            dg_ref[...] = jnp.zeros_like(dg_ref)
            dwdw_ref[...] = jnp.zeros_like(dwdw_ref)
            dbdw_ref[...] = jnp.zeros_like(dbdw_ref)

        dxo = dxo_ref[...]
        dxb = dxo.astype(bf16)
        for c in range(DFF // cw):
            cs = slice(c * cw, (c + 1) * cw)
            vs = slice(DFF + c * cw, DFF + (c + 1) * cw)
            d_act = lax.dot_general(dxb, wdn_ref[cs, :], _NT, preferred_element_type=f32)
            ug = up_ref[:, cs].astype(f32)
            uv = up_ref[:, vs].astype(f32)
            gate = gate_ref[:, cs].astype(f32)
            sg = _sigmoid(gate)
            dup_ref[:, vs] = (d_act * (gate * sg)).astype(bf16)
            d_gate = d_act * uv * _dsilu(gate, sg)
            ext2_ref[0:tm, :] = d_gate
            ext2_ref[tm:tm + 8, :] = carry_ref[:, cs]
            carry_ref[:, cs] = d_gate[0:8, :]
            ext = ext2_ref[...]
            ahead1 = pltpu.roll(ext, tm + 8 - 1, 0)[:tm]
            ahead2 = pltpu.roll(ext, tm + 8 - 2, 0)[:tm]
            dbdw_ref[:, cs] += _colsum8(d_gate)
            dwdw_ref[0, :, cs] += _colsum8(ahead2 * ug)
            dwdw_ref[1, :, cs] += _colsum8(ahead1 * ug)
            dwdw_ref[2, :, cs] += _colsum8(d_gate * ug)
            d_ug = wdw_ref[0:1, cs] * ahead2 + wdw_ref[1:2, cs] * ahead1 + wdw_ref[2:3, cs] * d_gate
            dup_ref[:, cs] = d_ug.astype(bf16)
        dh = lax.dot_general(dup_ref[...], wup_ref[...], _NT, preferred_element_type=f32)
        gv = g_ref[...]
        _, xn, r = _rms_fwd(x_ref[...], gv)
        dx, dg = _rms_bwd(xn, r, gv, dh)
        dxi_ref[...] = dxo + dx
        dg_ref[...] += dg

    return _hosted(
        rider, body, grid=(nt,), name="ffn_bwd",
        in_specs=[
            pl.BlockSpec((tm, D), rev), pl.BlockSpec((tm, D), rev), _const((1, D)),
            pl.BlockSpec((tm, 2 * DFF), rev), pl.BlockSpec((tm, DFF), rev),
            _const((3, DFF)), _const((DFF, D)), _const((D, 2 * DFF)),
        ],
        out_specs=[pl.BlockSpec((tm, D), rev), pl.BlockSpec((tm, 2 * DFF), rev), _acc((SUB, D)),
                   _acc((3, SUB, DFF)), _acc((SUB, DFF))],
        out_shape=[SDS((s, D), f32), SDS((s, 2 * DFF), bf16), SDS((SUB, D), f32), SDS((3, SUB, DFF), f32),
                   SDS((SUB, DFF), f32)],
        scratch_shapes=[pltpu.VMEM((8, DFF), f32), pltpu.VMEM((tm + 8, cw), f32)],
        compiler_params=_cp("arbitrary"),
    )(dxo, x, g, up, gate, wdw, wdn, wup)


def pw1_fwd(x, g, w, b, tm):
    s = x.shape[0]

    def body(x_ref, g_ref, w_ref, b_ref, h_ref, a_ref, u_ref):
        h, _, _ = _rms_fwd(x_ref[...], g_ref[...])
        hb = h.astype(bf16)
        h_ref[...] = hb
        a = _dot(hb, w_ref[...]) + b_ref[...]
        a_ref[...] = a.astype(bf16)
        u_ref[...] = a[:, :D] * _sigmoid(a[:, D:])

    return pl.pallas_call(
        body, grid=(s // tm,), name="pw1_fwd",
        in_specs=[_row(tm, D), _const((1, D)), _const((D, 2 * D)), _const((1, 2 * D))],
        out_specs=[_row(tm, D), _row(tm, 2 * D), _row(tm, D)],
        out_shape=[SDS((s, D), bf16), SDS((s, 2 * D), bf16), SDS((s, D), f32)],
        compiler_params=_cp("parallel"),
    )(x, g, w, b)


def _ln_silu(c, lg, lb):
    mu = jnp.mean(c, axis=-1, keepdims=True)
    cc = c - mu
    var = jnp.mean(cc * cc, axis=-1, keepdims=True)
    rstd = lax.rsqrt(var + LN_EPS)
    xh = cc * rstd
    ln = xh * lg + lb
    sg = _sigmoid(ln)
    return xh, rstd, ln, sg


def _shifted_copies(ext_ref, sh_ref, cs, tm):
    n = CONV_HALO - SUB + tm
    for k in range(1, SUB):
        sh_ref[k - 1] = ext_ref[pl.ds(k, n), cs]


def _shifted_rows(ext_ref, sh_ref, cs, start, rows):
    q, k = divmod(start, SUB)
    if k == 0:
        return ext_ref[pl.ds(start, rows), cs]
    return sh_ref[k - 1, pl.ds(q * SUB, rows), :]


def conv_fwd(u, x, wdw, bdw, lg, lb, w2, b2, tm, rc):
    s = x.shape[0]
    hl = CONV_HALO
    off = hl - (CONV_W - 1)

    def body(u_ref, halo_ref, x_ref, wdw_ref, bdw_ref, lg_ref, lb_ref, w2_ref, b2_ref, c_ref, xo_ref, ext_ref, sh_ref):
        has_prev = (pl.program_id(0) > 0).astype(f32)
        ext_ref[0:hl, :] = halo_ref[...] * has_prev
        ext_ref[hl:hl + tm, :] = u_ref[...]
        for cc in range(D // LANES):
            cs = slice(cc * LANES, (cc + 1) * LANES)
            _shifted_copies(ext_ref, sh_ref, cs, tm)
            for rr in range(tm // rc):
                acc = jnp.zeros((rc, LANES), f32) + bdw_ref[:, cs]
                for j in range(CONV_W):
                    acc = acc + wdw_ref[j:j + 1, cs] * _shifted_rows(ext_ref, sh_ref, cs, rr * rc + off + j, rc)
                c_ref[rr * rc:(rr + 1) * rc, cs] = acc
        _, _, ln, sg = _ln_silu(c_ref[...], lg_ref[...], lb_ref[...])
        xo_ref[...] = x_ref[...] + _dot((ln * sg).astype(bf16), w2_ref[...]) + b2_ref[...]

    return pl.pallas_call(
        body, grid=(s // tm,), name="conv_fwd",
        in_specs=[_row(tm, D), pl.BlockSpec((hl, D), lambda i: (jnp.maximum(i * (tm // hl) - 1, 0), 0)), _row(tm, D),
                  _const((CONV_W, D)), _const((1, D)), _const((1, D)), _const((1, D)), _const((D, D)), _const((1, D))],
        out_specs=[_row(tm, D), _row(tm, D)],
        out_shape=[SDS((s, D), f32), SDS((s, D), f32)],
        scratch_shapes=[pltpu.VMEM((hl + tm, D), f32), pltpu.VMEM((SUB - 1, hl - SUB + tm, LANES), f32)],
        compiler_params=_cp("parallel"),
    )(u, u, x, wdw, bdw, lg, lb, w2, b2)


def conv_bwd_a(dy, c, lg, lb, w2t, tm, rider=None):
    s = dy.shape[0]

    def body(dy_ref, c_ref, lg_ref, lb_ref, w2t_ref, dc_ref, u3_ref, dlg_ref, dlb_ref, db2_ref, dbdw_ref):
        @pl.when(pl.program_id(0) == 0)
        def _():
            for r in (dlg_ref, dlb_ref, db2_ref, dbdw_ref):
                r[...] = jnp.zeros_like(r)

        dy = dy_ref[...]
        lg = lg_ref[...]
        xh, rstd, ln, sg = _ln_silu(c_ref[...], lg, lb_ref[...])
        u3_ref[...] = (ln * sg).astype(bf16)
        du3 = _dot(dy.astype(bf16), w2t_ref[...])
        dln = du3 * _dsilu(ln, sg)
        dxh = dln * lg
        dc = rstd * (dxh - jnp.mean(dxh, axis=-1, keepdims=True) - xh * jnp.mean(dxh * xh, axis=-1, keepdims=True))
        dc_ref[...] = dc
        dlg_ref[...] += _colsum8(dln * xh)
        dlb_ref[...] += _colsum8(dln)
        db2_ref[...] += _colsum8(dy)
        dbdw_ref[...] += _colsum8(dc)

    return _hosted(
        rider, body, grid=(s // tm,), name="conv_bwd_a",
        in_specs=[_row(tm, D), _row(tm, D), _const((1, D)), _const((1, D)), _const((D, D))],
        out_specs=[_row(tm, D), _row(tm, D)] + [_acc((SUB, D))] * 4,
        out_shape=[SDS((s, D), f32), SDS((s, D), bf16)] + [SDS((SUB, D), f32)] * 4,
        compiler_params=_cp("arbitrary"),
    )(dy, c, lg, lb, w2t)


def conv_bwd_b(dc, u, a, wdw, x, g, w1t, dres, tm, rc, rider=None):
    s = x.shape[0]
    nt = s // tm
    hl = CONV_HALO
    off = hl - (CONV_W - 1)

    def body(dc_ref, dnext_ref, u_ref, uprev_ref, a_ref, wdw_ref, x_ref, g_ref, w1t_ref, dres_ref,
             dx_ref, da_ref, dwdw_ref, db1_ref, dg_ref, ext_ref, ext2_ref, du_ref, sh_ref, sh2_ref):
        i = pl.program_id(0)

        @pl.when(i == 0)
        def _():
            for r in (dwdw_ref, db1_ref, dg_ref):
                r[...] = jnp.zeros_like(r)

        ext_ref[0:hl, :] = uprev_ref[...] * (i > 0).astype(f32)
        ext_ref[hl:hl + tm, :] = u_ref[...]
        ext2_ref[0:tm, :] = dc_ref[...]
        ext2_ref[tm:tm + hl, :] = dnext_ref[...] * (i < nt - 1).astype(f32)
        for cc in range(D // LANES):
            cs = slice(cc * LANES, (cc + 1) * LANES)
            _shifted_copies(ext_ref, sh_ref, cs, tm)
            _shifted_copies(ext2_ref, sh2_ref, cs, tm)
            for rr in range(tm // rc):
                r0 = rr * rc
                dcb = ext2_ref[r0:r0 + rc, cs]
                acc = jnp.zeros((rc, LANES), f32)
                for j in range(CONV_W):
                    acc = acc + wdw_ref[j:j + 1, cs] * _shifted_rows(ext2_ref, sh2_ref, cs, r0 + CONV_W - 1 - j, rc)
                    dwdw_ref[j, :, cs] += _colsum8(dcb * _shifted_rows(ext_ref, sh_ref, cs, r0 + off + j, rc))
                du_ref[r0:r0 + rc, cs] = acc
        du = du_ref[...]
        a1 = a_ref[:, :D].astype(f32)
        sg = _sigmoid(a_ref[:, D:].astype(f32))
        da1 = du * sg
        da2 = du * a1 * sg * (1.0 - sg)
        da_ref[:, :D] = da1.astype(bf16)
        da_ref[:, D:] = da2.astype(bf16)
        db1_ref[:, :D] += _colsum8(da1)
        db1_ref[:, D:] += _colsum8(da2)
        dh = _dot(da_ref[...], w1t_ref[...])
        gv = g_ref[...]
        _, xn, r = _rms_fwd(x_ref[...], gv)
        dx, dg = _rms_bwd(xn, r, gv, dh)
        dx_ref[...] = dres_ref[...] + dx
        dg_ref[...] += dg

    blocks = tm // hl
    return _hosted(
        rider, body, grid=(nt,), name="conv_bwd_b",
        in_specs=[
            _row(tm, D), pl.BlockSpec((hl, D), lambda i: (jnp.minimum((i + 1) * blocks, s // hl - 1), 0)),
            _row(tm, D), pl.BlockSpec((hl, D), lambda i: (jnp.maximum(i * blocks - 1, 0), 0)),
            _row(tm, 2 * D), _const((CONV_W, D)), _row(tm, D), _const((1, D)), _const((2 * D, D)), _row(tm, D),
        ],
        out_specs=[_row(tm, D), _row(tm, 2 * D), _acc((CONV_W, SUB, D)), _acc((SUB, 2 * D)), _acc((SUB, D))],
        out_shape=[SDS((s, D), f32), SDS((s, 2 * D), bf16), SDS((CONV_W, SUB, D), f32), SDS((SUB, 2 * D), f32),
                   SDS((SUB, D), f32)],
        scratch_shapes=[pltpu.VMEM((hl + tm, D), f32), pltpu.VMEM((tm + hl, D), f32), pltpu.VMEM((tm, D), f32),
                        pltpu.VMEM((SUB - 1, hl - SUB + tm, LANES), f32),
                        pltpu.VMEM((SUB - 1, hl - SUB + tm, LANES), f32)],
        compiler_params=_cp("arbitrary"),
    )(dc, dc, u, u, a, wdw, x, g, w1t, dres)


def wgrad(a, b, nb, tk, name, rider=None):
    s, k1 = a.shape
    n = b.shape[1]

    def body(a_ref, b_ref, o_ref):
        @pl.when(pl.program_id(1) == 0)
        def _():
            o_ref[...] = jnp.zeros_like(o_ref)

        o_ref[...] += lax.dot_general(a_ref[...], b_ref[...].astype(bf16), _TN, preferred_element_type=f32)

    return _hosted(
        rider, body, grid=(n // nb, s // tk), name=name,
        in_specs=[pl.BlockSpec((tk, k1), lambda j, k: (k, 0)), pl.BlockSpec((tk, nb), lambda j, k: (k, j))],
        out_specs=pl.BlockSpec((k1, nb), lambda j, k: (0, j)),
        out_shape=SDS((k1, n), f32),
        compiler_params=_cp("parallel", "arbitrary"),
    )(a, b)


def wgrad_cols(a, b, tk, name, rider=None):
    s, k1 = a.shape
    w = b.shape[1] // N_CHIPS

    def body(a_ref, b_ref, o_ref):
        @pl.when(pl.program_id(1) == 0)
        def _():
            o_ref[...] = jnp.zeros_like(o_ref)

        acc = lax.dot_general(a_ref[...], b_ref[...].astype(bf16), _TN, preferred_element_type=f32)
        o_ref[:, 0] += acc.reshape(2, k1 // 2, w)

    return _hosted(
        rider, body, grid=(N_CHIPS, s // tk), name=name,
        in_specs=[pl.BlockSpec((tk, k1), lambda j, k: (k, 0)), pl.BlockSpec((tk, w), lambda j, k: (k, j))],
        out_specs=pl.BlockSpec((2, 1, k1 // 2, w), lambda j, k: (0, j, 0, 0)),
        out_shape=SDS((2, N_CHIPS, k1 // 2, w), f32),
        compiler_params=_cp("parallel", "arbitrary"),
    )(a, b)


def wgrad_rows(a, b, nb, tk, name, rider=None):
    s, k1 = a.shape
    n = b.shape[1]
    r = k1 // (2 * N_CHIPS)

    def body(a_ref, b_ref, o_ref):
        @pl.when(pl.program_id(1) == 0)
        def _():
            o_ref[...] = jnp.zeros_like(o_ref)

        acc = lax.dot_general(a_ref[...], b_ref[...].astype(bf16), _TN, preferred_element_type=f32)
        for j in range(N_CHIPS):
            for h in range(2):
                o_ref[h, j] += acc[(2 * j + h) * r:(2 * j + h + 1) * r, :]

    return _hosted(
        rider, body, grid=(n // nb, s // tk), name=name,
        in_specs=[pl.BlockSpec((tk, k1), lambda j, k: (k, 0)), pl.BlockSpec((tk, nb), lambda j, k: (k, j))],
        out_specs=pl.BlockSpec((2, N_CHIPS, r, nb), lambda j, k: (0, 0, 0, j)),
        out_shape=SDS((2, N_CHIPS, r, n), f32),
        compiler_params=_cp("parallel", "arbitrary"),
    )(a, b)


def _adam_math(w, g, m, v):
    m = B1 * m + (1.0 - B1) * g
    v = B2 * v + (1.0 - B2) * (g * g)
    m_hat = m / (1.0 - B1 ** STEP)
    v_hat = v / (1.0 - B2 ** STEP)
    delta = -LR * (m_hat / (jnp.sqrt(v_hat) + ADAM_EPS) + WD * w)
    return delta, m, v


def _rows_tile(r, c, multiple=SUB):
    best = None
    for t in range(multiple, r + 1, multiple):
        if r % t == 0 and t * c * 4 <= ELEMENTWISE_BLOCK_BYTES:
            best = t
    return best if best is not None else r


def adamw(w, g, m, v, name):
    l, r, c = w.shape
    tr = _rows_tile(r, c)
    spec = pl.BlockSpec((1, tr, c), lambda i, j: (i, j, 0))

    def body(w_ref, g_ref, m_ref, v_ref, d_ref, mo_ref, vo_ref):
        d, mn, vn = _adam_math(w_ref[...], g_ref[...], m_ref[...], v_ref[...])
        d_ref[...] = d
        mo_ref[...] = mn
        vo_ref[...] = vn

    return pl.pallas_call(
        body, grid=(l, r // tr), name=name, in_specs=[spec] * 4, out_specs=[spec] * 3,
        out_shape=[SDS((l, r, c), f32)] * 3, compiler_params=_cp("parallel", "parallel"),
    )(w, g, m, v)


def _place():
    return lax.axis_index("x"), lax.axis_index("y"), lax.axis_index("c")


def _chip_peer(xi, yi, r):
    px = 1 - xi if r & 2 else xi
    py = 1 - yi if r & 1 else yi
    return px, py


def _gv_qkv(src, dst, j, h):
    rows = pl.ds(h * (D // 2), D // 2)
    return src.at[rows, :], dst.at[j, rows, :]


def _gv_rows(src, dst, j, h):
    r = src.shape[0] // 2
    return src.at[pl.ds(h * r, r), :], dst.at[pl.ds(j * 2 * r + h * r, r), :]


def _gv_cols(src, dst, j, h):
    r, w = src.shape[0] // 2, src.shape[1]
    return src.at[pl.ds(h * r, r), :], dst.at[pl.ds(h * r, r), pl.ds(j * w, w)]


def gather_rider(big, small, forward_at):
    nb, ns = len(big), len(small)
    n = nb + ns
    views = [v for _, v, _ in big]

    def env(ins, outs, sems):
        ici_send, ici_recv, d2d_send, d2d_recv, loc_sems = sems
        xi, yi, ci = _place()
        me = 2 * xi + yi

        def local(a, h):
            if a < nb:
                src, dst = views[a](ins[a], outs[a], me, h)
                return pltpu.make_async_copy(src, dst, loc_sems.at[2 * a + h])
            return pltpu.make_async_copy(ins[a], outs[a].at[me], loc_sems.at[nb + a])

        def ici(a, r, slot):
            px, py = _chip_peer(xi, yi, r)
            src, dst = views[a](ins[a], outs[a], slot, ci) if a < nb else (ins[a], outs[a].at[slot])
            k = 3 * a + r - 1
            return pltpu.make_async_remote_copy(src_ref=src, dst_ref=dst, send_sem=ici_send.at[k],
                                                recv_sem=ici_recv.at[k], device_id=(px, py, ci), device_id_type=MESH)

        def d2d(a, r, half):
            px, py = _chip_peer(xi, yi, r)
            _, dst = views[a](ins[a], outs[a], 2 * px + py, half)
            k = 3 * a + r - 1
            return pltpu.make_async_remote_copy(src_ref=dst, dst_ref=dst, send_sem=d2d_send.at[k],
                                                recv_sem=d2d_recv.at[k], device_id=(xi, yi, 1 - ci), device_id_type=MESH)

        return xi, yi, ci, me, local, ici, d2d

    def locals_of():
        return [(a, h) for a in range(nb) for h in range(2)] + [(a, 0) for a in range(nb, n)]

    def send(ins, outs, sems):
        _, _, _, me, local, ici, _ = env(ins, outs, sems)
        for a, h in locals_of():
            local(a, h).start()
        for a in range(n):
            for r in (1, 2, 3):
                ici(a, r, me).start()

    def forward(ins, outs, sems):
        xi, yi, ci, _, _, ici, d2d = env(ins, outs, sems)
        for a in range(n):
            for r in (1, 2, 3):
                px, py = _chip_peer(xi, yi, r)
                ici(a, r, 2 * px + py).wait_recv()
                if a < nb:
                    d2d(a, r, ci).start()

    def finish(ins, outs, sems):
        _, _, ci, me, local, ici, d2d = env(ins, outs, sems)
        for a in range(nb):
            for r in (1, 2, 3):
                d2d(a, r, 1 - ci).wait_recv()
        for a in range(n):
            for r in (1, 2, 3):
                ici(a, r, me).wait_send()
                if a < nb:
                    d2d(a, r, ci).wait_send()
        for a, h in locals_of():
            local(a, h).wait()

    dma = pltpu.SemaphoreType.DMA
    return Rider(
        ins=[b for b, _, _ in big] + list(small),
        out_shape=[SDS(shape, bf16) for _, _, shape in big] + [SDS((N_CHIPS,) + a.shape, a.dtype) for a in small],
        sem_shapes=[dma((3 * n,)), dma((3 * n,)), dma((max(3 * nb, 1),)), dma((max(3 * nb, 1),)), dma((2 * nb + ns,))],
        stages=[(0.0, send), (forward_at, forward)], final=finish)


def pair_send_rider(gs):
    n = len(gs)

    def copy(ins, outs, sems, a):
        xi, yi, ci = _place()
        return pltpu.make_async_remote_copy(
            src_ref=ins[a].at[1 - ci], dst_ref=outs[a], send_sem=sems[0].at[a], recv_sem=sems[1].at[a],
            device_id=(xi, yi, 1 - ci), device_id_type=MESH)

    def send(ins, outs, sems):
        for a in range(n):
            copy(ins, outs, sems, a).start()

    def finish(ins, outs, sems):
        for a in range(n):
            copy(ins, outs, sems, a).wait()

    dma = pltpu.SemaphoreType.DMA
    return Rider(ins=list(gs), out_shape=[SDS(g.shape[1:], g.dtype) for g in gs], sem_shapes=[dma((n,)), dma((n,))],
                 stages=[(0.0, send)], final=finish)


def chip_rider(pbs, p32s, forward_at):
    n = len(pbs)

    def env(ins, outs, sems):
        ici_send, ici_recv, d2d_send, d2d_recv, own_send, own_recv, loc_sems = sems
        pb, p32, recv, own = ins[:n], ins[n:], outs[:n], outs[n:]
        xi, yi, ci = _place()
        me = 2 * xi + yi
        sib = (xi, yi, 1 - ci)

        def ici(a, r, src_slot, dst_slot):
            px, py = _chip_peer(xi, yi, r)
            k = 3 * a + r - 1
            return pltpu.make_async_remote_copy(
                src_ref=pb[a].at[src_slot], dst_ref=recv[a].at[ci, dst_slot], send_sem=ici_send.at[k],
                recv_sem=ici_recv.at[k], device_id=(px, py, ci), device_id_type=MESH)

        def d2d(a, r, half):
            px, py = _chip_peer(xi, yi, r)
            blk = recv[a].at[half, 2 * px + py]
            k = 3 * a + r - 1
            return pltpu.make_async_remote_copy(src_ref=blk, dst_ref=blk, send_sem=d2d_send.at[k],
                                                recv_sem=d2d_recv.at[k], device_id=sib, device_id_type=MESH)

        def mine(a, half):
            return pltpu.make_async_remote_copy(src_ref=p32[a].at[me], dst_ref=own[a].at[half], send_sem=own_send.at[a],
                                                recv_sem=own_recv.at[a], device_id=sib, device_id_type=MESH)

        def local(a):
            return pltpu.make_async_copy(p32[a].at[me], own[a].at[ci], loc_sems.at[a])

        return xi, yi, ci, me, ici, d2d, mine, local

    def send(ins, outs, sems):
        xi, yi, ci, me, ici, _, mine, local = env(ins, outs, sems)
        for a in range(n):
            local(a).start()
            mine(a, ci).start()
            for r in (1, 2, 3):
                px, py = _chip_peer(xi, yi, r)
                ici(a, r, 2 * px + py, me).start()

    def forward(ins, outs, sems):
        xi, yi, ci, me, ici, d2d, _, _ = env(ins, outs, sems)
        for a in range(n):
            for r in (1, 2, 3):
                px, py = _chip_peer(xi, yi, r)
                ici(a, r, me, 2 * px + py).wait_recv()
                d2d(a, r, ci).start()

    def finish(ins, outs, sems):
        xi, yi, ci, me, ici, d2d, mine, local = env(ins, outs, sems)
        for a in range(n):
            mine(a, 1 - ci).wait_recv()
            for r in (1, 2, 3):
                d2d(a, r, 1 - ci).wait_recv()
        for a in range(n):
            mine(a, ci).wait_send()
            local(a).wait()
            for r in (1, 2, 3):
                px, py = _chip_peer(xi, yi, r)
                ici(a, r, 2 * px + py, me).wait_send()
                d2d(a, r, ci).wait_send()

    dma = pltpu.SemaphoreType.DMA
    return Rider(
        ins=list(pbs) + list(p32s),
        out_shape=[SDS((2,) + p.shape, bf16) for p in pbs] + [SDS((2,) + p.shape[1:], f32) for p in p32s],
        sem_shapes=[dma((3 * n,)), dma((3 * n,)), dma((3 * n,)), dma((3 * n,)), dma((n,)), dma((n,)), dma((n,))],
        stages=[(0.0, send), (forward_at, forward)], final=finish)


def rs_pair_add(g, other, ci, name):
    _, nsh, r, w = g.shape
    tr = _rows_tile(r, w, 16)

    def body(c_ref, g_ref, o_ref, p_ref, pb_ref):
        p = g_ref[0] + o_ref[...]
        p_ref[...] = p
        pb_ref[...] = p.astype(bf16)

    blk = pl.BlockSpec((1, tr, w), lambda j, i, c: (j, i, 0))
    return pl.pallas_call(
        body, name=name,
        grid_spec=pltpu.PrefetchScalarGridSpec(
            num_scalar_prefetch=1, grid=(nsh, r // tr),
            in_specs=[pl.BlockSpec((1, 1, tr, w), lambda j, i, c: (c[0], j, i, 0)), blk], out_specs=[blk, blk]),
        out_shape=[SDS((nsh, r, w), f32), SDS((nsh, r, w), bf16)],
        compiler_params=_cp("parallel", "parallel"),
    )(ci.reshape(1).astype(jnp.int32), g, other)


def rs_chip_add(own, recv, chip, name, layer=0, layers=1, into=None):
    _, nsh, r, w = recv.shape
    tr = _rows_tile(r, w, 16)

    def body(c_ref, own_ref, recv_ref, *refs):
        o_ref = refs[-1]
        me = c_ref[0]
        acc = None
        for j in range(N_CHIPS):
            term = jnp.where(me == j, own_ref[0], recv_ref[0, j].astype(f32))
            acc = term if acc is None else acc + term
        o_ref[0, 0] = acc

    args = (chip.reshape(1).astype(jnp.int32), own, recv) + (() if into is None else (into,))
    return pl.pallas_call(
        body, name=name,
        grid_spec=pltpu.PrefetchScalarGridSpec(
            num_scalar_prefetch=1, grid=(2, r // tr),
            in_specs=[pl.BlockSpec((1, tr, w), lambda h, i, c: (h, i, 0)),
                      pl.BlockSpec((1, nsh, tr, w), lambda h, i, c: (h, 0, i, 0))] + ([] if into is None else [_ANY]),
            out_specs=pl.BlockSpec((1, 1, tr, w), lambda h, i, c: (layer, h, i, 0))),
        out_shape=SDS((layers, 2, r, w), f32),
        input_output_aliases={} if into is None else {3: 0},
        compiler_params=_cp("parallel", "parallel"),
    )(*args)


class GradReduction:
    def __init__(self, keys, grads, ci, chip):
        self.keys, self.grads, self.ci, self.chip = keys, grads, ci, chip

    def pair_rider(self):
        return pair_send_rider(self.grads)

    def chip_rider(self, from_pair, forward_at):
        both = [rs_pair_add(g, o, self.ci, "rs_pair_add_" + k) for k, g, o in zip(self.keys, self.grads, from_pair)]
        return chip_rider([pb for _, pb in both], [p for p, _ in both], forward_at)

    def result(self, landed, place=None):
        n = len(self.keys)
        return {k: rs_chip_add(own, recv, self.chip, "rs_chip_add_" + k, *(place or {}).get(k, ()))
                for k, recv, own in zip(self.keys, landed[:n], landed[n:])}

    def alone(self):
        from_pair = run_rider(self.pair_rider(), "rs_pair_" + self.keys[0])
        return self.result(run_rider(self.chip_rider(from_pair, 0.0), "rs_chip_" + self.keys[0]))


STAGE_W = DFF
_ST = {"norm_mix0": 0, "norm_mix1": 1, "attn_b_qkv": 2, "attn_sinks": 3, "attn_b_o": 4, "final_norm": 5, "loss": 6,
       "norm_ffn0": 8, "norm_ffn1": 9, "ffn_b_dw0": 10, "ffn_b_dw1": 11, "conv_b_pw1": 12, "conv_b_dw": 13,
       "conv_ln_g": 14, "conv_ln_b": 15, "conv_b_pw2": 16, "ffn_w_dw0": 17, "ffn_w_dw1": 20, "conv_w_dw": 24}
STAGE_ROWS = 56
SMALL_REP = ("norm_mix", "attn_b_qkv", "attn_sinks", "attn_b_o", "norm_ffn", "ffn_b_dw", "final_norm")
SMALL_SH = ("conv_b_pw1", "conv_w_dw", "conv_b_dw", "conv_ln_g", "conv_ln_b", "conv_b_pw2", "ffn_w_dw")
_SMALL_PARTS = ("norm_mix0", "norm_mix1", "attn_b_qkv", "attn_sinks", "attn_b_o", "norm_ffn0", "norm_ffn1", "ffn_b_dw0",
                "ffn_b_dw1", "final_norm", "conv_b_pw1", "conv_b_dw", "conv_ln_g", "conv_ln_b", "conv_b_pw2", "loss",
                "ffn_w_dw0", "ffn_w_dw1", "conv_w_dw")


def small_reduce_adamw(parts, w, m, v):
    names = SMALL_REP + SMALL_SH
    npart, nw = len(_SMALL_PARTS), len(names)

    def body(*refs):
        part = dict(zip(_SMALL_PARTS, refs[:npart]))
        off = npart
        w_ref = dict(zip(names, refs[off:off + nw]))
        m_ref = dict(zip(names, refs[off + nw:off + 2 * nw]))
        v_ref = dict(zip(names, refs[off + 2 * nw:off + 3 * nw]))
        off += 3 * nw
        loss_ref = refs[off]
        g_out = dict(zip(names, refs[off + 1:off + 1 + nw]))
        d_out = dict(zip(names, refs[off + 1 + nw:off + 1 + 2 * nw]))
        m_out = dict(zip(names, refs[off + 1 + 2 * nw:off + 1 + 3 * nw]))
        v_out = dict(zip(names, refs[off + 1 + 3 * nw:off + 1 + 4 * nw]))
        stage_ref, buf_ref, tot_ref, send_sems, recv_sems = refs[off + 1 + 4 * nw:]

        xi, yi, ci = _place()
        me = 4 * xi + 2 * yi + ci
        chip = 2 * xi + yi

        stage_ref[...] = jnp.zeros_like(stage_ref)
        for name in _SMALL_PARTS:
            ref, r0 = part[name], _ST[name]
            if name in ("attn_sinks", "loss"):
                val = ref[...]
            elif name in ("ffn_w_dw0", "ffn_w_dw1", "conv_w_dw"):
                val = jnp.sum(ref[...], axis=1)
            else:
                val = jnp.sum(ref[...], axis=0, keepdims=True)
            stage_ref[r0:r0 + val.shape[0], 0:val.shape[1]] = val

        buf_ref[me] = stage_ref[...]

        def peer(r):
            px, py = _chip_peer(xi, yi, r >> 1)
            return px, py, (1 - ci if r & 1 else ci)

        def copy(r, slot):
            return pltpu.make_async_remote_copy(
                src_ref=stage_ref, dst_ref=buf_ref.at[slot], send_sem=send_sems.at[r - 1], recv_sem=recv_sems.at[r - 1],
                device_id=peer(r), device_id_type=MESH)

        sends = []
        for r in range(1, N_DEV):
            cp = copy(r, me)
            cp.start()
            sends.append(cp)
        for r in range(1, N_DEV):
            px, py, pc = peer(r)
            copy(r, 4 * px + 2 * py + pc).wait_recv()
        for cp in sends:
            cp.wait_send()
        acc = buf_ref[0]
        for d in range(1, N_DEV):
            acc = acc + buf_ref[d]
        tot_ref[...] = acc

        def rows(name, n, width):
            r0 = _ST[name]
            return tot_ref[r0:r0 + n, 0:width]

        def mine(name, n, width):
            r0 = _ST[name]
            out = tot_ref[r0:r0 + n, 0:width]
            for j in range(1, N_CHIPS):
                out = jnp.where(chip == j, tot_ref[r0:r0 + n, j * width:(j + 1) * width], out)
            return out

        loss_ref[...] = rows("loss", 1, 1)
        grads = {
            "norm_mix": rows("norm_mix0", 2, D), "attn_b_qkv": rows("attn_b_qkv", 1, QKV),
            "attn_sinks": rows("attn_sinks", 1, N_HEADS), "attn_b_o": rows("attn_b_o", 1, D),
            "norm_ffn": rows("norm_ffn0", 2, D), "ffn_b_dw": rows("ffn_b_dw0", 2, DFF),
            "final_norm": rows("final_norm", 1, D),
            "conv_b_pw1": mine("conv_b_pw1", 1, 2 * D // N_CHIPS), "conv_w_dw": mine("conv_w_dw", CONV_W, D // N_CHIPS),
            "conv_b_dw": mine("conv_b_dw", 1, D // N_CHIPS), "conv_ln_g": mine("conv_ln_g", 1, D // N_CHIPS),
            "conv_ln_b": mine("conv_ln_b", 1, D // N_CHIPS), "conv_b_pw2": mine("conv_b_pw2", 1, D // N_CHIPS),
        }
        for name in names:
            if name == "ffn_w_dw":
                continue
            at = 0 if name == "conv_w_dw" else Ellipsis
            g = grads[name]
            d, mn, vn = _adam_math(w_ref[name][at], g, m_ref[name][at], v_ref[name][at])
            g_out[name][at] = g
            d_out[name][at] = d
            m_out[name][at] = mn
            v_out[name][at] = vn
        for layer, key in enumerate(("ffn_w_dw0", "ffn_w_dw1")):
            g = mine(key, 3, DFF // N_CHIPS)
            d, mn, vn = _adam_math(w_ref["ffn_w_dw"][layer], g, m_ref["ffn_w_dw"][layer], v_ref["ffn_w_dw"][layer])
            g_out["ffn_w_dw"][layer] = g
            d_out["ffn_w_dw"][layer] = d
            m_out["ffn_w_dw"][layer] = mn
            v_out["ffn_w_dw"][layer] = vn

    ins = [parts[k] for k in _SMALL_PARTS] + [src[k] for src in (w, m, v) for k in names]
    wshapes = [SDS(w[k].shape, f32) for k in names]
    outs = pl.pallas_call(
        body, name="small_reduce_adamw", in_specs=[_VMEM] * len(ins), out_specs=[_VMEM] * (1 + 4 * nw),
        out_shape=[SDS((1, 1), f32)] + wshapes * 4,
        scratch_shapes=[pltpu.VMEM((STAGE_ROWS, STAGE_W), f32), pltpu.VMEM((N_DEV, STAGE_ROWS, STAGE_W), f32),
                        pltpu.VMEM((STAGE_ROWS, STAGE_W), f32), pltpu.SemaphoreType.DMA((N_DEV - 1,)),
                        pltpu.SemaphoreType.DMA((N_DEV - 1,))],
        compiler_params=pltpu.CompilerParams(vmem_limit_bytes=VMEM_LIMIT),
    )(*ins)
    loss = outs[0]
    g, d, mn, vn = (dict(zip(names, outs[1 + k * nw:1 + (k + 1) * nw])) for k in range(4))
    return loss, g, d, mn, vn


TM = 512
TM_LIGHT = 1024
TM_FFN = 256
FFN_CHUNK = 256
CONV_ROWS = 128
CONV_BWD_ROWS = 64
TK = 2048
TK_LIGHT = 4096
FORWARD_AT = 0.6
FORWARD_AT_LATE = 0.85


def kernel(x, norm_mix, attn_w_qkv, attn_b_qkv, attn_sinks, attn_w_o, attn_b_o, conv_w_pw1, conv_b_pw1, conv_w_dw, conv_b_dw, conv_ln_g, conv_ln_b, conv_w_pw2, conv_b_pw2, norm_ffn, ffn_w_up, ffn_w_dw, ffn_b_dw, ffn_w_down, final_norm, loss_target, m_norm_mix, m_attn_w_qkv, m_attn_b_qkv, m_attn_sinks, m_attn_w_o, m_attn_b_o, m_conv_w_pw1, m_conv_b_pw1, m_conv_w_dw, m_conv_b_dw, m_conv_ln_g, m_conv_ln_b, m_conv_w_pw2, m_conv_b_pw2, m_norm_ffn, m_ffn_w_up, m_ffn_w_dw, m_ffn_b_dw, m_ffn_w_down, m_final_norm, v_norm_mix, v_attn_w_qkv, v_attn_b_qkv, v_attn_sinks, v_attn_w_o, v_attn_b_o, v_conv_w_pw1, v_conv_b_pw1, v_conv_w_dw, v_conv_b_dw, v_conv_ln_g, v_conv_ln_b, v_conv_w_pw2, v_conv_b_pw2, v_norm_ffn, v_ffn_w_up, v_ffn_w_dw, v_ffn_b_dw, v_ffn_w_down, v_final_norm):
    w = dict(norm_mix=norm_mix, attn_w_qkv=attn_w_qkv, attn_b_qkv=attn_b_qkv, attn_sinks=attn_sinks, attn_w_o=attn_w_o,
             attn_b_o=attn_b_o, conv_w_pw1=conv_w_pw1, conv_b_pw1=conv_b_pw1, conv_w_dw=conv_w_dw, conv_b_dw=conv_b_dw,
             conv_ln_g=conv_ln_g, conv_ln_b=conv_ln_b, conv_w_pw2=conv_w_pw2, conv_b_pw2=conv_b_pw2, norm_ffn=norm_ffn,
             ffn_w_up=ffn_w_up, ffn_w_dw=ffn_w_dw, ffn_b_dw=ffn_b_dw, ffn_w_down=ffn_w_down, final_norm=final_norm)
    mom = dict(norm_mix=m_norm_mix, attn_w_qkv=m_attn_w_qkv, attn_b_qkv=m_attn_b_qkv, attn_sinks=m_attn_sinks,
               attn_w_o=m_attn_w_o, attn_b_o=m_attn_b_o, conv_w_pw1=m_conv_w_pw1, conv_b_pw1=m_conv_b_pw1,
               conv_w_dw=m_conv_w_dw, conv_b_dw=m_conv_b_dw, conv_ln_g=m_conv_ln_g, conv_ln_b=m_conv_ln_b,
               conv_w_pw2=m_conv_w_pw2, conv_b_pw2=m_conv_b_pw2, norm_ffn=m_norm_ffn, ffn_w_up=m_ffn_w_up,
               ffn_w_dw=m_ffn_w_dw, ffn_b_dw=m_ffn_b_dw, ffn_w_down=m_ffn_w_down, final_norm=m_final_norm)
    vel = dict(norm_mix=v_norm_mix, attn_w_qkv=v_attn_w_qkv, attn_b_qkv=v_attn_b_qkv, attn_sinks=v_attn_sinks,
               attn_w_o=v_attn_w_o, attn_b_o=v_attn_b_o, conv_w_pw1=v_conv_w_pw1, conv_b_pw1=v_conv_b_pw1,
               conv_w_dw=v_conv_w_dw, conv_b_dw=v_conv_b_dw, conv_ln_g=v_conv_ln_g, conv_ln_b=v_conv_ln_b,
               conv_w_pw2=v_conv_w_pw2, conv_b_pw2=v_conv_b_pw2, norm_ffn=v_norm_ffn, ffn_w_up=v_ffn_w_up,
               ffn_w_dw=v_ffn_w_dw, ffn_b_dw=v_ffn_b_dw, ffn_w_down=v_ffn_w_down, final_norm=v_final_norm)
    order = ("norm_mix", "attn_w_qkv", "attn_b_qkv", "attn_sinks", "attn_w_o", "attn_b_o", "conv_w_pw1", "conv_b_pw1",
             "conv_w_dw", "conv_b_dw", "conv_ln_g", "conv_ln_b", "conv_w_pw2", "conv_b_pw2", "norm_ffn", "ffn_w_up",
             "ffn_w_dw", "ffn_b_dw", "ffn_w_down", "final_norm")
    xi, yi, ci = _place()
    chip = 2 * xi + yi
    xs, target = x[0], loss_target[0]
    s = xs.shape[0]
    tm, tmf, tk, tkl, tml = min(TM, s), min(TM_FFN, s), min(TK, s), min(TK_LIGHT, s), min(TM_LIGHT, s)
    row = lambda v: v.reshape(1, -1)
    join = lambda a, axis: jnp.concatenate([a[j] for j in range(N_CHIPS)], axis=axis)
    cast = lambda a: a.astype(bf16)
    small, big = {}, {}

    got = run_rider(gather_rider(
        [(cast(attn_w_qkv[0]), _gv_qkv, (N_CHIPS, D, QKV // N_CHIPS)), (cast(attn_w_o[0]), _gv_rows, (D, D))],
        [w[k] for k in SMALL_SH], 0.0), "gather_attn")
    qkv4, w_o = got[:2]
    sm = dict(zip(SMALL_SH, got[2:]))
    w_qkv = jnp.transpose(qkv4, (1, 0, 2)).reshape(D, QKV)
    sinks = attn_sinks.reshape(N_HEADS)
    b_pw1, conv_dw, conv_bdw = join(sm["conv_b_pw1"], 1), join(sm["conv_w_dw"], 2)[0], join(sm["conv_b_dw"], 1)
    ln_g, ln_b, b_pw2, ffn_dw = (join(sm["conv_ln_g"], 1), join(sm["conv_ln_b"], 1), join(sm["conv_b_pw2"], 1),
                                 join(sm["ffn_w_dw"], 2))

    h0, qkv = qkv_fwd(xs, row(norm_mix[0]), w_qkv, attn_b_qkv, tml)
    (o, lse), (w_up0, w_dn0) = attn_fwd(qkv, sinks, rider=gather_rider(
        [(cast(ffn_w_up[0]), _gv_cols, (D, 2 * DFF)), (cast(ffn_w_down[0]), _gv_rows, (DFF, D))], [], FORWARD_AT))
    x1 = attn_out_fwd(xs, o, w_o, attn_b_o, tml)
    (x2, h1, up0, gate0, act0), (w_pw1, w_pw2, w_up1, w_dn1) = ffn_fwd(
        x1, row(norm_ffn[0]), w_up0, ffn_dw[0], row(ffn_b_dw[0]), w_dn0, tmf, FFN_CHUNK, rider=gather_rider(
            [(cast(conv_w_pw1[0]), _gv_cols, (D, 2 * D)), (cast(conv_w_pw2[0]), _gv_rows, (D, D)),
             (cast(ffn_w_up[1]), _gv_cols, (D, 2 * DFF)), (cast(ffn_w_down[1]), _gv_rows, (DFF, D))], [],
            FORWARD_AT_LATE))
    h2, a, u = pw1_fwd(x2, row(norm_mix[1]), w_pw1, b_pw1, tml)
    c, x3 = conv_fwd(u, x2, conv_dw, conv_bdw, ln_g, ln_b, w_pw2, b_pw2, tm, CONV_ROWS)
    dx4, h3, up1, gate1, act1, small["final_norm"], small["loss"] = ffn_fwd(
        x3, row(norm_ffn[1]), w_up1, ffn_dw[1], row(ffn_b_dw[1]), w_dn1, tmf, FFN_CHUNK,
        head=(final_norm.reshape(1, D), target))

    dx3, dup1, small["norm_ffn1"], small["ffn_w_dw1"], small["ffn_b_dw1"] = ffn_bwd(
        dx4, x3, row(norm_ffn[1]), up1, gate1, ffn_dw[1], w_dn1, w_up1, tmf, FFN_CHUNK)
    red1 = GradReduction(("up1", "down1"), [wgrad_cols(h3, dup1, tk, "wgrad_up1"),
                                           wgrad_rows(act1, dx4, 512, tk, "wgrad_down1")], ci, chip)
    (dc, u3, small["conv_ln_g"], small["conv_ln_b"], small["conv_b_pw2"], small["conv_b_dw"]), from_pair = conv_bwd_a(
        dx3, c, ln_g, ln_b, w_pw2.T, tml, rider=red1.pair_rider())
    g_pw2 = wgrad_rows(u3, dx3, 512, tkl, "wgrad_pw2")
    (dx2, da, small["conv_w_dw"], small["conv_b_pw1"], small["norm_mix1"]), landed = conv_bwd_b(
        dc, u, a, conv_dw, x2, row(norm_mix[1]), w_pw1.T, dx3, tm, CONV_BWD_ROWS,
        rider=red1.chip_rider(from_pair, FORWARD_AT))
    big.update(red1.result(landed, {"up1": (1, 2, None), "down1": (1, 2, None)}))
    g_pw1 = wgrad_cols(h2, da, tkl, "wgrad_pw1")

    red2 = GradReduction(("pw1", "pw2"), [g_pw1, g_pw2], ci, chip)
    (dx1, dup0, small["norm_ffn0"], small["ffn_w_dw0"], small["ffn_b_dw0"]), from_pair = ffn_bwd(
        dx2, x1, row(norm_ffn[0]), up0, gate0, ffn_dw[0], w_dn0, w_up0, tmf, FFN_CHUNK, rider=red2.pair_rider())
    g_up0, landed = wgrad_cols(h1, dup0, tk, "wgrad_up0", rider=red2.chip_rider(from_pair, FORWARD_AT))
    big.update(red2.result(landed))
    g_dn0 = wgrad_rows(act0, dx2, 512, tk, "wgrad_down0")

    red3 = GradReduction(("up0", "down0", "wo"), [g_up0, g_dn0, wgrad_rows(o, dx1, 512, tkl, "wgrad_o")], ci, chip)
    (do, small["attn_b_o"]), from_pair = attn_out_bwd(dx1, w_o.T, tml, rider=red3.pair_rider())
    (dq, dkv, small["attn_sinks"]), landed = attn_bwd(qkv, o, do, lse, sinks,
                                                      rider=red3.chip_rider(from_pair, FORWARD_AT))
    both = red3.result(landed, {"up0": (0, 2, big["up1"]), "down0": (0, 2, big["down1"])})
    big.update({"up": both.pop("up0"), "down": both.pop("down0"), **both})
    dx0, small["norm_mix0"], small["attn_b_qkv"], dkvb = qkv_bwd(dq, dkv, xs, row(norm_mix[0]), w_qkv.T, dx1, tml)
    g_qkv = jnp.concatenate([wgrad(h0, dq, 512, tkl, "wgrad_q"), wgrad(h0, dkvb, 2 * N_KV * HD, tkl, "wgrad_kv")], axis=1)
    g_qkv = jnp.transpose(g_qkv.reshape(2, D // 2, N_CHIPS, QKV // N_CHIPS), (0, 2, 1, 3))
    red4 = GradReduction(("qkv",), [g_qkv], ci, chip)
    big.update(red4.alone())

    gbig = {
        "attn_w_qkv": big["qkv"].reshape(1, D, QKV // N_CHIPS), "attn_w_o": big["wo"].reshape(1, D // N_CHIPS, D),
        "conv_w_pw1": big["pw1"].reshape(1, D, 2 * D // N_CHIPS), "conv_w_pw2": big["pw2"].reshape(1, D // N_CHIPS, D),
        "ffn_w_up": big["up"].reshape(2, D, 2 * DFF // N_CHIPS),
        "ffn_w_down": big["down"].reshape(2, DFF // N_CHIPS, D),
    }

    fix = lambda d: {**d, "final_norm": d["final_norm"].reshape(1, D)}
    loss, gs, ds, ms, vs = small_reduce_adamw(small, fix(w), fix(mom), fix(vel))
    unfix = lambda d: {**d, "final_norm": d["final_norm"].reshape(D)}
    gout, delta, new_m, new_v = unfix(gs), unfix(ds), unfix(ms), unfix(vs)

    for name, g in gbig.items():
        gout[name] = g
        delta[name], new_m[name], new_v[name] = adamw(w[name], g, mom[name], vel[name], "adamw_" + name)

    return (loss.reshape(()), dx0[None], *[gout[n] for n in order], *[delta[n] for n in order],
            *[new_m[n] for n in order], *[new_v[n] for n in order])
```

```python
import math

import jax
import jax.numpy as jnp
from jax import lax
from jax.experimental import pallas as pl
from jax.experimental.pallas import tpu as pltpu

f32 = jnp.float32
bf16 = jnp.bfloat16
SDS = jax.ShapeDtypeStruct
MESH = pl.DeviceIdType.MESH

D = 1024
N_HEADS = 16
N_KV = 2
GROUP = 8
HD = 64
BLK = 128
QKV = (N_HEADS + 2 * N_KV) * HD
KV_COL_BLOCK = (N_HEADS * HD) // (2 * N_KV * HD)
CONV_W = 31
CONV_HALO = 32
DFF = 2816
RMS_EPS = 1e-6
LN_EPS = 1e-5
LR, B1, B2, ADAM_EPS, WD, STEP = 0.001, 0.9, 0.999, 1e-08, 0.01, 10

N_CHIPS = 4
N_DEV = 8
VMEM_LIMIT = 56 * 1024 * 1024
LANES = 128
SUB = 8
ELEMENTWISE_BLOCK_BYTES = 1 << 20


def _cp(*sem):
    return pltpu.CompilerParams(dimension_semantics=sem, vmem_limit_bytes=VMEM_LIMIT)


def _row(tm, n):
    return pl.BlockSpec((tm, n), lambda i: (i, 0))


def _const(shape):
    return pl.BlockSpec(shape, lambda *_: (0,) * len(shape), pipeline_mode=pl.Buffered(1))


def _acc(shape):
    return pl.BlockSpec(shape, lambda *_: (0,) * len(shape))


def _rms_fwd(x, g):
    r = lax.rsqrt(jnp.mean(x * x, axis=-1, keepdims=True) + RMS_EPS)
    xn = x * r
    return xn * g, xn, r


def _colsum8(v):
    return jnp.sum(v.reshape(v.shape[0] // SUB, SUB, v.shape[1]), axis=0)


def _rms_bwd(xn, r, g, dh):
    dyn = dh * g
    dx = r * (dyn - xn * jnp.mean(dyn * xn, axis=-1, keepdims=True))
    return dx, _colsum8(dh * xn)


def _sigmoid(z):
    return 0.5 * jnp.tanh(0.5 * z) + 0.5


def _dsilu(z, sg):
    return sg * (1.0 + z * (1.0 - sg))


def _dot(a, b):
    return jnp.dot(a, b, preferred_element_type=f32)


_ANY = pl.BlockSpec(memory_space=pl.ANY)
_VMEM = pl.BlockSpec(memory_space=pltpu.VMEM)


class Rider:
    def __init__(self, ins, out_shape, sem_shapes, stages, final):
        self.ins, self.out_shape, self.sem_shapes, self.stages, self.final = ins, out_shape, sem_shapes, stages, final


def run_rider(rider, name):
    n_in, n_out = len(rider.ins), len(rider.out_shape)

    def body(*refs):
        parts = refs[:n_in], refs[n_in:n_in + n_out], refs[n_in + n_out:]
        for _, fn in rider.stages:
            fn(*parts)
        rider.final(*parts)

    return pl.pallas_call(
        body, name=name, in_specs=[_ANY] * n_in, out_specs=[_ANY] * n_out, out_shape=list(rider.out_shape),
        scratch_shapes=list(rider.sem_shapes),
    )(*rider.ins)


def _hosted(rider, body, *, grid, in_specs, out_specs, out_shape, name, compiler_params, scratch_shapes=()):
    if rider is None:
        return pl.pallas_call(body, grid=grid, in_specs=in_specs, out_specs=out_specs, out_shape=out_shape, name=name,
                              compiler_params=compiler_params, scratch_shapes=list(scratch_shapes))
    single = not isinstance(out_shape, (list, tuple))
    shapes = [out_shape] if single else list(out_shape)
    specs = [out_specs] if single else list(out_specs)
    n_in, n_out, n_sc = len(in_specs), len(shapes), len(scratch_shapes)
    r_in, r_out = len(rider.ins), len(rider.out_shape)
    total = math.prod(grid)

    def wrapped(*refs):
        own_in, refs = refs[:n_in], refs[n_in:]
        r_ins, refs = refs[:r_in], refs[r_in:]
        own_out, refs = refs[:n_out], refs[n_out:]
        r_outs, refs = refs[:r_out], refs[r_out:]
        own_sc, r_sems = refs[:n_sc], refs[n_sc:]
        step = 0
        for d, n in enumerate(grid):
            step = step * n + pl.program_id(d)
        for frac, fn in rider.stages:
            @pl.when(step == min(int(frac * total), total - 1))
            def _(fn=fn):
                fn(r_ins, r_outs, r_sems)

        body(*own_in, *own_out, *own_sc)

        @pl.when(step == total - 1)
        def _():
            rider.final(r_ins, r_outs, r_sems)

    call = pl.pallas_call(
        wrapped, grid=grid, in_specs=list(in_specs) + [_ANY] * r_in, out_specs=specs + [_ANY] * r_out,
        out_shape=shapes + list(rider.out_shape), scratch_shapes=list(scratch_shapes) + list(rider.sem_shapes),
        name=name, compiler_params=_cp(*(("arbitrary",) * len(grid))))

    def run(*args):
        res = call(*args, *rider.ins)
        own = res[:n_out]
        return (own[0] if single else own), res[n_out:]

    return run


def qkv_fwd(x, g, w, b, tm):
    s = x.shape[0]

    def body(x_ref, g_ref, w_ref, b_ref, h_ref, o_ref):
        h, _, _ = _rms_fwd(x_ref[...], g_ref[...])
        hb = h.astype(bf16)
        h_ref[...] = hb
        o_ref[...] = (_dot(hb, w_ref[...]) + b_ref[...]).astype(bf16)

    return pl.pallas_call(
        body, grid=(s // tm,), name="qkv_fwd",
        in_specs=[_row(tm, D), _const((1, D)), _const((D, QKV)), _const((1, QKV))],
        out_specs=[_row(tm, D), _row(tm, QKV)],
        out_shape=[SDS((s, D), bf16), SDS((s, QKV), bf16)],
        compiler_params=_cp("parallel"),
    )(x, g, w, b)


def _band_mask(i):
    qi = lax.broadcasted_iota(jnp.int32, (GROUP * BLK, 2 * BLK), 0) & (BLK - 1)
    ki = lax.broadcasted_iota(jnp.int32, (GROUP * BLK, 2 * BLK), 1)
    dist = qi + BLK - ki
    return (dist >= 0) & (dist < BLK) & ((ki >= BLK) | (i > 0))


_NEG = float(jnp.finfo(jnp.float32).min)
_NT = (((1,), (1,)), ((), ()))
_TN = (((0,), (0,)), ((), ()))


def _kv_heads(kvp_ref, kvc_ref, kvh):
    ks = slice(kvh * HD, (kvh + 1) * HD)
    vs = slice(N_KV * HD + kvh * HD, N_KV * HD + (kvh + 1) * HD)
    k = jnp.concatenate([kvp_ref[:, ks], kvc_ref[:, ks]], axis=0)
    v = jnp.concatenate([kvp_ref[:, vs], kvc_ref[:, vs]], axis=0)
    return k, v


def _stack_group(ref, kvh, width=HD):
    return jnp.concatenate([ref[:, (kvh * GROUP + gi) * width:(kvh * GROUP + gi + 1) * width] for gi in range(GROUP)],
                           axis=0)


def _group_sinks(sink_ref, kvh):
    row = lax.broadcasted_iota(jnp.int32, (GROUP * BLK, 1), 0)
    col = jnp.zeros((GROUP * BLK, 1), f32)
    for g in range(GROUP):
        col = jnp.where((row >= g * BLK) & (row < (g + 1) * BLK), sink_ref[kvh * GROUP + g], col)
    return col


def attn_fwd(qkv, sinks, rider=None):
    s = qkv.shape[0]
    scale = 1.0 / math.sqrt(HD)

    def body(q_ref, kvc_ref, kvp_ref, sink_ref, o_ref, lse_ref):
        valid = _band_mask(pl.program_id(0))
        for kvh in range(N_KV):
            k, v = _kv_heads(kvp_ref, kvc_ref, kvh)
            sc = lax.dot_general(_stack_group(q_ref, kvh), k, _NT, preferred_element_type=f32) * scale
            sc = jnp.where(valid, sc, _NEG)
            sink = _group_sinks(sink_ref, kvh)
            m = jnp.maximum(jnp.max(sc, axis=-1, keepdims=True), sink)
            p = jnp.exp(sc - m)
            denom = jnp.sum(p, axis=-1, keepdims=True) + jnp.exp(sink - m)
            og = _dot((p / denom).astype(bf16), v).astype(bf16)
            lse = m + jnp.log(denom)
            for gi in range(GROUP):
                h = kvh * GROUP + gi
                o_ref[:, h * HD:(h + 1) * HD] = og[gi * BLK:(gi + 1) * BLK]
                lse_ref[:, h:h + 1] = lse[gi * BLK:(gi + 1) * BLK]

    return _hosted(
        rider, body, grid=(s // BLK,), name="attn_fwd",
        in_specs=[
            pl.BlockSpec((BLK, N_HEADS * HD), lambda i: (i, 0)),
            pl.BlockSpec((BLK, 2 * N_KV * HD), lambda i: (i, KV_COL_BLOCK)),
            pl.BlockSpec((BLK, 2 * N_KV * HD), lambda i: (jnp.maximum(i - 1, 0), KV_COL_BLOCK)),
            pl.BlockSpec(memory_space=pltpu.SMEM),
        ],
        out_specs=[_row(BLK, D), _row(BLK, N_HEADS)],
        out_shape=[SDS((s, D), bf16), SDS((s, N_HEADS), f32)],
        compiler_params=_cp("parallel"),
    )(qkv, qkv, qkv, sinks)


def attn_out_fwd(x, o, w, b, tm):
    s = x.shape[0]

    def body(x_ref, o_ref, w_ref, b_ref, y_ref):
        y_ref[...] = x_ref[...] + _dot(o_ref[...], w_ref[...]) + b_ref[...]

    return pl.pallas_call(
        body, grid=(s // tm,), name="attn_out_fwd",
        in_specs=[_row(tm, D), _row(tm, D), _const((D, D)), _const((1, D))],
        out_specs=_row(tm, D), out_shape=SDS((s, D), f32),
        compiler_params=_cp("parallel"),
    )(x, o, w, b)


def attn_out_bwd(dy, wt, tm, rider=None):
    s = dy.shape[0]

    def body(dy_ref, wt_ref, do_ref, db_ref):
        @pl.when(pl.program_id(0) == 0)
        def _():
            db_ref[...] = jnp.zeros_like(db_ref)

        dy = dy_ref[...]
        do_ref[...] = _dot(dy.astype(bf16), wt_ref[...]).astype(bf16)
        db_ref[...] += _colsum8(dy)

    return _hosted(
        rider, body, grid=(s // tm,), name="attn_out_bwd",
        in_specs=[_row(tm, D), _const((D, D))],
        out_specs=[_row(tm, D), _acc((SUB, D))],
        out_shape=[SDS((s, D), bf16), SDS((SUB, D), f32)],
        compiler_params=_cp("arbitrary"),
    )(dy, wt)


def attn_bwd(qkv, o, do, lse, sinks, rider=None):
    s = qkv.shape[0]
    nb = s // BLK
    scale = 1.0 / math.sqrt(HD)
    kvw = 2 * N_KV * HD

    def body(q_ref, kvc_ref, kvp_ref, o_ref, do_ref, lse_ref, sink_ref, dq_ref, dkv_ref, ds_ref, carry_ref):
        i = pl.program_id(0)

        @pl.when(i == 0)
        def _():
            ds_ref[...] = jnp.zeros_like(ds_ref)
            carry_ref[...] = jnp.zeros_like(carry_ref)

        @pl.when(i < nb)
        def _():
            valid = _band_mask(i)
            for kvh in range(N_KV):
                k, v = _kv_heads(kvp_ref, kvc_ref, kvh)
                qg = _stack_group(q_ref, kvh)
                dog = _stack_group(do_ref, kvh)
                lse = _stack_group(lse_ref, kvh, 1)
                sc = lax.dot_general(qg, k, _NT, preferred_element_type=f32) * scale
                sc = jnp.where(valid, sc, _NEG)
                p = jnp.exp(sc - lse)
                dp = lax.dot_general(dog, v, _NT, preferred_element_type=f32)
                dlt = jnp.sum(dog.astype(f32) * _stack_group(o_ref, kvh).astype(f32), axis=-1, keepdims=True)
                dsc = (p * (dp - dlt)).astype(bf16)
                dqg = (_dot(dsc, k) * scale).astype(bf16)
                dk = lax.dot_general(dsc, qg, _TN, preferred_element_type=f32) * scale
                dv = lax.dot_general(p.astype(bf16), dog, _TN, preferred_element_type=f32)
                dsink = jnp.exp(_group_sinks(sink_ref, kvh) - lse) * dlt
                for gi in range(GROUP):
                    h = kvh * GROUP + gi
                    dq_ref[:, h * HD:(h + 1) * HD] = dqg[gi * BLK:(gi + 1) * BLK]
                    ds_ref[:, h:h + 1] += -jnp.sum(dsink[gi * BLK:(gi + 1) * BLK], axis=0, keepdims=True)
                ks = slice(kvh * HD, (kvh + 1) * HD)
                vs = slice(N_KV * HD + kvh * HD, N_KV * HD + (kvh + 1) * HD)
                dkv_ref[:, ks] = carry_ref[:, ks] + dk[:BLK]
                dkv_ref[:, vs] = carry_ref[:, vs] + dv[:BLK]
                carry_ref[:, ks] = dk[BLK:]
                carry_ref[:, vs] = dv[BLK:]

        @pl.when(i == nb)
        def _():
            dkv_ref[...] = carry_ref[...]

    cur = lambda i: (jnp.minimum(i, nb - 1), 0)
    prev = lambda i: (jnp.clip(i - 1, 0, nb - 1), KV_COL_BLOCK)
    return _hosted(
        rider, body, grid=(nb + 1,), name="attn_bwd",
        in_specs=[
            pl.BlockSpec((BLK, D), cur),
            pl.BlockSpec((BLK, kvw), lambda i: (jnp.minimum(i, nb - 1), KV_COL_BLOCK)),
            pl.BlockSpec((BLK, kvw), prev),
            pl.BlockSpec((BLK, D), cur),
            pl.BlockSpec((BLK, D), cur),
            pl.BlockSpec((BLK, N_HEADS), cur),
            pl.BlockSpec(memory_space=pltpu.SMEM),
        ],
        out_specs=[
            pl.BlockSpec((BLK, D), cur),
            pl.BlockSpec((BLK, kvw), lambda i: (jnp.maximum(i - 1, 0), 0)),
            _acc((1, N_HEADS)),
        ],
        out_shape=[SDS((s, D), bf16), SDS((s, kvw), f32), SDS((1, N_HEADS), f32)],
        scratch_shapes=[pltpu.VMEM((BLK, kvw), f32)],
        compiler_params=_cp("arbitrary"),
    )(qkv, qkv, qkv, o, do, lse, sinks)


def qkv_bwd(dq, dkv, x, g, wt, dres, tm):
    s = x.shape[0]
    qd = N_HEADS * HD
    kvw = 2 * N_KV * HD

    def body(dq_ref, dkv_ref, x_ref, g_ref, wt_ref, dres_ref, dx_ref, dg_ref, db_ref, dkvb_ref):
        @pl.when(pl.program_id(0) == 0)
        def _():
            dg_ref[...] = jnp.zeros_like(dg_ref)
            db_ref[...] = jnp.zeros_like(db_ref)

        dq = dq_ref[...]
        dkv = dkv_ref[...]
        dkvb = dkv.astype(bf16)
        dkvb_ref[...] = dkvb
        dh = _dot(dq, wt_ref[0:qd, :]) + _dot(dkvb, wt_ref[qd:QKV, :])
        g = g_ref[...]
        _, xn, r = _rms_fwd(x_ref[...], g)
        dx, dg = _rms_bwd(xn, r, g, dh)
        dx_ref[...] = dres_ref[...] + dx
        dg_ref[...] += dg
        db_ref[:, 0:qd] += _colsum8(dq.astype(f32))
        db_ref[:, qd:QKV] += _colsum8(dkv)

    return pl.pallas_call(
        body, grid=(s // tm,), name="qkv_bwd",
        in_specs=[_row(tm, qd), _row(tm, kvw), _row(tm, D), _const((1, D)), _const((QKV, D)), _row(tm, D)],
        out_specs=[_row(tm, D), _acc((SUB, D)), _acc((SUB, QKV)), _row(tm, kvw)],
        out_shape=[SDS((s, D), f32), SDS((SUB, D), f32), SDS((SUB, QKV), f32), SDS((s, kvw), bf16)],
        compiler_params=_cp("arbitrary"),
    )(dq, dkv, x, g, wt, dres)


def ffn_fwd(x, g, wup, wdw, bdw, wdn, tm, cw, rider=None, head=None):
    s = x.shape[0]
    tail = 8

    def body(x_ref, g_ref, wup_ref, wdw_ref, bdw_ref, wdn_ref, *refs):
        if head is None:
            xo_ref, h_ref, up_ref, gate_ref, act_ref, carry_ref, ext_ref = refs
        else:
            gf_ref, t_ref, xo_ref, h_ref, up_ref, gate_ref, act_ref, dgf_ref, loss_ref, carry_ref, ext_ref = refs

        @pl.when(pl.program_id(0) == 0)
        def _():
            carry_ref[...] = jnp.zeros_like(carry_ref)
            if head is not None:
                dgf_ref[...] = jnp.zeros_like(dgf_ref)
                loss_ref[...] = jnp.zeros_like(loss_ref)

        x = x_ref[...]
        h, _, _ = _rms_fwd(x, g_ref[...])
        hb = h.astype(bf16)
        h_ref[...] = hb
        for c in range(DFF // cw):
            cs = slice(c * cw, (c + 1) * cw)
            vs = slice(DFF + c * cw, DFF + (c + 1) * cw)
            ug = _dot(hb, wup_ref[:, cs])
            uv = _dot(hb, wup_ref[:, vs])
            up_ref[:, cs] = ug.astype(bf16)
            up_ref[:, vs] = uv.astype(bf16)
            ext_ref[0:tail, :] = carry_ref[:, cs]
            ext_ref[tail:tail + tm, :] = ug
            carry_ref[:, cs] = ug[tm - tail:, :]
            ext = ext_ref[...]
            gate = (wdw_ref[0:1, cs] * pltpu.roll(ext, 2, 0)[tail:] + wdw_ref[1:2, cs] * pltpu.roll(ext, 1, 0)[tail:]
                    + wdw_ref[2:3, cs] * ug) + bdw_ref[:, cs]
            gate_ref[:, cs] = gate.astype(bf16)
            act_ref[:, cs] = (gate * _sigmoid(gate) * uv).astype(bf16)
        xo = x + _dot(act_ref[...], wdn_ref[...])
        if head is None:
            xo_ref[...] = xo
        else:
            gf = gf_ref[...]
            y, xn, r = _rms_fwd(xo, gf)
            e = y - t_ref[...]
            loss_ref[...] += 0.5 * jnp.sum(jnp.mean(e * e, axis=-1, keepdims=True), axis=0, keepdims=True)
            dx, dgf = _rms_bwd(xn, r, gf, e / D)
            xo_ref[...] = dx
            dgf_ref[...] += dgf

    extra_in = [] if head is None else [_const((1, D)), _row(tm, D)]
    extra_out = [] if head is None else [_acc((SUB, D)), _acc((1, 1))]
    extra_shape = [] if head is None else [SDS((SUB, D), f32), SDS((1, 1), f32)]
    return _hosted(
        rider, body, grid=(s // tm,), name="ffn_fwd" if head is None else "ffn_fwd_loss",
        in_specs=[_row(tm, D), _const((1, D)), _const((D, 2 * DFF)), _const((3, DFF)), _const((1, DFF)),
                  _const((DFF, D))] + extra_in,
        out_specs=[_row(tm, D), _row(tm, D), _row(tm, 2 * DFF), _row(tm, DFF), _row(tm, DFF)] + extra_out,
        out_shape=[SDS((s, D), f32), SDS((s, D), bf16), SDS((s, 2 * DFF), bf16), SDS((s, DFF), bf16),
                   SDS((s, DFF), bf16)] + extra_shape,
        scratch_shapes=[pltpu.VMEM((tail, DFF), f32), pltpu.VMEM((tail + tm, cw), f32)],
        compiler_params=_cp("arbitrary"),
    )(x, g, wup, wdw, bdw, wdn, *(head or ()))


def ffn_bwd(dxo, x, g, up, gate, wdw, wdn, wup, tm, cw, rider=None):
    s = x.shape[0]
    nt = s // tm
    rev = lambda i: (nt - 1 - i, 0)

    def body(dxo_ref, x_ref, g_ref, up_ref, gate_ref, wdw_ref, wdn_ref, wup_ref,
             dxi_ref, dup_ref, dg_ref, dwdw_ref, dbdw_ref, carry_ref, ext2_ref):
        i = pl.program_id(0)

        @pl.when(i == 0)
        def _():
            carry_ref[...] = jnp.zeros_like(carry_ref)
            dg_ref[...] = jnp.zeros_like(dg_ref)
            dwdw_ref[...] = jnp.zeros_like(dwdw_ref)
            dbdw_ref[...] = jnp.zeros_like(dbdw_ref)

        dxo = dxo_ref[...]
        dxb = dxo.astype(bf16)
        for c in range(DFF // cw):
            cs = slice(c * cw, (c + 1) * cw)
            vs = slice(DFF + c * cw, DFF + (c + 1) * cw)
            d_act = lax.dot_general(dxb, wdn_ref[cs, :], _NT, preferred_element_type=f32)
            ug = up_ref[:, cs].astype(f32)
            uv = up_ref[:, vs].astype(f32)
            gate = gate_ref[:, cs].astype(f32)
            sg = _sigmoid(gate)
            dup_ref[:, vs] = (d_act * (gate * sg)).astype(bf16)
            d_gate = d_act * uv * _dsilu(gate, sg)
            ext2_ref[0:tm, :] = d_gate
            ext2_ref[tm:tm + 8, :] = carry_ref[:, cs]
            carry_ref[:, cs] = d_gate[0:8, :]
            ext = ext2_ref[...]
            ahead1 = pltpu.roll(ext, tm + 8 - 1, 0)[:tm]
            ahead2 = pltpu.roll(ext, tm + 8 - 2, 0)[:tm]
            dbdw_ref[:, cs] += _colsum8(d_gate)
            dwdw_ref[0, :, cs] += _colsum8(ahead2 * ug)
            dwdw_ref[1, :, cs] += _colsum8(ahead1 * ug)
            dwdw_ref[2, :, cs] += _colsum8(d_gate * ug)
            d_ug = wdw_ref[0:1, cs] * ahead2 + wdw_ref[1:2, cs] * ahead1 + wdw_ref[2:3, cs] * d_gate
            dup_ref[:, cs] = d_ug.astype(bf16)
        dh = lax.dot_general(dup_ref[...], wup_ref[...], _NT, preferred_element_type=f32)
        gv = g_ref[...]
        _, xn, r = _rms_fwd(x_ref[...], gv)
        dx, dg = _rms_bwd(xn, r, gv, dh)
        dxi_ref[...] = dxo + dx
        dg_ref[...] += dg

    return _hosted(
        rider, body, grid=(nt,), name="ffn_bwd",
        in_specs=[
            pl.BlockSpec((tm, D), rev), pl.BlockSpec((tm, D), rev), _const((1, D)),
            pl.BlockSpec((tm, 2 * DFF), rev), pl.BlockSpec((tm, DFF), rev),
            _const((3, DFF)), _const((DFF, D)), _const((D, 2 * DFF)),
        ],
        out_specs=[pl.BlockSpec((tm, D), rev), pl.BlockSpec((tm, 2 * DFF), rev), _acc((SUB, D)),
                   _acc((3, SUB, DFF)), _acc((SUB, DFF))],
        out_shape=[SDS((s, D), f32), SDS((s, 2 * DFF), bf16), SDS((SUB, D), f32), SDS((3, SUB, DFF), f32),
                   SDS((SUB, DFF), f32)],
        scratch_shapes=[pltpu.VMEM((8, DFF), f32), pltpu.VMEM((tm + 8, cw), f32)],
        compiler_params=_cp("arbitrary"),
    )(dxo, x, g, up, gate, wdw, wdn, wup)


def pw1_fwd(x, g, w, b, tm):
    s = x.shape[0]

    def body(x_ref, g_ref, w_ref, b_ref, h_ref, a_ref, u_ref):
        h, _, _ = _rms_fwd(x_ref[...], g_ref[...])
        hb = h.astype(bf16)
        h_ref[...] = hb
        a = _dot(hb, w_ref[...]) + b_ref[...]
        a_ref[...] = a.astype(bf16)
        u_ref[...] = a[:, :D] * _sigmoid(a[:, D:])

    return pl.pallas_call(
        body, grid=(s // tm,), name="pw1_fwd",
        in_specs=[_row(tm, D), _const((1, D)), _const((D, 2 * D)), _const((1, 2 * D))],
        out_specs=[_row(tm, D), _row(tm, 2 * D), _row(tm, D)],
        out_shape=[SDS((s, D), bf16), SDS((s, 2 * D), bf16), SDS((s, D), f32)],
        compiler_params=_cp("parallel"),
    )(x, g, w, b)


def _ln_silu(c, lg, lb):
    mu = jnp.mean(c, axis=-1, keepdims=True)
    cc = c - mu
    var = jnp.mean(cc * cc, axis=-1, keepdims=True)
    rstd = lax.rsqrt(var + LN_EPS)
    xh = cc * rstd
    ln = xh * lg + lb
    sg = _sigmoid(ln)
    return xh, rstd, ln, sg


def _shifted_copies(ext_ref, sh_ref, cs, tm):
    n = CONV_HALO - SUB + tm
    for k in range(1, SUB):
        sh_ref[k - 1] = ext_ref[pl.ds(k, n), cs]


def _shifted_rows(ext_ref, sh_ref, cs, start, rows):
    q, k = divmod(start, SUB)
    if k == 0:
        return ext_ref[pl.ds(start, rows), cs]
    return sh_ref[k - 1, pl.ds(q * SUB, rows), :]


def conv_fwd(u, x, wdw, bdw, lg, lb, w2, b2, tm, rc):
    s = x.shape[0]
    hl = CONV_HALO
    off = hl - (CONV_W - 1)

    def body(u_ref, halo_ref, x_ref, wdw_ref, bdw_ref, lg_ref, lb_ref, w2_ref, b2_ref, c_ref, xo_ref, ext_ref, sh_ref):
        has_prev = (pl.program_id(0) > 0).astype(f32)
        ext_ref[0:hl, :] = halo_ref[...] * has_prev
        ext_ref[hl:hl + tm, :] = u_ref[...]
        for cc in range(D // LANES):
            cs = slice(cc * LANES, (cc + 1) * LANES)
            _shifted_copies(ext_ref, sh_ref, cs, tm)
            for rr in range(tm // rc):
                acc = jnp.zeros((rc, LANES), f32) + bdw_ref[:, cs]
                for j in range(CONV_W):
                    acc = acc + wdw_ref[j:j + 1, cs] * _shifted_rows(ext_ref, sh_ref, cs, rr * rc + off + j, rc)
                c_ref[rr * rc:(rr + 1) * rc, cs] = acc
        _, _, ln, sg = _ln_silu(c_ref[...], lg_ref[...], lb_ref[...])
        xo_ref[...] = x_ref[...] + _dot((ln * sg).astype(bf16), w2_ref[...]) + b2_ref[...]

    return pl.pallas_call(
        body, grid=(s // tm,), name="conv_fwd",
        in_specs=[_row(tm, D), pl.BlockSpec((hl, D), lambda i: (jnp.maximum(i * (tm // hl) - 1, 0), 0)), _row(tm, D),
                  _const((CONV_W, D)), _const((1, D)), _const((1, D)), _const((1, D)), _const((D, D)), _const((1, D))],
        out_specs=[_row(tm, D), _row(tm, D)],
        out_shape=[SDS((s, D), f32), SDS((s, D), f32)],
        scratch_shapes=[pltpu.VMEM((hl + tm, D), f32), pltpu.VMEM((SUB - 1, hl - SUB + tm, LANES), f32)],
        compiler_params=_cp("parallel"),
    )(u, u, x, wdw, bdw, lg, lb, w2, b2)


def conv_bwd_a(dy, c, lg, lb, w2t, tm, rider=None):
    s = dy.shape[0]

    def body(dy_ref, c_ref, lg_ref, lb_ref, w2t_ref, dc_ref, u3_ref, dlg_ref, dlb_ref, db2_ref, dbdw_ref):
        @pl.when(pl.program_id(0) == 0)
        def _():
            for r in (dlg_ref, dlb_ref, db2_ref, dbdw_ref):
                r[...] = jnp.zeros_like(r)

        dy = dy_ref[...]
        lg = lg_ref[...]
        xh, rstd, ln, sg = _ln_silu(c_ref[...], lg, lb_ref[...])
        u3_ref[...] = (ln * sg).astype(bf16)
        du3 = _dot(dy.astype(bf16), w2t_ref[...])
        dln = du3 * _dsilu(ln, sg)
        dxh = dln * lg
        dc = rstd * (dxh - jnp.mean(dxh, axis=-1, keepdims=True) - xh * jnp.mean(dxh * xh, axis=-1, keepdims=True))
        dc_ref[...] = dc
        dlg_ref[...] += _colsum8(dln * xh)
        dlb_ref[...] += _colsum8(dln)
        db2_ref[...] += _colsum8(dy)
        dbdw_ref[...] += _colsum8(dc)

    return _hosted(
        rider, body, grid=(s // tm,), name="conv_bwd_a",
        in_specs=[_row(tm, D), _row(tm, D), _const((1, D)), _const((1, D)), _const((D, D))],
        out_specs=[_row(tm, D), _row(tm, D)] + [_acc((SUB, D))] * 4,
        out_shape=[SDS((s, D), f32), SDS((s, D), bf16)] + [SDS((SUB, D), f32)] * 4,
        compiler_params=_cp("arbitrary"),
    )(dy, c, lg, lb, w2t)


def conv_bwd_b(dc, u, a, wdw, x, g, w1t, dres, tm, rc, rider=None):
    s = x.shape[0]
    nt = s // tm
    hl = CONV_HALO
    off = hl - (CONV_W - 1)

    def body(dc_ref, dnext_ref, u_ref, uprev_ref, a_ref, wdw_ref, x_ref, g_ref, w1t_ref, dres_ref,
             dx_ref, da_ref, dwdw_ref, db1_ref, dg_ref, ext_ref, ext2_ref, du_ref, sh_ref, sh2_ref):
        i = pl.program_id(0)

        @pl.when(i == 0)
        def _():
            for r in (dwdw_ref, db1_ref, dg_ref):
                r[...] = jnp.zeros_like(r)

        ext_ref[0:hl, :] = uprev_ref[...] * (i > 0).astype(f32)
        ext_ref[hl:hl + tm, :] = u_ref[...]
        ext2_ref[0:tm, :] = dc_ref[...]
        ext2_ref[tm:tm + hl, :] = dnext_ref[...] * (i < nt - 1).astype(f32)
        for cc in range(D // LANES):
            cs = slice(cc * LANES, (cc + 1) * LANES)
            _shifted_copies(ext_ref, sh_ref, cs, tm)
            _shifted_copies(ext2_ref, sh2_ref, cs, tm)
            for rr in range(tm // rc):
                r0 = rr * rc
                dcb = ext2_ref[r0:r0 + rc, cs]
                acc = jnp.zeros((rc, LANES), f32)
                for j in range(CONV_W):
                    acc = acc + wdw_ref[j:j + 1, cs] * _shifted_rows(ext2_ref, sh2_ref, cs, r0 + CONV_W - 1 - j, rc)
                    dwdw_ref[j, :, cs] += _colsum8(dcb * _shifted_rows(ext_ref, sh_ref, cs, r0 + off + j, rc))
                du_ref[r0:r0 + rc, cs] = acc
        du = du_ref[...]
        a1 = a_ref[:, :D].astype(f32)
        sg = _sigmoid(a_ref[:, D:].astype(f32))
        da1 = du * sg
        da2 = du * a1 * sg * (1.0 - sg)
        da_ref[:, :D] = da1.astype(bf16)
        da_ref[:, D:] = da2.astype(bf16)
        db1_ref[:, :D] += _colsum8(da1)
        db1_ref[:, D:] += _colsum8(da2)
        dh = _dot(da_ref[...], w1t_ref[...])
        gv = g_ref[...]
        _, xn, r = _rms_fwd(x_ref[...], gv)
        dx, dg = _rms_bwd(xn, r, gv, dh)
        dx_ref[...] = dres_ref[...] + dx
        dg_ref[...] += dg

    blocks = tm // hl
    return _hosted(
        rider, body, grid=(nt,), name="conv_bwd_b",
        in_specs=[
            _row(tm, D), pl.BlockSpec((hl, D), lambda i: (jnp.minimum((i + 1) * blocks, s // hl - 1), 0)),
            _row(tm, D), pl.BlockSpec((hl, D), lambda i: (jnp.maximum(i * blocks - 1, 0), 0)),
            _row(tm, 2 * D), _const((CONV_W, D)), _row(tm, D), _const((1, D)), _const((2 * D, D)), _row(tm, D),
        ],
        out_specs=[_row(tm, D), _row(tm, 2 * D), _acc((CONV_W, SUB, D)), _acc((SUB, 2 * D)), _acc((SUB, D))],
        out_shape=[SDS((s, D), f32), SDS((s, 2 * D), bf16), SDS((CONV_W, SUB, D), f32), SDS((SUB, 2 * D), f32),
                   SDS((SUB, D), f32)],
        scratch_shapes=[pltpu.VMEM((hl + tm, D), f32), pltpu.VMEM((tm + hl, D), f32), pltpu.VMEM((tm, D), f32),
                        pltpu.VMEM((SUB - 1, hl - SUB + tm, LANES), f32),
                        pltpu.VMEM((SUB - 1, hl - SUB + tm, LANES), f32)],
        compiler_params=_cp("arbitrary"),
    )(dc, dc, u, u, a, wdw, x, g, w1t, dres)


def wgrad(a, b, nb, tk, name, rider=None):
    s, k1 = a.shape
    n = b.shape[1]

    def body(a_ref, b_ref, o_ref):
        @pl.when(pl.program_id(1) == 0)
        def _():
            o_ref[...] = jnp.zeros_like(o_ref)

        o_ref[...] += lax.dot_general(a_ref[...], b_ref[...].astype(bf16), _TN, preferred_element_type=f32)

    return _hosted(
        rider, body, grid=(n // nb, s // tk), name=name,
        in_specs=[pl.BlockSpec((tk, k1), lambda j, k: (k, 0)), pl.BlockSpec((tk, nb), lambda j, k: (k, j))],
        out_specs=pl.BlockSpec((k1, nb), lambda j, k: (0, j)),
        out_shape=SDS((k1, n), f32),
        compiler_params=_cp("parallel", "arbitrary"),
    )(a, b)


def wgrad_cols(a, b, tk, name, rider=None):
    s, k1 = a.shape
    w = b.shape[1] // N_CHIPS

    def body(a_ref, b_ref, o_ref):
        @pl.when(pl.program_id(1) == 0)
        def _():
            o_ref[...] = jnp.zeros_like(o_ref)

        acc = lax.dot_general(a_ref[...], b_ref[...].astype(bf16), _TN, preferred_element_type=f32)
        o_ref[:, 0] += acc.reshape(2, k1 // 2, w)

    return _hosted(
        rider, body, grid=(N_CHIPS, s // tk), name=name,
        in_specs=[pl.BlockSpec((tk, k1), lambda j, k: (k, 0)), pl.BlockSpec((tk, w), lambda j, k: (k, j))],
        out_specs=pl.BlockSpec((2, 1, k1 // 2, w), lambda j, k: (0, j, 0, 0)),
        out_shape=SDS((2, N_CHIPS, k1 // 2, w), f32),
        compiler_params=_cp("parallel", "arbitrary"),
    )(a, b)


def wgrad_rows(a, b, nb, tk, name, rider=None):
    s, k1 = a.shape
    n = b.shape[1]
    r = k1 // (2 * N_CHIPS)

    def body(a_ref, b_ref, o_ref):
        @pl.when(pl.program_id(1) == 0)
        def _():
            o_ref[...] = jnp.zeros_like(o_ref)

        acc = lax.dot_general(a_ref[...], b_ref[...].astype(bf16), _TN, preferred_element_type=f32)
        for j in range(N_CHIPS):
            for h in range(2):
                o_ref[h, j] += acc[(2 * j + h) * r:(2 * j + h + 1) * r, :]

    return _hosted(
        rider, body, grid=(n // nb, s // tk), name=name,
        in_specs=[pl.BlockSpec((tk, k1), lambda j, k: (k, 0)), pl.BlockSpec((tk, nb), lambda j, k: (k, j))],
        out_specs=pl.BlockSpec((2, N_CHIPS, r, nb), lambda j, k: (0, 0, 0, j)),
        out_shape=SDS((2, N_CHIPS, r, n), f32),
        compiler_params=_cp("parallel", "arbitrary"),
    )(a, b)


def _adam_math(w, g, m, v):
    m = B1 * m + (1.0 - B1) * g
    v = B2 * v + (1.0 - B2) * (g * g)
    m_hat = m / (1.0 - B1 ** STEP)
    v_hat = v / (1.0 - B2 ** STEP)
    delta = -LR * (m_hat / (jnp.sqrt(v_hat) + ADAM_EPS) + WD * w)
    return delta, m, v


def _rows_tile(r, c, multiple=SUB):
    best = None
    for t in range(multiple, r + 1, multiple):
        if r % t == 0 and t * c * 4 <= ELEMENTWISE_BLOCK_BYTES:
            best = t
    return best if best is not None else r


def adamw(w, g, m, v, name):
    l, r, c = w.shape
    tr = _rows_tile(r, c)
    spec = pl.BlockSpec((1, tr, c), lambda i, j: (i, j, 0))

    def body(w_ref, g_ref, m_ref, v_ref, d_ref, mo_ref, vo_ref):
        d, mn, vn = _adam_math(w_ref[...], g_ref[...], m_ref[...], v_ref[...])
        d_ref[...] = d
        mo_ref[...] = mn
        vo_ref[...] = vn

    return pl.pallas_call(
        body, grid=(l, r // tr), name=name, in_specs=[spec] * 4, out_specs=[spec] * 3,
        out_shape=[SDS((l, r, c), f32)] * 3, compiler_params=_cp("parallel", "parallel"),
    )(w, g, m, v)


def _place():
    return lax.axis_index("x"), lax.axis_index("y"), lax.axis_index("c")


def _chip_peer(xi, yi, r):
    px = 1 - xi if r & 2 else xi
    py = 1 - yi if r & 1 else yi
    return px, py


def _gv_qkv(src, dst, j, h):
    rows = pl.ds(h * (D // 2), D // 2)
    return src.at[rows, :], dst.at[j, rows, :]


def _gv_rows(src, dst, j, h):
    r = src.shape[0] // 2
    return src.at[pl.ds(h * r, r), :], dst.at[pl.ds(j * 2 * r + h * r, r), :]


def _gv_cols(src, dst, j, h):
    r, w = src.shape[0] // 2, src.shape[1]
    return src.at[pl.ds(h * r, r), :], dst.at[pl.ds(h * r, r), pl.ds(j * w, w)]


def gather_rider(big, small, forward_at):
    nb, ns = len(big), len(small)
    n = nb + ns
    views = [v for _, v, _ in big]

    def env(ins, outs, sems):
        ici_send, ici_recv, d2d_send, d2d_recv, loc_sems = sems
        xi, yi, ci = _place()
        me = 2 * xi + yi

        def local(a, h):
            if a < nb:
                src, dst = views[a](ins[a], outs[a], me, h)
                return pltpu.make_async_copy(src, dst, loc_sems.at[2 * a + h])
            return pltpu.make_async_copy(ins[a], outs[a].at[me], loc_sems.at[nb + a])

        def ici(a, r, slot):
            px, py = _chip_peer(xi, yi, r)
            src, dst = views[a](ins[a], outs[a], slot, ci) if a < nb else (ins[a], outs[a].at[slot])
            k = 3 * a + r - 1
            return pltpu.make_async_remote_copy(src_ref=src, dst_ref=dst, send_sem=ici_send.at[k],
                                                recv_sem=ici_recv.at[k], device_id=(px, py, ci), device_id_type=MESH)

        def d2d(a, r, half):
            px, py = _chip_peer(xi, yi, r)
            _, dst = views[a](ins[a], outs[a], 2 * px + py, half)
            k = 3 * a + r - 1
            return pltpu.make_async_remote_copy(src_ref=dst, dst_ref=dst, send_sem=d2d_send.at[k],
                                                recv_sem=d2d_recv.at[k], device_id=(xi, yi, 1 - ci), device_id_type=MESH)

        return xi, yi, ci, me, local, ici, d2d

    def locals_of():
        return [(a, h) for a in range(nb) for h in range(2)] + [(a, 0) for a in range(nb, n)]

    def send(ins, outs, sems):
        _, _, _, me, local, ici, _ = env(ins, outs, sems)
        for a, h in locals_of():
            local(a, h).start()
        for a in range(n):
            for r in (1, 2, 3):
                ici(a, r, me).start()

    def forward(ins, outs, sems):
        xi, yi, ci, _, _, ici, d2d = env(ins, outs, sems)
        for a in range(n):
            for r in (1, 2, 3):
                px, py = _chip_peer(xi, yi, r)
                ici(a, r, 2 * px + py).wait_recv()
                if a < nb:
                    d2d(a, r, ci).start()

    def finish(ins, outs, sems):
        _, _, ci, me, local, ici, d2d = env(ins, outs, sems)
        for a in range(nb):
            for r in (1, 2, 3):
                d2d(a, r, 1 - ci).wait_recv()
        for a in range(n):
            for r in (1, 2, 3):
                ici(a, r, me).wait_send()
                if a < nb:
                    d2d(a, r, ci).wait_send()
        for a, h in locals_of():
            local(a, h).wait()

    dma = pltpu.SemaphoreType.DMA
    return Rider(
        ins=[b for b, _, _ in big] + list(small),
        out_shape=[SDS(shape, bf16) for _, _, shape in big] + [SDS((N_CHIPS,) + a.shape, a.dtype) for a in small],
        sem_shapes=[dma((3 * n,)), dma((3 * n,)), dma((max(3 * nb, 1),)), dma((max(3 * nb, 1),)), dma((2 * nb + ns,))],
        stages=[(0.0, send), (forward_at, forward)], final=finish)


def pair_send_rider(gs):
    n = len(gs)

    def copy(ins, outs, sems, a):
        xi, yi, ci = _place()
        return pltpu.make_async_remote_copy(
            src_ref=ins[a].at[1 - ci], dst_ref=outs[a], send_sem=sems[0].at[a], recv_sem=sems[1].at[a],
            device_id=(xi, yi, 1 - ci), device_id_type=MESH)

    def send(ins, outs, sems):
        for a in range(n):
            copy(ins, outs, sems, a).start()

    def finish(ins, outs, sems):
        for a in range(n):
            copy(ins, outs, sems, a).wait()

    dma = pltpu.SemaphoreType.DMA
    return Rider(ins=list(gs), out_shape=[SDS(g.shape[1:], g.dtype) for g in gs], sem_shapes=[dma((n,)), dma((n,))],
                 stages=[(0.0, send)], final=finish)


def chip_rider(pbs, p32s, forward_at):
    n = len(pbs)

    def env(ins, outs, sems):
        ici_send, ici_recv, d2d_send, d2d_recv, own_send, own_recv, loc_sems = sems
        pb, p32, recv, own = ins[:n], ins[n:], outs[:n], outs[n:]
        xi, yi, ci = _place()
        me = 2 * xi + yi
        sib = (xi, yi, 1 - ci)

        def ici(a, r, src_slot, dst_slot):
            px, py = _chip_peer(xi, yi, r)
            k = 3 * a + r - 1
            return pltpu.make_async_remote_copy(
                src_ref=pb[a].at[src_slot], dst_ref=recv[a].at[ci, dst_slot], send_sem=ici_send.at[k],
                recv_sem=ici_recv.at[k], device_id=(px, py, ci), device_id_type=MESH)

        def d2d(a, r, half):
            px, py = _chip_peer(xi, yi, r)
            blk = recv[a].at[half, 2 * px + py]
            k = 3 * a + r - 1
            return pltpu.make_async_remote_copy(src_ref=blk, dst_ref=blk, send_sem=d2d_send.at[k],
                                                recv_sem=d2d_recv.at[k], device_id=sib, device_id_type=MESH)

        def mine(a, half):
            return pltpu.make_async_remote_copy(src_ref=p32[a].at[me], dst_ref=own[a].at[half], send_sem=own_send.at[a],
                                                recv_sem=own_recv.at[a], device_id=sib, device_id_type=MESH)

        def local(a):
            return pltpu.make_async_copy(p32[a].at[me], own[a].at[ci], loc_sems.at[a])

        return xi, yi, ci, me, ici, d2d, mine, local

    def send(ins, outs, sems):
        xi, yi, ci, me, ici, _, mine, local = env(ins, outs, sems)
        for a in range(n):
            local(a).start()
            mine(a, ci).start()
            for r in (1, 2, 3):
                px, py = _chip_peer(xi, yi, r)
                ici(a, r, 2 * px + py, me).start()

    def forward(ins, outs, sems):
        xi, yi, ci, me, ici, d2d, _, _ = env(ins, outs, sems)
        for a in range(n):
            for r in (1, 2, 3):
                px, py = _chip_peer(xi, yi, r)
                ici(a, r, me, 2 * px + py).wait_recv()
                d2d(a, r, ci).start()

    def finish(ins, outs, sems):
        xi, yi, ci, me, ici, d2d, mine, local = env(ins, outs, sems)
        for a in range(n):
            mine(a, 1 - ci).wait_recv()
            for r in (1, 2, 3):
                d2d(a, r, 1 - ci).wait_recv()
        for a in range(n):
            mine(a, ci).wait_send()
            local(a).wait()
            for r in (1, 2, 3):
                px, py = _chip_peer(xi, yi, r)
                ici(a, r, 2 * px + py, me).wait_send()
                d2d(a, r, ci).wait_send()

    dma = pltpu.SemaphoreType.DMA
    return Rider(
        ins=list(pbs) + list(p32s),
        out_shape=[SDS((2,) + p.shape, bf16) for p in pbs] + [SDS((2,) + p.shape[1:], f32) for p in p32s],
        sem_shapes=[dma((3 * n,)), dma((3 * n,)), dma((3 * n,)), dma((3 * n,)), dma((n,)), dma((n,)), dma((n,))],
        stages=[(0.0, send), (forward_at, forward)], final=finish)


def rs_pair_add(g, other, ci, name):
    _, nsh, r, w = g.shape
    tr = _rows_tile(r, w, 16)

    def body(c_ref, g_ref, o_ref, p_ref, pb_ref):
        p = g_ref[0] + o_ref[...]
        p_ref[...] = p
        pb_ref[...] = p.astype(bf16)

    blk = pl.BlockSpec((1, tr, w), lambda j, i, c: (j, i, 0))
    return pl.pallas_call(
        body, name=name,
        grid_spec=pltpu.PrefetchScalarGridSpec(
            num_scalar_prefetch=1, grid=(nsh, r // tr),
            in_specs=[pl.BlockSpec((1, 1, tr, w), lambda j, i, c: (c[0], j, i, 0)), blk], out_specs=[blk, blk]),
        out_shape=[SDS((nsh, r, w), f32), SDS((nsh, r, w), bf16)],
        compiler_params=_cp("parallel", "parallel"),
    )(ci.reshape(1).astype(jnp.int32), g, other)


def rs_chip_add(own, recv, chip, name, layer=0, layers=1, into=None):
    _, nsh, r, w = recv.shape
    tr = _rows_tile(r, w, 16)

    def body(c_ref, own_ref, recv_ref, *refs):
        o_ref = refs[-1]
        me = c_ref[0]
        acc = None
        for j in range(N_CHIPS):
            term = jnp.where(me == j, own_ref[0], recv_ref[0, j].astype(f32))
            acc = term if acc is None else acc + term
        o_ref[0, 0] = acc

    args = (chip.reshape(1).astype(jnp.int32), own, recv) + (() if into is None else (into,))
    return pl.pallas_call(
        body, name=name,
        grid_spec=pltpu.PrefetchScalarGridSpec(
            num_scalar_prefetch=1, grid=(2, r // tr),
            in_specs=[pl.BlockSpec((1, tr, w), lambda h, i, c: (h, i, 0)),
                      pl.BlockSpec((1, nsh, tr, w), lambda h, i, c: (h, 0, i, 0))] + ([] if into is None else [_ANY]),
            out_specs=pl.BlockSpec((1, 1, tr, w), lambda h, i, c: (layer, h, i, 0))),
        out_shape=SDS((layers, 2, r, w), f32),
        input_output_aliases={} if into is None else {3: 0},
        compiler_params=_cp("parallel", "parallel"),
    )(*args)


class GradReduction:
    def __init__(self, keys, grads, ci, chip):
        self.keys, self.grads, self.ci, self.chip = keys, grads, ci, chip

    def pair_rider(self):
        return pair_send_rider(self.grads)

    def chip_rider(self, from_pair, forward_at):
        both = [rs_pair_add(g, o, self.ci, "rs_pair_add_" + k) for k, g, o in zip(self.keys, self.grads, from_pair)]
        return chip_rider([pb for _, pb in both], [p for p, _ in both], forward_at)

    def result(self, landed, place=None):
        n = len(self.keys)
        return {k: rs_chip_add(own, recv, self.chip, "rs_chip_add_" + k, *(place or {}).get(k, ()))
                for k, recv, own in zip(self.keys, landed[:n], landed[n:])}

    def alone(self):
        from_pair = run_rider(self.pair_rider(), "rs_pair_" + self.keys[0])
        return self.result(run_rider(self.chip_rider(from_pair, 0.0), "rs_chip_" + self.keys[0]))


STAGE_W = DFF
_ST = {"norm_mix0": 0, "norm_mix1": 1, "attn_b_qkv": 2, "attn_sinks": 3, "attn_b_o": 4, "final_norm": 5, "loss": 6,
       "norm_ffn0": 8, "norm_ffn1": 9, "ffn_b_dw0": 10, "ffn_b_dw1": 11, "conv_b_pw1": 12, "conv_b_dw": 13,
       "conv_ln_g": 14, "conv_ln_b": 15, "conv_b_pw2": 16, "ffn_w_dw0": 17, "ffn_w_dw1": 20, "conv_w_dw": 24}
STAGE_ROWS = 56
SMALL_REP = ("norm_mix", "attn_b_qkv", "attn_sinks", "attn_b_o", "norm_ffn", "ffn_b_dw", "final_norm")
SMALL_SH = ("conv_b_pw1", "conv_w_dw", "conv_b_dw", "conv_ln_g", "conv_ln_b", "conv_b_pw2", "ffn_w_dw")
_SMALL_PARTS = ("norm_mix0", "norm_mix1", "attn_b_qkv", "attn_sinks", "attn_b_o", "norm_ffn0", "norm_ffn1", "ffn_b_dw0",
                "ffn_b_dw1", "final_norm", "conv_b_pw1", "conv_b_dw", "conv_ln_g", "conv_ln_b", "conv_b_pw2", "loss",
                "ffn_w_dw0", "ffn_w_dw1", "conv_w_dw")


def small_reduce_adamw(parts, w, m, v):
    names = SMALL_REP + SMALL_SH
    npart, nw = len(_SMALL_PARTS), len(names)

    def body(*refs):
        part = dict(zip(_SMALL_PARTS, refs[:npart]))
        off = npart
        w_ref = dict(zip(names, refs[off:off + nw]))
        m_ref = dict(zip(names, refs[off + nw:off + 2 * nw]))
        v_ref = dict(zip(names, refs[off + 2 * nw:off + 3 * nw]))
        off += 3 * nw
        loss_ref = refs[off]
        g_out = dict(zip(names, refs[off + 1:off + 1 + nw]))
        d_out = dict(zip(names, refs[off + 1 + nw:off + 1 + 2 * nw]))
        m_out = dict(zip(names, refs[off + 1 + 2 * nw:off + 1 + 3 * nw]))
        v_out = dict(zip(names, refs[off + 1 + 3 * nw:off + 1 + 4 * nw]))
        stage_ref, buf_ref, tot_ref, send_sems, recv_sems = refs[off + 1 + 4 * nw:]

        xi, yi, ci = _place()
        me = 4 * xi + 2 * yi + ci
        chip = 2 * xi + yi

        stage_ref[...] = jnp.zeros_like(stage_ref)
        for name in _SMALL_PARTS:
            ref, r0 = part[name], _ST[name]
            if name in ("attn_sinks", "loss"):
                val = ref[...]
            elif name in ("ffn_w_dw0", "ffn_w_dw1", "conv_w_dw"):
                val = jnp.sum(ref[...], axis=1)
            else:
                val = jnp.sum(ref[...], axis=0, keepdims=True)
            stage_ref[r0:r0 + val.shape[0], 0:val.shape[1]] = val

        buf_ref[me] = stage_ref[...]

        def peer(r):
            px, py = _chip_peer(xi, yi, r >> 1)
            return px, py, (1 - ci if r & 1 else ci)

        def copy(r, slot):
            return pltpu.make_async_remote_copy(
                src_ref=stage_ref, dst_ref=buf_ref.at[slot], send_sem=send_sems.at[r - 1], recv_sem=recv_sems.at[r - 1],
                device_id=peer(r), device_id_type=MESH)

        sends = []
        for r in range(1, N_DEV):
            cp = copy(r, me)
            cp.start()
            sends.append(cp)
        for r in range(1, N_DEV):
            px, py, pc = peer(r)
            copy(r, 4 * px + 2 * py + pc).wait_recv()
        for cp in sends:
            cp.wait_send()
        acc = buf_ref[0]
        for d in range(1, N_DEV):
            acc = acc + buf_ref[d]
        tot_ref[...] = acc

        def rows(name, n, width):
            r0 = _ST[name]
            return tot_ref[r0:r0 + n, 0:width]

        def mine(name, n, width):
            r0 = _ST[name]
            out = tot_ref[r0:r0 + n, 0:width]
            for j in range(1, N_CHIPS):
                out = jnp.where(chip == j, tot_ref[r0:r0 + n, j * width:(j + 1) * width], out)
            return out

        loss_ref[...] = rows("loss", 1, 1)
        grads = {
            "norm_mix": rows("norm_mix0", 2, D), "attn_b_qkv": rows("attn_b_qkv", 1, QKV),
            "attn_sinks": rows("attn_sinks", 1, N_HEADS), "attn_b_o": rows("attn_b_o", 1, D),
            "norm_ffn": rows("norm_ffn0", 2, D), "ffn_b_dw": rows("ffn_b_dw0", 2, DFF),
            "final_norm": rows("final_norm", 1, D),
            "conv_b_pw1": mine("conv_b_pw1", 1, 2 * D // N_CHIPS), "conv_w_dw": mine("conv_w_dw", CONV_W, D // N_CHIPS),
            "conv_b_dw": mine("conv_b_dw", 1, D // N_CHIPS), "conv_ln_g": mine("conv_ln_g", 1, D // N_CHIPS),
            "conv_ln_b": mine("conv_ln_b", 1, D // N_CHIPS), "conv_b_pw2": mine("conv_b_pw2", 1, D // N_CHIPS),
        }
        for name in names:
            if name == "ffn_w_dw":
                continue
            at = 0 if name == "conv_w_dw" else Ellipsis
            g = grads[name]
            d, mn, vn = _adam_math(w_ref[name][at], g, m_ref[name][at], v_ref[name][at])
            g_out[name][at] = g
            d_out[name][at] = d
            m_out[name][at] = mn
            v_out[name][at] = vn
        for layer, key in enumerate(("ffn_w_dw0", "ffn_w_dw1")):
            g = mine(key, 3, DFF // N_CHIPS)
            d, mn, vn = _adam_math(w_ref["ffn_w_dw"][layer], g, m_ref["ffn_w_dw"][layer], v_ref["ffn_w_dw"][layer])
            g_out["ffn_w_dw"][layer] = g
            d_out["ffn_w_dw"][layer] = d
            m_out["ffn_w_dw"][layer] = mn
            v_out["ffn_w_dw"][layer] = vn

    ins = [parts[k] for k in _SMALL_PARTS] + [src[k] for src in (w, m, v) for k in names]
    wshapes = [SDS(w[k].shape, f32) for k in names]
    outs = pl.pallas_call(
        body, name="small_reduce_adamw", in_specs=[_VMEM] * len(ins), out_specs=[_VMEM] * (1 + 4 * nw),
        out_shape=[SDS((1, 1), f32)] + wshapes * 4,
        scratch_shapes=[pltpu.VMEM((STAGE_ROWS, STAGE_W), f32), pltpu.VMEM((N_DEV, STAGE_ROWS, STAGE_W), f32),
                        pltpu.VMEM((STAGE_ROWS, STAGE_W), f32), pltpu.SemaphoreType.DMA((N_DEV - 1,)),
                        pltpu.SemaphoreType.DMA((N_DEV - 1,))],
        compiler_params=pltpu.CompilerParams(vmem_limit_bytes=VMEM_LIMIT),
    )(*ins)
    loss = outs[0]
    g, d, mn, vn = (dict(zip(names, outs[1 + k * nw:1 + (k + 1) * nw])) for k in range(4))
    return loss, g, d, mn, vn


TM = 512
TM_LIGHT = 1024
TM_FFN = 256
FFN_CHUNK = 256
CONV_ROWS = 128
CONV_BWD_ROWS = 64
TK = 2048
TK_LIGHT = 4096
FORWARD_AT = 0.6
FORWARD_AT_LATE = 0.85


def kernel(x, norm_mix, attn_w_qkv, attn_b_qkv, attn_sinks, attn_w_o, attn_b_o, conv_w_pw1, conv_b_pw1, conv_w_dw, conv_b_dw, conv_ln_g, conv_ln_b, conv_w_pw2, conv_b_pw2, norm_ffn, ffn_w_up, ffn_w_dw, ffn_b_dw, ffn_w_down, final_norm, loss_target, m_norm_mix, m_attn_w_qkv, m_attn_b_qkv, m_attn_sinks, m_attn_w_o, m_attn_b_o, m_conv_w_pw1, m_conv_b_pw1, m_conv_w_dw, m_conv_b_dw, m_conv_ln_g, m_conv_ln_b, m_conv_w_pw2, m_conv_b_pw2, m_norm_ffn, m_ffn_w_up, m_ffn_w_dw, m_ffn_b_dw, m_ffn_w_down, m_final_norm, v_norm_mix, v_attn_w_qkv, v_attn_b_qkv, v_attn_sinks, v_attn_w_o, v_attn_b_o, v_conv_w_pw1, v_conv_b_pw1, v_conv_w_dw, v_conv_b_dw, v_conv_ln_g, v_conv_ln_b, v_conv_w_pw2, v_conv_b_pw2, v_norm_ffn, v_ffn_w_up, v_ffn_w_dw, v_ffn_b_dw, v_ffn_w_down, v_final_norm):
    w = dict(norm_mix=norm_mix, attn_w_qkv=attn_w_qkv, attn_b_qkv=attn_b_qkv, attn_sinks=attn_sinks, attn_w_o=attn_w_o,
             attn_b_o=attn_b_o, conv_w_pw1=conv_w_pw1, conv_b_pw1=conv_b_pw1, conv_w_dw=conv_w_dw, conv_b_dw=conv_b_dw,
             conv_ln_g=conv_ln_g, conv_ln_b=conv_ln_b, conv_w_pw2=conv_w_pw2, conv_b_pw2=conv_b_pw2, norm_ffn=norm_ffn,
             ffn_w_up=ffn_w_up, ffn_w_dw=ffn_w_dw, ffn_b_dw=ffn_b_dw, ffn_w_down=ffn_w_down, final_norm=final_norm)
    mom = dict(norm_mix=m_norm_mix, attn_w_qkv=m_attn_w_qkv, attn_b_qkv=m_attn_b_qkv, attn_sinks=m_attn_sinks,
               attn_w_o=m_attn_w_o, attn_b_o=m_attn_b_o, conv_w_pw1=m_conv_w_pw1, conv_b_pw1=m_conv_b_pw1,
               conv_w_dw=m_conv_w_dw, conv_b_dw=m_conv_b_dw, conv_ln_g=m_conv_ln_g, conv_ln_b=m_conv_ln_b,
               conv_w_pw2=m_conv_w_pw2, conv_b_pw2=m_conv_b_pw2, norm_ffn=m_norm_ffn, ffn_w_up=m_ffn_w_up,
               ffn_w_dw=m_ffn_w_dw, ffn_b_dw=m_ffn_b_dw, ffn_w_down=m_ffn_w_down, final_norm=m_final_norm)
    vel = dict(norm_mix=v_norm_mix, attn_w_qkv=v_attn_w_qkv, attn_b_qkv=v_attn_b_qkv, attn_sinks=v_attn_sinks,
               attn_w_o=v_attn_w_o, attn_b_o=v_attn_b_o, conv_w_pw1=v_conv_w_pw1, conv_b_pw1=v_conv_b_pw1,
               conv_w_dw=v_conv_w_dw, conv_b_dw=v_conv_b_dw, conv_ln_g=v_conv_ln_g, conv_ln_b=v_conv_ln_b,
               conv_w_pw2=v_conv_w_pw2, conv_b_pw2=v_conv_b_pw2, norm_ffn=v_norm_ffn, ffn_w_up=v_ffn_w_up,
               ffn_w_dw=v_ffn_w_dw, ffn_b_dw=v_ffn_b_dw, ffn_w_down=v_ffn_w_down, final_norm=v_final_norm)
    order = ("norm_mix", "attn_w_qkv", "attn_b_qkv", "attn_sinks", "attn_w_o", "attn_b_o", "conv_w_pw1", "conv_b_pw1",
             "conv_w_dw", "conv_b_dw", "conv_ln_g", "conv_ln_b", "conv_w_pw2", "conv_b_pw2", "norm_ffn", "ffn_w_up",
             "ffn_w_dw", "ffn_b_dw", "ffn_w_down", "final_norm")
    xi, yi, ci = _place()
    chip = 2 * xi + yi
    xs, target = x[0], loss_target[0]
    s = xs.shape[0]
    tm, tmf, tk, tkl, tml = min(TM, s), min(TM_FFN, s), min(TK, s), min(TK_LIGHT, s), min(TM_LIGHT, s)
    row = lambda v: v.reshape(1, -1)
    join = lambda a, axis: jnp.concatenate([a[j] for j in range(N_CHIPS)], axis=axis)
    cast = lambda a: a.astype(bf16)
    small, big = {}, {}

    got = run_rider(gather_rider(
        [(cast(attn_w_qkv[0]), _gv_qkv, (N_CHIPS, D, QKV // N_CHIPS)), (cast(attn_w_o[0]), _gv_rows, (D, D))],
        [], 0.0), "gather_attn")
    qkv4, w_o = got[:2]
    w_qkv = jnp.transpose(qkv4, (1, 0, 2)).reshape(D, QKV)
    sinks = attn_sinks.reshape(N_HEADS)

    h0, qkv = qkv_fwd(xs, row(norm_mix[0]), w_qkv, attn_b_qkv, tml)
    (o, lse), landed = attn_fwd(qkv, sinks, rider=gather_rider(
        [(cast(ffn_w_up[0]), _gv_cols, (D, 2 * DFF)), (cast(ffn_w_down[0]), _gv_rows, (DFF, D))],
        [w[k] for k in SMALL_SH], FORWARD_AT))
    w_up0, w_dn0 = landed[:2]
    sm = dict(zip(SMALL_SH, landed[2:]))
    b_pw1, conv_dw, conv_bdw = join(sm["conv_b_pw1"], 1), join(sm["conv_w_dw"], 2)[0], join(sm["conv_b_dw"], 1)
    ln_g, ln_b, b_pw2, ffn_dw = (join(sm["conv_ln_g"], 1), join(sm["conv_ln_b"], 1), join(sm["conv_b_pw2"], 1),
                                 join(sm["ffn_w_dw"], 2))
    x1 = attn_out_fwd(xs, o, w_o, attn_b_o, tml)
    (x2, h1, up0, gate0, act0), (w_pw1, w_pw2, w_up1, w_dn1) = ffn_fwd(
        x1, row(norm_ffn[0]), w_up0, ffn_dw[0], row(ffn_b_dw[0]), w_dn0, tmf, FFN_CHUNK, rider=gather_rider(
            [(cast(conv_w_pw1[0]), _gv_cols, (D, 2 * D)), (cast(conv_w_pw2[0]), _gv_rows, (D, D)),
             (cast(ffn_w_up[1]), _gv_cols, (D, 2 * DFF)), (cast(ffn_w_down[1]), _gv_rows, (DFF, D))], [],
            FORWARD_AT_LATE))
    h2, a, u = pw1_fwd(x2, row(norm_mix[1]), w_pw1, b_pw1, tml)
    c, x3 = conv_fwd(u, x2, conv_dw, conv_bdw, ln_g, ln_b, w_pw2, b_pw2, tm, CONV_ROWS)
    dx4, h3, up1, gate1, act1, small["final_norm"], small["loss"] = ffn_fwd(
        x3, row(norm_ffn[1]), w_up1, ffn_dw[1], row(ffn_b_dw[1]), w_dn1, tmf, FFN_CHUNK,
        head=(final_norm.reshape(1, D), target))

    dx3, dup1, small["norm_ffn1"], small["ffn_w_dw1"], small["ffn_b_dw1"] = ffn_bwd(
        dx4, x3, row(norm_ffn[1]), up1, gate1, ffn_dw[1], w_dn1, w_up1, tmf, FFN_CHUNK)
    red1 = GradReduction(("up1", "down1"), [wgrad_cols(h3, dup1, tk, "wgrad_up1"),
                                           wgrad_rows(act1, dx4, 512, tk, "wgrad_down1")], ci, chip)
    (dc, u3, small["conv_ln_g"], small["conv_ln_b"], small["conv_b_pw2"], small["conv_b_dw"]), from_pair = conv_bwd_a(
        dx3, c, ln_g, ln_b, w_pw2.T, tml, rider=red1.pair_rider())
    g_pw2 = wgrad_rows(u3, dx3, 512, tkl, "wgrad_pw2")
    (dx2, da, small["conv_w_dw"], small["conv_b_pw1"], small["norm_mix1"]), landed = conv_bwd_b(
        dc, u, a, conv_dw, x2, row(norm_mix[1]), w_pw1.T, dx3, tm, CONV_BWD_ROWS,
        rider=red1.chip_rider(from_pair, FORWARD_AT))
    big.update(red1.result(landed, {"up1": (1, 2, None), "down1": (1, 2, None)}))
    g_pw1 = wgrad_cols(h2, da, tkl, "wgrad_pw1")

    red2 = GradReduction(("pw1", "pw2"), [g_pw1, g_pw2], ci, chip)
    (dx1, dup0, small["norm_ffn0"], small["ffn_w_dw0"], small["ffn_b_dw0"]), from_pair = ffn_bwd(
        dx2, x1, row(norm_ffn[0]), up0, gate0, ffn_dw[0], w_dn0, w_up0, tmf, FFN_CHUNK, rider=red2.pair_rider())
    g_up0, landed = wgrad_cols(h1, dup0, tk, "wgrad_up0", rider=red2.chip_rider(from_pair, FORWARD_AT))
    big.update(red2.result(landed))
    g_dn0 = wgrad_rows(act0, dx2, 512, tk, "wgrad_down0")

    red3 = GradReduction(("up0", "down0", "wo"), [g_up0, g_dn0, wgrad_rows(o, dx1, 512, tkl, "wgrad_o")], ci, chip)
    (do, small["attn_b_o"]), from_pair = attn_out_bwd(dx1, w_o.T, tml, rider=red3.pair_rider())
    (dq, dkv, small["attn_sinks"]), landed = attn_bwd(qkv, o, do, lse, sinks,
                                                      rider=red3.chip_rider(from_pair, FORWARD_AT))
    both = red3.result(landed, {"up0": (0, 2, big["up1"]), "down0": (0, 2, big["down1"])})
    big.update({"up": both.pop("up0"), "down": both.pop("down0"), **both})
    dx0, small["norm_mix0"], small["attn_b_qkv"], dkvb = qkv_bwd(dq, dkv, xs, row(norm_mix[0]), w_qkv.T, dx1, tml)
    g_qkv = jnp.concatenate([wgrad(h0, dq, 512, tkl, "wgrad_q"), wgrad(h0, dkvb, 2 * N_KV * HD, tkl, "wgrad_kv")], axis=1)
    g_qkv = jnp.transpose(g_qkv.reshape(2, D // 2, N_CHIPS, QKV // N_CHIPS), (0, 2, 1, 3))
    red4 = GradReduction(("qkv",), [g_qkv], ci, chip)
    big.update(red4.alone())

    gbig = {
        "attn_w_qkv": big["qkv"].reshape(1, D, QKV // N_CHIPS), "attn_w_o": big["wo"].reshape(1, D // N_CHIPS, D),
        "conv_w_pw1": big["pw1"].reshape(1, D, 2 * D // N_CHIPS), "conv_w_pw2": big["pw2"].reshape(1, D // N_CHIPS, D),
        "ffn_w_up": big["up"].reshape(2, D, 2 * DFF // N_CHIPS),
        "ffn_w_down": big["down"].reshape(2, DFF // N_CHIPS, D),
    }

    fix = lambda d: {**d, "final_norm": d["final_norm"].reshape(1, D)}
    loss, gs, ds, ms, vs = small_reduce_adamw(small, fix(w), fix(mom), fix(vel))
    unfix = lambda d: {**d, "final_norm": d["final_norm"].reshape(D)}
    gout, delta, new_m, new_v = unfix(gs), unfix(ds), unfix(ms), unfix(vs)

    for name, g in gbig.items():
        gout[name] = g
        delta[name], new_m[name], new_v[name] = adamw(w[name], g, mom[name], vel[name], "adamw_" + name)

    return (loss.reshape(()), dx0[None], *[gout[n] for n in order], *[delta[n] for n in order],
            *[new_m[n] for n in order], *[new_v[n] for n in order])
```
